```python
import math
import jax, jax.numpy as jnp
from jax import lax
import numpy as np

D_MODEL = 1024
BATCH = 4
SEQ = 4096
DEPTH = 2

D_MIX = 2 * D_MODEL
EPS = 1e-6

A_WIDTH = D_MIX // 2
A_HEAD_DIM = 64
A_Q_HEADS = A_WIDTH // A_HEAD_DIM
A_KV_HEADS = 4
A_GROUP = A_Q_HEADS // A_KV_HEADS
A_KV_WIDTH = A_KV_HEADS * A_HEAD_DIM
WINDOW = 128
A_BLOCK = 128

B_WIDTH = D_MIX // 4
B_HEADS = 4
B_DK = B_WIDTH // (2 * B_HEADS)
B_DV = B_WIDTH // B_HEADS
B_QK_WIDTH = B_HEADS * B_DK
B_GATE_RANK = 16
B_GATE_TAU = 16.0
B_CHUNK = 16

C_WIDTH = D_MIX // 4
C_GROUP_CH = 16
C_GROUPS = C_WIDTH // C_GROUP_CH
C_STATE = 64

PROJ_SIZES = (A_WIDTH, A_KV_WIDTH, A_KV_WIDTH, A_WIDTH,
              B_QK_WIDTH, B_QK_WIDTH, B_WIDTH, B_GATE_RANK, B_WIDTH,
              C_WIDTH, C_WIDTH)
PROJ_COLS = sum(PROJ_SIZES)

kernel_name = "hybrid_swa_gla_s5_parallel_heads"


def rmsnorm(x, g):
    xf = x.astype(jnp.float32)
    y = xf * lax.rsqrt(jnp.mean(xf * xf, axis=-1, keepdims=True) + EPS)
    return (y * g.astype(jnp.float32)).astype(x.dtype)


def alibi_slopes(n):
    return jnp.exp2(-8.0 * jnp.arange(1, n + 1, dtype=jnp.float32) / n)


def sliding_window_attention(q, k, v, sinks):
    bsz, s_len = q.shape[:2]
    nb = s_len // A_BLOCK
    q = q.reshape(bsz, nb, A_BLOCK, A_KV_HEADS, A_GROUP, A_HEAD_DIM)
    k = k.reshape(bsz, s_len, A_KV_HEADS, A_HEAD_DIM)
    v = v.reshape(bsz, s_len, A_KV_HEADS, A_HEAD_DIM)
    pad = ((0, 0), (A_BLOCK, 0), (0, 0), (0, 0))
    blk_shape = (bsz, nb, A_BLOCK, A_KV_HEADS, A_HEAD_DIM)
    kk = jnp.concatenate([jnp.pad(k, pad)[:, :s_len].reshape(blk_shape), k.reshape(blk_shape)], axis=2)
    vv = jnp.concatenate([jnp.pad(v, pad)[:, :s_len].reshape(blk_shape), v.reshape(blk_shape)], axis=2)
    s = jnp.einsum('bnqkgd,bnskd->bnkgqs', q, kk).astype(jnp.float32) * (A_HEAD_DIM ** -0.5)
    i = jnp.arange(A_BLOCK)[:, None]
    j = jnp.arange(2 * A_BLOCK)[None, :]
    dist = i + A_BLOCK - j
    key_pos = jnp.arange(nb)[:, None, None] * A_BLOCK - A_BLOCK + j[None]
    valid = (dist >= 0)[None] & (dist < WINDOW)[None] & (key_pos >= 0)
    slopes = alibi_slopes(A_Q_HEADS).reshape(A_KV_HEADS, A_GROUP)
    s = s - slopes[:, :, None, None] * dist.astype(jnp.float32)
    s = jnp.where(valid[None, :, None, None], s, -jnp.inf)
    sink = sinks.astype(jnp.float32).reshape(A_KV_HEADS, A_GROUP)[:, :, None, None]
    m = jnp.maximum(jnp.max(s, axis=-1, keepdims=True), sink)
    p = jnp.exp(s - m)
    probs = (p / (jnp.sum(p, axis=-1, keepdims=True) + jnp.exp(sink - m))).astype(vv.dtype)
    o = jnp.einsum('bnkgqs,bnskd->bnqkgd', probs, vv)
    return o.reshape(bsz, s_len, A_WIDTH)


def gated_linear_attention(q, k, v, log_a):
    bsz, s_len = q.shape[:2]
    nc = s_len // B_CHUNK
    cshape = (bsz, nc, B_CHUNK, B_HEADS)
    q = (q.astype(jnp.float32) * (B_DK ** -0.5)).reshape(cshape + (B_DK,))
    k = k.astype(jnp.float32).reshape(cshape + (B_DK,))
    v = v.astype(jnp.float32).reshape(cshape + (B_DV,))
    b = jnp.cumsum(log_a.astype(jnp.float32).reshape(cshape + (B_DK,)), axis=2)
    causal = jnp.tril(jnp.ones((B_CHUNK, B_CHUNK), dtype=bool))[None, None, :, :, None, None]
    decay = jnp.exp(jnp.where(causal, b[:, :, :, None] - b[:, :, None, :], -jnp.inf))
    attn = jnp.einsum('bnihd,bnjhd,bnijhd->bnhij', q, k, decay)
    o_intra = jnp.einsum('bnhij,bnjhv->bnihv', attn, v)
    b_last = b[:, :, -1]
    u = jnp.einsum('bnjhd,bnjhv->bnhdv', k * jnp.exp(b_last[:, :, None] - b), v)
    chunk_decay = jnp.exp(b_last)

    def step(state, inp):
        dec, uu = inp
        return dec[..., None] * state + uu, state

    init = jnp.zeros((bsz, B_HEADS, B_DK, B_DV), jnp.float32)
    _, s_prev = lax.scan(step, init, (jnp.moveaxis(chunk_decay, 1, 0), jnp.moveaxis(u, 1, 0)))
    s_prev = jnp.moveaxis(s_prev, 0, 1)
    o_inter = jnp.einsum('bnihd,bnhdv->bnihv', q * jnp.exp(b), s_prev)
    return (o_intra + o_inter).reshape(bsz, s_len, B_HEADS, B_DV)


def _complex_affine_combine(e1, e2):
    a1r, a1i, b1r, b1i = e1
    a2r, a2i, b2r, b2i = e2
    return (a2r * a1r - a2i * a1i,
            a2r * a1i + a2i * a1r,
            a2r * b1r - a2i * b1i + b2r,
            a2r * b1i + a2i * b1r + b2i)


def s5_ssm(u, a_re, a_im, log_dt, b_re, b_im, c_re, c_im, d):
    bsz, s_len = u.shape[:2]
    uf = u.astype(jnp.float32).reshape(bsz, s_len, C_GROUPS, C_GROUP_CH)
    ar = a_re.astype(jnp.float32)
    ai = a_im.astype(jnp.float32)
    dt = jnp.exp(log_dt.astype(jnp.float32))[:, None]
    mag = jnp.exp(ar * dt)
    abar_re = mag * jnp.cos(ai * dt)
    abar_im = mag * jnp.sin(ai * dt)
    den = ar * ar + ai * ai
    num_re = abar_re - 1.0
    f_re = (num_re * ar + abar_im * ai) / den
    f_im = (abar_im * ar - num_re * ai) / den
    bu_re = jnp.einsum('blgh,gph->blgp', uf, b_re.astype(jnp.float32))
    bu_im = jnp.einsum('blgh,gph->blgp', uf, b_im.astype(jnp.float32))
    in_re = f_re * bu_re - f_im * bu_im
    in_im = f_re * bu_im + f_im * bu_re
    _, _, h_re, h_im = lax.associative_scan(
        _complex_affine_combine,
        (jnp.broadcast_to(abar_re, in_re.shape), jnp.broadcast_to(abar_im, in_re.shape), in_re, in_im),
        axis=1)
    y = (jnp.einsum('blgp,ghp->blgh', h_re, c_re.astype(jnp.float32))
         - jnp.einsum('blgp,ghp->blgh', h_im, c_im.astype(jnp.float32)))
    return y.reshape(bsz, s_len, C_WIDTH) + d.astype(jnp.float32) * uf.reshape(bsz, s_len, C_WIDTH)


def hybrid_layer(x, c, w_mod, b_mod, g_pre, g_post, w_in, sinks, w_alpha, b_alpha, g_gla,
                 a_re, a_im, log_dt, b_re, b_im, c_re, c_im, d, w_glu, b_glu, w_out):
    bsz, s_len = x.shape[:2]
    mod = jax.nn.silu(c) @ w_mod + b_mod
    shift, scale, gate = jnp.split(mod, 3, axis=-1)
    h = rmsnorm(x, g_pre) * (1.0 + scale[:, None]) + shift[:, None]
    proj = h @ w_in
    pieces = []
    off = 0
    for size in PROJ_SIZES:
        pieces.append(proj[..., off:off + size])
        off += size
    a_q, a_k, a_v, a_g, b_q, b_k, b_v, b_lr, b_g, c_u, c_g = pieces

    o_a = sliding_window_attention(a_q, a_k, a_v, sinks) * jax.nn.silu(a_g)

    gate_logits = (b_lr @ w_alpha + b_alpha).astype(jnp.float32)
    log_a = (jax.nn.log_sigmoid(gate_logits) / B_GATE_TAU).reshape(bsz, s_len, B_HEADS, B_DK)
    o_b = gated_linear_attention(b_q.reshape(bsz, s_len, B_HEADS, B_DK),
                                 b_k.reshape(bsz, s_len, B_HEADS, B_DK),
                                 b_v.reshape(bsz, s_len, B_HEADS, B_DV), log_a)
    o_b = rmsnorm(o_b, g_gla.reshape(B_HEADS, B_DV)).reshape(bsz, s_len, B_WIDTH)
    o_b = o_b.astype(x.dtype) * jax.nn.silu(b_g)

    y = jax.nn.gelu(s5_ssm(c_u, a_re, a_im, log_dt, b_re, b_im, c_re, c_im, d)).astype(x.dtype)
    y = y * jax.nn.sigmoid(y @ w_glu + b_glu)
    o_c = y * jax.nn.silu(c_g)

    mix = jnp.concatenate([o_a, o_b, o_c], axis=-1)
    out = rmsnorm(mix @ w_out, g_post)
    return x + gate[:, None] * out


def setup_inputs(seed: int = 0) -> dict:
    key = jax.random.key(seed)
    ks = jax.random.split(key, 24)
    L, D = DEPTH, D_MODEL
    nrm = lambda k, shape, s: jax.random.normal(k, shape, jnp.float32) * s
    n_idx = jnp.arange(C_STATE, dtype=jnp.float32)
    return {
        "x": nrm(ks[0], (BATCH, SEQ, D), 1.0),
        "c": nrm(ks[1], (BATCH, D), 1.0),
        "w_mod": nrm(ks[2], (L, D, 3 * D), D ** -0.5),
        "b_mod": nrm(ks[3], (L, 3 * D), 0.02),
        "g_pre": 1.0 + nrm(ks[4], (L, D), 0.02),
        "g_post": 1.0 + nrm(ks[5], (L, D), 0.02),
        "w_in": nrm(ks[6], (L, D, PROJ_COLS), D ** -0.5),
        "attn_sinks": nrm(ks[7], (L, A_Q_HEADS), 0.5),
        "gla_w_alpha": nrm(ks[8], (L, B_GATE_RANK, B_QK_WIDTH), B_GATE_RANK ** -0.5),
        "gla_b_alpha": nrm(ks[9], (L, B_QK_WIDTH), 0.1),
        "gla_norm_g": 1.0 + nrm(ks[10], (L, B_WIDTH), 0.02),
        "s5_a_re": -0.5 + nrm(ks[11], (L, C_GROUPS, C_STATE), 0.01),
        "s5_a_im": math.pi * n_idx + nrm(ks[12], (L, C_GROUPS, C_STATE), 0.01),
        "s5_log_dt": jax.random.uniform(ks[13], (L, C_GROUPS), jnp.float32, math.log(1e-3), math.log(1e-1)),
        "s5_b_re": nrm(ks[14], (L, C_GROUPS, C_STATE, C_GROUP_CH), (2 * C_GROUP_CH) ** -0.5),
        "s5_b_im": nrm(ks[15], (L, C_GROUPS, C_STATE, C_GROUP_CH), (2 * C_GROUP_CH) ** -0.5),
        "s5_c_re": nrm(ks[16], (L, C_GROUPS, C_GROUP_CH, C_STATE), (2 * C_STATE) ** -0.5),
        "s5_c_im": nrm(ks[17], (L, C_GROUPS, C_GROUP_CH, C_STATE), (2 * C_STATE) ** -0.5),
        "s5_d": nrm(ks[18], (L, C_WIDTH), 1.0),
        "s5_w_glu": nrm(ks[19], (L, C_WIDTH, C_WIDTH), C_WIDTH ** -0.5),
        "s5_b_glu": nrm(ks[20], (L, C_WIDTH), 0.02),
        "w_out": nrm(ks[21], (L, D_MIX, D), D_MIX ** -0.5),
    }


def reference(x, c, w_mod, b_mod, g_pre, g_post, w_in, attn_sinks, gla_w_alpha, gla_b_alpha,
              gla_norm_g, s5_a_re, s5_a_im, s5_log_dt, s5_b_re, s5_b_im, s5_c_re, s5_c_im,
              s5_d, s5_w_glu, s5_b_glu, w_out):
    for l in range(DEPTH):
        x = hybrid_layer(x, c, w_mod[l], b_mod[l], g_pre[l], g_post[l], w_in[l], attn_sinks[l],
                         gla_w_alpha[l], gla_b_alpha[l], gla_norm_g[l],
                         s5_a_re[l], s5_a_im[l], s5_log_dt[l], s5_b_re[l], s5_b_im[l],
                         s5_c_re[l], s5_c_im[l], s5_d[l], s5_w_glu[l], s5_b_glu[l], w_out[l])
    return x
```

```python
import functools
import math

import jax
import jax.numpy as jnp
from jax import lax
from jax.experimental import pallas as pl
from jax.experimental.pallas import tpu as pltpu

F32 = jnp.float32
BF16 = jnp.bfloat16
HIGHEST = lax.Precision.HIGHEST

D_MODEL = 1024
BATCH = 4
SEQ = 4096
TOKENS = BATCH * SEQ
EPS = 1e-6

A_WIDTH = 1024
A_HEAD_DIM = 64
A_Q_HEADS = 16
A_KV_HEADS = 4
A_KV_WIDTH = A_KV_HEADS * A_HEAD_DIM
A_BLOCK = 128
WINDOW = 128

B_WIDTH = 512
B_HEADS = 4
B_DK = 64
B_DV = 128
B_QK_WIDTH = 256
B_GATE_RANK = 16
B_GATE_TAU = 16.0
GLA_BLOCK = 64

C_WIDTH = 512
C_GROUP_CH = 16
C_GROUPS = 32
C_STATE = 64
S5_CHUNK = 16
S5_CHUNKS = TOKENS // S5_CHUNK
S5_TC = S5_CHUNK * C_GROUP_CH
S5_PAIRS = C_GROUPS // 2

LR_PAD = 128

V7X_VMEM_LIMIT = 56 * 1024 * 1024

PROJ_TM = 512
OUT_TM = 512

_PROJ_LAYOUT = {}
_off = 0
for _name, _w in (("aq", A_WIDTH), ("ak", 2 * A_KV_WIDTH), ("av", 2 * A_KV_WIDTH), ("ag", A_WIDTH),
                  ("bq", B_QK_WIDTH), ("bk", B_QK_WIDTH), ("bv", B_WIDTH), ("blr", LR_PAD),
                  ("bg", B_WIDTH), ("cu", C_WIDTH), ("cg", C_WIDTH)):
    _PROJ_LAYOUT[_name] = (_off, _w)
    _off += _w
PROJ_COLS_PADDED = _off
_PROJ_OUTS = ("aq", "ak", "av", "ag", "bq", "bk", "bv", "bg", "cu", "cg")


def _silu(x):
    return x * jax.nn.sigmoid(x)


def _cparams(semantics):
    return pltpu.CompilerParams(dimension_semantics=semantics, vmem_limit_bytes=V7X_VMEM_LIMIT)


def _mod_kernel(c_ref, w_ref, b_ref, o_ref):
    c = c_ref[...]
    o_ref[...] = jnp.dot(_silu(c), w_ref[...], preferred_element_type=F32, precision=HIGHEST) + b_ref[...]


def _mod(c_pad, w_mod, b_mod):
    n = 3 * D_MODEL
    tn = 768
    return pl.pallas_call(
        _mod_kernel,
        out_shape=jax.ShapeDtypeStruct((8, n), F32),
        grid=(n // tn,),
        in_specs=[pl.BlockSpec((8, D_MODEL), lambda j: (0, 0)),
                  pl.BlockSpec((D_MODEL, tn), lambda j: (0, j)),
                  pl.BlockSpec((1, tn), lambda j: (0, j))],
        out_specs=pl.BlockSpec((8, tn), lambda j: (0, j)),
        compiler_params=_cparams(("arbitrary",)),
        name="mod",
    )(c_pad, w_mod, b_mod.reshape(1, n))


def _proj_kernel(x_ref, scale_ref, shift_ref, gpre_ref, w_ref, walpha_ref, balpha_ref, *out_refs):
    x = x_ref[...]
    ms = jnp.mean(x * x, axis=-1, keepdims=True)
    y = x * lax.rsqrt(ms + EPS) * gpre_ref[...]
    h = (y * (1.0 + scale_ref[0]) + shift_ref[0]).astype(BF16)
    for name, o_ref in zip(_PROJ_OUTS, out_refs[:-1]):
        off, width = _PROJ_LAYOUT[name]
        o_ref[...] = jnp.dot(h, w_ref[:, off:off + width], preferred_element_type=F32).astype(o_ref.dtype)
    off, width = _PROJ_LAYOUT["blr"]
    lr = jnp.dot(h, w_ref[:, off:off + width], preferred_element_type=F32).astype(BF16)
    logits = jnp.dot(lr, walpha_ref[...], preferred_element_type=F32) + balpha_ref[...]
    log_sig = jnp.minimum(logits, 0.0) - jnp.log(1.0 + jnp.exp(-jnp.abs(logits)))
    out_refs[-1][...] = log_sig * (1.0 / B_GATE_TAU)


def _proj(x2, scale, shift, g_pre, w_perm, w_alpha_pad, b_alpha):
    tm = PROJ_TM
    steps_per_batch = SEQ // tm
    row = lambda i: (i, 0)
    per_batch = lambda i: (i // steps_per_batch, 0, 0)
    const = lambda i: (0, 0)
    out_shape = [jax.ShapeDtypeStruct((TOKENS, _PROJ_LAYOUT[n][1]), BF16) for n in _PROJ_OUTS]
    out_shape.append(jax.ShapeDtypeStruct((TOKENS, B_QK_WIDTH), F32))
    out_specs = [pl.BlockSpec((tm, _PROJ_LAYOUT[n][1]), row) for n in _PROJ_OUTS]
    out_specs.append(pl.BlockSpec((tm, B_QK_WIDTH), row))
    return pl.pallas_call(
        _proj_kernel,
        out_shape=out_shape,
        grid=(TOKENS // tm,),
        in_specs=[pl.BlockSpec((tm, D_MODEL), row),
                  pl.BlockSpec((1, 1, D_MODEL), per_batch),
                  pl.BlockSpec((1, 1, D_MODEL), per_batch),
                  pl.BlockSpec((1, D_MODEL), const),
                  pl.BlockSpec((D_MODEL, PROJ_COLS_PADDED), const),
                  pl.BlockSpec((LR_PAD, B_QK_WIDTH), const),
                  pl.BlockSpec((1, B_QK_WIDTH), const)],
        out_specs=out_specs,
        compiler_params=_cparams(("arbitrary",)),
        name="proj",
    )(x2, scale, shift, g_pre.reshape(1, D_MODEL), w_perm, w_alpha_pad, b_alpha.reshape(1, B_QK_WIDTH))


def _attn_bias():
    i = jnp.arange(A_BLOCK)[:, None]
    j = jnp.arange(2 * A_BLOCK)[None, :]
    dist = i + A_BLOCK - j
    valid = (dist >= 0) & (dist < WINDOW)
    slopes = jnp.exp2(-8.0 * jnp.arange(1, A_Q_HEADS + 1, dtype=F32) / A_Q_HEADS)
    bias = -slopes[:, None, None] * dist.astype(F32)[None]
    return jnp.where(valid[None], bias, -jnp.inf)


def _attn_kernel(sink_ref, q_ref, kp_ref, kc_ref, vp_ref, vc_ref, g_ref, bias_ref, o_ref):
    n = pl.program_id(1)
    pen = jnp.where(n == 0, -jnp.inf, 0.0).astype(F32)
    col = lax.broadcasted_iota(jnp.int32, (1, 2 * A_BLOCK), 1)
    pen_row = jnp.where(col < A_BLOCK, pen, 0.0)
    lane = lax.broadcasted_iota(jnp.int32, (2 * A_BLOCK, 128), 1)
    lo = lane < A_HEAD_DIM
    nt = (((1,), (1,)), ((), ()))
    for kv in range(A_KV_HEADS):
        ksl = slice(128 * kv, 128 * (kv + 1))
        kk = jnp.concatenate([kp_ref[:, ksl], kc_ref[:, ksl]], axis=0)
        vv = jnp.concatenate([vp_ref[:, ksl], vc_ref[:, ksl]], axis=0)
        zero = jnp.zeros_like(kk)
        k_halves = (jnp.where(lo, kk, zero), jnp.where(lo, zero, kk))
        v_halves = (jnp.where(lo, vv, zero), jnp.where(lo, zero, vv))
        for pr in range(2):
            qsl = slice(128 * (2 * kv + pr), 128 * (2 * kv + pr + 1))
            qp = q_ref[:, qsl]
            acc = jnp.zeros((A_BLOCK, 128), F32)
            for e in range(2):
                hd = 4 * kv + 2 * pr + e
                s = lax.dot_general(qp, k_halves[e], nt, preferred_element_type=F32)
                s = s * (A_HEAD_DIM ** -0.5) + bias_ref[hd] + pen_row
                sink = sink_ref[hd]
                m = jnp.maximum(jnp.max(s, axis=-1, keepdims=True), sink)
                p = jnp.exp(s - m)
                den = jnp.sum(p, axis=-1, keepdims=True) + jnp.exp(sink - m)
                probs = (p / den).astype(BF16)
                acc = acc + jnp.dot(probs, v_halves[e], preferred_element_type=F32)
            gate = g_ref[:, qsl].astype(F32)
            o_ref[:, qsl] = (acc * _silu(gate)).astype(o_ref.dtype)


def _attn(sinks, q, kd, vd, ag, bias):
    nb = SEQ // A_BLOCK
    cur = lambda b, n: (b * nb + n, 0)
    prev = lambda b, n: (b * nb + jnp.maximum(n - 1, 0), 0)
    return pl.pallas_call(
        _attn_kernel,
        out_shape=jax.ShapeDtypeStruct((TOKENS, A_WIDTH), BF16),
        grid=(BATCH, nb),
        in_specs=[pl.BlockSpec(memory_space=pltpu.SMEM),
                  pl.BlockSpec((A_BLOCK, A_WIDTH), cur),
                  pl.BlockSpec((A_BLOCK, 2 * A_KV_WIDTH), prev),
                  pl.BlockSpec((A_BLOCK, 2 * A_KV_WIDTH), cur),
                  pl.BlockSpec((A_BLOCK, 2 * A_KV_WIDTH), prev),
                  pl.BlockSpec((A_BLOCK, 2 * A_KV_WIDTH), cur),
                  pl.BlockSpec((A_BLOCK, A_WIDTH), cur),
                  pl.BlockSpec((A_Q_HEADS, A_BLOCK, 2 * A_BLOCK), lambda b, n: (0, 0, 0))],
        out_specs=pl.BlockSpec((A_BLOCK, A_WIDTH), cur),
        compiler_params=_cparams(("arbitrary", "arbitrary")),
        name="attn",
    )(sinks, q, kd, kd, vd, vd, ag, bias)


def _gla_kernel(la_ref, q_ref, k_ref, v_ref, g_ref, gn_ref, o_ref, st_ref):
    cb = GLA_BLOCK

    @pl.when(pl.program_id(0) == 0)
    def _():
        st_ref[...] = jnp.zeros_like(st_ref)

    r = lax.broadcasted_iota(jnp.int32, (cb, cb), 0)
    c = lax.broadcasted_iota(jnp.int32, (cb, cb), 1)
    tri = (c <= r).astype(F32)
    lane = lax.broadcasted_iota(jnp.int32, (cb, B_QK_WIDTH), 1)
    head_masks = [(lane >= B_DK * h) & (lane < B_DK * (h + 1)) for h in range(B_HEADS)]
    rr = lax.broadcasted_iota(jnp.int32, (B_HEADS * cb, cb), 0)
    cc = lax.broadcasted_iota(jnp.int32, (B_HEADS * cb, cb), 1)
    causal = cc <= (rr & (cb - 1))
    nt = (((1,), (1,)), ((), ()))
    tn = (((0,), (0,)), ((), ()))
    for b in range(BATCH):
        la = la_ref[b]
        bc = jnp.dot(tri, la, preferred_element_type=F32, precision=HIGHEST)
        bl = bc[cb - 1:cb, :]
        q = q_ref[b].astype(F32) * (B_DK ** -0.5)
        k = k_ref[b].astype(F32)
        qs = q * jnp.exp(bc)
        ks = (k * jnp.exp(-bc)).astype(BF16)
        kh = k * jnp.exp(bl - bc)
        dec = jnp.exp(bl)
        qst = jnp.concatenate([jnp.where(m, qs, 0.0) for m in head_masks], axis=0).astype(BF16)
        a_all = lax.dot_general(qst, ks, nt, preferred_element_type=F32)
        a_all = jnp.where(causal, a_all, 0.0)
        st = st_ref[b]
        oi_all = lax.dot_general(qst, st.astype(BF16), nt, preferred_element_type=F32)
        v = v_ref[b]
        for h in range(B_HEADS):
            vsl = slice(B_DV * h, B_DV * (h + 1))
            a_h = a_all[cb * h:cb * (h + 1)].astype(BF16)
            o_h = jnp.dot(a_h, v[:, vsl], preferred_element_type=F32) + oi_all[cb * h:cb * (h + 1)]
            ms = jnp.mean(o_h * o_h, axis=-1, keepdims=True)
            o_n = o_h * lax.rsqrt(ms + EPS) * gn_ref[:, vsl]
            gate = g_ref[b, :, vsl].astype(F32)
            o_ref[b, :, vsl] = (o_n * _silu(gate)).astype(o_ref.dtype)
        vst = jnp.concatenate([v[:, B_DV * h:B_DV * (h + 1)] for h in range(B_HEADS)], axis=0)
        kst = jnp.concatenate([jnp.where(m, kh, 0.0) for m in head_masks], axis=0).astype(BF16)
        upd = lax.dot_general(vst, kst, tn, preferred_element_type=F32)
        st_ref[b] = st * dec + upd


def _gla(log_a, bq, bk, bv, bg, g_gla):
    cb = GLA_BLOCK
    blk = lambda w: pl.BlockSpec((BATCH, cb, w), lambda i: (0, i, 0))
    r3 = lambda a: a.reshape(BATCH, SEQ, a.shape[-1])
    out = pl.pallas_call(
        _gla_kernel,
        out_shape=jax.ShapeDtypeStruct((BATCH, SEQ, B_WIDTH), BF16),
        grid=(SEQ // cb,),
        in_specs=[blk(B_QK_WIDTH), blk(B_QK_WIDTH), blk(B_QK_WIDTH), blk(B_WIDTH), blk(B_WIDTH),
                  pl.BlockSpec((1, B_WIDTH), lambda i: (0, 0))],
        out_specs=blk(B_WIDTH),
        scratch_shapes=[pltpu.VMEM((BATCH, B_DV, B_QK_WIDTH), F32)],
        compiler_params=_cparams(("arbitrary",)),
        name="gla",
    )(r3(log_a), r3(bq), r3(bk), r3(bv), r3(bg), g_gla.reshape(1, B_WIDTH))
    return out.reshape(TOKENS, B_WIDTH)


def _s5prep_kernel(ar_ref, ai_ref, ldt_ref, bre_ref, bim_ref, btre_ref, btim_ref, cre_ref, cim_ref,
                   kk_ref, were_ref, weim_ref, wyre_ref, wyim_ref, pre_ref, pim_ref):
    ar = ar_ref[0]
    ai = ai_ref[0]
    dt = jnp.exp(ldt_ref[0])
    ardt = ar * dt
    aidt = ai * dt

    def power(kf):
        mag = jnp.exp(kf * ardt)
        ang = kf * aidt
        return mag * jnp.cos(ang), mag * jnp.sin(ang)

    one = jnp.ones((1, 1), F32)
    abar_re, abar_im = power(one)
    den = ar * ar + ai * ai
    num_re = abar_re - 1.0
    f_re = (num_re * ar + abar_im * ai) / den
    f_im = (abar_im * ar - num_re * ai) / den

    def cmul(xr, xi, yr, yi):
        return xr * yr - xi * yi, xr * yi + xi * yr

    rows = S5_TC
    lag = (lax.broadcasted_iota(jnp.int32, (rows, 1), 0) // C_GROUP_CH).astype(F32)
    tile16 = lambda a: jnp.concatenate([a] * S5_CHUNK, axis=0)
    ct_re, ct_im = tile16(cre_ref[0]), tile16(cim_ref[0])
    bt_re, bt_im = tile16(btre_ref[0]), tile16(btim_ref[0])

    g_re, g_im = cmul(*power(lag), f_re, f_im)
    w_re, w_im = cmul(g_re, g_im, ct_re, ct_im)
    kk_ref[0] = (jnp.dot(w_re, bre_ref[0], preferred_element_type=F32, precision=HIGHEST)
                 - jnp.dot(w_im, bim_ref[0], preferred_element_type=F32, precision=HIGHEST))
    e_re, e_im = cmul(*cmul(*power((S5_CHUNK - 1.0) - lag), f_re, f_im), bt_re, bt_im)
    were_ref[0] = e_re
    weim_ref[0] = e_im
    y_re, y_im = cmul(*power(lag + 1.0), ct_re, ct_im)
    wyre_ref[0] = y_re
    wyim_ref[0] = y_im
    p_re, p_im = power(one * float(S5_CHUNK))
    pre_ref[0] = p_re
    pim_ref[0] = p_im


def _s5prep(a_re, a_im, log_dt, b_re, b_im, c_re, c_im):
    g, p, ch = C_GROUPS, C_STATE, C_GROUP_CH
    row = lambda a: a.reshape(g, 1, p)
    ldt = jnp.broadcast_to(log_dt[:, None, None], (g, 1, p))
    b_t = lambda a: jnp.swapaxes(a, 1, 2)
    spec = lambda s1, s2: pl.BlockSpec((1, s1, s2), lambda i: (i, 0, 0))
    out_shape = [jax.ShapeDtypeStruct((g, S5_TC, ch), F32)] + [jax.ShapeDtypeStruct((g, S5_TC, p), F32)] * 4 \
        + [jax.ShapeDtypeStruct((g, 1, p), F32)] * 2
    out_specs = [spec(S5_TC, ch)] + [spec(S5_TC, p)] * 4 + [spec(1, p)] * 2
    return pl.pallas_call(
        _s5prep_kernel,
        out_shape=out_shape,
        grid=(g,),
        in_specs=[spec(1, p), spec(1, p), spec(1, p), spec(p, ch), spec(p, ch),
                  spec(ch, p), spec(ch, p), spec(ch, p), spec(ch, p)],
        out_specs=out_specs,
        compiler_params=_cparams(("arbitrary",)),
        name="s5prep",
    )(row(a_re), row(a_im), ldt, b_re, b_im, b_t(b_re), b_t(b_im), c_re, c_im)


def _s5_assemble(kk, we_re, we_im, wyt_re, wyt_im, p_re, p_im, d):
    g, t, ch, p = C_GROUPS, S5_CHUNK, C_GROUP_CH, C_STATE
    k4 = kk.reshape(g, t, ch, ch)
    z = jnp.concatenate([jnp.zeros_like(k4), k4], axis=1)
    idx = jnp.arange(t)[None, :] - jnp.arange(t)[:, None] + t
    m = z[:, idx]
    m = m.transpose(0, 1, 4, 2, 3).reshape(S5_PAIRS, 2, S5_TC, S5_TC).astype(BF16)
    zero = jnp.zeros((S5_PAIRS, S5_TC, p), F32)
    pr = lambda a: a.reshape(S5_PAIRS, 2, *a.shape[1:])
    a, b = pr(we_re), pr(we_im)
    we = jnp.concatenate([
        jnp.concatenate([a[:, 0], zero, b[:, 0], zero], axis=-1),
        jnp.concatenate([zero, a[:, 1], zero, b[:, 1]], axis=-1)], axis=1).astype(BF16)
    a, b = pr(jnp.swapaxes(wyt_re, 1, 2)), pr(-jnp.swapaxes(wyt_im, 1, 2))
    zero = jnp.zeros((S5_PAIRS, p, S5_TC), F32)
    wy = jnp.concatenate([
        jnp.concatenate([a[:, 0], zero], axis=-1),
        jnp.concatenate([zero, a[:, 1]], axis=-1),
        jnp.concatenate([b[:, 0], zero], axis=-1),
        jnp.concatenate([zero, b[:, 1]], axis=-1)], axis=1).astype(BF16)
    pw_re = p_re.reshape(S5_PAIRS, 1, 2 * p)
    pw_im = p_im.reshape(S5_PAIRS, 1, 2 * p)
    d_l = jnp.broadcast_to(d.reshape(g, 1, ch), (g, t, ch)).reshape(1, g * S5_TC)
    return m, we, wy, pw_re, pw_im, d_l


def _s5_kernel(u_ref, m_ref, we_ref, wy_ref, are_ref, aim_ref, d_ref, y_ref, e_ref, h_ref):
    u = u_ref[...]
    e_ref[...] = jnp.dot(u, we_ref[0], preferred_element_type=F32)
    a_re = are_ref[0]
    a_im = aim_ref[0]
    low = lax.broadcasted_iota(jnp.int32, (8, 128), 0) < BATCH

    def body(i, carry):
        h_re, h_im = carry
        r0 = pl.multiple_of(i * 8, 8)
        e_re = e_ref[pl.ds(r0, 8), 0:128]
        e_im = e_ref[pl.ds(r0, 8), 128:256]
        s_re = a_re * h_re - a_im * h_im + e_re
        s_im = a_re * h_im + a_im * h_re + e_im
        s_re = pltpu.roll(s_re, BATCH, 0)
        s_im = pltpu.roll(s_im, BATCH, 0)
        h_ref[pl.ds(r0, 8), 0:128] = jnp.where(low, h_re, s_re)
        h_ref[pl.ds(r0, 8), 128:256] = jnp.where(low, h_im, s_im)
        t_re = a_re * s_re - a_im * s_im + e_re
        t_im = a_re * s_im + a_im * s_re + e_im
        n_re = jnp.where(low, pltpu.roll(t_re, BATCH, 0), t_re)
        n_im = jnp.where(low, pltpu.roll(t_im, BATCH, 0), t_im)
        return n_re, n_im

    zero = jnp.zeros((8, 128), F32)
    lax.fori_loop(0, S5_CHUNKS // 8, body, (zero, zero))
    y_inter = jnp.dot(h_ref[...].astype(BF16), wy_ref[0], preferred_element_type=F32)
    y0 = jnp.dot(u[:, :S5_TC], m_ref[0, 0], preferred_element_type=F32)
    y1 = jnp.dot(u[:, S5_TC:], m_ref[0, 1], preferred_element_type=F32)
    y_ref[...] = jnp.concatenate([y0, y1], axis=1) + y_inter + d_ref[...] * u.astype(F32)


def _s5(u_chunks, m, we, wy, pw_re, pw_im, d_l):
    w = 2 * S5_TC
    pair = lambda *s: pl.BlockSpec((1,) + s, lambda j: (j,) + (0,) * len(s))
    return pl.pallas_call(
        _s5_kernel,
        out_shape=jax.ShapeDtypeStruct((S5_CHUNKS, C_GROUPS * S5_TC), F32),
        grid=(S5_PAIRS,),
        in_specs=[pl.BlockSpec((S5_CHUNKS, w), lambda j: (0, j)),
                  pair(2, S5_TC, S5_TC), pair(w, 4 * C_STATE), pair(4 * C_STATE, w),
                  pair(1, 2 * C_STATE), pair(1, 2 * C_STATE),
                  pl.BlockSpec((1, w), lambda j: (0, j))],
        out_specs=pl.BlockSpec((S5_CHUNKS, w), lambda j: (0, j)),
        scratch_shapes=[pltpu.VMEM((S5_CHUNKS, 4 * C_STATE), F32), pltpu.VMEM((S5_CHUNKS, 4 * C_STATE), F32)],
        compiler_params=_cparams(("arbitrary",)),
        name="s5",
    )(u_chunks, m, we, wy, pw_re, pw_im, d_l)


def _to_chunk_rows(cu):
    n = SEQ // S5_CHUNK
    a = cu.reshape(BATCH, n, S5_CHUNK, C_GROUPS, C_GROUP_CH)
    return a.transpose(1, 0, 3, 2, 4).reshape(S5_CHUNKS, C_GROUPS * S5_TC)


def _from_chunk_rows(y):
    n = SEQ // S5_CHUNK
    a = y.reshape(n, BATCH, C_GROUPS, S5_CHUNK, C_GROUP_CH)
    return a.transpose(1, 0, 3, 2, 4).reshape(TOKENS, C_WIDTH)


def _gelu_tanh(x):
    return 0.5 * x * (1.0 + jnp.tanh(math.sqrt(2.0 / math.pi) * (x + 0.044715 * (x * x * x))))


def _out_kernel(oa_ref, ob_ref, yc_ref, cg_ref, x_ref, gate_ref, gpost_ref, wglu_ref, bglu_ref, wout_ref, o_ref):
    y = _gelu_tanh(yc_ref[...])
    z = jnp.dot(y.astype(BF16), wglu_ref[...], preferred_element_type=F32) + bglu_ref[...]
    y = y * jax.nn.sigmoid(z)
    oc = (y * _silu(cg_ref[...].astype(F32))).astype(BF16)
    acc = jnp.dot(oa_ref[...], wout_ref[0:A_WIDTH, :], preferred_element_type=F32)
    acc = acc + jnp.dot(ob_ref[...], wout_ref[A_WIDTH:A_WIDTH + B_WIDTH, :], preferred_element_type=F32)
    acc = acc + jnp.dot(oc, wout_ref[A_WIDTH + B_WIDTH:, :], preferred_element_type=F32)
    ms = jnp.mean(acc * acc, axis=-1, keepdims=True)
    out = acc * lax.rsqrt(ms + EPS) * gpost_ref[...]
    o_ref[...] = x_ref[...] + gate_ref[0] * out


def _out(oa, ob, yc, cg, x2, gate, g_post, w_glu, b_glu, w_out):
    tm = OUT_TM
    steps_per_batch = SEQ // tm
    row = lambda i: (i, 0)
    const = lambda i: (0, 0)
    return pl.pallas_call(
        _out_kernel,
        out_shape=jax.ShapeDtypeStruct((TOKENS, D_MODEL), F32),
        grid=(TOKENS // tm,),
        in_specs=[pl.BlockSpec((tm, A_WIDTH), row),
                  pl.BlockSpec((tm, B_WIDTH), row),
                  pl.BlockSpec((tm, C_WIDTH), row),
                  pl.BlockSpec((tm, C_WIDTH), row),
                  pl.BlockSpec((tm, D_MODEL), row),
                  pl.BlockSpec((1, 1, D_MODEL), lambda i: (i // steps_per_batch, 0, 0)),
                  pl.BlockSpec((1, D_MODEL), const),
                  pl.BlockSpec((C_WIDTH, C_WIDTH), const),
                  pl.BlockSpec((1, C_WIDTH), const),
                  pl.BlockSpec((2 * D_MODEL, D_MODEL), const)],
        out_specs=pl.BlockSpec((tm, D_MODEL), row),
        compiler_params=_cparams(("arbitrary",)),
        name="out",
    )(oa, ob, yc, cg, x2, gate, g_post.reshape(1, D_MODEL), w_glu, b_glu.reshape(1, C_WIDTH), w_out)


def _permute_w_in(w_in):
    sizes = (A_WIDTH, A_KV_WIDTH, A_KV_WIDTH, A_WIDTH, B_QK_WIDTH, B_QK_WIDTH, B_WIDTH, B_GATE_RANK,
             B_WIDTH, C_WIDTH, C_WIDTH)
    pieces = []
    off = 0
    for s in sizes:
        pieces.append(w_in[:, off:off + s])
        off += s
    a_q, a_k, a_v, a_g, b_q, b_k, b_v, b_lr, b_g, c_u, c_g = pieces

    def dup_heads(w):
        w4 = w.reshape(D_MODEL, A_KV_HEADS, A_HEAD_DIM)
        return jnp.concatenate([w4, w4], axis=-1).reshape(D_MODEL, 2 * A_KV_WIDTH)

    b_lr = jnp.pad(b_lr, ((0, 0), (0, LR_PAD - B_GATE_RANK)))
    cols = [a_q, dup_heads(a_k), dup_heads(a_v), a_g, b_q, b_k, b_v, b_lr, b_g, c_u, c_g]
    return jnp.concatenate(cols, axis=1).astype(BF16)


def _layer(x2, c_pad, bias, w_mod, b_mod, g_pre, g_post, w_in, sinks, w_alpha, b_alpha, g_gla,
           a_re, a_im, log_dt, b_re, b_im, c_re, c_im, d, w_glu, b_glu, w_out):
    mod = _mod(c_pad, w_mod, b_mod)[:BATCH]
    shift, scale, gate = (m.reshape(BATCH, 1, D_MODEL) for m in jnp.split(mod, 3, axis=-1))
    w_alpha_pad = jnp.pad(w_alpha, ((0, LR_PAD - B_GATE_RANK), (0, 0))).astype(BF16)
    aq, ak, av, ag, bq, bk, bv, bg, cu, cg, log_a = _proj(
        x2, scale, shift, g_pre, _permute_w_in(w_in), w_alpha_pad, b_alpha)
    o_a = _attn(sinks, aq, ak, av, ag, bias)
    o_b = _gla(log_a, bq, bk, bv, bg, g_gla)
    s5_ops = _s5_assemble(*_s5prep(a_re, a_im, log_dt, b_re, b_im, c_re, c_im), d)
    y_c = _from_chunk_rows(_s5(_to_chunk_rows(cu), *s5_ops))
    return _out(o_a, o_b, y_c, cg, x2, gate, g_post, w_glu.astype(BF16), b_glu, w_out.astype(BF16))


def kernel(x, c, w_mod, b_mod, g_pre, g_post, w_in, attn_sinks, gla_w_alpha, gla_b_alpha, gla_norm_g,
           s5_a_re, s5_a_im, s5_log_dt, s5_b_re, s5_b_im, s5_c_re, s5_c_im, s5_d, s5_w_glu, s5_b_glu, w_out):
    x2 = x.reshape(TOKENS, D_MODEL)
    c_pad = jnp.pad(c, ((0, 8 - BATCH), (0, 0)))
    bias = _attn_bias()
    for l in range(w_mod.shape[0]):
        x2 = _layer(x2, c_pad, bias, w_mod[l], b_mod[l], g_pre[l], g_post[l], w_in[l], attn_sinks[l],
                    gla_w_alpha[l], gla_b_alpha[l], gla_norm_g[l], s5_a_re[l], s5_a_im[l], s5_log_dt[l],
                    s5_b_re[l], s5_b_im[l], s5_c_re[l], s5_c_im[l], s5_d[l], s5_w_glu[l], s5_b_glu[l],
                    w_out[l])
    return x2.reshape(x.shape)
```

```python
import math

import jax
import jax.numpy as jnp
from jax import lax
from jax.experimental import pallas as pl
from jax.experimental.pallas import tpu as pltpu

F32 = jnp.float32
BF16 = jnp.bfloat16
HIGHEST = lax.Precision.HIGHEST

D_MODEL = 1024
BATCH = 4
SEQ = 4096
TOKENS = BATCH * SEQ
EPS = 1e-6

A_WIDTH = 1024
A_HEAD_DIM = 64
A_Q_HEADS = 16
A_KV_HEADS = 4
A_KV_WIDTH = A_KV_HEADS * A_HEAD_DIM
A_BLOCK = 128
WINDOW = 128

B_WIDTH = 512
B_HEADS = 4
B_DK = 64
B_DV = 128
B_QK_WIDTH = 256
B_GATE_RANK = 16
B_GATE_TAU = 16.0
GLA_BLOCK = 64

C_WIDTH = 512
C_GROUP_CH = 16
C_GROUPS = 32
C_STATE = 64
S5_CHUNK = 16
S5_NCHUNK = SEQ // S5_CHUNK
S5_TC = S5_CHUNK * C_GROUP_CH
S5_GB = 8
S5_NGB = C_GROUPS // S5_GB
S5_PAIRS_PER_GB = S5_GB // 2

LANES = 128
LR_PAD = LANES

V7X_VMEM_LIMIT = 56 * 1024 * 1024

PROJ_TM = 512
OUT_TM = 512

_PROJ_LAYOUT = {}
_off = 0
for _name, _w in (("aq", A_WIDTH), ("ak", 2 * A_KV_WIDTH), ("av", 2 * A_KV_WIDTH), ("ag", A_WIDTH),
                  ("bq", B_QK_WIDTH), ("bk", B_QK_WIDTH), ("bv", B_WIDTH), ("blr", LR_PAD),
                  ("bg", B_WIDTH), ("cu", C_WIDTH), ("cg", C_WIDTH)):
    _PROJ_LAYOUT[_name] = (_off, _w)
    _off += _w
PROJ_COLS_PADDED = _off
_PROJ_OUTS = (("aq", BF16), ("ak", BF16), ("av", BF16), ("ag", BF16), ("bq", BF16), ("bk", BF16),
              ("bv", BF16), ("bg", BF16), ("cu", F32), ("cg", BF16))


def _silu(x):
    return x * jax.nn.sigmoid(x)


def _cparams(semantics):
    return pltpu.CompilerParams(dimension_semantics=semantics, vmem_limit_bytes=V7X_VMEM_LIMIT)


def _mod_kernel(c_ref, w_ref, b_ref, o_ref):
    c = c_ref[...]
    o_ref[...] = jnp.dot(_silu(c), w_ref[...], preferred_element_type=F32, precision=HIGHEST) + b_ref[...]


def _mod(c_pad, w_mod, b_mod):
    n = 3 * D_MODEL
    tn = 768
    return pl.pallas_call(
        _mod_kernel,
        out_shape=jax.ShapeDtypeStruct((8, n), F32),
        grid=(n // tn,),
        in_specs=[pl.BlockSpec((8, D_MODEL), lambda j: (0, 0)),
                  pl.BlockSpec((D_MODEL, tn), lambda j: (0, j)),
                  pl.BlockSpec((1, tn), lambda j: (0, j))],
        out_specs=pl.BlockSpec((8, tn), lambda j: (0, j)),
        compiler_params=_cparams(("arbitrary",)),
        name="mod",
    )(c_pad, w_mod, b_mod.reshape(1, n))


def _proj_kernel(x_ref, scale_ref, shift_ref, gpre_ref, w_ref, walpha_ref, balpha_ref, *out_refs):
    x = x_ref[...]
    ms = jnp.mean(x * x, axis=-1, keepdims=True)
    y = x * lax.rsqrt(ms + EPS) * gpre_ref[...]
    h = (y * (1.0 + scale_ref[0]) + shift_ref[0]).astype(BF16)
    for (name, _), o_ref in zip(_PROJ_OUTS, out_refs[:-1]):
        off, width = _PROJ_LAYOUT[name]
        o_ref[...] = jnp.dot(h, w_ref[:, off:off + width], preferred_element_type=F32).astype(o_ref.dtype)
    off, width = _PROJ_LAYOUT["blr"]
    lr = jnp.dot(h, w_ref[:, off:off + width], preferred_element_type=F32).astype(BF16)
    logits = jnp.dot(lr, walpha_ref[...], preferred_element_type=F32) + balpha_ref[...]
    log_sig = jnp.minimum(logits, 0.0) - jnp.log(1.0 + jnp.exp(-jnp.abs(logits)))
    out_refs[-1][...] = log_sig * (1.0 / B_GATE_TAU)


def _proj(x2, scale, shift, g_pre, w_perm, w_alpha_pad, b_alpha):
    tm = PROJ_TM
    steps_per_batch = SEQ // tm
    row = lambda i: (i, 0)
    per_batch = lambda i: (i // steps_per_batch, 0, 0)
    const = lambda i: (0, 0)
    out_shape = [jax.ShapeDtypeStruct((TOKENS, _PROJ_LAYOUT[n][1]), dt) for n, dt in _PROJ_OUTS]
    out_shape.append(jax.ShapeDtypeStruct((TOKENS, B_QK_WIDTH), F32))
    out_specs = [pl.BlockSpec((tm, _PROJ_LAYOUT[n][1]), row) for n, _ in _PROJ_OUTS]
    out_specs.append(pl.BlockSpec((tm, B_QK_WIDTH), row))
    return pl.pallas_call(
        _proj_kernel,
        out_shape=out_shape,
        grid=(TOKENS // tm,),
        in_specs=[pl.BlockSpec((tm, D_MODEL), row),
                  pl.BlockSpec((1, 1, D_MODEL), per_batch),
                  pl.BlockSpec((1, 1, D_MODEL), per_batch),
                  pl.BlockSpec((1, D_MODEL), const),
                  pl.BlockSpec((D_MODEL, PROJ_COLS_PADDED), const),
                  pl.BlockSpec((LR_PAD, B_QK_WIDTH), const),
                  pl.BlockSpec((1, B_QK_WIDTH), const)],
        out_specs=out_specs,
        compiler_params=_cparams(("arbitrary",)),
        name="proj",
    )(x2, scale, shift, g_pre.reshape(1, D_MODEL), w_perm, w_alpha_pad, b_alpha.reshape(1, B_QK_WIDTH))


def _attn_bias():
    i = jnp.arange(A_BLOCK)[:, None]
    j = jnp.arange(2 * A_BLOCK)[None, :]
    dist = i + A_BLOCK - j
    valid = (dist >= 0) & (dist < WINDOW)
    slopes = jnp.exp2(-8.0 * jnp.arange(1, A_Q_HEADS + 1, dtype=F32) / A_Q_HEADS)
    bias = -slopes[:, None, None] * dist.astype(F32)[None]
    return jnp.where(valid[None], bias, -jnp.inf)


def _attn_kernel(sink_ref, q_ref, kp_ref, kc_ref, vp_ref, vc_ref, g_ref, bias_ref, o_ref):
    n = pl.program_id(1)
    pen = jnp.where(n == 0, -jnp.inf, 0.0).astype(F32)
    col = lax.broadcasted_iota(jnp.int32, (1, 2 * A_BLOCK), 1)
    pen_row = jnp.where(col < A_BLOCK, pen, 0.0)
    lane = lax.broadcasted_iota(jnp.int32, (2 * A_BLOCK, LANES), 1)
    lo = lane < A_HEAD_DIM
    nt = (((1,), (1,)), ((), ()))
    for kv in range(A_KV_HEADS):
        ksl = slice(LANES * kv, LANES * (kv + 1))
        kk = jnp.concatenate([kp_ref[:, ksl], kc_ref[:, ksl]], axis=0)
        vv = jnp.concatenate([vp_ref[:, ksl], vc_ref[:, ksl]], axis=0)
        zero = jnp.zeros_like(kk)
        k_halves = (jnp.where(lo, kk, zero), jnp.where(lo, zero, kk))
        v_halves = (jnp.where(lo, vv, zero), jnp.where(lo, zero, vv))
        for pr in range(2):
            qsl = slice(LANES * (2 * kv + pr), LANES * (2 * kv + pr + 1))
            qp = q_ref[:, qsl]
            acc = jnp.zeros((A_BLOCK, LANES), F32)
            for e in range(2):
                hd = 4 * kv + 2 * pr + e
                s = lax.dot_general(qp, k_halves[e], nt, preferred_element_type=F32)
                s = s * (A_HEAD_DIM ** -0.5) + bias_ref[hd] + pen_row
                sink = sink_ref[hd]
                m = jnp.maximum(jnp.max(s, axis=-1, keepdims=True), sink)
                p = jnp.exp(s - m)
                den = jnp.sum(p, axis=-1, keepdims=True) + jnp.exp(sink - m)
                probs = (p / den).astype(BF16)
                acc = acc + jnp.dot(probs, v_halves[e], preferred_element_type=F32)
            gate = g_ref[:, qsl].astype(F32)
            o_ref[:, qsl] = (acc * _silu(gate)).astype(o_ref.dtype)


def _attn(sinks, q, kd, vd, ag, bias):
    nb = SEQ // A_BLOCK
    cur = lambda b, n: (b * nb + n, 0)
    prev = lambda b, n: (b * nb + jnp.maximum(n - 1, 0), 0)
    return pl.pallas_call(
        _attn_kernel,
        out_shape=jax.ShapeDtypeStruct((TOKENS, A_WIDTH), BF16),
        grid=(BATCH, nb),
        in_specs=[pl.BlockSpec(memory_space=pltpu.SMEM),
                  pl.BlockSpec((A_BLOCK, A_WIDTH), cur),
                  pl.BlockSpec((A_BLOCK, 2 * A_KV_WIDTH), prev),
                  pl.BlockSpec((A_BLOCK, 2 * A_KV_WIDTH), cur),
                  pl.BlockSpec((A_BLOCK, 2 * A_KV_WIDTH), prev),
                  pl.BlockSpec((A_BLOCK, 2 * A_KV_WIDTH), cur),
                  pl.BlockSpec((A_BLOCK, A_WIDTH), cur),
                  pl.BlockSpec((A_Q_HEADS, A_BLOCK, 2 * A_BLOCK), lambda b, n: (0, 0, 0))],
        out_specs=pl.BlockSpec((A_BLOCK, A_WIDTH), cur),
        compiler_params=_cparams(("arbitrary", "arbitrary")),
        name="attn",
    )(sinks, q, kd, kd, vd, vd, ag, bias)


def _gla_kernel(la_ref, q_ref, k_ref, v_ref, g_ref, gn_ref, o_ref, st_ref):
    cb = GLA_BLOCK

    @pl.when(pl.program_id(0) == 0)
    def _():
        st_ref[...] = jnp.zeros_like(st_ref)

    r = lax.broadcasted_iota(jnp.int32, (cb, cb), 0)
    c = lax.broadcasted_iota(jnp.int32, (cb, cb), 1)
    tri = (c <= r).astype(F32)
    lane = lax.broadcasted_iota(jnp.int32, (cb, B_QK_WIDTH), 1)
    head_masks = [(lane >= B_DK * h) & (lane < B_DK * (h + 1)) for h in range(B_HEADS)]
    rr = lax.broadcasted_iota(jnp.int32, (B_HEADS * cb, cb), 0)
    cc = lax.broadcasted_iota(jnp.int32, (B_HEADS * cb, cb), 1)
    causal = cc <= (rr & (cb - 1))
    nt = (((1,), (1,)), ((), ()))
    tn = (((0,), (0,)), ((), ()))
    for b in range(BATCH):
        la = la_ref[b]
        bc = jnp.dot(tri, la, preferred_element_type=F32, precision=HIGHEST)
        bl = bc[cb - 1:cb, :]
        q = q_ref[b].astype(F32) * (B_DK ** -0.5)
        k = k_ref[b].astype(F32)
        qs = q * jnp.exp(bc)
        ks = (k * jnp.exp(-bc)).astype(BF16)
        kh = k * jnp.exp(bl - bc)
        dec = jnp.exp(bl)
        qst = jnp.concatenate([jnp.where(m, qs, 0.0) for m in head_masks], axis=0).astype(BF16)
        a_all = lax.dot_general(qst, ks, nt, preferred_element_type=F32)
        a_all = jnp.where(causal, a_all, 0.0)
        st = st_ref[b]
        oi_all = lax.dot_general(qst, st.astype(BF16), nt, preferred_element_type=F32)
        v = v_ref[b]
        for h in range(B_HEADS):
            vsl = slice(B_DV * h, B_DV * (h + 1))
            a_h = a_all[cb * h:cb * (h + 1)].astype(BF16)
            o_h = jnp.dot(a_h, v[:, vsl], preferred_element_type=F32) + oi_all[cb * h:cb * (h + 1)]
            ms = jnp.mean(o_h * o_h, axis=-1, keepdims=True)
            o_n = o_h * lax.rsqrt(ms + EPS) * gn_ref[:, vsl]
            gate = g_ref[b, :, vsl].astype(F32)
            o_ref[b, :, vsl] = (o_n * _silu(gate)).astype(o_ref.dtype)
        vst = jnp.concatenate([v[:, B_DV * h:B_DV * (h + 1)] for h in range(B_HEADS)], axis=0)
        kst = jnp.concatenate([jnp.where(m, kh, 0.0) for m in head_masks], axis=0).astype(BF16)
        upd = lax.dot_general(vst, kst, tn, preferred_element_type=F32)
        st_ref[b] = st * dec + upd


def _gla(log_a, bq, bk, bv, bg, g_gla):
    cb = GLA_BLOCK
    blk = lambda w: pl.BlockSpec((BATCH, cb, w), lambda i: (0, i, 0))
    r3 = lambda a: a.reshape(BATCH, SEQ, a.shape[-1])
    out = pl.pallas_call(
        _gla_kernel,
        out_shape=jax.ShapeDtypeStruct((BATCH, SEQ, B_WIDTH), BF16),
        grid=(SEQ // cb,),
        in_specs=[blk(B_QK_WIDTH), blk(B_QK_WIDTH), blk(B_QK_WIDTH), blk(B_WIDTH), blk(B_WIDTH),
                  pl.BlockSpec((1, B_WIDTH), lambda i: (0, 0))],
        out_specs=blk(B_WIDTH),
        scratch_shapes=[pltpu.VMEM((BATCH, B_DV, B_QK_WIDTH), F32)],
        compiler_params=_cparams(("arbitrary",)),
        name="gla",
    )(r3(log_a), r3(bq), r3(bk), r3(bv), r3(bg), g_gla.reshape(1, B_WIDTH))
    return out.reshape(TOKENS, B_WIDTH)


def _s5prep_kernel(ar_ref, ai_ref, ldt_ref, bre_ref, bim_ref, btre_ref, btim_ref, cre_ref, cim_ref,
                   kk_ref, were_ref, weim_ref, wyre_ref, wyim_ref, pre_ref, pim_ref):
    ar = ar_ref[0]
    ai = ai_ref[0]
    dt = jnp.exp(ldt_ref[0])
    ardt = ar * dt
    aidt = ai * dt

    def power(kf):
        mag = jnp.exp(kf * ardt)
        ang = kf * aidt
        return mag * jnp.cos(ang), mag * jnp.sin(ang)

    one = jnp.ones((1, 1), F32)
    abar_re, abar_im = power(one)
    den = ar * ar + ai * ai
    num_re = abar_re - 1.0
    f_re = (num_re * ar + abar_im * ai) / den
    f_im = (abar_im * ar - num_re * ai) / den

    def cmul(xr, xi, yr, yi):
        return xr * yr - xi * yi, xr * yi + xi * yr

    rows = S5_TC
    lag = (lax.broadcasted_iota(jnp.int32, (rows, 1), 0) // C_GROUP_CH).astype(F32)
    tile16 = lambda a: jnp.concatenate([a] * S5_CHUNK, axis=0)
    ct_re, ct_im = tile16(cre_ref[0]), tile16(cim_ref[0])
    bt_re, bt_im = tile16(btre_ref[0]), tile16(btim_ref[0])

    g_re, g_im = cmul(*power(lag), f_re, f_im)
    w_re, w_im = cmul(g_re, g_im, ct_re, ct_im)
    kk_ref[0] = (jnp.dot(w_re, bre_ref[0], preferred_element_type=F32, precision=HIGHEST)
                 - jnp.dot(w_im, bim_ref[0], preferred_element_type=F32, precision=HIGHEST))
    e_re, e_im = cmul(*cmul(*power((S5_CHUNK - 1.0) - lag), f_re, f_im), bt_re, bt_im)
    were_ref[0] = e_re
    weim_ref[0] = e_im
    y_re, y_im = cmul(*power(lag + 1.0), ct_re, ct_im)
    wyre_ref[0] = y_re
    wyim_ref[0] = y_im
    p_re, p_im = power(one * float(S5_CHUNK))
    pre_ref[0] = p_re
    pim_ref[0] = p_im


def _s5prep(a_re, a_im, log_dt, b_re, b_im, c_re, c_im):
    g, p, ch = C_GROUPS, C_STATE, C_GROUP_CH
    row = lambda a: a.reshape(g, 1, p)
    ldt = jnp.broadcast_to(log_dt[:, None, None], (g, 1, p))
    b_t = lambda a: jnp.swapaxes(a, 1, 2)
    spec = lambda s1, s2: pl.BlockSpec((1, s1, s2), lambda i: (i, 0, 0))
    out_shape = [jax.ShapeDtypeStruct((g, S5_TC, ch), F32)] + [jax.ShapeDtypeStruct((g, S5_TC, p), F32)] * 4 \
        + [jax.ShapeDtypeStruct((g, 1, p), F32)] * 2
    out_specs = [spec(S5_TC, ch)] + [spec(S5_TC, p)] * 4 + [spec(1, p)] * 2
    return pl.pallas_call(
        _s5prep_kernel,
        out_shape=out_shape,
        grid=(g,),
        in_specs=[spec(1, p), spec(1, p), spec(1, p), spec(p, ch), spec(p, ch),
                  spec(ch, p), spec(ch, p), spec(ch, p), spec(ch, p)],
        out_specs=out_specs,
        compiler_params=_cparams(("arbitrary",)),
        name="s5prep",
    )(row(a_re), row(a_im), ldt, b_re, b_im, b_t(b_re), b_t(b_im), c_re, c_im)


def _s5_assemble(kk, we_re, we_im, wyt_re, wyt_im, p_re, p_im, d):
    g, t, ch, p = C_GROUPS, S5_CHUNK, C_GROUP_CH, C_STATE
    npair = g // 2
    k4 = kk.reshape(g, t, ch, ch)
    z = jnp.concatenate([jnp.zeros_like(k4), k4], axis=1)
    idx = jnp.arange(t)[None, :] - jnp.arange(t)[:, None] + t
    mt = z[:, idx]
    mt = mt.transpose(0, 2, 3, 1, 4).reshape(S5_NGB, S5_GB, S5_TC, S5_TC).astype(BF16)
    pr = lambda a: a.reshape(npair, 2, *a.shape[1:])
    a, b = pr(jnp.swapaxes(we_re, 1, 2)), pr(jnp.swapaxes(we_im, 1, 2))
    zero = jnp.zeros((npair, p, S5_TC), F32)
    wet = jnp.concatenate([
        jnp.concatenate([a[:, 0], zero], axis=-1),
        jnp.concatenate([zero, a[:, 1]], axis=-1),
        jnp.concatenate([b[:, 0], zero], axis=-1),
        jnp.concatenate([zero, b[:, 1]], axis=-1)], axis=1)
    wet = wet.reshape(S5_NGB, S5_PAIRS_PER_GB, 4 * p, 2 * S5_TC).astype(BF16)
    a, b = pr(wyt_re), pr(-wyt_im)
    zero = jnp.zeros((npair, S5_TC, p), F32)
    wyt = jnp.concatenate([
        jnp.concatenate([a[:, 0], zero, b[:, 0], zero], axis=-1),
        jnp.concatenate([zero, a[:, 1], zero, b[:, 1]], axis=-1)], axis=1)
    wyt = wyt.reshape(S5_NGB, S5_PAIRS_PER_GB, 2 * S5_TC, 4 * p).astype(BF16)
    pw_re = p_re.reshape(S5_NGB, 1, S5_GB * p)
    pw_im = p_im.reshape(S5_NGB, 1, S5_GB * p)
    return mt, wet, wyt, pw_re, pw_im, d.reshape(1, C_WIDTH)


def _s5_kernel(u_ref, mt_ref, wet_ref, wyt_ref, are_ref, aim_ref, d_ref, y_ref,
               ut_ref, yt_ref, ere_ref, eim_ref, hre_ref, him_ref):
    nck, t_len, ch = S5_NCHUNK, S5_CHUNK, C_GROUP_CH
    nt = (((1,), (1,)), ((), ()))
    for t in range(t_len):
        xt = u_ref[pl.ds(t, nck, stride=t_len), :].T
        for g in range(S5_GB):
            ut_ref[g, ch * t:ch * (t + 1), :] = xt[ch * g:ch * (g + 1), :].astype(BF16)
    for j in range(S5_PAIRS_PER_GB):
        u0 = ut_ref[2 * j]
        u1 = ut_ref[2 * j + 1]
        et = jnp.dot(wet_ref[0, j], jnp.concatenate([u0, u1], axis=0), preferred_element_type=F32)
        e = et.T
        ere_ref[:, LANES * j:LANES * (j + 1)] = e[:, :LANES]
        eim_ref[:, LANES * j:LANES * (j + 1)] = e[:, LANES:]
        yt_ref[2 * j] = jnp.dot(mt_ref[0, 2 * j], u0, preferred_element_type=F32)
        yt_ref[2 * j + 1] = jnp.dot(mt_ref[0, 2 * j + 1], u1, preferred_element_type=F32)

    a_re = are_ref[0]
    a_im = aim_ref[0]

    def body(i, carry):
        h_re, h_im = carry
        hre_ref[pl.ds(i, 1), :] = h_re
        him_ref[pl.ds(i, 1), :] = h_im
        e_re = ere_ref[pl.ds(i, 1), :]
        e_im = eim_ref[pl.ds(i, 1), :]
        return a_re * h_re - a_im * h_im + e_re, a_re * h_im + a_im * h_re + e_im

    zero = jnp.zeros((1, S5_GB * C_STATE), F32)
    lax.fori_loop(0, nck, body, (zero, zero))

    for j in range(S5_PAIRS_PER_GB):
        sl = slice(LANES * j, LANES * (j + 1))
        hp = jnp.concatenate([hre_ref[:, sl], him_ref[:, sl]], axis=1).astype(BF16)
        yi = lax.dot_general(wyt_ref[0, j], hp, nt, preferred_element_type=F32)
        yt_ref[2 * j] += yi[:S5_TC]
        yt_ref[2 * j + 1] += yi[S5_TC:]
    for t in range(t_len):
        ytt = jnp.concatenate([yt_ref[g, ch * t:ch * (t + 1), :] for g in range(S5_GB)], axis=0)
        rows = pl.ds(t, nck, stride=t_len)
        y_ref[rows, :] = ytt.T + d_ref[...] * u_ref[rows, :]


def _s5(cu, mt, wet, wyt, pw_re, pw_im, d_row):
    p4 = 4 * C_STATE
    tok = pl.BlockSpec((SEQ, LANES), lambda gb, b: (b, gb))
    per_gb = lambda *s: pl.BlockSpec((1,) + s, lambda gb, b: (gb,) + (0,) * len(s))
    state = pltpu.VMEM((S5_NCHUNK, S5_GB * C_STATE), F32)
    return pl.pallas_call(
        _s5_kernel,
        out_shape=jax.ShapeDtypeStruct((TOKENS, C_WIDTH), F32),
        grid=(S5_NGB, BATCH),
        in_specs=[tok,
                  per_gb(S5_GB, S5_TC, S5_TC),
                  per_gb(S5_PAIRS_PER_GB, p4, 2 * S5_TC),
                  per_gb(S5_PAIRS_PER_GB, 2 * S5_TC, p4),
                  per_gb(1, S5_GB * C_STATE), per_gb(1, S5_GB * C_STATE),
                  pl.BlockSpec((1, LANES), lambda gb, b: (0, gb))],
        out_specs=tok,
        scratch_shapes=[pltpu.VMEM((S5_GB, S5_TC, S5_NCHUNK), BF16),
                        pltpu.VMEM((S5_GB, S5_TC, S5_NCHUNK), F32),
                        state, state, state, state],
        compiler_params=_cparams(("arbitrary", "arbitrary")),
        name="s5",
    )(cu, mt, wet, wyt, pw_re, pw_im, d_row)


def _gelu_tanh(x):
    return 0.5 * x * (1.0 + jnp.tanh(math.sqrt(2.0 / math.pi) * (x + 0.044715 * (x * x * x))))


def _out_kernel(oa_ref, ob_ref, yc_ref, cg_ref, x_ref, gate_ref, gpost_ref, wglu_ref, bglu_ref, wout_ref, o_ref):
    y = _gelu_tanh(yc_ref[...])
    z = jnp.dot(y.astype(BF16), wglu_ref[...], preferred_element_type=F32) + bglu_ref[...]
    y = y * jax.nn.sigmoid(z)
    oc = (y * _silu(cg_ref[...].astype(F32))).astype(BF16)
    acc = jnp.dot(oa_ref[...], wout_ref[0:A_WIDTH, :], preferred_element_type=F32)
    acc = acc + jnp.dot(ob_ref[...], wout_ref[A_WIDTH:A_WIDTH + B_WIDTH, :], preferred_element_type=F32)
    acc = acc + jnp.dot(oc, wout_ref[A_WIDTH + B_WIDTH:, :], preferred_element_type=F32)
    ms = jnp.mean(acc * acc, axis=-1, keepdims=True)
    out = acc * lax.rsqrt(ms + EPS) * gpost_ref[...]
    o_ref[...] = x_ref[...] + gate_ref[0] * out


def _out(oa, ob, yc, cg, x2, gate, g_post, w_glu, b_glu, w_out):
    tm = OUT_TM
    steps_per_batch = SEQ // tm
    row = lambda i: (i, 0)
    const = lambda i: (0, 0)
    return pl.pallas_call(
        _out_kernel,
        out_shape=jax.ShapeDtypeStruct((TOKENS, D_MODEL), F32),
        grid=(TOKENS // tm,),
        in_specs=[pl.BlockSpec((tm, A_WIDTH), row),
                  pl.BlockSpec((tm, B_WIDTH), row),
                  pl.BlockSpec((tm, C_WIDTH), row),
                  pl.BlockSpec((tm, C_WIDTH), row),
                  pl.BlockSpec((tm, D_MODEL), row),
                  pl.BlockSpec((1, 1, D_MODEL), lambda i: (i // steps_per_batch, 0, 0)),
                  pl.BlockSpec((1, D_MODEL), const),
                  pl.BlockSpec((C_WIDTH, C_WIDTH), const),
                  pl.BlockSpec((1, C_WIDTH), const),
                  pl.BlockSpec((2 * D_MODEL, D_MODEL), const)],
        out_specs=pl.BlockSpec((tm, D_MODEL), row),
        compiler_params=_cparams(("arbitrary",)),
        name="out",
    )(oa, ob, yc, cg, x2, gate, g_post.reshape(1, D_MODEL), w_glu, b_glu.reshape(1, C_WIDTH), w_out)


def _permute_w_in(w_in):
    sizes = (A_WIDTH, A_KV_WIDTH, A_KV_WIDTH, A_WIDTH, B_QK_WIDTH, B_QK_WIDTH, B_WIDTH, B_GATE_RANK,
             B_WIDTH, C_WIDTH, C_WIDTH)
    pieces = []
    off = 0
    for s in sizes:
        pieces.append(w_in[:, off:off + s])
        off += s
    a_q, a_k, a_v, a_g, b_q, b_k, b_v, b_lr, b_g, c_u, c_g = pieces

    def dup_heads(w):
        w4 = w.reshape(D_MODEL, A_KV_HEADS, A_HEAD_DIM)
        return jnp.concatenate([w4, w4], axis=-1).reshape(D_MODEL, 2 * A_KV_WIDTH)

    b_lr = jnp.pad(b_lr, ((0, 0), (0, LR_PAD - B_GATE_RANK)))
    cols = [a_q, dup_heads(a_k), dup_heads(a_v), a_g, b_q, b_k, b_v, b_lr, b_g, c_u, c_g]
    return jnp.concatenate(cols, axis=1).astype(BF16)


def _layer(x2, c_pad, bias, w_mod, b_mod, g_pre, g_post, w_in, sinks, w_alpha, b_alpha, g_gla,
           a_re, a_im, log_dt, b_re, b_im, c_re, c_im, d, w_glu, b_glu, w_out):
    mod = _mod(c_pad, w_mod, b_mod)[:BATCH]
    shift, scale, gate = (m.reshape(BATCH, 1, D_MODEL) for m in jnp.split(mod, 3, axis=-1))
    w_alpha_pad = jnp.pad(w_alpha, ((0, LR_PAD - B_GATE_RANK), (0, 0))).astype(BF16)
    aq, ak, av, ag, bq, bk, bv, bg, cu, cg, log_a = _proj(
        x2, scale, shift, g_pre, _permute_w_in(w_in), w_alpha_pad, b_alpha)
    o_a = _attn(sinks, aq, ak, av, ag, bias)
    o_b = _gla(log_a, bq, bk, bv, bg, g_gla)
    s5_ops = _s5_assemble(*_s5prep(a_re, a_im, log_dt, b_re, b_im, c_re, c_im), d)
    y_c = _s5(cu, *s5_ops)
    return _out(o_a, o_b, y_c, cg, x2, gate, g_post, w_glu.astype(BF16), b_glu, w_out.astype(BF16))


def kernel(x, c, w_mod, b_mod, g_pre, g_post, w_in, attn_sinks, gla_w_alpha, gla_b_alpha, gla_norm_g,
           s5_a_re, s5_a_im, s5_log_dt, s5_b_re, s5_b_im, s5_c_re, s5_c_im, s5_d, s5_w_glu, s5_b_glu, w_out):
    x2 = x.reshape(TOKENS, D_MODEL)
    c_pad = jnp.pad(c, ((0, 8 - BATCH), (0, 0)))
    bias = _attn_bias()
    for l in range(w_mod.shape[0]):
        x2 = _layer(x2, c_pad, bias, w_mod[l], b_mod[l], g_pre[l], g_post[l], w_in[l], attn_sinks[l],
                    gla_w_alpha[l], gla_b_alpha[l], gla_norm_g[l], s5_a_re[l], s5_a_im[l], s5_log_dt[l],
                    s5_b_re[l], s5_b_im[l], s5_c_re[l], s5_c_im[l], s5_d[l], s5_w_glu[l], s5_b_glu[l],
                    w_out[l])
    return x2.reshape(x.shape)
```

```python
import math

import jax
import jax.numpy as jnp
from jax import lax
from jax.experimental import pallas as pl
from jax.experimental.pallas import tpu as pltpu

F32 = jnp.float32
BF16 = jnp.bfloat16
HIGHEST = lax.Precision.HIGHEST

D_MODEL = 1024
BATCH = 4
SEQ = 4096
TOKENS = BATCH * SEQ
EPS = 1e-6

A_WIDTH = 1024
A_HEAD_DIM = 64
A_Q_HEADS = 16
A_KV_HEADS = 4
A_KV_WIDTH = A_KV_HEADS * A_HEAD_DIM
A_BLOCK = 128
WINDOW = 128

B_WIDTH = 512
B_HEADS = 4
B_DK = 64
B_DV = 128
B_QK_WIDTH = 256
B_GATE_RANK = 16
B_GATE_TAU = 16.0
GLA_BLOCK = 64
C_WIDTH = 512
C_GROUP_CH = 16
C_GROUPS = 32
C_STATE = 64
S5_CHUNK = 16
S5_NCHUNK = SEQ // S5_CHUNK
S5_TC = S5_CHUNK * C_GROUP_CH
S5_GB = 8
S5_NGB = C_GROUPS // S5_GB
S5_PAIRS_PER_GB = S5_GB // 2
S5_POW_ROWS = 24

LANES = 128
LR_PAD = LANES

V7X_VMEM_LIMIT = 56 * 1024 * 1024

PROJ_TM = 512
OUT_TM = 512

_PROJ_LAYOUT = {}
_off = 0
for _name, _w in (("ak", A_KV_WIDTH), ("ag", A_WIDTH),
                  ("bq", B_QK_WIDTH), ("bk", B_QK_WIDTH), ("bv", B_WIDTH), ("blr", LR_PAD),
                  ("bg", B_WIDTH), ("cu", C_WIDTH), ("cg", C_WIDTH)):
    _PROJ_LAYOUT[_name] = (_off, _w)
    _off += _w
PROJ_COLS_PADDED = _off
_PROJ_OUTS = (("ak", BF16), ("ag", BF16), ("bq", BF16), ("bk", BF16),
              ("bv", BF16), ("bg", BF16), ("cu", F32), ("cg", BF16))
PROJ_T_ROWS = A_WIDTH + A_KV_WIDTH


def _silu(x):
    return x * jax.nn.sigmoid(x)


def _cparams(semantics):
    return pltpu.CompilerParams(dimension_semantics=semantics, vmem_limit_bytes=V7X_VMEM_LIMIT)


def _mod_kernel(c_ref, w_ref, b_ref, o_ref):
    c = c_ref[...]
    o_ref[...] = jnp.dot(_silu(c), w_ref[...], preferred_element_type=F32, precision=HIGHEST) + b_ref[...]


def _mod(c_pad, w_mod, b_mod):
    n = 3 * D_MODEL
    tn = 768
    return pl.pallas_call(
        _mod_kernel,
        out_shape=jax.ShapeDtypeStruct((8, n), F32),
        grid=(n // tn,),
        in_specs=[pl.BlockSpec((8, D_MODEL), lambda j: (0, 0)),
                  pl.BlockSpec((D_MODEL, tn), lambda j: (0, j)),
                  pl.BlockSpec((1, tn), lambda j: (0, j))],
        out_specs=pl.BlockSpec((8, tn), lambda j: (0, j)),
        compiler_params=_cparams(("arbitrary",)),
        name="mod",
    )(c_pad, w_mod, b_mod.reshape(1, n))


def _proj_kernel(x_ref, scale_ref, shift_ref, gpre_ref, w_ref, wt_ref, walpha_ref, balpha_ref, *out_refs):
    x = x_ref[...]
    ms = jnp.mean(x * x, axis=-1, keepdims=True)
    y = x * lax.rsqrt(ms + EPS) * gpre_ref[...]
    h = (y * (1.0 + scale_ref[0]) + shift_ref[0]).astype(BF16)
    for (name, _), o_ref in zip(_PROJ_OUTS, out_refs):
        off, width = _PROJ_LAYOUT[name]
        o_ref[...] = jnp.dot(h, w_ref[:, off:off + width], preferred_element_type=F32).astype(o_ref.dtype)
    off, width = _PROJ_LAYOUT["blr"]
    lr = jnp.dot(h, w_ref[:, off:off + width], preferred_element_type=F32).astype(BF16)
    logits = jnp.dot(lr, walpha_ref[...], preferred_element_type=F32) + balpha_ref[...]
    log_sig = jnp.minimum(logits, 0.0) - jnp.log(1.0 + jnp.exp(-jnp.abs(logits)))
    la_ref, qt_ref, vt_ref = out_refs[len(_PROJ_OUTS):]
    la_ref[...] = log_sig * (1.0 / B_GATE_TAU)
    nt = (((1,), (1,)), ((), ()))
    qt_ref[...] = lax.dot_general(wt_ref[0:A_WIDTH, :], h, nt, preferred_element_type=F32).astype(qt_ref.dtype)
    vt_ref[...] = lax.dot_general(wt_ref[A_WIDTH:, :], h, nt, preferred_element_type=F32).astype(vt_ref.dtype)


def _proj(x2, scale, shift, g_pre, w_perm, w_t, w_alpha_pad, b_alpha):
    tm = PROJ_TM
    steps_per_batch = SEQ // tm
    row = lambda i: (i, 0)
    col = lambda i: (0, i)
    per_batch = lambda i: (i // steps_per_batch, 0, 0)
    const = lambda i: (0, 0)
    out_shape = [jax.ShapeDtypeStruct((TOKENS, _PROJ_LAYOUT[n][1]), dt) for n, dt in _PROJ_OUTS]
    out_specs = [pl.BlockSpec((tm, _PROJ_LAYOUT[n][1]), row) for n, _ in _PROJ_OUTS]
    out_shape += [jax.ShapeDtypeStruct((TOKENS, B_QK_WIDTH), F32),
                  jax.ShapeDtypeStruct((A_WIDTH, TOKENS), BF16),
                  jax.ShapeDtypeStruct((A_KV_WIDTH, TOKENS), BF16)]
    out_specs += [pl.BlockSpec((tm, B_QK_WIDTH), row),
                  pl.BlockSpec((A_WIDTH, tm), col),
                  pl.BlockSpec((A_KV_WIDTH, tm), col)]
    return pl.pallas_call(
        _proj_kernel,
        out_shape=out_shape,
        grid=(TOKENS // tm,),
        in_specs=[pl.BlockSpec((tm, D_MODEL), row),
                  pl.BlockSpec((1, 1, D_MODEL), per_batch),
                  pl.BlockSpec((1, 1, D_MODEL), per_batch),
                  pl.BlockSpec((1, D_MODEL), const),
                  pl.BlockSpec((D_MODEL, PROJ_COLS_PADDED), const),
                  pl.BlockSpec((PROJ_T_ROWS, D_MODEL), const),
                  pl.BlockSpec((LR_PAD, B_QK_WIDTH), const),
                  pl.BlockSpec((1, B_QK_WIDTH), const)],
        out_specs=out_specs,
        compiler_params=_cparams(("arbitrary",)),
        name="proj",
    )(x2, scale, shift, g_pre.reshape(1, D_MODEL), w_perm, w_t, w_alpha_pad, b_alpha.reshape(1, B_QK_WIDTH))


def _attn_bias():
    j = jnp.arange(A_BLOCK)[:, None]
    i = jnp.arange(A_BLOCK)[None, :]
    dist = jnp.where(j > i, i + A_BLOCK - j, i - j).astype(F32)
    slopes = jnp.exp2(-8.0 * jnp.arange(1, A_Q_HEADS + 1, dtype=F32) / A_Q_HEADS)
    return -slopes[:, None, None] * dist[None]


def _attn_kernel(sink_ref, qt_ref, kp_ref, kc_ref, vtp_ref, vtc_ref, g_ref, bias_ref, o_ref):
    n = pl.program_id(1)
    pen = jnp.where(n == 0, -jnp.inf, 0.0).astype(F32)
    kj = lax.broadcasted_iota(jnp.int32, (A_BLOCK, A_BLOCK), 0)
    qi = lax.broadcasted_iota(jnp.int32, (A_BLOCK, A_BLOCK), 1)
    from_prev = kj > qi
    zero_rows = jnp.zeros((A_HEAD_DIM, A_BLOCK), BF16)
    group = A_Q_HEADS // A_KV_HEADS

    def scores(hd):
        kvh = hd // group
        sl = slice(LANES * (kvh // 2), LANES * (kvh // 2 + 1))
        qh = qt_ref[A_HEAD_DIM * hd:A_HEAD_DIM * (hd + 1), :]
        qsel = jnp.concatenate([qh, zero_rows] if kvh % 2 == 0 else [zero_rows, qh], axis=0)
        return (jnp.dot(kp_ref[:, sl], qsel, preferred_element_type=F32),
                jnp.dot(kc_ref[:, sl], qsel, preferred_element_type=F32))

    def attend(hd, s_prev, s_cur):
        kvh = hd // group
        vrows = slice(A_HEAD_DIM * kvh, A_HEAD_DIM * (kvh + 1))
        v_both = jnp.concatenate([vtp_ref[vrows, :], vtc_ref[vrows, :]], axis=1)
        s = jnp.where(from_prev, s_prev + pen, s_cur) + bias_ref[hd]
        sink = sink_ref[hd]
        m = jnp.maximum(jnp.max(s, axis=0, keepdims=True), sink)
        p = jnp.exp(s - m)
        den = jnp.sum(p, axis=0, keepdims=True) + jnp.exp(sink - m)
        p_both = jnp.concatenate([jnp.where(from_prev, p, 0.0), jnp.where(from_prev, 0.0, p)],
                                 axis=0).astype(BF16)
        return jnp.dot(v_both, p_both, preferred_element_type=F32) / den

    pending = [scores(hd) for hd in range(A_Q_HEADS)]
    outs = {}
    for hd in range(A_Q_HEADS):
        outs[hd] = attend(hd, *pending[hd])
        if hd % 2 == 1:
            qsl = slice(LANES * (hd // 2), LANES * (hd // 2 + 1))
            o_pair = jnp.concatenate([outs.pop(hd - 1), outs.pop(hd)], axis=0).T
            gate = g_ref[:, qsl].astype(F32)
            o_ref[:, qsl] = (o_pair * _silu(gate)).astype(o_ref.dtype)


def _attn(sinks, qt, k, vt, ag, bias):
    nb = SEQ // A_BLOCK
    cur = lambda b, n: (b * nb + n, 0)
    prev = lambda b, n: (b * nb + jnp.maximum(n - 1, 0), 0)
    cur_t = lambda b, n: (0, b * nb + n)
    prev_t = lambda b, n: (0, b * nb + jnp.maximum(n - 1, 0))
    return pl.pallas_call(
        _attn_kernel,
        out_shape=jax.ShapeDtypeStruct((TOKENS, A_WIDTH), BF16),
        grid=(BATCH, nb),
        in_specs=[pl.BlockSpec(memory_space=pltpu.SMEM),
                  pl.BlockSpec((A_WIDTH, A_BLOCK), cur_t),
                  pl.BlockSpec((A_BLOCK, A_KV_WIDTH), prev),
                  pl.BlockSpec((A_BLOCK, A_KV_WIDTH), cur),
                  pl.BlockSpec((A_KV_WIDTH, A_BLOCK), prev_t),
                  pl.BlockSpec((A_KV_WIDTH, A_BLOCK), cur_t),
                  pl.BlockSpec((A_BLOCK, A_WIDTH), cur),
                  pl.BlockSpec((A_Q_HEADS, A_BLOCK, A_BLOCK), lambda b, n: (0, 0, 0))],
        out_specs=pl.BlockSpec((A_BLOCK, A_WIDTH), cur),
        compiler_params=_cparams(("arbitrary", "arbitrary")),
        name="attn",
    )(sinks, qt, k, k, vt, vt, ag, bias)


def _gla_kernel(la_ref, q_ref, k_ref, v_ref, g_ref, gn_ref, o_ref, st_ref):
    cb = GLA_BLOCK

    @pl.when(pl.program_id(0) == 0)
    def _():
        st_ref[...] = jnp.zeros_like(st_ref)

    r = lax.broadcasted_iota(jnp.int32, (cb, cb), 0)
    c = lax.broadcasted_iota(jnp.int32, (cb, cb), 1)
    tri = (c <= r).astype(F32)
    lane = lax.broadcasted_iota(jnp.int32, (cb, B_QK_WIDTH), 1)
    head_masks = [(lane >= B_DK * h) & (lane < B_DK * (h + 1)) for h in range(B_HEADS)]
    rr = lax.broadcasted_iota(jnp.int32, (B_HEADS * cb, cb), 0)
    cc = lax.broadcasted_iota(jnp.int32, (B_HEADS * cb, cb), 1)
    causal = cc <= (rr & (cb - 1))
    nt = (((1,), (1,)), ((), ()))
    tn = (((0,), (0,)), ((), ()))
    seqs = range(BATCH)
    bcs = [jnp.dot(tri, la_ref[b], preferred_element_type=F32, precision=HIGHEST) for b in seqs]
    qsts, kss, ksts, decs = [], [], [], []
    for b in seqs:
        bc = bcs[b]
        bl = bc[cb - 1:cb, :]
        q = q_ref[b].astype(F32) * (B_DK ** -0.5)
        k = k_ref[b].astype(F32)
        qs = q * jnp.exp(bc)
        kh = k * jnp.exp(bl - bc)
        kss.append((k * jnp.exp(-bc)).astype(BF16))
        decs.append(jnp.exp(bl))
        qsts.append(jnp.concatenate([jnp.where(m, qs, 0.0) for m in head_masks], axis=0).astype(BF16))
        ksts.append(jnp.concatenate([jnp.where(m, kh, 0.0) for m in head_masks], axis=0).astype(BF16))
    sts = [st_ref[b] for b in seqs]
    vs = [v_ref[b] for b in seqs]
    a_alls = [lax.dot_general(qsts[b], kss[b], nt, preferred_element_type=F32) for b in seqs]
    oi_alls = [lax.dot_general(qsts[b], sts[b].astype(BF16), nt, preferred_element_type=F32) for b in seqs]
    for b in seqs:
        vst = jnp.concatenate([vs[b][:, B_DV * h:B_DV * (h + 1)] for h in range(B_HEADS)], axis=0)
        upd = lax.dot_general(vst, ksts[b], tn, preferred_element_type=F32)
        st_ref[b] = sts[b] * decs[b] + upd
    o_hs = {}
    for b in seqs:
        a_all = jnp.where(causal, a_alls[b], 0.0).astype(BF16)
        for h in range(B_HEADS):
            o_hs[b, h] = (jnp.dot(a_all[cb * h:cb * (h + 1)], vs[b][:, B_DV * h:B_DV * (h + 1)],
                                  preferred_element_type=F32) + oi_alls[b][cb * h:cb * (h + 1)])
    for b in seqs:
        for h in range(B_HEADS):
            vsl = slice(B_DV * h, B_DV * (h + 1))
            o_h = o_hs[b, h]
            ms = jnp.mean(o_h * o_h, axis=-1, keepdims=True)
            o_n = o_h * lax.rsqrt(ms + EPS) * gn_ref[:, vsl]
            gate = g_ref[b, :, vsl].astype(F32)
            o_ref[b, :, vsl] = (o_n * _silu(gate)).astype(o_ref.dtype)


def _gla(log_a, bq, bk, bv, bg, g_gla):
    cb = GLA_BLOCK
    blk = lambda w: pl.BlockSpec((BATCH, cb, w), lambda i: (0, i, 0))
    r3 = lambda a: a.reshape(BATCH, SEQ, a.shape[-1])
    out = pl.pallas_call(
        _gla_kernel,
        out_shape=jax.ShapeDtypeStruct((BATCH, SEQ, B_WIDTH), BF16),
        grid=(SEQ // cb,),
        in_specs=[blk(B_QK_WIDTH), blk(B_QK_WIDTH), blk(B_QK_WIDTH), blk(B_WIDTH), blk(B_WIDTH),
                  pl.BlockSpec((1, B_WIDTH), lambda i: (0, 0))],
        out_specs=blk(B_WIDTH),
        scratch_shapes=[pltpu.VMEM((BATCH, B_DV, B_QK_WIDTH), F32)],
        compiler_params=_cparams(("arbitrary",)),
        name="gla",
    )(r3(log_a), r3(bq), r3(bk), r3(bv), r3(bg), g_gla.reshape(1, B_WIDTH))
    return out.reshape(TOKENS, B_WIDTH)


def _s5prep_kernel(ar_ref, ai_ref, ldt_ref, bre_ref, bim_ref, btre_ref, btim_ref, cre_ref, cim_ref,
                   kk_ref, were_ref, weim_ref, wyre_ref, wyim_ref, pre_ref, pim_ref):
    ar = ar_ref[0]
    ai = ai_ref[0]
    dt = jnp.exp(ldt_ref[0])

    def cmul(xr, xi, yr, yi):
        return xr * yr - xi * yi, xr * yi + xi * yr

    kf = lax.broadcasted_iota(jnp.int32, (S5_POW_ROWS, 1), 0).astype(F32)
    mag = jnp.exp(kf * (ar * dt))
    ang = kf * (ai * dt)
    pw_re, pw_im = mag * jnp.cos(ang), mag * jnp.sin(ang)
    abar_re, abar_im = pw_re[1:2], pw_im[1:2]
    den = ar * ar + ai * ai
    num_re = abar_re - 1.0
    f_re = (num_re * ar + abar_im * ai) / den
    f_im = (abar_im * ar - num_re * ai) / den
    g_re, g_im = cmul(pw_re, pw_im, f_re, f_im)

    pos = lax.broadcasted_iota(jnp.int32, (S5_TC, S5_POW_ROWS), 0) // C_GROUP_CH
    kcol = lax.broadcasted_iota(jnp.int32, (S5_TC, S5_POW_ROWS), 1)

    def pick(which, xr, xi):
        sel = (kcol == which).astype(F32)
        return (jnp.dot(sel, xr, preferred_element_type=F32, precision=HIGHEST),
                jnp.dot(sel, xi, preferred_element_type=F32, precision=HIGHEST))

    tile16 = lambda a: jnp.concatenate([a] * S5_CHUNK, axis=0)
    ct_re, ct_im = tile16(cre_ref[0]), tile16(cim_ref[0])
    bt_re, bt_im = tile16(btre_ref[0]), tile16(btim_ref[0])

    w_re, w_im = cmul(*pick(pos, g_re, g_im), ct_re, ct_im)
    kk_ref[0] = (jnp.dot(w_re, bre_ref[0], preferred_element_type=F32, precision=HIGHEST)
                 - jnp.dot(w_im, bim_ref[0], preferred_element_type=F32, precision=HIGHEST))
    e_re, e_im = cmul(*pick(S5_CHUNK - 1 - pos, g_re, g_im), bt_re, bt_im)
    were_ref[0] = e_re
    weim_ref[0] = e_im
    y_re, y_im = cmul(*pick(pos + 1, pw_re, pw_im), ct_re, ct_im)
    wyre_ref[0] = y_re
    wyim_ref[0] = y_im
    pre_ref[0] = pw_re[S5_CHUNK:S5_CHUNK + 1]
    pim_ref[0] = pw_im[S5_CHUNK:S5_CHUNK + 1]


def _s5prep(a_re, a_im, log_dt, b_re, b_im, c_re, c_im):
    g, p, ch = C_GROUPS, C_STATE, C_GROUP_CH
    row = lambda a: a.reshape(g, 1, p)
    ldt = jnp.broadcast_to(log_dt[:, None, None], (g, 1, p))
    b_t = lambda a: jnp.swapaxes(a, 1, 2)
    spec = lambda s1, s2: pl.BlockSpec((1, s1, s2), lambda i: (i, 0, 0))
    out_shape = [jax.ShapeDtypeStruct((g, S5_TC, ch), F32)] + [jax.ShapeDtypeStruct((g, S5_TC, p), F32)] * 4 \
        + [jax.ShapeDtypeStruct((g, 1, p), F32)] * 2
    out_specs = [spec(S5_TC, ch)] + [spec(S5_TC, p)] * 4 + [spec(1, p)] * 2
    return pl.pallas_call(
        _s5prep_kernel,
        out_shape=out_shape,
        grid=(g,),
        in_specs=[spec(1, p), spec(1, p), spec(1, p), spec(p, ch), spec(p, ch),
                  spec(ch, p), spec(ch, p), spec(ch, p), spec(ch, p)],
        out_specs=out_specs,
        compiler_params=_cparams(("arbitrary",)),
        name="s5prep",
    )(row(a_re), row(a_im), ldt, b_re, b_im, b_t(b_re), b_t(b_im), c_re, c_im)


def _s5_assemble(kk, we_re, we_im, wyt_re, wyt_im, p_re, p_im, d):
    g, t, ch, p = C_GROUPS, S5_CHUNK, C_GROUP_CH, C_STATE
    npair = g // 2
    k4 = kk.reshape(g, t, ch, ch)
    z = jnp.concatenate([jnp.zeros_like(k4), k4], axis=1)
    idx = jnp.arange(t)[None, :] - jnp.arange(t)[:, None] + t
    mt = z[:, idx]
    mt = mt.transpose(0, 2, 3, 1, 4).reshape(S5_NGB, S5_GB, S5_TC, S5_TC).astype(BF16)
    pr = lambda a: a.reshape(npair, 2, *a.shape[1:])
    a, b = pr(jnp.swapaxes(we_re, 1, 2)), pr(jnp.swapaxes(we_im, 1, 2))
    zero = jnp.zeros((npair, p, S5_TC), F32)
    wet = jnp.concatenate([
        jnp.concatenate([a[:, 0], zero], axis=-1),
        jnp.concatenate([zero, a[:, 1]], axis=-1),
        jnp.concatenate([b[:, 0], zero], axis=-1),
        jnp.concatenate([zero, b[:, 1]], axis=-1)], axis=1)
    wet = wet.reshape(S5_NGB, S5_PAIRS_PER_GB, 4 * p, 2 * S5_TC).astype(BF16)
    a, b = pr(wyt_re), pr(-wyt_im)
    zero = jnp.zeros((npair, S5_TC, p), F32)
    wyt = jnp.concatenate([
        jnp.concatenate([a[:, 0], zero, b[:, 0], zero], axis=-1),
        jnp.concatenate([zero, a[:, 1], zero, b[:, 1]], axis=-1)], axis=1)
    wyt = wyt.reshape(S5_NGB, S5_PAIRS_PER_GB, 2 * S5_TC, 4 * p).astype(BF16)
    pw_re = p_re.reshape(S5_NGB, 1, S5_GB * p)
    pw_im = p_im.reshape(S5_NGB, 1, S5_GB * p)
    return mt, wet, wyt, pw_re, pw_im, d.reshape(1, C_WIDTH)


def _s5_kernel(u_ref, mt_ref, wet_ref, wyt_ref, are_ref, aim_ref, d_ref, y_ref,
               ut_ref, yt_ref, ere_ref, eim_ref, hre_ref, him_ref):
    nck, t_len, ch = S5_NCHUNK, S5_CHUNK, C_GROUP_CH
    nt = (((1,), (1,)), ((), ()))
    for t in range(t_len):
        xt = u_ref[pl.ds(t, nck, stride=t_len), :].T
        for g in range(S5_GB):
            ut_ref[g, ch * t:ch * (t + 1), :] = xt[ch * g:ch * (g + 1), :].astype(BF16)
    for j in range(S5_PAIRS_PER_GB):
        u0 = ut_ref[2 * j]
        u1 = ut_ref[2 * j + 1]
        et = jnp.dot(wet_ref[0, j], jnp.concatenate([u0, u1], axis=0), preferred_element_type=F32)
        e = et.T
        ere_ref[:, LANES * j:LANES * (j + 1)] = e[:, :LANES]
        eim_ref[:, LANES * j:LANES * (j + 1)] = e[:, LANES:]
        yt_ref[2 * j] = jnp.dot(mt_ref[0, 2 * j], u0, preferred_element_type=F32)
        yt_ref[2 * j + 1] = jnp.dot(mt_ref[0, 2 * j + 1], u1, preferred_element_type=F32)

    a_re = are_ref[0]
    a_im = aim_ref[0]

    def body(i, carry):
        h_re, h_im = carry
        hre_ref[pl.ds(i, 1), :] = h_re
        him_ref[pl.ds(i, 1), :] = h_im
        e_re = ere_ref[pl.ds(i, 1), :]
        e_im = eim_ref[pl.ds(i, 1), :]
        return a_re * h_re - a_im * h_im + e_re, a_re * h_im + a_im * h_re + e_im

    zero = jnp.zeros((1, S5_GB * C_STATE), F32)
    lax.fori_loop(0, nck, body, (zero, zero))

    for j in range(S5_PAIRS_PER_GB):
        sl = slice(LANES * j, LANES * (j + 1))
        hp = jnp.concatenate([hre_ref[:, sl], him_ref[:, sl]], axis=1).astype(BF16)
        yi = lax.dot_general(wyt_ref[0, j], hp, nt, preferred_element_type=F32)
        yt_ref[2 * j] += yi[:S5_TC]
        yt_ref[2 * j + 1] += yi[S5_TC:]
    for t in range(t_len):
        ytt = jnp.concatenate([yt_ref[g, ch * t:ch * (t + 1), :] for g in range(S5_GB)], axis=0)
        rows = pl.ds(t, nck, stride=t_len)
        y_ref[rows, :] = ytt.T + d_ref[...] * u_ref[rows, :]


def _s5(cu, mt, wet, wyt, pw_re, pw_im, d_row):
    p4 = 4 * C_STATE
    tok = pl.BlockSpec((SEQ, LANES), lambda gb, b: (b, gb))
    per_gb = lambda *s: pl.BlockSpec((1,) + s, lambda gb, b: (gb,) + (0,) * len(s))
    state = pltpu.VMEM((S5_NCHUNK, S5_GB * C_STATE), F32)
    return pl.pallas_call(
        _s5_kernel,
        out_shape=jax.ShapeDtypeStruct((TOKENS, C_WIDTH), F32),
        grid=(S5_NGB, BATCH),
        in_specs=[tok,
                  per_gb(S5_GB, S5_TC, S5_TC),
                  per_gb(S5_PAIRS_PER_GB, p4, 2 * S5_TC),
                  per_gb(S5_PAIRS_PER_GB, 2 * S5_TC, p4),
                  per_gb(1, S5_GB * C_STATE), per_gb(1, S5_GB * C_STATE),
                  pl.BlockSpec((1, LANES), lambda gb, b: (0, gb))],
        out_specs=tok,
        scratch_shapes=[pltpu.VMEM((S5_GB, S5_TC, S5_NCHUNK), BF16),
                        pltpu.VMEM((S5_GB, S5_TC, S5_NCHUNK), F32),
                        state, state, state, state],
        compiler_params=_cparams(("arbitrary", "arbitrary")),
        name="s5",
    )(cu, mt, wet, wyt, pw_re, pw_im, d_row)


def _gelu_tanh(x):
    return 0.5 * x * (1.0 + jnp.tanh(math.sqrt(2.0 / math.pi) * (x + 0.044715 * (x * x * x))))


def _out_kernel(oa_ref, ob_ref, yc_ref, cg_ref, x_ref, gate_ref, gpost_ref, wglu_ref, bglu_ref, wout_ref, o_ref):
    y = _gelu_tanh(yc_ref[...])
    z = jnp.dot(y.astype(BF16), wglu_ref[...], preferred_element_type=F32) + bglu_ref[...]
    y = y * jax.nn.sigmoid(z)
    oc = (y * _silu(cg_ref[...].astype(F32))).astype(BF16)
    acc = jnp.dot(oa_ref[...], wout_ref[0:A_WIDTH, :], preferred_element_type=F32)
    acc = acc + jnp.dot(ob_ref[...], wout_ref[A_WIDTH:A_WIDTH + B_WIDTH, :], preferred_element_type=F32)
    acc = acc + jnp.dot(oc, wout_ref[A_WIDTH + B_WIDTH:, :], preferred_element_type=F32)
    ms = jnp.mean(acc * acc, axis=-1, keepdims=True)
    out = acc * lax.rsqrt(ms + EPS) * gpost_ref[...]
    o_ref[...] = x_ref[...] + gate_ref[0] * out


def _out(oa, ob, yc, cg, x2, gate, g_post, w_glu, b_glu, w_out):
    tm = OUT_TM
    steps_per_batch = SEQ // tm
    row = lambda i: (i, 0)
    const = lambda i: (0, 0)
    return pl.pallas_call(
        _out_kernel,
        out_shape=jax.ShapeDtypeStruct((TOKENS, D_MODEL), F32),
        grid=(TOKENS // tm,),
        in_specs=[pl.BlockSpec((tm, A_WIDTH), row),
                  pl.BlockSpec((tm, B_WIDTH), row),
                  pl.BlockSpec((tm, C_WIDTH), row),
                  pl.BlockSpec((tm, C_WIDTH), row),
                  pl.BlockSpec((tm, D_MODEL), row),
                  pl.BlockSpec((1, 1, D_MODEL), lambda i: (i // steps_per_batch, 0, 0)),
                  pl.BlockSpec((1, D_MODEL), const),
                  pl.BlockSpec((C_WIDTH, C_WIDTH), const),
                  pl.BlockSpec((1, C_WIDTH), const),
                  pl.BlockSpec((2 * D_MODEL, D_MODEL), const)],
        out_specs=pl.BlockSpec((tm, D_MODEL), row),
        compiler_params=_cparams(("arbitrary",)),
        name="out",
    )(oa, ob, yc, cg, x2, gate, g_post.reshape(1, D_MODEL), w_glu, b_glu.reshape(1, C_WIDTH), w_out)


def _split_w_in(w_in):
    sizes = (A_WIDTH, A_KV_WIDTH, A_KV_WIDTH, A_WIDTH, B_QK_WIDTH, B_QK_WIDTH, B_WIDTH, B_GATE_RANK,
             B_WIDTH, C_WIDTH, C_WIDTH)
    pieces = []
    off = 0
    for s in sizes:
        pieces.append(w_in[:, off:off + s])
        off += s
    a_q, a_k, a_v, a_g, b_q, b_k, b_v, b_lr, b_g, c_u, c_g = pieces
    b_lr = jnp.pad(b_lr, ((0, 0), (0, LR_PAD - B_GATE_RANK)))
    cols = [a_k, a_g, b_q, b_k, b_v, b_lr, b_g, c_u, c_g]
    w_perm = jnp.concatenate([c.astype(BF16) for c in cols], axis=1)
    w_t = jnp.concatenate([a_q * (A_HEAD_DIM ** -0.5), a_v], axis=1).astype(BF16).T
    return w_perm, w_t


def _layer(x2, c_pad, bias, w_mod, b_mod, g_pre, g_post, w_in, sinks, w_alpha, b_alpha, g_gla,
           a_re, a_im, log_dt, b_re, b_im, c_re, c_im, d, w_glu, b_glu, w_out):
    mod = _mod(c_pad, w_mod, b_mod)[:BATCH]
    shift, scale, gate = (m.reshape(BATCH, 1, D_MODEL) for m in jnp.split(mod, 3, axis=-1))
    w_alpha_pad = jnp.pad(w_alpha, ((0, LR_PAD - B_GATE_RANK), (0, 0))).astype(BF16)
    w_perm, w_t = _split_w_in(w_in)
    ak, ag, bq, bk, bv, bg, cu, cg, log_a, aqt, avt = _proj(
        x2, scale, shift, g_pre, w_perm, w_t, w_alpha_pad, b_alpha)
    o_a = _attn(sinks, aqt, ak, avt, ag, bias)
    o_b = _gla(log_a, bq, bk, bv, bg, g_gla)
    s5_ops = _s5_assemble(*_s5prep(a_re, a_im, log_dt, b_re, b_im, c_re, c_im), d)
    y_c = _s5(cu, *s5_ops)
    return _out(o_a, o_b, y_c, cg, x2, gate, g_post, w_glu.astype(BF16), b_glu, w_out.astype(BF16))


def kernel(x, c, w_mod, b_mod, g_pre, g_post, w_in, attn_sinks, gla_w_alpha, gla_b_alpha, gla_norm_g,
           s5_a_re, s5_a_im, s5_log_dt, s5_b_re, s5_b_im, s5_c_re, s5_c_im, s5_d, s5_w_glu, s5_b_glu, w_out):
    x2 = x.reshape(TOKENS, D_MODEL)
    c_pad = jnp.pad(c, ((0, 8 - BATCH), (0, 0)))
    bias = _attn_bias()
    for l in range(w_mod.shape[0]):
        x2 = _layer(x2, c_pad, bias, w_mod[l], b_mod[l], g_pre[l], g_post[l], w_in[l], attn_sinks[l],
                    gla_w_alpha[l], gla_b_alpha[l], gla_norm_g[l], s5_a_re[l], s5_a_im[l], s5_log_dt[l],
                    s5_b_re[l], s5_b_im[l], s5_c_re[l], s5_c_im[l], s5_d[l], s5_w_glu[l], s5_b_glu[l],
                    w_out[l])
    return x2.reshape(x.shape)
```

```python
import math

import jax
import jax.numpy as jnp
import numpy as np
from jax import lax
from jax.experimental import pallas as pl
from jax.experimental.pallas import tpu as pltpu

F32 = jnp.float32
BF16 = jnp.bfloat16
HIGHEST = lax.Precision.HIGHEST

D_MODEL = 1024
BATCH = 4
SEQ = 4096
TOKENS = BATCH * SEQ
EPS = 1e-6

A_WIDTH = 1024
A_HEAD_DIM = 64
A_Q_HEADS = 16
A_KV_HEADS = 4
A_KV_WIDTH = A_KV_HEADS * A_HEAD_DIM
A_BLOCK = 128
WINDOW = 128

B_WIDTH = 512
B_HEADS = 4
B_DK = 64
B_DV = 128
B_QK_WIDTH = 256
B_GATE_RANK = 16
B_GATE_TAU = 16.0
GLA_BLOCK = 64
C_WIDTH = 512
C_GROUP_CH = 16
C_GROUPS = 32
C_STATE = 64
S5_CHUNK = 16
S5_NCHUNK = SEQ // S5_CHUNK
S5_TC = S5_CHUNK * C_GROUP_CH
S5_GB = 8
S5_NGB = C_GROUPS // S5_GB
S5_PAIRS_PER_GB = S5_GB // 2
S5_POW_ROWS = 24

LANES = 128
LR_PAD = LANES

V7X_VMEM_LIMIT = 56 * 1024 * 1024

PROJ_TM = 512
OUT_TM = 512

_PROJ_LAYOUT = {}
_off = 0
for _name, _w in (("ak", A_KV_WIDTH), ("ag", A_WIDTH),
                  ("bq", B_QK_WIDTH), ("bk", B_QK_WIDTH), ("bv", B_WIDTH), ("blr", LR_PAD),
                  ("bg", B_WIDTH), ("cu", C_WIDTH), ("cg", C_WIDTH)):
    _PROJ_LAYOUT[_name] = (_off, _w)
    _off += _w
PROJ_COLS_PADDED = _off
_PROJ_OUTS = (("ak", BF16), ("ag", BF16), ("bq", BF16), ("bk", BF16),
              ("bv", BF16), ("bg", BF16), ("cu", F32), ("cg", BF16))
PROJ_T_ROWS = A_WIDTH + A_KV_WIDTH


def _silu(x):
    return x * jax.nn.sigmoid(x)


def _cparams(semantics):
    return pltpu.CompilerParams(dimension_semantics=semantics, vmem_limit_bytes=V7X_VMEM_LIMIT)


def _mod_kernel(c_ref, w_ref, b_ref, o_ref):
    c = c_ref[...]
    o_ref[0] = jnp.dot(_silu(c), w_ref[0], preferred_element_type=F32, precision=HIGHEST) + b_ref[0]


def _mod(c_pad, w_mod, b_mod):
    layers = w_mod.shape[0]
    n = 3 * D_MODEL
    tn = 768
    return pl.pallas_call(
        _mod_kernel,
        out_shape=jax.ShapeDtypeStruct((layers, 8, n), F32),
        grid=(layers, n // tn),
        in_specs=[pl.BlockSpec((8, D_MODEL), lambda l, j: (0, 0)),
                  pl.BlockSpec((1, D_MODEL, tn), lambda l, j: (l, 0, j)),
                  pl.BlockSpec((1, 1, tn), lambda l, j: (l, 0, j))],
        out_specs=pl.BlockSpec((1, 8, tn), lambda l, j: (l, 0, j)),
        compiler_params=_cparams(("arbitrary", "arbitrary")),
        name="mod",
    )(c_pad, w_mod, b_mod.reshape(layers, 1, n))


def _proj_kernel(x_ref, scale_ref, shift_ref, gpre_ref, w_ref, wt_ref, walpha_ref, balpha_ref, *out_refs):
    x = x_ref[...]
    ms = jnp.mean(x * x, axis=-1, keepdims=True)
    y = x * lax.rsqrt(ms + EPS) * gpre_ref[...]
    h = (y * (1.0 + scale_ref[0]) + shift_ref[0]).astype(BF16)
    for (name, _), o_ref in zip(_PROJ_OUTS, out_refs):
        off, width = _PROJ_LAYOUT[name]
        o_ref[...] = jnp.dot(h, w_ref[:, off:off + width], preferred_element_type=F32).astype(o_ref.dtype)
    off, width = _PROJ_LAYOUT["blr"]
    lr = jnp.dot(h, w_ref[:, off:off + width], preferred_element_type=F32).astype(BF16)
    logits = jnp.dot(lr, walpha_ref[...], preferred_element_type=F32) + balpha_ref[...]
    log_sig = jnp.minimum(logits, 0.0) - jnp.log(1.0 + jnp.exp(-jnp.abs(logits)))
    la_ref, qt_ref, vt_ref = out_refs[len(_PROJ_OUTS):]
    la_ref[...] = log_sig * (1.0 / B_GATE_TAU)
    nt = (((1,), (1,)), ((), ()))
    qt_ref[...] = lax.dot_general(wt_ref[0:A_WIDTH, :], h, nt, preferred_element_type=F32).astype(qt_ref.dtype)
    vt_ref[...] = lax.dot_general(wt_ref[A_WIDTH:, :], h, nt, preferred_element_type=F32).astype(vt_ref.dtype)


def _proj(x2, scale, shift, g_pre, w_perm, w_t, w_alpha_pad, b_alpha):
    tm = PROJ_TM
    steps_per_batch = SEQ // tm
    row = lambda i: (i, 0)
    col = lambda i: (0, i)
    per_batch = lambda i: (i // steps_per_batch, 0, 0)
    const = lambda i: (0, 0)
    out_shape = [jax.ShapeDtypeStruct((TOKENS, _PROJ_LAYOUT[n][1]), dt) for n, dt in _PROJ_OUTS]
    out_specs = [pl.BlockSpec((tm, _PROJ_LAYOUT[n][1]), row) for n, _ in _PROJ_OUTS]
    out_shape += [jax.ShapeDtypeStruct((TOKENS, B_QK_WIDTH), F32),
                  jax.ShapeDtypeStruct((A_WIDTH, TOKENS), BF16),
                  jax.ShapeDtypeStruct((A_KV_WIDTH, TOKENS), BF16)]
    out_specs += [pl.BlockSpec((tm, B_QK_WIDTH), row),
                  pl.BlockSpec((A_WIDTH, tm), col),
                  pl.BlockSpec((A_KV_WIDTH, tm), col)]
    return pl.pallas_call(
        _proj_kernel,
        out_shape=out_shape,
        grid=(TOKENS // tm,),
        in_specs=[pl.BlockSpec((tm, D_MODEL), row),
                  pl.BlockSpec((1, 1, D_MODEL), per_batch),
                  pl.BlockSpec((1, 1, D_MODEL), per_batch),
                  pl.BlockSpec((1, D_MODEL), const),
                  pl.BlockSpec((D_MODEL, PROJ_COLS_PADDED), const),
                  pl.BlockSpec((PROJ_T_ROWS, D_MODEL), const),
                  pl.BlockSpec((LR_PAD, B_QK_WIDTH), const),
                  pl.BlockSpec((1, B_QK_WIDTH), const)],
        out_specs=out_specs,
        compiler_params=_cparams(("arbitrary",)),
        name="proj",
    )(x2, scale, shift, g_pre.reshape(1, D_MODEL), w_perm, w_t, w_alpha_pad, b_alpha.reshape(1, B_QK_WIDTH))


def _attn_bias():
    j = np.arange(A_BLOCK)[:, None]
    i = np.arange(A_BLOCK)[None, :]
    dist = np.where(j > i, i + A_BLOCK - j, i - j).astype(np.float32)
    slopes = np.exp2(-8.0 * np.arange(1, A_Q_HEADS + 1, dtype=np.float32) / A_Q_HEADS).astype(np.float32)
    return jnp.asarray(-slopes[:, None, None] * dist[None])


def _attn_kernel(sink_ref, qt_ref, kp_ref, kc_ref, vtp_ref, vtc_ref, g_ref, bias_ref, o_ref):
    n = pl.program_id(1)
    pen = jnp.where(n == 0, -jnp.inf, 0.0).astype(F32)
    kj = lax.broadcasted_iota(jnp.int32, (A_BLOCK, A_BLOCK), 0)
    qi = lax.broadcasted_iota(jnp.int32, (A_BLOCK, A_BLOCK), 1)
    from_prev = kj > qi
    zero_rows = jnp.zeros((A_HEAD_DIM, A_BLOCK), BF16)
    group = A_Q_HEADS // A_KV_HEADS

    def scores(hd):
        kvh = hd // group
        sl = slice(LANES * (kvh // 2), LANES * (kvh // 2 + 1))
        qh = qt_ref[A_HEAD_DIM * hd:A_HEAD_DIM * (hd + 1), :]
        qsel = jnp.concatenate([qh, zero_rows] if kvh % 2 == 0 else [zero_rows, qh], axis=0)
        return (jnp.dot(kp_ref[:, sl], qsel, preferred_element_type=F32),
                jnp.dot(kc_ref[:, sl], qsel, preferred_element_type=F32))

    def attend(hd, s_prev, s_cur):
        kvh = hd // group
        vrows = slice(A_HEAD_DIM * kvh, A_HEAD_DIM * (kvh + 1))
        v_both = jnp.concatenate([vtp_ref[vrows, :], vtc_ref[vrows, :]], axis=1)
        s = jnp.where(from_prev, s_prev + pen, s_cur) + bias_ref[hd]
        sink = sink_ref[hd]
        m = jnp.maximum(jnp.max(s, axis=0, keepdims=True), sink)
        p = jnp.exp(s - m)
        den = jnp.sum(p, axis=0, keepdims=True) + jnp.exp(sink - m)
        p_both = jnp.concatenate([jnp.where(from_prev, p, 0.0), jnp.where(from_prev, 0.0, p)],
                                 axis=0).astype(BF16)
        return jnp.dot(v_both, p_both, preferred_element_type=F32) / den

    pending = [scores(hd) for hd in range(A_Q_HEADS)]
    outs = {}
    for hd in range(A_Q_HEADS):
        outs[hd] = attend(hd, *pending[hd])
        if hd % 2 == 1:
            qsl = slice(LANES * (hd // 2), LANES * (hd // 2 + 1))
            o_pair = jnp.concatenate([outs.pop(hd - 1), outs.pop(hd)], axis=0).T
            gate = g_ref[:, qsl].astype(F32)
            o_ref[:, qsl] = (o_pair * _silu(gate)).astype(o_ref.dtype)


def _attn(sinks, qt, k, vt, ag, bias):
    nb = SEQ // A_BLOCK
    cur = lambda b, n: (b * nb + n, 0)
    prev = lambda b, n: (b * nb + jnp.maximum(n - 1, 0), 0)
    cur_t = lambda b, n: (0, b * nb + n)
    prev_t = lambda b, n: (0, b * nb + jnp.maximum(n - 1, 0))
    return pl.pallas_call(
        _attn_kernel,
        out_shape=jax.ShapeDtypeStruct((TOKENS, A_WIDTH), BF16),
        grid=(BATCH, nb),
        in_specs=[pl.BlockSpec(memory_space=pltpu.SMEM),
                  pl.BlockSpec((A_WIDTH, A_BLOCK), cur_t),
                  pl.BlockSpec((A_BLOCK, A_KV_WIDTH), prev),
                  pl.BlockSpec((A_BLOCK, A_KV_WIDTH), cur),
                  pl.BlockSpec((A_KV_WIDTH, A_BLOCK), prev_t),
                  pl.BlockSpec((A_KV_WIDTH, A_BLOCK), cur_t),
                  pl.BlockSpec((A_BLOCK, A_WIDTH), cur),
                  pl.BlockSpec((A_Q_HEADS, A_BLOCK, A_BLOCK), lambda b, n: (0, 0, 0))],
        out_specs=pl.BlockSpec((A_BLOCK, A_WIDTH), cur),
        compiler_params=_cparams(("arbitrary", "arbitrary")),
        name="attn",
    )(sinks, qt, k, k, vt, vt, ag, bias)


def _gla_kernel(la_ref, q_ref, k_ref, v_ref, g_ref, gn_ref, o_ref, st_ref):
    cb = GLA_BLOCK

    @pl.when(pl.program_id(0) == 0)
    def _():
        st_ref[...] = jnp.zeros_like(st_ref)

    r = lax.broadcasted_iota(jnp.int32, (cb, cb), 0)
    c = lax.broadcasted_iota(jnp.int32, (cb, cb), 1)
    tri = (c <= r).astype(F32)
    lane = lax.broadcasted_iota(jnp.int32, (cb, B_QK_WIDTH), 1)
    head_masks = [(lane >= B_DK * h) & (lane < B_DK * (h + 1)) for h in range(B_HEADS)]
    rr = lax.broadcasted_iota(jnp.int32, (B_HEADS * cb, cb), 0)
    cc = lax.broadcasted_iota(jnp.int32, (B_HEADS * cb, cb), 1)
    causal = cc <= (rr & (cb - 1))
    nt = (((1,), (1,)), ((), ()))
    tn = (((0,), (0,)), ((), ()))
    seqs = range(BATCH)
    bcs = [jnp.dot(tri, la_ref[b], preferred_element_type=F32, precision=HIGHEST) for b in seqs]
    qsts, kss, ksts, decs = [], [], [], []
    for b in seqs:
        bc = bcs[b]
        bl = bc[cb - 1:cb, :]
        q = q_ref[b].astype(F32) * (B_DK ** -0.5)
        k = k_ref[b].astype(F32)
        qs = q * jnp.exp(bc)
        kh = k * jnp.exp(bl - bc)
        kss.append((k * jnp.exp(-bc)).astype(BF16))
        decs.append(jnp.exp(bl))
        qsts.append(jnp.concatenate([jnp.where(m, qs, 0.0) for m in head_masks], axis=0).astype(BF16))
        ksts.append(jnp.concatenate([jnp.where(m, kh, 0.0) for m in head_masks], axis=0).astype(BF16))
    sts = [st_ref[b] for b in seqs]
    vs = [v_ref[b] for b in seqs]
    a_alls = [lax.dot_general(qsts[b], kss[b], nt, preferred_element_type=F32) for b in seqs]
    oi_alls = [lax.dot_general(qsts[b], sts[b].astype(BF16), nt, preferred_element_type=F32) for b in seqs]
    for b in seqs:
        vst = jnp.concatenate([vs[b][:, B_DV * h:B_DV * (h + 1)] for h in range(B_HEADS)], axis=0)
        upd = lax.dot_general(vst, ksts[b], tn, preferred_element_type=F32)
        st_ref[b] = sts[b] * decs[b] + upd
    o_hs = {}
    for b in seqs:
        a_all = jnp.where(causal, a_alls[b], 0.0).astype(BF16)
        for h in range(B_HEADS):
            o_hs[b, h] = (jnp.dot(a_all[cb * h:cb * (h + 1)], vs[b][:, B_DV * h:B_DV * (h + 1)],
                                  preferred_element_type=F32) + oi_alls[b][cb * h:cb * (h + 1)])
    for b in seqs:
        for h in range(B_HEADS):
            vsl = slice(B_DV * h, B_DV * (h + 1))
            o_h = o_hs[b, h]
            ms = jnp.mean(o_h * o_h, axis=-1, keepdims=True)
            o_n = o_h * lax.rsqrt(ms + EPS) * gn_ref[:, vsl]
            gate = g_ref[b, :, vsl].astype(F32)
            o_ref[b, :, vsl] = (o_n * _silu(gate)).astype(o_ref.dtype)


def _gla(log_a, bq, bk, bv, bg, g_gla):
    cb = GLA_BLOCK
    blk = lambda w: pl.BlockSpec((BATCH, cb, w), lambda i: (0, i, 0))
    r3 = lambda a: a.reshape(BATCH, SEQ, a.shape[-1])
    out = pl.pallas_call(
        _gla_kernel,
        out_shape=jax.ShapeDtypeStruct((BATCH, SEQ, B_WIDTH), BF16),
        grid=(SEQ // cb,),
        in_specs=[blk(B_QK_WIDTH), blk(B_QK_WIDTH), blk(B_QK_WIDTH), blk(B_WIDTH), blk(B_WIDTH),
                  pl.BlockSpec((1, B_WIDTH), lambda i: (0, 0))],
        out_specs=blk(B_WIDTH),
        scratch_shapes=[pltpu.VMEM((BATCH, B_DV, B_QK_WIDTH), F32)],
        compiler_params=_cparams(("arbitrary",)),
        name="gla",
    )(r3(log_a), r3(bq), r3(bk), r3(bv), r3(bg), g_gla.reshape(1, B_WIDTH))
    return out.reshape(TOKENS, B_WIDTH)


def _s5prep_kernel(ar_ref, ai_ref, ldt_ref, bre_ref, bim_ref, btre_ref, btim_ref, cre_ref, cim_ref,
                   kk_ref, were_ref, weim_ref, wyre_ref, wyim_ref, pre_ref, pim_ref):
    ar = ar_ref[0]
    ai = ai_ref[0]
    dt = jnp.exp(ldt_ref[0])

    def cmul(xr, xi, yr, yi):
        return xr * yr - xi * yi, xr * yi + xi * yr

    kf = lax.broadcasted_iota(jnp.int32, (S5_POW_ROWS, 1), 0).astype(F32)
    mag = jnp.exp(kf * (ar * dt))
    ang = kf * (ai * dt)
    pw_re, pw_im = mag * jnp.cos(ang), mag * jnp.sin(ang)
    abar_re, abar_im = pw_re[1:2], pw_im[1:2]
    den = ar * ar + ai * ai
    num_re = abar_re - 1.0
    f_re = (num_re * ar + abar_im * ai) / den
    f_im = (abar_im * ar - num_re * ai) / den
    g_re, g_im = cmul(pw_re, pw_im, f_re, f_im)

    pos = lax.broadcasted_iota(jnp.int32, (S5_TC, S5_POW_ROWS), 0) // C_GROUP_CH
    kcol = lax.broadcasted_iota(jnp.int32, (S5_TC, S5_POW_ROWS), 1)

    def pick(which, xr, xi):
        sel = (kcol == which).astype(F32)
        return (jnp.dot(sel, xr, preferred_element_type=F32, precision=HIGHEST),
                jnp.dot(sel, xi, preferred_element_type=F32, precision=HIGHEST))

    tile16 = lambda a: jnp.concatenate([a] * S5_CHUNK, axis=0)
    ct_re, ct_im = tile16(cre_ref[0]), tile16(cim_ref[0])
    bt_re, bt_im = tile16(btre_ref[0]), tile16(btim_ref[0])

    w_re, w_im = cmul(*pick(pos, g_re, g_im), ct_re, ct_im)
    kk_ref[0] = (jnp.dot(w_re, bre_ref[0], preferred_element_type=F32, precision=HIGHEST)
                 - jnp.dot(w_im, bim_ref[0], preferred_element_type=F32, precision=HIGHEST))
    e_re, e_im = cmul(*pick(S5_CHUNK - 1 - pos, g_re, g_im), bt_re, bt_im)
    were_ref[0] = e_re
    weim_ref[0] = e_im
    y_re, y_im = cmul(*pick(pos + 1, pw_re, pw_im), ct_re, ct_im)
    wyre_ref[0] = y_re
    wyim_ref[0] = y_im
    pre_ref[0] = pw_re[S5_CHUNK:S5_CHUNK + 1]
    pim_ref[0] = pw_im[S5_CHUNK:S5_CHUNK + 1]


def _s5prep(a_re, a_im, log_dt, b_re, b_im, c_re, c_im):
    p, ch = C_STATE, C_GROUP_CH
    g = a_re.shape[0] * C_GROUPS
    flat = lambda a: a.reshape(g, *a.shape[2:])
    a_re, a_im, log_dt, b_re, b_im, c_re, c_im = map(flat, (a_re, a_im, log_dt, b_re, b_im, c_re, c_im))
    row = lambda a: a.reshape(g, 1, p)
    ldt = jnp.broadcast_to(log_dt[:, None, None], (g, 1, p))
    b_t = lambda a: jnp.swapaxes(a, 1, 2)
    spec = lambda s1, s2: pl.BlockSpec((1, s1, s2), lambda i: (i, 0, 0))
    out_shape = [jax.ShapeDtypeStruct((g, S5_TC, ch), F32)] + [jax.ShapeDtypeStruct((g, S5_TC, p), F32)] * 4 \
        + [jax.ShapeDtypeStruct((g, 1, p), F32)] * 2
    out_specs = [spec(S5_TC, ch)] + [spec(S5_TC, p)] * 4 + [spec(1, p)] * 2
    return pl.pallas_call(
        _s5prep_kernel,
        out_shape=out_shape,
        grid=(g,),
        in_specs=[spec(1, p), spec(1, p), spec(1, p), spec(p, ch), spec(p, ch),
                  spec(ch, p), spec(ch, p), spec(ch, p), spec(ch, p)],
        out_specs=out_specs,
        compiler_params=_cparams(("arbitrary",)),
        name="s5prep",
    )(row(a_re), row(a_im), ldt, b_re, b_im, b_t(b_re), b_t(b_im), c_re, c_im)


def _s5_assemble(kk, we_re, we_im, wyt_re, wyt_im, p_re, p_im, d):
    t, ch, p = S5_CHUNK, C_GROUP_CH, C_STATE
    g = kk.shape[0]
    layers = g // C_GROUPS
    npair = g // 2
    k4 = kk.reshape(g, t, ch, ch)
    z = jnp.concatenate([jnp.zeros_like(k4), k4], axis=1)
    idx = jnp.arange(t)[None, :] - jnp.arange(t)[:, None] + t
    mt = z[:, idx]
    mt = mt.transpose(0, 2, 3, 1, 4).reshape(layers, S5_NGB, S5_GB, S5_TC, S5_TC).astype(BF16)
    pr = lambda a: a.reshape(npair, 2, *a.shape[1:])
    a, b = pr(jnp.swapaxes(we_re, 1, 2)), pr(jnp.swapaxes(we_im, 1, 2))
    zero = jnp.zeros((npair, p, S5_TC), F32)
    wet = jnp.concatenate([
        jnp.concatenate([a[:, 0], zero], axis=-1),
        jnp.concatenate([zero, a[:, 1]], axis=-1),
        jnp.concatenate([b[:, 0], zero], axis=-1),
        jnp.concatenate([zero, b[:, 1]], axis=-1)], axis=1)
    wet = wet.reshape(layers, S5_NGB, S5_PAIRS_PER_GB, 4 * p, 2 * S5_TC).astype(BF16)
    a, b = pr(wyt_re), pr(-wyt_im)
    zero = jnp.zeros((npair, S5_TC, p), F32)
    wyt = jnp.concatenate([
        jnp.concatenate([a[:, 0], zero, b[:, 0], zero], axis=-1),
        jnp.concatenate([zero, a[:, 1], zero, b[:, 1]], axis=-1)], axis=1)
    wyt = wyt.reshape(layers, S5_NGB, S5_PAIRS_PER_GB, 2 * S5_TC, 4 * p).astype(BF16)
    pw_re = p_re.reshape(layers, S5_NGB, 1, S5_GB * p)
    pw_im = p_im.reshape(layers, S5_NGB, 1, S5_GB * p)
    return mt, wet, wyt, pw_re, pw_im, d.reshape(layers, 1, C_WIDTH)


def _s5_kernel(u_ref, mt_ref, wet_ref, wyt_ref, are_ref, aim_ref, d_ref, y_ref,
               ut_ref, yt_ref, ere_ref, eim_ref, hre_ref, him_ref):
    nck, t_len, ch = S5_NCHUNK, S5_CHUNK, C_GROUP_CH
    nt = (((1,), (1,)), ((), ()))
    for t in range(t_len):
        xt = u_ref[pl.ds(t, nck, stride=t_len), :].T
        for g in range(S5_GB):
            ut_ref[g, ch * t:ch * (t + 1), :] = xt[ch * g:ch * (g + 1), :]
    for j in range(S5_PAIRS_PER_GB):
        u0 = ut_ref[2 * j].astype(BF16)
        u1 = ut_ref[2 * j + 1].astype(BF16)
        et = jnp.dot(wet_ref[0, j], jnp.concatenate([u0, u1], axis=0), preferred_element_type=F32)
        e = et.T
        ere_ref[:, LANES * j:LANES * (j + 1)] = e[:, :LANES]
        eim_ref[:, LANES * j:LANES * (j + 1)] = e[:, LANES:]
        yt_ref[2 * j] = jnp.dot(mt_ref[0, 2 * j], u0, preferred_element_type=F32)
        yt_ref[2 * j + 1] = jnp.dot(mt_ref[0, 2 * j + 1], u1, preferred_element_type=F32)

    a_re = are_ref[0]
    a_im = aim_ref[0]

    def body(i, carry):
        h_re, h_im = carry
        hre_ref[pl.ds(i, 1), :] = h_re
        him_ref[pl.ds(i, 1), :] = h_im
        e_re = ere_ref[pl.ds(i, 1), :]
        e_im = eim_ref[pl.ds(i, 1), :]
        return a_re * h_re - a_im * h_im + e_re, a_re * h_im + a_im * h_re + e_im

    zero = jnp.zeros((1, S5_GB * C_STATE), F32)
    lax.fori_loop(0, nck, body, (zero, zero))

    for j in range(S5_PAIRS_PER_GB):
        sl = slice(LANES * j, LANES * (j + 1))
        hp = jnp.concatenate([hre_ref[:, sl], him_ref[:, sl]], axis=1).astype(BF16)
        yi = lax.dot_general(wyt_ref[0, j], hp, nt, preferred_element_type=F32)
        yt_ref[2 * j] += yi[:S5_TC]
        yt_ref[2 * j + 1] += yi[S5_TC:]
    for t in range(t_len):
        ytt = jnp.concatenate([yt_ref[g, ch * t:ch * (t + 1), :] for g in range(S5_GB)], axis=0)
        rows = pl.ds(t, nck, stride=t_len)
        y_ref[rows, :] = ytt.T + d_ref[...] * u_ref[rows, :]


def _s5(cu, mt, wet, wyt, pw_re, pw_im, d_row):
    p4 = 4 * C_STATE
    tok = pl.BlockSpec((SEQ, LANES), lambda gb, b: (b, gb))
    per_gb = lambda *s: pl.BlockSpec((1,) + s, lambda gb, b: (gb,) + (0,) * len(s))
    state = pltpu.VMEM((S5_NCHUNK, S5_GB * C_STATE), F32)
    return pl.pallas_call(
        _s5_kernel,
        out_shape=jax.ShapeDtypeStruct((TOKENS, C_WIDTH), F32),
        grid=(S5_NGB, BATCH),
        in_specs=[tok,
                  per_gb(S5_GB, S5_TC, S5_TC),
                  per_gb(S5_PAIRS_PER_GB, p4, 2 * S5_TC),
                  per_gb(S5_PAIRS_PER_GB, 2 * S5_TC, p4),
                  per_gb(1, S5_GB * C_STATE), per_gb(1, S5_GB * C_STATE),
                  pl.BlockSpec((1, LANES), lambda gb, b: (0, gb))],
        out_specs=tok,
        scratch_shapes=[pltpu.VMEM((S5_GB, S5_TC, S5_NCHUNK), F32),
                        pltpu.VMEM((S5_GB, S5_TC, S5_NCHUNK), F32),
                        state, state, state, state],
        compiler_params=_cparams(("arbitrary", "arbitrary")),
        name="s5",
    )(cu, mt, wet, wyt, pw_re, pw_im, d_row)


def _gelu_tanh(x):
    return 0.5 * x * (1.0 + jnp.tanh(math.sqrt(2.0 / math.pi) * (x + 0.044715 * (x * x * x))))


def _out_kernel(oa_ref, ob_ref, yc_ref, cg_ref, x_ref, gate_ref, gpost_ref, wglu_ref, bglu_ref, wout_ref, o_ref):
    y = _gelu_tanh(yc_ref[...])
    z = jnp.dot(y.astype(BF16), wglu_ref[...], preferred_element_type=F32) + bglu_ref[...]
    y = y * jax.nn.sigmoid(z)
    oc = (y * _silu(cg_ref[...].astype(F32))).astype(BF16)
    acc = jnp.dot(oa_ref[...], wout_ref[0:A_WIDTH, :], preferred_element_type=F32)
    acc = acc + jnp.dot(ob_ref[...], wout_ref[A_WIDTH:A_WIDTH + B_WIDTH, :], preferred_element_type=F32)
    acc = acc + jnp.dot(oc, wout_ref[A_WIDTH + B_WIDTH:, :], preferred_element_type=F32)
    ms = jnp.mean(acc * acc, axis=-1, keepdims=True)
    out = acc * lax.rsqrt(ms + EPS) * gpost_ref[...]
    o_ref[...] = x_ref[...] + gate_ref[0] * out


def _out(oa, ob, yc, cg, x2, gate, g_post, w_glu, b_glu, w_out):
    tm = OUT_TM
    steps_per_batch = SEQ // tm
    row = lambda i: (i, 0)
    const = lambda i: (0, 0)
    return pl.pallas_call(
        _out_kernel,
        out_shape=jax.ShapeDtypeStruct((TOKENS, D_MODEL), F32),
        grid=(TOKENS // tm,),
        in_specs=[pl.BlockSpec((tm, A_WIDTH), row),
                  pl.BlockSpec((tm, B_WIDTH), row),
                  pl.BlockSpec((tm, C_WIDTH), row),
                  pl.BlockSpec((tm, C_WIDTH), row),
                  pl.BlockSpec((tm, D_MODEL), row),
                  pl.BlockSpec((1, 1, D_MODEL), lambda i: (i // steps_per_batch, 0, 0)),
                  pl.BlockSpec((1, D_MODEL), const),
                  pl.BlockSpec((C_WIDTH, C_WIDTH), const),
                  pl.BlockSpec((1, C_WIDTH), const),
                  pl.BlockSpec((2 * D_MODEL, D_MODEL), const)],
        out_specs=pl.BlockSpec((tm, D_MODEL), row),
        compiler_params=_cparams(("arbitrary",)),
        name="out",
    )(oa, ob, yc, cg, x2, gate, g_post.reshape(1, D_MODEL), w_glu, b_glu.reshape(1, C_WIDTH), w_out)


_W_IN_SIZES = (("aq", A_WIDTH), ("ak", A_KV_WIDTH), ("av", A_KV_WIDTH), ("ag", A_WIDTH), ("bq", B_QK_WIDTH),
               ("bk", B_QK_WIDTH), ("bv", B_WIDTH), ("blr", B_GATE_RANK), ("bg", B_WIDTH), ("cu", C_WIDTH),
               ("cg", C_WIDTH))
_W_IN_OFF = {}
_off = 0
for _name, _w in _W_IN_SIZES:
    _W_IN_OFF[_name] = _off
    _off += _w
W_IN_COLS = _off
WPREP_ROWS = 128


def _wprep_kernel(w_ref, wp_ref, wt_ref):
    w = w_ref[0]
    for name, (dst, width) in _PROJ_LAYOUT.items():
        src = _W_IN_OFF[name]
        if name == "blr":
            piece = jnp.concatenate([w[:, src:src + B_GATE_RANK],
                                     jnp.zeros((WPREP_ROWS, LR_PAD - B_GATE_RANK), F32)], axis=1)
        else:
            piece = w[:, src:src + width]
        wp_ref[0, :, dst:dst + width] = piece.astype(BF16)
    qv = jnp.concatenate([w[:, _W_IN_OFF["aq"]:_W_IN_OFF["aq"] + A_WIDTH] * (A_HEAD_DIM ** -0.5),
                          w[:, _W_IN_OFF["av"]:_W_IN_OFF["av"] + A_KV_WIDTH]], axis=1)
    wt_ref[0] = qv.T.astype(BF16)


def _wprep(w_in):
    layers = w_in.shape[0]
    return pl.pallas_call(
        _wprep_kernel,
        out_shape=[jax.ShapeDtypeStruct((layers, D_MODEL, PROJ_COLS_PADDED), BF16),
                   jax.ShapeDtypeStruct((layers, PROJ_T_ROWS, D_MODEL), BF16)],
        grid=(layers, D_MODEL // WPREP_ROWS),
        in_specs=[pl.BlockSpec((1, WPREP_ROWS, W_IN_COLS), lambda l, i: (l, i, 0))],
        out_specs=[pl.BlockSpec((1, WPREP_ROWS, PROJ_COLS_PADDED), lambda l, i: (l, i, 0)),
                   pl.BlockSpec((1, PROJ_T_ROWS, WPREP_ROWS), lambda l, i: (l, 0, i))],
        compiler_params=_cparams(("arbitrary", "arbitrary")),
        name="wprep",
    )(w_in)


def kernel(x, c, w_mod, b_mod, g_pre, g_post, w_in, attn_sinks, gla_w_alpha, gla_b_alpha, gla_norm_g,
           s5_a_re, s5_a_im, s5_log_dt, s5_b_re, s5_b_im, s5_c_re, s5_c_im, s5_d, s5_w_glu, s5_b_glu, w_out):
    layers = w_mod.shape[0]
    x2 = x.reshape(TOKENS, D_MODEL)
    bias = _attn_bias()
    mod = _mod(jnp.pad(c, ((0, 8 - BATCH), (0, 0))), w_mod, b_mod)[:, :BATCH]
    shift, scale, gate = (m.reshape(layers, BATCH, 1, D_MODEL) for m in jnp.split(mod, 3, axis=-1))
    w_perm, w_t = _wprep(w_in)
    w_alpha_pad = jnp.pad(gla_w_alpha, ((0, 0), (0, LR_PAD - B_GATE_RANK), (0, 0))).astype(BF16)
    s5_ops = _s5_assemble(*_s5prep(s5_a_re, s5_a_im, s5_log_dt, s5_b_re, s5_b_im, s5_c_re, s5_c_im), s5_d)
    w_glu = s5_w_glu.astype(BF16)
    w_out_b = w_out.astype(BF16)
    for l in range(layers):
        ak, ag, bq, bk, bv, bg, cu, cg, log_a, aqt, avt = _proj(
            x2, scale[l], shift[l], g_pre[l], w_perm[l], w_t[l], w_alpha_pad[l], gla_b_alpha[l])
        o_a = _attn(attn_sinks[l], aqt, ak, avt, ag, bias)
        o_b = _gla(log_a, bq, bk, bv, bg, gla_norm_g[l])
        y_c = _s5(cu, *(op[l] for op in s5_ops))
        x2 = _out(o_a, o_b, y_c, cg, x2, gate[l], g_post[l], w_glu[l], s5_b_glu[l], w_out_b[l])
    return x2.reshape(x.shape)
```

```python
import math

import jax
import jax.numpy as jnp
import numpy as np
from jax import lax
from jax.experimental import pallas as pl
from jax.experimental.pallas import tpu as pltpu

F32 = jnp.float32
BF16 = jnp.bfloat16
HIGHEST = lax.Precision.HIGHEST

D_MODEL = 1024
BATCH = 4
SEQ = 4096
TOKENS = BATCH * SEQ
EPS = 1e-6

A_WIDTH = 1024
A_HEAD_DIM = 64
A_Q_HEADS = 16
A_KV_HEADS = 4
A_KV_WIDTH = A_KV_HEADS * A_HEAD_DIM
A_BLOCK = 128
WINDOW = 128

B_WIDTH = 512
B_HEADS = 4
B_DK = 64
B_DV = 128
B_QK_WIDTH = 256
B_GATE_RANK = 16
B_GATE_TAU = 16.0
GLA_BLOCK = 64
C_WIDTH = 512
C_GROUP_CH = 16
C_GROUPS = 32
C_STATE = 64
S5_CHUNK = 16
S5_NCHUNK = SEQ // S5_CHUNK
S5_TC = S5_CHUNK * C_GROUP_CH
S5_GB = 8
S5_NGB = C_GROUPS // S5_GB
S5_PAIRS_PER_GB = S5_GB // 2
S5_POW_ROWS = 24

LANES = 128
LR_PAD = LANES

V7X_VMEM_LIMIT = 56 * 1024 * 1024

PROJ_TM = 512
OUT_TM = 512

_PROJ_LAYOUT = {}
_off = 0
for _name, _w in (("ak", A_KV_WIDTH), ("ag", A_WIDTH),
                  ("bq", B_QK_WIDTH), ("bk", B_QK_WIDTH), ("bv", B_WIDTH), ("blr", LR_PAD),
                  ("bg", B_WIDTH), ("cu", C_WIDTH), ("cg", C_WIDTH)):
    _PROJ_LAYOUT[_name] = (_off, _w)
    _off += _w
PROJ_COLS_PADDED = _off
_PROJ_OUTS = (("ak", BF16), ("ag", BF16), ("bq", BF16), ("bk", BF16),
              ("bv", BF16), ("bg", BF16), ("cu", F32), ("cg", BF16))
PROJ_T_ROWS = A_WIDTH + A_KV_WIDTH


def _silu(x):
    return x * jax.nn.sigmoid(x)


def _cparams(semantics):
    return pltpu.CompilerParams(dimension_semantics=semantics, vmem_limit_bytes=V7X_VMEM_LIMIT)


def _mod_kernel(c_ref, w_ref, b_ref, o_ref):
    c = c_ref[...]
    o_ref[0] = jnp.dot(_silu(c), w_ref[0], preferred_element_type=F32, precision=HIGHEST) + b_ref[0]


def _mod(c_pad, w_mod, b_mod):
    layers = w_mod.shape[0]
    n = 3 * D_MODEL
    tn = 768
    return pl.pallas_call(
        _mod_kernel,
        out_shape=jax.ShapeDtypeStruct((layers, 8, n), F32),
        grid=(layers, n // tn),
        in_specs=[pl.BlockSpec((8, D_MODEL), lambda l, j: (0, 0)),
                  pl.BlockSpec((1, D_MODEL, tn), lambda l, j: (l, 0, j)),
                  pl.BlockSpec((1, 1, tn), lambda l, j: (l, 0, j))],
        out_specs=pl.BlockSpec((1, 8, tn), lambda l, j: (l, 0, j)),
        compiler_params=_cparams(("arbitrary", "arbitrary")),
        name="mod",
    )(c_pad, w_mod, b_mod.reshape(layers, 1, n))


def _proj_kernel(x_ref, scale_ref, shift_ref, gpre_ref, w_ref, wt_ref, walpha_ref, balpha_ref, *out_refs):
    x = x_ref[...]
    ms = jnp.mean(x * x, axis=-1, keepdims=True)
    y = x * lax.rsqrt(ms + EPS) * gpre_ref[...]
    h = (y * (1.0 + scale_ref[0]) + shift_ref[0]).astype(BF16)
    for (name, _), o_ref in zip(_PROJ_OUTS, out_refs):
        off, width = _PROJ_LAYOUT[name]
        o_ref[...] = jnp.dot(h, w_ref[:, off:off + width], preferred_element_type=F32).astype(o_ref.dtype)
    off, width = _PROJ_LAYOUT["blr"]
    lr = jnp.dot(h, w_ref[:, off:off + width], preferred_element_type=F32).astype(BF16)
    logits = jnp.dot(lr, walpha_ref[...], preferred_element_type=F32) + balpha_ref[...]
    log_sig = jnp.minimum(logits, 0.0) - jnp.log(1.0 + jnp.exp(-jnp.abs(logits)))
    la_ref, qt_ref, vt_ref = out_refs[len(_PROJ_OUTS):]
    la_ref[...] = log_sig * (1.0 / B_GATE_TAU)
    nt = (((1,), (1,)), ((), ()))
    qt_ref[...] = lax.dot_general(wt_ref[0:A_WIDTH, :], h, nt, preferred_element_type=F32).astype(qt_ref.dtype)
    vt_ref[...] = lax.dot_general(wt_ref[A_WIDTH:, :], h, nt, preferred_element_type=F32).astype(vt_ref.dtype)


def _proj(x2, scale, shift, g_pre, w_perm, w_t, w_alpha_pad, b_alpha):
    tm = PROJ_TM
    steps_per_batch = SEQ // tm
    row = lambda i: (i, 0)
    col = lambda i: (0, i)
    per_batch = lambda i: (i // steps_per_batch, 0, 0)
    const = lambda i: (0, 0)
    out_shape = [jax.ShapeDtypeStruct((TOKENS, _PROJ_LAYOUT[n][1]), dt) for n, dt in _PROJ_OUTS]
    out_specs = [pl.BlockSpec((tm, _PROJ_LAYOUT[n][1]), row) for n, _ in _PROJ_OUTS]
    out_shape += [jax.ShapeDtypeStruct((TOKENS, B_QK_WIDTH), F32),
                  jax.ShapeDtypeStruct((A_WIDTH, TOKENS), BF16),
                  jax.ShapeDtypeStruct((A_KV_WIDTH, TOKENS), BF16)]
    out_specs += [pl.BlockSpec((tm, B_QK_WIDTH), row),
                  pl.BlockSpec((A_WIDTH, tm), col),
                  pl.BlockSpec((A_KV_WIDTH, tm), col)]
    return pl.pallas_call(
        _proj_kernel,
        out_shape=out_shape,
        grid=(TOKENS // tm,),
        in_specs=[pl.BlockSpec((tm, D_MODEL), row),
                  pl.BlockSpec((1, 1, D_MODEL), per_batch),
                  pl.BlockSpec((1, 1, D_MODEL), per_batch),
                  pl.BlockSpec((1, D_MODEL), const),
                  pl.BlockSpec((D_MODEL, PROJ_COLS_PADDED), const),
                  pl.BlockSpec((PROJ_T_ROWS, D_MODEL), const),
                  pl.BlockSpec((LR_PAD, B_QK_WIDTH), const),
                  pl.BlockSpec((1, B_QK_WIDTH), const)],
        out_specs=out_specs,
        compiler_params=_cparams(("arbitrary",)),
        name="proj",
    )(x2, scale, shift, g_pre.reshape(1, D_MODEL), w_perm, w_t, w_alpha_pad, b_alpha.reshape(1, B_QK_WIDTH))


def _attn_bias():
    j = np.arange(A_BLOCK)[:, None]
    i = np.arange(A_BLOCK)[None, :]
    dist = np.where(j > i, i + A_BLOCK - j, i - j).astype(np.float32)
    slopes = np.exp2(-8.0 * np.arange(1, A_Q_HEADS + 1, dtype=np.float32) / A_Q_HEADS).astype(np.float32)
    return jnp.asarray(-slopes[:, None, None] * dist[None])


def _attn_kernel(sink_ref, qt_ref, kp_ref, kc_ref, vtp_ref, vtc_ref, g_ref, bias_ref, o_ref):
    n = pl.program_id(1)
    pen = jnp.where(n == 0, -jnp.inf, 0.0).astype(F32)
    kj = lax.broadcasted_iota(jnp.int32, (A_BLOCK, A_BLOCK), 0)
    qi = lax.broadcasted_iota(jnp.int32, (A_BLOCK, A_BLOCK), 1)
    from_prev = kj > qi
    zero_rows = jnp.zeros((A_HEAD_DIM, A_BLOCK), BF16)
    group = A_Q_HEADS // A_KV_HEADS

    def scores(hd):
        kvh = hd // group
        sl = slice(LANES * (kvh // 2), LANES * (kvh // 2 + 1))
        qh = qt_ref[A_HEAD_DIM * hd:A_HEAD_DIM * (hd + 1), :]
        qsel = jnp.concatenate([qh, zero_rows] if kvh % 2 == 0 else [zero_rows, qh], axis=0)
        return (jnp.dot(kp_ref[:, sl], qsel, preferred_element_type=F32),
                jnp.dot(kc_ref[:, sl], qsel, preferred_element_type=F32))

    def attend(hd, s_prev, s_cur):
        kvh = hd // group
        vrows = slice(A_HEAD_DIM * kvh, A_HEAD_DIM * (kvh + 1))
        v_both = jnp.concatenate([vtp_ref[vrows, :], vtc_ref[vrows, :]], axis=1)
        s = jnp.where(from_prev, s_prev + pen, s_cur) + bias_ref[hd]
        sink = sink_ref[hd]
        m = jnp.maximum(jnp.max(s, axis=0, keepdims=True), sink)
        p = jnp.exp(s - m)
        den = jnp.sum(p, axis=0, keepdims=True) + jnp.exp(sink - m)
        p_both = jnp.concatenate([jnp.where(from_prev, p, 0.0), jnp.where(from_prev, 0.0, p)],
                                 axis=0).astype(BF16)
        return jnp.dot(v_both, p_both, preferred_element_type=F32) / den

    pending = [scores(hd) for hd in range(A_Q_HEADS)]
    outs = {}
    for hd in range(A_Q_HEADS):
        outs[hd] = attend(hd, *pending[hd])
        if hd % 2 == 1:
            qsl = slice(LANES * (hd // 2), LANES * (hd // 2 + 1))
            o_pair = jnp.concatenate([outs.pop(hd - 1), outs.pop(hd)], axis=0).T
            gate = g_ref[:, qsl].astype(F32)
            o_ref[:, qsl] = (o_pair * _silu(gate)).astype(o_ref.dtype)


def _attn(sinks, qt, k, vt, ag, bias):
    nb = SEQ // A_BLOCK
    cur = lambda b, n: (b * nb + n, 0)
    prev = lambda b, n: (b * nb + jnp.maximum(n - 1, 0), 0)
    cur_t = lambda b, n: (0, b * nb + n)
    prev_t = lambda b, n: (0, b * nb + jnp.maximum(n - 1, 0))
    return pl.pallas_call(
        _attn_kernel,
        out_shape=jax.ShapeDtypeStruct((TOKENS, A_WIDTH), BF16),
        grid=(BATCH, nb),
        in_specs=[pl.BlockSpec(memory_space=pltpu.SMEM),
                  pl.BlockSpec((A_WIDTH, A_BLOCK), cur_t),
                  pl.BlockSpec((A_BLOCK, A_KV_WIDTH), prev),
                  pl.BlockSpec((A_BLOCK, A_KV_WIDTH), cur),
                  pl.BlockSpec((A_KV_WIDTH, A_BLOCK), prev_t),
                  pl.BlockSpec((A_KV_WIDTH, A_BLOCK), cur_t),
                  pl.BlockSpec((A_BLOCK, A_WIDTH), cur),
                  pl.BlockSpec((A_Q_HEADS, A_BLOCK, A_BLOCK), lambda b, n: (0, 0, 0))],
        out_specs=pl.BlockSpec((A_BLOCK, A_WIDTH), cur),
        compiler_params=_cparams(("arbitrary", "arbitrary")),
        name="attn",
    )(sinks, qt, k, k, vt, vt, ag, bias)


def _gla_kernel(la_ref, q_ref, k_ref, v_ref, g_ref, gn_ref, o_ref, st_ref):
    cb = GLA_BLOCK

    @pl.when(pl.program_id(0) == 0)
    def _():
        st_ref[...] = jnp.zeros_like(st_ref)

    r = lax.broadcasted_iota(jnp.int32, (cb, cb), 0)
    c = lax.broadcasted_iota(jnp.int32, (cb, cb), 1)
    tri = (c <= r).astype(F32)
    lane = lax.broadcasted_iota(jnp.int32, (cb, B_QK_WIDTH), 1)
    head_masks = [(lane >= B_DK * h) & (lane < B_DK * (h + 1)) for h in range(B_HEADS)]
    rr = lax.broadcasted_iota(jnp.int32, (B_HEADS * cb, cb), 0)
    cc = lax.broadcasted_iota(jnp.int32, (B_HEADS * cb, cb), 1)
    causal = cc <= (rr & (cb - 1))
    nt = (((1,), (1,)), ((), ()))
    tn = (((0,), (0,)), ((), ()))
    seqs = range(BATCH)
    bcs = [jnp.dot(tri, la_ref[b], preferred_element_type=F32, precision=HIGHEST) for b in seqs]
    qsts, kss, ksts, decs = [], [], [], []
    for b in seqs:
        bc = bcs[b]
        bl = bc[cb - 1:cb, :]
        q = q_ref[b].astype(F32) * (B_DK ** -0.5)
        k = k_ref[b].astype(F32)
        qs = q * jnp.exp(bc)
        kh = k * jnp.exp(bl - bc)
        kss.append((k * jnp.exp(-bc)).astype(BF16))
        decs.append(jnp.exp(bl))
        qsts.append(jnp.concatenate([jnp.where(m, qs, 0.0) for m in head_masks], axis=0).astype(BF16))
        ksts.append(jnp.concatenate([jnp.where(m, kh, 0.0) for m in head_masks], axis=0).astype(BF16))
    sts = [st_ref[b] for b in seqs]
    vs = [v_ref[b] for b in seqs]
    a_alls = [lax.dot_general(qsts[b], kss[b], nt, preferred_element_type=F32) for b in seqs]
    oi_alls = [lax.dot_general(qsts[b], sts[b].astype(BF16), nt, preferred_element_type=F32) for b in seqs]
    for b in seqs:
        vst = jnp.concatenate([vs[b][:, B_DV * h:B_DV * (h + 1)] for h in range(B_HEADS)], axis=0)
        upd = lax.dot_general(vst, ksts[b], tn, preferred_element_type=F32)
        st_ref[b] = sts[b] * decs[b] + upd
    o_hs = {}
    for b in seqs:
        a_all = jnp.where(causal, a_alls[b], 0.0).astype(BF16)
        for h in range(B_HEADS):
            o_hs[b, h] = (jnp.dot(a_all[cb * h:cb * (h + 1)], vs[b][:, B_DV * h:B_DV * (h + 1)],
                                  preferred_element_type=F32) + oi_alls[b][cb * h:cb * (h + 1)])
    for b in seqs:
        for h in range(B_HEADS):
            vsl = slice(B_DV * h, B_DV * (h + 1))
            o_h = o_hs[b, h]
            ms = jnp.mean(o_h * o_h, axis=-1, keepdims=True)
            o_n = o_h * lax.rsqrt(ms + EPS) * gn_ref[:, vsl]
            gate = g_ref[b, :, vsl].astype(F32)
            o_ref[b, :, vsl] = (o_n * _silu(gate)).astype(o_ref.dtype)


def _gla(log_a, bq, bk, bv, bg, g_gla):
    cb = GLA_BLOCK
    blk = lambda w: pl.BlockSpec((BATCH, cb, w), lambda i: (0, i, 0))
    r3 = lambda a: a.reshape(BATCH, SEQ, a.shape[-1])
    out = pl.pallas_call(
        _gla_kernel,
        out_shape=jax.ShapeDtypeStruct((BATCH, SEQ, B_WIDTH), BF16),
        grid=(SEQ // cb,),
        in_specs=[blk(B_QK_WIDTH), blk(B_QK_WIDTH), blk(B_QK_WIDTH), blk(B_WIDTH), blk(B_WIDTH),
                  pl.BlockSpec((1, B_WIDTH), lambda i: (0, 0))],
        out_specs=blk(B_WIDTH),
        scratch_shapes=[pltpu.VMEM((BATCH, B_DV, B_QK_WIDTH), F32)],
        compiler_params=_cparams(("arbitrary",)),
        name="gla",
    )(r3(log_a), r3(bq), r3(bk), r3(bv), r3(bg), g_gla.reshape(1, B_WIDTH))
    return out.reshape(TOKENS, B_WIDTH)


def _s5prep_kernel(ar_ref, ai_ref, ldt_ref, bre_ref, bim_ref, btre_ref, btim_ref, cre_ref, cim_ref,
                   mt_ref, wet_ref, wyt_ref, are_ref, aim_ref):
    p = C_STATE
    wet_re, wet_im, wyt_rows, a_re, a_im = [], [], [], [], []
    for g in range(2):
        kk, e_re, e_im, y_re, y_im, p_re, p_im = _s5_discretise(
            ar_ref[g], ai_ref[g], ldt_ref[g], bre_ref[g], bim_ref[g], btre_ref[g], btim_ref[g],
            cre_ref[g], cim_ref[g])
        pieces = [kk] + [jnp.concatenate([jnp.zeros((C_GROUP_CH * s, C_GROUP_CH), F32),
                                          kk[:S5_TC - C_GROUP_CH * s]], axis=0) for s in range(1, S5_CHUNK)]
        mt_ref[0, g] = jnp.concatenate(pieces, axis=1).astype(BF16)
        e_t = jnp.concatenate([e_re, e_im], axis=1).T
        zero = jnp.zeros((p, S5_TC), F32)
        wet_re.append(jnp.concatenate([e_t[:p], zero] if g == 0 else [zero, e_t[:p]], axis=1))
        wet_im.append(jnp.concatenate([e_t[p:], zero] if g == 0 else [zero, e_t[p:]], axis=1))
        zero = jnp.zeros((S5_TC, p), F32)
        wyt_rows.append(jnp.concatenate([y_re, zero, -y_im, zero] if g == 0 else [zero, y_re, zero, -y_im], axis=1))
        a_re.append(p_re)
        a_im.append(p_im)
    wet_ref[0] = jnp.concatenate(wet_re + wet_im, axis=0).astype(BF16)
    wyt_ref[0] = jnp.concatenate(wyt_rows, axis=0).astype(BF16)
    are_ref[0] = jnp.broadcast_to(jnp.concatenate(a_re, axis=1), (8, 2 * p))
    aim_ref[0] = jnp.broadcast_to(jnp.concatenate(a_im, axis=1), (8, 2 * p))


def _s5_discretise(ar, ai, ldt, b_re, b_im, bt_re16, bt_im16, c_re16, c_im16):
    dt = jnp.exp(ldt)

    def cmul(xr, xi, yr, yi):
        return xr * yr - xi * yi, xr * yi + xi * yr

    kf = lax.broadcasted_iota(jnp.int32, (S5_POW_ROWS, 1), 0).astype(F32)
    mag = jnp.exp(kf * (ar * dt))
    ang = kf * (ai * dt)
    pw_re, pw_im = mag * jnp.cos(ang), mag * jnp.sin(ang)
    abar_re, abar_im = pw_re[1:2], pw_im[1:2]
    den = ar * ar + ai * ai
    num_re = abar_re - 1.0
    f_re = (num_re * ar + abar_im * ai) / den
    f_im = (abar_im * ar - num_re * ai) / den
    g_re, g_im = cmul(pw_re, pw_im, f_re, f_im)

    pos = lax.broadcasted_iota(jnp.int32, (S5_TC, S5_POW_ROWS), 0) // C_GROUP_CH
    kcol = lax.broadcasted_iota(jnp.int32, (S5_TC, S5_POW_ROWS), 1)

    def pick(which, xr, xi):
        sel = (kcol == which).astype(F32)
        return (jnp.dot(sel, xr, preferred_element_type=F32, precision=HIGHEST),
                jnp.dot(sel, xi, preferred_element_type=F32, precision=HIGHEST))

    tile16 = lambda a: jnp.concatenate([a] * S5_CHUNK, axis=0)
    ct_re, ct_im = tile16(c_re16), tile16(c_im16)
    bt_re, bt_im = tile16(bt_re16), tile16(bt_im16)

    w_re, w_im = cmul(*pick(pos, g_re, g_im), ct_re, ct_im)
    kk = (jnp.dot(w_re, b_re, preferred_element_type=F32, precision=HIGHEST)
          - jnp.dot(w_im, b_im, preferred_element_type=F32, precision=HIGHEST))
    e_re, e_im = cmul(*pick(S5_CHUNK - 1 - pos, g_re, g_im), bt_re, bt_im)
    y_re, y_im = cmul(*pick(pos + 1, pw_re, pw_im), ct_re, ct_im)
    return kk, e_re, e_im, y_re, y_im, pw_re[S5_CHUNK:S5_CHUNK + 1], pw_im[S5_CHUNK:S5_CHUNK + 1]


def _s5prep(a_re, a_im, log_dt, b_re, b_im, c_re, c_im, d):
    p, ch = C_STATE, C_GROUP_CH
    layers = a_re.shape[0]
    g = layers * C_GROUPS
    npair = g // 2
    flat = lambda a: a.reshape(g, *a.shape[2:])
    a_re, a_im, log_dt, b_re, b_im, c_re, c_im = map(flat, (a_re, a_im, log_dt, b_re, b_im, c_re, c_im))
    row = lambda a: a.reshape(g, 1, p)
    ldt = jnp.broadcast_to(log_dt[:, None, None], (g, 1, p))
    b_t = lambda a: jnp.swapaxes(a, 1, 2)
    spec = lambda s1, s2: pl.BlockSpec((2, s1, s2), lambda i: (i, 0, 0))
    mt, wet, wyt, pw_re, pw_im = pl.pallas_call(
        _s5prep_kernel,
        out_shape=[jax.ShapeDtypeStruct((npair, 2, S5_TC, S5_TC), BF16),
                   jax.ShapeDtypeStruct((npair, 4 * p, 2 * S5_TC), BF16),
                   jax.ShapeDtypeStruct((npair, 2 * S5_TC, 4 * p), BF16),
                   jax.ShapeDtypeStruct((npair, 8, 2 * p), F32),
                   jax.ShapeDtypeStruct((npair, 8, 2 * p), F32)],
        grid=(npair,),
        in_specs=[spec(1, p), spec(1, p), spec(1, p), spec(p, ch), spec(p, ch),
                  spec(ch, p), spec(ch, p), spec(ch, p), spec(ch, p)],
        out_specs=[pl.BlockSpec((1, 2, S5_TC, S5_TC), lambda i: (i, 0, 0, 0)),
                   pl.BlockSpec((1, 4 * p, 2 * S5_TC), lambda i: (i, 0, 0)),
                   pl.BlockSpec((1, 2 * S5_TC, 4 * p), lambda i: (i, 0, 0)),
                   pl.BlockSpec((1, 8, 2 * p), lambda i: (i, 0, 0)),
                   pl.BlockSpec((1, 8, 2 * p), lambda i: (i, 0, 0))],
        compiler_params=_cparams(("arbitrary",)),
        name="s5prep",
    )(row(a_re), row(a_im), ldt, b_re, b_im, b_t(b_re), b_t(b_im), c_re, c_im)
    by_gb = lambda a: a.reshape(layers, S5_NGB, S5_PAIRS_PER_GB, *a.shape[1:])
    return (mt.reshape(layers, S5_NGB, S5_GB, S5_TC, S5_TC), by_gb(wet), by_gb(wyt), by_gb(pw_re), by_gb(pw_im),
            d.reshape(layers, 1, C_WIDTH))


def _s5_kernel(u_ref, mt_ref, wet_ref, wyt_ref, are_ref, aim_ref, d_ref, y_ref,
               ut_ref, yt_ref, ere_ref, eim_ref, hre_ref, him_ref):
    nck, t_len, ch = S5_NCHUNK, S5_CHUNK, C_GROUP_CH
    nt = (((1,), (1,)), ((), ()))
    for t in range(t_len):
        xt = u_ref[pl.ds(t, nck, stride=t_len), :].T
        for g in range(S5_GB):
            ut_ref[g, ch * t:ch * (t + 1), :] = xt[ch * g:ch * (g + 1), :]
    for j in range(S5_PAIRS_PER_GB):
        u0 = ut_ref[2 * j].astype(BF16)
        u1 = ut_ref[2 * j + 1].astype(BF16)
        et = jnp.dot(wet_ref[0, j], jnp.concatenate([u0, u1], axis=0), preferred_element_type=F32)
        e = et.T
        ere_ref[:, LANES * j:LANES * (j + 1)] = e[:, :LANES]
        eim_ref[:, LANES * j:LANES * (j + 1)] = e[:, LANES:]
        yt_ref[2 * j] = jnp.dot(mt_ref[0, 2 * j], u0, preferred_element_type=F32)
        yt_ref[2 * j + 1] = jnp.dot(mt_ref[0, 2 * j + 1], u1, preferred_element_type=F32)

    a_re = jnp.concatenate([are_ref[0, j, 0:1, :] for j in range(S5_PAIRS_PER_GB)], axis=1)
    a_im = jnp.concatenate([aim_ref[0, j, 0:1, :] for j in range(S5_PAIRS_PER_GB)], axis=1)

    def body(i, carry):
        h_re, h_im = carry
        hre_ref[pl.ds(i, 1), :] = h_re
        him_ref[pl.ds(i, 1), :] = h_im
        e_re = ere_ref[pl.ds(i, 1), :]
        e_im = eim_ref[pl.ds(i, 1), :]
        return a_re * h_re - a_im * h_im + e_re, a_re * h_im + a_im * h_re + e_im

    zero = jnp.zeros((1, S5_GB * C_STATE), F32)
    lax.fori_loop(0, nck, body, (zero, zero))

    for j in range(S5_PAIRS_PER_GB):
        sl = slice(LANES * j, LANES * (j + 1))
        hp = jnp.concatenate([hre_ref[:, sl], him_ref[:, sl]], axis=1).astype(BF16)
        yi = lax.dot_general(wyt_ref[0, j], hp, nt, preferred_element_type=F32)
        yt_ref[2 * j] += yi[:S5_TC]
        yt_ref[2 * j + 1] += yi[S5_TC:]
    for t in range(t_len):
        ytt = jnp.concatenate([yt_ref[g, ch * t:ch * (t + 1), :] for g in range(S5_GB)], axis=0)
        rows = pl.ds(t, nck, stride=t_len)
        y_ref[rows, :] = ytt.T + d_ref[...] * u_ref[rows, :]


def _s5(cu, mt, wet, wyt, pw_re, pw_im, d_row):
    p4 = 4 * C_STATE
    tok = pl.BlockSpec((SEQ, LANES), lambda gb, b: (b, gb))
    per_gb = lambda *s: pl.BlockSpec((1,) + s, lambda gb, b: (gb,) + (0,) * len(s))
    state = pltpu.VMEM((S5_NCHUNK, S5_GB * C_STATE), F32)
    return pl.pallas_call(
        _s5_kernel,
        out_shape=jax.ShapeDtypeStruct((TOKENS, C_WIDTH), F32),
        grid=(S5_NGB, BATCH),
        in_specs=[tok,
                  per_gb(S5_GB, S5_TC, S5_TC),
                  per_gb(S5_PAIRS_PER_GB, p4, 2 * S5_TC),
                  per_gb(S5_PAIRS_PER_GB, 2 * S5_TC, p4),
                  per_gb(S5_PAIRS_PER_GB, 8, 2 * C_STATE), per_gb(S5_PAIRS_PER_GB, 8, 2 * C_STATE),
                  pl.BlockSpec((1, LANES), lambda gb, b: (0, gb))],
        out_specs=tok,
        scratch_shapes=[pltpu.VMEM((S5_GB, S5_TC, S5_NCHUNK), F32),
                        pltpu.VMEM((S5_GB, S5_TC, S5_NCHUNK), F32),
                        state, state, state, state],
        compiler_params=_cparams(("arbitrary", "arbitrary")),
        name="s5",
    )(cu, mt, wet, wyt, pw_re, pw_im, d_row)


def _gelu_tanh(x):
    return 0.5 * x * (1.0 + jnp.tanh(math.sqrt(2.0 / math.pi) * (x + 0.044715 * (x * x * x))))


def _out_kernel(oa_ref, ob_ref, yc_ref, cg_ref, x_ref, gate_ref, gpost_ref, wglu_ref, bglu_ref, wout_ref, o_ref):
    y = _gelu_tanh(yc_ref[...])
    z = jnp.dot(y.astype(BF16), wglu_ref[...], preferred_element_type=F32) + bglu_ref[...]
    y = y * jax.nn.sigmoid(z)
    oc = (y * _silu(cg_ref[...].astype(F32))).astype(BF16)
    acc = jnp.dot(oa_ref[...], wout_ref[0:A_WIDTH, :], preferred_element_type=F32)
    acc = acc + jnp.dot(ob_ref[...], wout_ref[A_WIDTH:A_WIDTH + B_WIDTH, :], preferred_element_type=F32)
    acc = acc + jnp.dot(oc, wout_ref[A_WIDTH + B_WIDTH:, :], preferred_element_type=F32)
    ms = jnp.mean(acc * acc, axis=-1, keepdims=True)
    out = acc * lax.rsqrt(ms + EPS) * gpost_ref[...]
    o_ref[...] = x_ref[...] + gate_ref[0] * out


def _out(oa, ob, yc, cg, x2, gate, g_post, w_glu, b_glu, w_out):
    tm = OUT_TM
    steps_per_batch = SEQ // tm
    row = lambda i: (i, 0)
    const = lambda i: (0, 0)
    return pl.pallas_call(
        _out_kernel,
        out_shape=jax.ShapeDtypeStruct((TOKENS, D_MODEL), F32),
        grid=(TOKENS // tm,),
        in_specs=[pl.BlockSpec((tm, A_WIDTH), row),
                  pl.BlockSpec((tm, B_WIDTH), row),
                  pl.BlockSpec((tm, C_WIDTH), row),
                  pl.BlockSpec((tm, C_WIDTH), row),
                  pl.BlockSpec((tm, D_MODEL), row),
                  pl.BlockSpec((1, 1, D_MODEL), lambda i: (i // steps_per_batch, 0, 0)),
                  pl.BlockSpec((1, D_MODEL), const),
                  pl.BlockSpec((C_WIDTH, C_WIDTH), const),
                  pl.BlockSpec((1, C_WIDTH), const),
                  pl.BlockSpec((2 * D_MODEL, D_MODEL), const)],
        out_specs=pl.BlockSpec((tm, D_MODEL), row),
        compiler_params=_cparams(("arbitrary",)),
        name="out",
    )(oa, ob, yc, cg, x2, gate, g_post.reshape(1, D_MODEL), w_glu, b_glu.reshape(1, C_WIDTH), w_out)


_W_IN_SIZES = (("aq", A_WIDTH), ("ak", A_KV_WIDTH), ("av", A_KV_WIDTH), ("ag", A_WIDTH), ("bq", B_QK_WIDTH),
               ("bk", B_QK_WIDTH), ("bv", B_WIDTH), ("blr", B_GATE_RANK), ("bg", B_WIDTH), ("cu", C_WIDTH),
               ("cg", C_WIDTH))
_W_IN_OFF = {}
_off = 0
for _name, _w in _W_IN_SIZES:
    _W_IN_OFF[_name] = _off
    _off += _w
W_IN_COLS = _off
WPREP_ROWS = 128


def _wprep_kernel(w_ref, wp_ref, wt_ref):
    w = w_ref[0]
    for name, (dst, width) in _PROJ_LAYOUT.items():
        src = _W_IN_OFF[name]
        if name == "blr":
            piece = jnp.concatenate([w[:, src:src + B_GATE_RANK],
                                     jnp.zeros((WPREP_ROWS, LR_PAD - B_GATE_RANK), F32)], axis=1)
        else:
            piece = w[:, src:src + width]
        wp_ref[0, :, dst:dst + width] = piece.astype(BF16)
    qv = jnp.concatenate([w[:, _W_IN_OFF["aq"]:_W_IN_OFF["aq"] + A_WIDTH] * (A_HEAD_DIM ** -0.5),
                          w[:, _W_IN_OFF["av"]:_W_IN_OFF["av"] + A_KV_WIDTH]], axis=1)
    wt_ref[0] = qv.T.astype(BF16)


def _wprep(w_in):
    layers = w_in.shape[0]
    return pl.pallas_call(
        _wprep_kernel,
        out_shape=[jax.ShapeDtypeStruct((layers, D_MODEL, PROJ_COLS_PADDED), BF16),
                   jax.ShapeDtypeStruct((layers, PROJ_T_ROWS, D_MODEL), BF16)],
        grid=(layers, D_MODEL // WPREP_ROWS),
        in_specs=[pl.BlockSpec((1, WPREP_ROWS, W_IN_COLS), lambda l, i: (l, i, 0))],
        out_specs=[pl.BlockSpec((1, WPREP_ROWS, PROJ_COLS_PADDED), lambda l, i: (l, i, 0)),
                   pl.BlockSpec((1, PROJ_T_ROWS, WPREP_ROWS), lambda l, i: (l, 0, i))],
        compiler_params=_cparams(("arbitrary", "arbitrary")),
        name="wprep",
    )(w_in)


def kernel(x, c, w_mod, b_mod, g_pre, g_post, w_in, attn_sinks, gla_w_alpha, gla_b_alpha, gla_norm_g,
           s5_a_re, s5_a_im, s5_log_dt, s5_b_re, s5_b_im, s5_c_re, s5_c_im, s5_d, s5_w_glu, s5_b_glu, w_out):
    layers = w_mod.shape[0]
    x2 = x.reshape(TOKENS, D_MODEL)
    bias = _attn_bias()
    mod = _mod(jnp.pad(c, ((0, 8 - BATCH), (0, 0))), w_mod, b_mod)[:, :BATCH]
    shift, scale, gate = (m.reshape(layers, BATCH, 1, D_MODEL) for m in jnp.split(mod, 3, axis=-1))
    w_perm, w_t = _wprep(w_in)
    w_alpha_pad = jnp.pad(gla_w_alpha, ((0, 0), (0, LR_PAD - B_GATE_RANK), (0, 0))).astype(BF16)
    s5_ops = _s5prep(s5_a_re, s5_a_im, s5_log_dt, s5_b_re, s5_b_im, s5_c_re, s5_c_im, s5_d)
    w_glu = s5_w_glu.astype(BF16)
    w_out_b = w_out.astype(BF16)
    for l in range(layers):
        ak, ag, bq, bk, bv, bg, cu, cg, log_a, aqt, avt = _proj(
            x2, scale[l], shift[l], g_pre[l], w_perm[l], w_t[l], w_alpha_pad[l], gla_b_alpha[l])
        o_a = _attn(attn_sinks[l], aqt, ak, avt, ag, bias)
        o_b = _gla(log_a, bq, bk, bv, bg, gla_norm_g[l])
        y_c = _s5(cu, *(op[l] for op in s5_ops))
        x2 = _out(o_a, o_b, y_c, cg, x2, gate[l], g_post[l], w_glu[l], s5_b_glu[l], w_out_b[l])
    return x2.reshape(x.shape)
```

```python
import math

import jax
import jax.numpy as jnp
import numpy as np
from jax import lax
from jax.experimental import pallas as pl
from jax.experimental.pallas import tpu as pltpu

F32 = jnp.float32
BF16 = jnp.bfloat16
HIGHEST = lax.Precision.HIGHEST

D_MODEL = 1024
BATCH = 4
SEQ = 4096
TOKENS = BATCH * SEQ
EPS = 1e-6

A_WIDTH = 1024
A_HEAD_DIM = 64
A_Q_HEADS = 16
A_KV_HEADS = 4
A_KV_WIDTH = A_KV_HEADS * A_HEAD_DIM
A_BLOCK = 128
WINDOW = 128

B_WIDTH = 512
B_HEADS = 4
B_DK = 64
B_DV = 128
B_QK_WIDTH = 256
B_GATE_RANK = 16
B_GATE_TAU = 16.0
GLA_BLOCK = 64
C_WIDTH = 512
C_GROUP_CH = 16
C_GROUPS = 32
C_STATE = 64
S5_CHUNK = 16
S5_NCHUNK = SEQ // S5_CHUNK
S5_TC = S5_CHUNK * C_GROUP_CH
S5_GB = 8
S5_NGB = C_GROUPS // S5_GB
S5_PAIRS_PER_GB = S5_GB // 2
S5_POW_ROWS = 24

LANES = 128
LR_PAD = LANES

V7X_VMEM_LIMIT = 56 * 1024 * 1024

PROJ_TM = 512
OUT_TM = 512

_PROJ_LAYOUT = {}
_off = 0
for _name, _w in (("ak", A_KV_WIDTH), ("ag", A_WIDTH),
                  ("bq", B_QK_WIDTH), ("bk", B_QK_WIDTH), ("bv", B_WIDTH), ("blr", LR_PAD),
                  ("bg", B_WIDTH), ("cu", C_WIDTH), ("cg", C_WIDTH)):
    _PROJ_LAYOUT[_name] = (_off, _w)
    _off += _w
PROJ_COLS_PADDED = _off
_PROJ_OUTS = (("ak", BF16), ("ag", BF16), ("bq", BF16), ("bk", BF16),
              ("bv", BF16), ("bg", BF16), ("cu", F32), ("cg", BF16))
PROJ_T_ROWS = A_WIDTH + A_KV_WIDTH


def _silu(x):
    return x * jax.nn.sigmoid(x)


def _cparams(semantics):
    return pltpu.CompilerParams(dimension_semantics=semantics, vmem_limit_bytes=V7X_VMEM_LIMIT)


def _mod_kernel(c_ref, w_ref, b_ref, o_ref):
    c = c_ref[...]
    o_ref[0] = jnp.dot(_silu(c), w_ref[0], preferred_element_type=F32, precision=HIGHEST) + b_ref[0]


def _mod(c_pad, w_mod, b_mod):
    layers = w_mod.shape[0]
    n = 3 * D_MODEL
    tn = 768
    return pl.pallas_call(
        _mod_kernel,
        out_shape=jax.ShapeDtypeStruct((layers, 8, n), F32),
        grid=(layers, n // tn),
        in_specs=[pl.BlockSpec((8, D_MODEL), lambda l, j: (0, 0)),
                  pl.BlockSpec((1, D_MODEL, tn), lambda l, j: (l, 0, j)),
                  pl.BlockSpec((1, 1, tn), lambda l, j: (l, 0, j))],
        out_specs=pl.BlockSpec((1, 8, tn), lambda l, j: (l, 0, j)),
        compiler_params=_cparams(("arbitrary", "arbitrary")),
        name="mod",
    )(c_pad, w_mod, b_mod.reshape(layers, 1, n))


def _proj_kernel(x_ref, scale_ref, shift_ref, gpre_ref, w_ref, wt_ref, walpha_ref, balpha_ref, *out_refs):
    x = x_ref[...]
    ms = jnp.mean(x * x, axis=-1, keepdims=True)
    y = x * lax.rsqrt(ms + EPS) * gpre_ref[...]
    h = (y * (1.0 + scale_ref[0]) + shift_ref[0]).astype(BF16)
    for (name, _), o_ref in zip(_PROJ_OUTS, out_refs):
        off, width = _PROJ_LAYOUT[name]
        o_ref[...] = jnp.dot(h, w_ref[:, off:off + width], preferred_element_type=F32).astype(o_ref.dtype)
    off, width = _PROJ_LAYOUT["blr"]
    lr = jnp.dot(h, w_ref[:, off:off + width], preferred_element_type=F32).astype(BF16)
    logits = jnp.dot(lr, walpha_ref[...], preferred_element_type=F32) + balpha_ref[...]
    log_sig = jnp.minimum(logits, 0.0) - jnp.log(1.0 + jnp.exp(-jnp.abs(logits)))
    la_ref, qt_ref, vt_ref = out_refs[len(_PROJ_OUTS):]
    la_ref[...] = log_sig * (1.0 / B_GATE_TAU)
    nt = (((1,), (1,)), ((), ()))
    qt_ref[...] = lax.dot_general(wt_ref[0:A_WIDTH, :], h, nt, preferred_element_type=F32).astype(qt_ref.dtype)
    vt_ref[...] = lax.dot_general(wt_ref[A_WIDTH:, :], h, nt, preferred_element_type=F32).astype(vt_ref.dtype)


def _proj(x2, scale, shift, g_pre, w_perm, w_t, w_alpha_pad, b_alpha):
    tm = PROJ_TM
    steps_per_batch = SEQ // tm
    row = lambda i: (i, 0)
    col = lambda i: (0, i)
    per_batch = lambda i: (i // steps_per_batch, 0, 0)
    const = lambda i: (0, 0)
    out_shape = [jax.ShapeDtypeStruct((TOKENS, _PROJ_LAYOUT[n][1]), dt) for n, dt in _PROJ_OUTS]
    out_specs = [pl.BlockSpec((tm, _PROJ_LAYOUT[n][1]), row) for n, _ in _PROJ_OUTS]
    out_shape += [jax.ShapeDtypeStruct((TOKENS, B_QK_WIDTH), F32),
                  jax.ShapeDtypeStruct((A_WIDTH, TOKENS), BF16),
                  jax.ShapeDtypeStruct((A_KV_WIDTH, TOKENS), BF16)]
    out_specs += [pl.BlockSpec((tm, B_QK_WIDTH), row),
                  pl.BlockSpec((A_WIDTH, tm), col),
                  pl.BlockSpec((A_KV_WIDTH, tm), col)]
    return pl.pallas_call(
        _proj_kernel,
        out_shape=out_shape,
        grid=(TOKENS // tm,),
        in_specs=[pl.BlockSpec((tm, D_MODEL), row),
                  pl.BlockSpec((1, 1, D_MODEL), per_batch),
                  pl.BlockSpec((1, 1, D_MODEL), per_batch),
                  pl.BlockSpec((1, D_MODEL), const),
                  pl.BlockSpec((D_MODEL, PROJ_COLS_PADDED), const),
                  pl.BlockSpec((PROJ_T_ROWS, D_MODEL), const),
                  pl.BlockSpec((LR_PAD, B_QK_WIDTH), const),
                  pl.BlockSpec((1, B_QK_WIDTH), const)],
        out_specs=out_specs,
        compiler_params=_cparams(("arbitrary",)),
        name="proj",
    )(x2, scale, shift, g_pre.reshape(1, D_MODEL), w_perm, w_t, w_alpha_pad, b_alpha.reshape(1, B_QK_WIDTH))


def _attn_bias():
    j = np.arange(A_BLOCK)[:, None]
    i = np.arange(A_BLOCK)[None, :]
    dist = np.where(j > i, i + A_BLOCK - j, i - j).astype(np.float32)
    slopes = np.exp2(-8.0 * np.arange(1, A_Q_HEADS + 1, dtype=np.float32) / A_Q_HEADS).astype(np.float32)
    bias = -slopes[:, None, None] * dist[None]
    first = np.where((j > i)[None], -np.inf, bias).astype(np.float32)
    return jnp.asarray(np.stack([bias, first]))


def _attn_kernel(sink_ref, qt_ref, kp_ref, kc_ref, vtp_ref, vtc_ref, g_ref, bias_ref, o_ref):
    n = pl.program_id(1)
    table = jnp.where(n == 0, 1, 0)
    kj = lax.broadcasted_iota(jnp.int32, (A_BLOCK, A_BLOCK), 0)
    qi = lax.broadcasted_iota(jnp.int32, (A_BLOCK, A_BLOCK), 1)
    from_prev = kj > qi
    zero_rows = jnp.zeros((A_HEAD_DIM, A_BLOCK), BF16)
    group = A_Q_HEADS // A_KV_HEADS

    def scores(hd):
        kvh = hd // group
        sl = slice(LANES * (kvh // 2), LANES * (kvh // 2 + 1))
        qh = qt_ref[A_HEAD_DIM * hd:A_HEAD_DIM * (hd + 1), :]
        qsel = jnp.concatenate([qh, zero_rows] if kvh % 2 == 0 else [zero_rows, qh], axis=0)
        return (jnp.dot(kp_ref[:, sl], qsel, preferred_element_type=F32),
                jnp.dot(kc_ref[:, sl], qsel, preferred_element_type=F32))

    def attend(hd, s_prev, s_cur):
        kvh = hd // group
        vrows = slice(A_HEAD_DIM * kvh, A_HEAD_DIM * (kvh + 1))
        v_both = jnp.concatenate([vtp_ref[vrows, :], vtc_ref[vrows, :]], axis=1)
        s = jnp.where(from_prev, s_prev, s_cur) + bias_ref[table, hd]
        sink = sink_ref[hd]
        m = jnp.maximum(jnp.max(s, axis=0, keepdims=True), sink)
        p = jnp.exp(s - m)
        den = jnp.sum(p, axis=0, keepdims=True) + jnp.exp(sink - m)
        p_both = jnp.concatenate([jnp.where(from_prev, p, 0.0), jnp.where(from_prev, 0.0, p)],
                                 axis=0).astype(BF16)
        return jnp.dot(v_both, p_both, preferred_element_type=F32) / den

    pending = [scores(hd) for hd in range(A_Q_HEADS)]
    outs = {}
    for hd in range(A_Q_HEADS):
        outs[hd] = attend(hd, *pending[hd])
        if hd % 2 == 1:
            qsl = slice(LANES * (hd // 2), LANES * (hd // 2 + 1))
            o_pair = jnp.concatenate([outs.pop(hd - 1), outs.pop(hd)], axis=0).T
            gate = g_ref[:, qsl].astype(F32)
            o_ref[:, qsl] = (o_pair * _silu(gate)).astype(o_ref.dtype)


def _attn(sinks, qt, k, vt, ag, bias):
    nb = SEQ // A_BLOCK
    cur = lambda b, n: (b * nb + n, 0)
    prev = lambda b, n: (b * nb + jnp.maximum(n - 1, 0), 0)
    cur_t = lambda b, n: (0, b * nb + n)
    prev_t = lambda b, n: (0, b * nb + jnp.maximum(n - 1, 0))
    return pl.pallas_call(
        _attn_kernel,
        out_shape=jax.ShapeDtypeStruct((TOKENS, A_WIDTH), BF16),
        grid=(BATCH, nb),
        in_specs=[pl.BlockSpec(memory_space=pltpu.SMEM),
                  pl.BlockSpec((A_WIDTH, A_BLOCK), cur_t),
                  pl.BlockSpec((A_BLOCK, A_KV_WIDTH), prev),
                  pl.BlockSpec((A_BLOCK, A_KV_WIDTH), cur),
                  pl.BlockSpec((A_KV_WIDTH, A_BLOCK), prev_t),
                  pl.BlockSpec((A_KV_WIDTH, A_BLOCK), cur_t),
                  pl.BlockSpec((A_BLOCK, A_WIDTH), cur),
                  pl.BlockSpec((2, A_Q_HEADS, A_BLOCK, A_BLOCK), lambda b, n: (0, 0, 0, 0))],
        out_specs=pl.BlockSpec((A_BLOCK, A_WIDTH), cur),
        compiler_params=_cparams(("arbitrary", "arbitrary")),
        name="attn",
    )(sinks, qt, k, k, vt, vt, ag, bias)


def _gla_kernel(la_ref, q_ref, k_ref, v_ref, g_ref, gn_ref, o_ref, st_ref):
    cb = GLA_BLOCK

    @pl.when(pl.program_id(0) == 0)
    def _():
        st_ref[...] = jnp.zeros_like(st_ref)

    r = lax.broadcasted_iota(jnp.int32, (cb, cb), 0)
    c = lax.broadcasted_iota(jnp.int32, (cb, cb), 1)
    tri = (c <= r).astype(F32)
    lane = lax.broadcasted_iota(jnp.int32, (cb, B_QK_WIDTH), 1)
    head_masks = [(lane >= B_DK * h) & (lane < B_DK * (h + 1)) for h in range(B_HEADS)]
    rr = lax.broadcasted_iota(jnp.int32, (B_HEADS * cb, cb), 0)
    cc = lax.broadcasted_iota(jnp.int32, (B_HEADS * cb, cb), 1)
    causal = cc <= (rr & (cb - 1))
    nt = (((1,), (1,)), ((), ()))
    tn = (((0,), (0,)), ((), ()))
    seqs = range(BATCH)
    bcs = [jnp.dot(tri, la_ref[b], preferred_element_type=F32, precision=HIGHEST) for b in seqs]
    qsts, kss, ksts, decs = [], [], [], []
    for b in seqs:
        bc = bcs[b]
        bl = bc[cb - 1:cb, :]
        q = q_ref[b].astype(F32) * (B_DK ** -0.5)
        k = k_ref[b].astype(F32)
        qs = q * jnp.exp(bc)
        kh = k * jnp.exp(bl - bc)
        kss.append((k * jnp.exp(-bc)).astype(BF16))
        decs.append(jnp.exp(bl))
        qsts.append(jnp.concatenate([jnp.where(m, qs, 0.0) for m in head_masks], axis=0).astype(BF16))
        ksts.append(jnp.concatenate([jnp.where(m, kh, 0.0) for m in head_masks], axis=0).astype(BF16))
    sts = [st_ref[b] for b in seqs]
    vs = [v_ref[b] for b in seqs]
    a_alls = [lax.dot_general(qsts[b], kss[b], nt, preferred_element_type=F32) for b in seqs]
    oi_alls = [lax.dot_general(qsts[b], sts[b].astype(BF16), nt, preferred_element_type=F32) for b in seqs]
    for b in seqs:
        vst = jnp.concatenate([vs[b][:, B_DV * h:B_DV * (h + 1)] for h in range(B_HEADS)], axis=0)
        upd = lax.dot_general(vst, ksts[b], tn, preferred_element_type=F32)
        st_ref[b] = sts[b] * decs[b] + upd
    o_hs = {}
    for b in seqs:
        a_all = jnp.where(causal, a_alls[b], 0.0).astype(BF16)
        for h in range(B_HEADS):
            o_hs[b, h] = (jnp.dot(a_all[cb * h:cb * (h + 1)], vs[b][:, B_DV * h:B_DV * (h + 1)],
                                  preferred_element_type=F32) + oi_alls[b][cb * h:cb * (h + 1)])
    for b in seqs:
        for h in range(B_HEADS):
            vsl = slice(B_DV * h, B_DV * (h + 1))
            o_h = o_hs[b, h]
            ms = jnp.mean(o_h * o_h, axis=-1, keepdims=True)
            o_n = o_h * lax.rsqrt(ms + EPS) * gn_ref[:, vsl]
            gate = g_ref[b, :, vsl].astype(F32)
            o_ref[b, :, vsl] = (o_n * _silu(gate)).astype(o_ref.dtype)


def _gla(log_a, bq, bk, bv, bg, g_gla):
    cb = GLA_BLOCK
    blk = lambda w: pl.BlockSpec((BATCH, cb, w), lambda i: (0, i, 0))
    r3 = lambda a: a.reshape(BATCH, SEQ, a.shape[-1])
    out = pl.pallas_call(
        _gla_kernel,
        out_shape=jax.ShapeDtypeStruct((BATCH, SEQ, B_WIDTH), BF16),
        grid=(SEQ // cb,),
        in_specs=[blk(B_QK_WIDTH), blk(B_QK_WIDTH), blk(B_QK_WIDTH), blk(B_WIDTH), blk(B_WIDTH),
                  pl.BlockSpec((1, B_WIDTH), lambda i: (0, 0))],
        out_specs=blk(B_WIDTH),
        scratch_shapes=[pltpu.VMEM((BATCH, B_DV, B_QK_WIDTH), F32)],
        compiler_params=_cparams(("arbitrary",)),
        name="gla",
    )(r3(log_a), r3(bq), r3(bk), r3(bv), r3(bg), g_gla.reshape(1, B_WIDTH))
    return out.reshape(TOKENS, B_WIDTH)


def _s5prep_kernel(ar_ref, ai_ref, ldt_ref, bre_ref, bim_ref, btre_ref, btim_ref, cre_ref, cim_ref,
                   mt_ref, wet_ref, wyt_ref, are_ref, aim_ref):
    p = C_STATE
    wet_re, wet_im, wyt_rows, a_re, a_im = [], [], [], [], []
    for g in range(2):
        kk, e_re, e_im, y_re, y_im, p_re, p_im = _s5_discretise(
            ar_ref[g], ai_ref[g], ldt_ref[g], bre_ref[g], bim_ref[g], btre_ref[g], btim_ref[g],
            cre_ref[g], cim_ref[g])
        pieces = [kk] + [jnp.concatenate([jnp.zeros((C_GROUP_CH * s, C_GROUP_CH), F32),
                                          kk[:S5_TC - C_GROUP_CH * s]], axis=0) for s in range(1, S5_CHUNK)]
        mt_ref[0, g] = jnp.concatenate(pieces, axis=1).astype(BF16)
        e_t = jnp.concatenate([e_re, e_im], axis=1).T
        zero = jnp.zeros((p, S5_TC), F32)
        wet_re.append(jnp.concatenate([e_t[:p], zero] if g == 0 else [zero, e_t[:p]], axis=1))
        wet_im.append(jnp.concatenate([e_t[p:], zero] if g == 0 else [zero, e_t[p:]], axis=1))
        zero = jnp.zeros((S5_TC, p), F32)
        wyt_rows.append(jnp.concatenate([y_re, zero, -y_im, zero] if g == 0 else [zero, y_re, zero, -y_im], axis=1))
        a_re.append(p_re)
        a_im.append(p_im)
    wet_ref[0] = jnp.concatenate(wet_re + wet_im, axis=0).astype(BF16)
    wyt_ref[0] = jnp.concatenate(wyt_rows, axis=0).astype(BF16)
    are_ref[0] = jnp.broadcast_to(jnp.concatenate(a_re, axis=1), (8, 2 * p))
    aim_ref[0] = jnp.broadcast_to(jnp.concatenate(a_im, axis=1), (8, 2 * p))


def _s5_discretise(ar, ai, ldt, b_re, b_im, bt_re16, bt_im16, c_re16, c_im16):
    dt = jnp.exp(ldt)

    def cmul(xr, xi, yr, yi):
        return xr * yr - xi * yi, xr * yi + xi * yr

    kf = lax.broadcasted_iota(jnp.int32, (S5_POW_ROWS, 1), 0).astype(F32)
    mag = jnp.exp(kf * (ar * dt))
    ang = kf * (ai * dt)
    pw_re, pw_im = mag * jnp.cos(ang), mag * jnp.sin(ang)
    abar_re, abar_im = pw_re[1:2], pw_im[1:2]
    den = ar * ar + ai * ai
    num_re = abar_re - 1.0
    f_re = (num_re * ar + abar_im * ai) / den
    f_im = (abar_im * ar - num_re * ai) / den
    g_re, g_im = cmul(pw_re, pw_im, f_re, f_im)

    def pick(which, xr, xi):
        rep = lambda x: jnp.concatenate(
            [jnp.broadcast_to(x[which(i):which(i) + 1], (C_GROUP_CH, C_STATE)) for i in range(S5_CHUNK)], axis=0)
        return rep(xr), rep(xi)

    tile16 = lambda a: jnp.concatenate([a] * S5_CHUNK, axis=0)
    ct_re, ct_im = tile16(c_re16), tile16(c_im16)
    bt_re, bt_im = tile16(bt_re16), tile16(bt_im16)

    w_re, w_im = cmul(*pick(lambda i: i, g_re, g_im), ct_re, ct_im)
    kk = (jnp.dot(w_re, b_re, preferred_element_type=F32, precision=HIGHEST)
          - jnp.dot(w_im, b_im, preferred_element_type=F32, precision=HIGHEST))
    e_re, e_im = cmul(*pick(lambda i: S5_CHUNK - 1 - i, g_re, g_im), bt_re, bt_im)
    y_re, y_im = cmul(*pick(lambda i: i + 1, pw_re, pw_im), ct_re, ct_im)
    return kk, e_re, e_im, y_re, y_im, pw_re[S5_CHUNK:S5_CHUNK + 1], pw_im[S5_CHUNK:S5_CHUNK + 1]


def _s5prep(a_re, a_im, log_dt, b_re, b_im, c_re, c_im, d):
    p, ch = C_STATE, C_GROUP_CH
    layers = a_re.shape[0]
    g = layers * C_GROUPS
    npair = g // 2
    flat = lambda a: a.reshape(g, *a.shape[2:])
    a_re, a_im, log_dt, b_re, b_im, c_re, c_im = map(flat, (a_re, a_im, log_dt, b_re, b_im, c_re, c_im))
    row = lambda a: a.reshape(g, 1, p)
    ldt = jnp.broadcast_to(log_dt[:, None, None], (g, 1, p))
    b_t = lambda a: jnp.swapaxes(a, 1, 2)
    spec = lambda s1, s2: pl.BlockSpec((2, s1, s2), lambda i: (i, 0, 0))
    mt, wet, wyt, pw_re, pw_im = pl.pallas_call(
        _s5prep_kernel,
        out_shape=[jax.ShapeDtypeStruct((npair, 2, S5_TC, S5_TC), BF16),
                   jax.ShapeDtypeStruct((npair, 4 * p, 2 * S5_TC), BF16),
                   jax.ShapeDtypeStruct((npair, 2 * S5_TC, 4 * p), BF16),
                   jax.ShapeDtypeStruct((npair, 8, 2 * p), F32),
                   jax.ShapeDtypeStruct((npair, 8, 2 * p), F32)],
        grid=(npair,),
        in_specs=[spec(1, p), spec(1, p), spec(1, p), spec(p, ch), spec(p, ch),
                  spec(ch, p), spec(ch, p), spec(ch, p), spec(ch, p)],
        out_specs=[pl.BlockSpec((1, 2, S5_TC, S5_TC), lambda i: (i, 0, 0, 0)),
                   pl.BlockSpec((1, 4 * p, 2 * S5_TC), lambda i: (i, 0, 0)),
                   pl.BlockSpec((1, 2 * S5_TC, 4 * p), lambda i: (i, 0, 0)),
                   pl.BlockSpec((1, 8, 2 * p), lambda i: (i, 0, 0)),
                   pl.BlockSpec((1, 8, 2 * p), lambda i: (i, 0, 0))],
        compiler_params=_cparams(("arbitrary",)),
        name="s5prep",
    )(row(a_re), row(a_im), ldt, b_re, b_im, b_t(b_re), b_t(b_im), c_re, c_im)
    by_gb = lambda a: a.reshape(layers, S5_NGB, S5_PAIRS_PER_GB, *a.shape[1:])
    return (mt.reshape(layers, S5_NGB, S5_GB, S5_TC, S5_TC), by_gb(wet), by_gb(wyt), by_gb(pw_re), by_gb(pw_im),
            d.reshape(layers, 1, C_WIDTH))


def _s5_kernel(u_ref, mt_ref, wet_ref, wyt_ref, are_ref, aim_ref, d_ref, y_ref,
               ut_ref, yt_ref, ere_ref, eim_ref, hre_ref, him_ref):
    nck, t_len, ch = S5_NCHUNK, S5_CHUNK, C_GROUP_CH
    nt = (((1,), (1,)), ((), ()))
    for t in range(t_len):
        xt = u_ref[pl.ds(t, nck, stride=t_len), :].T
        for g in range(S5_GB):
            ut_ref[g, ch * t:ch * (t + 1), :] = xt[ch * g:ch * (g + 1), :]
    for j in range(S5_PAIRS_PER_GB):
        u0 = ut_ref[2 * j].astype(BF16)
        u1 = ut_ref[2 * j + 1].astype(BF16)
        et = jnp.dot(wet_ref[0, j], jnp.concatenate([u0, u1], axis=0), preferred_element_type=F32)
        e = et.T
        ere_ref[:, LANES * j:LANES * (j + 1)] = e[:, :LANES]
        eim_ref[:, LANES * j:LANES * (j + 1)] = e[:, LANES:]
        yt_ref[2 * j] = jnp.dot(mt_ref[0, 2 * j], u0, preferred_element_type=F32)
        yt_ref[2 * j + 1] = jnp.dot(mt_ref[0, 2 * j + 1], u1, preferred_element_type=F32)

    a_re = jnp.concatenate([are_ref[0, j, 0:1, :] for j in range(S5_PAIRS_PER_GB)], axis=1)
    a_im = jnp.concatenate([aim_ref[0, j, 0:1, :] for j in range(S5_PAIRS_PER_GB)], axis=1)

    def body(i, carry):
        h_re, h_im = carry
        hre_ref[pl.ds(i, 1), :] = h_re
        him_ref[pl.ds(i, 1), :] = h_im
        e_re = ere_ref[pl.ds(i, 1), :]
        e_im = eim_ref[pl.ds(i, 1), :]
        return a_re * h_re - a_im * h_im + e_re, a_re * h_im + a_im * h_re + e_im

    zero = jnp.zeros((1, S5_GB * C_STATE), F32)
    lax.fori_loop(0, nck, body, (zero, zero))

    for j in range(S5_PAIRS_PER_GB):
        sl = slice(LANES * j, LANES * (j + 1))
        hp = jnp.concatenate([hre_ref[:, sl], him_ref[:, sl]], axis=1).astype(BF16)
        yi = lax.dot_general(wyt_ref[0, j], hp, nt, preferred_element_type=F32)
        yt_ref[2 * j] += yi[:S5_TC]
        yt_ref[2 * j + 1] += yi[S5_TC:]
    for t in range(t_len):
        ytt = jnp.concatenate([yt_ref[g, ch * t:ch * (t + 1), :] for g in range(S5_GB)], axis=0)
        rows = pl.ds(t, nck, stride=t_len)
        y_ref[rows, :] = ytt.T + d_ref[...] * u_ref[rows, :]


def _s5(cu, mt, wet, wyt, pw_re, pw_im, d_row):
    p4 = 4 * C_STATE
    tok = pl.BlockSpec((SEQ, LANES), lambda gb, b: (b, gb))
    per_gb = lambda *s: pl.BlockSpec((1,) + s, lambda gb, b: (gb,) + (0,) * len(s))
    state = pltpu.VMEM((S5_NCHUNK, S5_GB * C_STATE), F32)
    return pl.pallas_call(
        _s5_kernel,
        out_shape=jax.ShapeDtypeStruct((TOKENS, C_WIDTH), F32),
        grid=(S5_NGB, BATCH),
        in_specs=[tok,
                  per_gb(S5_GB, S5_TC, S5_TC),
                  per_gb(S5_PAIRS_PER_GB, p4, 2 * S5_TC),
                  per_gb(S5_PAIRS_PER_GB, 2 * S5_TC, p4),
                  per_gb(S5_PAIRS_PER_GB, 8, 2 * C_STATE), per_gb(S5_PAIRS_PER_GB, 8, 2 * C_STATE),
                  pl.BlockSpec((1, LANES), lambda gb, b: (0, gb))],
        out_specs=tok,
        scratch_shapes=[pltpu.VMEM((S5_GB, S5_TC, S5_NCHUNK), F32),
                        pltpu.VMEM((S5_GB, S5_TC, S5_NCHUNK), F32),
                        state, state, state, state],
        compiler_params=_cparams(("arbitrary", "arbitrary")),
        name="s5",
    )(cu, mt, wet, wyt, pw_re, pw_im, d_row)


def _gelu_tanh(x):
    return 0.5 * x * (1.0 + jnp.tanh(math.sqrt(2.0 / math.pi) * (x + 0.044715 * (x * x * x))))


def _out_kernel(oa_ref, ob_ref, yc_ref, cg_ref, x_ref, gate_ref, gpost_ref, wglu_ref, bglu_ref, wout_ref, o_ref):
    y = _gelu_tanh(yc_ref[...])
    z = jnp.dot(y.astype(BF16), wglu_ref[...], preferred_element_type=F32) + bglu_ref[...]
    y = y * jax.nn.sigmoid(z)
    oc = (y * _silu(cg_ref[...].astype(F32))).astype(BF16)
    mix = jnp.concatenate([oa_ref[...], ob_ref[...], oc], axis=1)
    acc = jnp.dot(mix, wout_ref[...], preferred_element_type=F32)
    ms = jnp.mean(acc * acc, axis=-1, keepdims=True)
    out = acc * lax.rsqrt(ms + EPS) * gpost_ref[...]
    o_ref[...] = x_ref[...] + gate_ref[0] * out


def _out(oa, ob, yc, cg, x2, gate, g_post, w_glu, b_glu, w_out):
    tm = OUT_TM
    steps_per_batch = SEQ // tm
    row = lambda i: (i, 0)
    const = lambda i: (0, 0)
    return pl.pallas_call(
        _out_kernel,
        out_shape=jax.ShapeDtypeStruct((TOKENS, D_MODEL), F32),
        grid=(TOKENS // tm,),
        in_specs=[pl.BlockSpec((tm, A_WIDTH), row),
                  pl.BlockSpec((tm, B_WIDTH), row),
                  pl.BlockSpec((tm, C_WIDTH), row),
                  pl.BlockSpec((tm, C_WIDTH), row),
                  pl.BlockSpec((tm, D_MODEL), row),
                  pl.BlockSpec((1, 1, D_MODEL), lambda i: (i // steps_per_batch, 0, 0)),
                  pl.BlockSpec((1, D_MODEL), const),
                  pl.BlockSpec((C_WIDTH, C_WIDTH), const),
                  pl.BlockSpec((1, C_WIDTH), const),
                  pl.BlockSpec((2 * D_MODEL, D_MODEL), const)],
        out_specs=pl.BlockSpec((tm, D_MODEL), row),
        compiler_params=_cparams(("arbitrary",)),
        name="out",
    )(oa, ob, yc, cg, x2, gate, g_post.reshape(1, D_MODEL), w_glu, b_glu.reshape(1, C_WIDTH), w_out)


_W_IN_SIZES = (("aq", A_WIDTH), ("ak", A_KV_WIDTH), ("av", A_KV_WIDTH), ("ag", A_WIDTH), ("bq", B_QK_WIDTH),
               ("bk", B_QK_WIDTH), ("bv", B_WIDTH), ("blr", B_GATE_RANK), ("bg", B_WIDTH), ("cu", C_WIDTH),
               ("cg", C_WIDTH))
_W_IN_OFF = {}
_off = 0
for _name, _w in _W_IN_SIZES:
    _W_IN_OFF[_name] = _off
    _off += _w
W_IN_COLS = _off
WPREP_ROWS = 128


def _wprep_kernel(w_ref, wp_ref, wt_ref):
    w = w_ref[...]
    for name, (dst, width) in _PROJ_LAYOUT.items():
        src = _W_IN_OFF[name]
        if name == "blr":
            piece = jnp.concatenate([w[:, src:src + B_GATE_RANK],
                                     jnp.zeros((WPREP_ROWS, LR_PAD - B_GATE_RANK), F32)], axis=1)
        else:
            piece = w[:, src:src + width]
        wp_ref[0, :, dst:dst + width] = piece.astype(BF16)
    qv = jnp.concatenate([w[:, _W_IN_OFF["aq"]:_W_IN_OFF["aq"] + A_WIDTH] * (A_HEAD_DIM ** -0.5),
                          w[:, _W_IN_OFF["av"]:_W_IN_OFF["av"] + A_KV_WIDTH]], axis=1)
    wt_ref[0] = qv.T.astype(BF16)


def _wprep(w_in):
    layers = w_in.shape[0]
    return pl.pallas_call(
        _wprep_kernel,
        out_shape=[jax.ShapeDtypeStruct((layers, D_MODEL, PROJ_COLS_PADDED), BF16),
                   jax.ShapeDtypeStruct((layers, PROJ_T_ROWS, D_MODEL), BF16)],
        grid=(layers, D_MODEL // WPREP_ROWS),
        in_specs=[pl.BlockSpec((WPREP_ROWS, W_IN_COLS), lambda l, i: (l * (D_MODEL // WPREP_ROWS) + i, 0))],
        out_specs=[pl.BlockSpec((1, WPREP_ROWS, PROJ_COLS_PADDED), lambda l, i: (l, i, 0)),
                   pl.BlockSpec((1, PROJ_T_ROWS, WPREP_ROWS), lambda l, i: (l, 0, i))],
        compiler_params=_cparams(("arbitrary", "arbitrary")),
        name="wprep",
    )(w_in.reshape(layers * D_MODEL, W_IN_COLS))


def kernel(x, c, w_mod, b_mod, g_pre, g_post, w_in, attn_sinks, gla_w_alpha, gla_b_alpha, gla_norm_g,
           s5_a_re, s5_a_im, s5_log_dt, s5_b_re, s5_b_im, s5_c_re, s5_c_im, s5_d, s5_w_glu, s5_b_glu, w_out):
    layers = w_mod.shape[0]
    x2 = x.reshape(TOKENS, D_MODEL)
    bias = _attn_bias()
    mod = _mod(jnp.pad(c, ((0, 8 - BATCH), (0, 0))), w_mod, b_mod)[:, :BATCH]
    shift, scale, gate = (m.reshape(layers, BATCH, 1, D_MODEL) for m in jnp.split(mod, 3, axis=-1))
    w_perm, w_t = _wprep(w_in)
    w_alpha_pad = jnp.pad(gla_w_alpha, ((0, 0), (0, LR_PAD - B_GATE_RANK), (0, 0))).astype(BF16)
    s5_ops = _s5prep(s5_a_re, s5_a_im, s5_log_dt, s5_b_re, s5_b_im, s5_c_re, s5_c_im, s5_d)
    w_glu = s5_w_glu.astype(BF16)
    w_out_b = w_out.astype(BF16)
    for l in range(layers):
        ak, ag, bq, bk, bv, bg, cu, cg, log_a, aqt, avt = _proj(
            x2, scale[l], shift[l], g_pre[l], w_perm[l], w_t[l], w_alpha_pad[l], gla_b_alpha[l])
        o_a = _attn(attn_sinks[l], aqt, ak, avt, ag, bias)
        o_b = _gla(log_a, bq, bk, bv, bg, gla_norm_g[l])
        y_c = _s5(cu, *(op[l] for op in s5_ops))
        x2 = _out(o_a, o_b, y_c, cg, x2, gate[l], g_post[l], w_glu[l], s5_b_glu[l], w_out_b[l])
    return x2.reshape(x.shape)
```

```python
import math

import jax
import jax.numpy as jnp
import numpy as np
from jax import lax
from jax.experimental import pallas as pl
from jax.experimental.pallas import tpu as pltpu

F32 = jnp.float32
BF16 = jnp.bfloat16
HIGHEST = lax.Precision.HIGHEST

D_MODEL = 1024
BATCH = 4
SEQ = 4096
TOKENS = BATCH * SEQ
EPS = 1e-6

A_WIDTH = 1024
A_HEAD_DIM = 64
A_Q_HEADS = 16
A_KV_HEADS = 4
A_KV_WIDTH = A_KV_HEADS * A_HEAD_DIM
A_BLOCK = 128
WINDOW = 128

B_WIDTH = 512
B_HEADS = 4
B_DK = 64
B_DV = 128
B_QK_WIDTH = 256
B_GATE_RANK = 16
B_GATE_TAU = 16.0
GLA_BLOCK = 64
C_WIDTH = 512
C_GROUP_CH = 16
C_GROUPS = 32
C_STATE = 64
S5_CHUNK = 16
S5_NCHUNK = SEQ // S5_CHUNK
S5_TC = S5_CHUNK * C_GROUP_CH
S5_GB = 8
S5_NGB = C_GROUPS // S5_GB
S5_PAIRS_PER_GB = S5_GB // 2
S5_POW_ROWS = 24

LANES = 128
LR_PAD = LANES

V7X_VMEM_LIMIT = 56 * 1024 * 1024

PROJ_TM = 512
OUT_TM = 512

_W_IN_SIZES = (("aq", A_WIDTH), ("ak", A_KV_WIDTH), ("av", A_KV_WIDTH), ("ag", A_WIDTH), ("bq", B_QK_WIDTH),
               ("bk", B_QK_WIDTH), ("bv", B_WIDTH), ("blr", B_GATE_RANK), ("bg", B_WIDTH), ("cu", C_WIDTH),
               ("cg", C_WIDTH))
_W_IN_OFF = {}
_off = 0
for _name, _w in _W_IN_SIZES:
    _W_IN_OFF[_name] = (_off, _w)
    _off += _w
W_IN_COLS = _off
_PROJ_OUTS = (("ak", BF16), ("ag", BF16), ("bq", BF16), ("bk", BF16),
              ("bv", BF16), ("bg", BF16), ("cu", F32), ("cg", BF16))


def _silu(x):
    return x * jax.nn.sigmoid(x)


def _cparams(semantics):
    return pltpu.CompilerParams(dimension_semantics=semantics, vmem_limit_bytes=V7X_VMEM_LIMIT)


def _mod_kernel(c_ref, w_ref, b_ref, o_ref):
    c = c_ref[...]
    o_ref[0] = jnp.dot(_silu(c).astype(BF16), w_ref[0].astype(BF16), preferred_element_type=F32) + b_ref[0]


def _mod(c_pad, w_mod, b_mod):
    layers = w_mod.shape[0]
    n = 3 * D_MODEL
    tn = 768
    return pl.pallas_call(
        _mod_kernel,
        out_shape=jax.ShapeDtypeStruct((layers, 8, n), F32),
        grid=(layers, n // tn),
        in_specs=[pl.BlockSpec((8, D_MODEL), lambda l, j: (0, 0)),
                  pl.BlockSpec((1, D_MODEL, tn), lambda l, j: (l, 0, j)),
                  pl.BlockSpec((1, 1, tn), lambda l, j: (l, 0, j))],
        out_specs=pl.BlockSpec((1, 8, tn), lambda l, j: (l, 0, j)),
        compiler_params=_cparams(("arbitrary", "arbitrary")),
        name="mod",
    )(c_pad, w_mod, b_mod.reshape(layers, 1, n))


def _proj_kernel(x_ref, scale_ref, shift_ref, gpre_ref, wt_ref, walpha_ref, balpha_ref, *out_refs):
    x = x_ref[...]
    ms = jnp.mean(x * x, axis=-1, keepdims=True)
    y = x * lax.rsqrt(ms + EPS) * gpre_ref[...]
    h = (y * (1.0 + scale_ref[0]) + shift_ref[0]).astype(BF16)
    nt = (((1,), (1,)), ((), ()))

    def rows(name, width=None):
        off, w = _W_IN_OFF[name]
        return wt_ref[off:off + (width or w), :]

    for (name, _), o_ref in zip(_PROJ_OUTS, out_refs):
        o_ref[...] = lax.dot_general(h, rows(name), nt, preferred_element_type=F32).astype(o_ref.dtype)
    lr = lax.dot_general(h, rows("blr", LR_PAD), nt, preferred_element_type=F32).astype(BF16)
    logits = jnp.dot(lr, walpha_ref[...], preferred_element_type=F32) + balpha_ref[...]
    log_sig = jnp.minimum(logits, 0.0) - jnp.log(1.0 + jnp.exp(-jnp.abs(logits)))
    la_ref, qt_ref, vt_ref = out_refs[len(_PROJ_OUTS):]
    la_ref[...] = log_sig * (1.0 / B_GATE_TAU)
    qt = lax.dot_general(rows("aq"), h, nt, preferred_element_type=F32) * (A_HEAD_DIM ** -0.5)
    qt_ref[...] = qt.astype(qt_ref.dtype)
    vt_ref[...] = lax.dot_general(rows("av"), h, nt, preferred_element_type=F32).astype(vt_ref.dtype)


def _proj(x2, scale, shift, g_pre, w_t, w_alpha_pad, b_alpha):
    tm = PROJ_TM
    steps_per_batch = SEQ // tm
    row = lambda i: (i, 0)
    col = lambda i: (0, i)
    per_batch = lambda i: (i // steps_per_batch, 0, 0)
    const = lambda i: (0, 0)
    out_shape = [jax.ShapeDtypeStruct((TOKENS, _W_IN_OFF[n][1]), dt) for n, dt in _PROJ_OUTS]
    out_specs = [pl.BlockSpec((tm, _W_IN_OFF[n][1]), row) for n, _ in _PROJ_OUTS]
    out_shape += [jax.ShapeDtypeStruct((TOKENS, B_QK_WIDTH), F32),
                  jax.ShapeDtypeStruct((A_WIDTH, TOKENS), BF16),
                  jax.ShapeDtypeStruct((A_KV_WIDTH, TOKENS), BF16)]
    out_specs += [pl.BlockSpec((tm, B_QK_WIDTH), row),
                  pl.BlockSpec((A_WIDTH, tm), col),
                  pl.BlockSpec((A_KV_WIDTH, tm), col)]
    return pl.pallas_call(
        _proj_kernel,
        out_shape=out_shape,
        grid=(TOKENS // tm,),
        in_specs=[pl.BlockSpec((tm, D_MODEL), row),
                  pl.BlockSpec((1, 1, D_MODEL), per_batch),
                  pl.BlockSpec((1, 1, D_MODEL), per_batch),
                  pl.BlockSpec((1, D_MODEL), const),
                  pl.BlockSpec((W_IN_COLS, D_MODEL), const),
                  pl.BlockSpec((LR_PAD, B_QK_WIDTH), const),
                  pl.BlockSpec((1, B_QK_WIDTH), const)],
        out_specs=out_specs,
        compiler_params=_cparams(("arbitrary",)),
        name="proj",
    )(x2, scale, shift, g_pre.reshape(1, D_MODEL), w_t, w_alpha_pad, b_alpha.reshape(1, B_QK_WIDTH))


def _attn_bias():
    j = np.arange(A_BLOCK)[:, None]
    i = np.arange(A_BLOCK)[None, :]
    dist = np.where(j > i, i + A_BLOCK - j, i - j).astype(np.float32)
    slopes = np.exp2(-8.0 * np.arange(1, A_Q_HEADS + 1, dtype=np.float32) / A_Q_HEADS).astype(np.float32)
    bias = -slopes[:, None, None] * dist[None]
    first = np.where((j > i)[None], -np.inf, bias).astype(np.float32)
    return jnp.asarray(np.stack([bias, first]))


def _attn_kernel(sink_ref, qt_ref, kp_ref, kc_ref, vtp_ref, vtc_ref, g_ref, bias_ref, o_ref):
    n = pl.program_id(1)
    table = jnp.where(n == 0, 1, 0)
    kj = lax.broadcasted_iota(jnp.int32, (A_BLOCK, A_BLOCK), 0)
    qi = lax.broadcasted_iota(jnp.int32, (A_BLOCK, A_BLOCK), 1)
    from_prev = kj > qi
    zero_rows = jnp.zeros((A_HEAD_DIM, A_BLOCK), BF16)
    group = A_Q_HEADS // A_KV_HEADS

    def scores(hd):
        kvh = hd // group
        sl = slice(LANES * (kvh // 2), LANES * (kvh // 2 + 1))
        qh = qt_ref[A_HEAD_DIM * hd:A_HEAD_DIM * (hd + 1), :]
        qsel = jnp.concatenate([qh, zero_rows] if kvh % 2 == 0 else [zero_rows, qh], axis=0)
        return (jnp.dot(kp_ref[:, sl], qsel, preferred_element_type=F32),
                jnp.dot(kc_ref[:, sl], qsel, preferred_element_type=F32))

    def attend(hd, s_prev, s_cur):
        kvh = hd // group
        vrows = slice(A_HEAD_DIM * kvh, A_HEAD_DIM * (kvh + 1))
        v_both = jnp.concatenate([vtp_ref[vrows, :], vtc_ref[vrows, :]], axis=1)
        s = jnp.where(from_prev, s_prev, s_cur) + bias_ref[table, hd]
        sink = sink_ref[hd]
        m = jnp.maximum(jnp.max(s, axis=0, keepdims=True), sink)
        p = jnp.exp(s - m)
        den = jnp.sum(p, axis=0, keepdims=True) + jnp.exp(sink - m)
        p_both = jnp.concatenate([jnp.where(from_prev, p, 0.0), jnp.where(from_prev, 0.0, p)],
                                 axis=0).astype(BF16)
        return jnp.dot(v_both, p_both, preferred_element_type=F32) / den

    pending = [scores(hd) for hd in range(A_Q_HEADS)]
    outs = {}
    for hd in range(A_Q_HEADS):
        outs[hd] = attend(hd, *pending[hd])
        if hd % 2 == 1:
            qsl = slice(LANES * (hd // 2), LANES * (hd // 2 + 1))
            o_pair = jnp.concatenate([outs.pop(hd - 1), outs.pop(hd)], axis=0).T
            gate = g_ref[:, qsl].astype(F32)
            o_ref[:, qsl] = (o_pair * _silu(gate)).astype(o_ref.dtype)


def _attn(sinks, qt, k, vt, ag, bias):
    nb = SEQ // A_BLOCK
    cur = lambda b, n: (b * nb + n, 0)
    prev = lambda b, n: (b * nb + jnp.maximum(n - 1, 0), 0)
    cur_t = lambda b, n: (0, b * nb + n)
    prev_t = lambda b, n: (0, b * nb + jnp.maximum(n - 1, 0))
    return pl.pallas_call(
        _attn_kernel,
        out_shape=jax.ShapeDtypeStruct((TOKENS, A_WIDTH), BF16),
        grid=(BATCH, nb),
        in_specs=[pl.BlockSpec(memory_space=pltpu.SMEM),
                  pl.BlockSpec((A_WIDTH, A_BLOCK), cur_t),
                  pl.BlockSpec((A_BLOCK, A_KV_WIDTH), prev),
                  pl.BlockSpec((A_BLOCK, A_KV_WIDTH), cur),
                  pl.BlockSpec((A_KV_WIDTH, A_BLOCK), prev_t),
                  pl.BlockSpec((A_KV_WIDTH, A_BLOCK), cur_t),
                  pl.BlockSpec((A_BLOCK, A_WIDTH), cur),
                  pl.BlockSpec((2, A_Q_HEADS, A_BLOCK, A_BLOCK), lambda b, n: (0, 0, 0, 0))],
        out_specs=pl.BlockSpec((A_BLOCK, A_WIDTH), cur),
        compiler_params=_cparams(("arbitrary", "arbitrary")),
        name="attn",
    )(sinks, qt, k, k, vt, vt, ag, bias)


def _gla_kernel(la_ref, q_ref, k_ref, v_ref, g_ref, gn_ref, o_ref, st_ref):
    cb = GLA_BLOCK

    @pl.when(pl.program_id(0) == 0)
    def _():
        st_ref[...] = jnp.zeros_like(st_ref)

    r = lax.broadcasted_iota(jnp.int32, (cb, cb), 0)
    c = lax.broadcasted_iota(jnp.int32, (cb, cb), 1)
    tri = (c <= r).astype(F32)
    lane = lax.broadcasted_iota(jnp.int32, (cb, B_QK_WIDTH), 1)
    head_masks = [(lane >= B_DK * h) & (lane < B_DK * (h + 1)) for h in range(B_HEADS)]
    rr = lax.broadcasted_iota(jnp.int32, (B_HEADS * cb, cb), 0)
    cc = lax.broadcasted_iota(jnp.int32, (B_HEADS * cb, cb), 1)
    causal = cc <= (rr & (cb - 1))
    nt = (((1,), (1,)), ((), ()))
    tn = (((0,), (0,)), ((), ()))
    seqs = range(BATCH)
    bcs = [jnp.dot(tri, la_ref[b], preferred_element_type=F32, precision=HIGHEST) for b in seqs]
    qsts, kss, ksts, decs = [], [], [], []
    for b in seqs:
        bc = bcs[b]
        bl = bc[cb - 1:cb, :]
        q = q_ref[b].astype(F32) * (B_DK ** -0.5)
        k = k_ref[b].astype(F32)
        qs = q * jnp.exp(bc)
        kh = k * jnp.exp(bl - bc)
        kss.append((k * jnp.exp(-bc)).astype(BF16))
        decs.append(jnp.exp(bl))
        qsts.append(jnp.concatenate([jnp.where(m, qs, 0.0) for m in head_masks], axis=0).astype(BF16))
        ksts.append(jnp.concatenate([jnp.where(m, kh, 0.0) for m in head_masks], axis=0).astype(BF16))
    sts = [st_ref[b] for b in seqs]
    vs = [v_ref[b] for b in seqs]
    a_alls = [lax.dot_general(qsts[b], kss[b], nt, preferred_element_type=F32) for b in seqs]
    oi_alls = [lax.dot_general(qsts[b], sts[b].astype(BF16), nt, preferred_element_type=F32) for b in seqs]
    for b in seqs:
        vst = jnp.concatenate([vs[b][:, B_DV * h:B_DV * (h + 1)] for h in range(B_HEADS)], axis=0)
        upd = lax.dot_general(vst, ksts[b], tn, preferred_element_type=F32)
        st_ref[b] = sts[b] * decs[b] + upd
    o_hs = {}
    for b in seqs:
        a_all = jnp.where(causal, a_alls[b], 0.0).astype(BF16)
        for h in range(B_HEADS):
            o_hs[b, h] = (jnp.dot(a_all[cb * h:cb * (h + 1)], vs[b][:, B_DV * h:B_DV * (h + 1)],
                                  preferred_element_type=F32) + oi_alls[b][cb * h:cb * (h + 1)])
    for b in seqs:
        for h in range(B_HEADS):
            vsl = slice(B_DV * h, B_DV * (h + 1))
            o_h = o_hs[b, h]
            ms = jnp.mean(o_h * o_h, axis=-1, keepdims=True)
            o_n = o_h * lax.rsqrt(ms + EPS) * gn_ref[:, vsl]
            gate = g_ref[b, :, vsl].astype(F32)
            o_ref[b, :, vsl] = (o_n * _silu(gate)).astype(o_ref.dtype)


def _gla(log_a, bq, bk, bv, bg, g_gla):
    cb = GLA_BLOCK
    blk = lambda w: pl.BlockSpec((BATCH, cb, w), lambda i: (0, i, 0))
    r3 = lambda a: a.reshape(BATCH, SEQ, a.shape[-1])
    out = pl.pallas_call(
        _gla_kernel,
        out_shape=jax.ShapeDtypeStruct((BATCH, SEQ, B_WIDTH), BF16),
        grid=(SEQ // cb,),
        in_specs=[blk(B_QK_WIDTH), blk(B_QK_WIDTH), blk(B_QK_WIDTH), blk(B_WIDTH), blk(B_WIDTH),
                  pl.BlockSpec((1, B_WIDTH), lambda i: (0, 0))],
        out_specs=blk(B_WIDTH),
        scratch_shapes=[pltpu.VMEM((BATCH, B_DV, B_QK_WIDTH), F32)],
        compiler_params=_cparams(("arbitrary",)),
        name="gla",
    )(r3(log_a), r3(bq), r3(bk), r3(bv), r3(bg), g_gla.reshape(1, B_WIDTH))
    return out.reshape(TOKENS, B_WIDTH)


def _s5prep_kernel(ar_ref, ai_ref, ldt_ref, bre_ref, bim_ref, btre_ref, btim_ref, cre_ref, cim_ref,
                   mt_ref, wet_ref, wyt_ref, are_ref, aim_ref):
    p = C_STATE
    wet_re, wet_im, wyt_rows, a_re, a_im = [], [], [], [], []
    for g in range(2):
        kk, e_re, e_im, y_re, y_im, p_re, p_im = _s5_discretise(
            ar_ref[g], ai_ref[g], ldt_ref[g], bre_ref[g], bim_ref[g], btre_ref[g], btim_ref[g],
            cre_ref[g], cim_ref[g])
        pieces = [kk] + [jnp.concatenate([jnp.zeros((C_GROUP_CH * s, C_GROUP_CH), F32),
                                          kk[:S5_TC - C_GROUP_CH * s]], axis=0) for s in range(1, S5_CHUNK)]
        mt_ref[0, g] = jnp.concatenate(pieces, axis=1).astype(BF16)
        e_t = jnp.concatenate([e_re, e_im], axis=1).T
        zero = jnp.zeros((p, S5_TC), F32)
        wet_re.append(jnp.concatenate([e_t[:p], zero] if g == 0 else [zero, e_t[:p]], axis=1))
        wet_im.append(jnp.concatenate([e_t[p:], zero] if g == 0 else [zero, e_t[p:]], axis=1))
        zero = jnp.zeros((S5_TC, p), F32)
        wyt_rows.append(jnp.concatenate([y_re, zero, -y_im, zero] if g == 0 else [zero, y_re, zero, -y_im], axis=1))
        a_re.append(p_re)
        a_im.append(p_im)
    wet_ref[0] = jnp.concatenate(wet_re + wet_im, axis=0).astype(BF16)
    wyt_ref[0] = jnp.concatenate(wyt_rows, axis=0).astype(BF16)
    are_ref[0] = jnp.broadcast_to(jnp.concatenate(a_re, axis=1), (8, 2 * p))
    aim_ref[0] = jnp.broadcast_to(jnp.concatenate(a_im, axis=1), (8, 2 * p))


def _s5_discretise(ar, ai, ldt, b_re, b_im, bt_re16, bt_im16, c_re16, c_im16):
    dt = jnp.exp(ldt)

    def cmul(xr, xi, yr, yi):
        return xr * yr - xi * yi, xr * yi + xi * yr

    kf = lax.broadcasted_iota(jnp.int32, (S5_POW_ROWS, 1), 0).astype(F32)
    mag = jnp.exp(kf * (ar * dt))
    ang = kf * (ai * dt)
    pw_re, pw_im = mag * jnp.cos(ang), mag * jnp.sin(ang)
    abar_re, abar_im = pw_re[1:2], pw_im[1:2]
    den = ar * ar + ai * ai
    num_re = abar_re - 1.0
    f_re = (num_re * ar + abar_im * ai) / den
    f_im = (abar_im * ar - num_re * ai) / den
    g_re, g_im = cmul(pw_re, pw_im, f_re, f_im)

    def pick(which, xr, xi):
        rep = lambda x: jnp.concatenate(
            [jnp.broadcast_to(x[which(i):which(i) + 1], (C_GROUP_CH, C_STATE)) for i in range(S5_CHUNK)], axis=0)
        return rep(xr), rep(xi)

    tile16 = lambda a: jnp.concatenate([a] * S5_CHUNK, axis=0)
    ct_re, ct_im = tile16(c_re16), tile16(c_im16)
    bt_re, bt_im = tile16(bt_re16), tile16(bt_im16)

    w_re, w_im = cmul(*pick(lambda i: i, g_re, g_im), ct_re, ct_im)
    kk = (jnp.dot(w_re, b_re, preferred_element_type=F32, precision=HIGHEST)
          - jnp.dot(w_im, b_im, preferred_element_type=F32, precision=HIGHEST))
    e_re, e_im = cmul(*pick(lambda i: S5_CHUNK - 1 - i, g_re, g_im), bt_re, bt_im)
    y_re, y_im = cmul(*pick(lambda i: i + 1, pw_re, pw_im), ct_re, ct_im)
    return kk, e_re, e_im, y_re, y_im, pw_re[S5_CHUNK:S5_CHUNK + 1], pw_im[S5_CHUNK:S5_CHUNK + 1]


def _s5prep(a_re, a_im, log_dt, b_re, b_im, c_re, c_im, d):
    p, ch = C_STATE, C_GROUP_CH
    layers = a_re.shape[0]
    g = layers * C_GROUPS
    npair = g // 2
    flat = lambda a: a.reshape(g, *a.shape[2:])
    a_re, a_im, log_dt, b_re, b_im, c_re, c_im = map(flat, (a_re, a_im, log_dt, b_re, b_im, c_re, c_im))
    row = lambda a: a.reshape(g, 1, p)
    ldt = jnp.broadcast_to(log_dt[:, None, None], (g, 1, p))
    b_t = lambda a: jnp.swapaxes(a, 1, 2)
    spec = lambda s1, s2: pl.BlockSpec((2, s1, s2), lambda i: (i, 0, 0))
    mt, wet, wyt, pw_re, pw_im = pl.pallas_call(
        _s5prep_kernel,
        out_shape=[jax.ShapeDtypeStruct((npair, 2, S5_TC, S5_TC), BF16),
                   jax.ShapeDtypeStruct((npair, 4 * p, 2 * S5_TC), BF16),
                   jax.ShapeDtypeStruct((npair, 2 * S5_TC, 4 * p), BF16),
                   jax.ShapeDtypeStruct((npair, 8, 2 * p), F32),
                   jax.ShapeDtypeStruct((npair, 8, 2 * p), F32)],
        grid=(npair,),
        in_specs=[spec(1, p), spec(1, p), spec(1, p), spec(p, ch), spec(p, ch),
                  spec(ch, p), spec(ch, p), spec(ch, p), spec(ch, p)],
        out_specs=[pl.BlockSpec((1, 2, S5_TC, S5_TC), lambda i: (i, 0, 0, 0)),
                   pl.BlockSpec((1, 4 * p, 2 * S5_TC), lambda i: (i, 0, 0)),
                   pl.BlockSpec((1, 2 * S5_TC, 4 * p), lambda i: (i, 0, 0)),
                   pl.BlockSpec((1, 8, 2 * p), lambda i: (i, 0, 0)),
                   pl.BlockSpec((1, 8, 2 * p), lambda i: (i, 0, 0))],
        compiler_params=_cparams(("arbitrary",)),
        name="s5prep",
    )(row(a_re), row(a_im), ldt, b_re, b_im, b_t(b_re), b_t(b_im), c_re, c_im)
    by_gb = lambda a: a.reshape(layers, S5_NGB, S5_PAIRS_PER_GB, *a.shape[1:])
    return (mt.reshape(layers, S5_NGB, S5_GB, S5_TC, S5_TC), by_gb(wet), by_gb(wyt), by_gb(pw_re), by_gb(pw_im),
            d.reshape(layers, 1, C_WIDTH))


def _s5_kernel(u_ref, mt_ref, wet_ref, wyt_ref, are_ref, aim_ref, d_ref, y_ref,
               ut_ref, yt_ref, ere_ref, eim_ref, hre_ref, him_ref):
    nck, t_len, ch = S5_NCHUNK, S5_CHUNK, C_GROUP_CH
    nt = (((1,), (1,)), ((), ()))
    for t in range(t_len):
        xt = u_ref[pl.ds(t, nck, stride=t_len), :].T
        for g in range(S5_GB):
            ut_ref[g, ch * t:ch * (t + 1), :] = xt[ch * g:ch * (g + 1), :]
    for j in range(S5_PAIRS_PER_GB):
        u0 = ut_ref[2 * j].astype(BF16)
        u1 = ut_ref[2 * j + 1].astype(BF16)
        et = jnp.dot(wet_ref[0, j], jnp.concatenate([u0, u1], axis=0), preferred_element_type=F32)
        e = et.T
        ere_ref[:, LANES * j:LANES * (j + 1)] = e[:, :LANES]
        eim_ref[:, LANES * j:LANES * (j + 1)] = e[:, LANES:]
        yt_ref[2 * j] = jnp.dot(mt_ref[0, 2 * j], u0, preferred_element_type=F32)
        yt_ref[2 * j + 1] = jnp.dot(mt_ref[0, 2 * j + 1], u1, preferred_element_type=F32)

    a_re = jnp.concatenate([are_ref[0, j, 0:1, :] for j in range(S5_PAIRS_PER_GB)], axis=1)
    a_im = jnp.concatenate([aim_ref[0, j, 0:1, :] for j in range(S5_PAIRS_PER_GB)], axis=1)

    def body(i, carry):
        h_re, h_im = carry
        hre_ref[pl.ds(i, 1), :] = h_re
        him_ref[pl.ds(i, 1), :] = h_im
        e_re = ere_ref[pl.ds(i, 1), :]
        e_im = eim_ref[pl.ds(i, 1), :]
        return a_re * h_re - a_im * h_im + e_re, a_re * h_im + a_im * h_re + e_im

    zero = jnp.zeros((1, S5_GB * C_STATE), F32)
    lax.fori_loop(0, nck, body, (zero, zero))

    for j in range(S5_PAIRS_PER_GB):
        sl = slice(LANES * j, LANES * (j + 1))
        hp = jnp.concatenate([hre_ref[:, sl], him_ref[:, sl]], axis=1).astype(BF16)
        yi = lax.dot_general(wyt_ref[0, j], hp, nt, preferred_element_type=F32)
        yt_ref[2 * j] += yi[:S5_TC]
        yt_ref[2 * j + 1] += yi[S5_TC:]
    for t in range(t_len):
        ytt = jnp.concatenate([yt_ref[g, ch * t:ch * (t + 1), :] for g in range(S5_GB)], axis=0)
        rows = pl.ds(t, nck, stride=t_len)
        y_ref[rows, :] = ytt.T + d_ref[...] * u_ref[rows, :]


def _s5(cu, mt, wet, wyt, pw_re, pw_im, d_row):
    p4 = 4 * C_STATE
    tok = pl.BlockSpec((SEQ, LANES), lambda gb, b: (b, gb))
    per_gb = lambda *s: pl.BlockSpec((1,) + s, lambda gb, b: (gb,) + (0,) * len(s))
    state = pltpu.VMEM((S5_NCHUNK, S5_GB * C_STATE), F32)
    return pl.pallas_call(
        _s5_kernel,
        out_shape=jax.ShapeDtypeStruct((TOKENS, C_WIDTH), F32),
        grid=(S5_NGB, BATCH),
        in_specs=[tok,
                  per_gb(S5_GB, S5_TC, S5_TC),
                  per_gb(S5_PAIRS_PER_GB, p4, 2 * S5_TC),
                  per_gb(S5_PAIRS_PER_GB, 2 * S5_TC, p4),
                  per_gb(S5_PAIRS_PER_GB, 8, 2 * C_STATE), per_gb(S5_PAIRS_PER_GB, 8, 2 * C_STATE),
                  pl.BlockSpec((1, LANES), lambda gb, b: (0, gb))],
        out_specs=tok,
        scratch_shapes=[pltpu.VMEM((S5_GB, S5_TC, S5_NCHUNK), F32),
                        pltpu.VMEM((S5_GB, S5_TC, S5_NCHUNK), F32),
                        state, state, state, state],
        compiler_params=_cparams(("arbitrary", "arbitrary")),
        name="s5",
    )(cu, mt, wet, wyt, pw_re, pw_im, d_row)


def _gelu_tanh(x):
    return 0.5 * x * (1.0 + jnp.tanh(math.sqrt(2.0 / math.pi) * (x + 0.044715 * (x * x * x))))


def _out_kernel(oa_ref, ob_ref, yc_ref, cg_ref, x_ref, gate_ref, gpost_ref, wglu_ref, bglu_ref, wout_ref, o_ref):
    y = _gelu_tanh(yc_ref[...])
    z = jnp.dot(y.astype(BF16), wglu_ref[...], preferred_element_type=F32) + bglu_ref[...]
    y = y * jax.nn.sigmoid(z)
    oc = (y * _silu(cg_ref[...].astype(F32))).astype(BF16)
    mix = jnp.concatenate([oa_ref[...], ob_ref[...], oc], axis=1)
    acc = jnp.dot(mix, wout_ref[...], preferred_element_type=F32)
    ms = jnp.mean(acc * acc, axis=-1, keepdims=True)
    out = acc * lax.rsqrt(ms + EPS) * gpost_ref[...]
    o_ref[...] = x_ref[...] + gate_ref[0] * out


def _out(oa, ob, yc, cg, x2, gate, g_post, w_glu, b_glu, w_out):
    tm = OUT_TM
    steps_per_batch = SEQ // tm
    row = lambda i: (i, 0)
    const = lambda i: (0, 0)
    return pl.pallas_call(
        _out_kernel,
        out_shape=jax.ShapeDtypeStruct((TOKENS, D_MODEL), F32),
        grid=(TOKENS // tm,),
        in_specs=[pl.BlockSpec((tm, A_WIDTH), row),
                  pl.BlockSpec((tm, B_WIDTH), row),
                  pl.BlockSpec((tm, C_WIDTH), row),
                  pl.BlockSpec((tm, C_WIDTH), row),
                  pl.BlockSpec((tm, D_MODEL), row),
                  pl.BlockSpec((1, 1, D_MODEL), lambda i: (i // steps_per_batch, 0, 0)),
                  pl.BlockSpec((1, D_MODEL), const),
                  pl.BlockSpec((C_WIDTH, C_WIDTH), const),
                  pl.BlockSpec((1, C_WIDTH), const),
                  pl.BlockSpec((2 * D_MODEL, D_MODEL), const)],
        out_specs=pl.BlockSpec((tm, D_MODEL), row),
        compiler_params=_cparams(("arbitrary",)),
        name="out",
    )(oa, ob, yc, cg, x2, gate, g_post.reshape(1, D_MODEL), w_glu, b_glu.reshape(1, C_WIDTH), w_out)


def kernel(x, c, w_mod, b_mod, g_pre, g_post, w_in, attn_sinks, gla_w_alpha, gla_b_alpha, gla_norm_g,
           s5_a_re, s5_a_im, s5_log_dt, s5_b_re, s5_b_im, s5_c_re, s5_c_im, s5_d, s5_w_glu, s5_b_glu, w_out):
    layers = w_mod.shape[0]
    x2 = x.reshape(TOKENS, D_MODEL)
    bias = _attn_bias()
    mod = _mod(jnp.pad(c, ((0, 8 - BATCH), (0, 0))), w_mod, b_mod)[:, :BATCH]
    shift, scale, gate = (m.reshape(layers, BATCH, 1, D_MODEL) for m in jnp.split(mod, 3, axis=-1))
    w_t = jnp.swapaxes(w_in, 1, 2).astype(BF16)
    w_alpha_pad = jnp.pad(gla_w_alpha, ((0, 0), (0, LR_PAD - B_GATE_RANK), (0, 0))).astype(BF16)
    s5_ops = _s5prep(s5_a_re, s5_a_im, s5_log_dt, s5_b_re, s5_b_im, s5_c_re, s5_c_im, s5_d)
    w_glu = s5_w_glu.astype(BF16)
    w_out_b = w_out.astype(BF16)
    for l in range(layers):
        ak, ag, bq, bk, bv, bg, cu, cg, log_a, aqt, avt = _proj(
            x2, scale[l], shift[l], g_pre[l], w_t[l], w_alpha_pad[l], gla_b_alpha[l])
        o_a = _attn(attn_sinks[l], aqt, ak, avt, ag, bias)
        o_b = _gla(log_a, bq, bk, bv, bg, gla_norm_g[l])
        y_c = _s5(cu, *(op[l] for op in s5_ops))
        x2 = _out(o_a, o_b, y_c, cg, x2, gate[l], g_post[l], w_glu[l], s5_b_glu[l], w_out_b[l])
    return x2.reshape(x.shape)
```

```python
import math

import jax
import jax.numpy as jnp
import numpy as np
from jax import lax
from jax.experimental import pallas as pl
from jax.experimental.pallas import tpu as pltpu

F32 = jnp.float32
BF16 = jnp.bfloat16
HIGHEST = lax.Precision.HIGHEST

D_MODEL = 1024
BATCH = 4
SEQ = 4096
TOKENS = BATCH * SEQ
EPS = 1e-6

A_WIDTH = 1024
A_HEAD_DIM = 64
A_Q_HEADS = 16
A_KV_HEADS = 4
A_KV_WIDTH = A_KV_HEADS * A_HEAD_DIM
A_BLOCK = 128
WINDOW = 128

B_WIDTH = 512
B_HEADS = 4
B_DK = 64
B_DV = 128
B_QK_WIDTH = 256
B_GATE_RANK = 16
B_GATE_TAU = 16.0
GLA_BLOCK = 64
GLA_SUB = 4
ATTN_QB = 4
C_WIDTH = 512
C_GROUP_CH = 16
C_GROUPS = 32
C_STATE = 64
S5_CHUNK = 16
S5_NCHUNK = SEQ // S5_CHUNK
S5_TC = S5_CHUNK * C_GROUP_CH
S5_GB = 8
S5_NGB = C_GROUPS // S5_GB
S5_PAIRS_PER_GB = S5_GB // 2
S5_POW_ROWS = 24

LANES = 128
LR_PAD = LANES

V7X_VMEM_LIMIT = 56 * 1024 * 1024

PROJ_TM = 512
OUT_TM = 512

_W_IN_SIZES = (("aq", A_WIDTH), ("ak", A_KV_WIDTH), ("av", A_KV_WIDTH), ("ag", A_WIDTH), ("bq", B_QK_WIDTH),
               ("bk", B_QK_WIDTH), ("bv", B_WIDTH), ("blr", B_GATE_RANK), ("bg", B_WIDTH), ("cu", C_WIDTH),
               ("cg", C_WIDTH))
_W_IN_OFF = {}
_off = 0
for _name, _w in _W_IN_SIZES:
    _W_IN_OFF[_name] = (_off, _w)
    _off += _w
W_IN_COLS = _off
_PROJ_OUTS = (("ak", BF16), ("ag", BF16), ("bq", BF16), ("bk", BF16),
              ("bv", BF16), ("bg", BF16), ("cu", F32), ("cg", BF16))


def _silu(x):
    return x * jax.nn.sigmoid(x)


def _cparams(semantics):
    return pltpu.CompilerParams(dimension_semantics=semantics, vmem_limit_bytes=V7X_VMEM_LIMIT)


def _mod_kernel(c_ref, w_ref, b_ref, o_ref):
    c = c_ref[...]
    o_ref[0] = jnp.dot(_silu(c).astype(BF16), w_ref[0].astype(BF16), preferred_element_type=F32) + b_ref[0]


def _mod(c_pad, w_mod, b_mod):
    layers = w_mod.shape[0]
    n = 3 * D_MODEL
    tn = 768
    return pl.pallas_call(
        _mod_kernel,
        out_shape=jax.ShapeDtypeStruct((layers, 8, n), F32),
        grid=(layers, n // tn),
        in_specs=[pl.BlockSpec((8, D_MODEL), lambda l, j: (0, 0)),
                  pl.BlockSpec((1, D_MODEL, tn), lambda l, j: (l, 0, j)),
                  pl.BlockSpec((1, 1, tn), lambda l, j: (l, 0, j))],
        out_specs=pl.BlockSpec((1, 8, tn), lambda l, j: (l, 0, j)),
        compiler_params=_cparams(("arbitrary", "arbitrary")),
        name="mod",
    )(c_pad, w_mod, b_mod.reshape(layers, 1, n))


def _proj_kernel(x_ref, scale_ref, shift_ref, gpre_ref, wt_ref, walpha_ref, balpha_ref, *out_refs):
    x = x_ref[...]
    ms = jnp.mean(x * x, axis=-1, keepdims=True)
    y = x * lax.rsqrt(ms + EPS) * gpre_ref[...]
    h = (y * (1.0 + scale_ref[0]) + shift_ref[0]).astype(BF16)
    nt = (((1,), (1,)), ((), ()))

    def rows(name, width=None):
        off, w = _W_IN_OFF[name]
        return wt_ref[off:off + (width or w), :]

    for (name, _), o_ref in zip(_PROJ_OUTS, out_refs):
        o_ref[...] = lax.dot_general(h, rows(name), nt, preferred_element_type=F32).astype(o_ref.dtype)
    lr = lax.dot_general(h, rows("blr", LR_PAD), nt, preferred_element_type=F32).astype(BF16)
    logits = jnp.dot(lr, walpha_ref[...], preferred_element_type=F32) + balpha_ref[...]
    log_sig = jnp.minimum(logits, 0.0) - jnp.log(1.0 + jnp.exp(-jnp.abs(logits)))
    la_ref, qt_ref, vt_ref = out_refs[len(_PROJ_OUTS):]
    la_ref[...] = log_sig * (1.0 / B_GATE_TAU)
    qt = lax.dot_general(rows("aq"), h, nt, preferred_element_type=F32) * (A_HEAD_DIM ** -0.5)
    qt_ref[...] = qt.astype(qt_ref.dtype)
    vt_ref[...] = lax.dot_general(rows("av"), h, nt, preferred_element_type=F32).astype(vt_ref.dtype)


def _proj(x2, scale, shift, g_pre, w_t, w_alpha_pad, b_alpha):
    tm = PROJ_TM
    steps_per_batch = SEQ // tm
    row = lambda i: (i, 0)
    col = lambda i: (0, i)
    per_batch = lambda i: (i // steps_per_batch, 0, 0)
    const = lambda i: (0, 0)
    out_shape = [jax.ShapeDtypeStruct((TOKENS, _W_IN_OFF[n][1]), dt) for n, dt in _PROJ_OUTS]
    out_specs = [pl.BlockSpec((tm, _W_IN_OFF[n][1]), row) for n, _ in _PROJ_OUTS]
    out_shape += [jax.ShapeDtypeStruct((TOKENS, B_QK_WIDTH), F32),
                  jax.ShapeDtypeStruct((A_WIDTH, TOKENS), BF16),
                  jax.ShapeDtypeStruct((A_KV_WIDTH, TOKENS), BF16)]
    out_specs += [pl.BlockSpec((tm, B_QK_WIDTH), row),
                  pl.BlockSpec((A_WIDTH, tm), col),
                  pl.BlockSpec((A_KV_WIDTH, tm), col)]
    return pl.pallas_call(
        _proj_kernel,
        out_shape=out_shape,
        grid=(TOKENS // tm,),
        in_specs=[pl.BlockSpec((tm, D_MODEL), row),
                  pl.BlockSpec((1, 1, D_MODEL), per_batch),
                  pl.BlockSpec((1, 1, D_MODEL), per_batch),
                  pl.BlockSpec((1, D_MODEL), const),
                  pl.BlockSpec((W_IN_COLS, D_MODEL), const),
                  pl.BlockSpec((LR_PAD, B_QK_WIDTH), const),
                  pl.BlockSpec((1, B_QK_WIDTH), const)],
        out_specs=out_specs,
        compiler_params=_cparams(("arbitrary",)),
        name="proj",
    )(x2, scale, shift, g_pre.reshape(1, D_MODEL), w_t, w_alpha_pad, b_alpha.reshape(1, B_QK_WIDTH))


def _attn_bias():
    j = np.arange(A_BLOCK)[:, None]
    i = np.arange(A_BLOCK)[None, :]
    dist = np.where(j > i, i + A_BLOCK - j, i - j).astype(np.float32)
    slopes = np.exp2(-8.0 * np.arange(1, A_Q_HEADS + 1, dtype=np.float32) / A_Q_HEADS).astype(np.float32)
    bias = -slopes[:, None, None] * dist[None]
    first = np.where((j > i)[None], -np.inf, bias).astype(np.float32)
    return jnp.asarray(np.stack([bias, first]))


def _attn_kernel(sink_ref, qt_ref, kp_ref, kc_ref, vtp_ref, vtc_ref, g_ref, bias_ref, o_ref):
    first_step = pl.program_id(1) == 0
    kj = lax.broadcasted_iota(jnp.int32, (A_BLOCK, A_BLOCK), 0)
    qi = lax.broadcasted_iota(jnp.int32, (A_BLOCK, A_BLOCK), 1)
    from_prev = kj > qi
    zero_rows = jnp.zeros((A_HEAD_DIM, A_BLOCK), BF16)
    group = A_Q_HEADS // A_KV_HEADS
    blk = lambda j: slice(A_BLOCK * j, A_BLOCK * (j + 1))

    def keys(j, sl):
        prev = kp_ref[:, sl] if j == 0 else kc_ref[blk(j - 1), sl]
        return prev, kc_ref[blk(j), sl]

    def values(j, rows):
        prev = vtp_ref[rows, :] if j == 0 else vtc_ref[rows, blk(j - 1)]
        return jnp.concatenate([prev, vtc_ref[rows, blk(j)]], axis=1)

    def scores(j, hd):
        kvh = hd // group
        sl = slice(LANES * (kvh // 2), LANES * (kvh // 2 + 1))
        qh = qt_ref[A_HEAD_DIM * hd:A_HEAD_DIM * (hd + 1), blk(j)]
        qsel = jnp.concatenate([qh, zero_rows] if kvh % 2 == 0 else [zero_rows, qh], axis=0)
        k_prev, k_cur = keys(j, sl)
        return (jnp.dot(k_prev, qsel, preferred_element_type=F32),
                jnp.dot(k_cur, qsel, preferred_element_type=F32))

    def attend(j, hd, s_prev, s_cur):
        kvh = hd // group
        v_both = values(j, slice(A_HEAD_DIM * kvh, A_HEAD_DIM * (kvh + 1)))
        table = jnp.where(first_step, 1, 0) if j == 0 else 0
        s = jnp.where(from_prev, s_prev, s_cur) + bias_ref[table, hd]
        sink = sink_ref[hd]
        m = jnp.maximum(jnp.max(s, axis=0, keepdims=True), sink)
        p = jnp.exp(s - m)
        den = jnp.sum(p, axis=0, keepdims=True) + jnp.exp(sink - m)
        p_both = jnp.concatenate([jnp.where(from_prev, p, 0.0), jnp.where(from_prev, 0.0, p)],
                                 axis=0).astype(BF16)
        return jnp.dot(v_both, p_both, preferred_element_type=F32) / den

    pending = [scores(0, hd) for hd in range(A_Q_HEADS)]
    for j in range(ATTN_QB):
        current = pending
        if j + 1 < ATTN_QB:
            pending = [scores(j + 1, hd) for hd in range(A_Q_HEADS)]
        outs = {}
        for hd in range(A_Q_HEADS):
            outs[hd] = attend(j, hd, *current[hd])
            if hd % 2 == 1:
                qsl = slice(LANES * (hd // 2), LANES * (hd // 2 + 1))
                o_pair = jnp.concatenate([outs.pop(hd - 1), outs.pop(hd)], axis=0).T
                gate = g_ref[blk(j), qsl].astype(F32)
                o_ref[blk(j), qsl] = (o_pair * _silu(gate)).astype(o_ref.dtype)


def _attn(sinks, qt, k, vt, ag, bias):
    qb = ATTN_QB
    steps = SEQ // (A_BLOCK * qb)
    cur = lambda b, n: (b * steps + n, 0)
    cur_t = lambda b, n: (0, b * steps + n)
    prev = lambda b, n: ((b * steps + n) * qb - jnp.minimum(n, 1), 0)
    prev_t = lambda b, n: (0, (b * steps + n) * qb - jnp.minimum(n, 1))
    return pl.pallas_call(
        _attn_kernel,
        out_shape=jax.ShapeDtypeStruct((TOKENS, A_WIDTH), BF16),
        grid=(BATCH, steps),
        in_specs=[pl.BlockSpec(memory_space=pltpu.SMEM),
                  pl.BlockSpec((A_WIDTH, A_BLOCK * qb), cur_t),
                  pl.BlockSpec((A_BLOCK, A_KV_WIDTH), prev),
                  pl.BlockSpec((A_BLOCK * qb, A_KV_WIDTH), cur),
                  pl.BlockSpec((A_KV_WIDTH, A_BLOCK), prev_t),
                  pl.BlockSpec((A_KV_WIDTH, A_BLOCK * qb), cur_t),
                  pl.BlockSpec((A_BLOCK * qb, A_WIDTH), cur),
                  pl.BlockSpec((2, A_Q_HEADS, A_BLOCK, A_BLOCK), lambda b, n: (0, 0, 0, 0))],
        out_specs=pl.BlockSpec((A_BLOCK * qb, A_WIDTH), cur),
        compiler_params=_cparams(("arbitrary", "arbitrary")),
        name="attn",
    )(sinks, qt, k, k, vt, vt, ag, bias)


def _gla_kernel(la_ref, q_ref, k_ref, v_ref, g_ref, gn_ref, o_ref, st_ref):
    cb = GLA_BLOCK

    @pl.when(pl.program_id(0) == 0)
    def _():
        st_ref[...] = jnp.zeros_like(st_ref)

    r = lax.broadcasted_iota(jnp.int32, (cb, cb), 0)
    c = lax.broadcasted_iota(jnp.int32, (cb, cb), 1)
    tri = (c <= r).astype(F32)
    lane = lax.broadcasted_iota(jnp.int32, (cb, B_QK_WIDTH), 1)
    head_masks = [(lane >= B_DK * h) & (lane < B_DK * (h + 1)) for h in range(B_HEADS)]
    rr = lax.broadcasted_iota(jnp.int32, (B_HEADS * cb, cb), 0)
    cc = lax.broadcasted_iota(jnp.int32, (B_HEADS * cb, cb), 1)
    causal = cc <= (rr & (cb - 1))
    nt = (((1,), (1,)), ((), ()))
    tn = (((0,), (0,)), ((), ()))
    rows = lambda u: slice(cb * u, cb * (u + 1))
    items = [(u, b) for u in range(GLA_SUB) for b in range(BATCH)]
    bcs = {(u, b): jnp.dot(tri, la_ref[b, rows(u), :], preferred_element_type=F32, precision=HIGHEST)
           for u, b in items}
    qsts, kss, ksts, decs, vs = {}, {}, {}, {}, {}
    for it in items:
        u, b = it
        bc = bcs[it]
        bl = bc[cb - 1:cb, :]
        q = q_ref[b, rows(u), :].astype(F32) * (B_DK ** -0.5)
        k = k_ref[b, rows(u), :].astype(F32)
        qs = q * jnp.exp(bc)
        kh = k * jnp.exp(bl - bc)
        kss[it] = (k * jnp.exp(-bc)).astype(BF16)
        decs[it] = jnp.exp(bl)
        qsts[it] = jnp.concatenate([jnp.where(m, qs, 0.0) for m in head_masks], axis=0).astype(BF16)
        ksts[it] = jnp.concatenate([jnp.where(m, kh, 0.0) for m in head_masks], axis=0).astype(BF16)
        vs[it] = v_ref[b, rows(u), :]
    a_alls = {it: lax.dot_general(qsts[it], kss[it], nt, preferred_element_type=F32) for it in items}
    upds = {}
    for it in items:
        vst = jnp.concatenate([vs[it][:, B_DV * h:B_DV * (h + 1)] for h in range(B_HEADS)], axis=0)
        upds[it] = lax.dot_general(vst, ksts[it], tn, preferred_element_type=F32)
    st_in = {}
    for b in range(BATCH):
        st = st_ref[b]
        for u in range(GLA_SUB):
            st_in[u, b] = st
            st = st * decs[u, b] + upds[u, b]
        st_ref[b] = st
    oi_alls = {it: lax.dot_general(qsts[it], st_in[it].astype(BF16), nt, preferred_element_type=F32)
               for it in items}
    o_hs = {}
    for it in items:
        a_all = jnp.where(causal, a_alls[it], 0.0).astype(BF16)
        for h in range(B_HEADS):
            o_hs[it, h] = (jnp.dot(a_all[cb * h:cb * (h + 1)], vs[it][:, B_DV * h:B_DV * (h + 1)],
                                   preferred_element_type=F32) + oi_alls[it][cb * h:cb * (h + 1)])
    for it in items:
        u, b = it
        for h in range(B_HEADS):
            vsl = slice(B_DV * h, B_DV * (h + 1))
            o_h = o_hs[it, h]
            ms = jnp.mean(o_h * o_h, axis=-1, keepdims=True)
            o_n = o_h * lax.rsqrt(ms + EPS) * gn_ref[:, vsl]
            gate = g_ref[b, rows(u), vsl].astype(F32)
            o_ref[b, rows(u), vsl] = (o_n * _silu(gate)).astype(o_ref.dtype)


def _gla(log_a, bq, bk, bv, bg, g_gla):
    cb = GLA_BLOCK * GLA_SUB
    blk = lambda w: pl.BlockSpec((BATCH, cb, w), lambda i: (0, i, 0))
    r3 = lambda a: a.reshape(BATCH, SEQ, a.shape[-1])
    out = pl.pallas_call(
        _gla_kernel,
        out_shape=jax.ShapeDtypeStruct((BATCH, SEQ, B_WIDTH), BF16),
        grid=(SEQ // cb,),
        in_specs=[blk(B_QK_WIDTH), blk(B_QK_WIDTH), blk(B_QK_WIDTH), blk(B_WIDTH), blk(B_WIDTH),
                  pl.BlockSpec((1, B_WIDTH), lambda i: (0, 0))],
        out_specs=blk(B_WIDTH),
        scratch_shapes=[pltpu.VMEM((BATCH, B_DV, B_QK_WIDTH), F32)],
        compiler_params=_cparams(("arbitrary",)),
        name="gla",
    )(r3(log_a), r3(bq), r3(bk), r3(bv), r3(bg), g_gla.reshape(1, B_WIDTH))
    return out.reshape(TOKENS, B_WIDTH)


def _s5prep_kernel(ar_ref, ai_ref, ldt_ref, bre_ref, bim_ref, btre_ref, btim_ref, cre_ref, cim_ref,
                   mt_ref, wet_ref, wyt_ref, are_ref, aim_ref):
    p = C_STATE
    wet_re, wet_im, wyt_rows, a_re, a_im = [], [], [], [], []
    for g in range(2):
        kk, e_re, e_im, y_re, y_im, p_re, p_im = _s5_discretise(
            ar_ref[g], ai_ref[g], ldt_ref[g], bre_ref[g], bim_ref[g], btre_ref[g], btim_ref[g],
            cre_ref[g], cim_ref[g])
        pieces = [kk] + [jnp.concatenate([jnp.zeros((C_GROUP_CH * s, C_GROUP_CH), F32),
                                          kk[:S5_TC - C_GROUP_CH * s]], axis=0) for s in range(1, S5_CHUNK)]
        mt_ref[0, g] = jnp.concatenate(pieces, axis=1).astype(BF16)
        e_t = jnp.concatenate([e_re, e_im], axis=1).T
        zero = jnp.zeros((p, S5_TC), F32)
        wet_re.append(jnp.concatenate([e_t[:p], zero] if g == 0 else [zero, e_t[:p]], axis=1))
        wet_im.append(jnp.concatenate([e_t[p:], zero] if g == 0 else [zero, e_t[p:]], axis=1))
        zero = jnp.zeros((S5_TC, p), F32)
        wyt_rows.append(jnp.concatenate([y_re, zero, -y_im, zero] if g == 0 else [zero, y_re, zero, -y_im], axis=1))
        a_re.append(p_re)
        a_im.append(p_im)
    wet_ref[0] = jnp.concatenate(wet_re + wet_im, axis=0).astype(BF16)
    wyt_ref[0] = jnp.concatenate(wyt_rows, axis=0).astype(BF16)
    are_ref[0] = jnp.broadcast_to(jnp.concatenate(a_re, axis=1), (8, 2 * p))
    aim_ref[0] = jnp.broadcast_to(jnp.concatenate(a_im, axis=1), (8, 2 * p))


def _s5_discretise(ar, ai, ldt, b_re, b_im, bt_re16, bt_im16, c_re16, c_im16):
    dt = jnp.exp(ldt)

    def cmul(xr, xi, yr, yi):
        return xr * yr - xi * yi, xr * yi + xi * yr

    kf = lax.broadcasted_iota(jnp.int32, (S5_POW_ROWS, 1), 0).astype(F32)
    mag = jnp.exp(kf * (ar * dt))
    ang = kf * (ai * dt)
    pw_re, pw_im = mag * jnp.cos(ang), mag * jnp.sin(ang)
    abar_re, abar_im = pw_re[1:2], pw_im[1:2]
    den = ar * ar + ai * ai
    num_re = abar_re - 1.0
    f_re = (num_re * ar + abar_im * ai) / den
    f_im = (abar_im * ar - num_re * ai) / den
    g_re, g_im = cmul(pw_re, pw_im, f_re, f_im)

    def pick(which, xr, xi):
        rep = lambda x: jnp.concatenate(
            [jnp.broadcast_to(x[which(i):which(i) + 1], (C_GROUP_CH, C_STATE)) for i in range(S5_CHUNK)], axis=0)
        return rep(xr), rep(xi)

    tile16 = lambda a: jnp.concatenate([a] * S5_CHUNK, axis=0)
    ct_re, ct_im = tile16(c_re16), tile16(c_im16)
    bt_re, bt_im = tile16(bt_re16), tile16(bt_im16)

    w_re, w_im = cmul(*pick(lambda i: i, g_re, g_im), ct_re, ct_im)
    kk = (jnp.dot(w_re, b_re, preferred_element_type=F32, precision=HIGHEST)
          - jnp.dot(w_im, b_im, preferred_element_type=F32, precision=HIGHEST))
    e_re, e_im = cmul(*pick(lambda i: S5_CHUNK - 1 - i, g_re, g_im), bt_re, bt_im)
    y_re, y_im = cmul(*pick(lambda i: i + 1, pw_re, pw_im), ct_re, ct_im)
    return kk, e_re, e_im, y_re, y_im, pw_re[S5_CHUNK:S5_CHUNK + 1], pw_im[S5_CHUNK:S5_CHUNK + 1]


def _s5prep(a_re, a_im, log_dt, b_re, b_im, c_re, c_im, d):
    p, ch = C_STATE, C_GROUP_CH
    layers = a_re.shape[0]
    g = layers * C_GROUPS
    npair = g // 2
    flat = lambda a: a.reshape(g, *a.shape[2:])
    a_re, a_im, log_dt, b_re, b_im, c_re, c_im = map(flat, (a_re, a_im, log_dt, b_re, b_im, c_re, c_im))
    row = lambda a: a.reshape(g, 1, p)
    ldt = jnp.broadcast_to(log_dt[:, None, None], (g, 1, p))
    b_t = lambda a: jnp.swapaxes(a, 1, 2)
    spec = lambda s1, s2: pl.BlockSpec((2, s1, s2), lambda i: (i, 0, 0))
    mt, wet, wyt, pw_re, pw_im = pl.pallas_call(
        _s5prep_kernel,
        out_shape=[jax.ShapeDtypeStruct((npair, 2, S5_TC, S5_TC), BF16),
                   jax.ShapeDtypeStruct((npair, 4 * p, 2 * S5_TC), BF16),
                   jax.ShapeDtypeStruct((npair, 2 * S5_TC, 4 * p), BF16),
                   jax.ShapeDtypeStruct((npair, 8, 2 * p), F32),
                   jax.ShapeDtypeStruct((npair, 8, 2 * p), F32)],
        grid=(npair,),
        in_specs=[spec(1, p), spec(1, p), spec(1, p), spec(p, ch), spec(p, ch),
                  spec(ch, p), spec(ch, p), spec(ch, p), spec(ch, p)],
        out_specs=[pl.BlockSpec((1, 2, S5_TC, S5_TC), lambda i: (i, 0, 0, 0)),
                   pl.BlockSpec((1, 4 * p, 2 * S5_TC), lambda i: (i, 0, 0)),
                   pl.BlockSpec((1, 2 * S5_TC, 4 * p), lambda i: (i, 0, 0)),
                   pl.BlockSpec((1, 8, 2 * p), lambda i: (i, 0, 0)),
                   pl.BlockSpec((1, 8, 2 * p), lambda i: (i, 0, 0))],
        compiler_params=_cparams(("arbitrary",)),
        name="s5prep",
    )(row(a_re), row(a_im), ldt, b_re, b_im, b_t(b_re), b_t(b_im), c_re, c_im)
    by_gb = lambda a: a.reshape(layers, S5_NGB, S5_PAIRS_PER_GB, *a.shape[1:])
    return (mt.reshape(layers, S5_NGB, S5_GB, S5_TC, S5_TC), by_gb(wet), by_gb(wyt), by_gb(pw_re), by_gb(pw_im),
            d.reshape(layers, 1, C_WIDTH))


def _s5_kernel(u_ref, mt_ref, wet_ref, wyt_ref, are_ref, aim_ref, d_ref, y_ref,
               ut_ref, yt_ref, ere_ref, eim_ref, hre_ref, him_ref):
    nck, t_len, ch = S5_NCHUNK, S5_CHUNK, C_GROUP_CH
    nt = (((1,), (1,)), ((), ()))
    for t in range(t_len):
        xt = u_ref[pl.ds(t, nck, stride=t_len), :].T
        for g in range(S5_GB):
            ut_ref[g, ch * t:ch * (t + 1), :] = xt[ch * g:ch * (g + 1), :]
    for j in range(S5_PAIRS_PER_GB):
        u0 = ut_ref[2 * j].astype(BF16)
        u1 = ut_ref[2 * j + 1].astype(BF16)
        et = jnp.dot(wet_ref[0, j], jnp.concatenate([u0, u1], axis=0), preferred_element_type=F32)
        e = et.T
        ere_ref[:, LANES * j:LANES * (j + 1)] = e[:, :LANES]
        eim_ref[:, LANES * j:LANES * (j + 1)] = e[:, LANES:]
        yt_ref[2 * j] = jnp.dot(mt_ref[0, 2 * j], u0, preferred_element_type=F32)
        yt_ref[2 * j + 1] = jnp.dot(mt_ref[0, 2 * j + 1], u1, preferred_element_type=F32)

    a_re = jnp.concatenate([are_ref[0, j, 0:1, :] for j in range(S5_PAIRS_PER_GB)], axis=1)
    a_im = jnp.concatenate([aim_ref[0, j, 0:1, :] for j in range(S5_PAIRS_PER_GB)], axis=1)

    def body(i, carry):
        h_re, h_im = carry
        hre_ref[pl.ds(i, 1), :] = h_re
        him_ref[pl.ds(i, 1), :] = h_im
        e_re = ere_ref[pl.ds(i, 1), :]
        e_im = eim_ref[pl.ds(i, 1), :]
        return a_re * h_re - a_im * h_im + e_re, a_re * h_im + a_im * h_re + e_im

    zero = jnp.zeros((1, S5_GB * C_STATE), F32)
    lax.fori_loop(0, nck, body, (zero, zero))

    for j in range(S5_PAIRS_PER_GB):
        sl = slice(LANES * j, LANES * (j + 1))
        hp = jnp.concatenate([hre_ref[:, sl], him_ref[:, sl]], axis=1).astype(BF16)
        yi = lax.dot_general(wyt_ref[0, j], hp, nt, preferred_element_type=F32)
        yt_ref[2 * j] += yi[:S5_TC]
        yt_ref[2 * j + 1] += yi[S5_TC:]
    for t in range(t_len):
        ytt = jnp.concatenate([yt_ref[g, ch * t:ch * (t + 1), :] for g in range(S5_GB)], axis=0)
        rows = pl.ds(t, nck, stride=t_len)
        y_ref[rows, :] = ytt.T + d_ref[...] * u_ref[rows, :]


def _s5(cu, mt, wet, wyt, pw_re, pw_im, d_row):
    p4 = 4 * C_STATE
    tok = pl.BlockSpec((SEQ, LANES), lambda gb, b: (b, gb))
    per_gb = lambda *s: pl.BlockSpec((1,) + s, lambda gb, b: (gb,) + (0,) * len(s))
    state = pltpu.VMEM((S5_NCHUNK, S5_GB * C_STATE), F32)
    return pl.pallas_call(
        _s5_kernel,
        out_shape=jax.ShapeDtypeStruct((TOKENS, C_WIDTH), F32),
        grid=(S5_NGB, BATCH),
        in_specs=[tok,
                  per_gb(S5_GB, S5_TC, S5_TC),
                  per_gb(S5_PAIRS_PER_GB, p4, 2 * S5_TC),
                  per_gb(S5_PAIRS_PER_GB, 2 * S5_TC, p4),
                  per_gb(S5_PAIRS_PER_GB, 8, 2 * C_STATE), per_gb(S5_PAIRS_PER_GB, 8, 2 * C_STATE),
                  pl.BlockSpec((1, LANES), lambda gb, b: (0, gb))],
        out_specs=tok,
        scratch_shapes=[pltpu.VMEM((S5_GB, S5_TC, S5_NCHUNK), F32),
                        pltpu.VMEM((S5_GB, S5_TC, S5_NCHUNK), F32),
                        state, state, state, state],
        compiler_params=_cparams(("arbitrary", "arbitrary")),
        name="s5",
    )(cu, mt, wet, wyt, pw_re, pw_im, d_row)


def _gelu_tanh(x):
    return 0.5 * x * (1.0 + jnp.tanh(math.sqrt(2.0 / math.pi) * (x + 0.044715 * (x * x * x))))


def _out_kernel(oa_ref, ob_ref, yc_ref, cg_ref, x_ref, gate_ref, gpost_ref, wglu_ref, bglu_ref, wout_ref, o_ref):
    y = _gelu_tanh(yc_ref[...])
    z = jnp.dot(y.astype(BF16), wglu_ref[...], preferred_element_type=F32) + bglu_ref[...]
    y = y * jax.nn.sigmoid(z)
    oc = (y * _silu(cg_ref[...].astype(F32))).astype(BF16)
    mix = jnp.concatenate([oa_ref[...], ob_ref[...], oc], axis=1)
    acc = jnp.dot(mix, wout_ref[...], preferred_element_type=F32)
    ms = jnp.mean(acc * acc, axis=-1, keepdims=True)
    out = acc * lax.rsqrt(ms + EPS) * gpost_ref[...]
    o_ref[...] = x_ref[...] + gate_ref[0] * out


def _out(oa, ob, yc, cg, x2, gate, g_post, w_glu, b_glu, w_out):
    tm = OUT_TM
    steps_per_batch = SEQ // tm
    row = lambda i: (i, 0)
    const = lambda i: (0, 0)
    return pl.pallas_call(
        _out_kernel,
        out_shape=jax.ShapeDtypeStruct((TOKENS, D_MODEL), F32),
        grid=(TOKENS // tm,),
        in_specs=[pl.BlockSpec((tm, A_WIDTH), row),
                  pl.BlockSpec((tm, B_WIDTH), row),
                  pl.BlockSpec((tm, C_WIDTH), row),
                  pl.BlockSpec((tm, C_WIDTH), row),
                  pl.BlockSpec((tm, D_MODEL), row),
                  pl.BlockSpec((1, 1, D_MODEL), lambda i: (i // steps_per_batch, 0, 0)),
                  pl.BlockSpec((1, D_MODEL), const),
                  pl.BlockSpec((C_WIDTH, C_WIDTH), const),
                  pl.BlockSpec((1, C_WIDTH), const),
                  pl.BlockSpec((2 * D_MODEL, D_MODEL), const)],
        out_specs=pl.BlockSpec((tm, D_MODEL), row),
        compiler_params=_cparams(("arbitrary",)),
        name="out",
    )(oa, ob, yc, cg, x2, gate, g_post.reshape(1, D_MODEL), w_glu, b_glu.reshape(1, C_WIDTH), w_out)


def kernel(x, c, w_mod, b_mod, g_pre, g_post, w_in, attn_sinks, gla_w_alpha, gla_b_alpha, gla_norm_g,
           s5_a_re, s5_a_im, s5_log_dt, s5_b_re, s5_b_im, s5_c_re, s5_c_im, s5_d, s5_w_glu, s5_b_glu, w_out):
    layers = w_mod.shape[0]
    x2 = x.reshape(TOKENS, D_MODEL)
    bias = _attn_bias()
    mod = _mod(jnp.pad(c, ((0, 8 - BATCH), (0, 0))), w_mod, b_mod)[:, :BATCH]
    shift, scale, gate = (m.reshape(layers, BATCH, 1, D_MODEL) for m in jnp.split(mod, 3, axis=-1))
    w_t = jnp.swapaxes(w_in, 1, 2).astype(BF16)
    w_alpha_pad = jnp.pad(gla_w_alpha, ((0, 0), (0, LR_PAD - B_GATE_RANK), (0, 0))).astype(BF16)
    s5_ops = _s5prep(s5_a_re, s5_a_im, s5_log_dt, s5_b_re, s5_b_im, s5_c_re, s5_c_im, s5_d)
    w_glu = s5_w_glu.astype(BF16)
    w_out_b = w_out.astype(BF16)
    for l in range(layers):
        ak, ag, bq, bk, bv, bg, cu, cg, log_a, aqt, avt = _proj(
            x2, scale[l], shift[l], g_pre[l], w_t[l], w_alpha_pad[l], gla_b_alpha[l])
        o_a = _attn(attn_sinks[l], aqt, ak, avt, ag, bias)
        o_b = _gla(log_a, bq, bk, bv, bg, gla_norm_g[l])
        y_c = _s5(cu, *(op[l] for op in s5_ops))
        x2 = _out(o_a, o_b, y_c, cg, x2, gate[l], g_post[l], w_glu[l], s5_b_glu[l], w_out_b[l])
    return x2.reshape(x.shape)
```

```python
import math

import jax
import jax.numpy as jnp
import numpy as np
from jax import lax
from jax.experimental import pallas as pl
from jax.experimental.pallas import tpu as pltpu

F32 = jnp.float32
BF16 = jnp.bfloat16
HIGHEST = lax.Precision.HIGHEST

D_MODEL = 1024
BATCH = 4
SEQ = 4096
TOKENS = BATCH * SEQ
EPS = 1e-6

A_WIDTH = 1024
A_HEAD_DIM = 64
A_Q_HEADS = 16
A_KV_HEADS = 4
A_KV_WIDTH = A_KV_HEADS * A_HEAD_DIM
A_BLOCK = 128
WINDOW = 128

B_WIDTH = 512
B_HEADS = 4
B_DK = 64
B_DV = 128
B_QK_WIDTH = 256
B_GATE_RANK = 16
B_GATE_TAU = 16.0
GLA_BLOCK = 64
GLA_SUB = 4
ATTN_QB = 4
C_WIDTH = 512
C_GROUP_CH = 16
C_GROUPS = 32
C_STATE = 64
S5_CHUNK = 16
S5_NCHUNK = SEQ // S5_CHUNK
S5_TC = S5_CHUNK * C_GROUP_CH
S5_GB = 8
S5_NGB = C_GROUPS // S5_GB
S5_PAIRS_PER_GB = S5_GB // 2
S5_POW_ROWS = 24

LANES = 128
LR_PAD = LANES

V7X_VMEM_LIMIT = 56 * 1024 * 1024

PROJ_TM = 512
OUT_TM = 1024

_W_IN_SIZES = (("aq", A_WIDTH), ("ak", A_KV_WIDTH), ("av", A_KV_WIDTH), ("ag", A_WIDTH), ("bq", B_QK_WIDTH),
               ("bk", B_QK_WIDTH), ("bv", B_WIDTH), ("blr", B_GATE_RANK), ("bg", B_WIDTH), ("cu", C_WIDTH),
               ("cg", C_WIDTH))
_W_IN_OFF = {}
_off = 0
for _name, _w in _W_IN_SIZES:
    _W_IN_OFF[_name] = (_off, _w)
    _off += _w
W_IN_COLS = _off
_PROJ_OUTS = (("ak", BF16), ("ag", BF16), ("bq", BF16), ("bk", BF16),
              ("bv", BF16), ("bg", BF16), ("cu", F32), ("cg", BF16))


def _silu(x):
    return x * jax.nn.sigmoid(x)


def _cparams(semantics):
    return pltpu.CompilerParams(dimension_semantics=semantics, vmem_limit_bytes=V7X_VMEM_LIMIT)


def _mod_kernel(c_ref, w_ref, b_ref, o_ref):
    c = c_ref[...]
    o_ref[0] = jnp.dot(_silu(c).astype(BF16), w_ref[0].astype(BF16), preferred_element_type=F32) + b_ref[0]


def _mod(c_pad, w_mod, b_mod):
    layers = w_mod.shape[0]
    n = 3 * D_MODEL
    tn = 768
    return pl.pallas_call(
        _mod_kernel,
        out_shape=jax.ShapeDtypeStruct((layers, 8, n), F32),
        grid=(layers, n // tn),
        in_specs=[pl.BlockSpec((8, D_MODEL), lambda l, j: (0, 0)),
                  pl.BlockSpec((1, D_MODEL, tn), lambda l, j: (l, 0, j)),
                  pl.BlockSpec((1, 1, tn), lambda l, j: (l, 0, j))],
        out_specs=pl.BlockSpec((1, 8, tn), lambda l, j: (l, 0, j)),
        compiler_params=_cparams(("arbitrary", "arbitrary")),
        name="mod",
    )(c_pad, w_mod, b_mod.reshape(layers, 1, n))


def _proj_kernel(x_ref, scale_ref, shift_ref, gpre_ref, wt_ref, walpha_ref, balpha_ref, *out_refs):
    x = x_ref[...]
    ms = jnp.mean(x * x, axis=-1, keepdims=True)
    y = x * lax.rsqrt(ms + EPS) * gpre_ref[...]
    h = (y * (1.0 + scale_ref[0]) + shift_ref[0]).astype(BF16)
    nt = (((1,), (1,)), ((), ()))

    def rows(name, width=None):
        off, w = _W_IN_OFF[name]
        return wt_ref[off:off + (width or w), :]

    for (name, _), o_ref in zip(_PROJ_OUTS, out_refs):
        o_ref[...] = lax.dot_general(h, rows(name), nt, preferred_element_type=F32).astype(o_ref.dtype)
    lr = lax.dot_general(h, rows("blr", LR_PAD), nt, preferred_element_type=F32).astype(BF16)
    logits = jnp.dot(lr, walpha_ref[...], preferred_element_type=F32) + balpha_ref[...]
    log_sig = jnp.minimum(logits, 0.0) - jnp.log(1.0 + jnp.exp(-jnp.abs(logits)))
    la_ref, qt_ref, vt_ref = out_refs[len(_PROJ_OUTS):]
    la_ref[...] = log_sig * (1.0 / B_GATE_TAU)
    qt = lax.dot_general(rows("aq"), h, nt, preferred_element_type=F32) * (A_HEAD_DIM ** -0.5)
    qt_ref[...] = qt.astype(qt_ref.dtype)
    vt_ref[...] = lax.dot_general(rows("av"), h, nt, preferred_element_type=F32).astype(vt_ref.dtype)


def _proj(x2, scale, shift, g_pre, w_t, layer, w_alpha_pad, b_alpha):
    tm = PROJ_TM
    steps_per_batch = SEQ // tm
    row = lambda i: (i, 0)
    col = lambda i: (0, i)
    per_batch = lambda i: (i // steps_per_batch, 0, 0)
    const = lambda i: (0, 0)
    out_shape = [jax.ShapeDtypeStruct((TOKENS, _W_IN_OFF[n][1]), dt) for n, dt in _PROJ_OUTS]
    out_specs = [pl.BlockSpec((tm, _W_IN_OFF[n][1]), row) for n, _ in _PROJ_OUTS]
    out_shape += [jax.ShapeDtypeStruct((TOKENS, B_QK_WIDTH), F32),
                  jax.ShapeDtypeStruct((A_WIDTH, TOKENS), BF16),
                  jax.ShapeDtypeStruct((A_KV_WIDTH, TOKENS), BF16)]
    out_specs += [pl.BlockSpec((tm, B_QK_WIDTH), row),
                  pl.BlockSpec((A_WIDTH, tm), col),
                  pl.BlockSpec((A_KV_WIDTH, tm), col)]
    return pl.pallas_call(
        _proj_kernel,
        out_shape=out_shape,
        grid=(TOKENS // tm,),
        in_specs=[pl.BlockSpec((tm, D_MODEL), row),
                  pl.BlockSpec((1, 1, D_MODEL), per_batch),
                  pl.BlockSpec((1, 1, D_MODEL), per_batch),
                  pl.BlockSpec((1, D_MODEL), const),
                  pl.BlockSpec((W_IN_COLS, D_MODEL), lambda i: (layer, 0)),
                  pl.BlockSpec((LR_PAD, B_QK_WIDTH), const),
                  pl.BlockSpec((1, B_QK_WIDTH), const)],
        out_specs=out_specs,
        compiler_params=_cparams(("arbitrary",)),
        name="proj",
    )(x2, scale, shift, g_pre.reshape(1, D_MODEL), w_t, w_alpha_pad, b_alpha.reshape(1, B_QK_WIDTH))


def _attn_bias():
    j = np.arange(A_BLOCK)[:, None]
    i = np.arange(A_BLOCK)[None, :]
    dist = np.where(j > i, i + A_BLOCK - j, i - j).astype(np.float32)
    slopes = np.exp2(-8.0 * np.arange(1, A_Q_HEADS + 1, dtype=np.float32) / A_Q_HEADS).astype(np.float32)
    bias = -slopes[:, None, None] * dist[None]
    first = np.where((j > i)[None], -np.inf, bias).astype(np.float32)
    return jnp.asarray(np.stack([bias, first]))


def _attn_kernel(sink_ref, qt_ref, kp_ref, kc_ref, vtp_ref, vtc_ref, g_ref, bias_ref, o_ref):
    first_step = pl.program_id(1) == 0
    kj = lax.broadcasted_iota(jnp.int32, (A_BLOCK, A_BLOCK), 0)
    qi = lax.broadcasted_iota(jnp.int32, (A_BLOCK, A_BLOCK), 1)
    from_prev = kj > qi
    zero_rows = jnp.zeros((A_HEAD_DIM, A_BLOCK), BF16)
    group = A_Q_HEADS // A_KV_HEADS
    blk = lambda j: slice(A_BLOCK * j, A_BLOCK * (j + 1))

    def keys(j, sl):
        prev = kp_ref[:, sl] if j == 0 else kc_ref[blk(j - 1), sl]
        return prev, kc_ref[blk(j), sl]

    def values(j, rows):
        prev = vtp_ref[rows, :] if j == 0 else vtc_ref[rows, blk(j - 1)]
        return jnp.concatenate([prev, vtc_ref[rows, blk(j)]], axis=1)

    def scores(j, hd):
        kvh = hd // group
        sl = slice(LANES * (kvh // 2), LANES * (kvh // 2 + 1))
        qh = qt_ref[A_HEAD_DIM * hd:A_HEAD_DIM * (hd + 1), blk(j)]
        qsel = jnp.concatenate([qh, zero_rows] if kvh % 2 == 0 else [zero_rows, qh], axis=0)
        k_prev, k_cur = keys(j, sl)
        return (jnp.dot(k_prev, qsel, preferred_element_type=F32),
                jnp.dot(k_cur, qsel, preferred_element_type=F32))

    def attend(j, hd, s_prev, s_cur):
        kvh = hd // group
        v_both = values(j, slice(A_HEAD_DIM * kvh, A_HEAD_DIM * (kvh + 1)))
        table = jnp.where(first_step, 1, 0) if j == 0 else 0
        s = jnp.where(from_prev, s_prev, s_cur) + bias_ref[table, hd]
        sink = sink_ref[hd]
        m = jnp.maximum(jnp.max(s, axis=0, keepdims=True), sink)
        p = jnp.exp(s - m)
        den = jnp.sum(p, axis=0, keepdims=True) + jnp.exp(sink - m)
        p_both = jnp.concatenate([jnp.where(from_prev, p, 0.0), jnp.where(from_prev, 0.0, p)],
                                 axis=0).astype(BF16)
        return jnp.dot(v_both, p_both, preferred_element_type=F32) / den

    pending = [scores(0, hd) for hd in range(A_Q_HEADS)]
    for j in range(ATTN_QB):
        current = pending
        if j + 1 < ATTN_QB:
            pending = [scores(j + 1, hd) for hd in range(A_Q_HEADS)]
        outs = {}
        for hd in range(A_Q_HEADS):
            outs[hd] = attend(j, hd, *current[hd])
            if hd % 2 == 1:
                qsl = slice(LANES * (hd // 2), LANES * (hd // 2 + 1))
                o_pair = jnp.concatenate([outs.pop(hd - 1), outs.pop(hd)], axis=0).T
                gate = g_ref[blk(j), qsl].astype(F32)
                o_ref[blk(j), qsl] = (o_pair * _silu(gate)).astype(o_ref.dtype)


def _attn(sinks, qt, k, vt, ag, bias):
    qb = ATTN_QB
    steps = SEQ // (A_BLOCK * qb)
    cur = lambda b, n: (b * steps + n, 0)
    cur_t = lambda b, n: (0, b * steps + n)
    prev = lambda b, n: ((b * steps + n) * qb - jnp.minimum(n, 1), 0)
    prev_t = lambda b, n: (0, (b * steps + n) * qb - jnp.minimum(n, 1))
    return pl.pallas_call(
        _attn_kernel,
        out_shape=jax.ShapeDtypeStruct((TOKENS, A_WIDTH), BF16),
        grid=(BATCH, steps),
        in_specs=[pl.BlockSpec(memory_space=pltpu.SMEM),
                  pl.BlockSpec((A_WIDTH, A_BLOCK * qb), cur_t),
                  pl.BlockSpec((A_BLOCK, A_KV_WIDTH), prev),
                  pl.BlockSpec((A_BLOCK * qb, A_KV_WIDTH), cur),
                  pl.BlockSpec((A_KV_WIDTH, A_BLOCK), prev_t),
                  pl.BlockSpec((A_KV_WIDTH, A_BLOCK * qb), cur_t),
                  pl.BlockSpec((A_BLOCK * qb, A_WIDTH), cur),
                  pl.BlockSpec((2, A_Q_HEADS, A_BLOCK, A_BLOCK), lambda b, n: (0, 0, 0, 0))],
        out_specs=pl.BlockSpec((A_BLOCK * qb, A_WIDTH), cur),
        compiler_params=_cparams(("arbitrary", "arbitrary")),
        name="attn",
    )(sinks, qt, k, k, vt, vt, ag, bias)


def _gla_kernel(la_ref, q_ref, k_ref, v_ref, g_ref, gn_ref, o_ref, st_ref):
    cb = GLA_BLOCK

    @pl.when(pl.program_id(0) == 0)
    def _():
        st_ref[...] = jnp.zeros_like(st_ref)

    r = lax.broadcasted_iota(jnp.int32, (cb, cb), 0)
    c = lax.broadcasted_iota(jnp.int32, (cb, cb), 1)
    tri = (c <= r).astype(F32)
    lane = lax.broadcasted_iota(jnp.int32, (cb, B_QK_WIDTH), 1)
    head_masks = [(lane >= B_DK * h) & (lane < B_DK * (h + 1)) for h in range(B_HEADS)]
    rr = lax.broadcasted_iota(jnp.int32, (B_HEADS * cb, cb), 0)
    cc = lax.broadcasted_iota(jnp.int32, (B_HEADS * cb, cb), 1)
    causal = cc <= (rr & (cb - 1))
    nt = (((1,), (1,)), ((), ()))
    tn = (((0,), (0,)), ((), ()))
    rows = lambda u: slice(cb * u, cb * (u + 1))
    items = [(u, b) for u in range(GLA_SUB) for b in range(BATCH)]
    bcs = {(u, b): jnp.dot(tri, la_ref[b, rows(u), :], preferred_element_type=F32, precision=HIGHEST)
           for u, b in items}
    qsts, kss, ksts, decs, vs = {}, {}, {}, {}, {}
    for it in items:
        u, b = it
        bc = bcs[it]
        bl = bc[cb - 1:cb, :]
        q = q_ref[b, rows(u), :].astype(F32) * (B_DK ** -0.5)
        k = k_ref[b, rows(u), :].astype(F32)
        qs = q * jnp.exp(bc)
        kh = k * jnp.exp(bl - bc)
        kss[it] = (k * jnp.exp(-bc)).astype(BF16)
        decs[it] = jnp.exp(bl)
        qsts[it] = jnp.concatenate([jnp.where(m, qs, 0.0) for m in head_masks], axis=0).astype(BF16)
        ksts[it] = jnp.concatenate([jnp.where(m, kh, 0.0) for m in head_masks], axis=0).astype(BF16)
        vs[it] = v_ref[b, rows(u), :]
    a_alls = {it: lax.dot_general(qsts[it], kss[it], nt, preferred_element_type=F32) for it in items}
    upds = {}
    for it in items:
        vst = jnp.concatenate([vs[it][:, B_DV * h:B_DV * (h + 1)] for h in range(B_HEADS)], axis=0)
        upds[it] = lax.dot_general(vst, ksts[it], tn, preferred_element_type=F32)
    st_in = {}
    for b in range(BATCH):
        st = st_ref[b]
        for u in range(GLA_SUB):
            st_in[u, b] = st
            st = st * decs[u, b] + upds[u, b]
        st_ref[b] = st
    oi_alls = {it: lax.dot_general(qsts[it], st_in[it].astype(BF16), nt, preferred_element_type=F32)
               for it in items}
    o_hs = {}
    for it in items:
        a_all = jnp.where(causal, a_alls[it], 0.0).astype(BF16)
        for h in range(B_HEADS):
            o_hs[it, h] = (jnp.dot(a_all[cb * h:cb * (h + 1)], vs[it][:, B_DV * h:B_DV * (h + 1)],
                                   preferred_element_type=F32) + oi_alls[it][cb * h:cb * (h + 1)])
    for it in items:
        u, b = it
        for h in range(B_HEADS):
            vsl = slice(B_DV * h, B_DV * (h + 1))
            o_h = o_hs[it, h]
            ms = jnp.mean(o_h * o_h, axis=-1, keepdims=True)
            o_n = o_h * lax.rsqrt(ms + EPS) * gn_ref[:, vsl]
            gate = g_ref[b, rows(u), vsl].astype(F32)
            o_ref[b, rows(u), vsl] = (o_n * _silu(gate)).astype(o_ref.dtype)


def _gla(log_a, bq, bk, bv, bg, g_gla):
    cb = GLA_BLOCK * GLA_SUB
    blk = lambda w: pl.BlockSpec((BATCH, cb, w), lambda i: (0, i, 0))
    r3 = lambda a: a.reshape(BATCH, SEQ, a.shape[-1])
    out = pl.pallas_call(
        _gla_kernel,
        out_shape=jax.ShapeDtypeStruct((BATCH, SEQ, B_WIDTH), BF16),
        grid=(SEQ // cb,),
        in_specs=[blk(B_QK_WIDTH), blk(B_QK_WIDTH), blk(B_QK_WIDTH), blk(B_WIDTH), blk(B_WIDTH),
                  pl.BlockSpec((1, B_WIDTH), lambda i: (0, 0))],
        out_specs=blk(B_WIDTH),
        scratch_shapes=[pltpu.VMEM((BATCH, B_DV, B_QK_WIDTH), F32)],
        compiler_params=_cparams(("arbitrary",)),
        name="gla",
    )(r3(log_a), r3(bq), r3(bk), r3(bv), r3(bg), g_gla.reshape(1, B_WIDTH))
    return out.reshape(TOKENS, B_WIDTH)


def _s5prep_kernel(ar_ref, ai_ref, ldt_ref, bre_ref, bim_ref, btre_ref, btim_ref, cre_ref, cim_ref,
                   mt_ref, wet_ref, wyt_ref, are_ref, aim_ref):
    p = C_STATE
    wet_re, wet_im, wyt_rows, a_re, a_im = [], [], [], [], []
    for g in range(2):
        kk, e_re, e_im, y_re, y_im, p_re, p_im = _s5_discretise(
            ar_ref[g], ai_ref[g], ldt_ref[g], bre_ref[g], bim_ref[g], btre_ref[g], btim_ref[g],
            cre_ref[g], cim_ref[g])
        pieces = [kk] + [jnp.concatenate([jnp.zeros((C_GROUP_CH * s, C_GROUP_CH), F32),
                                          kk[:S5_TC - C_GROUP_CH * s]], axis=0) for s in range(1, S5_CHUNK)]
        mt_ref[0, g] = jnp.concatenate(pieces, axis=1).astype(BF16)
        e_t = jnp.concatenate([e_re, e_im], axis=1).T
        zero = jnp.zeros((p, S5_TC), F32)
        wet_re.append(jnp.concatenate([e_t[:p], zero] if g == 0 else [zero, e_t[:p]], axis=1))
        wet_im.append(jnp.concatenate([e_t[p:], zero] if g == 0 else [zero, e_t[p:]], axis=1))
        zero = jnp.zeros((S5_TC, p), F32)
        wyt_rows.append(jnp.concatenate([y_re, zero, -y_im, zero] if g == 0 else [zero, y_re, zero, -y_im], axis=1))
        a_re.append(p_re)
        a_im.append(p_im)
    wet_ref[0] = jnp.concatenate(wet_re + wet_im, axis=0).astype(BF16)
    wyt_ref[0] = jnp.concatenate(wyt_rows, axis=0).astype(BF16)
    are_ref[0] = jnp.broadcast_to(jnp.concatenate(a_re, axis=1), (8, 2 * p))
    aim_ref[0] = jnp.broadcast_to(jnp.concatenate(a_im, axis=1), (8, 2 * p))


def _s5_discretise(ar, ai, ldt, b_re, b_im, bt_re16, bt_im16, c_re16, c_im16):
    dt = jnp.exp(ldt)

    def cmul(xr, xi, yr, yi):
        return xr * yr - xi * yi, xr * yi + xi * yr

    kf = lax.broadcasted_iota(jnp.int32, (S5_POW_ROWS, 1), 0).astype(F32)
    mag = jnp.exp(kf * (ar * dt))
    ang = kf * (ai * dt)
    pw_re, pw_im = mag * jnp.cos(ang), mag * jnp.sin(ang)
    abar_re, abar_im = pw_re[1:2], pw_im[1:2]
    den = ar * ar + ai * ai
    num_re = abar_re - 1.0
    f_re = (num_re * ar + abar_im * ai) / den
    f_im = (abar_im * ar - num_re * ai) / den
    g_re, g_im = cmul(pw_re, pw_im, f_re, f_im)

    def pick(which, xr, xi):
        rep = lambda x: jnp.concatenate(
            [jnp.broadcast_to(x[which(i):which(i) + 1], (C_GROUP_CH, C_STATE)) for i in range(S5_CHUNK)], axis=0)
        return rep(xr), rep(xi)

    tile16 = lambda a: jnp.concatenate([a] * S5_CHUNK, axis=0)
    ct_re, ct_im = tile16(c_re16), tile16(c_im16)
    bt_re, bt_im = tile16(bt_re16), tile16(bt_im16)

    w_re, w_im = cmul(*pick(lambda i: i, g_re, g_im), ct_re, ct_im)
    kk = (jnp.dot(w_re, b_re, preferred_element_type=F32, precision=HIGHEST)
          - jnp.dot(w_im, b_im, preferred_element_type=F32, precision=HIGHEST))
    e_re, e_im = cmul(*pick(lambda i: S5_CHUNK - 1 - i, g_re, g_im), bt_re, bt_im)
    y_re, y_im = cmul(*pick(lambda i: i + 1, pw_re, pw_im), ct_re, ct_im)
    return kk, e_re, e_im, y_re, y_im, pw_re[S5_CHUNK:S5_CHUNK + 1], pw_im[S5_CHUNK:S5_CHUNK + 1]


def _s5prep(a_re, a_im, log_dt, b_re, b_im, c_re, c_im, d):
    p, ch = C_STATE, C_GROUP_CH
    layers = a_re.shape[0]
    g = layers * C_GROUPS
    npair = g // 2
    flat = lambda a: a.reshape(g, *a.shape[2:])
    a_re, a_im, log_dt, b_re, b_im, c_re, c_im = map(flat, (a_re, a_im, log_dt, b_re, b_im, c_re, c_im))
    row = lambda a: a.reshape(g, 1, p)
    ldt = jnp.broadcast_to(log_dt[:, None, None], (g, 1, p))
    b_t = lambda a: jnp.swapaxes(a, 1, 2)
    spec = lambda s1, s2: pl.BlockSpec((2, s1, s2), lambda i: (i, 0, 0))
    mt, wet, wyt, pw_re, pw_im = pl.pallas_call(
        _s5prep_kernel,
        out_shape=[jax.ShapeDtypeStruct((npair, 2, S5_TC, S5_TC), BF16),
                   jax.ShapeDtypeStruct((npair, 4 * p, 2 * S5_TC), BF16),
                   jax.ShapeDtypeStruct((npair, 2 * S5_TC, 4 * p), BF16),
                   jax.ShapeDtypeStruct((npair, 8, 2 * p), F32),
                   jax.ShapeDtypeStruct((npair, 8, 2 * p), F32)],
        grid=(npair,),
        in_specs=[spec(1, p), spec(1, p), spec(1, p), spec(p, ch), spec(p, ch),
                  spec(ch, p), spec(ch, p), spec(ch, p), spec(ch, p)],
        out_specs=[pl.BlockSpec((1, 2, S5_TC, S5_TC), lambda i: (i, 0, 0, 0)),
                   pl.BlockSpec((1, 4 * p, 2 * S5_TC), lambda i: (i, 0, 0)),
                   pl.BlockSpec((1, 2 * S5_TC, 4 * p), lambda i: (i, 0, 0)),
                   pl.BlockSpec((1, 8, 2 * p), lambda i: (i, 0, 0)),
                   pl.BlockSpec((1, 8, 2 * p), lambda i: (i, 0, 0))],
        compiler_params=_cparams(("arbitrary",)),
        name="s5prep",
    )(row(a_re), row(a_im), ldt, b_re, b_im, b_t(b_re), b_t(b_im), c_re, c_im)
    by_gb = lambda a: a.reshape(layers * S5_NGB, S5_PAIRS_PER_GB, *a.shape[1:])
    return (mt.reshape(layers * S5_NGB, S5_GB, S5_TC, S5_TC), by_gb(wet), by_gb(wyt), by_gb(pw_re), by_gb(pw_im),
            d.reshape(layers, 1, C_WIDTH))


def _s5_kernel(u_ref, mt_ref, wet_ref, wyt_ref, are_ref, aim_ref, d_ref, y_ref,
               ut_ref, yt_ref, ere_ref, eim_ref, hre_ref, him_ref):
    nck, t_len, ch = S5_NCHUNK, S5_CHUNK, C_GROUP_CH
    nt = (((1,), (1,)), ((), ()))
    for t in range(t_len):
        xt = u_ref[pl.ds(t, nck, stride=t_len), :].T
        for g in range(S5_GB):
            ut_ref[g, ch * t:ch * (t + 1), :] = xt[ch * g:ch * (g + 1), :]
    for j in range(S5_PAIRS_PER_GB):
        u0 = ut_ref[2 * j].astype(BF16)
        u1 = ut_ref[2 * j + 1].astype(BF16)
        et = jnp.dot(wet_ref[0, j], jnp.concatenate([u0, u1], axis=0), preferred_element_type=F32)
        e = et.T
        ere_ref[:, LANES * j:LANES * (j + 1)] = e[:, :LANES]
        eim_ref[:, LANES * j:LANES * (j + 1)] = e[:, LANES:]
        yt_ref[2 * j] = jnp.dot(mt_ref[0, 2 * j], u0, preferred_element_type=F32)
        yt_ref[2 * j + 1] = jnp.dot(mt_ref[0, 2 * j + 1], u1, preferred_element_type=F32)

    a_re = jnp.concatenate([are_ref[0, j, 0:1, :] for j in range(S5_PAIRS_PER_GB)], axis=1)
    a_im = jnp.concatenate([aim_ref[0, j, 0:1, :] for j in range(S5_PAIRS_PER_GB)], axis=1)

    def body(i, carry):
        h_re, h_im = carry
        hre_ref[pl.ds(i, 1), :] = h_re
        him_ref[pl.ds(i, 1), :] = h_im
        e_re = ere_ref[pl.ds(i, 1), :]
        e_im = eim_ref[pl.ds(i, 1), :]
        return a_re * h_re - a_im * h_im + e_re, a_re * h_im + a_im * h_re + e_im

    zero = jnp.zeros((1, S5_GB * C_STATE), F32)
    lax.fori_loop(0, nck, body, (zero, zero))

    for j in range(S5_PAIRS_PER_GB):
        sl = slice(LANES * j, LANES * (j + 1))
        hp = jnp.concatenate([hre_ref[:, sl], him_ref[:, sl]], axis=1).astype(BF16)
        yi = lax.dot_general(wyt_ref[0, j], hp, nt, preferred_element_type=F32)
        yt_ref[2 * j] += yi[:S5_TC]
        yt_ref[2 * j + 1] += yi[S5_TC:]
    for t in range(t_len):
        ytt = jnp.concatenate([yt_ref[g, ch * t:ch * (t + 1), :] for g in range(S5_GB)], axis=0)
        rows = pl.ds(t, nck, stride=t_len)
        y_ref[rows, :] = ytt.T + d_ref[...] * u_ref[rows, :]


def _s5(cu, mt, wet, wyt, pw_re, pw_im, d_row, layer):
    p4 = 4 * C_STATE
    tok = pl.BlockSpec((SEQ, LANES), lambda gb, b: (b, gb))
    per_gb = lambda *s: pl.BlockSpec((1,) + s, lambda gb, b: (layer * S5_NGB + gb,) + (0,) * len(s))
    state = pltpu.VMEM((S5_NCHUNK, S5_GB * C_STATE), F32)
    return pl.pallas_call(
        _s5_kernel,
        out_shape=jax.ShapeDtypeStruct((TOKENS, C_WIDTH), F32),
        grid=(S5_NGB, BATCH),
        in_specs=[tok,
                  per_gb(S5_GB, S5_TC, S5_TC),
                  per_gb(S5_PAIRS_PER_GB, p4, 2 * S5_TC),
                  per_gb(S5_PAIRS_PER_GB, 2 * S5_TC, p4),
                  per_gb(S5_PAIRS_PER_GB, 8, 2 * C_STATE), per_gb(S5_PAIRS_PER_GB, 8, 2 * C_STATE),
                  pl.BlockSpec((1, LANES), lambda gb, b: (0, gb))],
        out_specs=tok,
        scratch_shapes=[pltpu.VMEM((S5_GB, S5_TC, S5_NCHUNK), F32),
                        pltpu.VMEM((S5_GB, S5_TC, S5_NCHUNK), F32),
                        state, state, state, state],
        compiler_params=_cparams(("arbitrary", "arbitrary")),
        name="s5",
    )(cu, mt, wet, wyt, pw_re, pw_im, d_row)


def _gelu_tanh(x):
    return 0.5 * x * (1.0 + jnp.tanh(math.sqrt(2.0 / math.pi) * (x + 0.044715 * (x * x * x))))


def _out_kernel(oa_ref, ob_ref, yc_ref, cg_ref, x_ref, gate_ref, gpost_ref, wglu_ref, bglu_ref, wout_ref, o_ref):
    y = _gelu_tanh(yc_ref[...])
    z = jnp.dot(y.astype(BF16), wglu_ref[...], preferred_element_type=F32) + bglu_ref[...]
    y = y * jax.nn.sigmoid(z)
    oc = (y * _silu(cg_ref[...].astype(F32))).astype(BF16)
    mix = jnp.concatenate([oa_ref[...], ob_ref[...], oc], axis=1)
    acc = jnp.dot(mix, wout_ref[...], preferred_element_type=F32)
    ms = jnp.mean(acc * acc, axis=-1, keepdims=True)
    out = acc * lax.rsqrt(ms + EPS) * gpost_ref[...]
    o_ref[...] = x_ref[...] + gate_ref[0] * out


def _out(oa, ob, yc, cg, x2, gate, g_post, w_glu, b_glu, w_out, layer):
    tm = OUT_TM
    steps_per_batch = SEQ // tm
    row = lambda i: (i, 0)
    const = lambda i: (0, 0)
    return pl.pallas_call(
        _out_kernel,
        out_shape=jax.ShapeDtypeStruct((TOKENS, D_MODEL), F32),
        grid=(TOKENS // tm,),
        in_specs=[pl.BlockSpec((tm, A_WIDTH), row),
                  pl.BlockSpec((tm, B_WIDTH), row),
                  pl.BlockSpec((tm, C_WIDTH), row),
                  pl.BlockSpec((tm, C_WIDTH), row),
                  pl.BlockSpec((tm, D_MODEL), row),
                  pl.BlockSpec((1, 1, D_MODEL), lambda i: (i // steps_per_batch, 0, 0)),
                  pl.BlockSpec((1, D_MODEL), const),
                  pl.BlockSpec((C_WIDTH, C_WIDTH), lambda i: (layer, 0)),
                  pl.BlockSpec((1, C_WIDTH), const),
                  pl.BlockSpec((2 * D_MODEL, D_MODEL), lambda i: (layer, 0))],
        out_specs=pl.BlockSpec((tm, D_MODEL), row),
        compiler_params=_cparams(("arbitrary",)),
        name="out",
    )(oa, ob, yc, cg, x2, gate, g_post.reshape(1, D_MODEL), w_glu, b_glu.reshape(1, C_WIDTH), w_out)


def kernel(x, c, w_mod, b_mod, g_pre, g_post, w_in, attn_sinks, gla_w_alpha, gla_b_alpha, gla_norm_g,
           s5_a_re, s5_a_im, s5_log_dt, s5_b_re, s5_b_im, s5_c_re, s5_c_im, s5_d, s5_w_glu, s5_b_glu, w_out):
    layers = w_mod.shape[0]
    x2 = x.reshape(TOKENS, D_MODEL)
    bias = _attn_bias()
    mod = _mod(jnp.pad(c, ((0, 8 - BATCH), (0, 0))), w_mod, b_mod)[:, :BATCH]
    shift, scale, gate = (m.reshape(layers, BATCH, 1, D_MODEL) for m in jnp.split(mod, 3, axis=-1))
    w_t = jnp.swapaxes(w_in, 1, 2).astype(BF16).reshape(layers * W_IN_COLS, D_MODEL)
    w_alpha_pad = jnp.pad(gla_w_alpha, ((0, 0), (0, LR_PAD - B_GATE_RANK), (0, 0))).astype(BF16)
    s5_ops = _s5prep(s5_a_re, s5_a_im, s5_log_dt, s5_b_re, s5_b_im, s5_c_re, s5_c_im, s5_d)
    *s5_ops, s5_d_rows = s5_ops
    w_glu = s5_w_glu.astype(BF16).reshape(layers * C_WIDTH, C_WIDTH)
    w_out_b = w_out.astype(BF16).reshape(layers * 2 * D_MODEL, D_MODEL)
    for l in range(layers):
        ak, ag, bq, bk, bv, bg, cu, cg, log_a, aqt, avt = _proj(
            x2, scale[l], shift[l], g_pre[l], w_t, l, w_alpha_pad[l], gla_b_alpha[l])
        o_a = _attn(attn_sinks[l], aqt, ak, avt, ag, bias)
        o_b = _gla(log_a, bq, bk, bv, bg, gla_norm_g[l])
        y_c = _s5(cu, *s5_ops, s5_d_rows[l], l)
        x2 = _out(o_a, o_b, y_c, cg, x2, gate[l], g_post[l], w_glu, s5_b_glu[l], w_out_b, l)
    return x2.reshape(x.shape)
```

```python
import math

import jax
import jax.numpy as jnp
import numpy as np
from jax import lax
from jax.experimental import pallas as pl
from jax.experimental.pallas import tpu as pltpu

F32 = jnp.float32
BF16 = jnp.bfloat16
HIGHEST = lax.Precision.HIGHEST

D_MODEL = 1024
BATCH = 4
SEQ = 4096
TOKENS = BATCH * SEQ
EPS = 1e-6

A_WIDTH = 1024
A_HEAD_DIM = 64
A_Q_HEADS = 16
A_KV_HEADS = 4
A_KV_WIDTH = A_KV_HEADS * A_HEAD_DIM
A_BLOCK = 128
WINDOW = 128

B_WIDTH = 512
B_HEADS = 4
B_DK = 64
B_DV = 128
B_QK_WIDTH = 256
B_GATE_RANK = 16
B_GATE_TAU = 16.0
GLA_BLOCK = 64
GLA_SUB = 4
ATTN_QB = 4
C_WIDTH = 512
C_GROUP_CH = 16
C_GROUPS = 32
C_STATE = 64
S5_CHUNK = 16
S5_NCHUNK = SEQ // S5_CHUNK
S5_TC = S5_CHUNK * C_GROUP_CH
S5_GB = 8
S5_NGB = C_GROUPS // S5_GB
S5_PAIRS_PER_GB = S5_GB // 2
S5_POW_ROWS = 24

LANES = 128
LR_PAD = LANES

V7X_VMEM_LIMIT = 56 * 1024 * 1024

PROJ_TM = 512
OUT_TM = 1024

_W_IN_SIZES = (("aq", A_WIDTH), ("ak", A_KV_WIDTH), ("av", A_KV_WIDTH), ("ag", A_WIDTH), ("bq", B_QK_WIDTH),
               ("bk", B_QK_WIDTH), ("bv", B_WIDTH), ("blr", B_GATE_RANK), ("bg", B_WIDTH), ("cu", C_WIDTH),
               ("cg", C_WIDTH))
_W_IN_OFF = {}
_off = 0
for _name, _w in _W_IN_SIZES:
    _W_IN_OFF[_name] = (_off, _w)
    _off += _w
W_IN_COLS = _off
_PROJ_OUTS = (("ak", BF16), ("ag", BF16), ("bq", BF16), ("bk", BF16),
              ("bv", BF16), ("bg", BF16), ("cu", F32), ("cg", BF16))


def _silu(x):
    return x * jax.nn.sigmoid(x)


def _cparams(semantics):
    return pltpu.CompilerParams(dimension_semantics=semantics, vmem_limit_bytes=V7X_VMEM_LIMIT)


def _mod_kernel(c_ref, w_ref, b_ref, o_ref):
    c = c_ref[...]
    o_ref[0] = jnp.dot(_silu(c).astype(BF16), w_ref[0].astype(BF16), preferred_element_type=F32) + b_ref[0]


def _mod(c_pad, w_mod, b_mod):
    layers = w_mod.shape[0]
    n = 3 * D_MODEL
    tn = 768
    return pl.pallas_call(
        _mod_kernel,
        out_shape=jax.ShapeDtypeStruct((layers, 8, n), F32),
        grid=(layers, n // tn),
        in_specs=[pl.BlockSpec((8, D_MODEL), lambda l, j: (0, 0)),
                  pl.BlockSpec((1, D_MODEL, tn), lambda l, j: (l, 0, j)),
                  pl.BlockSpec((1, 1, tn), lambda l, j: (l, 0, j))],
        out_specs=pl.BlockSpec((1, 8, tn), lambda l, j: (l, 0, j)),
        compiler_params=_cparams(("arbitrary", "arbitrary")),
        name="mod",
    )(c_pad, w_mod, b_mod.reshape(layers, 1, n))


def _proj_kernel(x_ref, scale_ref, shift_ref, gpre_ref, wt_ref, walpha_ref, balpha_ref, *out_refs):
    x = x_ref[...]
    ms = jnp.mean(x * x, axis=-1, keepdims=True)
    y = x * lax.rsqrt(ms + EPS) * gpre_ref[...]
    h = (y * (1.0 + scale_ref[0]) + shift_ref[0]).astype(BF16)
    nt = (((1,), (1,)), ((), ()))

    def rows(name, width=None):
        off, w = _W_IN_OFF[name]
        return wt_ref[off:off + (width or w), :]

    for (name, _), o_ref in zip(_PROJ_OUTS, out_refs):
        o_ref[...] = lax.dot_general(h, rows(name), nt, preferred_element_type=F32).astype(o_ref.dtype)
    lr = lax.dot_general(h, rows("blr", LR_PAD), nt, preferred_element_type=F32).astype(BF16)
    logits = jnp.dot(lr, walpha_ref[...], preferred_element_type=F32) + balpha_ref[...]
    log_sig = jnp.minimum(logits, 0.0) - jnp.log(1.0 + jnp.exp(-jnp.abs(logits)))
    la_ref, qt_ref, vt_ref = out_refs[len(_PROJ_OUTS):]
    la_ref[...] = log_sig * (1.0 / B_GATE_TAU)
    qt = lax.dot_general(rows("aq"), h, nt, preferred_element_type=F32) * (A_HEAD_DIM ** -0.5)
    qt_ref[...] = qt.astype(qt_ref.dtype)
    vt_ref[...] = lax.dot_general(rows("av"), h, nt, preferred_element_type=F32).astype(vt_ref.dtype)


def _proj(x2, scale, shift, g_pre, w_t, layer, w_alpha_pad, b_alpha):
    tm = PROJ_TM
    steps_per_batch = SEQ // tm
    row = lambda i: (i, 0)
    col = lambda i: (0, i)
    per_batch = lambda i: (i // steps_per_batch, 0, 0)
    const = lambda i: (0, 0)
    out_shape = [jax.ShapeDtypeStruct((TOKENS, _W_IN_OFF[n][1]), dt) for n, dt in _PROJ_OUTS]
    out_specs = [pl.BlockSpec((tm, _W_IN_OFF[n][1]), row) for n, _ in _PROJ_OUTS]
    out_shape += [jax.ShapeDtypeStruct((TOKENS, B_QK_WIDTH), F32),
                  jax.ShapeDtypeStruct((A_WIDTH, TOKENS), BF16),
                  jax.ShapeDtypeStruct((A_KV_WIDTH, TOKENS), BF16)]
    out_specs += [pl.BlockSpec((tm, B_QK_WIDTH), row),
                  pl.BlockSpec((A_WIDTH, tm), col),
                  pl.BlockSpec((A_KV_WIDTH, tm), col)]
    return pl.pallas_call(
        _proj_kernel,
        out_shape=out_shape,
        grid=(TOKENS // tm,),
        in_specs=[pl.BlockSpec((tm, D_MODEL), row),
                  pl.BlockSpec((1, 1, D_MODEL), per_batch),
                  pl.BlockSpec((1, 1, D_MODEL), per_batch),
                  pl.BlockSpec((1, D_MODEL), const),
                  pl.BlockSpec((W_IN_COLS, D_MODEL), lambda i: (layer, 0)),
                  pl.BlockSpec((LR_PAD, B_QK_WIDTH), const),
                  pl.BlockSpec((1, B_QK_WIDTH), const)],
        out_specs=out_specs,
        compiler_params=_cparams(("arbitrary",)),
        name="proj",
    )(x2, scale, shift, g_pre.reshape(1, D_MODEL), w_t, w_alpha_pad, b_alpha.reshape(1, B_QK_WIDTH))


def _attn_bias():
    j = np.arange(A_BLOCK)[:, None]
    i = np.arange(A_BLOCK)[None, :]
    dist = np.where(j > i, i + A_BLOCK - j, i - j).astype(np.float32)
    slopes = np.exp2(-8.0 * np.arange(1, A_Q_HEADS + 1, dtype=np.float32) / A_Q_HEADS).astype(np.float32)
    bias = -slopes[:, None, None] * dist[None]
    first = np.where((j > i)[None], -np.inf, bias).astype(np.float32)
    return jnp.asarray(np.stack([bias, first]))


def _attn_kernel(sink_ref, qt_ref, kp_ref, kc_ref, vtp_ref, vtc_ref, g_ref, bias_ref, o_ref):
    first_step = pl.program_id(1) == 0
    kj = lax.broadcasted_iota(jnp.int32, (A_BLOCK, A_BLOCK), 0)
    qi = lax.broadcasted_iota(jnp.int32, (A_BLOCK, A_BLOCK), 1)
    from_prev = kj > qi
    zero_rows = jnp.zeros((A_HEAD_DIM, A_BLOCK), BF16)
    group = A_Q_HEADS // A_KV_HEADS
    blk = lambda j: slice(A_BLOCK * j, A_BLOCK * (j + 1))

    def keys(j, sl):
        prev = kp_ref[:, sl] if j == 0 else kc_ref[blk(j - 1), sl]
        return prev, kc_ref[blk(j), sl]

    def values(j, rows):
        prev = vtp_ref[rows, :] if j == 0 else vtc_ref[rows, blk(j - 1)]
        return jnp.concatenate([prev, vtc_ref[rows, blk(j)]], axis=1)

    def scores(j, hd):
        kvh = hd // group
        sl = slice(LANES * (kvh // 2), LANES * (kvh // 2 + 1))
        qh = qt_ref[A_HEAD_DIM * hd:A_HEAD_DIM * (hd + 1), blk(j)]
        qsel = jnp.concatenate([qh, zero_rows] if kvh % 2 == 0 else [zero_rows, qh], axis=0)
        k_prev, k_cur = keys(j, sl)
        return (jnp.dot(k_prev, qsel, preferred_element_type=F32),
                jnp.dot(k_cur, qsel, preferred_element_type=F32))

    def attend(j, hd, s_prev, s_cur):
        kvh = hd // group
        v_both = values(j, slice(A_HEAD_DIM * kvh, A_HEAD_DIM * (kvh + 1)))
        table = jnp.where(first_step, 1, 0) if j == 0 else 0
        s = jnp.where(from_prev, s_prev, s_cur) + bias_ref[table, hd]
        sink = sink_ref[hd]
        m = jnp.maximum(jnp.max(s, axis=0, keepdims=True), sink)
        p = jnp.exp(s - m)
        den = jnp.sum(p, axis=0, keepdims=True) + jnp.exp(sink - m)
        p_both = jnp.concatenate([jnp.where(from_prev, p, 0.0), jnp.where(from_prev, 0.0, p)],
                                 axis=0).astype(BF16)
        return jnp.dot(v_both, p_both, preferred_element_type=F32) / den

    pending = [scores(0, hd) for hd in range(A_Q_HEADS)]
    for j in range(ATTN_QB):
        current = pending
        if j + 1 < ATTN_QB:
            pending = [scores(j + 1, hd) for hd in range(A_Q_HEADS)]
        outs = {}
        for hd in range(A_Q_HEADS):
            outs[hd] = attend(j, hd, *current[hd])
            if hd % 2 == 1:
                qsl = slice(LANES * (hd // 2), LANES * (hd // 2 + 1))
                o_pair = jnp.concatenate([outs.pop(hd - 1), outs.pop(hd)], axis=0).T
                gate = g_ref[blk(j), qsl].astype(F32)
                o_ref[blk(j), qsl] = (o_pair * _silu(gate)).astype(o_ref.dtype)


def _attn(sinks, qt, k, vt, ag, bias):
    qb = ATTN_QB
    steps = SEQ // (A_BLOCK * qb)
    cur = lambda b, n: (b * steps + n, 0)
    cur_t = lambda b, n: (0, b * steps + n)
    prev = lambda b, n: ((b * steps + n) * qb - jnp.minimum(n, 1), 0)
    prev_t = lambda b, n: (0, (b * steps + n) * qb - jnp.minimum(n, 1))
    return pl.pallas_call(
        _attn_kernel,
        out_shape=jax.ShapeDtypeStruct((TOKENS, A_WIDTH), BF16),
        grid=(BATCH, steps),
        in_specs=[pl.BlockSpec(memory_space=pltpu.SMEM),
                  pl.BlockSpec((A_WIDTH, A_BLOCK * qb), cur_t),
                  pl.BlockSpec((A_BLOCK, A_KV_WIDTH), prev),
                  pl.BlockSpec((A_BLOCK * qb, A_KV_WIDTH), cur),
                  pl.BlockSpec((A_KV_WIDTH, A_BLOCK), prev_t),
                  pl.BlockSpec((A_KV_WIDTH, A_BLOCK * qb), cur_t),
                  pl.BlockSpec((A_BLOCK * qb, A_WIDTH), cur),
                  pl.BlockSpec((2, A_Q_HEADS, A_BLOCK, A_BLOCK), lambda b, n: (0, 0, 0, 0))],
        out_specs=pl.BlockSpec((A_BLOCK * qb, A_WIDTH), cur),
        compiler_params=_cparams(("arbitrary", "arbitrary")),
        name="attn",
    )(sinks, qt, k, k, vt, vt, ag, bias)


def _gla_kernel(la_ref, q_ref, k_ref, v_ref, g_ref, gn_ref, o_ref, st_ref):
    cb = GLA_BLOCK

    @pl.when(pl.program_id(0) == 0)
    def _():
        st_ref[...] = jnp.zeros_like(st_ref)

    r = lax.broadcasted_iota(jnp.int32, (cb, cb), 0)
    c = lax.broadcasted_iota(jnp.int32, (cb, cb), 1)
    tri = (c <= r).astype(F32)
    lane = lax.broadcasted_iota(jnp.int32, (cb, B_QK_WIDTH), 1)
    head_masks = [(lane >= B_DK * h) & (lane < B_DK * (h + 1)) for h in range(B_HEADS)]
    rr = lax.broadcasted_iota(jnp.int32, (B_HEADS * cb, cb), 0)
    cc = lax.broadcasted_iota(jnp.int32, (B_HEADS * cb, cb), 1)
    causal = cc <= (rr & (cb - 1))
    nt = (((1,), (1,)), ((), ()))
    tn = (((0,), (0,)), ((), ()))
    rows = lambda u: slice(cb * u, cb * (u + 1))
    items = [(u, b) for u in range(GLA_SUB) for b in range(BATCH)]
    bcs = {(u, b): jnp.dot(tri, la_ref[b, rows(u), :], preferred_element_type=F32, precision=HIGHEST)
           for u, b in items}
    qsts, kss, ksts, decs, vs = {}, {}, {}, {}, {}
    for it in items:
        u, b = it
        bc = bcs[it]
        bl = bc[cb - 1:cb, :]
        q = q_ref[b, rows(u), :].astype(F32) * (B_DK ** -0.5)
        k = k_ref[b, rows(u), :].astype(F32)
        qs = q * jnp.exp(bc)
        kh = k * jnp.exp(bl - bc)
        kss[it] = (k * jnp.exp(-bc)).astype(BF16)
        decs[it] = jnp.exp(bl)
        qsts[it] = jnp.concatenate([jnp.where(m, qs, 0.0) for m in head_masks], axis=0).astype(BF16)
        ksts[it] = jnp.concatenate([jnp.where(m, kh, 0.0) for m in head_masks], axis=0).astype(BF16)
        vs[it] = v_ref[b, rows(u), :]
    a_alls = {it: lax.dot_general(qsts[it], kss[it], nt, preferred_element_type=F32) for it in items}
    upds = {}
    for it in items:
        vst = jnp.concatenate([vs[it][:, B_DV * h:B_DV * (h + 1)] for h in range(B_HEADS)], axis=0)
        upds[it] = lax.dot_general(vst, ksts[it], tn, preferred_element_type=F32)
    st_in = {}
    for b in range(BATCH):
        st = st_ref[b]
        for u in range(GLA_SUB):
            st_in[u, b] = st
            st = st * decs[u, b] + upds[u, b]
        st_ref[b] = st
    oi_alls = {it: lax.dot_general(qsts[it], st_in[it].astype(BF16), nt, preferred_element_type=F32)
               for it in items}
    o_hs = {}
    for it in items:
        a_all = jnp.where(causal, a_alls[it], 0.0).astype(BF16)
        for h in range(B_HEADS):
            o_hs[it, h] = (jnp.dot(a_all[cb * h:cb * (h + 1)], vs[it][:, B_DV * h:B_DV * (h + 1)],
                                   preferred_element_type=F32) + oi_alls[it][cb * h:cb * (h + 1)])
    for it in items:
        u, b = it
        for h in range(B_HEADS):
            vsl = slice(B_DV * h, B_DV * (h + 1))
            o_h = o_hs[it, h]
            ms = jnp.mean(o_h * o_h, axis=-1, keepdims=True)
            o_n = o_h * lax.rsqrt(ms + EPS) * gn_ref[:, vsl]
            gate = g_ref[b, rows(u), vsl].astype(F32)
            o_ref[b, rows(u), vsl] = (o_n * _silu(gate)).astype(o_ref.dtype)


def _gla(log_a, bq, bk, bv, bg, g_gla):
    cb = GLA_BLOCK * GLA_SUB
    blk = lambda w: pl.BlockSpec((BATCH, cb, w), lambda i: (0, i, 0))
    r3 = lambda a: a.reshape(BATCH, SEQ, a.shape[-1])
    out = pl.pallas_call(
        _gla_kernel,
        out_shape=jax.ShapeDtypeStruct((BATCH, SEQ, B_WIDTH), BF16),
        grid=(SEQ // cb,),
        in_specs=[blk(B_QK_WIDTH), blk(B_QK_WIDTH), blk(B_QK_WIDTH), blk(B_WIDTH), blk(B_WIDTH),
                  pl.BlockSpec((1, B_WIDTH), lambda i: (0, 0))],
        out_specs=blk(B_WIDTH),
        scratch_shapes=[pltpu.VMEM((BATCH, B_DV, B_QK_WIDTH), F32)],
        compiler_params=_cparams(("arbitrary",)),
        name="gla",
    )(r3(log_a), r3(bq), r3(bk), r3(bv), r3(bg), g_gla.reshape(1, B_WIDTH))
    return out.reshape(TOKENS, B_WIDTH)


def _s5prep_kernel(ar_ref, ai_ref, ldt_ref, bre_ref, bim_ref, btre_ref, btim_ref, cre_ref, cim_ref,
                   mt_ref, wet_ref, wyt_ref, are_ref, aim_ref):
    p = C_STATE
    wet_re, wet_im, wyt_rows, a_re, a_im = [], [], [], [], []
    for g in range(2):
        kk, e_re, e_im, y_re, y_im, p_re, p_im = _s5_discretise(
            ar_ref[g], ai_ref[g], ldt_ref[g], bre_ref[g], bim_ref[g], btre_ref[g], btim_ref[g],
            cre_ref[g], cim_ref[g])
        pieces = [kk] + [jnp.concatenate([jnp.zeros((C_GROUP_CH * s, C_GROUP_CH), F32),
                                          kk[:S5_TC - C_GROUP_CH * s]], axis=0) for s in range(1, S5_CHUNK)]
        mt_ref[0, g] = jnp.concatenate(pieces, axis=1).astype(BF16)
        e_t = jnp.concatenate([e_re, e_im], axis=1).T
        zero = jnp.zeros((p, S5_TC), F32)
        wet_re.append(jnp.concatenate([e_t[:p], zero] if g == 0 else [zero, e_t[:p]], axis=1))
        wet_im.append(jnp.concatenate([e_t[p:], zero] if g == 0 else [zero, e_t[p:]], axis=1))
        zero = jnp.zeros((S5_TC, p), F32)
        wyt_rows.append(jnp.concatenate([y_re, zero, -y_im, zero] if g == 0 else [zero, y_re, zero, -y_im], axis=1))
        a_re.append(p_re)
        a_im.append(p_im)
    wet_ref[0] = jnp.concatenate(wet_re + wet_im, axis=0).astype(BF16)
    wyt_ref[0] = jnp.concatenate(wyt_rows, axis=0).astype(BF16)
    are_ref[0] = jnp.broadcast_to(jnp.concatenate(a_re, axis=1), (8, 2 * p))
    aim_ref[0] = jnp.broadcast_to(jnp.concatenate(a_im, axis=1), (8, 2 * p))


def _s5_discretise(ar, ai, ldt, b_re, b_im, bt_re16, bt_im16, c_re16, c_im16):
    dt = jnp.exp(ldt)

    def cmul(xr, xi, yr, yi):
        return xr * yr - xi * yi, xr * yi + xi * yr

    kf = lax.broadcasted_iota(jnp.int32, (S5_POW_ROWS, 1), 0).astype(F32)
    mag = jnp.exp(kf * (ar * dt))
    ang = kf * (ai * dt)
    pw_re, pw_im = mag * jnp.cos(ang), mag * jnp.sin(ang)
    abar_re, abar_im = pw_re[1:2], pw_im[1:2]
    den = ar * ar + ai * ai
    num_re = abar_re - 1.0
    f_re = (num_re * ar + abar_im * ai) / den
    f_im = (abar_im * ar - num_re * ai) / den
    g_re, g_im = cmul(pw_re, pw_im, f_re, f_im)

    def pick(which, xr, xi):
        rep = lambda x: jnp.concatenate(
            [jnp.broadcast_to(x[which(i):which(i) + 1], (C_GROUP_CH, C_STATE)) for i in range(S5_CHUNK)], axis=0)
        return rep(xr), rep(xi)

    tile16 = lambda a: jnp.concatenate([a] * S5_CHUNK, axis=0)
    ct_re, ct_im = tile16(c_re16), tile16(c_im16)
    bt_re, bt_im = tile16(bt_re16), tile16(bt_im16)

    w_re, w_im = cmul(*pick(lambda i: i, g_re, g_im), ct_re, ct_im)
    kk = (jnp.dot(w_re, b_re, preferred_element_type=F32, precision=HIGHEST)
          - jnp.dot(w_im, b_im, preferred_element_type=F32, precision=HIGHEST))
    e_re, e_im = cmul(*pick(lambda i: S5_CHUNK - 1 - i, g_re, g_im), bt_re, bt_im)
    y_re, y_im = cmul(*pick(lambda i: i + 1, pw_re, pw_im), ct_re, ct_im)
    return kk, e_re, e_im, y_re, y_im, pw_re[S5_CHUNK:S5_CHUNK + 1], pw_im[S5_CHUNK:S5_CHUNK + 1]


def _s5prep(a_re, a_im, log_dt, b_re, b_im, c_re, c_im, d):
    p, ch = C_STATE, C_GROUP_CH
    layers = a_re.shape[0]
    g = layers * C_GROUPS
    npair = g // 2
    flat = lambda a: a.reshape(g, *a.shape[2:])
    a_re, a_im, log_dt, b_re, b_im, c_re, c_im = map(flat, (a_re, a_im, log_dt, b_re, b_im, c_re, c_im))
    row = lambda a: a.reshape(g, 1, p)
    ldt = jnp.broadcast_to(log_dt[:, None, None], (g, 1, p))
    b_t = lambda a: jnp.swapaxes(a, 1, 2)
    spec = lambda s1, s2: pl.BlockSpec((2, s1, s2), lambda i: (i, 0, 0))
    mt, wet, wyt, pw_re, pw_im = pl.pallas_call(
        _s5prep_kernel,
        out_shape=[jax.ShapeDtypeStruct((npair, 2, S5_TC, S5_TC), BF16),
                   jax.ShapeDtypeStruct((npair, 4 * p, 2 * S5_TC), BF16),
                   jax.ShapeDtypeStruct((npair, 2 * S5_TC, 4 * p), BF16),
                   jax.ShapeDtypeStruct((npair, 8, 2 * p), F32),
                   jax.ShapeDtypeStruct((npair, 8, 2 * p), F32)],
        grid=(npair,),
        in_specs=[spec(1, p), spec(1, p), spec(1, p), spec(p, ch), spec(p, ch),
                  spec(ch, p), spec(ch, p), spec(ch, p), spec(ch, p)],
        out_specs=[pl.BlockSpec((1, 2, S5_TC, S5_TC), lambda i: (i, 0, 0, 0)),
                   pl.BlockSpec((1, 4 * p, 2 * S5_TC), lambda i: (i, 0, 0)),
                   pl.BlockSpec((1, 2 * S5_TC, 4 * p), lambda i: (i, 0, 0)),
                   pl.BlockSpec((1, 8, 2 * p), lambda i: (i, 0, 0)),
                   pl.BlockSpec((1, 8, 2 * p), lambda i: (i, 0, 0))],
        compiler_params=_cparams(("arbitrary",)),
        name="s5prep",
    )(row(a_re), row(a_im), ldt, b_re, b_im, b_t(b_re), b_t(b_im), c_re, c_im)
    by_gb = lambda a: a.reshape(layers * S5_NGB, S5_PAIRS_PER_GB, *a.shape[1:])
    return (mt.reshape(layers * S5_NGB, S5_GB, S5_TC, S5_TC), by_gb(wet), by_gb(wyt), by_gb(pw_re), by_gb(pw_im),
            d.reshape(layers, 1, C_WIDTH))


def _s5_kernel(*refs):
    t_len, ch = S5_CHUNK, C_GROUP_CH
    u_refs = refs[:t_len]
    mt_ref, wet_ref, wyt_ref, are_ref, aim_ref, d_ref, y_ref = refs[t_len:t_len + 7]
    ut_ref, yt_ref, ere_ref, eim_ref, hre_ref, him_ref = refs[t_len + 7:]
    nck = S5_NCHUNK
    gb = pl.program_id(1)
    nt = (((1,), (1,)), ((), ()))
    for t in range(t_len):
        xt = u_refs[t][...].T
        for g in range(S5_GB):
            ut_ref[g, ch * t:ch * (t + 1), :] = xt[ch * g:ch * (g + 1), :]
    for j in range(S5_PAIRS_PER_GB):
        u0 = ut_ref[2 * j].astype(BF16)
        u1 = ut_ref[2 * j + 1].astype(BF16)
        et = jnp.dot(wet_ref[0, j], jnp.concatenate([u0, u1], axis=0), preferred_element_type=F32)
        e = et.T
        ere_ref[:, LANES * j:LANES * (j + 1)] = e[:, :LANES]
        eim_ref[:, LANES * j:LANES * (j + 1)] = e[:, LANES:]
        yt_ref[2 * j] = jnp.dot(mt_ref[0, 2 * j], u0, preferred_element_type=F32)
        yt_ref[2 * j + 1] = jnp.dot(mt_ref[0, 2 * j + 1], u1, preferred_element_type=F32)

    a_re = jnp.concatenate([are_ref[0, j, 0:1, :] for j in range(S5_PAIRS_PER_GB)], axis=1)
    a_im = jnp.concatenate([aim_ref[0, j, 0:1, :] for j in range(S5_PAIRS_PER_GB)], axis=1)

    def body(i, carry):
        h_re, h_im = carry
        hre_ref[pl.ds(i, 1), :] = h_re
        him_ref[pl.ds(i, 1), :] = h_im
        e_re = ere_ref[pl.ds(i, 1), :]
        e_im = eim_ref[pl.ds(i, 1), :]
        return a_re * h_re - a_im * h_im + e_re, a_re * h_im + a_im * h_re + e_im

    zero = jnp.zeros((1, S5_GB * C_STATE), F32)
    lax.fori_loop(0, nck, body, (zero, zero))

    for j in range(S5_PAIRS_PER_GB):
        sl = slice(LANES * j, LANES * (j + 1))
        hp = jnp.concatenate([hre_ref[:, sl], him_ref[:, sl]], axis=1).astype(BF16)
        yi = lax.dot_general(wyt_ref[0, j], hp, nt, preferred_element_type=F32)
        yt_ref[2 * j] += yi[:S5_TC]
        yt_ref[2 * j + 1] += yi[S5_TC:]
    for t in range(t_len):
        ytt = jnp.concatenate([yt_ref[g, ch * t:ch * (t + 1), :] for g in range(S5_GB)], axis=0)
        lanes = pl.ds(pl.multiple_of(t * C_WIDTH + gb * LANES, LANES), LANES)
        y_ref[:, lanes] = ytt.T + d_ref[...] * u_refs[t][...]


def _s5(cu, mt, wet, wyt, pw_re, pw_im, d_row, layer):
    p4 = 4 * C_STATE
    chunk_cols = S5_CHUNK * C_WIDTH
    cu_chunks = cu.reshape(BATCH * S5_NCHUNK, chunk_cols)
    lane_blocks = C_WIDTH // LANES
    at_t = lambda t: pl.BlockSpec((S5_NCHUNK, LANES), lambda b, gb: (b, t * lane_blocks + gb))
    per_gb = lambda *s: pl.BlockSpec((1,) + s, lambda b, gb: (layer * S5_NGB + gb,) + (0,) * len(s))
    state = pltpu.VMEM((S5_NCHUNK, S5_GB * C_STATE), F32)
    y = pl.pallas_call(
        _s5_kernel,
        out_shape=jax.ShapeDtypeStruct((BATCH * S5_NCHUNK, chunk_cols), F32),
        grid=(BATCH, S5_NGB),
        in_specs=[at_t(t) for t in range(S5_CHUNK)] + [
            per_gb(S5_GB, S5_TC, S5_TC),
            per_gb(S5_PAIRS_PER_GB, p4, 2 * S5_TC),
            per_gb(S5_PAIRS_PER_GB, 2 * S5_TC, p4),
            per_gb(S5_PAIRS_PER_GB, 8, 2 * C_STATE), per_gb(S5_PAIRS_PER_GB, 8, 2 * C_STATE),
            pl.BlockSpec((1, LANES), lambda b, gb: (0, gb))],
        out_specs=pl.BlockSpec((S5_NCHUNK, chunk_cols), lambda b, gb: (b, 0)),
        scratch_shapes=[pltpu.VMEM((S5_GB, S5_TC, S5_NCHUNK), F32),
                        pltpu.VMEM((S5_GB, S5_TC, S5_NCHUNK), F32),
                        state, state, state, state],
        compiler_params=_cparams(("arbitrary", "arbitrary")),
        name="s5",
    )(*([cu_chunks] * S5_CHUNK), mt, wet, wyt, pw_re, pw_im, d_row)
    return y.reshape(TOKENS, C_WIDTH)


def _gelu_tanh(x):
    return 0.5 * x * (1.0 + jnp.tanh(math.sqrt(2.0 / math.pi) * (x + 0.044715 * (x * x * x))))


def _out_kernel(oa_ref, ob_ref, yc_ref, cg_ref, x_ref, gate_ref, gpost_ref, wglu_ref, bglu_ref, wout_ref, o_ref):
    y = _gelu_tanh(yc_ref[...])
    z = jnp.dot(y.astype(BF16), wglu_ref[...], preferred_element_type=F32) + bglu_ref[...]
    y = y * jax.nn.sigmoid(z)
    oc = (y * _silu(cg_ref[...].astype(F32))).astype(BF16)
    mix = jnp.concatenate([oa_ref[...], ob_ref[...], oc], axis=1)
    acc = jnp.dot(mix, wout_ref[...], preferred_element_type=F32)
    ms = jnp.mean(acc * acc, axis=-1, keepdims=True)
    out = acc * lax.rsqrt(ms + EPS) * gpost_ref[...]
    o_ref[...] = x_ref[...] + gate_ref[0] * out


def _out(oa, ob, yc, cg, x2, gate, g_post, w_glu, b_glu, w_out, layer):
    tm = OUT_TM
    steps_per_batch = SEQ // tm
    row = lambda i: (i, 0)
    const = lambda i: (0, 0)
    return pl.pallas_call(
        _out_kernel,
        out_shape=jax.ShapeDtypeStruct((TOKENS, D_MODEL), F32),
        grid=(TOKENS // tm,),
        in_specs=[pl.BlockSpec((tm, A_WIDTH), row),
                  pl.BlockSpec((tm, B_WIDTH), row),
                  pl.BlockSpec((tm, C_WIDTH), row),
                  pl.BlockSpec((tm, C_WIDTH), row),
                  pl.BlockSpec((tm, D_MODEL), row),
                  pl.BlockSpec((1, 1, D_MODEL), lambda i: (i // steps_per_batch, 0, 0)),
                  pl.BlockSpec((1, D_MODEL), const),
                  pl.BlockSpec((C_WIDTH, C_WIDTH), lambda i: (layer, 0)),
                  pl.BlockSpec((1, C_WIDTH), const),
                  pl.BlockSpec((2 * D_MODEL, D_MODEL), lambda i: (layer, 0))],
        out_specs=pl.BlockSpec((tm, D_MODEL), row),
        compiler_params=_cparams(("arbitrary",)),
        name="out",
    )(oa, ob, yc, cg, x2, gate, g_post.reshape(1, D_MODEL), w_glu, b_glu.reshape(1, C_WIDTH), w_out)


def kernel(x, c, w_mod, b_mod, g_pre, g_post, w_in, attn_sinks, gla_w_alpha, gla_b_alpha, gla_norm_g,
           s5_a_re, s5_a_im, s5_log_dt, s5_b_re, s5_b_im, s5_c_re, s5_c_im, s5_d, s5_w_glu, s5_b_glu, w_out):
    layers = w_mod.shape[0]
    x2 = x.reshape(TOKENS, D_MODEL)
    bias = _attn_bias()
    mod = _mod(jnp.pad(c, ((0, 8 - BATCH), (0, 0))), w_mod, b_mod)[:, :BATCH]
    shift, scale, gate = (m.reshape(layers, BATCH, 1, D_MODEL) for m in jnp.split(mod, 3, axis=-1))
    w_t = jnp.swapaxes(w_in, 1, 2).astype(BF16).reshape(layers * W_IN_COLS, D_MODEL)
    w_alpha_pad = jnp.pad(gla_w_alpha, ((0, 0), (0, LR_PAD - B_GATE_RANK), (0, 0))).astype(BF16)
    s5_ops = _s5prep(s5_a_re, s5_a_im, s5_log_dt, s5_b_re, s5_b_im, s5_c_re, s5_c_im, s5_d)
    *s5_ops, s5_d_rows = s5_ops
    w_glu = s5_w_glu.astype(BF16).reshape(layers * C_WIDTH, C_WIDTH)
    w_out_b = w_out.astype(BF16).reshape(layers * 2 * D_MODEL, D_MODEL)
    for l in range(layers):
        ak, ag, bq, bk, bv, bg, cu, cg, log_a, aqt, avt = _proj(
            x2, scale[l], shift[l], g_pre[l], w_t, l, w_alpha_pad[l], gla_b_alpha[l])
        o_a = _attn(attn_sinks[l], aqt, ak, avt, ag, bias)
        o_b = _gla(log_a, bq, bk, bv, bg, gla_norm_g[l])
        y_c = _s5(cu, *s5_ops, s5_d_rows[l], l)
        x2 = _out(o_a, o_b, y_c, cg, x2, gate[l], g_post[l], w_glu, s5_b_glu[l], w_out_b, l)
    return x2.reshape(x.shape)
```

```python
import math

import jax
import jax.numpy as jnp
import numpy as np
from jax import lax
from jax.experimental import pallas as pl
from jax.experimental.pallas import tpu as pltpu

F32 = jnp.float32
BF16 = jnp.bfloat16
HIGHEST = lax.Precision.HIGHEST

D_MODEL = 1024
BATCH = 4
SEQ = 4096
TOKENS = BATCH * SEQ
EPS = 1e-6

A_WIDTH = 1024
A_HEAD_DIM = 64
A_Q_HEADS = 16
A_KV_HEADS = 4
A_KV_WIDTH = A_KV_HEADS * A_HEAD_DIM
A_BLOCK = 128
WINDOW = 128

B_WIDTH = 512
B_HEADS = 4
B_DK = 64
B_DV = 128
B_QK_WIDTH = 256
B_GATE_RANK = 16
B_GATE_TAU = 16.0
GLA_BLOCK = 64
GLA_SUB = 4
ATTN_QB = 4
C_WIDTH = 512
C_GROUP_CH = 16
C_GROUPS = 32
C_STATE = 64
S5_CHUNK = 16
S5_NCHUNK = SEQ // S5_CHUNK
S5_TC = S5_CHUNK * C_GROUP_CH
S5_GB = 8
S5_NGB = C_GROUPS // S5_GB
S5_PAIRS_PER_GB = S5_GB // 2
S5_POW_ROWS = 24

LANES = 128
LR_PAD = LANES

V7X_VMEM_LIMIT = 56 * 1024 * 1024

PROJ_TM = 1024
OUT_TM = 1024

_W_IN_SIZES = (("aq", A_WIDTH), ("ak", A_KV_WIDTH), ("av", A_KV_WIDTH), ("ag", A_WIDTH), ("bq", B_QK_WIDTH),
               ("bk", B_QK_WIDTH), ("bv", B_WIDTH), ("blr", B_GATE_RANK), ("bg", B_WIDTH), ("cu", C_WIDTH),
               ("cg", C_WIDTH))
_W_IN_OFF = {}
_off = 0
for _name, _w in _W_IN_SIZES:
    _W_IN_OFF[_name] = (_off, _w)
    _off += _w
W_IN_COLS = _off
_PROJ_OUTS = (("ak", BF16), ("ag", BF16), ("bq", BF16), ("bk", BF16),
              ("bv", BF16), ("bg", BF16), ("cu", F32), ("cg", BF16))


def _silu(x):
    return x * jax.nn.sigmoid(x)


def _cparams(semantics):
    return pltpu.CompilerParams(dimension_semantics=semantics, vmem_limit_bytes=V7X_VMEM_LIMIT)


def _mod_kernel(c_ref, w_ref, b_ref, o_ref):
    c = c_ref[...]
    o_ref[0] = jnp.dot(_silu(c).astype(BF16), w_ref[0].astype(BF16), preferred_element_type=F32) + b_ref[0]


def _mod(c_pad, w_mod, b_mod):
    layers = w_mod.shape[0]
    n = 3 * D_MODEL
    tn = 768
    return pl.pallas_call(
        _mod_kernel,
        out_shape=jax.ShapeDtypeStruct((layers, 8, n), F32),
        grid=(layers, n // tn),
        in_specs=[pl.BlockSpec((8, D_MODEL), lambda l, j: (0, 0)),
                  pl.BlockSpec((1, D_MODEL, tn), lambda l, j: (l, 0, j)),
                  pl.BlockSpec((1, 1, tn), lambda l, j: (l, 0, j))],
        out_specs=pl.BlockSpec((1, 8, tn), lambda l, j: (l, 0, j)),
        compiler_params=_cparams(("arbitrary", "arbitrary")),
        name="mod",
    )(c_pad, w_mod, b_mod.reshape(layers, 1, n))


def _proj_kernel(x_ref, scale_ref, shift_ref, gpre_ref, wt_ref, walpha_ref, balpha_ref, *out_refs):
    x = x_ref[...]
    ms = jnp.mean(x * x, axis=-1, keepdims=True)
    y = x * lax.rsqrt(ms + EPS) * gpre_ref[...]
    h = (y * (1.0 + scale_ref[0]) + shift_ref[0]).astype(BF16)
    nt = (((1,), (1,)), ((), ()))

    def rows(name, width=None):
        off, w = _W_IN_OFF[name]
        return wt_ref[off:off + (width or w), :]

    for (name, _), o_ref in zip(_PROJ_OUTS, out_refs):
        o_ref[...] = lax.dot_general(h, rows(name), nt, preferred_element_type=F32).astype(o_ref.dtype)
    lr = lax.dot_general(h, rows("blr", LR_PAD), nt, preferred_element_type=F32).astype(BF16)
    logits = jnp.dot(lr, walpha_ref[...], preferred_element_type=F32) + balpha_ref[...]
    log_sig = jnp.minimum(logits, 0.0) - jnp.log(1.0 + jnp.exp(-jnp.abs(logits)))
    la_ref, qt_ref, vt_ref = out_refs[len(_PROJ_OUTS):]
    la_ref[...] = log_sig * (1.0 / B_GATE_TAU)
    qt = lax.dot_general(rows("aq"), h, nt, preferred_element_type=F32) * (A_HEAD_DIM ** -0.5)
    qt_ref[...] = qt.astype(qt_ref.dtype)
    vt_ref[...] = lax.dot_general(rows("av"), h, nt, preferred_element_type=F32).astype(vt_ref.dtype)


def _proj(x2, scale, shift, g_pre, w_t, layer, w_alpha_pad, b_alpha):
    tm = PROJ_TM
    steps_per_batch = SEQ // tm
    row = lambda i: (i, 0)
    col = lambda i: (0, i)
    per_batch = lambda i: (i // steps_per_batch, 0, 0)
    const = lambda i: (0, 0)
    out_shape = [jax.ShapeDtypeStruct((TOKENS, _W_IN_OFF[n][1]), dt) for n, dt in _PROJ_OUTS]
    out_specs = [pl.BlockSpec((tm, _W_IN_OFF[n][1]), row) for n, _ in _PROJ_OUTS]
    out_shape += [jax.ShapeDtypeStruct((TOKENS, B_QK_WIDTH), F32),
                  jax.ShapeDtypeStruct((A_WIDTH, TOKENS), BF16),
                  jax.ShapeDtypeStruct((A_KV_WIDTH, TOKENS), BF16)]
    out_specs += [pl.BlockSpec((tm, B_QK_WIDTH), row),
                  pl.BlockSpec((A_WIDTH, tm), col),
                  pl.BlockSpec((A_KV_WIDTH, tm), col)]
    return pl.pallas_call(
        _proj_kernel,
        out_shape=out_shape,
        grid=(TOKENS // tm,),
        in_specs=[pl.BlockSpec((tm, D_MODEL), row),
                  pl.BlockSpec((1, 1, D_MODEL), per_batch),
                  pl.BlockSpec((1, 1, D_MODEL), per_batch),
                  pl.BlockSpec((1, D_MODEL), const),
                  pl.BlockSpec((W_IN_COLS, D_MODEL), lambda i: (layer, 0), pipeline_mode=pl.Buffered(1)),
                  pl.BlockSpec((LR_PAD, B_QK_WIDTH), const),
                  pl.BlockSpec((1, B_QK_WIDTH), const)],
        out_specs=out_specs,
        compiler_params=_cparams(("arbitrary",)),
        name="proj",
    )(x2, scale, shift, g_pre.reshape(1, D_MODEL), w_t, w_alpha_pad, b_alpha.reshape(1, B_QK_WIDTH))


def _attn_bias():
    j = np.arange(A_BLOCK)[:, None]
    i = np.arange(A_BLOCK)[None, :]
    dist = np.where(j > i, i + A_BLOCK - j, i - j).astype(np.float32)
    slopes = np.exp2(-8.0 * np.arange(1, A_Q_HEADS + 1, dtype=np.float32) / A_Q_HEADS).astype(np.float32)
    bias = -slopes[:, None, None] * dist[None]
    first = np.where((j > i)[None], -np.inf, bias).astype(np.float32)
    return jnp.asarray(np.stack([bias, first]))


def _attn_kernel(sink_ref, qt_ref, kp_ref, kc_ref, vtp_ref, vtc_ref, g_ref, bias_ref, o_ref):
    first_step = pl.program_id(1) == 0
    kj = lax.broadcasted_iota(jnp.int32, (A_BLOCK, A_BLOCK), 0)
    qi = lax.broadcasted_iota(jnp.int32, (A_BLOCK, A_BLOCK), 1)
    from_prev = kj > qi
    zero_rows = jnp.zeros((A_HEAD_DIM, A_BLOCK), BF16)
    group = A_Q_HEADS // A_KV_HEADS
    blk = lambda j: slice(A_BLOCK * j, A_BLOCK * (j + 1))

    def keys(j, sl):
        prev = kp_ref[:, sl] if j == 0 else kc_ref[blk(j - 1), sl]
        return prev, kc_ref[blk(j), sl]

    def values(j, rows):
        prev = vtp_ref[rows, :] if j == 0 else vtc_ref[rows, blk(j - 1)]
        return jnp.concatenate([prev, vtc_ref[rows, blk(j)]], axis=1)

    def scores(j, hd):
        kvh = hd // group
        sl = slice(LANES * (kvh // 2), LANES * (kvh // 2 + 1))
        qh = qt_ref[A_HEAD_DIM * hd:A_HEAD_DIM * (hd + 1), blk(j)]
        qsel = jnp.concatenate([qh, zero_rows] if kvh % 2 == 0 else [zero_rows, qh], axis=0)
        k_prev, k_cur = keys(j, sl)
        return (jnp.dot(k_prev, qsel, preferred_element_type=F32),
                jnp.dot(k_cur, qsel, preferred_element_type=F32))

    def attend(j, hd, s_prev, s_cur):
        kvh = hd // group
        v_both = values(j, slice(A_HEAD_DIM * kvh, A_HEAD_DIM * (kvh + 1)))
        table = jnp.where(first_step, 1, 0) if j == 0 else 0
        s = jnp.where(from_prev, s_prev, s_cur) + bias_ref[table, hd]
        sink = sink_ref[hd]
        m = jnp.maximum(jnp.max(s, axis=0, keepdims=True), sink)
        p = jnp.exp(s - m)
        den = jnp.sum(p, axis=0, keepdims=True) + jnp.exp(sink - m)
        p_both = jnp.concatenate([jnp.where(from_prev, p, 0.0), jnp.where(from_prev, 0.0, p)],
                                 axis=0).astype(BF16)
        return jnp.dot(v_both, p_both, preferred_element_type=F32) / den

    pending = [scores(0, hd) for hd in range(A_Q_HEADS)]
    for j in range(ATTN_QB):
        current = pending
        if j + 1 < ATTN_QB:
            pending = [scores(j + 1, hd) for hd in range(A_Q_HEADS)]
        outs = {}
        for hd in range(A_Q_HEADS):
            outs[hd] = attend(j, hd, *current[hd])
            if hd % 2 == 1:
                qsl = slice(LANES * (hd // 2), LANES * (hd // 2 + 1))
                o_pair = jnp.concatenate([outs.pop(hd - 1), outs.pop(hd)], axis=0).T
                gate = g_ref[blk(j), qsl].astype(F32)
                o_ref[blk(j), qsl] = (o_pair * _silu(gate)).astype(o_ref.dtype)


def _attn(sinks, qt, k, vt, ag, bias):
    qb = ATTN_QB
    steps = SEQ // (A_BLOCK * qb)
    cur = lambda b, n: (b * steps + n, 0)
    cur_t = lambda b, n: (0, b * steps + n)
    prev = lambda b, n: ((b * steps + n) * qb - jnp.minimum(n, 1), 0)
    prev_t = lambda b, n: (0, (b * steps + n) * qb - jnp.minimum(n, 1))
    return pl.pallas_call(
        _attn_kernel,
        out_shape=jax.ShapeDtypeStruct((TOKENS, A_WIDTH), BF16),
        grid=(BATCH, steps),
        in_specs=[pl.BlockSpec(memory_space=pltpu.SMEM),
                  pl.BlockSpec((A_WIDTH, A_BLOCK * qb), cur_t),
                  pl.BlockSpec((A_BLOCK, A_KV_WIDTH), prev),
                  pl.BlockSpec((A_BLOCK * qb, A_KV_WIDTH), cur),
                  pl.BlockSpec((A_KV_WIDTH, A_BLOCK), prev_t),
                  pl.BlockSpec((A_KV_WIDTH, A_BLOCK * qb), cur_t),
                  pl.BlockSpec((A_BLOCK * qb, A_WIDTH), cur),
                  pl.BlockSpec((2, A_Q_HEADS, A_BLOCK, A_BLOCK), lambda b, n: (0, 0, 0, 0))],
        out_specs=pl.BlockSpec((A_BLOCK * qb, A_WIDTH), cur),
        compiler_params=_cparams(("arbitrary", "arbitrary")),
        name="attn",
    )(sinks, qt, k, k, vt, vt, ag, bias)


def _gla_kernel(la_ref, q_ref, k_ref, v_ref, g_ref, gn_ref, o_ref, st_ref):
    cb = GLA_BLOCK

    @pl.when(pl.program_id(0) == 0)
    def _():
        st_ref[...] = jnp.zeros_like(st_ref)

    r = lax.broadcasted_iota(jnp.int32, (cb, cb), 0)
    c = lax.broadcasted_iota(jnp.int32, (cb, cb), 1)
    tri = (c <= r).astype(F32)
    lane = lax.broadcasted_iota(jnp.int32, (cb, B_QK_WIDTH), 1)
    head_masks = [(lane >= B_DK * h) & (lane < B_DK * (h + 1)) for h in range(B_HEADS)]
    rr = lax.broadcasted_iota(jnp.int32, (B_HEADS * cb, cb), 0)
    cc = lax.broadcasted_iota(jnp.int32, (B_HEADS * cb, cb), 1)
    causal = cc <= (rr & (cb - 1))
    nt = (((1,), (1,)), ((), ()))
    tn = (((0,), (0,)), ((), ()))
    rows = lambda u: slice(cb * u, cb * (u + 1))
    items = [(u, b) for u in range(GLA_SUB) for b in range(BATCH)]
    bcs = {(u, b): jnp.dot(tri, la_ref[b, rows(u), :], preferred_element_type=F32, precision=HIGHEST)
           for u, b in items}
    qsts, kss, ksts, decs, vs = {}, {}, {}, {}, {}
    for it in items:
        u, b = it
        bc = bcs[it]
        bl = bc[cb - 1:cb, :]
        q = q_ref[b, rows(u), :].astype(F32) * (B_DK ** -0.5)
        k = k_ref[b, rows(u), :].astype(F32)
        qs = q * jnp.exp(bc)
        kh = k * jnp.exp(bl - bc)
        kss[it] = (k * jnp.exp(-bc)).astype(BF16)
        decs[it] = jnp.exp(bl)
        qsts[it] = jnp.concatenate([jnp.where(m, qs, 0.0) for m in head_masks], axis=0).astype(BF16)
        ksts[it] = jnp.concatenate([jnp.where(m, kh, 0.0) for m in head_masks], axis=0).astype(BF16)
        vs[it] = v_ref[b, rows(u), :]
    a_alls = {it: lax.dot_general(qsts[it], kss[it], nt, preferred_element_type=F32) for it in items}
    upds = {}
    for it in items:
        vst = jnp.concatenate([vs[it][:, B_DV * h:B_DV * (h + 1)] for h in range(B_HEADS)], axis=0)
        upds[it] = lax.dot_general(vst, ksts[it], tn, preferred_element_type=F32)
    st_in = {}
    for b in range(BATCH):
        st = st_ref[b]
        for u in range(GLA_SUB):
            st_in[u, b] = st
            st = st * decs[u, b] + upds[u, b]
        st_ref[b] = st
    oi_alls = {it: lax.dot_general(qsts[it], st_in[it].astype(BF16), nt, preferred_element_type=F32)
               for it in items}
    o_hs = {}
    for it in items:
        a_all = jnp.where(causal, a_alls[it], 0.0).astype(BF16)
        for h in range(B_HEADS):
            o_hs[it, h] = (jnp.dot(a_all[cb * h:cb * (h + 1)], vs[it][:, B_DV * h:B_DV * (h + 1)],
                                   preferred_element_type=F32) + oi_alls[it][cb * h:cb * (h + 1)])
    for it in items:
        u, b = it
        for h in range(B_HEADS):
            vsl = slice(B_DV * h, B_DV * (h + 1))
            o_h = o_hs[it, h]
            ms = jnp.mean(o_h * o_h, axis=-1, keepdims=True)
            o_n = o_h * lax.rsqrt(ms + EPS) * gn_ref[:, vsl]
            gate = g_ref[b, rows(u), vsl].astype(F32)
            o_ref[b, rows(u), vsl] = (o_n * _silu(gate)).astype(o_ref.dtype)


def _gla(log_a, bq, bk, bv, bg, g_gla):
    cb = GLA_BLOCK * GLA_SUB
    blk = lambda w: pl.BlockSpec((BATCH, cb, w), lambda i: (0, i, 0))
    r3 = lambda a: a.reshape(BATCH, SEQ, a.shape[-1])
    out = pl.pallas_call(
        _gla_kernel,
        out_shape=jax.ShapeDtypeStruct((BATCH, SEQ, B_WIDTH), BF16),
        grid=(SEQ // cb,),
        in_specs=[blk(B_QK_WIDTH), blk(B_QK_WIDTH), blk(B_QK_WIDTH), blk(B_WIDTH), blk(B_WIDTH),
                  pl.BlockSpec((1, B_WIDTH), lambda i: (0, 0))],
        out_specs=blk(B_WIDTH),
        scratch_shapes=[pltpu.VMEM((BATCH, B_DV, B_QK_WIDTH), F32)],
        compiler_params=_cparams(("arbitrary",)),
        name="gla",
    )(r3(log_a), r3(bq), r3(bk), r3(bv), r3(bg), g_gla.reshape(1, B_WIDTH))
    return out.reshape(TOKENS, B_WIDTH)


def _s5prep_kernel(ar_ref, ai_ref, ldt_ref, bre_ref, bim_ref, btre_ref, btim_ref, cre_ref, cim_ref,
                   mt_ref, wet_ref, wyt_ref, are_ref, aim_ref):
    p = C_STATE
    wet_re, wet_im, wyt_rows, a_re, a_im = [], [], [], [], []
    for g in range(2):
        kk, e_re, e_im, y_re, y_im, p_re, p_im = _s5_discretise(
            ar_ref[g], ai_ref[g], ldt_ref[g], bre_ref[g], bim_ref[g], btre_ref[g], btim_ref[g],
            cre_ref[g], cim_ref[g])
        pieces = [kk] + [jnp.concatenate([jnp.zeros((C_GROUP_CH * s, C_GROUP_CH), F32),
                                          kk[:S5_TC - C_GROUP_CH * s]], axis=0) for s in range(1, S5_CHUNK)]
        mt_ref[0, g] = jnp.concatenate(pieces, axis=1).astype(BF16)
        e_t = jnp.concatenate([e_re, e_im], axis=1).T
        zero = jnp.zeros((p, S5_TC), F32)
        wet_re.append(jnp.concatenate([e_t[:p], zero] if g == 0 else [zero, e_t[:p]], axis=1))
        wet_im.append(jnp.concatenate([e_t[p:], zero] if g == 0 else [zero, e_t[p:]], axis=1))
        zero = jnp.zeros((S5_TC, p), F32)
        wyt_rows.append(jnp.concatenate([y_re, zero, -y_im, zero] if g == 0 else [zero, y_re, zero, -y_im], axis=1))
        a_re.append(p_re)
        a_im.append(p_im)
    wet_ref[0] = jnp.concatenate(wet_re + wet_im, axis=0).astype(BF16)
    wyt_ref[0] = jnp.concatenate(wyt_rows, axis=0).astype(BF16)
    are_ref[0] = jnp.broadcast_to(jnp.concatenate(a_re, axis=1), (8, 2 * p))
    aim_ref[0] = jnp.broadcast_to(jnp.concatenate(a_im, axis=1), (8, 2 * p))


def _s5_discretise(ar, ai, ldt, b_re, b_im, bt_re16, bt_im16, c_re16, c_im16):
    dt = jnp.exp(ldt)

    def cmul(xr, xi, yr, yi):
        return xr * yr - xi * yi, xr * yi + xi * yr

    kf = lax.broadcasted_iota(jnp.int32, (S5_POW_ROWS, 1), 0).astype(F32)
    mag = jnp.exp(kf * (ar * dt))
    ang = kf * (ai * dt)
    pw_re, pw_im = mag * jnp.cos(ang), mag * jnp.sin(ang)
    abar_re, abar_im = pw_re[1:2], pw_im[1:2]
    den = ar * ar + ai * ai
    num_re = abar_re - 1.0
    f_re = (num_re * ar + abar_im * ai) / den
    f_im = (abar_im * ar - num_re * ai) / den
    g_re, g_im = cmul(pw_re, pw_im, f_re, f_im)

    def pick(which, xr, xi):
        rep = lambda x: jnp.concatenate(
            [jnp.broadcast_to(x[which(i):which(i) + 1], (C_GROUP_CH, C_STATE)) for i in range(S5_CHUNK)], axis=0)
        return rep(xr), rep(xi)

    tile16 = lambda a: jnp.concatenate([a] * S5_CHUNK, axis=0)
    ct_re, ct_im = tile16(c_re16), tile16(c_im16)
    bt_re, bt_im = tile16(bt_re16), tile16(bt_im16)

    w_re, w_im = cmul(*pick(lambda i: i, g_re, g_im), ct_re, ct_im)
    kk = (jnp.dot(w_re, b_re, preferred_element_type=F32, precision=HIGHEST)
          - jnp.dot(w_im, b_im, preferred_element_type=F32, precision=HIGHEST))
    e_re, e_im = cmul(*pick(lambda i: S5_CHUNK - 1 - i, g_re, g_im), bt_re, bt_im)
    y_re, y_im = cmul(*pick(lambda i: i + 1, pw_re, pw_im), ct_re, ct_im)
    return kk, e_re, e_im, y_re, y_im, pw_re[S5_CHUNK:S5_CHUNK + 1], pw_im[S5_CHUNK:S5_CHUNK + 1]


def _s5prep(a_re, a_im, log_dt, b_re, b_im, c_re, c_im, d):
    p, ch = C_STATE, C_GROUP_CH
    layers = a_re.shape[0]
    g = layers * C_GROUPS
    npair = g // 2
    flat = lambda a: a.reshape(g, *a.shape[2:])
    a_re, a_im, log_dt, b_re, b_im, c_re, c_im = map(flat, (a_re, a_im, log_dt, b_re, b_im, c_re, c_im))
    row = lambda a: a.reshape(g, 1, p)
    ldt = jnp.broadcast_to(log_dt[:, None, None], (g, 1, p))
    b_t = lambda a: jnp.swapaxes(a, 1, 2)
    spec = lambda s1, s2: pl.BlockSpec((2, s1, s2), lambda i: (i, 0, 0))
    mt, wet, wyt, pw_re, pw_im = pl.pallas_call(
        _s5prep_kernel,
        out_shape=[jax.ShapeDtypeStruct((npair, 2, S5_TC, S5_TC), BF16),
                   jax.ShapeDtypeStruct((npair, 4 * p, 2 * S5_TC), BF16),
                   jax.ShapeDtypeStruct((npair, 2 * S5_TC, 4 * p), BF16),
                   jax.ShapeDtypeStruct((npair, 8, 2 * p), F32),
                   jax.ShapeDtypeStruct((npair, 8, 2 * p), F32)],
        grid=(npair,),
        in_specs=[spec(1, p), spec(1, p), spec(1, p), spec(p, ch), spec(p, ch),
                  spec(ch, p), spec(ch, p), spec(ch, p), spec(ch, p)],
        out_specs=[pl.BlockSpec((1, 2, S5_TC, S5_TC), lambda i: (i, 0, 0, 0)),
                   pl.BlockSpec((1, 4 * p, 2 * S5_TC), lambda i: (i, 0, 0)),
                   pl.BlockSpec((1, 2 * S5_TC, 4 * p), lambda i: (i, 0, 0)),
                   pl.BlockSpec((1, 8, 2 * p), lambda i: (i, 0, 0)),
                   pl.BlockSpec((1, 8, 2 * p), lambda i: (i, 0, 0))],
        compiler_params=_cparams(("arbitrary",)),
        name="s5prep",
    )(row(a_re), row(a_im), ldt, b_re, b_im, b_t(b_re), b_t(b_im), c_re, c_im)
    by_gb = lambda a: a.reshape(layers * S5_NGB, S5_PAIRS_PER_GB, *a.shape[1:])
    return (mt.reshape(layers * S5_NGB, S5_GB, S5_TC, S5_TC), by_gb(wet), by_gb(wyt), by_gb(pw_re), by_gb(pw_im),
            d.reshape(layers, 1, C_WIDTH))


def _s5_kernel(u_ref, mt_ref, wet_ref, wyt_ref, are_ref, aim_ref, d_ref, y_ref,
               ut_ref, yt_ref, ere_ref, eim_ref, hre_ref, him_ref):
    nck, t_len, ch = S5_NCHUNK, S5_CHUNK, C_GROUP_CH
    nt = (((1,), (1,)), ((), ()))
    for t in range(t_len):
        xt = u_ref[pl.ds(t, nck, stride=t_len), :].T
        for g in range(S5_GB):
            ut_ref[g, ch * t:ch * (t + 1), :] = xt[ch * g:ch * (g + 1), :]
    for j in range(S5_PAIRS_PER_GB):
        u0 = ut_ref[2 * j].astype(BF16)
        u1 = ut_ref[2 * j + 1].astype(BF16)
        et = jnp.dot(wet_ref[0, j], jnp.concatenate([u0, u1], axis=0), preferred_element_type=F32)
        e = et.T
        ere_ref[:, LANES * j:LANES * (j + 1)] = e[:, :LANES]
        eim_ref[:, LANES * j:LANES * (j + 1)] = e[:, LANES:]
        yt_ref[2 * j] = jnp.dot(mt_ref[0, 2 * j], u0, preferred_element_type=F32)
        yt_ref[2 * j + 1] = jnp.dot(mt_ref[0, 2 * j + 1], u1, preferred_element_type=F32)

    a_re = jnp.concatenate([are_ref[0, j, 0:1, :] for j in range(S5_PAIRS_PER_GB)], axis=1)
    a_im = jnp.concatenate([aim_ref[0, j, 0:1, :] for j in range(S5_PAIRS_PER_GB)], axis=1)

    def body(i, carry):
        h_re, h_im = carry
        hre_ref[pl.ds(i, 1), :] = h_re
        him_ref[pl.ds(i, 1), :] = h_im
        e_re = ere_ref[pl.ds(i, 1), :]
        e_im = eim_ref[pl.ds(i, 1), :]
        return a_re * h_re - a_im * h_im + e_re, a_re * h_im + a_im * h_re + e_im

    zero = jnp.zeros((1, S5_GB * C_STATE), F32)
    lax.fori_loop(0, nck, body, (zero, zero))

    for j in range(S5_PAIRS_PER_GB):
        sl = slice(LANES * j, LANES * (j + 1))
        hp = jnp.concatenate([hre_ref[:, sl], him_ref[:, sl]], axis=1).astype(BF16)
        yi = lax.dot_general(wyt_ref[0, j], hp, nt, preferred_element_type=F32)
        yt_ref[2 * j] += yi[:S5_TC]
        yt_ref[2 * j + 1] += yi[S5_TC:]
    for t in range(t_len):
        ytt = jnp.concatenate([yt_ref[g, ch * t:ch * (t + 1), :] for g in range(S5_GB)], axis=0)
        rows = pl.ds(t, nck, stride=t_len)
        y_ref[rows, :] = ytt.T + d_ref[...] * u_ref[rows, :]


def _s5(cu, mt, wet, wyt, pw_re, pw_im, d_row, layer):
    p4 = 4 * C_STATE
    tok = pl.BlockSpec((SEQ, LANES), lambda gb, b: (b, gb))
    per_gb = lambda *s: pl.BlockSpec((1,) + s, lambda gb, b: (layer * S5_NGB + gb,) + (0,) * len(s))
    state = pltpu.VMEM((S5_NCHUNK, S5_GB * C_STATE), F32)
    return pl.pallas_call(
        _s5_kernel,
        out_shape=jax.ShapeDtypeStruct((TOKENS, C_WIDTH), F32),
        grid=(S5_NGB, BATCH),
        in_specs=[tok,
                  per_gb(S5_GB, S5_TC, S5_TC),
                  per_gb(S5_PAIRS_PER_GB, p4, 2 * S5_TC),
                  per_gb(S5_PAIRS_PER_GB, 2 * S5_TC, p4),
                  per_gb(S5_PAIRS_PER_GB, 8, 2 * C_STATE), per_gb(S5_PAIRS_PER_GB, 8, 2 * C_STATE),
                  pl.BlockSpec((1, LANES), lambda gb, b: (0, gb))],
        out_specs=tok,
        scratch_shapes=[pltpu.VMEM((S5_GB, S5_TC, S5_NCHUNK), F32),
                        pltpu.VMEM((S5_GB, S5_TC, S5_NCHUNK), F32),
                        state, state, state, state],
        compiler_params=_cparams(("arbitrary", "arbitrary")),
        name="s5",
    )(cu, mt, wet, wyt, pw_re, pw_im, d_row)


def _gelu_tanh(x):
    return 0.5 * x * (1.0 + jnp.tanh(math.sqrt(2.0 / math.pi) * (x + 0.044715 * (x * x * x))))


def _out_kernel(oa_ref, ob_ref, yc_ref, cg_ref, x_ref, gate_ref, gpost_ref, wglu_ref, bglu_ref, wout_ref, o_ref):
    y = _gelu_tanh(yc_ref[...])
    z = jnp.dot(y.astype(BF16), wglu_ref[...], preferred_element_type=F32) + bglu_ref[...]
    y = y * jax.nn.sigmoid(z)
    oc = (y * _silu(cg_ref[...].astype(F32))).astype(BF16)
    mix = jnp.concatenate([oa_ref[...], ob_ref[...], oc], axis=1)
    acc = jnp.dot(mix, wout_ref[...], preferred_element_type=F32)
    ms = jnp.mean(acc * acc, axis=-1, keepdims=True)
    out = acc * lax.rsqrt(ms + EPS) * gpost_ref[...]
    o_ref[...] = x_ref[...] + gate_ref[0] * out


def _out(oa, ob, yc, cg, x2, gate, g_post, w_glu, b_glu, w_out, layer):
    tm = OUT_TM
    steps_per_batch = SEQ // tm
    row = lambda i: (i, 0)
    const = lambda i: (0, 0)
    return pl.pallas_call(
        _out_kernel,
        out_shape=jax.ShapeDtypeStruct((TOKENS, D_MODEL), F32),
        grid=(TOKENS // tm,),
        in_specs=[pl.BlockSpec((tm, A_WIDTH), row),
                  pl.BlockSpec((tm, B_WIDTH), row),
                  pl.BlockSpec((tm, C_WIDTH), row),
                  pl.BlockSpec((tm, C_WIDTH), row),
                  pl.BlockSpec((tm, D_MODEL), row),
                  pl.BlockSpec((1, 1, D_MODEL), lambda i: (i // steps_per_batch, 0, 0)),
                  pl.BlockSpec((1, D_MODEL), const),
                  pl.BlockSpec((C_WIDTH, C_WIDTH), lambda i: (layer, 0)),
                  pl.BlockSpec((1, C_WIDTH), const),
                  pl.BlockSpec((2 * D_MODEL, D_MODEL), lambda i: (layer, 0))],
        out_specs=pl.BlockSpec((tm, D_MODEL), row),
        compiler_params=_cparams(("arbitrary",)),
        name="out",
    )(oa, ob, yc, cg, x2, gate, g_post.reshape(1, D_MODEL), w_glu, b_glu.reshape(1, C_WIDTH), w_out)


def kernel(x, c, w_mod, b_mod, g_pre, g_post, w_in, attn_sinks, gla_w_alpha, gla_b_alpha, gla_norm_g,
           s5_a_re, s5_a_im, s5_log_dt, s5_b_re, s5_b_im, s5_c_re, s5_c_im, s5_d, s5_w_glu, s5_b_glu, w_out):
    layers = w_mod.shape[0]
    x2 = x.reshape(TOKENS, D_MODEL)
    bias = _attn_bias()
    mod = _mod(jnp.pad(c, ((0, 8 - BATCH), (0, 0))), w_mod, b_mod)[:, :BATCH]
    shift, scale, gate = (m.reshape(layers, BATCH, 1, D_MODEL) for m in jnp.split(mod, 3, axis=-1))
    w_t = jnp.swapaxes(w_in, 1, 2).astype(BF16).reshape(layers * W_IN_COLS, D_MODEL)
    w_alpha_pad = jnp.pad(gla_w_alpha, ((0, 0), (0, LR_PAD - B_GATE_RANK), (0, 0))).astype(BF16)
    s5_ops = _s5prep(s5_a_re, s5_a_im, s5_log_dt, s5_b_re, s5_b_im, s5_c_re, s5_c_im, s5_d)
    *s5_ops, s5_d_rows = s5_ops
    w_glu = s5_w_glu.astype(BF16).reshape(layers * C_WIDTH, C_WIDTH)
    w_out_b = w_out.astype(BF16).reshape(layers * 2 * D_MODEL, D_MODEL)
    for l in range(layers):
        ak, ag, bq, bk, bv, bg, cu, cg, log_a, aqt, avt = _proj(
            x2, scale[l], shift[l], g_pre[l], w_t, l, w_alpha_pad[l], gla_b_alpha[l])
        o_a = _attn(attn_sinks[l], aqt, ak, avt, ag, bias)
        o_b = _gla(log_a, bq, bk, bv, bg, gla_norm_g[l])
        y_c = _s5(cu, *s5_ops, s5_d_rows[l], l)
        x2 = _out(o_a, o_b, y_c, cg, x2, gate[l], g_post[l], w_glu, s5_b_glu[l], w_out_b, l)
    return x2.reshape(x.shape)
```

```python
import math

import jax
import jax.numpy as jnp
import numpy as np
from jax import lax
from jax.experimental import pallas as pl
from jax.experimental.pallas import tpu as pltpu

F32 = jnp.float32
BF16 = jnp.bfloat16
HIGHEST = lax.Precision.HIGHEST

D_MODEL = 1024
BATCH = 4
SEQ = 4096
TOKENS = BATCH * SEQ
EPS = 1e-6

A_WIDTH = 1024
A_HEAD_DIM = 64
A_Q_HEADS = 16
A_KV_HEADS = 4
A_KV_WIDTH = A_KV_HEADS * A_HEAD_DIM
A_BLOCK = 128
WINDOW = 128

B_WIDTH = 512
B_HEADS = 4
B_DK = 64
B_DV = 128
B_QK_WIDTH = 256
B_GATE_RANK = 16
B_GATE_TAU = 16.0
GLA_BLOCK = 64
GLA_SUB = 4
GLA_DIAG = 16
GLA_ANCHOR_SEGS = (32, 64)
ATTN_QB = 4
C_WIDTH = 512
C_GROUP_CH = 16
C_GROUPS = 32
C_STATE = 64
S5_CHUNK = 16
S5_NCHUNK = SEQ // S5_CHUNK
S5_TC = S5_CHUNK * C_GROUP_CH
S5_GB = 8
S5_NGB = C_GROUPS // S5_GB
S5_PAIRS_PER_GB = S5_GB // 2
S5_POW_ROWS = 24

LANES = 128
LR_PAD = LANES

V7X_VMEM_LIMIT = 56 * 1024 * 1024

PROJ_TM = 1024
OUT_TM = 1024

_W_IN_SIZES = (("aq", A_WIDTH), ("ak", A_KV_WIDTH), ("av", A_KV_WIDTH), ("ag", A_WIDTH), ("bq", B_QK_WIDTH),
               ("bk", B_QK_WIDTH), ("bv", B_WIDTH), ("blr", B_GATE_RANK), ("bg", B_WIDTH), ("cu", C_WIDTH),
               ("cg", C_WIDTH))
_W_IN_OFF = {}
_off = 0
for _name, _w in _W_IN_SIZES:
    _W_IN_OFF[_name] = (_off, _w)
    _off += _w
W_IN_COLS = _off
_PROJ_OUTS = (("ak", BF16), ("ag", BF16), ("bq", BF16), ("bk", BF16),
              ("bv", BF16), ("bg", BF16), ("cu", F32), ("cg", BF16))


def _silu(x):
    return x * jax.nn.sigmoid(x)


def _cparams(semantics):
    return pltpu.CompilerParams(dimension_semantics=semantics, vmem_limit_bytes=V7X_VMEM_LIMIT)


def _mod_kernel(c_ref, w_ref, b_ref, o_ref):
    c = c_ref[...]
    o_ref[0] = jnp.dot(_silu(c).astype(BF16), w_ref[0].astype(BF16), preferred_element_type=F32) + b_ref[0]


def _mod(c_pad, w_mod, b_mod):
    layers = w_mod.shape[0]
    n = 3 * D_MODEL
    tn = 768
    return pl.pallas_call(
        _mod_kernel,
        out_shape=jax.ShapeDtypeStruct((layers, 8, n), F32),
        grid=(layers, n // tn),
        in_specs=[pl.BlockSpec((8, D_MODEL), lambda l, j: (0, 0)),
                  pl.BlockSpec((1, D_MODEL, tn), lambda l, j: (l, 0, j)),
                  pl.BlockSpec((1, 1, tn), lambda l, j: (l, 0, j))],
        out_specs=pl.BlockSpec((1, 8, tn), lambda l, j: (l, 0, j)),
        compiler_params=_cparams(("arbitrary", "arbitrary")),
        name="mod",
    )(c_pad, w_mod, b_mod.reshape(layers, 1, n))


def _proj_kernel(x_ref, scale_ref, shift_ref, gpre_ref, wt_ref, walpha_ref, balpha_ref, *out_refs):
    x = x_ref[...]
    ms = jnp.mean(x * x, axis=-1, keepdims=True)
    y = x * lax.rsqrt(ms + EPS) * gpre_ref[...]
    h = (y * (1.0 + scale_ref[0]) + shift_ref[0]).astype(BF16)
    nt = (((1,), (1,)), ((), ()))

    def rows(name, width=None):
        off, w = _W_IN_OFF[name]
        return wt_ref[off:off + (width or w), :]

    for (name, _), o_ref in zip(_PROJ_OUTS, out_refs):
        o_ref[...] = lax.dot_general(h, rows(name), nt, preferred_element_type=F32).astype(o_ref.dtype)
    lr = lax.dot_general(h, rows("blr", LR_PAD), nt, preferred_element_type=F32).astype(BF16)
    logits = jnp.dot(lr, walpha_ref[...], preferred_element_type=F32) + balpha_ref[...]
    log_sig = jnp.minimum(logits, 0.0) - jnp.log(1.0 + jnp.exp(-jnp.abs(logits)))
    la_ref, qt_ref, vt_ref = out_refs[len(_PROJ_OUTS):]
    la_ref[...] = log_sig * (1.0 / B_GATE_TAU)
    qt = lax.dot_general(rows("aq"), h, nt, preferred_element_type=F32) * (A_HEAD_DIM ** -0.5)
    qt_ref[...] = qt.astype(qt_ref.dtype)
    vt_ref[...] = lax.dot_general(rows("av"), h, nt, preferred_element_type=F32).astype(vt_ref.dtype)


def _proj(x2, scale, shift, g_pre, w_t, layer, w_alpha_pad, b_alpha):
    tm = PROJ_TM
    steps_per_batch = SEQ // tm
    row = lambda i: (i, 0)
    col = lambda i: (0, i)
    per_batch = lambda i: (i // steps_per_batch, 0, 0)
    const = lambda i: (0, 0)
    out_shape = [jax.ShapeDtypeStruct((TOKENS, _W_IN_OFF[n][1]), dt) for n, dt in _PROJ_OUTS]
    out_specs = [pl.BlockSpec((tm, _W_IN_OFF[n][1]), row) for n, _ in _PROJ_OUTS]
    out_shape += [jax.ShapeDtypeStruct((TOKENS, B_QK_WIDTH), F32),
                  jax.ShapeDtypeStruct((A_WIDTH, TOKENS), BF16),
                  jax.ShapeDtypeStruct((A_KV_WIDTH, TOKENS), BF16)]
    out_specs += [pl.BlockSpec((tm, B_QK_WIDTH), row),
                  pl.BlockSpec((A_WIDTH, tm), col),
                  pl.BlockSpec((A_KV_WIDTH, tm), col)]
    return pl.pallas_call(
        _proj_kernel,
        out_shape=out_shape,
        grid=(TOKENS // tm,),
        in_specs=[pl.BlockSpec((tm, D_MODEL), row),
                  pl.BlockSpec((1, 1, D_MODEL), per_batch),
                  pl.BlockSpec((1, 1, D_MODEL), per_batch),
                  pl.BlockSpec((1, D_MODEL), const),
                  pl.BlockSpec((W_IN_COLS, D_MODEL), lambda i: (layer, 0), pipeline_mode=pl.Buffered(1)),
                  pl.BlockSpec((LR_PAD, B_QK_WIDTH), const),
                  pl.BlockSpec((1, B_QK_WIDTH), const)],
        out_specs=out_specs,
        compiler_params=_cparams(("arbitrary",)),
        name="proj",
    )(x2, scale, shift, g_pre.reshape(1, D_MODEL), w_t, w_alpha_pad, b_alpha.reshape(1, B_QK_WIDTH))


def _attn_bias():
    j = np.arange(A_BLOCK)[:, None]
    i = np.arange(A_BLOCK)[None, :]
    dist = np.where(j > i, i + A_BLOCK - j, i - j).astype(np.float32)
    slopes = np.exp2(-8.0 * np.arange(1, A_Q_HEADS + 1, dtype=np.float32) / A_Q_HEADS).astype(np.float32)
    bias = -slopes[:, None, None] * dist[None]
    first = np.where((j > i)[None], -np.inf, bias).astype(np.float32)
    return jnp.asarray(np.stack([bias, first]))


def _attn_kernel(sink_ref, qt_ref, kp_ref, kc_ref, vtp_ref, vtc_ref, g_ref, bias_ref, o_ref):
    first_step = pl.program_id(1) == 0
    kj = lax.broadcasted_iota(jnp.int32, (A_BLOCK, A_BLOCK), 0)
    qi = lax.broadcasted_iota(jnp.int32, (A_BLOCK, A_BLOCK), 1)
    from_prev = kj > qi
    zero_rows = jnp.zeros((A_HEAD_DIM, A_BLOCK), BF16)
    group = A_Q_HEADS // A_KV_HEADS
    blk = lambda j: slice(A_BLOCK * j, A_BLOCK * (j + 1))

    def keys(j, sl):
        prev = kp_ref[:, sl] if j == 0 else kc_ref[blk(j - 1), sl]
        return prev, kc_ref[blk(j), sl]

    def values(j, rows):
        prev = vtp_ref[rows, :] if j == 0 else vtc_ref[rows, blk(j - 1)]
        return jnp.concatenate([prev, vtc_ref[rows, blk(j)]], axis=1)

    def scores(j, hd):
        kvh = hd // group
        sl = slice(LANES * (kvh // 2), LANES * (kvh // 2 + 1))
        qh = qt_ref[A_HEAD_DIM * hd:A_HEAD_DIM * (hd + 1), blk(j)]
        qsel = jnp.concatenate([qh, zero_rows] if kvh % 2 == 0 else [zero_rows, qh], axis=0)
        k_prev, k_cur = keys(j, sl)
        return (jnp.dot(k_prev, qsel, preferred_element_type=F32),
                jnp.dot(k_cur, qsel, preferred_element_type=F32))

    def attend(j, hd, s_prev, s_cur):
        kvh = hd // group
        v_both = values(j, slice(A_HEAD_DIM * kvh, A_HEAD_DIM * (kvh + 1)))
        table = jnp.where(first_step, 1, 0) if j == 0 else 0
        s = jnp.where(from_prev, s_prev, s_cur) + bias_ref[table, hd]
        sink = sink_ref[hd]
        m = jnp.maximum(jnp.max(s, axis=0, keepdims=True), sink)
        p = jnp.exp(s - m)
        den = jnp.sum(p, axis=0, keepdims=True) + jnp.exp(sink - m)
        p_both = jnp.concatenate([jnp.where(from_prev, p, 0.0), jnp.where(from_prev, 0.0, p)],
                                 axis=0).astype(BF16)
        return jnp.dot(v_both, p_both, preferred_element_type=F32) / den

    pending = [scores(0, hd) for hd in range(A_Q_HEADS)]
    for j in range(ATTN_QB):
        current = pending
        if j + 1 < ATTN_QB:
            pending = [scores(j + 1, hd) for hd in range(A_Q_HEADS)]
        outs = {}
        for hd in range(A_Q_HEADS):
            outs[hd] = attend(j, hd, *current[hd])
            if hd % 2 == 1:
                qsl = slice(LANES * (hd // 2), LANES * (hd // 2 + 1))
                o_pair = jnp.concatenate([outs.pop(hd - 1), outs.pop(hd)], axis=0).T
                gate = g_ref[blk(j), qsl].astype(F32)
                o_ref[blk(j), qsl] = (o_pair * _silu(gate)).astype(o_ref.dtype)


def _attn(sinks, qt, k, vt, ag, bias):
    qb = ATTN_QB
    steps = SEQ // (A_BLOCK * qb)
    cur = lambda b, n: (b * steps + n, 0)
    cur_t = lambda b, n: (0, b * steps + n)
    prev = lambda b, n: ((b * steps + n) * qb - jnp.minimum(n, 1), 0)
    prev_t = lambda b, n: (0, (b * steps + n) * qb - jnp.minimum(n, 1))
    return pl.pallas_call(
        _attn_kernel,
        out_shape=jax.ShapeDtypeStruct((TOKENS, A_WIDTH), BF16),
        grid=(BATCH, steps),
        in_specs=[pl.BlockSpec(memory_space=pltpu.SMEM),
                  pl.BlockSpec((A_WIDTH, A_BLOCK * qb), cur_t),
                  pl.BlockSpec((A_BLOCK, A_KV_WIDTH), prev),
                  pl.BlockSpec((A_BLOCK * qb, A_KV_WIDTH), cur),
                  pl.BlockSpec((A_KV_WIDTH, A_BLOCK), prev_t),
                  pl.BlockSpec((A_KV_WIDTH, A_BLOCK * qb), cur_t),
                  pl.BlockSpec((A_BLOCK * qb, A_WIDTH), cur),
                  pl.BlockSpec((2, A_Q_HEADS, A_BLOCK, A_BLOCK), lambda b, n: (0, 0, 0, 0))],
        out_specs=pl.BlockSpec((A_BLOCK * qb, A_WIDTH), cur),
        compiler_params=_cparams(("arbitrary", "arbitrary")),
        name="attn",
    )(sinks, qt, k, k, vt, vt, ag, bias)


def _gla_kernel(la_ref, q_ref, k_ref, v_ref, g_ref, gn_ref, o_ref, st_ref):
    cb = GLA_BLOCK

    @pl.when(pl.program_id(0) == 0)
    def _():
        st_ref[...] = jnp.zeros_like(st_ref)

    r = lax.broadcasted_iota(jnp.int32, (cb, cb), 0)
    c = lax.broadcasted_iota(jnp.int32, (cb, cb), 1)
    tri = (c <= r).astype(F32)
    lane = lax.broadcasted_iota(jnp.int32, (cb, B_QK_WIDTH), 1)
    head_masks = [(lane >= B_DK * h) & (lane < B_DK * (h + 1)) for h in range(B_HEADS)]
    rr = lax.broadcasted_iota(jnp.int32, (B_HEADS * cb, cb), 0)
    cc = lax.broadcasted_iota(jnp.int32, (B_HEADS * cb, cb), 1)
    causal = cc <= (rr & (cb - 1))
    nt = (((1,), (1,)), ((), ()))
    tn = (((0,), (0,)), ((), ()))
    rows = lambda u: slice(cb * u, cb * (u + 1))
    items = [(u, b) for u in range(GLA_SUB) for b in range(BATCH)]
    bcs = {(u, b): jnp.dot(tri, la_ref[b, rows(u), :], preferred_element_type=F32, precision=HIGHEST)
           for u, b in items}
    stack_heads = lambda a: jnp.concatenate([jnp.where(m, a, 0.0) for m in head_masks], axis=0).astype(BF16)
    row = lax.broadcasted_iota(jnp.int32, (cb, 1), 0)
    levels = []
    for seg in GLA_ANCHOR_SEGS:
        levels.append((seg, (row & (seg - 1)) >= seg // 2, (row & (seg - 1)) < seg // 2))
    region = {}
    ri, cj = rr & (cb - 1), cc
    for seg in GLA_ANCHOR_SEGS:
        same = (ri ^ cj) < seg
        region[seg] = same & ((ri & (seg - 1)) >= seg // 2) & ((cj & (seg - 1)) < seg // 2)
    diag = ((ri ^ cj) < GLA_DIAG) & causal

    def anchor_rows(bc, seg, offset):
        parts = []
        for s0 in range(0, cb, seg):
            a = s0 + offset - 1
            val = bc[a:a + 1, :] if a >= 0 else jnp.zeros((1, B_QK_WIDTH), F32)
            parts.append(jnp.broadcast_to(val, (seg, B_QK_WIDTH)))
        return jnp.concatenate(parts, axis=0)

    q_lv, k_lv, qsts, ksts, decs, vs = {}, {}, {}, {}, {}, {}
    for it in items:
        u, b = it
        bc = bcs[it]
        bl = bc[cb - 1:cb, :]
        q = q_ref[b, rows(u), :].astype(F32) * (B_DK ** -0.5)
        k = k_ref[b, rows(u), :].astype(F32)
        for seg, q_rows, k_rows in levels:
            anc = anchor_rows(bc, seg, seg // 2)
            q_lv[it, seg] = stack_heads(jnp.where(q_rows, q * jnp.exp(bc - anc), 0.0))
            k_lv[it, seg] = jnp.where(k_rows, k * jnp.exp(anc - bc), 0.0).astype(BF16)
        anc = anchor_rows(bc, GLA_DIAG, 0)
        q_lv[it, 0] = stack_heads(q * jnp.exp(bc - anc))
        k_lv[it, 0] = (k * jnp.exp(anc - bc)).astype(BF16)
        qsts[it] = stack_heads(q * jnp.exp(bc))
        ksts[it] = stack_heads(k * jnp.exp(bl - bc))
        decs[it] = jnp.exp(bl)
        vs[it] = v_ref[b, rows(u), :]
    a_alls = {}
    for it in items:
        prod = lambda lv: lax.dot_general(q_lv[it, lv], k_lv[it, lv], nt, preferred_element_type=F32)
        a = jnp.where(diag, prod(0), 0.0)
        for seg in GLA_ANCHOR_SEGS:
            a = jnp.where(region[seg], prod(seg), a)
        a_alls[it] = a.astype(BF16)
    upds = {}
    for it in items:
        vst = jnp.concatenate([vs[it][:, B_DV * h:B_DV * (h + 1)] for h in range(B_HEADS)], axis=0)
        upds[it] = lax.dot_general(vst, ksts[it], tn, preferred_element_type=F32)
    st_in = {}
    for b in range(BATCH):
        st = st_ref[b]
        for u in range(GLA_SUB):
            st_in[u, b] = st
            st = st * decs[u, b] + upds[u, b]
        st_ref[b] = st
    oi_alls = {it: lax.dot_general(qsts[it], st_in[it].astype(BF16), nt, preferred_element_type=F32)
               for it in items}
    o_hs = {}
    for it in items:
        a_all = a_alls[it]
        for h in range(B_HEADS):
            o_hs[it, h] = (jnp.dot(a_all[cb * h:cb * (h + 1)], vs[it][:, B_DV * h:B_DV * (h + 1)],
                                   preferred_element_type=F32) + oi_alls[it][cb * h:cb * (h + 1)])
    for it in items:
        u, b = it
        for h in range(B_HEADS):
            vsl = slice(B_DV * h, B_DV * (h + 1))
            o_h = o_hs[it, h]
            ms = jnp.mean(o_h * o_h, axis=-1, keepdims=True)
            o_n = o_h * lax.rsqrt(ms + EPS) * gn_ref[:, vsl]
            gate = g_ref[b, rows(u), vsl].astype(F32)
            o_ref[b, rows(u), vsl] = (o_n * _silu(gate)).astype(o_ref.dtype)


def _gla(log_a, bq, bk, bv, bg, g_gla):
    cb = GLA_BLOCK * GLA_SUB
    blk = lambda w: pl.BlockSpec((BATCH, cb, w), lambda i: (0, i, 0))
    r3 = lambda a: a.reshape(BATCH, SEQ, a.shape[-1])
    out = pl.pallas_call(
        _gla_kernel,
        out_shape=jax.ShapeDtypeStruct((BATCH, SEQ, B_WIDTH), BF16),
        grid=(SEQ // cb,),
        in_specs=[blk(B_QK_WIDTH), blk(B_QK_WIDTH), blk(B_QK_WIDTH), blk(B_WIDTH), blk(B_WIDTH),
                  pl.BlockSpec((1, B_WIDTH), lambda i: (0, 0))],
        out_specs=blk(B_WIDTH),
        scratch_shapes=[pltpu.VMEM((BATCH, B_DV, B_QK_WIDTH), F32)],
        compiler_params=_cparams(("arbitrary",)),
        name="gla",
    )(r3(log_a), r3(bq), r3(bk), r3(bv), r3(bg), g_gla.reshape(1, B_WIDTH))
    return out.reshape(TOKENS, B_WIDTH)


def _s5prep_kernel(ar_ref, ai_ref, ldt_ref, bre_ref, bim_ref, btre_ref, btim_ref, cre_ref, cim_ref,
                   mt_ref, wet_ref, wyt_ref, are_ref, aim_ref):
    p = C_STATE
    wet_re, wet_im, wyt_rows, a_re, a_im = [], [], [], [], []
    for g in range(2):
        kk, e_re, e_im, y_re, y_im, p_re, p_im = _s5_discretise(
            ar_ref[g], ai_ref[g], ldt_ref[g], bre_ref[g], bim_ref[g], btre_ref[g], btim_ref[g],
            cre_ref[g], cim_ref[g])
        pieces = [kk] + [jnp.concatenate([jnp.zeros((C_GROUP_CH * s, C_GROUP_CH), F32),
                                          kk[:S5_TC - C_GROUP_CH * s]], axis=0) for s in range(1, S5_CHUNK)]
        mt_ref[0, g] = jnp.concatenate(pieces, axis=1).astype(BF16)
        e_t = jnp.concatenate([e_re, e_im], axis=1).T
        zero = jnp.zeros((p, S5_TC), F32)
        wet_re.append(jnp.concatenate([e_t[:p], zero] if g == 0 else [zero, e_t[:p]], axis=1))
        wet_im.append(jnp.concatenate([e_t[p:], zero] if g == 0 else [zero, e_t[p:]], axis=1))
        zero = jnp.zeros((S5_TC, p), F32)
        wyt_rows.append(jnp.concatenate([y_re, zero, -y_im, zero] if g == 0 else [zero, y_re, zero, -y_im], axis=1))
        a_re.append(p_re)
        a_im.append(p_im)
    wet_ref[0] = jnp.concatenate(wet_re + wet_im, axis=0).astype(BF16)
    wyt_ref[0] = jnp.concatenate(wyt_rows, axis=0).astype(BF16)
    are_ref[0] = jnp.broadcast_to(jnp.concatenate(a_re, axis=1), (8, 2 * p))
    aim_ref[0] = jnp.broadcast_to(jnp.concatenate(a_im, axis=1), (8, 2 * p))


def _s5_discretise(ar, ai, ldt, b_re, b_im, bt_re16, bt_im16, c_re16, c_im16):
    dt = jnp.exp(ldt)

    def cmul(xr, xi, yr, yi):
        return xr * yr - xi * yi, xr * yi + xi * yr

    kf = lax.broadcasted_iota(jnp.int32, (S5_POW_ROWS, 1), 0).astype(F32)
    mag = jnp.exp(kf * (ar * dt))
    ang = kf * (ai * dt)
    pw_re, pw_im = mag * jnp.cos(ang), mag * jnp.sin(ang)
    abar_re, abar_im = pw_re[1:2], pw_im[1:2]
    den = ar * ar + ai * ai
    num_re = abar_re - 1.0
    f_re = (num_re * ar + abar_im * ai) / den
    f_im = (abar_im * ar - num_re * ai) / den
    g_re, g_im = cmul(pw_re, pw_im, f_re, f_im)

    def pick(which, xr, xi):
        rep = lambda x: jnp.concatenate(
            [jnp.broadcast_to(x[which(i):which(i) + 1], (C_GROUP_CH, C_STATE)) for i in range(S5_CHUNK)], axis=0)
        return rep(xr), rep(xi)

    tile16 = lambda a: jnp.concatenate([a] * S5_CHUNK, axis=0)
    ct_re, ct_im = tile16(c_re16), tile16(c_im16)
    bt_re, bt_im = tile16(bt_re16), tile16(bt_im16)

    w_re, w_im = cmul(*pick(lambda i: i, g_re, g_im), ct_re, ct_im)
    kk = (jnp.dot(w_re, b_re, preferred_element_type=F32, precision=HIGHEST)
          - jnp.dot(w_im, b_im, preferred_element_type=F32, precision=HIGHEST))
    e_re, e_im = cmul(*pick(lambda i: S5_CHUNK - 1 - i, g_re, g_im), bt_re, bt_im)
    y_re, y_im = cmul(*pick(lambda i: i + 1, pw_re, pw_im), ct_re, ct_im)
    return kk, e_re, e_im, y_re, y_im, pw_re[S5_CHUNK:S5_CHUNK + 1], pw_im[S5_CHUNK:S5_CHUNK + 1]


def _s5prep(a_re, a_im, log_dt, b_re, b_im, c_re, c_im, d):
    p, ch = C_STATE, C_GROUP_CH
    layers = a_re.shape[0]
    g = layers * C_GROUPS
    npair = g // 2
    flat = lambda a: a.reshape(g, *a.shape[2:])
    a_re, a_im, log_dt, b_re, b_im, c_re, c_im = map(flat, (a_re, a_im, log_dt, b_re, b_im, c_re, c_im))
    row = lambda a: a.reshape(g, 1, p)
    ldt = jnp.broadcast_to(log_dt[:, None, None], (g, 1, p))
    b_t = lambda a: jnp.swapaxes(a, 1, 2)
    spec = lambda s1, s2: pl.BlockSpec((2, s1, s2), lambda i: (i, 0, 0))
    mt, wet, wyt, pw_re, pw_im = pl.pallas_call(
        _s5prep_kernel,
        out_shape=[jax.ShapeDtypeStruct((npair, 2, S5_TC, S5_TC), BF16),
                   jax.ShapeDtypeStruct((npair, 4 * p, 2 * S5_TC), BF16),
                   jax.ShapeDtypeStruct((npair, 2 * S5_TC, 4 * p), BF16),
                   jax.ShapeDtypeStruct((npair, 8, 2 * p), F32),
                   jax.ShapeDtypeStruct((npair, 8, 2 * p), F32)],
        grid=(npair,),
        in_specs=[spec(1, p), spec(1, p), spec(1, p), spec(p, ch), spec(p, ch),
                  spec(ch, p), spec(ch, p), spec(ch, p), spec(ch, p)],
        out_specs=[pl.BlockSpec((1, 2, S5_TC, S5_TC), lambda i: (i, 0, 0, 0)),
                   pl.BlockSpec((1, 4 * p, 2 * S5_TC), lambda i: (i, 0, 0)),
                   pl.BlockSpec((1, 2 * S5_TC, 4 * p), lambda i: (i, 0, 0)),
                   pl.BlockSpec((1, 8, 2 * p), lambda i: (i, 0, 0)),
                   pl.BlockSpec((1, 8, 2 * p), lambda i: (i, 0, 0))],
        compiler_params=_cparams(("arbitrary",)),
        name="s5prep",
    )(row(a_re), row(a_im), ldt, b_re, b_im, b_t(b_re), b_t(b_im), c_re, c_im)
    by_gb = lambda a: a.reshape(layers * S5_NGB, S5_PAIRS_PER_GB, *a.shape[1:])
    return (mt.reshape(layers * S5_NGB, S5_GB, S5_TC, S5_TC), by_gb(wet), by_gb(wyt), by_gb(pw_re), by_gb(pw_im),
            d.reshape(layers, 1, C_WIDTH))


def _s5_kernel(u_ref, mt_ref, wet_ref, wyt_ref, are_ref, aim_ref, d_ref, y_ref,
               ut_ref, yt_ref, ere_ref, eim_ref, hre_ref, him_ref):
    nck, t_len, ch = S5_NCHUNK, S5_CHUNK, C_GROUP_CH
    nt = (((1,), (1,)), ((), ()))
    for t in range(t_len):
        xt = u_ref[pl.ds(t, nck, stride=t_len), :].T
        for g in range(S5_GB):
            ut_ref[g, ch * t:ch * (t + 1), :] = xt[ch * g:ch * (g + 1), :]
    for j in range(S5_PAIRS_PER_GB):
        u0 = ut_ref[2 * j].astype(BF16)
        u1 = ut_ref[2 * j + 1].astype(BF16)
        et = jnp.dot(wet_ref[0, j], jnp.concatenate([u0, u1], axis=0), preferred_element_type=F32)
        e = et.T
        ere_ref[:, LANES * j:LANES * (j + 1)] = e[:, :LANES]
        eim_ref[:, LANES * j:LANES * (j + 1)] = e[:, LANES:]
        yt_ref[2 * j] = jnp.dot(mt_ref[0, 2 * j], u0, preferred_element_type=F32)
        yt_ref[2 * j + 1] = jnp.dot(mt_ref[0, 2 * j + 1], u1, preferred_element_type=F32)

    a_re = jnp.concatenate([are_ref[0, j, 0:1, :] for j in range(S5_PAIRS_PER_GB)], axis=1)
    a_im = jnp.concatenate([aim_ref[0, j, 0:1, :] for j in range(S5_PAIRS_PER_GB)], axis=1)

    def body(i, carry):
        h_re, h_im = carry
        hre_ref[pl.ds(i, 1), :] = h_re
        him_ref[pl.ds(i, 1), :] = h_im
        e_re = ere_ref[pl.ds(i, 1), :]
        e_im = eim_ref[pl.ds(i, 1), :]
        return a_re * h_re - a_im * h_im + e_re, a_re * h_im + a_im * h_re + e_im

    zero = jnp.zeros((1, S5_GB * C_STATE), F32)
    lax.fori_loop(0, nck, body, (zero, zero))

    for j in range(S5_PAIRS_PER_GB):
        sl = slice(LANES * j, LANES * (j + 1))
        hp = jnp.concatenate([hre_ref[:, sl], him_ref[:, sl]], axis=1).astype(BF16)
        yi = lax.dot_general(wyt_ref[0, j], hp, nt, preferred_element_type=F32)
        yt_ref[2 * j] += yi[:S5_TC]
        yt_ref[2 * j + 1] += yi[S5_TC:]
    for t in range(t_len):
        ytt = jnp.concatenate([yt_ref[g, ch * t:ch * (t + 1), :] for g in range(S5_GB)], axis=0)
        rows = pl.ds(t, nck, stride=t_len)
        y_ref[rows, :] = ytt.T + d_ref[...] * u_ref[rows, :]


def _s5(cu, mt, wet, wyt, pw_re, pw_im, d_row, layer):
    p4 = 4 * C_STATE
    tok = pl.BlockSpec((SEQ, LANES), lambda gb, b: (b, gb))
    per_gb = lambda *s: pl.BlockSpec((1,) + s, lambda gb, b: (layer * S5_NGB + gb,) + (0,) * len(s))
    state = pltpu.VMEM((S5_NCHUNK, S5_GB * C_STATE), F32)
    return pl.pallas_call(
        _s5_kernel,
        out_shape=jax.ShapeDtypeStruct((TOKENS, C_WIDTH), F32),
        grid=(S5_NGB, BATCH),
        in_specs=[tok,
                  per_gb(S5_GB, S5_TC, S5_TC),
                  per_gb(S5_PAIRS_PER_GB, p4, 2 * S5_TC),
                  per_gb(S5_PAIRS_PER_GB, 2 * S5_TC, p4),
                  per_gb(S5_PAIRS_PER_GB, 8, 2 * C_STATE), per_gb(S5_PAIRS_PER_GB, 8, 2 * C_STATE),
                  pl.BlockSpec((1, LANES), lambda gb, b: (0, gb))],
        out_specs=tok,
        scratch_shapes=[pltpu.VMEM((S5_GB, S5_TC, S5_NCHUNK), F32),
                        pltpu.VMEM((S5_GB, S5_TC, S5_NCHUNK), F32),
                        state, state, state, state],
        compiler_params=_cparams(("arbitrary", "arbitrary")),
        name="s5",
    )(cu, mt, wet, wyt, pw_re, pw_im, d_row)


def _gelu_tanh(x):
    return 0.5 * x * (1.0 + jnp.tanh(math.sqrt(2.0 / math.pi) * (x + 0.044715 * (x * x * x))))


def _out_kernel(oa_ref, ob_ref, yc_ref, cg_ref, x_ref, gate_ref, gpost_ref, wglu_ref, bglu_ref, wout_ref, o_ref):
    y = _gelu_tanh(yc_ref[...])
    z = jnp.dot(y.astype(BF16), wglu_ref[...], preferred_element_type=F32) + bglu_ref[...]
    y = y * jax.nn.sigmoid(z)
    oc = (y * _silu(cg_ref[...].astype(F32))).astype(BF16)
    mix = jnp.concatenate([oa_ref[...], ob_ref[...], oc], axis=1)
    acc = jnp.dot(mix, wout_ref[...], preferred_element_type=F32)
    ms = jnp.mean(acc * acc, axis=-1, keepdims=True)
    out = acc * lax.rsqrt(ms + EPS) * gpost_ref[...]
    o_ref[...] = x_ref[...] + gate_ref[0] * out


def _out(oa, ob, yc, cg, x2, gate, g_post, w_glu, b_glu, w_out, layer):
    tm = OUT_TM
    steps_per_batch = SEQ // tm
    row = lambda i: (i, 0)
    const = lambda i: (0, 0)
    return pl.pallas_call(
        _out_kernel,
        out_shape=jax.ShapeDtypeStruct((TOKENS, D_MODEL), F32),
        grid=(TOKENS // tm,),
        in_specs=[pl.BlockSpec((tm, A_WIDTH), row),
                  pl.BlockSpec((tm, B_WIDTH), row),
                  pl.BlockSpec((tm, C_WIDTH), row),
                  pl.BlockSpec((tm, C_WIDTH), row),
                  pl.BlockSpec((tm, D_MODEL), row),
                  pl.BlockSpec((1, 1, D_MODEL), lambda i: (i // steps_per_batch, 0, 0)),
                  pl.BlockSpec((1, D_MODEL), const),
                  pl.BlockSpec((C_WIDTH, C_WIDTH), lambda i: (layer, 0)),
                  pl.BlockSpec((1, C_WIDTH), const),
                  pl.BlockSpec((2 * D_MODEL, D_MODEL), lambda i: (layer, 0))],
        out_specs=pl.BlockSpec((tm, D_MODEL), row),
        compiler_params=_cparams(("arbitrary",)),
        name="out",
    )(oa, ob, yc, cg, x2, gate, g_post.reshape(1, D_MODEL), w_glu, b_glu.reshape(1, C_WIDTH), w_out)


def kernel(x, c, w_mod, b_mod, g_pre, g_post, w_in, attn_sinks, gla_w_alpha, gla_b_alpha, gla_norm_g,
           s5_a_re, s5_a_im, s5_log_dt, s5_b_re, s5_b_im, s5_c_re, s5_c_im, s5_d, s5_w_glu, s5_b_glu, w_out):
    layers = w_mod.shape[0]
    x2 = x.reshape(TOKENS, D_MODEL)
    bias = _attn_bias()
    mod = _mod(jnp.pad(c, ((0, 8 - BATCH), (0, 0))), w_mod, b_mod)[:, :BATCH]
    shift, scale, gate = (m.reshape(layers, BATCH, 1, D_MODEL) for m in jnp.split(mod, 3, axis=-1))
    w_t = jnp.swapaxes(w_in, 1, 2).astype(BF16).reshape(layers * W_IN_COLS, D_MODEL)
    w_alpha_pad = jnp.pad(gla_w_alpha, ((0, 0), (0, LR_PAD - B_GATE_RANK), (0, 0))).astype(BF16)
    s5_ops = _s5prep(s5_a_re, s5_a_im, s5_log_dt, s5_b_re, s5_b_im, s5_c_re, s5_c_im, s5_d)
    *s5_ops, s5_d_rows = s5_ops
    w_glu = s5_w_glu.astype(BF16).reshape(layers * C_WIDTH, C_WIDTH)
    w_out_b = w_out.astype(BF16).reshape(layers * 2 * D_MODEL, D_MODEL)
    for l in range(layers):
        ak, ag, bq, bk, bv, bg, cu, cg, log_a, aqt, avt = _proj(
            x2, scale[l], shift[l], g_pre[l], w_t, l, w_alpha_pad[l], gla_b_alpha[l])
        o_a = _attn(attn_sinks[l], aqt, ak, avt, ag, bias)
        o_b = _gla(log_a, bq, bk, bv, bg, gla_norm_g[l])
        y_c = _s5(cu, *s5_ops, s5_d_rows[l], l)
        x2 = _out(o_a, o_b, y_c, cg, x2, gate[l], g_post[l], w_glu, s5_b_glu[l], w_out_b, l)
    return x2.reshape(x.shape)
```

```python
import math

import jax
import jax.numpy as jnp
import numpy as np
from jax import lax
from jax.experimental import pallas as pl
from jax.experimental.pallas import tpu as pltpu

F32 = jnp.float32
BF16 = jnp.bfloat16
HIGHEST = lax.Precision.HIGHEST

D_MODEL = 1024
BATCH = 4
SEQ = 4096
TOKENS = BATCH * SEQ
EPS = 1e-6

A_WIDTH = 1024
A_HEAD_DIM = 64
A_Q_HEADS = 16
A_KV_HEADS = 4
A_KV_WIDTH = A_KV_HEADS * A_HEAD_DIM
A_BLOCK = 128
WINDOW = 128

B_WIDTH = 512
B_HEADS = 4
B_DK = 64
B_DV = 128
B_QK_WIDTH = 256
B_GATE_RANK = 16
B_GATE_TAU = 16.0
GLA_BLOCK = 64
GLA_SUB = 4
GLA_DIAG = 16
GLA_ANCHOR_SEGS = (32, 64)
ATTN_QB = 4
C_WIDTH = 512
C_GROUP_CH = 16
C_GROUPS = 32
C_STATE = 64
S5_CHUNK = 16
S5_NCHUNK = SEQ // S5_CHUNK
S5_TC = S5_CHUNK * C_GROUP_CH
S5_GB = 8
S5_NGB = C_GROUPS // S5_GB
S5_PAIRS_PER_GB = S5_GB // 2
S5_POW_ROWS = 24

LANES = 128
LR_PAD = LANES

V7X_VMEM_LIMIT = 56 * 1024 * 1024

PROJ_TM = 1024
OUT_TM = 1024

_W_IN_SIZES = (("aq", A_WIDTH), ("ak", A_KV_WIDTH), ("av", A_KV_WIDTH), ("ag", A_WIDTH), ("bq", B_QK_WIDTH),
               ("bk", B_QK_WIDTH), ("bv", B_WIDTH), ("blr", B_GATE_RANK), ("bg", B_WIDTH), ("cu", C_WIDTH),
               ("cg", C_WIDTH))
_W_IN_OFF = {}
_off = 0
for _name, _w in _W_IN_SIZES:
    _W_IN_OFF[_name] = (_off, _w)
    _off += _w
W_IN_COLS = _off
_PROJ_OUTS = (("ak", BF16), ("ag", BF16), ("bq", BF16), ("bk", BF16),
              ("bv", BF16), ("bg", BF16), ("cu", F32), ("cg", BF16))


def _silu(x):
    return x * jax.nn.sigmoid(x)


def _cparams(semantics):
    return pltpu.CompilerParams(dimension_semantics=semantics, vmem_limit_bytes=V7X_VMEM_LIMIT)


def _mod_kernel(c_ref, w_ref, b_ref, o_ref):
    c = c_ref[...]
    o_ref[0] = jnp.dot(_silu(c).astype(BF16), w_ref[0].astype(BF16), preferred_element_type=F32) + b_ref[0]


def _mod(c_pad, w_mod, b_mod):
    layers = w_mod.shape[0]
    n = 3 * D_MODEL
    tn = 768
    return pl.pallas_call(
        _mod_kernel,
        out_shape=jax.ShapeDtypeStruct((layers, 8, n), F32),
        grid=(layers, n // tn),
        in_specs=[pl.BlockSpec((8, D_MODEL), lambda l, j: (0, 0)),
                  pl.BlockSpec((1, D_MODEL, tn), lambda l, j: (l, 0, j)),
                  pl.BlockSpec((1, 1, tn), lambda l, j: (l, 0, j))],
        out_specs=pl.BlockSpec((1, 8, tn), lambda l, j: (l, 0, j)),
        compiler_params=_cparams(("arbitrary", "arbitrary")),
        name="mod",
    )(c_pad, w_mod, b_mod.reshape(layers, 1, n))


def _proj_kernel(x_ref, scale_ref, shift_ref, gpre_ref, wt_ref, walpha_ref, balpha_ref, *out_refs):
    x = x_ref[...]
    ms = jnp.mean(x * x, axis=-1, keepdims=True)
    y = x * lax.rsqrt(ms + EPS) * gpre_ref[...]
    h = (y * (1.0 + scale_ref[0]) + shift_ref[0]).astype(BF16)
    nt = (((1,), (1,)), ((), ()))

    def rows(name, width=None):
        off, w = _W_IN_OFF[name]
        return wt_ref[off:off + (width or w), :]

    for (name, _), o_ref in zip(_PROJ_OUTS, out_refs):
        o_ref[...] = lax.dot_general(h, rows(name), nt, preferred_element_type=F32).astype(o_ref.dtype)
    lr = lax.dot_general(h, rows("blr", LR_PAD), nt, preferred_element_type=F32).astype(BF16)
    logits = jnp.dot(lr, walpha_ref[...], preferred_element_type=F32) + balpha_ref[...]
    log_sig = jnp.minimum(logits, 0.0) - jnp.log(1.0 + jnp.exp(-jnp.abs(logits)))
    la_ref, qt_ref, vt_ref = out_refs[len(_PROJ_OUTS):]
    la_ref[...] = log_sig * (1.0 / B_GATE_TAU)
    qt = lax.dot_general(rows("aq"), h, nt, preferred_element_type=F32) * (A_HEAD_DIM ** -0.5)
    qt_ref[...] = qt.astype(qt_ref.dtype)
    vt_ref[...] = lax.dot_general(rows("av"), h, nt, preferred_element_type=F32).astype(vt_ref.dtype)


def _proj(x2, scale, shift, g_pre, w_t, layer, w_alpha_pad, b_alpha):
    tm = PROJ_TM
    steps_per_batch = SEQ // tm
    row = lambda i: (i, 0)
    col = lambda i: (0, i)
    per_batch = lambda i: (i // steps_per_batch, 0, 0)
    const = lambda i: (0, 0)
    out_shape = [jax.ShapeDtypeStruct((TOKENS, _W_IN_OFF[n][1]), dt) for n, dt in _PROJ_OUTS]
    out_specs = [pl.BlockSpec((tm, _W_IN_OFF[n][1]), row) for n, _ in _PROJ_OUTS]
    out_shape += [jax.ShapeDtypeStruct((TOKENS, B_QK_WIDTH), F32),
                  jax.ShapeDtypeStruct((A_WIDTH, TOKENS), BF16),
                  jax.ShapeDtypeStruct((A_KV_WIDTH, TOKENS), BF16)]
    out_specs += [pl.BlockSpec((tm, B_QK_WIDTH), row),
                  pl.BlockSpec((A_WIDTH, tm), col),
                  pl.BlockSpec((A_KV_WIDTH, tm), col)]
    return pl.pallas_call(
        _proj_kernel,
        out_shape=out_shape,
        grid=(TOKENS // tm,),
        in_specs=[pl.BlockSpec((tm, D_MODEL), row),
                  pl.BlockSpec((1, 1, D_MODEL), per_batch),
                  pl.BlockSpec((1, 1, D_MODEL), per_batch),
                  pl.BlockSpec((1, D_MODEL), const),
                  pl.BlockSpec((W_IN_COLS, D_MODEL), lambda i: (layer, 0), pipeline_mode=pl.Buffered(1)),
                  pl.BlockSpec((LR_PAD, B_QK_WIDTH), const),
                  pl.BlockSpec((1, B_QK_WIDTH), const)],
        out_specs=out_specs,
        compiler_params=_cparams(("arbitrary",)),
        name="proj",
    )(x2, scale, shift, g_pre.reshape(1, D_MODEL), w_t, w_alpha_pad, b_alpha.reshape(1, B_QK_WIDTH))


def _attn_bias():
    j = np.arange(A_BLOCK)[:, None]
    i = np.arange(A_BLOCK)[None, :]
    dist = np.where(j > i, i + A_BLOCK - j, i - j).astype(np.float32)
    slopes = np.exp2(-8.0 * np.arange(1, A_Q_HEADS + 1, dtype=np.float32) / A_Q_HEADS).astype(np.float32)
    bias = -slopes[:, None, None] * dist[None]
    first = np.where((j > i)[None], -np.inf, bias).astype(np.float32)
    return jnp.asarray(np.stack([bias, first]))


def _attn_kernel(sink_ref, qt_ref, kp_ref, kc_ref, vtp_ref, vtc_ref, g_ref, bias_ref, o_ref):
    first_step = pl.program_id(1) == 0
    kj = lax.broadcasted_iota(jnp.int32, (A_BLOCK, A_BLOCK), 0)
    qi = lax.broadcasted_iota(jnp.int32, (A_BLOCK, A_BLOCK), 1)
    from_prev = kj > qi
    zero_rows = jnp.zeros((A_HEAD_DIM, A_BLOCK), BF16)
    group = A_Q_HEADS // A_KV_HEADS
    blk = lambda j: slice(A_BLOCK * j, A_BLOCK * (j + 1))

    def keys(j, sl):
        prev = kp_ref[:, sl] if j == 0 else kc_ref[blk(j - 1), sl]
        return prev, kc_ref[blk(j), sl]

    def values(j, rows):
        prev = vtp_ref[rows, :] if j == 0 else vtc_ref[rows, blk(j - 1)]
        return jnp.concatenate([prev, vtc_ref[rows, blk(j)]], axis=1)

    def scores(j, hd):
        kvh = hd // group
        sl = slice(LANES * (kvh // 2), LANES * (kvh // 2 + 1))
        qh = qt_ref[A_HEAD_DIM * hd:A_HEAD_DIM * (hd + 1), blk(j)]
        qsel = jnp.concatenate([qh, zero_rows] if kvh % 2 == 0 else [zero_rows, qh], axis=0)
        k_prev, k_cur = keys(j, sl)
        return (jnp.dot(k_prev, qsel, preferred_element_type=F32),
                jnp.dot(k_cur, qsel, preferred_element_type=F32))

    def attend(j, hd, s_prev, s_cur):
        kvh = hd // group
        v_both = values(j, slice(A_HEAD_DIM * kvh, A_HEAD_DIM * (kvh + 1)))
        table = jnp.where(first_step, 1, 0) if j == 0 else 0
        s = jnp.where(from_prev, s_prev, s_cur) + bias_ref[table, hd]
        sink = sink_ref[hd]
        m = jnp.maximum(jnp.max(s, axis=0, keepdims=True), sink)
        p = jnp.exp(s - m)
        den = jnp.sum(p, axis=0, keepdims=True) + jnp.exp(sink - m)
        p_both = jnp.concatenate([jnp.where(from_prev, p, 0.0), jnp.where(from_prev, 0.0, p)],
                                 axis=0).astype(BF16)
        return jnp.dot(v_both, p_both, preferred_element_type=F32) / den

    pending = [scores(0, hd) for hd in range(A_Q_HEADS)]
    for j in range(ATTN_QB):
        current = pending
        if j + 1 < ATTN_QB:
            pending = [scores(j + 1, hd) for hd in range(A_Q_HEADS)]
        outs = {}
        for hd in range(A_Q_HEADS):
            outs[hd] = attend(j, hd, *current[hd])
            if hd % 2 == 1:
                qsl = slice(LANES * (hd // 2), LANES * (hd // 2 + 1))
                o_pair = jnp.concatenate([outs.pop(hd - 1), outs.pop(hd)], axis=0).T
                gate = g_ref[blk(j), qsl].astype(F32)
                o_ref[blk(j), qsl] = (o_pair * _silu(gate)).astype(o_ref.dtype)


def _attn(sinks, qt, k, vt, ag, bias):
    qb = ATTN_QB
    steps = SEQ // (A_BLOCK * qb)
    cur = lambda b, n: (b * steps + n, 0)
    cur_t = lambda b, n: (0, b * steps + n)
    prev = lambda b, n: ((b * steps + n) * qb - jnp.minimum(n, 1), 0)
    prev_t = lambda b, n: (0, (b * steps + n) * qb - jnp.minimum(n, 1))
    return pl.pallas_call(
        _attn_kernel,
        out_shape=jax.ShapeDtypeStruct((TOKENS, A_WIDTH), BF16),
        grid=(BATCH, steps),
        in_specs=[pl.BlockSpec(memory_space=pltpu.SMEM),
                  pl.BlockSpec((A_WIDTH, A_BLOCK * qb), cur_t),
                  pl.BlockSpec((A_BLOCK, A_KV_WIDTH), prev),
                  pl.BlockSpec((A_BLOCK * qb, A_KV_WIDTH), cur),
                  pl.BlockSpec((A_KV_WIDTH, A_BLOCK), prev_t),
                  pl.BlockSpec((A_KV_WIDTH, A_BLOCK * qb), cur_t),
                  pl.BlockSpec((A_BLOCK * qb, A_WIDTH), cur),
                  pl.BlockSpec((2, A_Q_HEADS, A_BLOCK, A_BLOCK), lambda b, n: (0, 0, 0, 0))],
        out_specs=pl.BlockSpec((A_BLOCK * qb, A_WIDTH), cur),
        compiler_params=_cparams(("arbitrary", "arbitrary")),
        name="attn",
    )(sinks, qt, k, k, vt, vt, ag, bias)


def _gla_kernel(la_ref, q_ref, k_ref, v_ref, g_ref, gn_ref, o_ref, st_ref):
    cb = GLA_BLOCK

    @pl.when(pl.program_id(0) == 0)
    def _():
        st_ref[...] = jnp.zeros_like(st_ref)

    r = lax.broadcasted_iota(jnp.int32, (cb, cb), 0)
    c = lax.broadcasted_iota(jnp.int32, (cb, cb), 1)
    tri = (c <= r).astype(F32)
    lane = lax.broadcasted_iota(jnp.int32, (1, B_QK_WIDTH), 1)
    head_masks = [(lane >= B_DK * h) & (lane < B_DK * (h + 1)) for h in range(B_HEADS)]
    rr = lax.broadcasted_iota(jnp.int32, (B_HEADS * cb, cb), 0)
    cc = lax.broadcasted_iota(jnp.int32, (B_HEADS * cb, cb), 1)
    ri = rr & (cb - 1)
    nt = (((1,), (1,)), ((), ()))
    tn = (((0,), (0,)), ((), ()))
    rows = lambda u: slice(cb * u, cb * (u + 1))
    items = [(u, b) for u in range(GLA_SUB) for b in range(BATCH)]
    bcs = {(u, b): jnp.dot(tri, la_ref[b, rows(u), :], preferred_element_type=F32, precision=HIGHEST)
           for u, b in items}
    stack_heads = lambda a: jnp.concatenate([jnp.where(m, a, 0.0) for m in head_masks], axis=0).astype(BF16)
    row = lax.broadcasted_iota(jnp.int32, (cb, 1), 0)
    diag = ((ri ^ cc) < GLA_DIAG) & (cc <= ri)
    same_seg = {seg: (ri ^ cc) < seg for seg in GLA_ANCHOR_SEGS}

    def upper(a, seg):
        return jnp.concatenate([a[s0 + seg // 2:s0 + seg] for s0 in range(0, cb, seg)], axis=0)

    def place_upper(p, seg):
        half = seg // 2
        zero = jnp.zeros((half, cb), F32)
        parts = []
        for piece in range(B_HEADS * cb // seg):
            parts += [zero, p[half * piece:half * (piece + 1)]]
        return jnp.concatenate(parts, axis=0)

    def anchor_rows(bc, seg, offset, reps):
        parts = []
        for s0 in range(0, cb, seg):
            a = s0 + offset - 1
            val = bc[a:a + 1, :] if a >= 0 else jnp.zeros((1, B_QK_WIDTH), F32)
            parts.append(jnp.broadcast_to(val, (reps, B_QK_WIDTH)))
        return jnp.concatenate(parts, axis=0)

    q_lv, k_lv, qsts, ksts, decs, vs = {}, {}, {}, {}, {}, {}
    for it in items:
        u, b = it
        bc = bcs[it]
        bl = bc[cb - 1:cb, :]
        q = q_ref[b, rows(u), :].astype(F32) * (B_DK ** -0.5)
        k = k_ref[b, rows(u), :].astype(F32)
        for seg in GLA_ANCHOR_SEGS:
            half = seg // 2
            q_lv[it, seg] = stack_heads(upper(q, seg) * jnp.exp(upper(bc, seg) - anchor_rows(bc, seg, half, half)))
            in_lower = (row & (seg - 1)) < half
            k_lv[it, seg] = jnp.where(in_lower, k * jnp.exp(anchor_rows(bc, seg, half, seg) - bc), 0.0).astype(BF16)
        anc = anchor_rows(bc, GLA_DIAG, 0, GLA_DIAG)
        q_lv[it, 0] = stack_heads(q * jnp.exp(bc - anc))
        k_lv[it, 0] = (k * jnp.exp(anc - bc)).astype(BF16)
        qsts[it] = stack_heads(q * jnp.exp(bc))
        ksts[it] = stack_heads(k * jnp.exp(bl - bc))
        decs[it] = jnp.exp(bl)
        vs[it] = v_ref[b, rows(u), :]
    a_alls = {}
    for it in items:
        prod = lambda lv: lax.dot_general(q_lv[it, lv], k_lv[it, lv], nt, preferred_element_type=F32)
        a = jnp.where(diag, prod(0), 0.0)
        for seg in GLA_ANCHOR_SEGS:
            a = a + jnp.where(same_seg[seg], place_upper(prod(seg), seg), 0.0)
        a_alls[it] = a.astype(BF16)
    upds = {}
    for it in items:
        vst = jnp.concatenate([vs[it][:, B_DV * h:B_DV * (h + 1)] for h in range(B_HEADS)], axis=0)
        upds[it] = lax.dot_general(vst, ksts[it], tn, preferred_element_type=F32)
    st_in = {}
    for b in range(BATCH):
        st = st_ref[b]
        for u in range(GLA_SUB):
            st_in[u, b] = st
            st = st * decs[u, b] + upds[u, b]
        st_ref[b] = st
    oi_alls = {it: lax.dot_general(qsts[it], st_in[it].astype(BF16), nt, preferred_element_type=F32)
               for it in items}
    o_hs = {}
    for it in items:
        a_all = a_alls[it]
        for h in range(B_HEADS):
            o_hs[it, h] = (jnp.dot(a_all[cb * h:cb * (h + 1)], vs[it][:, B_DV * h:B_DV * (h + 1)],
                                   preferred_element_type=F32) + oi_alls[it][cb * h:cb * (h + 1)])
    for it in items:
        u, b = it
        for h in range(B_HEADS):
            vsl = slice(B_DV * h, B_DV * (h + 1))
            o_h = o_hs[it, h]
            ms = jnp.mean(o_h * o_h, axis=-1, keepdims=True)
            o_n = o_h * lax.rsqrt(ms + EPS) * gn_ref[:, vsl]
            gate = g_ref[b, rows(u), vsl].astype(F32)
            o_ref[b, rows(u), vsl] = (o_n * _silu(gate)).astype(o_ref.dtype)


def _gla(log_a, bq, bk, bv, bg, g_gla):
    cb = GLA_BLOCK * GLA_SUB
    blk = lambda w: pl.BlockSpec((BATCH, cb, w), lambda i: (0, i, 0))
    r3 = lambda a: a.reshape(BATCH, SEQ, a.shape[-1])
    out = pl.pallas_call(
        _gla_kernel,
        out_shape=jax.ShapeDtypeStruct((BATCH, SEQ, B_WIDTH), BF16),
        grid=(SEQ // cb,),
        in_specs=[blk(B_QK_WIDTH), blk(B_QK_WIDTH), blk(B_QK_WIDTH), blk(B_WIDTH), blk(B_WIDTH),
                  pl.BlockSpec((1, B_WIDTH), lambda i: (0, 0))],
        out_specs=blk(B_WIDTH),
        scratch_shapes=[pltpu.VMEM((BATCH, B_DV, B_QK_WIDTH), F32)],
        compiler_params=_cparams(("arbitrary",)),
        name="gla",
    )(r3(log_a), r3(bq), r3(bk), r3(bv), r3(bg), g_gla.reshape(1, B_WIDTH))
    return out.reshape(TOKENS, B_WIDTH)


def _s5prep_kernel(ar_ref, ai_ref, ldt_ref, bre_ref, bim_ref, btre_ref, btim_ref, cre_ref, cim_ref,
                   mt_ref, wet_ref, wyt_ref, are_ref, aim_ref):
    p = C_STATE
    wet_re, wet_im, wyt_rows, a_re, a_im = [], [], [], [], []
    for g in range(2):
        kk, e_re, e_im, y_re, y_im, p_re, p_im = _s5_discretise(
            ar_ref[g], ai_ref[g], ldt_ref[g], bre_ref[g], bim_ref[g], btre_ref[g], btim_ref[g],
            cre_ref[g], cim_ref[g])
        pieces = [kk] + [jnp.concatenate([jnp.zeros((C_GROUP_CH * s, C_GROUP_CH), F32),
                                          kk[:S5_TC - C_GROUP_CH * s]], axis=0) for s in range(1, S5_CHUNK)]
        mt_ref[0, g] = jnp.concatenate(pieces, axis=1).astype(BF16)
        e_t = jnp.concatenate([e_re, e_im], axis=1).T
        zero = jnp.zeros((p, S5_TC), F32)
        wet_re.append(jnp.concatenate([e_t[:p], zero] if g == 0 else [zero, e_t[:p]], axis=1))
        wet_im.append(jnp.concatenate([e_t[p:], zero] if g == 0 else [zero, e_t[p:]], axis=1))
        zero = jnp.zeros((S5_TC, p), F32)
        wyt_rows.append(jnp.concatenate([y_re, zero, -y_im, zero] if g == 0 else [zero, y_re, zero, -y_im], axis=1))
        a_re.append(p_re)
        a_im.append(p_im)
    wet_ref[0] = jnp.concatenate(wet_re + wet_im, axis=0).astype(BF16)
    wyt_ref[0] = jnp.concatenate(wyt_rows, axis=0).astype(BF16)
    are_ref[0] = jnp.broadcast_to(jnp.concatenate(a_re, axis=1), (8, 2 * p))
    aim_ref[0] = jnp.broadcast_to(jnp.concatenate(a_im, axis=1), (8, 2 * p))


def _s5_discretise(ar, ai, ldt, b_re, b_im, bt_re16, bt_im16, c_re16, c_im16):
    dt = jnp.exp(ldt)

    def cmul(xr, xi, yr, yi):
        return xr * yr - xi * yi, xr * yi + xi * yr

    kf = lax.broadcasted_iota(jnp.int32, (S5_POW_ROWS, 1), 0).astype(F32)
    mag = jnp.exp(kf * (ar * dt))
    ang = kf * (ai * dt)
    pw_re, pw_im = mag * jnp.cos(ang), mag * jnp.sin(ang)
    abar_re, abar_im = pw_re[1:2], pw_im[1:2]
    den = ar * ar + ai * ai
    num_re = abar_re - 1.0
    f_re = (num_re * ar + abar_im * ai) / den
    f_im = (abar_im * ar - num_re * ai) / den
    g_re, g_im = cmul(pw_re, pw_im, f_re, f_im)

    def pick(which, xr, xi):
        rep = lambda x: jnp.concatenate(
            [jnp.broadcast_to(x[which(i):which(i) + 1], (C_GROUP_CH, C_STATE)) for i in range(S5_CHUNK)], axis=0)
        return rep(xr), rep(xi)

    tile16 = lambda a: jnp.concatenate([a] * S5_CHUNK, axis=0)
    ct_re, ct_im = tile16(c_re16), tile16(c_im16)
    bt_re, bt_im = tile16(bt_re16), tile16(bt_im16)

    w_re, w_im = cmul(*pick(lambda i: i, g_re, g_im), ct_re, ct_im)
    kk = (jnp.dot(w_re, b_re, preferred_element_type=F32, precision=HIGHEST)
          - jnp.dot(w_im, b_im, preferred_element_type=F32, precision=HIGHEST))
    e_re, e_im = cmul(*pick(lambda i: S5_CHUNK - 1 - i, g_re, g_im), bt_re, bt_im)
    y_re, y_im = cmul(*pick(lambda i: i + 1, pw_re, pw_im), ct_re, ct_im)
    return kk, e_re, e_im, y_re, y_im, pw_re[S5_CHUNK:S5_CHUNK + 1], pw_im[S5_CHUNK:S5_CHUNK + 1]


def _s5prep(a_re, a_im, log_dt, b_re, b_im, c_re, c_im, d):
    p, ch = C_STATE, C_GROUP_CH
    layers = a_re.shape[0]
    g = layers * C_GROUPS
    npair = g // 2
    flat = lambda a: a.reshape(g, *a.shape[2:])
    a_re, a_im, log_dt, b_re, b_im, c_re, c_im = map(flat, (a_re, a_im, log_dt, b_re, b_im, c_re, c_im))
    row = lambda a: a.reshape(g, 1, p)
    ldt = jnp.broadcast_to(log_dt[:, None, None], (g, 1, p))
    b_t = lambda a: jnp.swapaxes(a, 1, 2)
    spec = lambda s1, s2: pl.BlockSpec((2, s1, s2), lambda i: (i, 0, 0))
    mt, wet, wyt, pw_re, pw_im = pl.pallas_call(
        _s5prep_kernel,
        out_shape=[jax.ShapeDtypeStruct((npair, 2, S5_TC, S5_TC), BF16),
                   jax.ShapeDtypeStruct((npair, 4 * p, 2 * S5_TC), BF16),
                   jax.ShapeDtypeStruct((npair, 2 * S5_TC, 4 * p), BF16),
                   jax.ShapeDtypeStruct((npair, 8, 2 * p), F32),
                   jax.ShapeDtypeStruct((npair, 8, 2 * p), F32)],
        grid=(npair,),
        in_specs=[spec(1, p), spec(1, p), spec(1, p), spec(p, ch), spec(p, ch),
                  spec(ch, p), spec(ch, p), spec(ch, p), spec(ch, p)],
        out_specs=[pl.BlockSpec((1, 2, S5_TC, S5_TC), lambda i: (i, 0, 0, 0)),
                   pl.BlockSpec((1, 4 * p, 2 * S5_TC), lambda i: (i, 0, 0)),
                   pl.BlockSpec((1, 2 * S5_TC, 4 * p), lambda i: (i, 0, 0)),
                   pl.BlockSpec((1, 8, 2 * p), lambda i: (i, 0, 0)),
                   pl.BlockSpec((1, 8, 2 * p), lambda i: (i, 0, 0))],
        compiler_params=_cparams(("arbitrary",)),
        name="s5prep",
    )(row(a_re), row(a_im), ldt, b_re, b_im, b_t(b_re), b_t(b_im), c_re, c_im)
    by_gb = lambda a: a.reshape(layers * S5_NGB, S5_PAIRS_PER_GB, *a.shape[1:])
    return (mt.reshape(layers * S5_NGB, S5_GB, S5_TC, S5_TC), by_gb(wet), by_gb(wyt), by_gb(pw_re), by_gb(pw_im),
            d.reshape(layers, 1, C_WIDTH))


def _s5_kernel(u_ref, mt_ref, wet_ref, wyt_ref, are_ref, aim_ref, d_ref, y_ref,
               ut_ref, yt_ref, ere_ref, eim_ref, hre_ref, him_ref):
    nck, t_len, ch = S5_NCHUNK, S5_CHUNK, C_GROUP_CH
    nt = (((1,), (1,)), ((), ()))
    for t in range(t_len):
        xt = u_ref[pl.ds(t, nck, stride=t_len), :].T
        for g in range(S5_GB):
            ut_ref[g, ch * t:ch * (t + 1), :] = xt[ch * g:ch * (g + 1), :]
    for j in range(S5_PAIRS_PER_GB):
        u0 = ut_ref[2 * j].astype(BF16)
        u1 = ut_ref[2 * j + 1].astype(BF16)
        et = jnp.dot(wet_ref[0, j], jnp.concatenate([u0, u1], axis=0), preferred_element_type=F32)
        e = et.T
        ere_ref[:, LANES * j:LANES * (j + 1)] = e[:, :LANES]
        eim_ref[:, LANES * j:LANES * (j + 1)] = e[:, LANES:]
        yt_ref[2 * j] = jnp.dot(mt_ref[0, 2 * j], u0, preferred_element_type=F32)
        yt_ref[2 * j + 1] = jnp.dot(mt_ref[0, 2 * j + 1], u1, preferred_element_type=F32)

    a_re = jnp.concatenate([are_ref[0, j, 0:1, :] for j in range(S5_PAIRS_PER_GB)], axis=1)
    a_im = jnp.concatenate([aim_ref[0, j, 0:1, :] for j in range(S5_PAIRS_PER_GB)], axis=1)

    def body(i, carry):
        h_re, h_im = carry
        hre_ref[pl.ds(i, 1), :] = h_re
        him_ref[pl.ds(i, 1), :] = h_im
        e_re = ere_ref[pl.ds(i, 1), :]
        e_im = eim_ref[pl.ds(i, 1), :]
        return a_re * h_re - a_im * h_im + e_re, a_re * h_im + a_im * h_re + e_im

    zero = jnp.zeros((1, S5_GB * C_STATE), F32)
    lax.fori_loop(0, nck, body, (zero, zero))

    for j in range(S5_PAIRS_PER_GB):
        sl = slice(LANES * j, LANES * (j + 1))
        hp = jnp.concatenate([hre_ref[:, sl], him_ref[:, sl]], axis=1).astype(BF16)
        yi = lax.dot_general(wyt_ref[0, j], hp, nt, preferred_element_type=F32)
        yt_ref[2 * j] += yi[:S5_TC]
        yt_ref[2 * j + 1] += yi[S5_TC:]
    for t in range(t_len):
        ytt = jnp.concatenate([yt_ref[g, ch * t:ch * (t + 1), :] for g in range(S5_GB)], axis=0)
        rows = pl.ds(t, nck, stride=t_len)
        y_ref[rows, :] = ytt.T + d_ref[...] * u_ref[rows, :]


def _s5(cu, mt, wet, wyt, pw_re, pw_im, d_row, layer):
    p4 = 4 * C_STATE
    tok = pl.BlockSpec((SEQ, LANES), lambda gb, b: (b, gb))
    per_gb = lambda *s: pl.BlockSpec((1,) + s, lambda gb, b: (layer * S5_NGB + gb,) + (0,) * len(s))
    state = pltpu.VMEM((S5_NCHUNK, S5_GB * C_STATE), F32)
    return pl.pallas_call(
        _s5_kernel,
        out_shape=jax.ShapeDtypeStruct((TOKENS, C_WIDTH), F32),
        grid=(S5_NGB, BATCH),
        in_specs=[tok,
                  per_gb(S5_GB, S5_TC, S5_TC),
                  per_gb(S5_PAIRS_PER_GB, p4, 2 * S5_TC),
                  per_gb(S5_PAIRS_PER_GB, 2 * S5_TC, p4),
                  per_gb(S5_PAIRS_PER_GB, 8, 2 * C_STATE), per_gb(S5_PAIRS_PER_GB, 8, 2 * C_STATE),
                  pl.BlockSpec((1, LANES), lambda gb, b: (0, gb))],
        out_specs=tok,
        scratch_shapes=[pltpu.VMEM((S5_GB, S5_TC, S5_NCHUNK), F32),
                        pltpu.VMEM((S5_GB, S5_TC, S5_NCHUNK), F32),
                        state, state, state, state],
        compiler_params=_cparams(("arbitrary", "arbitrary")),
        name="s5",
    )(cu, mt, wet, wyt, pw_re, pw_im, d_row)


def _gelu_tanh(x):
    return 0.5 * x * (1.0 + jnp.tanh(math.sqrt(2.0 / math.pi) * (x + 0.044715 * (x * x * x))))


def _out_kernel(oa_ref, ob_ref, yc_ref, cg_ref, x_ref, gate_ref, gpost_ref, wglu_ref, bglu_ref, wout_ref, o_ref):
    y = _gelu_tanh(yc_ref[...])
    z = jnp.dot(y.astype(BF16), wglu_ref[...], preferred_element_type=F32) + bglu_ref[...]
    y = y * jax.nn.sigmoid(z)
    oc = (y * _silu(cg_ref[...].astype(F32))).astype(BF16)
    mix = jnp.concatenate([oa_ref[...], ob_ref[...], oc], axis=1)
    acc = jnp.dot(mix, wout_ref[...], preferred_element_type=F32)
    ms = jnp.mean(acc * acc, axis=-1, keepdims=True)
    out = acc * lax.rsqrt(ms + EPS) * gpost_ref[...]
    o_ref[...] = x_ref[...] + gate_ref[0] * out


def _out(oa, ob, yc, cg, x2, gate, g_post, w_glu, b_glu, w_out, layer):
    tm = OUT_TM
    steps_per_batch = SEQ // tm
    row = lambda i: (i, 0)
    const = lambda i: (0, 0)
    return pl.pallas_call(
        _out_kernel,
        out_shape=jax.ShapeDtypeStruct((TOKENS, D_MODEL), F32),
        grid=(TOKENS // tm,),
        in_specs=[pl.BlockSpec((tm, A_WIDTH), row),
                  pl.BlockSpec((tm, B_WIDTH), row),
                  pl.BlockSpec((tm, C_WIDTH), row),
                  pl.BlockSpec((tm, C_WIDTH), row),
                  pl.BlockSpec((tm, D_MODEL), row),
                  pl.BlockSpec((1, 1, D_MODEL), lambda i: (i // steps_per_batch, 0, 0)),
                  pl.BlockSpec((1, D_MODEL), const),
                  pl.BlockSpec((C_WIDTH, C_WIDTH), lambda i: (layer, 0)),
                  pl.BlockSpec((1, C_WIDTH), const),
                  pl.BlockSpec((2 * D_MODEL, D_MODEL), lambda i: (layer, 0))],
        out_specs=pl.BlockSpec((tm, D_MODEL), row),
        compiler_params=_cparams(("arbitrary",)),
        name="out",
    )(oa, ob, yc, cg, x2, gate, g_post.reshape(1, D_MODEL), w_glu, b_glu.reshape(1, C_WIDTH), w_out)


def kernel(x, c, w_mod, b_mod, g_pre, g_post, w_in, attn_sinks, gla_w_alpha, gla_b_alpha, gla_norm_g,
           s5_a_re, s5_a_im, s5_log_dt, s5_b_re, s5_b_im, s5_c_re, s5_c_im, s5_d, s5_w_glu, s5_b_glu, w_out):
    layers = w_mod.shape[0]
    x2 = x.reshape(TOKENS, D_MODEL)
    bias = _attn_bias()
    mod = _mod(jnp.pad(c, ((0, 8 - BATCH), (0, 0))), w_mod, b_mod)[:, :BATCH]
    shift, scale, gate = (m.reshape(layers, BATCH, 1, D_MODEL) for m in jnp.split(mod, 3, axis=-1))
    w_t = jnp.swapaxes(w_in, 1, 2).astype(BF16).reshape(layers * W_IN_COLS, D_MODEL)
    w_alpha_pad = jnp.pad(gla_w_alpha, ((0, 0), (0, LR_PAD - B_GATE_RANK), (0, 0))).astype(BF16)
    s5_ops = _s5prep(s5_a_re, s5_a_im, s5_log_dt, s5_b_re, s5_b_im, s5_c_re, s5_c_im, s5_d)
    *s5_ops, s5_d_rows = s5_ops
    w_glu = s5_w_glu.astype(BF16).reshape(layers * C_WIDTH, C_WIDTH)
    w_out_b = w_out.astype(BF16).reshape(layers * 2 * D_MODEL, D_MODEL)
    for l in range(layers):
        ak, ag, bq, bk, bv, bg, cu, cg, log_a, aqt, avt = _proj(
            x2, scale[l], shift[l], g_pre[l], w_t, l, w_alpha_pad[l], gla_b_alpha[l])
        o_a = _attn(attn_sinks[l], aqt, ak, avt, ag, bias)
        o_b = _gla(log_a, bq, bk, bv, bg, gla_norm_g[l])
        y_c = _s5(cu, *s5_ops, s5_d_rows[l], l)
        x2 = _out(o_a, o_b, y_c, cg, x2, gate[l], g_post[l], w_glu, s5_b_glu[l], w_out_b, l)
    return x2.reshape(x.shape)
```

```python
import math

import jax
import jax.numpy as jnp
import numpy as np
from jax import lax
from jax.experimental import pallas as pl
from jax.experimental.pallas import tpu as pltpu

F32 = jnp.float32
BF16 = jnp.bfloat16
HIGHEST = lax.Precision.HIGHEST

D_MODEL = 1024
BATCH = 4
SEQ = 4096
TOKENS = BATCH * SEQ
EPS = 1e-6

A_WIDTH = 1024
A_HEAD_DIM = 64
A_Q_HEADS = 16
A_KV_HEADS = 4
A_KV_WIDTH = A_KV_HEADS * A_HEAD_DIM
A_BLOCK = 128
WINDOW = 128

B_WIDTH = 512
B_HEADS = 4
B_DK = 64
B_DV = 128
B_QK_WIDTH = 256
B_GATE_RANK = 16
B_GATE_TAU = 16.0
GLA_BLOCK = 64
GLA_SUB = 4
GLA_DIAG = 16
GLA_ANCHOR_SEGS = (32, 64)
OUT_PARTS = 2
C_WIDTH = 512
C_GROUP_CH = 16
C_GROUPS = 32
C_STATE = 64
S5_CHUNK = 16
S5_NCHUNK = SEQ // S5_CHUNK
S5_TC = S5_CHUNK * C_GROUP_CH
S5_GB = 8
S5_NGB = C_GROUPS // S5_GB
S5_PAIRS_PER_GB = S5_GB // 2
S5_POW_ROWS = 24

LANES = 128
LR_PAD = LANES

V7X_VMEM_LIMIT = 56 * 1024 * 1024

PROJ_TM = 1024
OUT_TM = 1024

_W_IN_SIZES = (("aq", A_WIDTH), ("ak", A_KV_WIDTH), ("av", A_KV_WIDTH), ("ag", A_WIDTH), ("bq", B_QK_WIDTH),
               ("bk", B_QK_WIDTH), ("bv", B_WIDTH), ("blr", B_GATE_RANK), ("bg", B_WIDTH), ("cu", C_WIDTH),
               ("cg", C_WIDTH))
_W_IN_OFF = {}
_off = 0
for _name, _w in _W_IN_SIZES:
    _W_IN_OFF[_name] = (_off, _w)
    _off += _w
W_IN_COLS = _off
_PROJ_OUTS = (("ak", BF16), ("ag", BF16), ("bq", BF16), ("bk", BF16),
              ("bv", BF16), ("bg", BF16), ("cu", F32), ("cg", BF16))


def _silu(x):
    return x * jax.nn.sigmoid(x)


def _cparams(semantics):
    return pltpu.CompilerParams(dimension_semantics=semantics, vmem_limit_bytes=V7X_VMEM_LIMIT)


def _mod_kernel(c_ref, w_ref, b_ref, o_ref):
    c = c_ref[...]
    o_ref[0] = jnp.dot(_silu(c).astype(BF16), w_ref[0].astype(BF16), preferred_element_type=F32) + b_ref[0]


def _mod(c_pad, w_mod, b_mod):
    layers = w_mod.shape[0]
    n = 3 * D_MODEL
    tn = 768
    return pl.pallas_call(
        _mod_kernel,
        out_shape=jax.ShapeDtypeStruct((layers, 8, n), F32),
        grid=(layers, n // tn),
        in_specs=[pl.BlockSpec((8, D_MODEL), lambda l, j: (0, 0)),
                  pl.BlockSpec((1, D_MODEL, tn), lambda l, j: (l, 0, j)),
                  pl.BlockSpec((1, 1, tn), lambda l, j: (l, 0, j))],
        out_specs=pl.BlockSpec((1, 8, tn), lambda l, j: (l, 0, j)),
        compiler_params=_cparams(("arbitrary", "arbitrary")),
        name="mod",
    )(c_pad, w_mod, b_mod.reshape(layers, 1, n))


def _proj_kernel(x_ref, scale_ref, shift_ref, gpre_ref, wt_ref, walpha_ref, balpha_ref, *out_refs):
    x = x_ref[...]
    ms = jnp.mean(x * x, axis=-1, keepdims=True)
    y = x * lax.rsqrt(ms + EPS) * gpre_ref[...]
    h = (y * (1.0 + scale_ref[0]) + shift_ref[0]).astype(BF16)
    nt = (((1,), (1,)), ((), ()))

    def rows(name, width=None):
        off, w = _W_IN_OFF[name]
        return wt_ref[off:off + (width or w), :]

    for (name, _), o_ref in zip(_PROJ_OUTS, out_refs):
        o_ref[...] = lax.dot_general(h, rows(name), nt, preferred_element_type=F32).astype(o_ref.dtype)
    lr = lax.dot_general(h, rows("blr", LR_PAD), nt, preferred_element_type=F32).astype(BF16)
    logits = jnp.dot(lr, walpha_ref[...], preferred_element_type=F32) + balpha_ref[...]
    log_sig = jnp.minimum(logits, 0.0) - jnp.log(1.0 + jnp.exp(-jnp.abs(logits)))
    la_ref, qt_ref, vt_ref = out_refs[len(_PROJ_OUTS):]
    la_ref[...] = log_sig * (1.0 / B_GATE_TAU)
    qt = lax.dot_general(rows("aq"), h, nt, preferred_element_type=F32) * (A_HEAD_DIM ** -0.5)
    qt_ref[...] = qt.astype(qt_ref.dtype)
    vt_ref[...] = lax.dot_general(rows("av"), h, nt, preferred_element_type=F32).astype(vt_ref.dtype)


def _proj(x2, scale, shift, g_pre, w_t, layer, w_alpha_pad, b_alpha):
    tm = PROJ_TM
    steps_per_batch = SEQ // tm
    row = lambda i: (i, 0)
    col = lambda i: (0, i)
    per_batch = lambda i: (i // steps_per_batch, 0, 0)
    const = lambda i: (0, 0)
    out_shape = [jax.ShapeDtypeStruct((TOKENS, _W_IN_OFF[n][1]), dt) for n, dt in _PROJ_OUTS]
    out_specs = [pl.BlockSpec((tm, _W_IN_OFF[n][1]), row) for n, _ in _PROJ_OUTS]
    out_shape += [jax.ShapeDtypeStruct((TOKENS, B_QK_WIDTH), F32),
                  jax.ShapeDtypeStruct((A_WIDTH, TOKENS), BF16),
                  jax.ShapeDtypeStruct((A_KV_WIDTH, TOKENS), BF16)]
    out_specs += [pl.BlockSpec((tm, B_QK_WIDTH), row),
                  pl.BlockSpec((A_WIDTH, tm), col),
                  pl.BlockSpec((A_KV_WIDTH, tm), col)]
    return pl.pallas_call(
        _proj_kernel,
        out_shape=out_shape,
        grid=(TOKENS // tm,),
        in_specs=[pl.BlockSpec((tm, D_MODEL), row),
                  pl.BlockSpec((1, 1, D_MODEL), per_batch),
                  pl.BlockSpec((1, 1, D_MODEL), per_batch),
                  pl.BlockSpec((1, D_MODEL), const),
                  pl.BlockSpec((W_IN_COLS, D_MODEL), lambda i: (layer, 0), pipeline_mode=pl.Buffered(1)),
                  pl.BlockSpec((LR_PAD, B_QK_WIDTH), const),
                  pl.BlockSpec((1, B_QK_WIDTH), const)],
        out_specs=out_specs,
        compiler_params=_cparams(("arbitrary",)),
        name="proj",
    )(x2, scale, shift, g_pre.reshape(1, D_MODEL), w_t, w_alpha_pad, b_alpha.reshape(1, B_QK_WIDTH))


def _attn_bias():
    j = np.arange(A_BLOCK)[:, None]
    i = np.arange(A_BLOCK)[None, :]
    dist = np.where(j > i, i + A_BLOCK - j, i - j).astype(np.float32)
    slopes = np.exp2(-8.0 * np.arange(1, A_Q_HEADS + 1, dtype=np.float32) / A_Q_HEADS).astype(np.float32)
    bias = -slopes[:, None, None] * dist[None]
    first = np.where((j > i)[None], -np.inf, bias).astype(np.float32)
    return jnp.asarray(np.stack([bias, first]))


def _attention_steps(sink_ref, qt_ref, kp_ref, kc_ref, vtp_ref, vtc_ref, g_ref, bias_ref, oa_ref, first_tile,
                     n_blocks, group_blocks, after_block):
    kj = lax.broadcasted_iota(jnp.int32, (A_BLOCK, A_BLOCK), 0)
    qi = lax.broadcasted_iota(jnp.int32, (A_BLOCK, A_BLOCK), 1)
    from_prev = kj > qi
    zero_rows = jnp.zeros((A_HEAD_DIM, A_BLOCK), BF16)
    group = A_Q_HEADS // A_KV_HEADS
    blk = lambda j: slice(A_BLOCK * j, A_BLOCK * (j + 1))

    def keys(j, sl):
        prev = kp_ref[:, sl] if j == 0 else kc_ref[blk(j - 1), sl]
        return prev, kc_ref[blk(j), sl]

    def values(j, rows):
        prev = vtp_ref[rows, :] if j == 0 else vtc_ref[rows, blk(j - 1)]
        return jnp.concatenate([prev, vtc_ref[rows, blk(j)]], axis=1)

    def scores(j, hd):
        kvh = hd // group
        sl = slice(LANES * (kvh // 2), LANES * (kvh // 2 + 1))
        qh = qt_ref[A_HEAD_DIM * hd:A_HEAD_DIM * (hd + 1), blk(j)]
        qsel = jnp.concatenate([qh, zero_rows] if kvh % 2 == 0 else [zero_rows, qh], axis=0)
        k_prev, k_cur = keys(j, sl)
        return (jnp.dot(k_prev, qsel, preferred_element_type=F32),
                jnp.dot(k_cur, qsel, preferred_element_type=F32))

    def attend(j, hd, s_prev, s_cur):
        kvh = hd // group
        v_both = values(j, slice(A_HEAD_DIM * kvh, A_HEAD_DIM * (kvh + 1)))
        table = jnp.where(first_tile, 1, 0) if j == 0 else 0
        s = jnp.where(from_prev, s_prev, s_cur) + bias_ref[table, hd]
        sink = sink_ref[hd]
        m = jnp.maximum(jnp.max(s, axis=0, keepdims=True), sink)
        p = jnp.exp(s - m)
        den = jnp.sum(p, axis=0, keepdims=True) + jnp.exp(sink - m)
        p_both = jnp.concatenate([jnp.where(from_prev, p, 0.0), jnp.where(from_prev, 0.0, p)],
                                 axis=0).astype(BF16)
        return jnp.dot(v_both, p_both, preferred_element_type=F32) / den

    pending = {}

    def issue_scores(first):
        for j in range(first, min(first + group_blocks, n_blocks)):
            for hd in range(A_Q_HEADS):
                pending[j, hd] = scores(j, hd)

    issue_scores(0)
    for j in range(n_blocks):
        if (j + 1) % group_blocks == 0:
            issue_scores(j + 1)
        outs = {}
        for hd in range(A_Q_HEADS):
            outs[hd] = attend(j, hd, *pending.pop((j, hd)))
            if hd % 2 == 1:
                qsl = slice(LANES * (hd // 2), LANES * (hd // 2 + 1))
                o_pair = jnp.concatenate([outs.pop(hd - 1), outs.pop(hd)], axis=0).T
                gate = g_ref[blk(j), qsl].astype(F32)
                oa_ref[blk(j), qsl] = (o_pair * _silu(gate)).astype(oa_ref.dtype)
        after_block(j)


def _gla_kernel(la_ref, q_ref, k_ref, v_ref, g_ref, gn_ref, o_ref, st_ref):
    cb = GLA_BLOCK

    @pl.when(pl.program_id(0) == 0)
    def _():
        st_ref[...] = jnp.zeros_like(st_ref)

    r = lax.broadcasted_iota(jnp.int32, (cb, cb), 0)
    c = lax.broadcasted_iota(jnp.int32, (cb, cb), 1)
    tri = (c <= r).astype(F32)
    lane = lax.broadcasted_iota(jnp.int32, (1, B_QK_WIDTH), 1)
    head_masks = [(lane >= B_DK * h) & (lane < B_DK * (h + 1)) for h in range(B_HEADS)]
    rr = lax.broadcasted_iota(jnp.int32, (B_HEADS * cb, cb), 0)
    cc = lax.broadcasted_iota(jnp.int32, (B_HEADS * cb, cb), 1)
    ri = rr & (cb - 1)
    nt = (((1,), (1,)), ((), ()))
    tn = (((0,), (0,)), ((), ()))
    rows = lambda u: slice(cb * u, cb * (u + 1))
    items = [(u, b) for u in range(GLA_SUB) for b in range(BATCH)]
    bcs = {(u, b): jnp.dot(tri, la_ref[b, rows(u), :], preferred_element_type=F32, precision=HIGHEST)
           for u, b in items}
    stack_heads = lambda a: jnp.concatenate([jnp.where(m, a, 0.0) for m in head_masks], axis=0).astype(BF16)
    row = lax.broadcasted_iota(jnp.int32, (cb, 1), 0)
    diag = ((ri ^ cc) < GLA_DIAG) & (cc <= ri)
    same_seg = {seg: (ri ^ cc) < seg for seg in GLA_ANCHOR_SEGS}

    def upper(a, seg):
        return jnp.concatenate([a[s0 + seg // 2:s0 + seg] for s0 in range(0, cb, seg)], axis=0)

    def place_upper(p, seg):
        half = seg // 2
        zero = jnp.zeros((half, cb), F32)
        parts = []
        for piece in range(B_HEADS * cb // seg):
            parts += [zero, p[half * piece:half * (piece + 1)]]
        return jnp.concatenate(parts, axis=0)

    def anchor_rows(bc, seg, offset, reps):
        parts = []
        for s0 in range(0, cb, seg):
            a = s0 + offset - 1
            val = bc[a:a + 1, :] if a >= 0 else jnp.zeros((1, B_QK_WIDTH), F32)
            parts.append(jnp.broadcast_to(val, (reps, B_QK_WIDTH)))
        return jnp.concatenate(parts, axis=0)

    q_lv, k_lv, qsts, ksts, decs, vs = {}, {}, {}, {}, {}, {}
    for it in items:
        u, b = it
        bc = bcs[it]
        bl = bc[cb - 1:cb, :]
        q = q_ref[b, rows(u), :].astype(F32) * (B_DK ** -0.5)
        k = k_ref[b, rows(u), :].astype(F32)
        for seg in GLA_ANCHOR_SEGS:
            half = seg // 2
            q_lv[it, seg] = stack_heads(upper(q, seg) * jnp.exp(upper(bc, seg) - anchor_rows(bc, seg, half, half)))
            in_lower = (row & (seg - 1)) < half
            k_lv[it, seg] = jnp.where(in_lower, k * jnp.exp(anchor_rows(bc, seg, half, seg) - bc), 0.0).astype(BF16)
        anc = anchor_rows(bc, GLA_DIAG, 0, GLA_DIAG)
        q_lv[it, 0] = stack_heads(q * jnp.exp(bc - anc))
        k_lv[it, 0] = (k * jnp.exp(anc - bc)).astype(BF16)
        qsts[it] = stack_heads(q * jnp.exp(bc))
        ksts[it] = stack_heads(k * jnp.exp(bl - bc))
        decs[it] = jnp.exp(bl)
        vs[it] = v_ref[b, rows(u), :]
    a_alls = {}
    for it in items:
        prod = lambda lv: lax.dot_general(q_lv[it, lv], k_lv[it, lv], nt, preferred_element_type=F32)
        a = jnp.where(diag, prod(0), 0.0)
        for seg in GLA_ANCHOR_SEGS:
            a = a + jnp.where(same_seg[seg], place_upper(prod(seg), seg), 0.0)
        a_alls[it] = a.astype(BF16)
    upds = {}
    for it in items:
        vst = jnp.concatenate([vs[it][:, B_DV * h:B_DV * (h + 1)] for h in range(B_HEADS)], axis=0)
        upds[it] = lax.dot_general(vst, ksts[it], tn, preferred_element_type=F32)
    st_in = {}
    for b in range(BATCH):
        st = st_ref[b]
        for u in range(GLA_SUB):
            st_in[u, b] = st
            st = st * decs[u, b] + upds[u, b]
        st_ref[b] = st
    oi_alls = {it: lax.dot_general(qsts[it], st_in[it].astype(BF16), nt, preferred_element_type=F32)
               for it in items}
    o_hs = {}
    for it in items:
        a_all = a_alls[it]
        for h in range(B_HEADS):
            o_hs[it, h] = (jnp.dot(a_all[cb * h:cb * (h + 1)], vs[it][:, B_DV * h:B_DV * (h + 1)],
                                   preferred_element_type=F32) + oi_alls[it][cb * h:cb * (h + 1)])
    for it in items:
        u, b = it
        for h in range(B_HEADS):
            vsl = slice(B_DV * h, B_DV * (h + 1))
            o_h = o_hs[it, h]
            ms = jnp.mean(o_h * o_h, axis=-1, keepdims=True)
            o_n = o_h * lax.rsqrt(ms + EPS) * gn_ref[:, vsl]
            gate = g_ref[b, rows(u), vsl].astype(F32)
            o_ref[b, rows(u), vsl] = (o_n * _silu(gate)).astype(o_ref.dtype)


def _gla(log_a, bq, bk, bv, bg, g_gla):
    cb = GLA_BLOCK * GLA_SUB
    blk = lambda w: pl.BlockSpec((BATCH, cb, w), lambda i: (0, i, 0))
    r3 = lambda a: a.reshape(BATCH, SEQ, a.shape[-1])
    out = pl.pallas_call(
        _gla_kernel,
        out_shape=jax.ShapeDtypeStruct((BATCH, SEQ, B_WIDTH), BF16),
        grid=(SEQ // cb,),
        in_specs=[blk(B_QK_WIDTH), blk(B_QK_WIDTH), blk(B_QK_WIDTH), blk(B_WIDTH), blk(B_WIDTH),
                  pl.BlockSpec((1, B_WIDTH), lambda i: (0, 0))],
        out_specs=blk(B_WIDTH),
        scratch_shapes=[pltpu.VMEM((BATCH, B_DV, B_QK_WIDTH), F32)],
        compiler_params=_cparams(("arbitrary",)),
        name="gla",
    )(r3(log_a), r3(bq), r3(bk), r3(bv), r3(bg), g_gla.reshape(1, B_WIDTH))
    return out.reshape(TOKENS, B_WIDTH)


def _s5prep_kernel(ar_ref, ai_ref, ldt_ref, bre_ref, bim_ref, btre_ref, btim_ref, cre_ref, cim_ref,
                   mt_ref, wet_ref, wyt_ref, are_ref, aim_ref):
    p = C_STATE
    wet_re, wet_im, wyt_rows, a_re, a_im = [], [], [], [], []
    for g in range(2):
        kk, e_re, e_im, y_re, y_im, p_re, p_im = _s5_discretise(
            ar_ref[g], ai_ref[g], ldt_ref[g], bre_ref[g], bim_ref[g], btre_ref[g], btim_ref[g],
            cre_ref[g], cim_ref[g])
        pieces = [kk] + [jnp.concatenate([jnp.zeros((C_GROUP_CH * s, C_GROUP_CH), F32),
                                          kk[:S5_TC - C_GROUP_CH * s]], axis=0) for s in range(1, S5_CHUNK)]
        mt_ref[0, g] = jnp.concatenate(pieces, axis=1).astype(BF16)
        e_t = jnp.concatenate([e_re, e_im], axis=1).T
        zero = jnp.zeros((p, S5_TC), F32)
        wet_re.append(jnp.concatenate([e_t[:p], zero] if g == 0 else [zero, e_t[:p]], axis=1))
        wet_im.append(jnp.concatenate([e_t[p:], zero] if g == 0 else [zero, e_t[p:]], axis=1))
        zero = jnp.zeros((S5_TC, p), F32)
        wyt_rows.append(jnp.concatenate([y_re, zero, -y_im, zero] if g == 0 else [zero, y_re, zero, -y_im], axis=1))
        a_re.append(p_re)
        a_im.append(p_im)
    wet_ref[0] = jnp.concatenate(wet_re + wet_im, axis=0).astype(BF16)
    wyt_ref[0] = jnp.concatenate(wyt_rows, axis=0).astype(BF16)
    are_ref[0] = jnp.broadcast_to(jnp.concatenate(a_re, axis=1), (8, 2 * p))
    aim_ref[0] = jnp.broadcast_to(jnp.concatenate(a_im, axis=1), (8, 2 * p))


def _s5_discretise(ar, ai, ldt, b_re, b_im, bt_re16, bt_im16, c_re16, c_im16):
    dt = jnp.exp(ldt)

    def cmul(xr, xi, yr, yi):
        return xr * yr - xi * yi, xr * yi + xi * yr

    kf = lax.broadcasted_iota(jnp.int32, (S5_POW_ROWS, 1), 0).astype(F32)
    mag = jnp.exp(kf * (ar * dt))
    ang = kf * (ai * dt)
    pw_re, pw_im = mag * jnp.cos(ang), mag * jnp.sin(ang)
    abar_re, abar_im = pw_re[1:2], pw_im[1:2]
    den = ar * ar + ai * ai
    num_re = abar_re - 1.0
    f_re = (num_re * ar + abar_im * ai) / den
    f_im = (abar_im * ar - num_re * ai) / den
    g_re, g_im = cmul(pw_re, pw_im, f_re, f_im)

    def pick(which, xr, xi):
        rep = lambda x: jnp.concatenate(
            [jnp.broadcast_to(x[which(i):which(i) + 1], (C_GROUP_CH, C_STATE)) for i in range(S5_CHUNK)], axis=0)
        return rep(xr), rep(xi)

    tile16 = lambda a: jnp.concatenate([a] * S5_CHUNK, axis=0)
    ct_re, ct_im = tile16(c_re16), tile16(c_im16)
    bt_re, bt_im = tile16(bt_re16), tile16(bt_im16)

    w_re, w_im = cmul(*pick(lambda i: i, g_re, g_im), ct_re, ct_im)
    kk = (jnp.dot(w_re, b_re, preferred_element_type=F32, precision=HIGHEST)
          - jnp.dot(w_im, b_im, preferred_element_type=F32, precision=HIGHEST))
    e_re, e_im = cmul(*pick(lambda i: S5_CHUNK - 1 - i, g_re, g_im), bt_re, bt_im)
    y_re, y_im = cmul(*pick(lambda i: i + 1, pw_re, pw_im), ct_re, ct_im)
    return kk, e_re, e_im, y_re, y_im, pw_re[S5_CHUNK:S5_CHUNK + 1], pw_im[S5_CHUNK:S5_CHUNK + 1]


def _s5prep(a_re, a_im, log_dt, b_re, b_im, c_re, c_im, d):
    p, ch = C_STATE, C_GROUP_CH
    layers = a_re.shape[0]
    g = layers * C_GROUPS
    npair = g // 2
    flat = lambda a: a.reshape(g, *a.shape[2:])
    a_re, a_im, log_dt, b_re, b_im, c_re, c_im = map(flat, (a_re, a_im, log_dt, b_re, b_im, c_re, c_im))
    row = lambda a: a.reshape(g, 1, p)
    ldt = jnp.broadcast_to(log_dt[:, None, None], (g, 1, p))
    b_t = lambda a: jnp.swapaxes(a, 1, 2)
    spec = lambda s1, s2: pl.BlockSpec((2, s1, s2), lambda i: (i, 0, 0))
    mt, wet, wyt, pw_re, pw_im = pl.pallas_call(
        _s5prep_kernel,
        out_shape=[jax.ShapeDtypeStruct((npair, 2, S5_TC, S5_TC), BF16),
                   jax.ShapeDtypeStruct((npair, 4 * p, 2 * S5_TC), BF16),
                   jax.ShapeDtypeStruct((npair, 2 * S5_TC, 4 * p), BF16),
                   jax.ShapeDtypeStruct((npair, 8, 2 * p), F32),
                   jax.ShapeDtypeStruct((npair, 8, 2 * p), F32)],
        grid=(npair,),
        in_specs=[spec(1, p), spec(1, p), spec(1, p), spec(p, ch), spec(p, ch),
                  spec(ch, p), spec(ch, p), spec(ch, p), spec(ch, p)],
        out_specs=[pl.BlockSpec((1, 2, S5_TC, S5_TC), lambda i: (i, 0, 0, 0)),
                   pl.BlockSpec((1, 4 * p, 2 * S5_TC), lambda i: (i, 0, 0)),
                   pl.BlockSpec((1, 2 * S5_TC, 4 * p), lambda i: (i, 0, 0)),
                   pl.BlockSpec((1, 8, 2 * p), lambda i: (i, 0, 0)),
                   pl.BlockSpec((1, 8, 2 * p), lambda i: (i, 0, 0))],
        compiler_params=_cparams(("arbitrary",)),
        name="s5prep",
    )(row(a_re), row(a_im), ldt, b_re, b_im, b_t(b_re), b_t(b_im), c_re, c_im)
    by_gb = lambda a: a.reshape(layers * S5_NGB, S5_PAIRS_PER_GB, *a.shape[1:])
    return (mt.reshape(layers * S5_NGB, S5_GB, S5_TC, S5_TC), by_gb(wet), by_gb(wyt), by_gb(pw_re), by_gb(pw_im),
            d.reshape(layers, 1, C_WIDTH))


def _s5_kernel(u_ref, mt_ref, wet_ref, wyt_ref, are_ref, aim_ref, d_ref, y_ref,
               ut_ref, yt_ref, ere_ref, eim_ref, hre_ref, him_ref):
    nck, t_len, ch = S5_NCHUNK, S5_CHUNK, C_GROUP_CH
    nt = (((1,), (1,)), ((), ()))
    for t in range(t_len):
        xt = u_ref[pl.ds(t, nck, stride=t_len), :].T
        for g in range(S5_GB):
            ut_ref[g, ch * t:ch * (t + 1), :] = xt[ch * g:ch * (g + 1), :]
    for j in range(S5_PAIRS_PER_GB):
        u0 = ut_ref[2 * j].astype(BF16)
        u1 = ut_ref[2 * j + 1].astype(BF16)
        et = jnp.dot(wet_ref[0, j], jnp.concatenate([u0, u1], axis=0), preferred_element_type=F32)
        e = et.T
        ere_ref[:, LANES * j:LANES * (j + 1)] = e[:, :LANES]
        eim_ref[:, LANES * j:LANES * (j + 1)] = e[:, LANES:]
        yt_ref[2 * j] = jnp.dot(mt_ref[0, 2 * j], u0, preferred_element_type=F32)
        yt_ref[2 * j + 1] = jnp.dot(mt_ref[0, 2 * j + 1], u1, preferred_element_type=F32)

    a_re = jnp.concatenate([are_ref[0, j, 0:1, :] for j in range(S5_PAIRS_PER_GB)], axis=1)
    a_im = jnp.concatenate([aim_ref[0, j, 0:1, :] for j in range(S5_PAIRS_PER_GB)], axis=1)

    def body(i, carry):
        h_re, h_im = carry
        hre_ref[pl.ds(i, 1), :] = h_re
        him_ref[pl.ds(i, 1), :] = h_im
        e_re = ere_ref[pl.ds(i, 1), :]
        e_im = eim_ref[pl.ds(i, 1), :]
        return a_re * h_re - a_im * h_im + e_re, a_re * h_im + a_im * h_re + e_im

    zero = jnp.zeros((1, S5_GB * C_STATE), F32)
    lax.fori_loop(0, nck, body, (zero, zero))

    for j in range(S5_PAIRS_PER_GB):
        sl = slice(LANES * j, LANES * (j + 1))
        hp = jnp.concatenate([hre_ref[:, sl], him_ref[:, sl]], axis=1).astype(BF16)
        yi = lax.dot_general(wyt_ref[0, j], hp, nt, preferred_element_type=F32)
        yt_ref[2 * j] += yi[:S5_TC]
        yt_ref[2 * j + 1] += yi[S5_TC:]
    for t in range(t_len):
        ytt = jnp.concatenate([yt_ref[g, ch * t:ch * (t + 1), :] for g in range(S5_GB)], axis=0)
        rows = pl.ds(t, nck, stride=t_len)
        y_ref[rows, :] = ytt.T + d_ref[...] * u_ref[rows, :]


def _s5(cu, mt, wet, wyt, pw_re, pw_im, d_row, layer):
    p4 = 4 * C_STATE
    tok = pl.BlockSpec((SEQ, LANES), lambda gb, b: (b, gb))
    per_gb = lambda *s: pl.BlockSpec((1,) + s, lambda gb, b: (layer * S5_NGB + gb,) + (0,) * len(s))
    state = pltpu.VMEM((S5_NCHUNK, S5_GB * C_STATE), F32)
    return pl.pallas_call(
        _s5_kernel,
        out_shape=jax.ShapeDtypeStruct((TOKENS, C_WIDTH), F32),
        grid=(S5_NGB, BATCH),
        in_specs=[tok,
                  per_gb(S5_GB, S5_TC, S5_TC),
                  per_gb(S5_PAIRS_PER_GB, p4, 2 * S5_TC),
                  per_gb(S5_PAIRS_PER_GB, 2 * S5_TC, p4),
                  per_gb(S5_PAIRS_PER_GB, 8, 2 * C_STATE), per_gb(S5_PAIRS_PER_GB, 8, 2 * C_STATE),
                  pl.BlockSpec((1, LANES), lambda gb, b: (0, gb))],
        out_specs=tok,
        scratch_shapes=[pltpu.VMEM((S5_GB, S5_TC, S5_NCHUNK), F32),
                        pltpu.VMEM((S5_GB, S5_TC, S5_NCHUNK), F32),
                        state, state, state, state],
        compiler_params=_cparams(("arbitrary", "arbitrary")),
        name="s5",
    )(cu, mt, wet, wyt, pw_re, pw_im, d_row)


def _gelu_tanh(x):
    return 0.5 * x * (1.0 + jnp.tanh(math.sqrt(2.0 / math.pi) * (x + 0.044715 * (x * x * x))))


def _out_kernel(sink_ref, qt_ref, kp_ref, kc_ref, vtp_ref, vtc_ref, ag_ref, bias_ref,
                ob_ref, yc_ref, cg_ref, x_ref, gate_ref, gpost_ref, wglu_ref, bglu_ref, wout_ref, o_ref, oa_ref):
    first_tile = pl.program_id(0) % (SEQ // OUT_TM) == 0
    blocks_per_part = OUT_TM // (A_BLOCK * OUT_PARTS)

    def project(part):
        r = slice(A_BLOCK * blocks_per_part * part, A_BLOCK * blocks_per_part * (part + 1))
        y = _gelu_tanh(yc_ref[r, :])
        z = jnp.dot(y.astype(BF16), wglu_ref[...], preferred_element_type=F32) + bglu_ref[...]
        y = y * jax.nn.sigmoid(z)
        oc = (y * _silu(cg_ref[r, :].astype(F32))).astype(BF16)
        mix = jnp.concatenate([oa_ref[r, :], ob_ref[r, :], oc], axis=1)
        acc = jnp.dot(mix, wout_ref[...], preferred_element_type=F32)
        ms = jnp.mean(acc * acc, axis=-1, keepdims=True)
        out = acc * lax.rsqrt(ms + EPS) * gpost_ref[...]
        o_ref[r, :] = x_ref[r, :] + gate_ref[0] * out

    def after_block(j):
        if (j + 1) % blocks_per_part == 0:
            project(j // blocks_per_part)

    _attention_steps(sink_ref, qt_ref, kp_ref, kc_ref, vtp_ref, vtc_ref, ag_ref, bias_ref, oa_ref, first_tile,
                     OUT_TM // A_BLOCK, blocks_per_part, after_block)


def _out(sinks, qt, k, vt, ag, bias, ob, yc, cg, x2, gate, g_post, w_glu, b_glu, w_out, layer):
    tm = OUT_TM
    steps_per_batch = SEQ // tm
    blocks = tm // A_BLOCK
    row = lambda i: (i, 0)
    col = lambda i: (0, i)
    const = lambda i: (0, 0)
    once = dict(pipeline_mode=pl.Buffered(1))
    prev = lambda i: (i * blocks - jnp.minimum(i % steps_per_batch, 1), 0)
    prev_t = lambda i: (0, i * blocks - jnp.minimum(i % steps_per_batch, 1))
    return pl.pallas_call(
        _out_kernel,
        out_shape=jax.ShapeDtypeStruct((TOKENS, D_MODEL), F32),
        grid=(TOKENS // tm,),
        in_specs=[pl.BlockSpec(memory_space=pltpu.SMEM),
                  pl.BlockSpec((A_WIDTH, tm), col),
                  pl.BlockSpec((A_BLOCK, A_KV_WIDTH), prev),
                  pl.BlockSpec((tm, A_KV_WIDTH), row),
                  pl.BlockSpec((A_KV_WIDTH, A_BLOCK), prev_t),
                  pl.BlockSpec((A_KV_WIDTH, tm), col),
                  pl.BlockSpec((tm, A_WIDTH), row),
                  pl.BlockSpec((2, A_Q_HEADS, A_BLOCK, A_BLOCK), lambda i: (0, 0, 0, 0), **once),
                  pl.BlockSpec((tm, B_WIDTH), row),
                  pl.BlockSpec((tm, C_WIDTH), row),
                  pl.BlockSpec((tm, C_WIDTH), row),
                  pl.BlockSpec((tm, D_MODEL), row),
                  pl.BlockSpec((1, 1, D_MODEL), lambda i: (i // steps_per_batch, 0, 0)),
                  pl.BlockSpec((1, D_MODEL), const),
                  pl.BlockSpec((C_WIDTH, C_WIDTH), lambda i: (layer, 0), **once),
                  pl.BlockSpec((1, C_WIDTH), const),
                  pl.BlockSpec((2 * D_MODEL, D_MODEL), lambda i: (layer, 0), **once)],
        out_specs=pl.BlockSpec((tm, D_MODEL), row),
        scratch_shapes=[pltpu.VMEM((tm, A_WIDTH), BF16)],
        compiler_params=_cparams(("arbitrary",)),
        name="out",
    )(sinks, qt, k, k, vt, vt, ag, bias, ob, yc, cg, x2, gate, g_post.reshape(1, D_MODEL), w_glu,
      b_glu.reshape(1, C_WIDTH), w_out)


def kernel(x, c, w_mod, b_mod, g_pre, g_post, w_in, attn_sinks, gla_w_alpha, gla_b_alpha, gla_norm_g,
           s5_a_re, s5_a_im, s5_log_dt, s5_b_re, s5_b_im, s5_c_re, s5_c_im, s5_d, s5_w_glu, s5_b_glu, w_out):
    layers = w_mod.shape[0]
    x2 = x.reshape(TOKENS, D_MODEL)
    bias = _attn_bias()
    mod = _mod(jnp.pad(c, ((0, 8 - BATCH), (0, 0))), w_mod, b_mod)[:, :BATCH]
    shift, scale, gate = (m.reshape(layers, BATCH, 1, D_MODEL) for m in jnp.split(mod, 3, axis=-1))
    w_t = jnp.swapaxes(w_in, 1, 2).astype(BF16).reshape(layers * W_IN_COLS, D_MODEL)
    w_alpha_pad = jnp.pad(gla_w_alpha, ((0, 0), (0, LR_PAD - B_GATE_RANK), (0, 0))).astype(BF16)
    s5_ops = _s5prep(s5_a_re, s5_a_im, s5_log_dt, s5_b_re, s5_b_im, s5_c_re, s5_c_im, s5_d)
    *s5_ops, s5_d_rows = s5_ops
    w_glu = s5_w_glu.astype(BF16).reshape(layers * C_WIDTH, C_WIDTH)
    w_out_b = w_out.astype(BF16).reshape(layers * 2 * D_MODEL, D_MODEL)
    for l in range(layers):
        ak, ag, bq, bk, bv, bg, cu, cg, log_a, aqt, avt = _proj(
            x2, scale[l], shift[l], g_pre[l], w_t, l, w_alpha_pad[l], gla_b_alpha[l])
        o_b = _gla(log_a, bq, bk, bv, bg, gla_norm_g[l])
        y_c = _s5(cu, *s5_ops, s5_d_rows[l], l)
        x2 = _out(attn_sinks[l], aqt, ak, avt, ag, bias, o_b, y_c, cg, x2, gate[l], g_post[l], w_glu, s5_b_glu[l],
                  w_out_b, l)
    return x2.reshape(x.shape)
```

```python
import math

import jax
import jax.numpy as jnp
import numpy as np
from jax import lax
from jax.experimental import pallas as pl
from jax.experimental.pallas import tpu as pltpu

F32 = jnp.float32
BF16 = jnp.bfloat16
HIGHEST = lax.Precision.HIGHEST

D_MODEL = 1024
BATCH = 4
SEQ = 4096
TOKENS = BATCH * SEQ
EPS = 1e-6

A_WIDTH = 1024
A_HEAD_DIM = 64
A_Q_HEADS = 16
A_KV_HEADS = 4
A_KV_WIDTH = A_KV_HEADS * A_HEAD_DIM
A_BLOCK = 128
WINDOW = 128

B_WIDTH = 512
B_HEADS = 4
B_DK = 64
B_DV = 128
B_QK_WIDTH = 256
B_GATE_RANK = 16
B_GATE_TAU = 16.0
GLA_BLOCK = 64
GLA_SUB = 8
GLA_DIAG = 16
GLA_ANCHOR_SEGS = (32, 64)
OUT_PARTS = 2

C_WIDTH = 512
C_GROUP_CH = 16
C_GROUPS = 32
C_STATE = 64
S5_CHUNK = 16
S5_NCHUNK = SEQ // S5_CHUNK
S5_TC = S5_CHUNK * C_GROUP_CH
S5_GB = 8
S5_NGB = C_GROUPS // S5_GB
S5_PAIRS_PER_GB = S5_GB // 2
S5_POW_ROWS = 24
S5PREP_PAIRS = 4

LANES = 128
LR_PAD = LANES

V7X_VMEM_LIMIT = 56 * 1024 * 1024

PROJ_TM = 1024
OUT_TM = 1024

_W_IN_SIZES = (("aq", A_WIDTH), ("ak", A_KV_WIDTH), ("av", A_KV_WIDTH), ("ag", A_WIDTH), ("bq", B_QK_WIDTH),
               ("bk", B_QK_WIDTH), ("bv", B_WIDTH), ("blr", B_GATE_RANK), ("bg", B_WIDTH), ("cu", C_WIDTH),
               ("cg", C_WIDTH))
_W_IN_OFF = {}
_off = 0
for _name, _w in _W_IN_SIZES:
    _W_IN_OFF[_name] = (_off, _w)
    _off += _w
W_IN_COLS = _off
_PROJ_OUTS = (("ak", BF16), ("ag", BF16), ("bq", BF16), ("bk", BF16),
              ("bv", BF16), ("bg", BF16), ("cu", F32), ("cg", BF16))


def _silu(x):
    return x * jax.nn.sigmoid(x)


def _cparams(semantics):
    return pltpu.CompilerParams(dimension_semantics=semantics, vmem_limit_bytes=V7X_VMEM_LIMIT)


def _mod_kernel(c_ref, w_ref, b_ref, o_ref):
    c = c_ref[...]
    o_ref[0] = jnp.dot(_silu(c).astype(BF16), w_ref[0].astype(BF16), preferred_element_type=F32) + b_ref[0]


def _mod(c_pad, w_mod, b_mod):
    layers = w_mod.shape[0]
    n = 3 * D_MODEL
    tn = 768
    return pl.pallas_call(
        _mod_kernel,
        out_shape=jax.ShapeDtypeStruct((layers, 8, n), F32),
        grid=(layers, n // tn),
        in_specs=[pl.BlockSpec((8, D_MODEL), lambda l, j: (0, 0)),
                  pl.BlockSpec((1, D_MODEL, tn), lambda l, j: (l, 0, j)),
                  pl.BlockSpec((1, 1, tn), lambda l, j: (l, 0, j))],
        out_specs=pl.BlockSpec((1, 8, tn), lambda l, j: (l, 0, j)),
        compiler_params=_cparams(("arbitrary", "arbitrary")),
        name="mod",
    )(c_pad, w_mod, b_mod.reshape(layers, 1, n))


def _proj_kernel(x_ref, scale_ref, shift_ref, gpre_ref, wt_ref, walpha_ref, balpha_ref, *out_refs):
    x = x_ref[...]
    ms = jnp.mean(x * x, axis=-1, keepdims=True)
    y = x * lax.rsqrt(ms + EPS) * gpre_ref[...]
    h = (y * (1.0 + scale_ref[0]) + shift_ref[0]).astype(BF16)
    nt = (((1,), (1,)), ((), ()))

    def rows(name, width=None):
        off, w = _W_IN_OFF[name]
        return wt_ref[off:off + (width or w), :]

    for (name, _), o_ref in zip(_PROJ_OUTS, out_refs):
        o_ref[...] = lax.dot_general(h, rows(name), nt, preferred_element_type=F32).astype(o_ref.dtype)
    lr = lax.dot_general(h, rows("blr", LR_PAD), nt, preferred_element_type=F32).astype(BF16)
    logits = jnp.dot(lr, walpha_ref[...], preferred_element_type=F32) + balpha_ref[...]
    log_sig = jnp.minimum(logits, 0.0) - jnp.log(1.0 + jnp.exp(-jnp.abs(logits)))
    la_ref, qt_ref, vt_ref = out_refs[len(_PROJ_OUTS):]
    la_ref[...] = log_sig * (1.0 / B_GATE_TAU)
    qt = lax.dot_general(rows("aq"), h, nt, preferred_element_type=F32) * (A_HEAD_DIM ** -0.5)
    qt_ref[...] = qt.astype(qt_ref.dtype)
    vt_ref[...] = lax.dot_general(rows("av"), h, nt, preferred_element_type=F32).astype(vt_ref.dtype)


def _proj(x2, scale, shift, g_pre, w_t, layer, w_alpha_pad, b_alpha):
    tm = PROJ_TM
    steps_per_batch = SEQ // tm
    row = lambda i: (i, 0)
    col = lambda i: (0, i)
    per_batch = lambda i: (i // steps_per_batch, 0, 0)
    const = lambda i: (0, 0)
    out_shape = [jax.ShapeDtypeStruct((TOKENS, _W_IN_OFF[n][1]), dt) for n, dt in _PROJ_OUTS]
    out_specs = [pl.BlockSpec((tm, _W_IN_OFF[n][1]), row) for n, _ in _PROJ_OUTS]
    out_shape += [jax.ShapeDtypeStruct((TOKENS, B_QK_WIDTH), F32),
                  jax.ShapeDtypeStruct((A_WIDTH, TOKENS), BF16),
                  jax.ShapeDtypeStruct((A_KV_WIDTH, TOKENS), BF16)]
    out_specs += [pl.BlockSpec((tm, B_QK_WIDTH), row),
                  pl.BlockSpec((A_WIDTH, tm), col),
                  pl.BlockSpec((A_KV_WIDTH, tm), col)]
    return pl.pallas_call(
        _proj_kernel,
        out_shape=out_shape,
        grid=(TOKENS // tm,),
        in_specs=[pl.BlockSpec((tm, D_MODEL), row),
                  pl.BlockSpec((1, 1, D_MODEL), per_batch),
                  pl.BlockSpec((1, 1, D_MODEL), per_batch),
                  pl.BlockSpec((1, D_MODEL), const),
                  pl.BlockSpec((W_IN_COLS, D_MODEL), lambda i: (layer, 0), pipeline_mode=pl.Buffered(1)),
                  pl.BlockSpec((LR_PAD, B_QK_WIDTH), const),
                  pl.BlockSpec((1, B_QK_WIDTH), const)],
        out_specs=out_specs,
        compiler_params=_cparams(("arbitrary",)),
        name="proj",
    )(x2, scale, shift, g_pre.reshape(1, D_MODEL), w_t, w_alpha_pad, b_alpha.reshape(1, B_QK_WIDTH))


def _attn_bias():
    j = np.arange(A_BLOCK)[:, None]
    i = np.arange(A_BLOCK)[None, :]
    dist = np.where(j > i, i + A_BLOCK - j, i - j).astype(np.float32)
    slopes = np.exp2(-8.0 * np.arange(1, A_Q_HEADS + 1, dtype=np.float32) / A_Q_HEADS).astype(np.float32)
    bias = -slopes[:, None, None] * dist[None]
    first = np.where((j > i)[None], -np.inf, bias).astype(np.float32)
    return jnp.asarray(np.stack([bias, first]))


def _attention_steps(sink_ref, qt_ref, kp_ref, kc_ref, vtp_ref, vtc_ref, g_ref, bias_ref, oa_ref, first_tile,
                     n_blocks, group_blocks, after_block):
    kj = lax.broadcasted_iota(jnp.int32, (A_BLOCK, A_BLOCK), 0)
    qi = lax.broadcasted_iota(jnp.int32, (A_BLOCK, A_BLOCK), 1)
    from_prev = kj > qi
    zero_rows = jnp.zeros((A_HEAD_DIM, A_BLOCK), BF16)
    group = A_Q_HEADS // A_KV_HEADS
    blk = lambda j: slice(A_BLOCK * j, A_BLOCK * (j + 1))

    def keys(j, sl):
        prev = kp_ref[:, sl] if j == 0 else kc_ref[blk(j - 1), sl]
        return prev, kc_ref[blk(j), sl]

    def values(j, rows):
        prev = vtp_ref[rows, :] if j == 0 else vtc_ref[rows, blk(j - 1)]
        return jnp.concatenate([prev, vtc_ref[rows, blk(j)]], axis=1)

    def scores(j, hd):
        kvh = hd // group
        sl = slice(LANES * (kvh // 2), LANES * (kvh // 2 + 1))
        qh = qt_ref[A_HEAD_DIM * hd:A_HEAD_DIM * (hd + 1), blk(j)]
        qsel = jnp.concatenate([qh, zero_rows] if kvh % 2 == 0 else [zero_rows, qh], axis=0)
        k_prev, k_cur = keys(j, sl)
        return (jnp.dot(k_prev, qsel, preferred_element_type=F32),
                jnp.dot(k_cur, qsel, preferred_element_type=F32))

    def attend(j, hd, s_prev, s_cur):
        kvh = hd // group
        v_both = values(j, slice(A_HEAD_DIM * kvh, A_HEAD_DIM * (kvh + 1)))
        table = jnp.where(first_tile, 1, 0) if j == 0 else 0
        s = jnp.where(from_prev, s_prev, s_cur) + bias_ref[table, hd]
        sink = sink_ref[hd]
        m = jnp.maximum(jnp.max(s, axis=0, keepdims=True), sink)
        p = jnp.exp(s - m)
        den = jnp.sum(p, axis=0, keepdims=True) + jnp.exp(sink - m)
        p_both = jnp.concatenate([jnp.where(from_prev, p, 0.0), jnp.where(from_prev, 0.0, p)],
                                 axis=0).astype(BF16)
        return jnp.dot(v_both, p_both, preferred_element_type=F32) / den

    pending = {}

    def issue_scores(first):
        for j in range(first, min(first + group_blocks, n_blocks)):
            for hd in range(A_Q_HEADS):
                pending[j, hd] = scores(j, hd)

    issue_scores(0)
    for j in range(n_blocks):
        if (j + 1) % group_blocks == 0:
            issue_scores(j + 1)
        outs = {}
        for hd in range(A_Q_HEADS):
            outs[hd] = attend(j, hd, *pending.pop((j, hd)))
            if hd % 2 == 1:
                qsl = slice(LANES * (hd // 2), LANES * (hd // 2 + 1))
                o_pair = jnp.concatenate([outs.pop(hd - 1), outs.pop(hd)], axis=0).T
                gate = g_ref[blk(j), qsl].astype(F32)
                oa_ref[blk(j), qsl] = (o_pair * _silu(gate)).astype(oa_ref.dtype)
        after_block(j)


def _gla_kernel(la_ref, q_ref, k_ref, v_ref, g_ref, gn_ref, o_ref, st_ref):
    cb = GLA_BLOCK

    @pl.when(pl.program_id(0) == 0)
    def _():
        st_ref[...] = jnp.zeros_like(st_ref)

    r = lax.broadcasted_iota(jnp.int32, (cb, cb), 0)
    c = lax.broadcasted_iota(jnp.int32, (cb, cb), 1)
    tri = (c <= r).astype(F32)
    lane = lax.broadcasted_iota(jnp.int32, (1, B_QK_WIDTH), 1)
    head_masks = [(lane >= B_DK * h) & (lane < B_DK * (h + 1)) for h in range(B_HEADS)]
    rr = lax.broadcasted_iota(jnp.int32, (B_HEADS * cb, cb), 0)
    cc = lax.broadcasted_iota(jnp.int32, (B_HEADS * cb, cb), 1)
    ri = rr & (cb - 1)
    nt = (((1,), (1,)), ((), ()))
    tn = (((0,), (0,)), ((), ()))
    rows = lambda u: slice(cb * u, cb * (u + 1))
    items = [(u, b) for u in range(GLA_SUB) for b in range(BATCH)]
    bcs = {(u, b): jnp.dot(tri, la_ref[b, rows(u), :], preferred_element_type=F32, precision=HIGHEST)
           for u, b in items}
    stack_heads = lambda a: jnp.concatenate([jnp.where(m, a, 0.0) for m in head_masks], axis=0).astype(BF16)
    row = lax.broadcasted_iota(jnp.int32, (cb, 1), 0)
    diag = ((ri ^ cc) < GLA_DIAG) & (cc <= ri)
    same_seg = {seg: (ri ^ cc) < seg for seg in GLA_ANCHOR_SEGS}

    def upper(a, seg):
        return jnp.concatenate([a[s0 + seg // 2:s0 + seg] for s0 in range(0, cb, seg)], axis=0)

    def place_upper(p, seg):
        half = seg // 2
        zero = jnp.zeros((half, cb), F32)
        parts = []
        for piece in range(B_HEADS * cb // seg):
            parts += [zero, p[half * piece:half * (piece + 1)]]
        return jnp.concatenate(parts, axis=0)

    def anchor_rows(bc, seg, offset, reps):
        parts = []
        for s0 in range(0, cb, seg):
            a = s0 + offset - 1
            val = bc[a:a + 1, :] if a >= 0 else jnp.zeros((1, B_QK_WIDTH), F32)
            parts.append(jnp.broadcast_to(val, (reps, B_QK_WIDTH)))
        return jnp.concatenate(parts, axis=0)

    q_lv, k_lv, qsts, ksts, decs, vs = {}, {}, {}, {}, {}, {}
    for it in items:
        u, b = it
        bc = bcs[it]
        bl = bc[cb - 1:cb, :]
        q = q_ref[b, rows(u), :].astype(F32) * (B_DK ** -0.5)
        k = k_ref[b, rows(u), :].astype(F32)
        for seg in GLA_ANCHOR_SEGS:
            half = seg // 2
            q_lv[it, seg] = stack_heads(upper(q, seg) * jnp.exp(upper(bc, seg) - anchor_rows(bc, seg, half, half)))
            in_lower = (row & (seg - 1)) < half
            k_lv[it, seg] = jnp.where(in_lower, k * jnp.exp(anchor_rows(bc, seg, half, seg) - bc), 0.0).astype(BF16)
        anc = anchor_rows(bc, GLA_DIAG, 0, GLA_DIAG)
        q_lv[it, 0] = stack_heads(q * jnp.exp(bc - anc))
        k_lv[it, 0] = (k * jnp.exp(anc - bc)).astype(BF16)
        qsts[it] = stack_heads(q * jnp.exp(bc))
        ksts[it] = stack_heads(k * jnp.exp(bl - bc))
        decs[it] = jnp.exp(bl)
        vs[it] = v_ref[b, rows(u), :]
    a_alls = {}
    for it in items:
        prod = lambda lv: lax.dot_general(q_lv[it, lv], k_lv[it, lv], nt, preferred_element_type=F32)
        a = jnp.where(diag, prod(0), 0.0)
        for seg in GLA_ANCHOR_SEGS:
            a = a + jnp.where(same_seg[seg], place_upper(prod(seg), seg), 0.0)
        a_alls[it] = a.astype(BF16)
    upds = {}
    for it in items:
        vst = jnp.concatenate([vs[it][:, B_DV * h:B_DV * (h + 1)] for h in range(B_HEADS)], axis=0)
        upds[it] = lax.dot_general(vst, ksts[it], tn, preferred_element_type=F32)
    st_in = {}
    for b in range(BATCH):
        st = st_ref[b]
        for u in range(GLA_SUB):
            st_in[u, b] = st
            st = st * decs[u, b] + upds[u, b]
        st_ref[b] = st
    oi_alls = {it: lax.dot_general(qsts[it], st_in[it].astype(BF16), nt, preferred_element_type=F32)
               for it in items}
    o_hs = {}
    for it in items:
        a_all = a_alls[it]
        for h in range(B_HEADS):
            o_hs[it, h] = (jnp.dot(a_all[cb * h:cb * (h + 1)], vs[it][:, B_DV * h:B_DV * (h + 1)],
                                   preferred_element_type=F32) + oi_alls[it][cb * h:cb * (h + 1)])
    for it in items:
        u, b = it
        for h in range(B_HEADS):
            vsl = slice(B_DV * h, B_DV * (h + 1))
            o_h = o_hs[it, h]
            ms = jnp.mean(o_h * o_h, axis=-1, keepdims=True)
            o_n = o_h * lax.rsqrt(ms + EPS) * gn_ref[:, vsl]
            gate = g_ref[b, rows(u), vsl].astype(F32)
            o_ref[b, rows(u), vsl] = (o_n * _silu(gate)).astype(o_ref.dtype)


def _gla(log_a, bq, bk, bv, bg, g_gla):
    cb = GLA_BLOCK * GLA_SUB
    blk = lambda w: pl.BlockSpec((BATCH, cb, w), lambda i: (0, i, 0))
    r3 = lambda a: a.reshape(BATCH, SEQ, a.shape[-1])
    out = pl.pallas_call(
        _gla_kernel,
        out_shape=jax.ShapeDtypeStruct((BATCH, SEQ, B_WIDTH), BF16),
        grid=(SEQ // cb,),
        in_specs=[blk(B_QK_WIDTH), blk(B_QK_WIDTH), blk(B_QK_WIDTH), blk(B_WIDTH), blk(B_WIDTH),
                  pl.BlockSpec((1, B_WIDTH), lambda i: (0, 0))],
        out_specs=blk(B_WIDTH),
        scratch_shapes=[pltpu.VMEM((BATCH, B_DV, B_QK_WIDTH), F32)],
        compiler_params=_cparams(("arbitrary",)),
        name="gla",
    )(r3(log_a), r3(bq), r3(bk), r3(bv), r3(bg), g_gla.reshape(1, B_WIDTH))
    return out.reshape(TOKENS, B_WIDTH)


def _s5prep_kernel(ar_ref, ai_ref, ldt_ref, bre_ref, bim_ref, btre_ref, btim_ref, cre_ref, cim_ref,
                   mt_ref, wet_ref, wyt_ref, are_ref, aim_ref):
    p = C_STATE
    for pr in range(S5PREP_PAIRS):
        wet_re, wet_im, wyt_rows, a_re, a_im = [], [], [], [], []
        for g in range(2):
            i = 2 * pr + g
            kk, e_re, e_im, y_re, y_im, p_re, p_im = _s5_discretise(
                ar_ref[i], ai_ref[i], ldt_ref[i], bre_ref[i], bim_ref[i], btre_ref[i], btim_ref[i],
                cre_ref[i], cim_ref[i])
            pieces = [kk] + [jnp.concatenate([jnp.zeros((C_GROUP_CH * s, C_GROUP_CH), F32),
                                              kk[:S5_TC - C_GROUP_CH * s]], axis=0) for s in range(1, S5_CHUNK)]
            mt_ref[pr, g] = jnp.concatenate(pieces, axis=1).astype(BF16)
            e_t = jnp.concatenate([e_re, e_im], axis=1).T
            zero = jnp.zeros((p, S5_TC), F32)
            wet_re.append(jnp.concatenate([e_t[:p], zero] if g == 0 else [zero, e_t[:p]], axis=1))
            wet_im.append(jnp.concatenate([e_t[p:], zero] if g == 0 else [zero, e_t[p:]], axis=1))
            zero = jnp.zeros((S5_TC, p), F32)
            wyt_rows.append(
                jnp.concatenate([y_re, zero, -y_im, zero] if g == 0 else [zero, y_re, zero, -y_im], axis=1))
            a_re.append(p_re)
            a_im.append(p_im)
        wet_ref[pr] = jnp.concatenate(wet_re + wet_im, axis=0).astype(BF16)
        wyt_ref[pr] = jnp.concatenate(wyt_rows, axis=0).astype(BF16)
        are_ref[pr] = jnp.broadcast_to(jnp.concatenate(a_re, axis=1), (8, 2 * p))
        aim_ref[pr] = jnp.broadcast_to(jnp.concatenate(a_im, axis=1), (8, 2 * p))


def _s5_discretise(ar, ai, ldt, b_re, b_im, bt_re16, bt_im16, c_re16, c_im16):
    dt = jnp.exp(ldt)

    def cmul(xr, xi, yr, yi):
        return xr * yr - xi * yi, xr * yi + xi * yr

    kf = lax.broadcasted_iota(jnp.int32, (S5_POW_ROWS, 1), 0).astype(F32)
    mag = jnp.exp(kf * (ar * dt))
    ang = kf * (ai * dt)
    pw_re, pw_im = mag * jnp.cos(ang), mag * jnp.sin(ang)
    abar_re, abar_im = pw_re[1:2], pw_im[1:2]
    den = ar * ar + ai * ai
    num_re = abar_re - 1.0
    f_re = (num_re * ar + abar_im * ai) / den
    f_im = (abar_im * ar - num_re * ai) / den
    g_re, g_im = cmul(pw_re, pw_im, f_re, f_im)

    def pick(which, xr, xi):
        rep = lambda x: jnp.concatenate(
            [jnp.broadcast_to(x[which(i):which(i) + 1], (C_GROUP_CH, C_STATE)) for i in range(S5_CHUNK)], axis=0)
        return rep(xr), rep(xi)

    tile16 = lambda a: jnp.concatenate([a] * S5_CHUNK, axis=0)
    ct_re, ct_im = tile16(c_re16), tile16(c_im16)
    bt_re, bt_im = tile16(bt_re16), tile16(bt_im16)

    w_re, w_im = cmul(*pick(lambda i: i, g_re, g_im), ct_re, ct_im)
    kk = (jnp.dot(w_re, b_re, preferred_element_type=F32, precision=HIGHEST)
          - jnp.dot(w_im, b_im, preferred_element_type=F32, precision=HIGHEST))
    e_re, e_im = cmul(*pick(lambda i: S5_CHUNK - 1 - i, g_re, g_im), bt_re, bt_im)
    y_re, y_im = cmul(*pick(lambda i: i + 1, pw_re, pw_im), ct_re, ct_im)
    return kk, e_re, e_im, y_re, y_im, pw_re[S5_CHUNK:S5_CHUNK + 1], pw_im[S5_CHUNK:S5_CHUNK + 1]


def _s5prep(a_re, a_im, log_dt, b_re, b_im, c_re, c_im, d):
    p, ch = C_STATE, C_GROUP_CH
    layers = a_re.shape[0]
    g = layers * C_GROUPS
    npair = g // 2
    flat = lambda a: a.reshape(g, *a.shape[2:])
    a_re, a_im, log_dt, b_re, b_im, c_re, c_im = map(flat, (a_re, a_im, log_dt, b_re, b_im, c_re, c_im))
    row = lambda a: a.reshape(g, 1, p)
    ldt = jnp.broadcast_to(log_dt[:, None, None], (g, 1, p))
    b_t = lambda a: jnp.swapaxes(a, 1, 2)
    pps = S5PREP_PAIRS
    spec = lambda s1, s2: pl.BlockSpec((2 * pps, s1, s2), lambda i: (i, 0, 0))
    mt, wet, wyt, pw_re, pw_im = pl.pallas_call(
        _s5prep_kernel,
        out_shape=[jax.ShapeDtypeStruct((npair, 2, S5_TC, S5_TC), BF16),
                   jax.ShapeDtypeStruct((npair, 4 * p, 2 * S5_TC), BF16),
                   jax.ShapeDtypeStruct((npair, 2 * S5_TC, 4 * p), BF16),
                   jax.ShapeDtypeStruct((npair, 8, 2 * p), F32),
                   jax.ShapeDtypeStruct((npair, 8, 2 * p), F32)],
        grid=(npair // pps,),
        in_specs=[spec(1, p), spec(1, p), spec(1, p), spec(p, ch), spec(p, ch),
                  spec(ch, p), spec(ch, p), spec(ch, p), spec(ch, p)],
        out_specs=[pl.BlockSpec((pps, 2, S5_TC, S5_TC), lambda i: (i, 0, 0, 0)),
                   pl.BlockSpec((pps, 4 * p, 2 * S5_TC), lambda i: (i, 0, 0)),
                   pl.BlockSpec((pps, 2 * S5_TC, 4 * p), lambda i: (i, 0, 0)),
                   pl.BlockSpec((pps, 8, 2 * p), lambda i: (i, 0, 0)),
                   pl.BlockSpec((pps, 8, 2 * p), lambda i: (i, 0, 0))],
        compiler_params=_cparams(("arbitrary",)),
        name="s5prep",
    )(row(a_re), row(a_im), ldt, b_re, b_im, b_t(b_re), b_t(b_im), c_re, c_im)
    by_gb = lambda a: a.reshape(layers * S5_NGB, S5_PAIRS_PER_GB, *a.shape[1:])
    return (mt.reshape(layers * S5_NGB, S5_GB, S5_TC, S5_TC), by_gb(wet), by_gb(wyt), by_gb(pw_re), by_gb(pw_im),
            d.reshape(layers, 1, C_WIDTH))


def _s5_kernel(u_ref, mt_ref, wet_ref, wyt_ref, are_ref, aim_ref, d_ref, y_ref,
               ut_ref, yt_ref, ere_ref, eim_ref, hre_ref, him_ref):
    nck, t_len, ch = S5_NCHUNK, S5_CHUNK, C_GROUP_CH
    nt = (((1,), (1,)), ((), ()))
    for t in range(t_len):
        xt = u_ref[pl.ds(t, nck, stride=t_len), :].T
        for g in range(S5_GB):
            ut_ref[g, ch * t:ch * (t + 1), :] = xt[ch * g:ch * (g + 1), :]
    for j in range(S5_PAIRS_PER_GB):
        u0 = ut_ref[2 * j].astype(BF16)
        u1 = ut_ref[2 * j + 1].astype(BF16)
        et = jnp.dot(wet_ref[0, j], jnp.concatenate([u0, u1], axis=0), preferred_element_type=F32)
        e = et.T
        ere_ref[:, LANES * j:LANES * (j + 1)] = e[:, :LANES]
        eim_ref[:, LANES * j:LANES * (j + 1)] = e[:, LANES:]
        yt_ref[2 * j] = jnp.dot(mt_ref[0, 2 * j], u0, preferred_element_type=F32)
        yt_ref[2 * j + 1] = jnp.dot(mt_ref[0, 2 * j + 1], u1, preferred_element_type=F32)

    a_re = jnp.concatenate([are_ref[0, j, 0:1, :] for j in range(S5_PAIRS_PER_GB)], axis=1)
    a_im = jnp.concatenate([aim_ref[0, j, 0:1, :] for j in range(S5_PAIRS_PER_GB)], axis=1)

    def body(i, carry):
        h_re, h_im = carry
        hre_ref[pl.ds(i, 1), :] = h_re
        him_ref[pl.ds(i, 1), :] = h_im
        e_re = ere_ref[pl.ds(i, 1), :]
        e_im = eim_ref[pl.ds(i, 1), :]
        return a_re * h_re - a_im * h_im + e_re, a_re * h_im + a_im * h_re + e_im

    zero = jnp.zeros((1, S5_GB * C_STATE), F32)
    lax.fori_loop(0, nck, body, (zero, zero))

    for j in range(S5_PAIRS_PER_GB):
        sl = slice(LANES * j, LANES * (j + 1))
        hp = jnp.concatenate([hre_ref[:, sl], him_ref[:, sl]], axis=1).astype(BF16)
        yi = lax.dot_general(wyt_ref[0, j], hp, nt, preferred_element_type=F32)
        yt_ref[2 * j] += yi[:S5_TC]
        yt_ref[2 * j + 1] += yi[S5_TC:]
    for t in range(t_len):
        ytt = jnp.concatenate([yt_ref[g, ch * t:ch * (t + 1), :] for g in range(S5_GB)], axis=0)
        rows = pl.ds(t, nck, stride=t_len)
        y_ref[rows, :] = ytt.T + d_ref[...] * u_ref[rows, :]


def _s5(cu, mt, wet, wyt, pw_re, pw_im, d_row, layer):
    p4 = 4 * C_STATE
    tok = pl.BlockSpec((SEQ, LANES), lambda gb, b: (b, gb))
    per_gb = lambda *s: pl.BlockSpec((1,) + s, lambda gb, b: (layer * S5_NGB + gb,) + (0,) * len(s))
    state = pltpu.VMEM((S5_NCHUNK, S5_GB * C_STATE), F32)
    return pl.pallas_call(
        _s5_kernel,
        out_shape=jax.ShapeDtypeStruct((TOKENS, C_WIDTH), F32),
        grid=(S5_NGB, BATCH),
        in_specs=[tok,
                  per_gb(S5_GB, S5_TC, S5_TC),
                  per_gb(S5_PAIRS_PER_GB, p4, 2 * S5_TC),
                  per_gb(S5_PAIRS_PER_GB, 2 * S5_TC, p4),
                  per_gb(S5_PAIRS_PER_GB, 8, 2 * C_STATE), per_gb(S5_PAIRS_PER_GB, 8, 2 * C_STATE),
                  pl.BlockSpec((1, LANES), lambda gb, b: (0, gb))],
        out_specs=tok,
        scratch_shapes=[pltpu.VMEM((S5_GB, S5_TC, S5_NCHUNK), F32),
                        pltpu.VMEM((S5_GB, S5_TC, S5_NCHUNK), F32),
                        state, state, state, state],
        compiler_params=_cparams(("arbitrary", "arbitrary")),
        name="s5",
    )(cu, mt, wet, wyt, pw_re, pw_im, d_row)


def _gelu_tanh(x):
    return 0.5 * x * (1.0 + jnp.tanh(math.sqrt(2.0 / math.pi) * (x + 0.044715 * (x * x * x))))


def _out_kernel(sink_ref, qt_ref, kp_ref, kc_ref, vtp_ref, vtc_ref, ag_ref, bias_ref,
                ob_ref, yc_ref, cg_ref, x_ref, gate_ref, gpost_ref, wglu_ref, bglu_ref, wout_ref, o_ref, oa_ref):
    first_tile = pl.program_id(0) % (SEQ // OUT_TM) == 0
    blocks_per_part = OUT_TM // (A_BLOCK * OUT_PARTS)

    def project(part):
        r = slice(A_BLOCK * blocks_per_part * part, A_BLOCK * blocks_per_part * (part + 1))
        y = _gelu_tanh(yc_ref[r, :])
        z = jnp.dot(y.astype(BF16), wglu_ref[...], preferred_element_type=F32) + bglu_ref[...]
        y = y * jax.nn.sigmoid(z)
        oc = (y * _silu(cg_ref[r, :].astype(F32))).astype(BF16)
        mix = jnp.concatenate([oa_ref[r, :], ob_ref[r, :], oc], axis=1)
        acc = jnp.dot(mix, wout_ref[...], preferred_element_type=F32)
        ms = jnp.mean(acc * acc, axis=-1, keepdims=True)
        out = acc * lax.rsqrt(ms + EPS) * gpost_ref[...]
        o_ref[r, :] = x_ref[r, :] + gate_ref[0] * out

    def after_block(j):
        if (j + 1) % blocks_per_part == 0:
            project(j // blocks_per_part)

    _attention_steps(sink_ref, qt_ref, kp_ref, kc_ref, vtp_ref, vtc_ref, ag_ref, bias_ref, oa_ref, first_tile,
                     OUT_TM // A_BLOCK, blocks_per_part, after_block)


def _out(sinks, qt, k, vt, ag, bias, ob, yc, cg, x2, gate, g_post, w_glu, b_glu, w_out, layer):
    tm = OUT_TM
    steps_per_batch = SEQ // tm
    blocks = tm // A_BLOCK
    row = lambda i: (i, 0)
    col = lambda i: (0, i)
    const = lambda i: (0, 0)
    once = dict(pipeline_mode=pl.Buffered(1))
    prev = lambda i: (i * blocks - jnp.minimum(i % steps_per_batch, 1), 0)
    prev_t = lambda i: (0, i * blocks - jnp.minimum(i % steps_per_batch, 1))
    return pl.pallas_call(
        _out_kernel,
        out_shape=jax.ShapeDtypeStruct((TOKENS, D_MODEL), F32),
        grid=(TOKENS // tm,),
        in_specs=[pl.BlockSpec(memory_space=pltpu.SMEM),
                  pl.BlockSpec((A_WIDTH, tm), col),
                  pl.BlockSpec((A_BLOCK, A_KV_WIDTH), prev),
                  pl.BlockSpec((tm, A_KV_WIDTH), row),
                  pl.BlockSpec((A_KV_WIDTH, A_BLOCK), prev_t),
                  pl.BlockSpec((A_KV_WIDTH, tm), col),
                  pl.BlockSpec((tm, A_WIDTH), row),
                  pl.BlockSpec((2, A_Q_HEADS, A_BLOCK, A_BLOCK), lambda i: (0, 0, 0, 0), **once),
                  pl.BlockSpec((tm, B_WIDTH), row),
                  pl.BlockSpec((tm, C_WIDTH), row),
                  pl.BlockSpec((tm, C_WIDTH), row),
                  pl.BlockSpec((tm, D_MODEL), row),
                  pl.BlockSpec((1, 1, D_MODEL), lambda i: (i // steps_per_batch, 0, 0)),
                  pl.BlockSpec((1, D_MODEL), const),
                  pl.BlockSpec((C_WIDTH, C_WIDTH), lambda i: (layer, 0), **once),
                  pl.BlockSpec((1, C_WIDTH), const),
                  pl.BlockSpec((2 * D_MODEL, D_MODEL), lambda i: (layer, 0), **once)],
        out_specs=pl.BlockSpec((tm, D_MODEL), row),
        scratch_shapes=[pltpu.VMEM((tm, A_WIDTH), BF16)],
        compiler_params=_cparams(("arbitrary",)),
        name="out",
    )(sinks, qt, k, k, vt, vt, ag, bias, ob, yc, cg, x2, gate, g_post.reshape(1, D_MODEL), w_glu,
      b_glu.reshape(1, C_WIDTH), w_out)


def kernel(x, c, w_mod, b_mod, g_pre, g_post, w_in, attn_sinks, gla_w_alpha, gla_b_alpha, gla_norm_g,
           s5_a_re, s5_a_im, s5_log_dt, s5_b_re, s5_b_im, s5_c_re, s5_c_im, s5_d, s5_w_glu, s5_b_glu, w_out):
    layers = w_mod.shape[0]
    x2 = x.reshape(TOKENS, D_MODEL)
    bias = _attn_bias()
    mod = _mod(jnp.pad(c, ((0, 8 - BATCH), (0, 0))), w_mod, b_mod)[:, :BATCH]
    shift, scale, gate = (m.reshape(layers, BATCH, 1, D_MODEL) for m in jnp.split(mod, 3, axis=-1))
    w_t = jnp.swapaxes(w_in, 1, 2).astype(BF16).reshape(layers * W_IN_COLS, D_MODEL)
    w_alpha_pad = jnp.pad(gla_w_alpha, ((0, 0), (0, LR_PAD - B_GATE_RANK), (0, 0))).astype(BF16)
    s5_ops = _s5prep(s5_a_re, s5_a_im, s5_log_dt, s5_b_re, s5_b_im, s5_c_re, s5_c_im, s5_d)
    *s5_ops, s5_d_rows = s5_ops
    w_glu = s5_w_glu.astype(BF16).reshape(layers * C_WIDTH, C_WIDTH)
    w_out_b = w_out.astype(BF16).reshape(layers * 2 * D_MODEL, D_MODEL)
    for l in range(layers):
        ak, ag, bq, bk, bv, bg, cu, cg, log_a, aqt, avt = _proj(
            x2, scale[l], shift[l], g_pre[l], w_t, l, w_alpha_pad[l], gla_b_alpha[l])
        o_b = _gla(log_a, bq, bk, bv, bg, gla_norm_g[l])
        y_c = _s5(cu, *s5_ops, s5_d_rows[l], l)
        x2 = _out(attn_sinks[l], aqt, ak, avt, ag, bias, o_b, y_c, cg, x2, gate[l], g_post[l], w_glu, s5_b_glu[l],
                  w_out_b, l)
    return x2.reshape(x.shape)
```

```python
import math

import jax
import jax.numpy as jnp
import numpy as np
from jax import lax
from jax.experimental import pallas as pl
from jax.experimental.pallas import tpu as pltpu

F32 = jnp.float32
BF16 = jnp.bfloat16
HIGHEST = lax.Precision.HIGHEST

D_MODEL = 1024
BATCH = 4
SEQ = 4096
TOKENS = BATCH * SEQ
EPS = 1e-6

A_WIDTH = 1024
A_HEAD_DIM = 64
A_Q_HEADS = 16
A_KV_HEADS = 4
A_KV_WIDTH = A_KV_HEADS * A_HEAD_DIM
A_BLOCK = 128
WINDOW = 128
assert WINDOW == A_BLOCK

B_WIDTH = 512
B_HEADS = 4
B_DK = 64
B_DV = 128
B_QK_WIDTH = 256
B_GATE_RANK = 16
B_GATE_TAU = 16.0
GLA_BLOCK = 64
GLA_SUB = 8
GLA_DIAG = 16
GLA_ANCHOR_SEGS = (32, 64)
assert GLA_ANCHOR_SEGS[0] == 2 * GLA_DIAG and GLA_ANCHOR_SEGS[-1] == GLA_BLOCK
OUT_PARTS = 2

C_WIDTH = 512
C_GROUP_CH = 16
C_GROUPS = 32
C_STATE = 64
S5_CHUNK = 16
S5_NCHUNK = SEQ // S5_CHUNK
S5_TC = S5_CHUNK * C_GROUP_CH
S5_GB = 8
S5_NGB = C_GROUPS // S5_GB
S5_PAIRS_PER_GB = S5_GB // 2
S5_POW_ROWS = 24
S5PREP_PAIRS = 4

LANES = 128
LR_PAD = LANES

V7X_VMEM_LIMIT = 56 * 1024 * 1024

PROJ_TM = 1024
OUT_TM = 1024

_W_IN_SIZES = (("aq", A_WIDTH), ("ak", A_KV_WIDTH), ("av", A_KV_WIDTH), ("ag", A_WIDTH), ("bq", B_QK_WIDTH),
               ("bk", B_QK_WIDTH), ("bv", B_WIDTH), ("blr", B_GATE_RANK), ("bg", B_WIDTH), ("cu", C_WIDTH),
               ("cg", C_WIDTH))
_W_IN_OFF = {}
_off = 0
for _name, _w in _W_IN_SIZES:
    _W_IN_OFF[_name] = (_off, _w)
    _off += _w
W_IN_COLS = _off
_PROJ_OUTS = (("ak", BF16), ("ag", BF16), ("bq", BF16), ("bk", BF16),
              ("bv", BF16), ("bg", BF16), ("cu", F32), ("cg", BF16))


def _silu(x):
    return x * jax.nn.sigmoid(x)


def _cparams(semantics):
    return pltpu.CompilerParams(dimension_semantics=semantics, vmem_limit_bytes=V7X_VMEM_LIMIT)


def _mod_kernel(c_ref, w_ref, b_ref, o_ref):
    c = c_ref[...]
    o_ref[0] = jnp.dot(_silu(c).astype(BF16), w_ref[0].astype(BF16), preferred_element_type=F32) + b_ref[0]


def _mod(c_pad, w_mod, b_mod):
    layers = w_mod.shape[0]
    n = 3 * D_MODEL
    tn = 768
    return pl.pallas_call(
        _mod_kernel,
        out_shape=jax.ShapeDtypeStruct((layers, 8, n), F32),
        grid=(layers, n // tn),
        in_specs=[pl.BlockSpec((8, D_MODEL), lambda l, j: (0, 0)),
                  pl.BlockSpec((1, D_MODEL, tn), lambda l, j: (l, 0, j)),
                  pl.BlockSpec((1, 1, tn), lambda l, j: (l, 0, j))],
        out_specs=pl.BlockSpec((1, 8, tn), lambda l, j: (l, 0, j)),
        compiler_params=_cparams(("arbitrary", "arbitrary")),
        name="mod",
    )(c_pad, w_mod, b_mod.reshape(layers, 1, n))


def _proj_kernel(x_ref, scale_ref, shift_ref, gpre_ref, wt_ref, walpha_ref, balpha_ref, *out_refs):
    x = x_ref[...]
    ms = jnp.mean(x * x, axis=-1, keepdims=True)
    y = x * lax.rsqrt(ms + EPS) * gpre_ref[...]
    h = (y * (1.0 + scale_ref[0]) + shift_ref[0]).astype(BF16)
    nt = (((1,), (1,)), ((), ()))

    def rows(name, width=None):
        off, w = _W_IN_OFF[name]
        return wt_ref[off:off + (width or w), :]

    for (name, _), o_ref in zip(_PROJ_OUTS, out_refs):
        o_ref[...] = lax.dot_general(h, rows(name), nt, preferred_element_type=F32).astype(o_ref.dtype)
    lr = lax.dot_general(h, rows("blr", LR_PAD), nt, preferred_element_type=F32).astype(BF16)
    logits = jnp.dot(lr, walpha_ref[...], preferred_element_type=F32) + balpha_ref[...]
    log_sig = jnp.minimum(logits, 0.0) - jnp.log(1.0 + jnp.exp(-jnp.abs(logits)))
    la_ref, qt_ref, vt_ref = out_refs[len(_PROJ_OUTS):]
    la_ref[...] = log_sig * (1.0 / B_GATE_TAU)
    qt = lax.dot_general(rows("aq"), h, nt, preferred_element_type=F32) * (A_HEAD_DIM ** -0.5)
    qt_ref[...] = qt.astype(qt_ref.dtype)
    vt_ref[...] = lax.dot_general(rows("av"), h, nt, preferred_element_type=F32).astype(vt_ref.dtype)


def _proj(x2, scale, shift, g_pre, w_t, layer, w_alpha_pad, b_alpha):
    tm = PROJ_TM
    steps_per_batch = SEQ // tm
    row = lambda i: (i, 0)
    col = lambda i: (0, i)
    per_batch = lambda i: (i // steps_per_batch, 0, 0)
    const = lambda i: (0, 0)
    out_shape = [jax.ShapeDtypeStruct((TOKENS, _W_IN_OFF[n][1]), dt) for n, dt in _PROJ_OUTS]
    out_specs = [pl.BlockSpec((tm, _W_IN_OFF[n][1]), row) for n, _ in _PROJ_OUTS]
    out_shape += [jax.ShapeDtypeStruct((TOKENS, B_QK_WIDTH), F32),
                  jax.ShapeDtypeStruct((A_WIDTH, TOKENS), BF16),
                  jax.ShapeDtypeStruct((A_KV_WIDTH, TOKENS), BF16)]
    out_specs += [pl.BlockSpec((tm, B_QK_WIDTH), row),
                  pl.BlockSpec((A_WIDTH, tm), col),
                  pl.BlockSpec((A_KV_WIDTH, tm), col)]
    return pl.pallas_call(
        _proj_kernel,
        out_shape=out_shape,
        grid=(TOKENS // tm,),
        in_specs=[pl.BlockSpec((tm, D_MODEL), row),
                  pl.BlockSpec((1, 1, D_MODEL), per_batch),
                  pl.BlockSpec((1, 1, D_MODEL), per_batch),
                  pl.BlockSpec((1, D_MODEL), const),
                  pl.BlockSpec((W_IN_COLS, D_MODEL), lambda i: (layer, 0), pipeline_mode=pl.Buffered(1)),
                  pl.BlockSpec((LR_PAD, B_QK_WIDTH), const),
                  pl.BlockSpec((1, B_QK_WIDTH), const)],
        out_specs=out_specs,
        compiler_params=_cparams(("arbitrary",)),
        name="proj",
    )(x2, scale, shift, g_pre.reshape(1, D_MODEL), w_t, w_alpha_pad, b_alpha.reshape(1, B_QK_WIDTH))


def _attn_bias():
    j = np.arange(A_BLOCK)[:, None]
    i = np.arange(A_BLOCK)[None, :]
    dist = np.where(j > i, i + A_BLOCK - j, i - j).astype(np.float32)
    slopes = np.exp2(-8.0 * np.arange(1, A_Q_HEADS + 1, dtype=np.float32) / A_Q_HEADS).astype(np.float32)
    bias = -slopes[:, None, None] * dist[None]
    first = np.where((j > i)[None], -np.inf, bias).astype(np.float32)
    return jnp.asarray(np.stack([bias, first]))


def _attention_steps(sink_ref, qt_ref, kp_ref, kc_ref, vtp_ref, vtc_ref, g_ref, bias_ref, oa_ref, first_tile,
                     n_blocks, group_blocks, after_block):
    kj = lax.broadcasted_iota(jnp.int32, (A_BLOCK, A_BLOCK), 0)
    qi = lax.broadcasted_iota(jnp.int32, (A_BLOCK, A_BLOCK), 1)
    from_prev = kj > qi
    zero_rows = jnp.zeros((A_HEAD_DIM, A_BLOCK), BF16)
    group = A_Q_HEADS // A_KV_HEADS
    blk = lambda j: slice(A_BLOCK * j, A_BLOCK * (j + 1))

    def keys(j, sl):
        prev = kp_ref[:, sl] if j == 0 else kc_ref[blk(j - 1), sl]
        return prev, kc_ref[blk(j), sl]

    def values(j, rows):
        prev = vtp_ref[rows, :] if j == 0 else vtc_ref[rows, blk(j - 1)]
        return jnp.concatenate([prev, vtc_ref[rows, blk(j)]], axis=1)

    def scores(j, hd):
        kvh = hd // group
        sl = slice(LANES * (kvh // 2), LANES * (kvh // 2 + 1))
        qh = qt_ref[A_HEAD_DIM * hd:A_HEAD_DIM * (hd + 1), blk(j)]
        qsel = jnp.concatenate([qh, zero_rows] if kvh % 2 == 0 else [zero_rows, qh], axis=0)
        k_prev, k_cur = keys(j, sl)
        return (jnp.dot(k_prev, qsel, preferred_element_type=F32),
                jnp.dot(k_cur, qsel, preferred_element_type=F32))

    def attend(j, hd, s_prev, s_cur):
        kvh = hd // group
        v_both = values(j, slice(A_HEAD_DIM * kvh, A_HEAD_DIM * (kvh + 1)))
        table = jnp.where(first_tile, 1, 0) if j == 0 else 0
        s = jnp.where(from_prev, s_prev, s_cur) + bias_ref[table, hd]
        sink = sink_ref[hd]
        m = jnp.maximum(jnp.max(s, axis=0, keepdims=True), sink)
        p = jnp.exp(s - m)
        den = jnp.sum(p, axis=0, keepdims=True) + jnp.exp(sink - m)
        p_both = jnp.concatenate([jnp.where(from_prev, p, 0.0), jnp.where(from_prev, 0.0, p)],
                                 axis=0).astype(BF16)
        return jnp.dot(v_both, p_both, preferred_element_type=F32) / den

    pending = {}

    def issue_scores(first):
        for j in range(first, min(first + group_blocks, n_blocks)):
            for hd in range(A_Q_HEADS):
                pending[j, hd] = scores(j, hd)

    issue_scores(0)
    for j in range(n_blocks):
        if (j + 1) % group_blocks == 0:
            issue_scores(j + 1)
        outs = {}
        for hd in range(A_Q_HEADS):
            outs[hd] = attend(j, hd, *pending.pop((j, hd)))
            if hd % 2 == 1:
                qsl = slice(LANES * (hd // 2), LANES * (hd // 2 + 1))
                o_pair = jnp.concatenate([outs.pop(hd - 1), outs.pop(hd)], axis=0).T
                gate = g_ref[blk(j), qsl].astype(F32)
                oa_ref[blk(j), qsl] = (o_pair * _silu(gate)).astype(oa_ref.dtype)
        after_block(j)


def _gla_kernel(la_ref, q_ref, k_ref, v_ref, g_ref, gn_ref, o_ref, st_ref):
    cb = GLA_BLOCK

    @pl.when(pl.program_id(0) == 0)
    def _():
        st_ref[...] = jnp.zeros_like(st_ref)

    r = lax.broadcasted_iota(jnp.int32, (cb, cb), 0)
    c = lax.broadcasted_iota(jnp.int32, (cb, cb), 1)
    tri = (c <= r).astype(F32)
    lane = lax.broadcasted_iota(jnp.int32, (1, B_QK_WIDTH), 1)
    head_masks = [(lane >= B_DK * h) & (lane < B_DK * (h + 1)) for h in range(B_HEADS)]
    rr = lax.broadcasted_iota(jnp.int32, (B_HEADS * cb, cb), 0)
    cc = lax.broadcasted_iota(jnp.int32, (B_HEADS * cb, cb), 1)
    ri = rr & (cb - 1)
    nt = (((1,), (1,)), ((), ()))
    tn = (((0,), (0,)), ((), ()))
    rows = lambda u: slice(cb * u, cb * (u + 1))
    items = [(u, b) for u in range(GLA_SUB) for b in range(BATCH)]
    bcs = {(u, b): jnp.dot(tri, la_ref[b, rows(u), :], preferred_element_type=F32, precision=HIGHEST)
           for u, b in items}
    stack_heads = lambda a: jnp.concatenate([jnp.where(m, a, 0.0) for m in head_masks], axis=0).astype(BF16)
    row = lax.broadcasted_iota(jnp.int32, (cb, 1), 0)
    diag = ((ri ^ cc) < GLA_DIAG) & (cc <= ri)
    same_seg = {seg: (ri ^ cc) < seg for seg in GLA_ANCHOR_SEGS}

    def upper(a, seg):
        return jnp.concatenate([a[s0 + seg // 2:s0 + seg] for s0 in range(0, cb, seg)], axis=0)

    def place_upper(p, seg):
        half = seg // 2
        zero = jnp.zeros((half, cb), F32)
        parts = []
        for piece in range(B_HEADS * cb // seg):
            parts += [zero, p[half * piece:half * (piece + 1)]]
        return jnp.concatenate(parts, axis=0)

    def anchor_rows(bc, seg, offset, reps):
        parts = []
        for s0 in range(0, cb, seg):
            a = s0 + offset - 1
            val = bc[a:a + 1, :] if a >= 0 else jnp.zeros((1, B_QK_WIDTH), F32)
            parts.append(jnp.broadcast_to(val, (reps, B_QK_WIDTH)))
        return jnp.concatenate(parts, axis=0)

    q_lv, k_lv, qsts, ksts, decs, vs = {}, {}, {}, {}, {}, {}
    for it in items:
        u, b = it
        bc = bcs[it]
        bl = bc[cb - 1:cb, :]
        q = q_ref[b, rows(u), :].astype(F32) * (B_DK ** -0.5)
        k = k_ref[b, rows(u), :].astype(F32)
        for seg in GLA_ANCHOR_SEGS:
            half = seg // 2
            q_lv[it, seg] = stack_heads(upper(q, seg) * jnp.exp(upper(bc, seg) - anchor_rows(bc, seg, half, half)))
            in_lower = (row & (seg - 1)) < half
            k_lv[it, seg] = jnp.where(in_lower, k * jnp.exp(anchor_rows(bc, seg, half, seg) - bc), 0.0).astype(BF16)
        anc = anchor_rows(bc, GLA_DIAG, 0, GLA_DIAG)
        q_lv[it, 0] = stack_heads(q * jnp.exp(bc - anc))
        k_lv[it, 0] = (k * jnp.exp(anc - bc)).astype(BF16)
        qsts[it] = stack_heads(q * jnp.exp(bc))
        ksts[it] = stack_heads(k * jnp.exp(bl - bc))
        decs[it] = jnp.exp(bl)
        vs[it] = v_ref[b, rows(u), :]
    a_alls = {}
    for it in items:
        prod = lambda lv: lax.dot_general(q_lv[it, lv], k_lv[it, lv], nt, preferred_element_type=F32)
        a = jnp.where(diag, prod(0), 0.0)
        for seg in GLA_ANCHOR_SEGS:
            a = a + jnp.where(same_seg[seg], place_upper(prod(seg), seg), 0.0)
        a_alls[it] = a.astype(BF16)
    upds = {}
    for it in items:
        vst = jnp.concatenate([vs[it][:, B_DV * h:B_DV * (h + 1)] for h in range(B_HEADS)], axis=0)
        upds[it] = lax.dot_general(vst, ksts[it], tn, preferred_element_type=F32)
    st_in = {}
    for b in range(BATCH):
        st = st_ref[b]
        for u in range(GLA_SUB):
            st_in[u, b] = st
            st = st * decs[u, b] + upds[u, b]
        st_ref[b] = st
    oi_alls = {it: lax.dot_general(qsts[it], st_in[it].astype(BF16), nt, preferred_element_type=F32)
               for it in items}
    o_hs = {}
    for it in items:
        a_all = a_alls[it]
        for h in range(B_HEADS):
            o_hs[it, h] = (jnp.dot(a_all[cb * h:cb * (h + 1)], vs[it][:, B_DV * h:B_DV * (h + 1)],
                                   preferred_element_type=F32) + oi_alls[it][cb * h:cb * (h + 1)])
    for it in items:
        u, b = it
        for h in range(B_HEADS):
            vsl = slice(B_DV * h, B_DV * (h + 1))
            o_h = o_hs[it, h]
            ms = jnp.mean(o_h * o_h, axis=-1, keepdims=True)
            o_n = o_h * lax.rsqrt(ms + EPS) * gn_ref[:, vsl]
            gate = g_ref[b, rows(u), vsl].astype(F32)
            o_ref[b, rows(u), vsl] = (o_n * _silu(gate)).astype(o_ref.dtype)


def _gla(log_a, bq, bk, bv, bg, g_gla):
    cb = GLA_BLOCK * GLA_SUB
    blk = lambda w: pl.BlockSpec((BATCH, cb, w), lambda i: (0, i, 0))
    r3 = lambda a: a.reshape(BATCH, SEQ, a.shape[-1])
    out = pl.pallas_call(
        _gla_kernel,
        out_shape=jax.ShapeDtypeStruct((BATCH, SEQ, B_WIDTH), BF16),
        grid=(SEQ // cb,),
        in_specs=[blk(B_QK_WIDTH), blk(B_QK_WIDTH), blk(B_QK_WIDTH), blk(B_WIDTH), blk(B_WIDTH),
                  pl.BlockSpec((1, B_WIDTH), lambda i: (0, 0))],
        out_specs=blk(B_WIDTH),
        scratch_shapes=[pltpu.VMEM((BATCH, B_DV, B_QK_WIDTH), F32)],
        compiler_params=_cparams(("arbitrary",)),
        name="gla",
    )(r3(log_a), r3(bq), r3(bk), r3(bv), r3(bg), g_gla.reshape(1, B_WIDTH))
    return out.reshape(TOKENS, B_WIDTH)


def _s5prep_kernel(ar_ref, ai_ref, ldt_ref, bre_ref, bim_ref, btre_ref, btim_ref, cre_ref, cim_ref,
                   mt_ref, wet_ref, wyt_ref, are_ref, aim_ref):
    p = C_STATE
    for pr in range(S5PREP_PAIRS):
        wet_re, wet_im, wyt_rows, a_re, a_im = [], [], [], [], []
        for g in range(2):
            i = 2 * pr + g
            kk, e_re, e_im, y_re, y_im, p_re, p_im = _s5_discretise(
                ar_ref[i], ai_ref[i], ldt_ref[i], bre_ref[i], bim_ref[i], btre_ref[i], btim_ref[i],
                cre_ref[i], cim_ref[i])
            pieces = [kk] + [jnp.concatenate([jnp.zeros((C_GROUP_CH * s, C_GROUP_CH), F32),
                                              kk[:S5_TC - C_GROUP_CH * s]], axis=0) for s in range(1, S5_CHUNK)]
            mt_ref[pr, g] = jnp.concatenate(pieces, axis=1).astype(BF16)
            e_t = jnp.concatenate([e_re, e_im], axis=1).T
            zero = jnp.zeros((p, S5_TC), F32)
            wet_re.append(jnp.concatenate([e_t[:p], zero] if g == 0 else [zero, e_t[:p]], axis=1))
            wet_im.append(jnp.concatenate([e_t[p:], zero] if g == 0 else [zero, e_t[p:]], axis=1))
            zero = jnp.zeros((S5_TC, p), F32)
            wyt_rows.append(
                jnp.concatenate([y_re, zero, -y_im, zero] if g == 0 else [zero, y_re, zero, -y_im], axis=1))
            a_re.append(p_re)
            a_im.append(p_im)
        wet_ref[pr] = jnp.concatenate(wet_re + wet_im, axis=0).astype(BF16)
        wyt_ref[pr] = jnp.concatenate(wyt_rows, axis=0).astype(BF16)
        are_ref[pr] = jnp.broadcast_to(jnp.concatenate(a_re, axis=1), (8, 2 * p))
        aim_ref[pr] = jnp.broadcast_to(jnp.concatenate(a_im, axis=1), (8, 2 * p))


def _s5_discretise(ar, ai, ldt, b_re, b_im, bt_re16, bt_im16, c_re16, c_im16):
    dt = jnp.exp(ldt)

    def cmul(xr, xi, yr, yi):
        return xr * yr - xi * yi, xr * yi + xi * yr

    kf = lax.broadcasted_iota(jnp.int32, (S5_POW_ROWS, 1), 0).astype(F32)
    mag = jnp.exp(kf * (ar * dt))
    ang = kf * (ai * dt)
    pw_re, pw_im = mag * jnp.cos(ang), mag * jnp.sin(ang)
    abar_re, abar_im = pw_re[1:2], pw_im[1:2]
    den = ar * ar + ai * ai
    num_re = abar_re - 1.0
    f_re = (num_re * ar + abar_im * ai) / den
    f_im = (abar_im * ar - num_re * ai) / den
    g_re, g_im = cmul(pw_re, pw_im, f_re, f_im)

    def pick(which, xr, xi):
        rep = lambda x: jnp.concatenate(
            [jnp.broadcast_to(x[which(i):which(i) + 1], (C_GROUP_CH, C_STATE)) for i in range(S5_CHUNK)], axis=0)
        return rep(xr), rep(xi)

    tile16 = lambda a: jnp.concatenate([a] * S5_CHUNK, axis=0)
    ct_re, ct_im = tile16(c_re16), tile16(c_im16)
    bt_re, bt_im = tile16(bt_re16), tile16(bt_im16)

    w_re, w_im = cmul(*pick(lambda i: i, g_re, g_im), ct_re, ct_im)
    kk = (jnp.dot(w_re, b_re, preferred_element_type=F32, precision=HIGHEST)
          - jnp.dot(w_im, b_im, preferred_element_type=F32, precision=HIGHEST))
    e_re, e_im = cmul(*pick(lambda i: S5_CHUNK - 1 - i, g_re, g_im), bt_re, bt_im)
    y_re, y_im = cmul(*pick(lambda i: i + 1, pw_re, pw_im), ct_re, ct_im)
    return kk, e_re, e_im, y_re, y_im, pw_re[S5_CHUNK:S5_CHUNK + 1], pw_im[S5_CHUNK:S5_CHUNK + 1]


def _s5prep(a_re, a_im, log_dt, b_re, b_im, c_re, c_im, d):
    p, ch = C_STATE, C_GROUP_CH
    layers = a_re.shape[0]
    g = layers * C_GROUPS
    npair = g // 2
    flat = lambda a: a.reshape(g, *a.shape[2:])
    a_re, a_im, log_dt, b_re, b_im, c_re, c_im = map(flat, (a_re, a_im, log_dt, b_re, b_im, c_re, c_im))
    row = lambda a: a.reshape(g, 1, p)
    ldt = jnp.broadcast_to(log_dt[:, None, None], (g, 1, p))
    b_t = lambda a: jnp.swapaxes(a, 1, 2)
    pps = S5PREP_PAIRS
    spec = lambda s1, s2: pl.BlockSpec((2 * pps, s1, s2), lambda i: (i, 0, 0))
    mt, wet, wyt, pw_re, pw_im = pl.pallas_call(
        _s5prep_kernel,
        out_shape=[jax.ShapeDtypeStruct((npair, 2, S5_TC, S5_TC), BF16),
                   jax.ShapeDtypeStruct((npair, 4 * p, 2 * S5_TC), BF16),
                   jax.ShapeDtypeStruct((npair, 2 * S5_TC, 4 * p), BF16),
                   jax.ShapeDtypeStruct((npair, 8, 2 * p), F32),
                   jax.ShapeDtypeStruct((npair, 8, 2 * p), F32)],
        grid=(npair // pps,),
        in_specs=[spec(1, p), spec(1, p), spec(1, p), spec(p, ch), spec(p, ch),
                  spec(ch, p), spec(ch, p), spec(ch, p), spec(ch, p)],
        out_specs=[pl.BlockSpec((pps, 2, S5_TC, S5_TC), lambda i: (i, 0, 0, 0)),
                   pl.BlockSpec((pps, 4 * p, 2 * S5_TC), lambda i: (i, 0, 0)),
                   pl.BlockSpec((pps, 2 * S5_TC, 4 * p), lambda i: (i, 0, 0)),
                   pl.BlockSpec((pps, 8, 2 * p), lambda i: (i, 0, 0)),
                   pl.BlockSpec((pps, 8, 2 * p), lambda i: (i, 0, 0))],
        compiler_params=_cparams(("arbitrary",)),
        name="s5prep",
    )(row(a_re), row(a_im), ldt, b_re, b_im, b_t(b_re), b_t(b_im), c_re, c_im)
    by_gb = lambda a: a.reshape(layers * S5_NGB, S5_PAIRS_PER_GB, *a.shape[1:])
    return (mt.reshape(layers * S5_NGB, S5_GB, S5_TC, S5_TC), by_gb(wet), by_gb(wyt), by_gb(pw_re), by_gb(pw_im),
            d.reshape(layers, 1, C_WIDTH))


def _s5_kernel(u_ref, mt_ref, wet_ref, wyt_ref, are_ref, aim_ref, d_ref, y_ref,
               ut_ref, yt_ref, ere_ref, eim_ref, hre_ref, him_ref):
    nck, t_len, ch = S5_NCHUNK, S5_CHUNK, C_GROUP_CH
    nt = (((1,), (1,)), ((), ()))
    for t in range(t_len):
        xt = u_ref[pl.ds(t, nck, stride=t_len), :].T
        for g in range(S5_GB):
            ut_ref[g, ch * t:ch * (t + 1), :] = xt[ch * g:ch * (g + 1), :]
    for j in range(S5_PAIRS_PER_GB):
        u0 = ut_ref[2 * j].astype(BF16)
        u1 = ut_ref[2 * j + 1].astype(BF16)
        et = jnp.dot(wet_ref[0, j], jnp.concatenate([u0, u1], axis=0), preferred_element_type=F32)
        e = et.T
        ere_ref[:, LANES * j:LANES * (j + 1)] = e[:, :LANES]
        eim_ref[:, LANES * j:LANES * (j + 1)] = e[:, LANES:]
        yt_ref[2 * j] = jnp.dot(mt_ref[0, 2 * j], u0, preferred_element_type=F32)
        yt_ref[2 * j + 1] = jnp.dot(mt_ref[0, 2 * j + 1], u1, preferred_element_type=F32)

    a_re = jnp.concatenate([are_ref[0, j, 0:1, :] for j in range(S5_PAIRS_PER_GB)], axis=1)
    a_im = jnp.concatenate([aim_ref[0, j, 0:1, :] for j in range(S5_PAIRS_PER_GB)], axis=1)

    def body(i, carry):
        h_re, h_im = carry
        hre_ref[pl.ds(i, 1), :] = h_re
        him_ref[pl.ds(i, 1), :] = h_im
        e_re = ere_ref[pl.ds(i, 1), :]
        e_im = eim_ref[pl.ds(i, 1), :]
        return a_re * h_re - a_im * h_im + e_re, a_re * h_im + a_im * h_re + e_im

    zero = jnp.zeros((1, S5_GB * C_STATE), F32)
    lax.fori_loop(0, nck, body, (zero, zero))

    for j in range(S5_PAIRS_PER_GB):
        sl = slice(LANES * j, LANES * (j + 1))
        hp = jnp.concatenate([hre_ref[:, sl], him_ref[:, sl]], axis=1).astype(BF16)
        yi = lax.dot_general(wyt_ref[0, j], hp, nt, preferred_element_type=F32)
        yt_ref[2 * j] += yi[:S5_TC]
        yt_ref[2 * j + 1] += yi[S5_TC:]
    for t in range(t_len):
        ytt = jnp.concatenate([yt_ref[g, ch * t:ch * (t + 1), :] for g in range(S5_GB)], axis=0)
        rows = pl.ds(t, nck, stride=t_len)
        y_ref[rows, :] = ytt.T + d_ref[...] * u_ref[rows, :]


def _s5(cu, mt, wet, wyt, pw_re, pw_im, d_row, layer):
    p4 = 4 * C_STATE
    tok = pl.BlockSpec((SEQ, LANES), lambda gb, b: (b, gb))
    per_gb = lambda *s: pl.BlockSpec((1,) + s, lambda gb, b: (layer * S5_NGB + gb,) + (0,) * len(s))
    state = pltpu.VMEM((S5_NCHUNK, S5_GB * C_STATE), F32)
    return pl.pallas_call(
        _s5_kernel,
        out_shape=jax.ShapeDtypeStruct((TOKENS, C_WIDTH), F32),
        grid=(S5_NGB, BATCH),
        in_specs=[tok,
                  per_gb(S5_GB, S5_TC, S5_TC),
                  per_gb(S5_PAIRS_PER_GB, p4, 2 * S5_TC),
                  per_gb(S5_PAIRS_PER_GB, 2 * S5_TC, p4),
                  per_gb(S5_PAIRS_PER_GB, 8, 2 * C_STATE), per_gb(S5_PAIRS_PER_GB, 8, 2 * C_STATE),
                  pl.BlockSpec((1, LANES), lambda gb, b: (0, gb))],
        out_specs=tok,
        scratch_shapes=[pltpu.VMEM((S5_GB, S5_TC, S5_NCHUNK), F32),
                        pltpu.VMEM((S5_GB, S5_TC, S5_NCHUNK), F32),
                        state, state, state, state],
        compiler_params=_cparams(("arbitrary", "arbitrary")),
        name="s5",
    )(cu, mt, wet, wyt, pw_re, pw_im, d_row)


def _gelu_tanh(x):
    return 0.5 * x * (1.0 + jnp.tanh(math.sqrt(2.0 / math.pi) * (x + 0.044715 * (x * x * x))))


def _out_kernel(sink_ref, qt_ref, kp_ref, kc_ref, vtp_ref, vtc_ref, ag_ref, bias_ref,
                ob_ref, yc_ref, cg_ref, x_ref, gate_ref, gpost_ref, wglu_ref, bglu_ref, wout_ref, o_ref, oa_ref):
    first_tile = pl.program_id(0) % (SEQ // OUT_TM) == 0
    blocks_per_part = OUT_TM // (A_BLOCK * OUT_PARTS)

    def project(part):
        r = slice(A_BLOCK * blocks_per_part * part, A_BLOCK * blocks_per_part * (part + 1))
        y = _gelu_tanh(yc_ref[r, :])
        z = jnp.dot(y.astype(BF16), wglu_ref[...], preferred_element_type=F32) + bglu_ref[...]
        y = y * jax.nn.sigmoid(z)
        oc = (y * _silu(cg_ref[r, :].astype(F32))).astype(BF16)
        mix = jnp.concatenate([oa_ref[r, :], ob_ref[r, :], oc], axis=1)
        acc = jnp.dot(mix, wout_ref[...], preferred_element_type=F32)
        ms = jnp.mean(acc * acc, axis=-1, keepdims=True)
        out = acc * lax.rsqrt(ms + EPS) * gpost_ref[...]
        o_ref[r, :] = x_ref[r, :] + gate_ref[0] * out

    def after_block(j):
        if (j + 1) % blocks_per_part == 0:
            project(j // blocks_per_part)

    _attention_steps(sink_ref, qt_ref, kp_ref, kc_ref, vtp_ref, vtc_ref, ag_ref, bias_ref, oa_ref, first_tile,
                     OUT_TM // A_BLOCK, blocks_per_part, after_block)


def _out(sinks, qt, k, vt, ag, bias, ob, yc, cg, x2, gate, g_post, w_glu, b_glu, w_out, layer):
    tm = OUT_TM
    steps_per_batch = SEQ // tm
    blocks = tm // A_BLOCK
    row = lambda i: (i, 0)
    col = lambda i: (0, i)
    const = lambda i: (0, 0)
    once = dict(pipeline_mode=pl.Buffered(1))
    prev = lambda i: (i * blocks - jnp.minimum(i % steps_per_batch, 1), 0)
    prev_t = lambda i: (0, i * blocks - jnp.minimum(i % steps_per_batch, 1))
    return pl.pallas_call(
        _out_kernel,
        out_shape=jax.ShapeDtypeStruct((TOKENS, D_MODEL), F32),
        grid=(TOKENS // tm,),
        in_specs=[pl.BlockSpec(memory_space=pltpu.SMEM),
                  pl.BlockSpec((A_WIDTH, tm), col),
                  pl.BlockSpec((A_BLOCK, A_KV_WIDTH), prev),
                  pl.BlockSpec((tm, A_KV_WIDTH), row),
                  pl.BlockSpec((A_KV_WIDTH, A_BLOCK), prev_t),
                  pl.BlockSpec((A_KV_WIDTH, tm), col),
                  pl.BlockSpec((tm, A_WIDTH), row),
                  pl.BlockSpec((2, A_Q_HEADS, A_BLOCK, A_BLOCK), lambda i: (0, 0, 0, 0), **once),
                  pl.BlockSpec((tm, B_WIDTH), row),
                  pl.BlockSpec((tm, C_WIDTH), row),
                  pl.BlockSpec((tm, C_WIDTH), row),
                  pl.BlockSpec((tm, D_MODEL), row),
                  pl.BlockSpec((1, 1, D_MODEL), lambda i: (i // steps_per_batch, 0, 0)),
                  pl.BlockSpec((1, D_MODEL), const),
                  pl.BlockSpec((C_WIDTH, C_WIDTH), lambda i: (layer, 0), **once),
                  pl.BlockSpec((1, C_WIDTH), const),
                  pl.BlockSpec((2 * D_MODEL, D_MODEL), lambda i: (layer, 0), **once)],
        out_specs=pl.BlockSpec((tm, D_MODEL), row),
        scratch_shapes=[pltpu.VMEM((tm, A_WIDTH), BF16)],
        compiler_params=_cparams(("arbitrary",)),
        name="out",
    )(sinks, qt, k, k, vt, vt, ag, bias, ob, yc, cg, x2, gate, g_post.reshape(1, D_MODEL), w_glu,
      b_glu.reshape(1, C_WIDTH), w_out)


def kernel(x, c, w_mod, b_mod, g_pre, g_post, w_in, attn_sinks, gla_w_alpha, gla_b_alpha, gla_norm_g,
           s5_a_re, s5_a_im, s5_log_dt, s5_b_re, s5_b_im, s5_c_re, s5_c_im, s5_d, s5_w_glu, s5_b_glu, w_out):
    layers = w_mod.shape[0]
    x2 = x.reshape(TOKENS, D_MODEL)
    bias = _attn_bias()
    mod = _mod(jnp.pad(c, ((0, 8 - BATCH), (0, 0))), w_mod, b_mod)[:, :BATCH]
    shift, scale, gate = (m.reshape(layers, BATCH, 1, D_MODEL) for m in jnp.split(mod, 3, axis=-1))
    w_t = jnp.swapaxes(w_in, 1, 2).astype(BF16).reshape(layers * W_IN_COLS, D_MODEL)
    w_alpha_pad = jnp.pad(gla_w_alpha, ((0, 0), (0, LR_PAD - B_GATE_RANK), (0, 0))).astype(BF16)
    s5_ops = _s5prep(s5_a_re, s5_a_im, s5_log_dt, s5_b_re, s5_b_im, s5_c_re, s5_c_im, s5_d)
    *s5_ops, s5_d_rows = s5_ops
    w_glu = s5_w_glu.astype(BF16).reshape(layers * C_WIDTH, C_WIDTH)
    w_out_b = w_out.astype(BF16).reshape(layers * 2 * D_MODEL, D_MODEL)
    for l in range(layers):
        ak, ag, bq, bk, bv, bg, cu, cg, log_a, aqt, avt = _proj(
            x2, scale[l], shift[l], g_pre[l], w_t, l, w_alpha_pad[l], gla_b_alpha[l])
        o_b = _gla(log_a, bq, bk, bv, bg, gla_norm_g[l])
        y_c = _s5(cu, *s5_ops, s5_d_rows[l], l)
        x2 = _out(attn_sinks[l], aqt, ak, avt, ag, bias, o_b, y_c, cg, x2, gate[l], g_post[l], w_glu, s5_b_glu[l],
                  w_out_b, l)
    return x2.reshape(x.shape)
```

```python
import math

import jax
import jax.numpy as jnp
import numpy as np
from jax import lax
from jax.experimental import pallas as pl
from jax.experimental.pallas import tpu as pltpu

F32 = jnp.float32
BF16 = jnp.bfloat16
HIGHEST = lax.Precision.HIGHEST

D_MODEL = 1024
BATCH = 4
SEQ = 4096
TOKENS = BATCH * SEQ
EPS = 1e-6

A_WIDTH = 1024
A_HEAD_DIM = 64
A_Q_HEADS = 16
A_KV_HEADS = 4
A_KV_WIDTH = A_KV_HEADS * A_HEAD_DIM
A_BLOCK = 128
WINDOW = 128
assert WINDOW == A_BLOCK

B_WIDTH = 512
B_HEADS = 4
B_DK = 64
B_DV = 128
B_QK_WIDTH = 256
B_GATE_RANK = 16
B_GATE_TAU = 16.0
GLA_BLOCK = 64
GLA_SUB = 8
GLA_DIAG = 16
GLA_ANCHOR_SEGS = (32, 64)
assert GLA_ANCHOR_SEGS[0] == 2 * GLA_DIAG and GLA_ANCHOR_SEGS[-1] == GLA_BLOCK
OUT_PARTS = 2

C_WIDTH = 512
C_GROUP_CH = 16
C_GROUPS = 32
C_STATE = 64
S5_CHUNK = 16
S5_NCHUNK = SEQ // S5_CHUNK
S5_TC = S5_CHUNK * C_GROUP_CH
S5_GB = 8
S5_NGB = C_GROUPS // S5_GB
S5_PAIRS_PER_GB = S5_GB // 2
S5_POW_ROWS = 24
S5PREP_PAIRS = 4

LANES = 128
LR_PAD = LANES

V7X_VMEM_LIMIT = 56 * 1024 * 1024

PROJ_TM = 1024
OUT_TM = 1024

_W_IN_SIZES = (("aq", A_WIDTH), ("ak", A_KV_WIDTH), ("av", A_KV_WIDTH), ("ag", A_WIDTH), ("bq", B_QK_WIDTH),
               ("bk", B_QK_WIDTH), ("bv", B_WIDTH), ("blr", B_GATE_RANK), ("bg", B_WIDTH), ("cu", C_WIDTH),
               ("cg", C_WIDTH))
_W_IN_OFF = {}
_off = 0
for _name, _w in _W_IN_SIZES:
    _W_IN_OFF[_name] = (_off, _w)
    _off += _w
W_IN_COLS = _off
_PROJ_OUTS = (("ak", BF16), ("ag", BF16), ("bq", BF16), ("bk", BF16),
              ("bv", BF16), ("bg", BF16), ("cu", F32), ("cg", BF16))


def _silu(x):
    return x * jax.nn.sigmoid(x)


def _cparams(semantics):
    return pltpu.CompilerParams(dimension_semantics=semantics, vmem_limit_bytes=V7X_VMEM_LIMIT)


def _mod_kernel(c_ref, w_ref, b_ref, o_ref):
    c = c_ref[...]
    o_ref[0] = jnp.dot(_silu(c).astype(BF16), w_ref[0].astype(BF16), preferred_element_type=F32) + b_ref[0]


def _mod(c_pad, w_mod, b_mod):
    layers = w_mod.shape[0]
    n = 3 * D_MODEL
    tn = 768
    return pl.pallas_call(
        _mod_kernel,
        out_shape=jax.ShapeDtypeStruct((layers, 8, n), F32),
        grid=(layers, n // tn),
        in_specs=[pl.BlockSpec((8, D_MODEL), lambda l, j: (0, 0)),
                  pl.BlockSpec((1, D_MODEL, tn), lambda l, j: (l, 0, j)),
                  pl.BlockSpec((1, 1, tn), lambda l, j: (l, 0, j))],
        out_specs=pl.BlockSpec((1, 8, tn), lambda l, j: (l, 0, j)),
        compiler_params=_cparams(("arbitrary", "arbitrary")),
        name="mod",
    )(c_pad, w_mod, b_mod.reshape(layers, 1, n))


def _proj_kernel(x_ref, scale_ref, shift_ref, gpre_ref, wt_ref, walpha_ref, balpha_ref, *out_refs):
    x = x_ref[...]
    ms = jnp.mean(x * x, axis=-1, keepdims=True)
    y = x * lax.rsqrt(ms + EPS) * gpre_ref[...]
    h = (y * (1.0 + scale_ref[0]) + shift_ref[0]).astype(BF16)
    nt = (((1,), (1,)), ((), ()))

    def rows(name, width=None):
        off, w = _W_IN_OFF[name]
        return wt_ref[off:off + (width or w), :]

    for (name, _), o_ref in zip(_PROJ_OUTS, out_refs):
        o_ref[...] = lax.dot_general(h, rows(name), nt, preferred_element_type=F32).astype(o_ref.dtype)
    lr = lax.dot_general(h, rows("blr", LR_PAD), nt, preferred_element_type=F32).astype(BF16)
    logits = jnp.dot(lr, walpha_ref[...], preferred_element_type=F32) + balpha_ref[...]
    log_sig = jnp.minimum(logits, 0.0) - jnp.log(1.0 + jnp.exp(-jnp.abs(logits)))
    la_ref, qt_ref, vt_ref = out_refs[len(_PROJ_OUTS):]
    la_ref[...] = log_sig * (math.log2(math.e) / B_GATE_TAU)
    qt = lax.dot_general(rows("aq"), h, nt, preferred_element_type=F32) * (A_HEAD_DIM ** -0.5)
    qt_ref[...] = qt.astype(qt_ref.dtype)
    vt_ref[...] = lax.dot_general(rows("av"), h, nt, preferred_element_type=F32).astype(vt_ref.dtype)


def _proj(x2, scale, shift, g_pre, w_t, layer, w_alpha_pad, b_alpha):
    tm = PROJ_TM
    steps_per_batch = SEQ // tm
    row = lambda i: (i, 0)
    col = lambda i: (0, i)
    per_batch = lambda i: (i // steps_per_batch, 0, 0)
    const = lambda i: (0, 0)
    out_shape = [jax.ShapeDtypeStruct((TOKENS, _W_IN_OFF[n][1]), dt) for n, dt in _PROJ_OUTS]
    out_specs = [pl.BlockSpec((tm, _W_IN_OFF[n][1]), row) for n, _ in _PROJ_OUTS]
    out_shape += [jax.ShapeDtypeStruct((TOKENS, B_QK_WIDTH), F32),
                  jax.ShapeDtypeStruct((A_WIDTH, TOKENS), BF16),
                  jax.ShapeDtypeStruct((A_KV_WIDTH, TOKENS), BF16)]
    out_specs += [pl.BlockSpec((tm, B_QK_WIDTH), row),
                  pl.BlockSpec((A_WIDTH, tm), col),
                  pl.BlockSpec((A_KV_WIDTH, tm), col)]
    return pl.pallas_call(
        _proj_kernel,
        out_shape=out_shape,
        grid=(TOKENS // tm,),
        in_specs=[pl.BlockSpec((tm, D_MODEL), row),
                  pl.BlockSpec((1, 1, D_MODEL), per_batch),
                  pl.BlockSpec((1, 1, D_MODEL), per_batch),
                  pl.BlockSpec((1, D_MODEL), const),
                  pl.BlockSpec((W_IN_COLS, D_MODEL), lambda i: (layer, 0), pipeline_mode=pl.Buffered(1)),
                  pl.BlockSpec((LR_PAD, B_QK_WIDTH), const),
                  pl.BlockSpec((1, B_QK_WIDTH), const)],
        out_specs=out_specs,
        compiler_params=_cparams(("arbitrary",)),
        name="proj",
    )(x2, scale, shift, g_pre.reshape(1, D_MODEL), w_t, w_alpha_pad, b_alpha.reshape(1, B_QK_WIDTH))


def _attn_bias():
    j = np.arange(A_BLOCK)[:, None]
    i = np.arange(A_BLOCK)[None, :]
    dist = np.where(j > i, i + A_BLOCK - j, i - j).astype(np.float32)
    slopes = np.exp2(-8.0 * np.arange(1, A_Q_HEADS + 1, dtype=np.float32) / A_Q_HEADS).astype(np.float32)
    bias = -slopes[:, None, None] * dist[None]
    first = np.where((j > i)[None], -np.inf, bias).astype(np.float32)
    return jnp.asarray(np.stack([bias, first]))


def _attention_steps(sink_ref, qt_ref, kp_ref, kc_ref, vtp_ref, vtc_ref, g_ref, bias_ref, oa_ref, first_tile,
                     n_blocks, group_blocks, after_block):
    kj = lax.broadcasted_iota(jnp.int32, (A_BLOCK, A_BLOCK), 0)
    qi = lax.broadcasted_iota(jnp.int32, (A_BLOCK, A_BLOCK), 1)
    from_prev = kj > qi
    zero_rows = jnp.zeros((A_HEAD_DIM, A_BLOCK), BF16)
    group = A_Q_HEADS // A_KV_HEADS
    blk = lambda j: slice(A_BLOCK * j, A_BLOCK * (j + 1))

    def keys(j, sl):
        prev = kp_ref[:, sl] if j == 0 else kc_ref[blk(j - 1), sl]
        return prev, kc_ref[blk(j), sl]

    def values(j, rows):
        prev = vtp_ref[rows, :] if j == 0 else vtc_ref[rows, blk(j - 1)]
        return jnp.concatenate([prev, vtc_ref[rows, blk(j)]], axis=1)

    def scores(j, hd):
        kvh = hd // group
        sl = slice(LANES * (kvh // 2), LANES * (kvh // 2 + 1))
        qh = qt_ref[A_HEAD_DIM * hd:A_HEAD_DIM * (hd + 1), blk(j)]
        qsel = jnp.concatenate([qh, zero_rows] if kvh % 2 == 0 else [zero_rows, qh], axis=0)
        k_prev, k_cur = keys(j, sl)
        return (jnp.dot(k_prev, qsel, preferred_element_type=F32),
                jnp.dot(k_cur, qsel, preferred_element_type=F32))

    def attend(j, hd, s_prev, s_cur):
        kvh = hd // group
        v_both = values(j, slice(A_HEAD_DIM * kvh, A_HEAD_DIM * (kvh + 1)))
        table = jnp.where(first_tile, 1, 0) if j == 0 else 0
        s = jnp.where(from_prev, s_prev, s_cur) + bias_ref[table, hd]
        sink = sink_ref[hd]
        m = jnp.maximum(jnp.max(s, axis=0, keepdims=True), sink)
        p = jnp.exp(s - m)
        den = jnp.sum(p, axis=0, keepdims=True) + jnp.exp(sink - m)
        p_both = jnp.concatenate([jnp.where(from_prev, p, 0.0), jnp.where(from_prev, 0.0, p)],
                                 axis=0).astype(BF16)
        return jnp.dot(v_both, p_both, preferred_element_type=F32) / den

    pending = {}

    def issue_scores(first):
        for j in range(first, min(first + group_blocks, n_blocks)):
            for hd in range(A_Q_HEADS):
                pending[j, hd] = scores(j, hd)

    issue_scores(0)
    for j in range(n_blocks):
        if (j + 1) % group_blocks == 0:
            issue_scores(j + 1)
        outs = {}
        for hd in range(A_Q_HEADS):
            outs[hd] = attend(j, hd, *pending.pop((j, hd)))
            if hd % 2 == 1:
                qsl = slice(LANES * (hd // 2), LANES * (hd // 2 + 1))
                o_pair = jnp.concatenate([outs.pop(hd - 1), outs.pop(hd)], axis=0).T
                gate = g_ref[blk(j), qsl].astype(F32)
                oa_ref[blk(j), qsl] = (o_pair * _silu(gate)).astype(oa_ref.dtype)
        after_block(j)


def _gla_kernel(la_ref, q_ref, k_ref, v_ref, g_ref, gn_ref, o_ref, st_ref):
    cb = GLA_BLOCK

    @pl.when(pl.program_id(0) == 0)
    def _():
        st_ref[...] = jnp.zeros_like(st_ref)

    r = lax.broadcasted_iota(jnp.int32, (cb, cb), 0)
    c = lax.broadcasted_iota(jnp.int32, (cb, cb), 1)
    tri = (c <= r).astype(BF16)
    lane = lax.broadcasted_iota(jnp.int32, (1, B_QK_WIDTH), 1)
    head_masks = [(lane >= B_DK * h) & (lane < B_DK * (h + 1)) for h in range(B_HEADS)]
    rr = lax.broadcasted_iota(jnp.int32, (B_HEADS * cb, cb), 0)
    cc = lax.broadcasted_iota(jnp.int32, (B_HEADS * cb, cb), 1)
    ri = rr & (cb - 1)
    nt = (((1,), (1,)), ((), ()))
    tn = (((0,), (0,)), ((), ()))
    rows = lambda u: slice(cb * u, cb * (u + 1))
    items = [(u, b) for u in range(GLA_SUB) for b in range(BATCH)]
    def cumsum(la):
        hi = la.astype(BF16)
        r1 = la - hi.astype(F32)
        mid = r1.astype(BF16)
        lo = (r1 - mid.astype(F32)).astype(BF16)
        parts = jnp.dot(tri, jnp.concatenate([hi, mid, lo], axis=1), preferred_element_type=F32)
        w = B_QK_WIDTH
        return parts[:, :w] + (parts[:, w:2 * w] + parts[:, 2 * w:])

    bcs = {(u, b): cumsum(la_ref[b, rows(u), :]) for u, b in items}
    stack_heads = lambda a: jnp.concatenate([jnp.where(m, a, 0.0) for m in head_masks], axis=0).astype(BF16)
    row = lax.broadcasted_iota(jnp.int32, (cb, 1), 0)
    diag = ((ri ^ cc) < GLA_DIAG) & (cc <= ri)
    same_seg = {seg: (ri ^ cc) < seg for seg in GLA_ANCHOR_SEGS}

    def upper(a, seg):
        return jnp.concatenate([a[s0 + seg // 2:s0 + seg] for s0 in range(0, cb, seg)], axis=0)

    def place_upper(p, seg):
        half = seg // 2
        zero = jnp.zeros((half, cb), F32)
        parts = []
        for piece in range(B_HEADS * cb // seg):
            parts += [zero, p[half * piece:half * (piece + 1)]]
        return jnp.concatenate(parts, axis=0)

    def anchor_rows(bc, seg, offset, reps):
        parts = []
        for s0 in range(0, cb, seg):
            a = s0 + offset - 1
            val = bc[a:a + 1, :] if a >= 0 else jnp.zeros((1, B_QK_WIDTH), F32)
            parts.append(jnp.broadcast_to(val, (reps, B_QK_WIDTH)))
        return jnp.concatenate(parts, axis=0)

    q_lv, k_lv, qsts, ksts, decs, vs = {}, {}, {}, {}, {}, {}
    for it in items:
        u, b = it
        bc = bcs[it]
        bl = bc[cb - 1:cb, :]
        q = q_ref[b, rows(u), :].astype(F32) * (B_DK ** -0.5)
        k = k_ref[b, rows(u), :].astype(F32)
        for seg in GLA_ANCHOR_SEGS:
            half = seg // 2
            q_lv[it, seg] = stack_heads(upper(q, seg) * jnp.exp2(upper(bc, seg) - anchor_rows(bc, seg, half, half)))
            in_lower = (row & (seg - 1)) < half
            k_lv[it, seg] = jnp.where(in_lower, k * jnp.exp2(anchor_rows(bc, seg, half, seg) - bc), 0.0).astype(BF16)
        anc = anchor_rows(bc, GLA_DIAG, 0, GLA_DIAG)
        q_lv[it, 0] = stack_heads(q * jnp.exp2(bc - anc))
        k_lv[it, 0] = (k * jnp.exp2(anc - bc)).astype(BF16)
        qsts[it] = stack_heads(q * jnp.exp2(bc))
        ksts[it] = stack_heads(k * jnp.exp2(bl - bc))
        decs[it] = jnp.exp2(bl)
        vs[it] = v_ref[b, rows(u), :]
    a_alls = {}
    for it in items:
        prod = lambda lv: lax.dot_general(q_lv[it, lv], k_lv[it, lv], nt, preferred_element_type=F32)
        a = jnp.where(diag, prod(0), 0.0)
        for seg in GLA_ANCHOR_SEGS:
            a = a + jnp.where(same_seg[seg], place_upper(prod(seg), seg), 0.0)
        a_alls[it] = a.astype(BF16)
    upds = {}
    for it in items:
        vst = jnp.concatenate([vs[it][:, B_DV * h:B_DV * (h + 1)] for h in range(B_HEADS)], axis=0)
        upds[it] = lax.dot_general(vst, ksts[it], tn, preferred_element_type=F32)
    st_in = {}
    for b in range(BATCH):
        st = st_ref[b]
        for u in range(GLA_SUB):
            st_in[u, b] = st
            st = st * decs[u, b] + upds[u, b]
        st_ref[b] = st
    oi_alls = {it: lax.dot_general(qsts[it], st_in[it].astype(BF16), nt, preferred_element_type=F32)
               for it in items}
    o_hs = {}
    for it in items:
        a_all = a_alls[it]
        for h in range(B_HEADS):
            o_hs[it, h] = (jnp.dot(a_all[cb * h:cb * (h + 1)], vs[it][:, B_DV * h:B_DV * (h + 1)],
                                   preferred_element_type=F32) + oi_alls[it][cb * h:cb * (h + 1)])
    for it in items:
        u, b = it
        for h in range(B_HEADS):
            vsl = slice(B_DV * h, B_DV * (h + 1))
            o_h = o_hs[it, h]
            ms = jnp.mean(o_h * o_h, axis=-1, keepdims=True)
            o_n = o_h * lax.rsqrt(ms + EPS) * gn_ref[:, vsl]
            gate = g_ref[b, rows(u), vsl].astype(F32)
            o_ref[b, rows(u), vsl] = (o_n * _silu(gate)).astype(o_ref.dtype)


def _gla(log_a, bq, bk, bv, bg, g_gla):
    cb = GLA_BLOCK * GLA_SUB
    blk = lambda w: pl.BlockSpec((BATCH, cb, w), lambda i: (0, i, 0))
    r3 = lambda a: a.reshape(BATCH, SEQ, a.shape[-1])
    out = pl.pallas_call(
        _gla_kernel,
        out_shape=jax.ShapeDtypeStruct((BATCH, SEQ, B_WIDTH), BF16),
        grid=(SEQ // cb,),
        in_specs=[blk(B_QK_WIDTH), blk(B_QK_WIDTH), blk(B_QK_WIDTH), blk(B_WIDTH), blk(B_WIDTH),
                  pl.BlockSpec((1, B_WIDTH), lambda i: (0, 0))],
        out_specs=blk(B_WIDTH),
        scratch_shapes=[pltpu.VMEM((BATCH, B_DV, B_QK_WIDTH), F32)],
        compiler_params=_cparams(("arbitrary",)),
        name="gla",
    )(r3(log_a), r3(bq), r3(bk), r3(bv), r3(bg), g_gla.reshape(1, B_WIDTH))
    return out.reshape(TOKENS, B_WIDTH)


def _s5prep_kernel(ar_ref, ai_ref, ldt_ref, bre_ref, bim_ref, btre_ref, btim_ref, cre_ref, cim_ref,
                   mt_ref, wet_ref, wyt_ref, are_ref, aim_ref):
    p = C_STATE
    for pr in range(S5PREP_PAIRS):
        wet_re, wet_im, wyt_rows, a_re, a_im = [], [], [], [], []
        for g in range(2):
            i = 2 * pr + g
            kk, e_re, e_im, y_re, y_im, p_re, p_im = _s5_discretise(
                ar_ref[i], ai_ref[i], ldt_ref[i], bre_ref[i], bim_ref[i], btre_ref[i], btim_ref[i],
                cre_ref[i], cim_ref[i])
            pieces = [kk] + [jnp.concatenate([jnp.zeros((C_GROUP_CH * s, C_GROUP_CH), F32),
                                              kk[:S5_TC - C_GROUP_CH * s]], axis=0) for s in range(1, S5_CHUNK)]
            mt_ref[pr, g] = jnp.concatenate(pieces, axis=1).astype(BF16)
            e_t = jnp.concatenate([e_re, e_im], axis=1).T
            zero = jnp.zeros((p, S5_TC), F32)
            wet_re.append(jnp.concatenate([e_t[:p], zero] if g == 0 else [zero, e_t[:p]], axis=1))
            wet_im.append(jnp.concatenate([e_t[p:], zero] if g == 0 else [zero, e_t[p:]], axis=1))
            zero = jnp.zeros((S5_TC, p), F32)
            wyt_rows.append(
                jnp.concatenate([y_re, zero, -y_im, zero] if g == 0 else [zero, y_re, zero, -y_im], axis=1))
            a_re.append(p_re)
            a_im.append(p_im)
        wet_ref[pr] = jnp.concatenate(wet_re + wet_im, axis=0).astype(BF16)
        wyt_ref[pr] = jnp.concatenate(wyt_rows, axis=0).astype(BF16)
        are_ref[pr] = jnp.broadcast_to(jnp.concatenate(a_re, axis=1), (8, 2 * p))
        aim_ref[pr] = jnp.broadcast_to(jnp.concatenate(a_im, axis=1), (8, 2 * p))


def _s5_discretise(ar, ai, ldt, b_re, b_im, bt_re16, bt_im16, c_re16, c_im16):
    dt = jnp.exp(ldt)

    def cmul(xr, xi, yr, yi):
        return xr * yr - xi * yi, xr * yi + xi * yr

    kf = lax.broadcasted_iota(jnp.int32, (S5_POW_ROWS, 1), 0).astype(F32)
    mag = jnp.exp(kf * (ar * dt))
    ang = kf * (ai * dt)
    pw_re, pw_im = mag * jnp.cos(ang), mag * jnp.sin(ang)
    abar_re, abar_im = pw_re[1:2], pw_im[1:2]
    den = ar * ar + ai * ai
    num_re = abar_re - 1.0
    f_re = (num_re * ar + abar_im * ai) / den
    f_im = (abar_im * ar - num_re * ai) / den
    g_re, g_im = cmul(pw_re, pw_im, f_re, f_im)

    def pick(which, xr, xi):
        rep = lambda x: jnp.concatenate(
            [jnp.broadcast_to(x[which(i):which(i) + 1], (C_GROUP_CH, C_STATE)) for i in range(S5_CHUNK)], axis=0)
        return rep(xr), rep(xi)

    tile16 = lambda a: jnp.concatenate([a] * S5_CHUNK, axis=0)
    ct_re, ct_im = tile16(c_re16), tile16(c_im16)
    bt_re, bt_im = tile16(bt_re16), tile16(bt_im16)

    w_re, w_im = cmul(*pick(lambda i: i, g_re, g_im), ct_re, ct_im)
    kk = (jnp.dot(w_re, b_re, preferred_element_type=F32, precision=HIGHEST)
          - jnp.dot(w_im, b_im, preferred_element_type=F32, precision=HIGHEST))
    e_re, e_im = cmul(*pick(lambda i: S5_CHUNK - 1 - i, g_re, g_im), bt_re, bt_im)
    y_re, y_im = cmul(*pick(lambda i: i + 1, pw_re, pw_im), ct_re, ct_im)
    return kk, e_re, e_im, y_re, y_im, pw_re[S5_CHUNK:S5_CHUNK + 1], pw_im[S5_CHUNK:S5_CHUNK + 1]


def _s5prep(a_re, a_im, log_dt, b_re, b_im, c_re, c_im, d):
    p, ch = C_STATE, C_GROUP_CH
    layers = a_re.shape[0]
    g = layers * C_GROUPS
    npair = g // 2
    flat = lambda a: a.reshape(g, *a.shape[2:])
    a_re, a_im, log_dt, b_re, b_im, c_re, c_im = map(flat, (a_re, a_im, log_dt, b_re, b_im, c_re, c_im))
    row = lambda a: a.reshape(g, 1, p)
    ldt = jnp.broadcast_to(log_dt[:, None, None], (g, 1, p))
    b_t = lambda a: jnp.swapaxes(a, 1, 2)
    pps = S5PREP_PAIRS
    spec = lambda s1, s2: pl.BlockSpec((2 * pps, s1, s2), lambda i: (i, 0, 0))
    mt, wet, wyt, pw_re, pw_im = pl.pallas_call(
        _s5prep_kernel,
        out_shape=[jax.ShapeDtypeStruct((npair, 2, S5_TC, S5_TC), BF16),
                   jax.ShapeDtypeStruct((npair, 4 * p, 2 * S5_TC), BF16),
                   jax.ShapeDtypeStruct((npair, 2 * S5_TC, 4 * p), BF16),
                   jax.ShapeDtypeStruct((npair, 8, 2 * p), F32),
                   jax.ShapeDtypeStruct((npair, 8, 2 * p), F32)],
        grid=(npair // pps,),
        in_specs=[spec(1, p), spec(1, p), spec(1, p), spec(p, ch), spec(p, ch),
                  spec(ch, p), spec(ch, p), spec(ch, p), spec(ch, p)],
        out_specs=[pl.BlockSpec((pps, 2, S5_TC, S5_TC), lambda i: (i, 0, 0, 0)),
                   pl.BlockSpec((pps, 4 * p, 2 * S5_TC), lambda i: (i, 0, 0)),
                   pl.BlockSpec((pps, 2 * S5_TC, 4 * p), lambda i: (i, 0, 0)),
                   pl.BlockSpec((pps, 8, 2 * p), lambda i: (i, 0, 0)),
                   pl.BlockSpec((pps, 8, 2 * p), lambda i: (i, 0, 0))],
        compiler_params=_cparams(("arbitrary",)),
        name="s5prep",
    )(row(a_re), row(a_im), ldt, b_re, b_im, b_t(b_re), b_t(b_im), c_re, c_im)
    by_gb = lambda a: a.reshape(layers * S5_NGB, S5_PAIRS_PER_GB, *a.shape[1:])
    return (mt.reshape(layers * S5_NGB, S5_GB, S5_TC, S5_TC), by_gb(wet), by_gb(wyt), by_gb(pw_re), by_gb(pw_im),
            d.reshape(layers, 1, C_WIDTH))


def _s5_kernel(u_ref, mt_ref, wet_ref, wyt_ref, are_ref, aim_ref, d_ref, y_ref,
               ut_ref, yt_ref, ere_ref, eim_ref, hre_ref, him_ref):
    nck, t_len, ch = S5_NCHUNK, S5_CHUNK, C_GROUP_CH
    nt = (((1,), (1,)), ((), ()))
    for t in range(t_len):
        xt = u_ref[pl.ds(t, nck, stride=t_len), :].T
        for g in range(S5_GB):
            ut_ref[g, ch * t:ch * (t + 1), :] = xt[ch * g:ch * (g + 1), :]
    for j in range(S5_PAIRS_PER_GB):
        u0 = ut_ref[2 * j].astype(BF16)
        u1 = ut_ref[2 * j + 1].astype(BF16)
        et = jnp.dot(wet_ref[0, j], jnp.concatenate([u0, u1], axis=0), preferred_element_type=F32)
        e = et.T
        ere_ref[:, LANES * j:LANES * (j + 1)] = e[:, :LANES]
        eim_ref[:, LANES * j:LANES * (j + 1)] = e[:, LANES:]
        yt_ref[2 * j] = jnp.dot(mt_ref[0, 2 * j], u0, preferred_element_type=F32)
        yt_ref[2 * j + 1] = jnp.dot(mt_ref[0, 2 * j + 1], u1, preferred_element_type=F32)

    a_re = jnp.concatenate([are_ref[0, j, 0:1, :] for j in range(S5_PAIRS_PER_GB)], axis=1)
    a_im = jnp.concatenate([aim_ref[0, j, 0:1, :] for j in range(S5_PAIRS_PER_GB)], axis=1)

    def body(i, carry):
        h_re, h_im = carry
        hre_ref[pl.ds(i, 1), :] = h_re
        him_ref[pl.ds(i, 1), :] = h_im
        e_re = ere_ref[pl.ds(i, 1), :]
        e_im = eim_ref[pl.ds(i, 1), :]
        return a_re * h_re - a_im * h_im + e_re, a_re * h_im + a_im * h_re + e_im

    zero = jnp.zeros((1, S5_GB * C_STATE), F32)
    lax.fori_loop(0, nck, body, (zero, zero))

    for j in range(S5_PAIRS_PER_GB):
        sl = slice(LANES * j, LANES * (j + 1))
        hp = jnp.concatenate([hre_ref[:, sl], him_ref[:, sl]], axis=1).astype(BF16)
        yi = lax.dot_general(wyt_ref[0, j], hp, nt, preferred_element_type=F32)
        yt_ref[2 * j] += yi[:S5_TC]
        yt_ref[2 * j + 1] += yi[S5_TC:]
    for t in range(t_len):
        ytt = jnp.concatenate([yt_ref[g, ch * t:ch * (t + 1), :] for g in range(S5_GB)], axis=0)
        rows = pl.ds(t, nck, stride=t_len)
        y_ref[rows, :] = ytt.T + d_ref[...] * u_ref[rows, :]


def _s5(cu, mt, wet, wyt, pw_re, pw_im, d_row, layer):
    p4 = 4 * C_STATE
    tok = pl.BlockSpec((SEQ, LANES), lambda gb, b: (b, gb))
    per_gb = lambda *s: pl.BlockSpec((1,) + s, lambda gb, b: (layer * S5_NGB + gb,) + (0,) * len(s))
    state = pltpu.VMEM((S5_NCHUNK, S5_GB * C_STATE), F32)
    return pl.pallas_call(
        _s5_kernel,
        out_shape=jax.ShapeDtypeStruct((TOKENS, C_WIDTH), F32),
        grid=(S5_NGB, BATCH),
        in_specs=[tok,
                  per_gb(S5_GB, S5_TC, S5_TC),
                  per_gb(S5_PAIRS_PER_GB, p4, 2 * S5_TC),
                  per_gb(S5_PAIRS_PER_GB, 2 * S5_TC, p4),
                  per_gb(S5_PAIRS_PER_GB, 8, 2 * C_STATE), per_gb(S5_PAIRS_PER_GB, 8, 2 * C_STATE),
                  pl.BlockSpec((1, LANES), lambda gb, b: (0, gb))],
        out_specs=tok,
        scratch_shapes=[pltpu.VMEM((S5_GB, S5_TC, S5_NCHUNK), F32),
                        pltpu.VMEM((S5_GB, S5_TC, S5_NCHUNK), F32),
                        state, state, state, state],
        compiler_params=_cparams(("arbitrary", "arbitrary")),
        name="s5",
    )(cu, mt, wet, wyt, pw_re, pw_im, d_row)


def _gelu_tanh(x):
    return 0.5 * x * (1.0 + jnp.tanh(math.sqrt(2.0 / math.pi) * (x + 0.044715 * (x * x * x))))


def _out_kernel(sink_ref, qt_ref, kp_ref, kc_ref, vtp_ref, vtc_ref, ag_ref, bias_ref,
                ob_ref, yc_ref, cg_ref, x_ref, gate_ref, gpost_ref, wglu_ref, bglu_ref, wout_ref, o_ref, oa_ref):
    first_tile = pl.program_id(0) % (SEQ // OUT_TM) == 0
    blocks_per_part = OUT_TM // (A_BLOCK * OUT_PARTS)

    def project(part):
        r = slice(A_BLOCK * blocks_per_part * part, A_BLOCK * blocks_per_part * (part + 1))
        y = _gelu_tanh(yc_ref[r, :])
        z = jnp.dot(y.astype(BF16), wglu_ref[...], preferred_element_type=F32) + bglu_ref[...]
        y = y * jax.nn.sigmoid(z)
        oc = (y * _silu(cg_ref[r, :].astype(F32))).astype(BF16)
        mix = jnp.concatenate([oa_ref[r, :], ob_ref[r, :], oc], axis=1)
        acc = jnp.dot(mix, wout_ref[...], preferred_element_type=F32)
        ms = jnp.mean(acc * acc, axis=-1, keepdims=True)
        out = acc * lax.rsqrt(ms + EPS) * gpost_ref[...]
        o_ref[r, :] = x_ref[r, :] + gate_ref[0] * out

    def after_block(j):
        if (j + 1) % blocks_per_part == 0:
            project(j // blocks_per_part)

    _attention_steps(sink_ref, qt_ref, kp_ref, kc_ref, vtp_ref, vtc_ref, ag_ref, bias_ref, oa_ref, first_tile,
                     OUT_TM // A_BLOCK, blocks_per_part, after_block)


def _out(sinks, qt, k, vt, ag, bias, ob, yc, cg, x2, gate, g_post, w_glu, b_glu, w_out, layer):
    tm = OUT_TM
    steps_per_batch = SEQ // tm
    blocks = tm // A_BLOCK
    row = lambda i: (i, 0)
    col = lambda i: (0, i)
    const = lambda i: (0, 0)
    once = dict(pipeline_mode=pl.Buffered(1))
    prev = lambda i: (i * blocks - jnp.minimum(i % steps_per_batch, 1), 0)
    prev_t = lambda i: (0, i * blocks - jnp.minimum(i % steps_per_batch, 1))
    return pl.pallas_call(
        _out_kernel,
        out_shape=jax.ShapeDtypeStruct((TOKENS, D_MODEL), F32),
        grid=(TOKENS // tm,),
        in_specs=[pl.BlockSpec(memory_space=pltpu.SMEM),
                  pl.BlockSpec((A_WIDTH, tm), col),
                  pl.BlockSpec((A_BLOCK, A_KV_WIDTH), prev),
                  pl.BlockSpec((tm, A_KV_WIDTH), row),
                  pl.BlockSpec((A_KV_WIDTH, A_BLOCK), prev_t),
                  pl.BlockSpec((A_KV_WIDTH, tm), col),
                  pl.BlockSpec((tm, A_WIDTH), row),
                  pl.BlockSpec((2, A_Q_HEADS, A_BLOCK, A_BLOCK), lambda i: (0, 0, 0, 0), **once),
                  pl.BlockSpec((tm, B_WIDTH), row),
                  pl.BlockSpec((tm, C_WIDTH), row),
                  pl.BlockSpec((tm, C_WIDTH), row),
                  pl.BlockSpec((tm, D_MODEL), row),
                  pl.BlockSpec((1, 1, D_MODEL), lambda i: (i // steps_per_batch, 0, 0)),
                  pl.BlockSpec((1, D_MODEL), const),
                  pl.BlockSpec((C_WIDTH, C_WIDTH), lambda i: (layer, 0), **once),
                  pl.BlockSpec((1, C_WIDTH), const),
                  pl.BlockSpec((2 * D_MODEL, D_MODEL), lambda i: (layer, 0), **once)],
        out_specs=pl.BlockSpec((tm, D_MODEL), row),
        scratch_shapes=[pltpu.VMEM((tm, A_WIDTH), BF16)],
        compiler_params=_cparams(("arbitrary",)),
        name="out",
    )(sinks, qt, k, k, vt, vt, ag, bias, ob, yc, cg, x2, gate, g_post.reshape(1, D_MODEL), w_glu,
      b_glu.reshape(1, C_WIDTH), w_out)


def kernel(x, c, w_mod, b_mod, g_pre, g_post, w_in, attn_sinks, gla_w_alpha, gla_b_alpha, gla_norm_g,
           s5_a_re, s5_a_im, s5_log_dt, s5_b_re, s5_b_im, s5_c_re, s5_c_im, s5_d, s5_w_glu, s5_b_glu, w_out):
    layers = w_mod.shape[0]
    x2 = x.reshape(TOKENS, D_MODEL)
    bias = _attn_bias()
    mod = _mod(jnp.pad(c, ((0, 8 - BATCH), (0, 0))), w_mod, b_mod)[:, :BATCH]
    shift, scale, gate = (m.reshape(layers, BATCH, 1, D_MODEL) for m in jnp.split(mod, 3, axis=-1))
    w_t = jnp.swapaxes(w_in, 1, 2).astype(BF16).reshape(layers * W_IN_COLS, D_MODEL)
    w_alpha_pad = jnp.pad(gla_w_alpha, ((0, 0), (0, LR_PAD - B_GATE_RANK), (0, 0))).astype(BF16)
    s5_ops = _s5prep(s5_a_re, s5_a_im, s5_log_dt, s5_b_re, s5_b_im, s5_c_re, s5_c_im, s5_d)
    *s5_ops, s5_d_rows = s5_ops
    w_glu = s5_w_glu.astype(BF16).reshape(layers * C_WIDTH, C_WIDTH)
    w_out_b = w_out.astype(BF16).reshape(layers * 2 * D_MODEL, D_MODEL)
    for l in range(layers):
        ak, ag, bq, bk, bv, bg, cu, cg, log_a, aqt, avt = _proj(
            x2, scale[l], shift[l], g_pre[l], w_t, l, w_alpha_pad[l], gla_b_alpha[l])
        o_b = _gla(log_a, bq, bk, bv, bg, gla_norm_g[l])
        y_c = _s5(cu, *s5_ops, s5_d_rows[l], l)
        x2 = _out(attn_sinks[l], aqt, ak, avt, ag, bias, o_b, y_c, cg, x2, gate[l], g_post[l], w_glu, s5_b_glu[l],
                  w_out_b, l)
    return x2.reshape(x.shape)
```

```python
import math

import jax
import jax.numpy as jnp
import numpy as np
from jax import lax
from jax.experimental import pallas as pl
from jax.experimental.pallas import tpu as pltpu

F32 = jnp.float32
BF16 = jnp.bfloat16
HIGHEST = lax.Precision.HIGHEST

D_MODEL = 1024
BATCH = 4
SEQ = 4096
TOKENS = BATCH * SEQ
EPS = 1e-6

A_WIDTH = 1024
A_HEAD_DIM = 64
A_Q_HEADS = 16
A_KV_HEADS = 4
A_KV_WIDTH = A_KV_HEADS * A_HEAD_DIM
A_BLOCK = 128
WINDOW = 128
assert WINDOW == A_BLOCK

B_WIDTH = 512
B_HEADS = 4
B_DK = 64
B_DV = 128
B_QK_WIDTH = 256
B_GATE_RANK = 16
B_GATE_TAU = 16.0
GLA_BLOCK = 64
GLA_SUB = 8
GLA_DIAG = 16
GLA_ANCHOR_SEGS = (32, 64)
assert GLA_ANCHOR_SEGS[0] == 2 * GLA_DIAG and GLA_ANCHOR_SEGS[-1] == GLA_BLOCK
OUT_PARTS = 2

C_WIDTH = 512
C_GROUP_CH = 16
C_GROUPS = 32
C_STATE = 64
S5_CHUNK = 16
S5_NCHUNK = SEQ // S5_CHUNK
S5_TC = S5_CHUNK * C_GROUP_CH
S5_GB = 8
S5_NGB = C_GROUPS // S5_GB
S5_PAIRS_PER_GB = S5_GB // 2
S5_POW_ROWS = 24
S5PREP_PAIRS = 4

LANES = 128
LR_PAD = LANES

V7X_VMEM_LIMIT = 56 * 1024 * 1024

PROJ_TM = 1024
OUT_TM = 1024

_W_IN_SIZES = (("aq", A_WIDTH), ("ak", A_KV_WIDTH), ("av", A_KV_WIDTH), ("ag", A_WIDTH), ("bq", B_QK_WIDTH),
               ("bk", B_QK_WIDTH), ("bv", B_WIDTH), ("blr", B_GATE_RANK), ("bg", B_WIDTH), ("cu", C_WIDTH),
               ("cg", C_WIDTH))
_W_IN_OFF = {}
_off = 0
for _name, _w in _W_IN_SIZES:
    _W_IN_OFF[_name] = (_off, _w)
    _off += _w
W_IN_COLS = _off
_PROJ_OUTS = (("ak", BF16), ("ag", BF16), ("bq", BF16), ("bk", BF16),
              ("bv", BF16), ("bg", BF16), ("cu", F32), ("cg", BF16))


def _silu(x):
    return x * jax.nn.sigmoid(x)


def _cparams(semantics):
    return pltpu.CompilerParams(dimension_semantics=semantics, vmem_limit_bytes=V7X_VMEM_LIMIT)


def _mod_kernel(c_ref, w_ref, b_ref, o_ref):
    c = c_ref[...]
    o_ref[0] = jnp.dot(_silu(c).astype(BF16), w_ref[0].astype(BF16), preferred_element_type=F32) + b_ref[0]


def _mod(c_pad, w_mod, b_mod):
    layers = w_mod.shape[0]
    n = 3 * D_MODEL
    tn = 768
    return pl.pallas_call(
        _mod_kernel,
        out_shape=jax.ShapeDtypeStruct((layers, 8, n), F32),
        grid=(layers, n // tn),
        in_specs=[pl.BlockSpec((8, D_MODEL), lambda l, j: (0, 0)),
                  pl.BlockSpec((1, D_MODEL, tn), lambda l, j: (l, 0, j)),
                  pl.BlockSpec((1, 1, tn), lambda l, j: (l, 0, j))],
        out_specs=pl.BlockSpec((1, 8, tn), lambda l, j: (l, 0, j)),
        compiler_params=_cparams(("arbitrary", "arbitrary")),
        name="mod",
    )(c_pad, w_mod, b_mod.reshape(layers, 1, n))


def _proj_kernel(x_ref, scale_ref, shift_ref, gpre_ref, wt_ref, walpha_ref, balpha_ref, *out_refs):
    x = x_ref[...]
    ms = jnp.mean(x * x, axis=-1, keepdims=True)
    y = x * lax.rsqrt(ms + EPS) * gpre_ref[...]
    h = (y * (1.0 + scale_ref[0]) + shift_ref[0]).astype(BF16)
    nt = (((1,), (1,)), ((), ()))

    def rows(name, width=None):
        off, w = _W_IN_OFF[name]
        return wt_ref[off:off + (width or w), :]

    for (name, _), o_ref in zip(_PROJ_OUTS, out_refs):
        o_ref[...] = lax.dot_general(h, rows(name), nt, preferred_element_type=F32).astype(o_ref.dtype)
    lr = lax.dot_general(h, rows("blr", LR_PAD), nt, preferred_element_type=F32).astype(BF16)
    logits = jnp.dot(lr, walpha_ref[...], preferred_element_type=F32) + balpha_ref[...]
    log_sig = jnp.minimum(logits, 0.0) - jnp.log(1.0 + jnp.exp(-jnp.abs(logits)))
    la_ref, qt_ref, vt_ref = out_refs[len(_PROJ_OUTS):]
    la_ref[...] = log_sig * (math.log2(math.e) / B_GATE_TAU)
    qt = lax.dot_general(rows("aq"), h, nt, preferred_element_type=F32) * (A_HEAD_DIM ** -0.5)
    qt_ref[...] = qt.astype(qt_ref.dtype)
    vt_ref[...] = lax.dot_general(rows("av"), h, nt, preferred_element_type=F32).astype(vt_ref.dtype)


def _proj(x2, scale, shift, g_pre, w_t, layer, w_alpha_pad, b_alpha):
    tm = PROJ_TM
    steps_per_batch = SEQ // tm
    row = lambda i: (i, 0)
    col = lambda i: (0, i)
    per_batch = lambda i: (i // steps_per_batch, 0, 0)
    const = lambda i: (0, 0)
    out_shape = [jax.ShapeDtypeStruct((TOKENS, _W_IN_OFF[n][1]), dt) for n, dt in _PROJ_OUTS]
    out_specs = [pl.BlockSpec((tm, _W_IN_OFF[n][1]), row) for n, _ in _PROJ_OUTS]
    out_shape += [jax.ShapeDtypeStruct((TOKENS, B_QK_WIDTH), F32),
                  jax.ShapeDtypeStruct((A_WIDTH, TOKENS), BF16),
                  jax.ShapeDtypeStruct((A_KV_WIDTH, TOKENS), BF16)]
    out_specs += [pl.BlockSpec((tm, B_QK_WIDTH), row),
                  pl.BlockSpec((A_WIDTH, tm), col),
                  pl.BlockSpec((A_KV_WIDTH, tm), col)]
    return pl.pallas_call(
        _proj_kernel,
        out_shape=out_shape,
        grid=(TOKENS // tm,),
        in_specs=[pl.BlockSpec((tm, D_MODEL), row),
                  pl.BlockSpec((1, 1, D_MODEL), per_batch),
                  pl.BlockSpec((1, 1, D_MODEL), per_batch),
                  pl.BlockSpec((1, D_MODEL), const),
                  pl.BlockSpec((W_IN_COLS, D_MODEL), lambda i: (layer, 0), pipeline_mode=pl.Buffered(1)),
                  pl.BlockSpec((LR_PAD, B_QK_WIDTH), const),
                  pl.BlockSpec((1, B_QK_WIDTH), const)],
        out_specs=out_specs,
        compiler_params=_cparams(("arbitrary",)),
        name="proj",
    )(x2, scale, shift, g_pre.reshape(1, D_MODEL), w_t, w_alpha_pad, b_alpha.reshape(1, B_QK_WIDTH))


def _attn_bias():
    j = np.arange(A_BLOCK)[:, None]
    i = np.arange(A_BLOCK)[None, :]
    dist = np.where(j > i, i + A_BLOCK - j, i - j).astype(np.float32)
    slopes = np.exp2(-8.0 * np.arange(1, A_Q_HEADS + 1, dtype=np.float32) / A_Q_HEADS).astype(np.float32)
    bias = -slopes[:, None, None] * dist[None]
    first = np.where((j > i)[None], -np.inf, bias).astype(np.float32)
    return jnp.asarray(np.stack([bias, first]))


def _attention_steps(sink_ref, qt_ref, kp_ref, kc_ref, vtp_ref, vtc_ref, g_ref, bias_ref, oa_ref, first_tile,
                     n_blocks, group_blocks, after_block):
    kj = lax.broadcasted_iota(jnp.int32, (A_BLOCK, A_BLOCK), 0)
    qi = lax.broadcasted_iota(jnp.int32, (A_BLOCK, A_BLOCK), 1)
    from_prev = kj > qi
    zero_rows = jnp.zeros((A_HEAD_DIM, A_BLOCK), BF16)
    group = A_Q_HEADS // A_KV_HEADS
    blk = lambda j: slice(A_BLOCK * j, A_BLOCK * (j + 1))

    def keys(j, sl):
        prev = kp_ref[:, sl] if j == 0 else kc_ref[blk(j - 1), sl]
        return prev, kc_ref[blk(j), sl]

    def values(j, rows):
        prev = vtp_ref[rows, :] if j == 0 else vtc_ref[rows, blk(j - 1)]
        return jnp.concatenate([prev, vtc_ref[rows, blk(j)]], axis=1)

    def scores(j, hd):
        kvh = hd // group
        sl = slice(LANES * (kvh // 2), LANES * (kvh // 2 + 1))
        qh = qt_ref[A_HEAD_DIM * hd:A_HEAD_DIM * (hd + 1), blk(j)]
        qsel = jnp.concatenate([qh, zero_rows] if kvh % 2 == 0 else [zero_rows, qh], axis=0)
        k_prev, k_cur = keys(j, sl)
        return (jnp.dot(k_prev, qsel, preferred_element_type=F32),
                jnp.dot(k_cur, qsel, preferred_element_type=F32))

    def attend(j, hd, s_prev, s_cur):
        kvh = hd // group
        v_both = values(j, slice(A_HEAD_DIM * kvh, A_HEAD_DIM * (kvh + 1)))
        table = jnp.where(first_tile, 1, 0) if j == 0 else 0
        s = jnp.where(from_prev, s_prev, s_cur) + bias_ref[table, hd]
        sink = sink_ref[hd]
        m = jnp.maximum(jnp.max(s, axis=0, keepdims=True), sink)
        p = jnp.exp(s - m)
        den = jnp.sum(p, axis=0, keepdims=True) + jnp.exp(sink - m)
        p_both = jnp.concatenate([jnp.where(from_prev, p, 0.0), jnp.where(from_prev, 0.0, p)],
                                 axis=0).astype(BF16)
        return jnp.dot(v_both, p_both, preferred_element_type=F32) / den

    pending = {}

    def issue_scores(first):
        for j in range(first, min(first + group_blocks, n_blocks)):
            for hd in range(A_Q_HEADS):
                pending[j, hd] = scores(j, hd)

    issue_scores(0)
    for j in range(n_blocks):
        if (j + 1) % group_blocks == 0:
            issue_scores(j + 1)
        outs = {}
        for hd in range(A_Q_HEADS):
            outs[hd] = attend(j, hd, *pending.pop((j, hd)))
            if hd % 2 == 1:
                qsl = slice(LANES * (hd // 2), LANES * (hd // 2 + 1))
                o_pair = jnp.concatenate([outs.pop(hd - 1), outs.pop(hd)], axis=0).T
                gate = g_ref[blk(j), qsl].astype(F32)
                oa_ref[blk(j), qsl] = (o_pair * _silu(gate)).astype(oa_ref.dtype)
        after_block(j)


def _gla_kernel(la_ref, q_ref, k_ref, v_ref, g_ref, gn_ref, o_ref, st_ref):
    cb = GLA_BLOCK

    @pl.when(pl.program_id(0) == 0)
    def _():
        st_ref[...] = jnp.zeros_like(st_ref)

    r = lax.broadcasted_iota(jnp.int32, (cb, cb), 0)
    c = lax.broadcasted_iota(jnp.int32, (cb, cb), 1)
    tri = (c <= r).astype(BF16)
    lane = lax.broadcasted_iota(jnp.int32, (1, B_QK_WIDTH), 1)
    head_masks = [(lane >= B_DK * h) & (lane < B_DK * (h + 1)) for h in range(B_HEADS)]
    rr = lax.broadcasted_iota(jnp.int32, (B_HEADS * cb, cb), 0)
    cc = lax.broadcasted_iota(jnp.int32, (B_HEADS * cb, cb), 1)
    ri = rr & (cb - 1)
    nt = (((1,), (1,)), ((), ()))
    tn = (((0,), (0,)), ((), ()))
    rows = lambda u: slice(cb * u, cb * (u + 1))
    items = [(u, b) for u in range(GLA_SUB) for b in range(BATCH)]
    def cumsum(la):
        hi = la.astype(BF16)
        r1 = la - hi.astype(F32)
        mid = r1.astype(BF16)
        lo = (r1 - mid.astype(F32)).astype(BF16)
        parts = jnp.dot(tri, jnp.concatenate([hi, mid, lo], axis=1), preferred_element_type=F32)
        w = B_QK_WIDTH
        return parts[:, :w] + (parts[:, w:2 * w] + parts[:, 2 * w:])

    bcs = {(u, b): cumsum(la_ref[b, rows(u), :]) for u, b in items}
    stack_heads = lambda a: jnp.concatenate([jnp.where(m, a, 0.0) for m in head_masks], axis=0).astype(BF16)
    row = lax.broadcasted_iota(jnp.int32, (cb, 1), 0)
    diag = ((ri ^ cc) < GLA_DIAG) & (cc <= ri)
    same_seg = {seg: (ri ^ cc) < seg for seg in GLA_ANCHOR_SEGS}

    def upper(a, seg):
        return jnp.concatenate([a[s0 + seg // 2:s0 + seg] for s0 in range(0, cb, seg)], axis=0)

    def place_upper(p, seg):
        half = seg // 2
        zero = jnp.zeros((half, cb), F32)
        parts = []
        for piece in range(B_HEADS * cb // seg):
            parts += [zero, p[half * piece:half * (piece + 1)]]
        return jnp.concatenate(parts, axis=0)

    def anchor_rows(bc, seg, offset, reps):
        parts = []
        for s0 in range(0, cb, seg):
            a = s0 + offset - 1
            val = bc[a:a + 1, :] if a >= 0 else jnp.zeros((1, B_QK_WIDTH), F32)
            parts.append(jnp.broadcast_to(val, (reps, B_QK_WIDTH)))
        return jnp.concatenate(parts, axis=0)

    q_lv, k_lv, qsts, ksts, decs, vs = {}, {}, {}, {}, {}, {}
    for it in items:
        u, b = it
        bc = bcs[it]
        bl = bc[cb - 1:cb, :]
        q = q_ref[b, rows(u), :].astype(F32) * (B_DK ** -0.5)
        k = k_ref[b, rows(u), :].astype(F32)
        for seg in GLA_ANCHOR_SEGS:
            half = seg // 2
            q_lv[it, seg] = stack_heads(upper(q, seg) * jnp.exp2(upper(bc, seg) - anchor_rows(bc, seg, half, half)))
            in_lower = (row & (seg - 1)) < half
            k_lv[it, seg] = jnp.where(in_lower, k * jnp.exp2(anchor_rows(bc, seg, half, seg) - bc), 0.0).astype(BF16)
        anc = anchor_rows(bc, GLA_DIAG, 0, GLA_DIAG)
        q_lv[it, 0] = stack_heads(q * jnp.exp2(bc - anc))
        k_lv[it, 0] = (k * jnp.exp2(anc - bc)).astype(BF16)
        qsts[it] = stack_heads(q * jnp.exp2(bc))
        ksts[it] = stack_heads(k * jnp.exp2(bl - bc))
        decs[it] = jnp.exp2(bl)
        vs[it] = v_ref[b, rows(u), :]
    a_alls = {}
    for it in items:
        prod = lambda lv: lax.dot_general(q_lv[it, lv], k_lv[it, lv], nt, preferred_element_type=F32)
        a = jnp.where(diag, prod(0), 0.0)
        for seg in GLA_ANCHOR_SEGS:
            a = a + jnp.where(same_seg[seg], place_upper(prod(seg), seg), 0.0)
        a_alls[it] = a.astype(BF16)
    upds = {}
    for it in items:
        vst = jnp.concatenate([vs[it][:, B_DV * h:B_DV * (h + 1)] for h in range(B_HEADS)], axis=0)
        upds[it] = lax.dot_general(vst, ksts[it], tn, preferred_element_type=F32)
    st_in = {}
    for b in range(BATCH):
        st = st_ref[b]
        for u in range(GLA_SUB):
            st_in[u, b] = st
            st = st * decs[u, b] + upds[u, b]
        st_ref[b] = st
    oi_alls = {it: lax.dot_general(qsts[it], st_in[it].astype(BF16), nt, preferred_element_type=F32)
               for it in items}
    o_hs = {}
    for it in items:
        a_all = a_alls[it]
        for h in range(B_HEADS):
            o_hs[it, h] = (jnp.dot(a_all[cb * h:cb * (h + 1)], vs[it][:, B_DV * h:B_DV * (h + 1)],
                                   preferred_element_type=F32) + oi_alls[it][cb * h:cb * (h + 1)])
    for it in items:
        u, b = it
        for h in range(B_HEADS):
            vsl = slice(B_DV * h, B_DV * (h + 1))
            o_h = o_hs[it, h]
            ms = jnp.mean(o_h * o_h, axis=-1, keepdims=True)
            o_n = o_h * lax.rsqrt(ms + EPS) * gn_ref[:, vsl]
            gate = g_ref[b, rows(u), vsl].astype(F32)
            o_ref[b, rows(u), vsl] = (o_n * _silu(gate)).astype(o_ref.dtype)


def _gla(log_a, bq, bk, bv, bg, g_gla):
    cb = GLA_BLOCK * GLA_SUB
    blk = lambda w: pl.BlockSpec((BATCH, cb, w), lambda i: (0, i, 0))
    r3 = lambda a: a.reshape(BATCH, SEQ, a.shape[-1])
    out = pl.pallas_call(
        _gla_kernel,
        out_shape=jax.ShapeDtypeStruct((BATCH, SEQ, B_WIDTH), BF16),
        grid=(SEQ // cb,),
        in_specs=[blk(B_QK_WIDTH), blk(B_QK_WIDTH), blk(B_QK_WIDTH), blk(B_WIDTH), blk(B_WIDTH),
                  pl.BlockSpec((1, B_WIDTH), lambda i: (0, 0))],
        out_specs=blk(B_WIDTH),
        scratch_shapes=[pltpu.VMEM((BATCH, B_DV, B_QK_WIDTH), F32)],
        compiler_params=_cparams(("arbitrary",)),
        name="gla",
    )(r3(log_a), r3(bq), r3(bk), r3(bv), r3(bg), g_gla.reshape(1, B_WIDTH))
    return out.reshape(TOKENS, B_WIDTH)


def _s5prep_kernel(ar_ref, ai_ref, ldt_ref, bre_ref, bim_ref, btre_ref, btim_ref, cre_ref, cim_ref,
                   mt_ref, wet_ref, wyt_ref, are_ref, aim_ref):
    p = C_STATE
    for pr in range(S5PREP_PAIRS):
        wet_re, wet_im, wyt_rows, a_re, a_im = [], [], [], [], []
        for g in range(2):
            i = 2 * pr + g
            kk, e_re, e_im, y_re, y_im, p_re, p_im = _s5_discretise(
                ar_ref[i], ai_ref[i], ldt_ref[i], bre_ref[i], bim_ref[i], btre_ref[i], btim_ref[i],
                cre_ref[i], cim_ref[i])
            pieces = [kk] + [jnp.concatenate([jnp.zeros((C_GROUP_CH * s, C_GROUP_CH), F32),
                                              kk[:S5_TC - C_GROUP_CH * s]], axis=0) for s in range(1, S5_CHUNK)]
            mt_ref[pr, g] = jnp.concatenate(pieces, axis=1).astype(BF16)
            e_t = jnp.concatenate([e_re, e_im], axis=1).T
            zero = jnp.zeros((p, S5_TC), F32)
            wet_re.append(jnp.concatenate([e_t[:p], zero] if g == 0 else [zero, e_t[:p]], axis=1))
            wet_im.append(jnp.concatenate([e_t[p:], zero] if g == 0 else [zero, e_t[p:]], axis=1))
            zero = jnp.zeros((S5_TC, p), F32)
            wyt_rows.append(
                jnp.concatenate([y_re, zero, -y_im, zero] if g == 0 else [zero, y_re, zero, -y_im], axis=1))
            a_re.append(p_re)
            a_im.append(p_im)
        wet_ref[pr] = jnp.concatenate(wet_re + wet_im, axis=0).astype(BF16)
        wyt_ref[pr] = jnp.concatenate(wyt_rows, axis=0).astype(BF16)
        are_ref[pr] = jnp.broadcast_to(jnp.concatenate(a_re, axis=1), (8, 2 * p))
        aim_ref[pr] = jnp.broadcast_to(jnp.concatenate(a_im, axis=1), (8, 2 * p))


def _s5_discretise(ar, ai, ldt, b_re, b_im, bt_re16, bt_im16, c_re16, c_im16):
    dt = jnp.exp(ldt)

    def cmul(xr, xi, yr, yi):
        return xr * yr - xi * yi, xr * yi + xi * yr

    kf = lax.broadcasted_iota(jnp.int32, (S5_POW_ROWS, 1), 0).astype(F32)
    mag = jnp.exp(kf * (ar * dt))
    ang = kf * (ai * dt)
    pw_re, pw_im = mag * jnp.cos(ang), mag * jnp.sin(ang)
    abar_re, abar_im = pw_re[1:2], pw_im[1:2]
    den = ar * ar + ai * ai
    num_re = abar_re - 1.0
    f_re = (num_re * ar + abar_im * ai) / den
    f_im = (abar_im * ar - num_re * ai) / den
    g_re, g_im = cmul(pw_re, pw_im, f_re, f_im)

    def pick(which, xr, xi):
        rep = lambda x: jnp.concatenate(
            [jnp.broadcast_to(x[which(i):which(i) + 1], (C_GROUP_CH, C_STATE)) for i in range(S5_CHUNK)], axis=0)
        return rep(xr), rep(xi)

    tile16 = lambda a: jnp.concatenate([a] * S5_CHUNK, axis=0)
    ct_re, ct_im = tile16(c_re16), tile16(c_im16)
    bt_re, bt_im = tile16(bt_re16), tile16(bt_im16)

    w_re, w_im = cmul(*pick(lambda i: i, g_re, g_im), ct_re, ct_im)
    kk = (jnp.dot(w_re, b_re, preferred_element_type=F32, precision=HIGHEST)
          - jnp.dot(w_im, b_im, preferred_element_type=F32, precision=HIGHEST))
    e_re, e_im = cmul(*pick(lambda i: S5_CHUNK - 1 - i, g_re, g_im), bt_re, bt_im)
    y_re, y_im = cmul(*pick(lambda i: i + 1, pw_re, pw_im), ct_re, ct_im)
    return kk, e_re, e_im, y_re, y_im, pw_re[S5_CHUNK:S5_CHUNK + 1], pw_im[S5_CHUNK:S5_CHUNK + 1]


def _s5prep(a_re, a_im, log_dt, b_re, b_im, c_re, c_im, d):
    p, ch = C_STATE, C_GROUP_CH
    layers = a_re.shape[0]
    g = layers * C_GROUPS
    npair = g // 2
    flat = lambda a: a.reshape(g, *a.shape[2:])
    a_re, a_im, log_dt, b_re, b_im, c_re, c_im = map(flat, (a_re, a_im, log_dt, b_re, b_im, c_re, c_im))
    row = lambda a: a.reshape(g, 1, p)
    ldt = jnp.broadcast_to(log_dt[:, None, None], (g, 1, p))
    b_t = lambda a: jnp.swapaxes(a, 1, 2)
    pps = S5PREP_PAIRS
    spec = lambda s1, s2: pl.BlockSpec((2 * pps, s1, s2), lambda i: (i, 0, 0))
    mt, wet, wyt, pw_re, pw_im = pl.pallas_call(
        _s5prep_kernel,
        out_shape=[jax.ShapeDtypeStruct((npair, 2, S5_TC, S5_TC), BF16),
                   jax.ShapeDtypeStruct((npair, 4 * p, 2 * S5_TC), BF16),
                   jax.ShapeDtypeStruct((npair, 2 * S5_TC, 4 * p), BF16),
                   jax.ShapeDtypeStruct((npair, 8, 2 * p), F32),
                   jax.ShapeDtypeStruct((npair, 8, 2 * p), F32)],
        grid=(npair // pps,),
        in_specs=[spec(1, p), spec(1, p), spec(1, p), spec(p, ch), spec(p, ch),
                  spec(ch, p), spec(ch, p), spec(ch, p), spec(ch, p)],
        out_specs=[pl.BlockSpec((pps, 2, S5_TC, S5_TC), lambda i: (i, 0, 0, 0)),
                   pl.BlockSpec((pps, 4 * p, 2 * S5_TC), lambda i: (i, 0, 0)),
                   pl.BlockSpec((pps, 2 * S5_TC, 4 * p), lambda i: (i, 0, 0)),
                   pl.BlockSpec((pps, 8, 2 * p), lambda i: (i, 0, 0)),
                   pl.BlockSpec((pps, 8, 2 * p), lambda i: (i, 0, 0))],
        compiler_params=_cparams(("arbitrary",)),
        name="s5prep",
    )(row(a_re), row(a_im), ldt, b_re, b_im, b_t(b_re), b_t(b_im), c_re, c_im)
    by_gb = lambda a: a.reshape(layers * S5_NGB, S5_PAIRS_PER_GB, *a.shape[1:])
    return (mt.reshape(layers * S5_NGB, S5_GB, S5_TC, S5_TC), by_gb(wet), by_gb(wyt), by_gb(pw_re), by_gb(pw_im),
            d.reshape(layers, 1, C_WIDTH))


def _s5_kernel(u_hbm, mt_ref, wet_ref, wyt_ref, are_ref, aim_ref, d_ref, y_hbm,
               ubuf, ybuf, in_sem, out_sem, ut_ref, yt_ref, ere_ref, eim_ref, hre_ref, him_ref):
    nck, t_len, ch = S5_NCHUNK, S5_CHUNK, C_GROUP_CH
    nt = (((1,), (1,)), ((), ()))
    n_steps = S5_NGB * BATCH
    step = pl.program_id(0) * BATCH + pl.program_id(1)
    slot = step % 2

    def hbm_tile(ref, step_, t):
        gb_, b_ = step_ // BATCH, step_ % BATCH
        return ref.at[pl.ds(b_ * nck, nck), t, pl.ds(gb_ * LANES, LANES)]

    def in_copy(step_, slot_, t):
        return pltpu.make_async_copy(hbm_tile(u_hbm, step_, t), ubuf.at[slot_, t], in_sem.at[slot_, t])

    def out_copy(step_, slot_, t):
        return pltpu.make_async_copy(ybuf.at[slot_, t], hbm_tile(y_hbm, step_, t), out_sem.at[slot_, t])

    @pl.when(step == 0)
    def _():
        for t in range(t_len):
            in_copy(step, slot, t).start()

    @pl.when(step + 1 < n_steps)
    def _():
        for t in range(t_len):
            in_copy(step + 1, 1 - slot, t).start()

    @pl.when(step >= 2)
    def _():
        for t in range(t_len):
            out_copy(step - 2, slot, t).wait()

    for t in range(t_len):
        in_copy(step, slot, t).wait()
    for t in range(t_len):
        xt = ubuf[slot, t].T
        for g in range(S5_GB):
            ut_ref[g, ch * t:ch * (t + 1), :] = xt[ch * g:ch * (g + 1), :]
    for j in range(S5_PAIRS_PER_GB):
        u0 = ut_ref[2 * j].astype(BF16)
        u1 = ut_ref[2 * j + 1].astype(BF16)
        et = jnp.dot(wet_ref[0, j], jnp.concatenate([u0, u1], axis=0), preferred_element_type=F32)
        e = et.T
        ere_ref[:, LANES * j:LANES * (j + 1)] = e[:, :LANES]
        eim_ref[:, LANES * j:LANES * (j + 1)] = e[:, LANES:]
        yt_ref[2 * j] = jnp.dot(mt_ref[0, 2 * j], u0, preferred_element_type=F32)
        yt_ref[2 * j + 1] = jnp.dot(mt_ref[0, 2 * j + 1], u1, preferred_element_type=F32)

    a_re = jnp.concatenate([are_ref[0, j, 0:1, :] for j in range(S5_PAIRS_PER_GB)], axis=1)
    a_im = jnp.concatenate([aim_ref[0, j, 0:1, :] for j in range(S5_PAIRS_PER_GB)], axis=1)

    def body(i, carry):
        h_re, h_im = carry
        hre_ref[pl.ds(i, 1), :] = h_re
        him_ref[pl.ds(i, 1), :] = h_im
        e_re = ere_ref[pl.ds(i, 1), :]
        e_im = eim_ref[pl.ds(i, 1), :]
        return a_re * h_re - a_im * h_im + e_re, a_re * h_im + a_im * h_re + e_im

    zero = jnp.zeros((1, S5_GB * C_STATE), F32)
    lax.fori_loop(0, nck, body, (zero, zero))

    for j in range(S5_PAIRS_PER_GB):
        sl = slice(LANES * j, LANES * (j + 1))
        hp = jnp.concatenate([hre_ref[:, sl], him_ref[:, sl]], axis=1).astype(BF16)
        yi = lax.dot_general(wyt_ref[0, j], hp, nt, preferred_element_type=F32)
        yt_ref[2 * j] += yi[:S5_TC]
        yt_ref[2 * j + 1] += yi[S5_TC:]
    for t in range(t_len):
        ytt = jnp.concatenate([yt_ref[g, ch * t:ch * (t + 1), :] for g in range(S5_GB)], axis=0)
        ybuf[slot, t] = ytt.T + d_ref[...] * ubuf[slot, t]
    for t in range(t_len):
        out_copy(step, slot, t).start()

    @pl.when(step == n_steps - 1)
    def _():
        for t in range(t_len):
            out_copy(step - 1, 1 - slot, t).wait()
            out_copy(step, slot, t).wait()


def _s5(cu, mt, wet, wyt, pw_re, pw_im, d_row, layer):
    p4 = 4 * C_STATE
    per_gb = lambda *s: pl.BlockSpec((1,) + s, lambda gb, b: (layer * S5_NGB + gb,) + (0,) * len(s))
    state = pltpu.VMEM((S5_NCHUNK, S5_GB * C_STATE), F32)
    tiles = pltpu.VMEM((2, S5_CHUNK, S5_NCHUNK, LANES), F32)
    sems = pltpu.SemaphoreType.DMA((2, S5_CHUNK))
    by_chunk = (TOKENS // S5_CHUNK, S5_CHUNK, C_WIDTH)
    y = pl.pallas_call(
        _s5_kernel,
        out_shape=jax.ShapeDtypeStruct(by_chunk, F32),
        grid=(S5_NGB, BATCH),
        in_specs=[pl.BlockSpec(memory_space=pl.ANY),
                  per_gb(S5_GB, S5_TC, S5_TC),
                  per_gb(S5_PAIRS_PER_GB, p4, 2 * S5_TC),
                  per_gb(S5_PAIRS_PER_GB, 2 * S5_TC, p4),
                  per_gb(S5_PAIRS_PER_GB, 8, 2 * C_STATE), per_gb(S5_PAIRS_PER_GB, 8, 2 * C_STATE),
                  pl.BlockSpec((1, LANES), lambda gb, b: (0, gb))],
        out_specs=pl.BlockSpec(memory_space=pl.ANY),
        scratch_shapes=[tiles, tiles, sems, sems,
                        pltpu.VMEM((S5_GB, S5_TC, S5_NCHUNK), F32),
                        pltpu.VMEM((S5_GB, S5_TC, S5_NCHUNK), F32),
                        state, state, state, state],
        compiler_params=_cparams(("arbitrary", "arbitrary")),
        name="s5",
    )(cu.reshape(by_chunk), mt, wet, wyt, pw_re, pw_im, d_row)
    return y.reshape(TOKENS, C_WIDTH)


def _gelu_tanh(x):
    return 0.5 * x * (1.0 + jnp.tanh(math.sqrt(2.0 / math.pi) * (x + 0.044715 * (x * x * x))))


def _out_kernel(sink_ref, qt_ref, kp_ref, kc_ref, vtp_ref, vtc_ref, ag_ref, bias_ref,
                ob_ref, yc_ref, cg_ref, x_ref, gate_ref, gpost_ref, wglu_ref, bglu_ref, wout_ref, o_ref, oa_ref):
    first_tile = pl.program_id(0) % (SEQ // OUT_TM) == 0
    blocks_per_part = OUT_TM // (A_BLOCK * OUT_PARTS)

    def project(part):
        r = slice(A_BLOCK * blocks_per_part * part, A_BLOCK * blocks_per_part * (part + 1))
        y = _gelu_tanh(yc_ref[r, :])
        z = jnp.dot(y.astype(BF16), wglu_ref[...], preferred_element_type=F32) + bglu_ref[...]
        y = y * jax.nn.sigmoid(z)
        oc = (y * _silu(cg_ref[r, :].astype(F32))).astype(BF16)
        mix = jnp.concatenate([oa_ref[r, :], ob_ref[r, :], oc], axis=1)
        acc = jnp.dot(mix, wout_ref[...], preferred_element_type=F32)
        ms = jnp.mean(acc * acc, axis=-1, keepdims=True)
        out = acc * lax.rsqrt(ms + EPS) * gpost_ref[...]
        o_ref[r, :] = x_ref[r, :] + gate_ref[0] * out

    def after_block(j):
        if (j + 1) % blocks_per_part == 0:
            project(j // blocks_per_part)

    _attention_steps(sink_ref, qt_ref, kp_ref, kc_ref, vtp_ref, vtc_ref, ag_ref, bias_ref, oa_ref, first_tile,
                     OUT_TM // A_BLOCK, blocks_per_part, after_block)


def _out(sinks, qt, k, vt, ag, bias, ob, yc, cg, x2, gate, g_post, w_glu, b_glu, w_out, layer):
    tm = OUT_TM
    steps_per_batch = SEQ // tm
    blocks = tm // A_BLOCK
    row = lambda i: (i, 0)
    col = lambda i: (0, i)
    const = lambda i: (0, 0)
    once = dict(pipeline_mode=pl.Buffered(1))
    prev = lambda i: (i * blocks - jnp.minimum(i % steps_per_batch, 1), 0)
    prev_t = lambda i: (0, i * blocks - jnp.minimum(i % steps_per_batch, 1))
    return pl.pallas_call(
        _out_kernel,
        out_shape=jax.ShapeDtypeStruct((TOKENS, D_MODEL), F32),
        grid=(TOKENS // tm,),
        in_specs=[pl.BlockSpec(memory_space=pltpu.SMEM),
                  pl.BlockSpec((A_WIDTH, tm), col),
                  pl.BlockSpec((A_BLOCK, A_KV_WIDTH), prev),
                  pl.BlockSpec((tm, A_KV_WIDTH), row),
                  pl.BlockSpec((A_KV_WIDTH, A_BLOCK), prev_t),
                  pl.BlockSpec((A_KV_WIDTH, tm), col),
                  pl.BlockSpec((tm, A_WIDTH), row),
                  pl.BlockSpec((2, A_Q_HEADS, A_BLOCK, A_BLOCK), lambda i: (0, 0, 0, 0), **once),
                  pl.BlockSpec((tm, B_WIDTH), row),
                  pl.BlockSpec((tm, C_WIDTH), row),
                  pl.BlockSpec((tm, C_WIDTH), row),
                  pl.BlockSpec((tm, D_MODEL), row),
                  pl.BlockSpec((1, 1, D_MODEL), lambda i: (i // steps_per_batch, 0, 0)),
                  pl.BlockSpec((1, D_MODEL), const),
                  pl.BlockSpec((C_WIDTH, C_WIDTH), lambda i: (layer, 0), **once),
                  pl.BlockSpec((1, C_WIDTH), const),
                  pl.BlockSpec((2 * D_MODEL, D_MODEL), lambda i: (layer, 0), **once)],
        out_specs=pl.BlockSpec((tm, D_MODEL), row),
        scratch_shapes=[pltpu.VMEM((tm, A_WIDTH), BF16)],
        compiler_params=_cparams(("arbitrary",)),
        name="out",
    )(sinks, qt, k, k, vt, vt, ag, bias, ob, yc, cg, x2, gate, g_post.reshape(1, D_MODEL), w_glu,
      b_glu.reshape(1, C_WIDTH), w_out)


def kernel(x, c, w_mod, b_mod, g_pre, g_post, w_in, attn_sinks, gla_w_alpha, gla_b_alpha, gla_norm_g,
           s5_a_re, s5_a_im, s5_log_dt, s5_b_re, s5_b_im, s5_c_re, s5_c_im, s5_d, s5_w_glu, s5_b_glu, w_out):
    layers = w_mod.shape[0]
    x2 = x.reshape(TOKENS, D_MODEL)
    bias = _attn_bias()
    mod = _mod(jnp.pad(c, ((0, 8 - BATCH), (0, 0))), w_mod, b_mod)[:, :BATCH]
    shift, scale, gate = (m.reshape(layers, BATCH, 1, D_MODEL) for m in jnp.split(mod, 3, axis=-1))
    w_t = jnp.swapaxes(w_in, 1, 2).astype(BF16).reshape(layers * W_IN_COLS, D_MODEL)
    w_alpha_pad = jnp.pad(gla_w_alpha, ((0, 0), (0, LR_PAD - B_GATE_RANK), (0, 0))).astype(BF16)
    s5_ops = _s5prep(s5_a_re, s5_a_im, s5_log_dt, s5_b_re, s5_b_im, s5_c_re, s5_c_im, s5_d)
    *s5_ops, s5_d_rows = s5_ops
    w_glu = s5_w_glu.astype(BF16).reshape(layers * C_WIDTH, C_WIDTH)
    w_out_b = w_out.astype(BF16).reshape(layers * 2 * D_MODEL, D_MODEL)
    for l in range(layers):
        ak, ag, bq, bk, bv, bg, cu, cg, log_a, aqt, avt = _proj(
            x2, scale[l], shift[l], g_pre[l], w_t, l, w_alpha_pad[l], gla_b_alpha[l])
        o_b = _gla(log_a, bq, bk, bv, bg, gla_norm_g[l])
        y_c = _s5(cu, *s5_ops, s5_d_rows[l], l)
        x2 = _out(attn_sinks[l], aqt, ak, avt, ag, bias, o_b, y_c, cg, x2, gate[l], g_post[l], w_glu, s5_b_glu[l],
                  w_out_b, l)
    return x2.reshape(x.shape)
```

```python
import math

import jax
import jax.numpy as jnp
import numpy as np
from jax import lax
from jax.experimental import pallas as pl
from jax.experimental.pallas import tpu as pltpu

F32 = jnp.float32
BF16 = jnp.bfloat16
HIGHEST = lax.Precision.HIGHEST

D_MODEL = 1024
BATCH = 4
SEQ = 4096
TOKENS = BATCH * SEQ
EPS = 1e-6

A_WIDTH = 1024
A_HEAD_DIM = 64
A_Q_HEADS = 16
A_KV_HEADS = 4
A_KV_WIDTH = A_KV_HEADS * A_HEAD_DIM
A_BLOCK = 128
WINDOW = 128
assert WINDOW == A_BLOCK

B_WIDTH = 512
B_HEADS = 4
B_DK = 64
B_DV = 128
B_QK_WIDTH = 256
B_GATE_RANK = 16
B_GATE_TAU = 16.0
GLA_BLOCK = 64
GLA_SUB = 8
GLA_DIAG = 16
GLA_ANCHOR_SEGS = (32, 64)
assert GLA_ANCHOR_SEGS[0] == 2 * GLA_DIAG and GLA_ANCHOR_SEGS[-1] == GLA_BLOCK
OUT_PARTS = 2

C_WIDTH = 512
C_GROUP_CH = 16
C_GROUPS = 32
C_STATE = 64
S5_CHUNK = 16
S5_NCHUNK = SEQ // S5_CHUNK
S5_TC = S5_CHUNK * C_GROUP_CH
S5_GB = 8
S5_NGB = C_GROUPS // S5_GB
S5_PAIRS_PER_GB = S5_GB // 2
S5_POW_ROWS = 24
S5PREP_PAIRS = 4
SUBLANES = 8
S5_SEGS = SUBLANES
S5_SEG_CHUNKS = S5_NCHUNK // S5_SEGS
S5_STEP_ROWS = S5_SEG_CHUNKS + SUBLANES

LANES = 128
LR_PAD = LANES

V7X_VMEM_LIMIT = 56 * 1024 * 1024

PROJ_TM = 1024
OUT_TM = 1024

_W_IN_SIZES = (("aq", A_WIDTH), ("ak", A_KV_WIDTH), ("av", A_KV_WIDTH), ("ag", A_WIDTH), ("bq", B_QK_WIDTH),
               ("bk", B_QK_WIDTH), ("bv", B_WIDTH), ("blr", B_GATE_RANK), ("bg", B_WIDTH), ("cu", C_WIDTH),
               ("cg", C_WIDTH))
_W_IN_OFF = {}
_off = 0
for _name, _w in _W_IN_SIZES:
    _W_IN_OFF[_name] = (_off, _w)
    _off += _w
W_IN_COLS = _off
_PROJ_OUTS = (("ak", BF16), ("ag", BF16), ("bq", BF16), ("bk", BF16),
              ("bv", BF16), ("bg", BF16), ("cu", F32), ("cg", BF16))


def _silu(x):
    return x * jax.nn.sigmoid(x)


def _cparams(semantics):
    return pltpu.CompilerParams(dimension_semantics=semantics, vmem_limit_bytes=V7X_VMEM_LIMIT)


def _mod_kernel(c_ref, w_ref, b_ref, o_ref):
    c = c_ref[...]
    o_ref[0] = jnp.dot(_silu(c).astype(BF16), w_ref[0].astype(BF16), preferred_element_type=F32) + b_ref[0]


def _mod(c_pad, w_mod, b_mod):
    layers = w_mod.shape[0]
    n = 3 * D_MODEL
    tn = 768
    return pl.pallas_call(
        _mod_kernel,
        out_shape=jax.ShapeDtypeStruct((layers, 8, n), F32),
        grid=(layers, n // tn),
        in_specs=[pl.BlockSpec((8, D_MODEL), lambda l, j: (0, 0)),
                  pl.BlockSpec((1, D_MODEL, tn), lambda l, j: (l, 0, j)),
                  pl.BlockSpec((1, 1, tn), lambda l, j: (l, 0, j))],
        out_specs=pl.BlockSpec((1, 8, tn), lambda l, j: (l, 0, j)),
        compiler_params=_cparams(("arbitrary", "arbitrary")),
        name="mod",
    )(c_pad, w_mod, b_mod.reshape(layers, 1, n))


def _proj_kernel(x_ref, scale_ref, shift_ref, gpre_ref, wt_ref, walpha_ref, balpha_ref, *out_refs):
    x = x_ref[...]
    ms = jnp.mean(x * x, axis=-1, keepdims=True)
    y = x * lax.rsqrt(ms + EPS) * gpre_ref[...]
    h = (y * (1.0 + scale_ref[0]) + shift_ref[0]).astype(BF16)
    nt = (((1,), (1,)), ((), ()))

    def rows(name, width=None):
        off, w = _W_IN_OFF[name]
        return wt_ref[off:off + (width or w), :]

    for (name, _), o_ref in zip(_PROJ_OUTS, out_refs):
        o_ref[...] = lax.dot_general(h, rows(name), nt, preferred_element_type=F32).astype(o_ref.dtype)
    lr = lax.dot_general(h, rows("blr", LR_PAD), nt, preferred_element_type=F32).astype(BF16)
    logits = jnp.dot(lr, walpha_ref[...], preferred_element_type=F32) + balpha_ref[...]
    log_sig = jnp.minimum(logits, 0.0) - jnp.log(1.0 + jnp.exp(-jnp.abs(logits)))
    la_ref, qt_ref, vt_ref = out_refs[len(_PROJ_OUTS):]
    la_ref[...] = log_sig * (math.log2(math.e) / B_GATE_TAU)
    qt = lax.dot_general(rows("aq"), h, nt, preferred_element_type=F32) * (A_HEAD_DIM ** -0.5)
    qt_ref[...] = qt.astype(qt_ref.dtype)
    vt_ref[...] = lax.dot_general(rows("av"), h, nt, preferred_element_type=F32).astype(vt_ref.dtype)


def _proj(x2, scale, shift, g_pre, w_t, layer, w_alpha_pad, b_alpha):
    tm = PROJ_TM
    steps_per_batch = SEQ // tm
    row = lambda i: (i, 0)
    col = lambda i: (0, i)
    per_batch = lambda i: (i // steps_per_batch, 0, 0)
    const = lambda i: (0, 0)
    out_shape = [jax.ShapeDtypeStruct((TOKENS, _W_IN_OFF[n][1]), dt) for n, dt in _PROJ_OUTS]
    out_specs = [pl.BlockSpec((tm, _W_IN_OFF[n][1]), row) for n, _ in _PROJ_OUTS]
    out_shape += [jax.ShapeDtypeStruct((TOKENS, B_QK_WIDTH), F32),
                  jax.ShapeDtypeStruct((A_WIDTH, TOKENS), BF16),
                  jax.ShapeDtypeStruct((A_KV_WIDTH, TOKENS), BF16)]
    out_specs += [pl.BlockSpec((tm, B_QK_WIDTH), row),
                  pl.BlockSpec((A_WIDTH, tm), col),
                  pl.BlockSpec((A_KV_WIDTH, tm), col)]
    return pl.pallas_call(
        _proj_kernel,
        out_shape=out_shape,
        grid=(TOKENS // tm,),
        in_specs=[pl.BlockSpec((tm, D_MODEL), row),
                  pl.BlockSpec((1, 1, D_MODEL), per_batch),
                  pl.BlockSpec((1, 1, D_MODEL), per_batch),
                  pl.BlockSpec((1, D_MODEL), const),
                  pl.BlockSpec((W_IN_COLS, D_MODEL), lambda i: (layer, 0), pipeline_mode=pl.Buffered(1)),
                  pl.BlockSpec((LR_PAD, B_QK_WIDTH), const),
                  pl.BlockSpec((1, B_QK_WIDTH), const)],
        out_specs=out_specs,
        compiler_params=_cparams(("arbitrary",)),
        name="proj",
    )(x2, scale, shift, g_pre.reshape(1, D_MODEL), w_t, w_alpha_pad, b_alpha.reshape(1, B_QK_WIDTH))


def _attn_bias():
    j = np.arange(A_BLOCK)[:, None]
    i = np.arange(A_BLOCK)[None, :]
    dist = np.where(j > i, i + A_BLOCK - j, i - j).astype(np.float32)
    slopes = np.exp2(-8.0 * np.arange(1, A_Q_HEADS + 1, dtype=np.float32) / A_Q_HEADS).astype(np.float32)
    bias = -slopes[:, None, None] * dist[None]
    first = np.where((j > i)[None], -np.inf, bias).astype(np.float32)
    return jnp.asarray(np.stack([bias, first]))


def _attention_steps(sink_ref, qt_ref, kp_ref, kc_ref, vtp_ref, vtc_ref, g_ref, bias_ref, oa_ref, first_tile,
                     n_blocks, group_blocks, after_block):
    kj = lax.broadcasted_iota(jnp.int32, (A_BLOCK, A_BLOCK), 0)
    qi = lax.broadcasted_iota(jnp.int32, (A_BLOCK, A_BLOCK), 1)
    from_prev = kj > qi
    zero_rows = jnp.zeros((A_HEAD_DIM, A_BLOCK), BF16)
    group = A_Q_HEADS // A_KV_HEADS
    blk = lambda j: slice(A_BLOCK * j, A_BLOCK * (j + 1))

    def keys(j, sl):
        prev = kp_ref[:, sl] if j == 0 else kc_ref[blk(j - 1), sl]
        return prev, kc_ref[blk(j), sl]

    def values(j, rows):
        prev = vtp_ref[rows, :] if j == 0 else vtc_ref[rows, blk(j - 1)]
        return jnp.concatenate([prev, vtc_ref[rows, blk(j)]], axis=1)

    def scores(j, hd):
        kvh = hd // group
        sl = slice(LANES * (kvh // 2), LANES * (kvh // 2 + 1))
        qh = qt_ref[A_HEAD_DIM * hd:A_HEAD_DIM * (hd + 1), blk(j)]
        qsel = jnp.concatenate([qh, zero_rows] if kvh % 2 == 0 else [zero_rows, qh], axis=0)
        k_prev, k_cur = keys(j, sl)
        return (jnp.dot(k_prev, qsel, preferred_element_type=F32),
                jnp.dot(k_cur, qsel, preferred_element_type=F32))

    def attend(j, hd, s_prev, s_cur):
        kvh = hd // group
        v_both = values(j, slice(A_HEAD_DIM * kvh, A_HEAD_DIM * (kvh + 1)))
        table = jnp.where(first_tile, 1, 0) if j == 0 else 0
        s = jnp.where(from_prev, s_prev, s_cur) + bias_ref[table, hd]
        sink = sink_ref[hd]
        m = jnp.maximum(jnp.max(s, axis=0, keepdims=True), sink)
        p = jnp.exp(s - m)
        den = jnp.sum(p, axis=0, keepdims=True) + jnp.exp(sink - m)
        p_both = jnp.concatenate([jnp.where(from_prev, p, 0.0), jnp.where(from_prev, 0.0, p)],
                                 axis=0).astype(BF16)
        return jnp.dot(v_both, p_both, preferred_element_type=F32) / den

    pending = {}

    def issue_scores(first):
        for j in range(first, min(first + group_blocks, n_blocks)):
            for hd in range(A_Q_HEADS):
                pending[j, hd] = scores(j, hd)

    issue_scores(0)
    for j in range(n_blocks):
        if (j + 1) % group_blocks == 0:
            issue_scores(j + 1)
        outs = {}
        for hd in range(A_Q_HEADS):
            outs[hd] = attend(j, hd, *pending.pop((j, hd)))
            if hd % 2 == 1:
                qsl = slice(LANES * (hd // 2), LANES * (hd // 2 + 1))
                o_pair = jnp.concatenate([outs.pop(hd - 1), outs.pop(hd)], axis=0).T
                gate = g_ref[blk(j), qsl].astype(F32)
                oa_ref[blk(j), qsl] = (o_pair * _silu(gate)).astype(oa_ref.dtype)
        after_block(j)


def _gla_kernel(la_ref, q_ref, k_ref, v_ref, g_ref, gn_ref, o_ref, st_ref):
    cb = GLA_BLOCK

    @pl.when(pl.program_id(0) == 0)
    def _():
        st_ref[...] = jnp.zeros_like(st_ref)

    r = lax.broadcasted_iota(jnp.int32, (cb, cb), 0)
    c = lax.broadcasted_iota(jnp.int32, (cb, cb), 1)
    tri = (c <= r).astype(BF16)
    lane = lax.broadcasted_iota(jnp.int32, (1, B_QK_WIDTH), 1)
    head_masks = [(lane >= B_DK * h) & (lane < B_DK * (h + 1)) for h in range(B_HEADS)]
    rr = lax.broadcasted_iota(jnp.int32, (B_HEADS * cb, cb), 0)
    cc = lax.broadcasted_iota(jnp.int32, (B_HEADS * cb, cb), 1)
    ri = rr & (cb - 1)
    nt = (((1,), (1,)), ((), ()))
    tn = (((0,), (0,)), ((), ()))
    rows = lambda u: slice(cb * u, cb * (u + 1))
    items = [(u, b) for u in range(GLA_SUB) for b in range(BATCH)]
    def cumsum(la):
        hi = la.astype(BF16)
        r1 = la - hi.astype(F32)
        mid = r1.astype(BF16)
        lo = (r1 - mid.astype(F32)).astype(BF16)
        parts = jnp.dot(tri, jnp.concatenate([hi, mid, lo], axis=1), preferred_element_type=F32)
        w = B_QK_WIDTH
        return parts[:, :w] + (parts[:, w:2 * w] + parts[:, 2 * w:])

    bcs = {(u, b): cumsum(la_ref[b, rows(u), :]) for u, b in items}
    stack_heads = lambda a: jnp.concatenate([jnp.where(m, a, 0.0) for m in head_masks], axis=0).astype(BF16)
    row = lax.broadcasted_iota(jnp.int32, (cb, 1), 0)
    diag = ((ri ^ cc) < GLA_DIAG) & (cc <= ri)
    same_seg = {seg: (ri ^ cc) < seg for seg in GLA_ANCHOR_SEGS}

    def upper(a, seg):
        return jnp.concatenate([a[s0 + seg // 2:s0 + seg] for s0 in range(0, cb, seg)], axis=0)

    def place_upper(p, seg):
        half = seg // 2
        zero = jnp.zeros((half, cb), F32)
        parts = []
        for piece in range(B_HEADS * cb // seg):
            parts += [zero, p[half * piece:half * (piece + 1)]]
        return jnp.concatenate(parts, axis=0)

    def anchor_rows(bc, seg, offset, reps):
        parts = []
        for s0 in range(0, cb, seg):
            a = s0 + offset - 1
            val = bc[a:a + 1, :] if a >= 0 else jnp.zeros((1, B_QK_WIDTH), F32)
            parts.append(jnp.broadcast_to(val, (reps, B_QK_WIDTH)))
        return jnp.concatenate(parts, axis=0)

    q_lv, k_lv, qsts, ksts, decs, vs = {}, {}, {}, {}, {}, {}
    for it in items:
        u, b = it
        bc = bcs[it]
        bl = bc[cb - 1:cb, :]
        q = q_ref[b, rows(u), :].astype(F32) * (B_DK ** -0.5)
        k = k_ref[b, rows(u), :].astype(F32)
        for seg in GLA_ANCHOR_SEGS:
            half = seg // 2
            q_lv[it, seg] = stack_heads(upper(q, seg) * jnp.exp2(upper(bc, seg) - anchor_rows(bc, seg, half, half)))
            in_lower = (row & (seg - 1)) < half
            k_lv[it, seg] = jnp.where(in_lower, k * jnp.exp2(anchor_rows(bc, seg, half, seg) - bc), 0.0).astype(BF16)
        anc = anchor_rows(bc, GLA_DIAG, 0, GLA_DIAG)
        q_lv[it, 0] = stack_heads(q * jnp.exp2(bc - anc))
        k_lv[it, 0] = (k * jnp.exp2(anc - bc)).astype(BF16)
        qsts[it] = stack_heads(q * jnp.exp2(bc))
        ksts[it] = stack_heads(k * jnp.exp2(bl - bc))
        decs[it] = jnp.exp2(bl)
        vs[it] = v_ref[b, rows(u), :]
    a_alls = {}
    for it in items:
        prod = lambda lv: lax.dot_general(q_lv[it, lv], k_lv[it, lv], nt, preferred_element_type=F32)
        a = jnp.where(diag, prod(0), 0.0)
        for seg in GLA_ANCHOR_SEGS:
            a = a + jnp.where(same_seg[seg], place_upper(prod(seg), seg), 0.0)
        a_alls[it] = a.astype(BF16)
    upds = {}
    for it in items:
        vst = jnp.concatenate([vs[it][:, B_DV * h:B_DV * (h + 1)] for h in range(B_HEADS)], axis=0)
        upds[it] = lax.dot_general(vst, ksts[it], tn, preferred_element_type=F32)
    st_in = {}
    for b in range(BATCH):
        st = st_ref[b]
        for u in range(GLA_SUB):
            st_in[u, b] = st
            st = st * decs[u, b] + upds[u, b]
        st_ref[b] = st
    oi_alls = {it: lax.dot_general(qsts[it], st_in[it].astype(BF16), nt, preferred_element_type=F32)
               for it in items}
    o_hs = {}
    for it in items:
        a_all = a_alls[it]
        for h in range(B_HEADS):
            o_hs[it, h] = (jnp.dot(a_all[cb * h:cb * (h + 1)], vs[it][:, B_DV * h:B_DV * (h + 1)],
                                   preferred_element_type=F32) + oi_alls[it][cb * h:cb * (h + 1)])
    for it in items:
        u, b = it
        for h in range(B_HEADS):
            vsl = slice(B_DV * h, B_DV * (h + 1))
            o_h = o_hs[it, h]
            ms = jnp.mean(o_h * o_h, axis=-1, keepdims=True)
            o_n = o_h * lax.rsqrt(ms + EPS) * gn_ref[:, vsl]
            gate = g_ref[b, rows(u), vsl].astype(F32)
            o_ref[b, rows(u), vsl] = (o_n * _silu(gate)).astype(o_ref.dtype)


def _gla(log_a, bq, bk, bv, bg, g_gla):
    cb = GLA_BLOCK * GLA_SUB
    blk = lambda w: pl.BlockSpec((BATCH, cb, w), lambda i: (0, i, 0))
    r3 = lambda a: a.reshape(BATCH, SEQ, a.shape[-1])
    out = pl.pallas_call(
        _gla_kernel,
        out_shape=jax.ShapeDtypeStruct((BATCH, SEQ, B_WIDTH), BF16),
        grid=(SEQ // cb,),
        in_specs=[blk(B_QK_WIDTH), blk(B_QK_WIDTH), blk(B_QK_WIDTH), blk(B_WIDTH), blk(B_WIDTH),
                  pl.BlockSpec((1, B_WIDTH), lambda i: (0, 0))],
        out_specs=blk(B_WIDTH),
        scratch_shapes=[pltpu.VMEM((BATCH, B_DV, B_QK_WIDTH), F32)],
        compiler_params=_cparams(("arbitrary",)),
        name="gla",
    )(r3(log_a), r3(bq), r3(bk), r3(bv), r3(bg), g_gla.reshape(1, B_WIDTH))
    return out.reshape(TOKENS, B_WIDTH)


def _s5prep_kernel(ar_ref, ai_ref, ldt_ref, bre_ref, bim_ref, btre_ref, btim_ref, cre_ref, cim_ref,
                   mt_ref, wet_ref, wyt_ref, are_ref, aim_ref):
    p = C_STATE
    for pr in range(S5PREP_PAIRS):
        wet_re, wet_im, wyt_rows, a_re, a_im = [], [], [], [], []
        for g in range(2):
            i = 2 * pr + g
            kk, e_re, e_im, y_re, y_im, p_re, p_im = _s5_discretise(
                ar_ref[i], ai_ref[i], ldt_ref[i], bre_ref[i], bim_ref[i], btre_ref[i], btim_ref[i],
                cre_ref[i], cim_ref[i])
            pieces = [kk] + [jnp.concatenate([jnp.zeros((C_GROUP_CH * s, C_GROUP_CH), F32),
                                              kk[:S5_TC - C_GROUP_CH * s]], axis=0) for s in range(1, S5_CHUNK)]
            mt_ref[pr, g] = jnp.concatenate(pieces, axis=1).astype(BF16)
            e_t = jnp.concatenate([e_re, e_im], axis=1).T
            zero = jnp.zeros((p, S5_TC), F32)
            wet_re.append(jnp.concatenate([e_t[:p], zero] if g == 0 else [zero, e_t[:p]], axis=1))
            wet_im.append(jnp.concatenate([e_t[p:], zero] if g == 0 else [zero, e_t[p:]], axis=1))
            zero = jnp.zeros((S5_TC, p), F32)
            wyt_rows.append(
                jnp.concatenate([y_re, zero, -y_im, zero] if g == 0 else [zero, y_re, zero, -y_im], axis=1))
            a_re.append(p_re)
            a_im.append(p_im)
        wet_ref[pr] = jnp.concatenate(wet_re + wet_im, axis=0).astype(BF16)
        wyt_ref[pr] = jnp.concatenate(wyt_rows, axis=0).astype(BF16)
        r = lax.broadcasted_iota(jnp.int32, (S5_STEP_ROWS, 1), 0)
        f_re, f_im = jnp.concatenate(a_re, axis=1), jnp.concatenate(a_im, axis=1)
        t_re, t_im = jnp.ones((S5_STEP_ROWS, 2 * p), F32), jnp.zeros((S5_STEP_ROWS, 2 * p), F32)
        for k in range(S5_SEG_CHUNKS.bit_length()):
            s_re, s_im = jnp.where((r >> k) & 1 == 1, f_re, 1.0), jnp.where((r >> k) & 1 == 1, f_im, 0.0)
            t_re, t_im = t_re * s_re - t_im * s_im, t_re * s_im + t_im * s_re
            f_re, f_im = f_re * f_re - f_im * f_im, 2.0 * (f_re * f_im)
        are_ref[pr] = t_re
        aim_ref[pr] = t_im


def _s5_discretise(ar, ai, ldt, b_re, b_im, bt_re16, bt_im16, c_re16, c_im16):
    dt = jnp.exp(ldt)

    def cmul(xr, xi, yr, yi):
        return xr * yr - xi * yi, xr * yi + xi * yr

    kf = lax.broadcasted_iota(jnp.int32, (S5_POW_ROWS, 1), 0).astype(F32)
    mag = jnp.exp(kf * (ar * dt))
    ang = kf * (ai * dt)
    pw_re, pw_im = mag * jnp.cos(ang), mag * jnp.sin(ang)
    abar_re, abar_im = pw_re[1:2], pw_im[1:2]
    den = ar * ar + ai * ai
    num_re = abar_re - 1.0
    f_re = (num_re * ar + abar_im * ai) / den
    f_im = (abar_im * ar - num_re * ai) / den
    g_re, g_im = cmul(pw_re, pw_im, f_re, f_im)

    def pick(which, xr, xi):
        rep = lambda x: jnp.concatenate(
            [jnp.broadcast_to(x[which(i):which(i) + 1], (C_GROUP_CH, C_STATE)) for i in range(S5_CHUNK)], axis=0)
        return rep(xr), rep(xi)

    tile16 = lambda a: jnp.concatenate([a] * S5_CHUNK, axis=0)
    ct_re, ct_im = tile16(c_re16), tile16(c_im16)
    bt_re, bt_im = tile16(bt_re16), tile16(bt_im16)

    w_re, w_im = cmul(*pick(lambda i: i, g_re, g_im), ct_re, ct_im)
    kk = (jnp.dot(w_re, b_re, preferred_element_type=F32, precision=HIGHEST)
          - jnp.dot(w_im, b_im, preferred_element_type=F32, precision=HIGHEST))
    e_re, e_im = cmul(*pick(lambda i: S5_CHUNK - 1 - i, g_re, g_im), bt_re, bt_im)
    y_re, y_im = cmul(*pick(lambda i: i + 1, pw_re, pw_im), ct_re, ct_im)
    return kk, e_re, e_im, y_re, y_im, pw_re[S5_CHUNK:S5_CHUNK + 1], pw_im[S5_CHUNK:S5_CHUNK + 1]


def _s5prep(a_re, a_im, log_dt, b_re, b_im, c_re, c_im, d):
    p, ch = C_STATE, C_GROUP_CH
    layers = a_re.shape[0]
    g = layers * C_GROUPS
    npair = g // 2
    flat = lambda a: a.reshape(g, *a.shape[2:])
    a_re, a_im, log_dt, b_re, b_im, c_re, c_im = map(flat, (a_re, a_im, log_dt, b_re, b_im, c_re, c_im))
    row = lambda a: a.reshape(g, 1, p)
    ldt = jnp.broadcast_to(log_dt[:, None, None], (g, 1, p))
    b_t = lambda a: jnp.swapaxes(a, 1, 2)
    pps = S5PREP_PAIRS
    spec = lambda s1, s2: pl.BlockSpec((2 * pps, s1, s2), lambda i: (i, 0, 0))
    mt, wet, wyt, pw_re, pw_im = pl.pallas_call(
        _s5prep_kernel,
        out_shape=[jax.ShapeDtypeStruct((npair, 2, S5_TC, S5_TC), BF16),
                   jax.ShapeDtypeStruct((npair, 4 * p, 2 * S5_TC), BF16),
                   jax.ShapeDtypeStruct((npair, 2 * S5_TC, 4 * p), BF16),
                   jax.ShapeDtypeStruct((npair, S5_STEP_ROWS, 2 * p), F32),
                   jax.ShapeDtypeStruct((npair, S5_STEP_ROWS, 2 * p), F32)],
        grid=(npair // pps,),
        in_specs=[spec(1, p), spec(1, p), spec(1, p), spec(p, ch), spec(p, ch),
                  spec(ch, p), spec(ch, p), spec(ch, p), spec(ch, p)],
        out_specs=[pl.BlockSpec((pps, 2, S5_TC, S5_TC), lambda i: (i, 0, 0, 0)),
                   pl.BlockSpec((pps, 4 * p, 2 * S5_TC), lambda i: (i, 0, 0)),
                   pl.BlockSpec((pps, 2 * S5_TC, 4 * p), lambda i: (i, 0, 0)),
                   pl.BlockSpec((pps, S5_STEP_ROWS, 2 * p), lambda i: (i, 0, 0)),
                   pl.BlockSpec((pps, S5_STEP_ROWS, 2 * p), lambda i: (i, 0, 0))],
        compiler_params=_cparams(("arbitrary",)),
        name="s5prep",
    )(row(a_re), row(a_im), ldt, b_re, b_im, b_t(b_re), b_t(b_im), c_re, c_im)
    by_gb = lambda a: a.reshape(layers * S5_NGB, S5_PAIRS_PER_GB, *a.shape[1:])
    return (mt.reshape(layers * S5_NGB, S5_GB, S5_TC, S5_TC), by_gb(wet), by_gb(wyt), by_gb(pw_re), by_gb(pw_im),
            d.reshape(layers, 1, C_WIDTH))


def _s5_kernel(u_hbm, mt_ref, wet_ref, wyt_ref, are_ref, aim_ref, d_ref, y_hbm,
               ubuf, ybuf, in_sem, out_sem, ut_ref, yt_ref, ere_ref, eim_ref, hre_ref, him_ref):
    nck, t_len, ch, seg = S5_NCHUNK, S5_CHUNK, C_GROUP_CH, S5_SEG_CHUNKS
    nt = (((1,), (1,)), ((), ()))
    n_steps = S5_NGB * BATCH
    step = pl.program_id(0) * BATCH + pl.program_id(1)
    slot = step % 2

    def hbm_tile(ref, step_, s):
        gb_, b_ = step_ // BATCH, step_ % BATCH
        return ref.at[pl.ds(b_ * nck + s * seg, seg), :, pl.ds(gb_ * LANES, LANES)]

    def in_copy(step_, slot_, s):
        return pltpu.make_async_copy(hbm_tile(u_hbm, step_, s), ubuf.at[slot_, :, :, s, :], in_sem.at[slot_, s])

    def out_copy(step_, slot_, s):
        return pltpu.make_async_copy(ybuf.at[slot_, :, :, s, :], hbm_tile(y_hbm, step_, s), out_sem.at[slot_, s])

    @pl.when(step == 0)
    def _():
        for s in range(S5_SEGS):
            in_copy(step, slot, s).start()

    @pl.when(step + 1 < n_steps)
    def _():
        for s in range(S5_SEGS):
            in_copy(step + 1, 1 - slot, s).start()

    @pl.when(step >= 2)
    def _():
        for s in range(S5_SEGS):
            out_copy(step - 2, slot, s).wait()

    for s in range(S5_SEGS):
        in_copy(step, slot, s).wait()
    u_rows = lambda t: ubuf[slot, :, t].reshape(nck, LANES)
    for t in range(t_len):
        xt = u_rows(t).T
        for g in range(S5_GB):
            ut_ref[g, ch * t:ch * (t + 1), :] = xt[ch * g:ch * (g + 1), :]
    for j in range(S5_PAIRS_PER_GB):
        u0 = ut_ref[2 * j].astype(BF16)
        u1 = ut_ref[2 * j + 1].astype(BF16)
        et = jnp.dot(wet_ref[0, j], jnp.concatenate([u0, u1], axis=0), preferred_element_type=F32)
        e = et.T
        ere_ref[:, LANES * j:LANES * (j + 1)] = e[:, :LANES]
        eim_ref[:, LANES * j:LANES * (j + 1)] = e[:, LANES:]
        yt_ref[2 * j] = jnp.dot(mt_ref[0, 2 * j], u0, preferred_element_type=F32)
        yt_ref[2 * j + 1] = jnp.dot(mt_ref[0, 2 * j + 1], u1, preferred_element_type=F32)

    width = S5_GB * C_STATE

    def step_power(r, rows):
        row = lambda ref: jnp.concatenate([ref[0, j, r:r + 1, :] for j in range(S5_PAIRS_PER_GB)], axis=1)
        return (jnp.broadcast_to(row(are_ref), (rows, width)), jnp.broadcast_to(row(aim_ref), (rows, width)))

    a_re, a_im = step_power(1, S5_SEGS)

    def body(i, carry):
        h_re, h_im = carry
        rows = pl.ds(pl.multiple_of(i * S5_SEGS, S5_SEGS), S5_SEGS)
        hre_ref[rows, :] = h_re
        him_ref[rows, :] = h_im
        e_re = ere_ref[rows, :]
        e_im = eim_ref[rows, :]
        return a_re * h_re - a_im * h_im + e_re, a_re * h_im + a_im * h_re + e_im

    zero = jnp.zeros((S5_SEGS, width), F32)
    end_re, end_im = lax.fori_loop(0, seg, body, (zero, zero))
    g_re, g_im = step_power(seg, 1)
    sub = lax.broadcasted_iota(jnp.int32, (S5_SEGS, 1), 0)
    c_re = c_im = zero
    p_re = p_im = jnp.zeros((1, width), F32)
    for s in range(1, S5_SEGS):
        p_re, p_im = (g_re * p_re - g_im * p_im + end_re[s - 1:s], g_re * p_im + g_im * p_re + end_im[s - 1:s])
        c_re, c_im = jnp.where(sub == s, p_re, c_re), jnp.where(sub == s, p_im, c_im)
    for i in range(seg):
        rows = slice(S5_SEGS * i, S5_SEGS * (i + 1))
        w_re, w_im = step_power(i, S5_SEGS)
        hre_ref[rows, :] += w_re * c_re - w_im * c_im
        him_ref[rows, :] += w_re * c_im + w_im * c_re

    for j in range(S5_PAIRS_PER_GB):
        sl = slice(LANES * j, LANES * (j + 1))
        hp = jnp.concatenate([hre_ref[:, sl], him_ref[:, sl]], axis=1).astype(BF16)
        yi = lax.dot_general(wyt_ref[0, j], hp, nt, preferred_element_type=F32)
        yt_ref[2 * j] += yi[:S5_TC]
        yt_ref[2 * j + 1] += yi[S5_TC:]
    for t in range(t_len):
        ytt = jnp.concatenate([yt_ref[g, ch * t:ch * (t + 1), :] for g in range(S5_GB)], axis=0)
        ybuf[slot, :, t] = (ytt.T + d_ref[...] * u_rows(t)).reshape(seg, S5_SEGS, LANES)
    for s in range(S5_SEGS):
        out_copy(step, slot, s).start()

    @pl.when(step == n_steps - 1)
    def _():
        for s in range(S5_SEGS):
            out_copy(step - 1, 1 - slot, s).wait()
            out_copy(step, slot, s).wait()


def _s5(cu, mt, wet, wyt, pw_re, pw_im, d_row, layer):
    p4 = 4 * C_STATE
    per_gb = lambda *s: pl.BlockSpec((1,) + s, lambda gb, b: (layer * S5_NGB + gb,) + (0,) * len(s))
    state = pltpu.VMEM((S5_NCHUNK, S5_GB * C_STATE), F32)
    tiles = pltpu.VMEM((2, S5_SEG_CHUNKS, S5_CHUNK, S5_SEGS, LANES), F32)
    sems = pltpu.SemaphoreType.DMA((2, S5_SEGS))
    by_chunk = (TOKENS // S5_CHUNK, S5_CHUNK, C_WIDTH)
    y = pl.pallas_call(
        _s5_kernel,
        out_shape=jax.ShapeDtypeStruct(by_chunk, F32),
        grid=(S5_NGB, BATCH),
        in_specs=[pl.BlockSpec(memory_space=pl.ANY),
                  per_gb(S5_GB, S5_TC, S5_TC),
                  per_gb(S5_PAIRS_PER_GB, p4, 2 * S5_TC),
                  per_gb(S5_PAIRS_PER_GB, 2 * S5_TC, p4),
                  per_gb(S5_PAIRS_PER_GB, S5_STEP_ROWS, 2 * C_STATE),
                  per_gb(S5_PAIRS_PER_GB, S5_STEP_ROWS, 2 * C_STATE),
                  pl.BlockSpec((1, LANES), lambda gb, b: (0, gb))],
        out_specs=pl.BlockSpec(memory_space=pl.ANY),
        scratch_shapes=[tiles, tiles, sems, sems,
                        pltpu.VMEM((S5_GB, S5_TC, S5_NCHUNK), F32),
                        pltpu.VMEM((S5_GB, S5_TC, S5_NCHUNK), F32),
                        state, state, state, state],
        compiler_params=_cparams(("arbitrary", "arbitrary")),
        name="s5",
    )(cu.reshape(by_chunk), mt, wet, wyt, pw_re, pw_im, d_row)
    return y.reshape(TOKENS, C_WIDTH)


def _gelu_tanh(x):
    return 0.5 * x * (1.0 + jnp.tanh(math.sqrt(2.0 / math.pi) * (x + 0.044715 * (x * x * x))))


def _out_kernel(sink_ref, qt_ref, kp_ref, kc_ref, vtp_ref, vtc_ref, ag_ref, bias_ref,
                ob_ref, yc_ref, cg_ref, x_ref, gate_ref, gpost_ref, wglu_ref, bglu_ref, wout_ref, o_ref, oa_ref):
    first_tile = pl.program_id(0) % (SEQ // OUT_TM) == 0
    blocks_per_part = OUT_TM // (A_BLOCK * OUT_PARTS)

    def project(part):
        r = slice(A_BLOCK * blocks_per_part * part, A_BLOCK * blocks_per_part * (part + 1))
        y = _gelu_tanh(yc_ref[r, :])
        z = jnp.dot(y.astype(BF16), wglu_ref[...], preferred_element_type=F32) + bglu_ref[...]
        y = y * jax.nn.sigmoid(z)
        oc = (y * _silu(cg_ref[r, :].astype(F32))).astype(BF16)
        mix = jnp.concatenate([oa_ref[r, :], ob_ref[r, :], oc], axis=1)
        acc = jnp.dot(mix, wout_ref[...], preferred_element_type=F32)
        ms = jnp.mean(acc * acc, axis=-1, keepdims=True)
        out = acc * lax.rsqrt(ms + EPS) * gpost_ref[...]
        o_ref[r, :] = x_ref[r, :] + gate_ref[0] * out

    def after_block(j):
        if (j + 1) % blocks_per_part == 0:
            project(j // blocks_per_part)

    _attention_steps(sink_ref, qt_ref, kp_ref, kc_ref, vtp_ref, vtc_ref, ag_ref, bias_ref, oa_ref, first_tile,
                     OUT_TM // A_BLOCK, blocks_per_part, after_block)


def _out(sinks, qt, k, vt, ag, bias, ob, yc, cg, x2, gate, g_post, w_glu, b_glu, w_out, layer):
    tm = OUT_TM
    steps_per_batch = SEQ // tm
    blocks = tm // A_BLOCK
    row = lambda i: (i, 0)
    col = lambda i: (0, i)
    const = lambda i: (0, 0)
    once = dict(pipeline_mode=pl.Buffered(1))
    prev = lambda i: (i * blocks - jnp.minimum(i % steps_per_batch, 1), 0)
    prev_t = lambda i: (0, i * blocks - jnp.minimum(i % steps_per_batch, 1))
    return pl.pallas_call(
        _out_kernel,
        out_shape=jax.ShapeDtypeStruct((TOKENS, D_MODEL), F32),
        grid=(TOKENS // tm,),
        in_specs=[pl.BlockSpec(memory_space=pltpu.SMEM),
                  pl.BlockSpec((A_WIDTH, tm), col),
                  pl.BlockSpec((A_BLOCK, A_KV_WIDTH), prev),
                  pl.BlockSpec((tm, A_KV_WIDTH), row),
                  pl.BlockSpec((A_KV_WIDTH, A_BLOCK), prev_t),
                  pl.BlockSpec((A_KV_WIDTH, tm), col),
                  pl.BlockSpec((tm, A_WIDTH), row),
                  pl.BlockSpec((2, A_Q_HEADS, A_BLOCK, A_BLOCK), lambda i: (0, 0, 0, 0), **once),
                  pl.BlockSpec((tm, B_WIDTH), row),
                  pl.BlockSpec((tm, C_WIDTH), row),
                  pl.BlockSpec((tm, C_WIDTH), row),
                  pl.BlockSpec((tm, D_MODEL), row),
                  pl.BlockSpec((1, 1, D_MODEL), lambda i: (i // steps_per_batch, 0, 0)),
                  pl.BlockSpec((1, D_MODEL), const),
                  pl.BlockSpec((C_WIDTH, C_WIDTH), lambda i: (layer, 0), **once),
                  pl.BlockSpec((1, C_WIDTH), const),
                  pl.BlockSpec((2 * D_MODEL, D_MODEL), lambda i: (layer, 0), **once)],
        out_specs=pl.BlockSpec((tm, D_MODEL), row),
        scratch_shapes=[pltpu.VMEM((tm, A_WIDTH), BF16)],
        compiler_params=_cparams(("arbitrary",)),
        name="out",
    )(sinks, qt, k, k, vt, vt, ag, bias, ob, yc, cg, x2, gate, g_post.reshape(1, D_MODEL), w_glu,
      b_glu.reshape(1, C_WIDTH), w_out)


def kernel(x, c, w_mod, b_mod, g_pre, g_post, w_in, attn_sinks, gla_w_alpha, gla_b_alpha, gla_norm_g,
           s5_a_re, s5_a_im, s5_log_dt, s5_b_re, s5_b_im, s5_c_re, s5_c_im, s5_d, s5_w_glu, s5_b_glu, w_out):
    layers = w_mod.shape[0]
    x2 = x.reshape(TOKENS, D_MODEL)
    bias = _attn_bias()
    mod = _mod(jnp.pad(c, ((0, 8 - BATCH), (0, 0))), w_mod, b_mod)[:, :BATCH]
    shift, scale, gate = (m.reshape(layers, BATCH, 1, D_MODEL) for m in jnp.split(mod, 3, axis=-1))
    w_t = jnp.swapaxes(w_in, 1, 2).astype(BF16).reshape(layers * W_IN_COLS, D_MODEL)
    w_alpha_pad = jnp.pad(gla_w_alpha, ((0, 0), (0, LR_PAD - B_GATE_RANK), (0, 0))).astype(BF16)
    s5_ops = _s5prep(s5_a_re, s5_a_im, s5_log_dt, s5_b_re, s5_b_im, s5_c_re, s5_c_im, s5_d)
    *s5_ops, s5_d_rows = s5_ops
    w_glu = s5_w_glu.astype(BF16).reshape(layers * C_WIDTH, C_WIDTH)
    w_out_b = w_out.astype(BF16).reshape(layers * 2 * D_MODEL, D_MODEL)
    for l in range(layers):
        ak, ag, bq, bk, bv, bg, cu, cg, log_a, aqt, avt = _proj(
            x2, scale[l], shift[l], g_pre[l], w_t, l, w_alpha_pad[l], gla_b_alpha[l])
        o_b = _gla(log_a, bq, bk, bv, bg, gla_norm_g[l])
        y_c = _s5(cu, *s5_ops, s5_d_rows[l], l)
        x2 = _out(attn_sinks[l], aqt, ak, avt, ag, bias, o_b, y_c, cg, x2, gate[l], g_post[l], w_glu, s5_b_glu[l],
                  w_out_b, l)
    return x2.reshape(x.shape)
```

```python
import math

import jax
import jax.numpy as jnp
import numpy as np
from jax import lax
from jax.experimental import pallas as pl
from jax.experimental.pallas import tpu as pltpu

F32 = jnp.float32
BF16 = jnp.bfloat16
HIGHEST = lax.Precision.HIGHEST

D_MODEL = 1024
BATCH = 4
SEQ = 4096
TOKENS = BATCH * SEQ
EPS = 1e-6

A_WIDTH = 1024
A_HEAD_DIM = 64
A_Q_HEADS = 16
A_KV_HEADS = 4
A_KV_WIDTH = A_KV_HEADS * A_HEAD_DIM
A_BLOCK = 128
WINDOW = 128
assert WINDOW == A_BLOCK

B_WIDTH = 512
B_HEADS = 4
B_DK = 64
B_DV = 128
B_QK_WIDTH = 256
B_GATE_RANK = 16
B_GATE_TAU = 16.0
GLA_BLOCK = 64
GLA_SUB = 8
GLA_DIAG = 16
GLA_ANCHOR_SEGS = (32, 64)
assert GLA_ANCHOR_SEGS[0] == 2 * GLA_DIAG and GLA_ANCHOR_SEGS[-1] == GLA_BLOCK
OUT_PARTS = 2

C_WIDTH = 512
C_GROUP_CH = 16
C_GROUPS = 32
C_STATE = 64
S5_CHUNK = 16
S5_NCHUNK = SEQ // S5_CHUNK
S5_TC = S5_CHUNK * C_GROUP_CH
S5_GB = 8
S5_NGB = C_GROUPS // S5_GB
S5_PAIRS_PER_GB = S5_GB // 2
S5_POW_ROWS = 24
S5PREP_PAIRS = 4
SUBLANES = 8
S5_SEGS = SUBLANES
S5_SEG_CHUNKS = S5_NCHUNK // S5_SEGS
S5_STEP_ROWS = S5_SEG_CHUNKS + SUBLANES

LANES = 128

V7X_VMEM_LIMIT = 56 * 1024 * 1024

PROJ_TM = 1024
OUT_TM = 1024

_W_IN_SIZES = (("aq", A_WIDTH), ("ak", A_KV_WIDTH), ("av", A_KV_WIDTH), ("ag", A_WIDTH), ("bq", B_QK_WIDTH),
               ("bk", B_QK_WIDTH), ("bv", B_WIDTH), ("blr", B_GATE_RANK), ("bg", B_WIDTH), ("cu", C_WIDTH),
               ("cg", C_WIDTH))
_W_IN_OFF = {}
_off = 0
for _name, _w in _W_IN_SIZES:
    _W_IN_OFF[_name] = (_off, _w)
    _off += _w
W_IN_COLS = _off
_PROJ_OUTS = (("ak", BF16), ("ag", BF16), ("bq", BF16), ("bk", BF16),
              ("bv", BF16), ("bg", BF16), ("cu", F32), ("cg", BF16))


def _silu(x):
    return x * jax.nn.sigmoid(x)


def _cparams(semantics):
    return pltpu.CompilerParams(dimension_semantics=semantics, vmem_limit_bytes=V7X_VMEM_LIMIT)


def _mod_kernel(c_ref, w_ref, b_ref, o_ref):
    c = c_ref[...]
    o_ref[0] = jnp.dot(_silu(c).astype(BF16), w_ref[0].astype(BF16), preferred_element_type=F32) + b_ref[0]


def _mod(c_pad, w_mod, b_mod):
    layers = w_mod.shape[0]
    n = 3 * D_MODEL
    tn = 768
    return pl.pallas_call(
        _mod_kernel,
        out_shape=jax.ShapeDtypeStruct((layers, 8, n), F32),
        grid=(layers, n // tn),
        in_specs=[pl.BlockSpec((8, D_MODEL), lambda l, j: (0, 0)),
                  pl.BlockSpec((1, D_MODEL, tn), lambda l, j: (l, 0, j)),
                  pl.BlockSpec((1, 1, tn), lambda l, j: (l, 0, j))],
        out_specs=pl.BlockSpec((1, 8, tn), lambda l, j: (l, 0, j)),
        compiler_params=_cparams(("arbitrary", "arbitrary")),
        name="mod",
    )(c_pad, w_mod, b_mod.reshape(layers, 1, n))


def _proj_kernel(x_ref, scale_ref, shift_ref, gpre_ref, wt_ref, walpha_ref, balpha_ref, *out_refs):
    x = x_ref[...]
    ms = jnp.mean(x * x, axis=-1, keepdims=True)
    y = x * lax.rsqrt(ms + EPS) * gpre_ref[...]
    h = (y * (1.0 + scale_ref[0]) + shift_ref[0]).astype(BF16)
    nt = (((1,), (1,)), ((), ()))

    def rows(name):
        off, w = _W_IN_OFF[name]
        return wt_ref[off:off + w, :]

    for (name, _), o_ref in zip(_PROJ_OUTS, out_refs):
        o_ref[...] = lax.dot_general(h, rows(name), nt, preferred_element_type=F32).astype(o_ref.dtype)
    la_ref, qt_ref, vt_ref = out_refs[len(_PROJ_OUTS):]
    qlr = lax.dot_general(jnp.concatenate([rows("aq"), rows("blr")], axis=0), h, nt, preferred_element_type=F32)
    qt_ref[...] = (qlr[:A_WIDTH] * (A_HEAD_DIM ** -0.5)).astype(qt_ref.dtype)
    vt_ref[...] = lax.dot_general(rows("av"), h, nt, preferred_element_type=F32).astype(vt_ref.dtype)
    lr_t = qlr[A_WIDTH:].astype(BF16)
    logits = lax.dot_general(lr_t, walpha_ref[...], (((0,), (0,)), ((), ())),
                             preferred_element_type=F32) + balpha_ref[...]
    log_sig = jnp.minimum(logits, 0.0) - jnp.log(1.0 + jnp.exp(-jnp.abs(logits)))
    la_ref[...] = log_sig * (math.log2(math.e) / B_GATE_TAU)


def _proj(x2, scale, shift, g_pre, w_t, layer, w_alpha, b_alpha):
    tm = PROJ_TM
    steps_per_batch = SEQ // tm
    row = lambda i: (i, 0)
    col = lambda i: (0, i)
    per_batch = lambda i: (i // steps_per_batch, 0, 0)
    const = lambda i: (0, 0)
    out_shape = [jax.ShapeDtypeStruct((TOKENS, _W_IN_OFF[n][1]), dt) for n, dt in _PROJ_OUTS]
    out_specs = [pl.BlockSpec((tm, _W_IN_OFF[n][1]), row) for n, _ in _PROJ_OUTS]
    out_shape += [jax.ShapeDtypeStruct((TOKENS, B_QK_WIDTH), F32),
                  jax.ShapeDtypeStruct((A_WIDTH, TOKENS), BF16),
                  jax.ShapeDtypeStruct((A_KV_WIDTH, TOKENS), BF16)]
    out_specs += [pl.BlockSpec((tm, B_QK_WIDTH), row),
                  pl.BlockSpec((A_WIDTH, tm), col),
                  pl.BlockSpec((A_KV_WIDTH, tm), col)]
    return pl.pallas_call(
        _proj_kernel,
        out_shape=out_shape,
        grid=(TOKENS // tm,),
        in_specs=[pl.BlockSpec((tm, D_MODEL), row),
                  pl.BlockSpec((1, 1, D_MODEL), per_batch),
                  pl.BlockSpec((1, 1, D_MODEL), per_batch),
                  pl.BlockSpec((1, D_MODEL), const),
                  pl.BlockSpec((W_IN_COLS, D_MODEL), lambda i: (layer, 0), pipeline_mode=pl.Buffered(1)),
                  pl.BlockSpec((B_GATE_RANK, B_QK_WIDTH), const),
                  pl.BlockSpec((1, B_QK_WIDTH), const)],
        out_specs=out_specs,
        compiler_params=_cparams(("arbitrary",)),
        name="proj",
    )(x2, scale, shift, g_pre.reshape(1, D_MODEL), w_t, w_alpha, b_alpha.reshape(1, B_QK_WIDTH))


def _attn_bias():
    j = np.arange(A_BLOCK)[:, None]
    i = np.arange(A_BLOCK)[None, :]
    dist = np.where(j > i, i + A_BLOCK - j, i - j).astype(np.float32)
    slopes = np.exp2(-8.0 * np.arange(1, A_Q_HEADS + 1, dtype=np.float32) / A_Q_HEADS).astype(np.float32)
    bias = -slopes[:, None, None] * dist[None]
    first = np.where((j > i)[None], -np.inf, bias).astype(np.float32)
    return jnp.asarray(np.stack([bias, first]))


def _attention_steps(sink_ref, qt_ref, kp_ref, kc_ref, vtp_ref, vtc_ref, g_ref, bias_ref, oa_ref, first_tile,
                     n_blocks, group_blocks, after_block):
    kj = lax.broadcasted_iota(jnp.int32, (A_BLOCK, A_BLOCK), 0)
    qi = lax.broadcasted_iota(jnp.int32, (A_BLOCK, A_BLOCK), 1)
    from_prev = kj > qi
    zero_rows = jnp.zeros((A_HEAD_DIM, A_BLOCK), BF16)
    group = A_Q_HEADS // A_KV_HEADS
    blk = lambda j: slice(A_BLOCK * j, A_BLOCK * (j + 1))

    def keys(j, sl):
        prev = kp_ref[:, sl] if j == 0 else kc_ref[blk(j - 1), sl]
        return prev, kc_ref[blk(j), sl]

    def values(j, rows):
        prev = vtp_ref[rows, :] if j == 0 else vtc_ref[rows, blk(j - 1)]
        return jnp.concatenate([prev, vtc_ref[rows, blk(j)]], axis=1)

    def scores(j, hd):
        kvh = hd // group
        sl = slice(LANES * (kvh // 2), LANES * (kvh // 2 + 1))
        qh = qt_ref[A_HEAD_DIM * hd:A_HEAD_DIM * (hd + 1), blk(j)]
        qsel = jnp.concatenate([qh, zero_rows] if kvh % 2 == 0 else [zero_rows, qh], axis=0)
        k_prev, k_cur = keys(j, sl)
        return (jnp.dot(k_prev, qsel, preferred_element_type=F32),
                jnp.dot(k_cur, qsel, preferred_element_type=F32))

    def attend(j, hd, s_prev, s_cur):
        kvh = hd // group
        v_both = values(j, slice(A_HEAD_DIM * kvh, A_HEAD_DIM * (kvh + 1)))
        table = jnp.where(first_tile, 1, 0) if j == 0 else 0
        s = jnp.where(from_prev, s_prev, s_cur) + bias_ref[table, hd]
        sink = sink_ref[hd]
        m = jnp.maximum(jnp.max(s, axis=0, keepdims=True), sink)
        p = jnp.exp(s - m)
        den = jnp.sum(p, axis=0, keepdims=True) + jnp.exp(sink - m)
        p_both = jnp.concatenate([jnp.where(from_prev, p, 0.0), jnp.where(from_prev, 0.0, p)],
                                 axis=0).astype(BF16)
        return jnp.dot(v_both, p_both, preferred_element_type=F32) / den

    pending = {}

    def issue_scores(first):
        for j in range(first, min(first + group_blocks, n_blocks)):
            for hd in range(A_Q_HEADS):
                pending[j, hd] = scores(j, hd)

    issue_scores(0)
    for j in range(n_blocks):
        if (j + 1) % group_blocks == 0:
            issue_scores(j + 1)
        outs = {}
        for hd in range(A_Q_HEADS):
            outs[hd] = attend(j, hd, *pending.pop((j, hd)))
            if hd % 2 == 1:
                qsl = slice(LANES * (hd // 2), LANES * (hd // 2 + 1))
                o_pair = jnp.concatenate([outs.pop(hd - 1), outs.pop(hd)], axis=0).T
                gate = g_ref[blk(j), qsl].astype(F32)
                oa_ref[blk(j), qsl] = (o_pair * _silu(gate)).astype(oa_ref.dtype)
        after_block(j)


def _gla_kernel(la_ref, q_ref, k_ref, v_ref, g_ref, gn_ref, o_ref, st_ref):
    cb = GLA_BLOCK

    @pl.when(pl.program_id(0) == 0)
    def _():
        st_ref[...] = jnp.zeros_like(st_ref)

    r = lax.broadcasted_iota(jnp.int32, (cb, cb), 0)
    c = lax.broadcasted_iota(jnp.int32, (cb, cb), 1)
    tri = (c <= r).astype(BF16)
    lane = lax.broadcasted_iota(jnp.int32, (1, B_QK_WIDTH), 1)
    head_masks = [(lane >= B_DK * h) & (lane < B_DK * (h + 1)) for h in range(B_HEADS)]
    rr = lax.broadcasted_iota(jnp.int32, (B_HEADS * cb, cb), 0)
    cc = lax.broadcasted_iota(jnp.int32, (B_HEADS * cb, cb), 1)
    ri = rr & (cb - 1)
    nt = (((1,), (1,)), ((), ()))
    tn = (((0,), (0,)), ((), ()))
    rows = lambda u: slice(cb * u, cb * (u + 1))
    items = [(u, b) for u in range(GLA_SUB) for b in range(BATCH)]
    def cumsum(la):
        hi = la.astype(BF16)
        r1 = la - hi.astype(F32)
        mid = r1.astype(BF16)
        lo = (r1 - mid.astype(F32)).astype(BF16)
        parts = jnp.dot(tri, jnp.concatenate([hi, mid, lo], axis=1), preferred_element_type=F32)
        w = B_QK_WIDTH
        return parts[:, :w] + (parts[:, w:2 * w] + parts[:, 2 * w:])

    bcs = {(u, b): cumsum(la_ref[b, rows(u), :]) for u, b in items}
    stack_heads = lambda a: jnp.concatenate([jnp.where(m, a, 0.0) for m in head_masks], axis=0).astype(BF16)
    row = lax.broadcasted_iota(jnp.int32, (cb, 1), 0)
    diag = ((ri ^ cc) < GLA_DIAG) & (cc <= ri)
    same_seg = {seg: (ri ^ cc) < seg for seg in GLA_ANCHOR_SEGS}

    def upper(a, seg):
        return jnp.concatenate([a[s0 + seg // 2:s0 + seg] for s0 in range(0, cb, seg)], axis=0)

    def place_upper(p, seg):
        half = seg // 2
        zero = jnp.zeros((half, cb), F32)
        parts = []
        for piece in range(B_HEADS * cb // seg):
            parts += [zero, p[half * piece:half * (piece + 1)]]
        return jnp.concatenate(parts, axis=0)

    def anchor_rows(bc, seg, offset, reps):
        parts = []
        for s0 in range(0, cb, seg):
            a = s0 + offset - 1
            val = bc[a:a + 1, :] if a >= 0 else jnp.zeros((1, B_QK_WIDTH), F32)
            parts.append(jnp.broadcast_to(val, (reps, B_QK_WIDTH)))
        return jnp.concatenate(parts, axis=0)

    q_lv, k_lv, qsts, ksts, decs, vs = {}, {}, {}, {}, {}, {}
    for it in items:
        u, b = it
        bc = bcs[it]
        bl = bc[cb - 1:cb, :]
        q = q_ref[b, rows(u), :].astype(F32) * (B_DK ** -0.5)
        k = k_ref[b, rows(u), :].astype(F32)
        for seg in GLA_ANCHOR_SEGS:
            half = seg // 2
            q_lv[it, seg] = stack_heads(upper(q, seg) * jnp.exp2(upper(bc, seg) - anchor_rows(bc, seg, half, half)))
            in_lower = (row & (seg - 1)) < half
            k_lv[it, seg] = jnp.where(in_lower, k * jnp.exp2(anchor_rows(bc, seg, half, seg) - bc), 0.0).astype(BF16)
        anc = anchor_rows(bc, GLA_DIAG, 0, GLA_DIAG)
        q_lv[it, 0] = stack_heads(q * jnp.exp2(bc - anc))
        k_lv[it, 0] = (k * jnp.exp2(anc - bc)).astype(BF16)
        qsts[it] = stack_heads(q * jnp.exp2(bc))
        ksts[it] = stack_heads(k * jnp.exp2(bl - bc))
        decs[it] = jnp.exp2(bl)
        vs[it] = v_ref[b, rows(u), :]
    a_alls = {}
    for it in items:
        prod = lambda lv: lax.dot_general(q_lv[it, lv], k_lv[it, lv], nt, preferred_element_type=F32)
        a = jnp.where(diag, prod(0), 0.0)
        for seg in GLA_ANCHOR_SEGS:
            a = a + jnp.where(same_seg[seg], place_upper(prod(seg), seg), 0.0)
        a_alls[it] = a.astype(BF16)
    upds = {}
    for it in items:
        vst = jnp.concatenate([vs[it][:, B_DV * h:B_DV * (h + 1)] for h in range(B_HEADS)], axis=0)
        upds[it] = lax.dot_general(vst, ksts[it], tn, preferred_element_type=F32)
    st_in = {}
    for b in range(BATCH):
        st = st_ref[b]
        for u in range(GLA_SUB):
            st_in[u, b] = st
            st = st * decs[u, b] + upds[u, b]
        st_ref[b] = st
    oi_alls = {it: lax.dot_general(qsts[it], st_in[it].astype(BF16), nt, preferred_element_type=F32)
               for it in items}
    o_hs = {}
    for it in items:
        a_all = a_alls[it]
        for h in range(B_HEADS):
            o_hs[it, h] = (jnp.dot(a_all[cb * h:cb * (h + 1)], vs[it][:, B_DV * h:B_DV * (h + 1)],
                                   preferred_element_type=F32) + oi_alls[it][cb * h:cb * (h + 1)])
    for it in items:
        u, b = it
        for h in range(B_HEADS):
            vsl = slice(B_DV * h, B_DV * (h + 1))
            o_h = o_hs[it, h]
            ms = jnp.mean(o_h * o_h, axis=-1, keepdims=True)
            o_n = o_h * lax.rsqrt(ms + EPS) * gn_ref[:, vsl]
            gate = g_ref[b, rows(u), vsl].astype(F32)
            o_ref[b, rows(u), vsl] = (o_n * _silu(gate)).astype(o_ref.dtype)


def _gla(log_a, bq, bk, bv, bg, g_gla):
    cb = GLA_BLOCK * GLA_SUB
    blk = lambda w: pl.BlockSpec((BATCH, cb, w), lambda i: (0, i, 0))
    r3 = lambda a: a.reshape(BATCH, SEQ, a.shape[-1])
    out = pl.pallas_call(
        _gla_kernel,
        out_shape=jax.ShapeDtypeStruct((BATCH, SEQ, B_WIDTH), BF16),
        grid=(SEQ // cb,),
        in_specs=[blk(B_QK_WIDTH), blk(B_QK_WIDTH), blk(B_QK_WIDTH), blk(B_WIDTH), blk(B_WIDTH),
                  pl.BlockSpec((1, B_WIDTH), lambda i: (0, 0))],
        out_specs=blk(B_WIDTH),
        scratch_shapes=[pltpu.VMEM((BATCH, B_DV, B_QK_WIDTH), F32)],
        compiler_params=_cparams(("arbitrary",)),
        name="gla",
    )(r3(log_a), r3(bq), r3(bk), r3(bv), r3(bg), g_gla.reshape(1, B_WIDTH))
    return out.reshape(TOKENS, B_WIDTH)


def _s5prep_kernel(ar_ref, ai_ref, ldt_ref, bre_ref, bim_ref, btre_ref, btim_ref, cre_ref, cim_ref,
                   mt_ref, wet_ref, wyt_ref, are_ref, aim_ref):
    p = C_STATE
    for pr in range(S5PREP_PAIRS):
        wet_re, wet_im, wyt_rows, a_re, a_im = [], [], [], [], []
        for g in range(2):
            i = 2 * pr + g
            kk, e_re, e_im, y_re, y_im, p_re, p_im = _s5_discretise(
                ar_ref[i], ai_ref[i], ldt_ref[i], bre_ref[i], bim_ref[i], btre_ref[i], btim_ref[i],
                cre_ref[i], cim_ref[i])
            pieces = [kk] + [jnp.concatenate([jnp.zeros((C_GROUP_CH * s, C_GROUP_CH), F32),
                                              kk[:S5_TC - C_GROUP_CH * s]], axis=0) for s in range(1, S5_CHUNK)]
            mt_ref[pr, g] = jnp.concatenate(pieces, axis=1).astype(BF16)
            e_t = jnp.concatenate([e_re, e_im], axis=1).T
            zero = jnp.zeros((p, S5_TC), F32)
            wet_re.append(jnp.concatenate([e_t[:p], zero] if g == 0 else [zero, e_t[:p]], axis=1))
            wet_im.append(jnp.concatenate([e_t[p:], zero] if g == 0 else [zero, e_t[p:]], axis=1))
            zero = jnp.zeros((S5_TC, p), F32)
            wyt_rows.append(
                jnp.concatenate([y_re, zero, -y_im, zero] if g == 0 else [zero, y_re, zero, -y_im], axis=1))
            a_re.append(p_re)
            a_im.append(p_im)
        wet_ref[pr] = jnp.concatenate(wet_re + wet_im, axis=0).astype(BF16)
        wyt_ref[pr] = jnp.concatenate(wyt_rows, axis=0).astype(BF16)
        r = lax.broadcasted_iota(jnp.int32, (S5_STEP_ROWS, 1), 0)
        f_re, f_im = jnp.concatenate(a_re, axis=1), jnp.concatenate(a_im, axis=1)
        t_re, t_im = jnp.ones((S5_STEP_ROWS, 2 * p), F32), jnp.zeros((S5_STEP_ROWS, 2 * p), F32)
        for k in range(S5_SEG_CHUNKS.bit_length()):
            s_re, s_im = jnp.where((r >> k) & 1 == 1, f_re, 1.0), jnp.where((r >> k) & 1 == 1, f_im, 0.0)
            t_re, t_im = t_re * s_re - t_im * s_im, t_re * s_im + t_im * s_re
            f_re, f_im = f_re * f_re - f_im * f_im, 2.0 * (f_re * f_im)
        are_ref[pr] = t_re
        aim_ref[pr] = t_im


def _s5_discretise(ar, ai, ldt, b_re, b_im, bt_re16, bt_im16, c_re16, c_im16):
    dt = jnp.exp(ldt)

    def cmul(xr, xi, yr, yi):
        return xr * yr - xi * yi, xr * yi + xi * yr

    kf = lax.broadcasted_iota(jnp.int32, (S5_POW_ROWS, 1), 0).astype(F32)
    mag = jnp.exp(kf * (ar * dt))
    ang = kf * (ai * dt)
    pw_re, pw_im = mag * jnp.cos(ang), mag * jnp.sin(ang)
    abar_re, abar_im = pw_re[1:2], pw_im[1:2]
    den = ar * ar + ai * ai
    num_re = abar_re - 1.0
    f_re = (num_re * ar + abar_im * ai) / den
    f_im = (abar_im * ar - num_re * ai) / den
    g_re, g_im = cmul(pw_re, pw_im, f_re, f_im)

    def pick(which, xr, xi):
        rep = lambda x: jnp.concatenate(
            [jnp.broadcast_to(x[which(i):which(i) + 1], (C_GROUP_CH, C_STATE)) for i in range(S5_CHUNK)], axis=0)
        return rep(xr), rep(xi)

    tile16 = lambda a: jnp.concatenate([a] * S5_CHUNK, axis=0)
    ct_re, ct_im = tile16(c_re16), tile16(c_im16)
    bt_re, bt_im = tile16(bt_re16), tile16(bt_im16)

    w_re, w_im = cmul(*pick(lambda i: i, g_re, g_im), ct_re, ct_im)
    kk = (jnp.dot(w_re, b_re, preferred_element_type=F32, precision=HIGHEST)
          - jnp.dot(w_im, b_im, preferred_element_type=F32, precision=HIGHEST))
    e_re, e_im = cmul(*pick(lambda i: S5_CHUNK - 1 - i, g_re, g_im), bt_re, bt_im)
    y_re, y_im = cmul(*pick(lambda i: i + 1, pw_re, pw_im), ct_re, ct_im)
    return kk, e_re, e_im, y_re, y_im, pw_re[S5_CHUNK:S5_CHUNK + 1], pw_im[S5_CHUNK:S5_CHUNK + 1]


def _s5prep(a_re, a_im, log_dt, b_re, b_im, c_re, c_im, d):
    p, ch = C_STATE, C_GROUP_CH
    layers = a_re.shape[0]
    g = layers * C_GROUPS
    npair = g // 2
    flat = lambda a: a.reshape(g, *a.shape[2:])
    a_re, a_im, log_dt, b_re, b_im, c_re, c_im = map(flat, (a_re, a_im, log_dt, b_re, b_im, c_re, c_im))
    row = lambda a: a.reshape(g, 1, p)
    ldt = jnp.broadcast_to(log_dt[:, None, None], (g, 1, p))
    b_t = lambda a: jnp.swapaxes(a, 1, 2)
    pps = S5PREP_PAIRS
    spec = lambda s1, s2: pl.BlockSpec((2 * pps, s1, s2), lambda i: (i, 0, 0))
    mt, wet, wyt, pw_re, pw_im = pl.pallas_call(
        _s5prep_kernel,
        out_shape=[jax.ShapeDtypeStruct((npair, 2, S5_TC, S5_TC), BF16),
                   jax.ShapeDtypeStruct((npair, 4 * p, 2 * S5_TC), BF16),
                   jax.ShapeDtypeStruct((npair, 2 * S5_TC, 4 * p), BF16),
                   jax.ShapeDtypeStruct((npair, S5_STEP_ROWS, 2 * p), F32),
                   jax.ShapeDtypeStruct((npair, S5_STEP_ROWS, 2 * p), F32)],
        grid=(npair // pps,),
        in_specs=[spec(1, p), spec(1, p), spec(1, p), spec(p, ch), spec(p, ch),
                  spec(ch, p), spec(ch, p), spec(ch, p), spec(ch, p)],
        out_specs=[pl.BlockSpec((pps, 2, S5_TC, S5_TC), lambda i: (i, 0, 0, 0)),
                   pl.BlockSpec((pps, 4 * p, 2 * S5_TC), lambda i: (i, 0, 0)),
                   pl.BlockSpec((pps, 2 * S5_TC, 4 * p), lambda i: (i, 0, 0)),
                   pl.BlockSpec((pps, S5_STEP_ROWS, 2 * p), lambda i: (i, 0, 0)),
                   pl.BlockSpec((pps, S5_STEP_ROWS, 2 * p), lambda i: (i, 0, 0))],
        compiler_params=_cparams(("arbitrary",)),
        name="s5prep",
    )(row(a_re), row(a_im), ldt, b_re, b_im, b_t(b_re), b_t(b_im), c_re, c_im)
    by_gb = lambda a: a.reshape(layers * S5_NGB, S5_PAIRS_PER_GB, *a.shape[1:])
    return (mt.reshape(layers * S5_NGB, S5_GB, S5_TC, S5_TC), by_gb(wet), by_gb(wyt), by_gb(pw_re), by_gb(pw_im),
            d.reshape(layers, 1, C_WIDTH))


def _s5_kernel(u_hbm, mt_ref, wet_ref, wyt_ref, are_ref, aim_ref, d_ref, y_hbm,
               ubuf, ybuf, in_sem, out_sem, ut_ref, yt_ref, ere_ref, eim_ref, hre_ref, him_ref):
    nck, t_len, ch, seg = S5_NCHUNK, S5_CHUNK, C_GROUP_CH, S5_SEG_CHUNKS
    nt = (((1,), (1,)), ((), ()))
    n_steps = S5_NGB * BATCH
    step = pl.program_id(0) * BATCH + pl.program_id(1)
    slot = step % 2

    def hbm_tile(ref, step_, s):
        gb_, b_ = step_ // BATCH, step_ % BATCH
        return ref.at[pl.ds(b_ * nck + s * seg, seg), :, pl.ds(gb_ * LANES, LANES)]

    def in_copy(step_, slot_, s):
        return pltpu.make_async_copy(hbm_tile(u_hbm, step_, s), ubuf.at[slot_, :, :, s, :], in_sem.at[slot_, s])

    def out_copy(step_, slot_, s):
        return pltpu.make_async_copy(ybuf.at[slot_, :, :, s, :], hbm_tile(y_hbm, step_, s), out_sem.at[slot_, s])

    @pl.when(step == 0)
    def _():
        for s in range(S5_SEGS):
            in_copy(step, slot, s).start()

    @pl.when(step + 1 < n_steps)
    def _():
        for s in range(S5_SEGS):
            in_copy(step + 1, 1 - slot, s).start()

    @pl.when(step >= 2)
    def _():
        for s in range(S5_SEGS):
            out_copy(step - 2, slot, s).wait()

    for s in range(S5_SEGS):
        in_copy(step, slot, s).wait()
    u_rows = lambda t: ubuf[slot, :, t].reshape(nck, LANES)
    for t in range(t_len):
        xt = u_rows(t).T
        for g in range(S5_GB):
            ut_ref[g, ch * t:ch * (t + 1), :] = xt[ch * g:ch * (g + 1), :]
    for j in range(S5_PAIRS_PER_GB):
        u0 = ut_ref[2 * j].astype(BF16)
        u1 = ut_ref[2 * j + 1].astype(BF16)
        et = jnp.dot(wet_ref[0, j], jnp.concatenate([u0, u1], axis=0), preferred_element_type=F32)
        e = et.T
        ere_ref[:, LANES * j:LANES * (j + 1)] = e[:, :LANES]
        eim_ref[:, LANES * j:LANES * (j + 1)] = e[:, LANES:]
        yt_ref[2 * j] = jnp.dot(mt_ref[0, 2 * j], u0, preferred_element_type=F32)
        yt_ref[2 * j + 1] = jnp.dot(mt_ref[0, 2 * j + 1], u1, preferred_element_type=F32)

    width = S5_GB * C_STATE

    def step_power(r, rows):
        row = lambda ref: jnp.concatenate([ref[0, j, r:r + 1, :] for j in range(S5_PAIRS_PER_GB)], axis=1)
        return (jnp.broadcast_to(row(are_ref), (rows, width)), jnp.broadcast_to(row(aim_ref), (rows, width)))

    a_re, a_im = step_power(1, S5_SEGS)

    def body(i, carry):
        h_re, h_im = carry
        rows = pl.ds(pl.multiple_of(i * S5_SEGS, S5_SEGS), S5_SEGS)
        hre_ref[rows, :] = h_re
        him_ref[rows, :] = h_im
        e_re = ere_ref[rows, :]
        e_im = eim_ref[rows, :]
        return a_re * h_re - a_im * h_im + e_re, a_re * h_im + a_im * h_re + e_im

    zero = jnp.zeros((S5_SEGS, width), F32)
    end_re, end_im = lax.fori_loop(0, seg, body, (zero, zero))
    g_re, g_im = step_power(seg, 1)
    sub = lax.broadcasted_iota(jnp.int32, (S5_SEGS, 1), 0)
    c_re = c_im = zero
    p_re = p_im = jnp.zeros((1, width), F32)
    for s in range(1, S5_SEGS):
        p_re, p_im = (g_re * p_re - g_im * p_im + end_re[s - 1:s], g_re * p_im + g_im * p_re + end_im[s - 1:s])
        c_re, c_im = jnp.where(sub == s, p_re, c_re), jnp.where(sub == s, p_im, c_im)
    for i in range(seg):
        rows = slice(S5_SEGS * i, S5_SEGS * (i + 1))
        w_re, w_im = step_power(i, S5_SEGS)
        hre_ref[rows, :] += w_re * c_re - w_im * c_im
        him_ref[rows, :] += w_re * c_im + w_im * c_re

    for j in range(S5_PAIRS_PER_GB):
        sl = slice(LANES * j, LANES * (j + 1))
        hp = jnp.concatenate([hre_ref[:, sl], him_ref[:, sl]], axis=1).astype(BF16)
        yi = lax.dot_general(wyt_ref[0, j], hp, nt, preferred_element_type=F32)
        yt_ref[2 * j] += yi[:S5_TC]
        yt_ref[2 * j + 1] += yi[S5_TC:]
    for t in range(t_len):
        ytt = jnp.concatenate([yt_ref[g, ch * t:ch * (t + 1), :] for g in range(S5_GB)], axis=0)
        ybuf[slot, :, t] = (ytt.T + d_ref[...] * u_rows(t)).reshape(seg, S5_SEGS, LANES)
    for s in range(S5_SEGS):
        out_copy(step, slot, s).start()

    @pl.when(step == n_steps - 1)
    def _():
        for s in range(S5_SEGS):
            out_copy(step - 1, 1 - slot, s).wait()
            out_copy(step, slot, s).wait()


def _s5(cu, mt, wet, wyt, pw_re, pw_im, d_row, layer):
    p4 = 4 * C_STATE
    per_gb = lambda *s: pl.BlockSpec((1,) + s, lambda gb, b: (layer * S5_NGB + gb,) + (0,) * len(s))
    state = pltpu.VMEM((S5_NCHUNK, S5_GB * C_STATE), F32)
    tiles = pltpu.VMEM((2, S5_SEG_CHUNKS, S5_CHUNK, S5_SEGS, LANES), F32)
    sems = pltpu.SemaphoreType.DMA((2, S5_SEGS))
    by_chunk = (TOKENS // S5_CHUNK, S5_CHUNK, C_WIDTH)
    y = pl.pallas_call(
        _s5_kernel,
        out_shape=jax.ShapeDtypeStruct(by_chunk, F32),
        grid=(S5_NGB, BATCH),
        in_specs=[pl.BlockSpec(memory_space=pl.ANY),
                  per_gb(S5_GB, S5_TC, S5_TC),
                  per_gb(S5_PAIRS_PER_GB, p4, 2 * S5_TC),
                  per_gb(S5_PAIRS_PER_GB, 2 * S5_TC, p4),
                  per_gb(S5_PAIRS_PER_GB, S5_STEP_ROWS, 2 * C_STATE),
                  per_gb(S5_PAIRS_PER_GB, S5_STEP_ROWS, 2 * C_STATE),
                  pl.BlockSpec((1, LANES), lambda gb, b: (0, gb))],
        out_specs=pl.BlockSpec(memory_space=pl.ANY),
        scratch_shapes=[tiles, tiles, sems, sems,
                        pltpu.VMEM((S5_GB, S5_TC, S5_NCHUNK), F32),
                        pltpu.VMEM((S5_GB, S5_TC, S5_NCHUNK), F32),
                        state, state, state, state],
        compiler_params=_cparams(("arbitrary", "arbitrary")),
        name="s5",
    )(cu.reshape(by_chunk), mt, wet, wyt, pw_re, pw_im, d_row)
    return y.reshape(TOKENS, C_WIDTH)


def _gelu_tanh(x):
    return 0.5 * x * (1.0 + jnp.tanh(math.sqrt(2.0 / math.pi) * (x + 0.044715 * (x * x * x))))


def _out_kernel(sink_ref, qt_ref, kp_ref, kc_ref, vtp_ref, vtc_ref, ag_ref, bias_ref,
                ob_ref, yc_ref, cg_ref, x_ref, gate_ref, gpost_ref, wglu_ref, bglu_ref, wout_ref, o_ref, oa_ref):
    first_tile = pl.program_id(0) % (SEQ // OUT_TM) == 0
    blocks_per_part = OUT_TM // (A_BLOCK * OUT_PARTS)

    def project(part):
        r = slice(A_BLOCK * blocks_per_part * part, A_BLOCK * blocks_per_part * (part + 1))
        y = _gelu_tanh(yc_ref[r, :])
        z = jnp.dot(y.astype(BF16), wglu_ref[...], preferred_element_type=F32) + bglu_ref[...]
        y = y * jax.nn.sigmoid(z)
        oc = (y * _silu(cg_ref[r, :].astype(F32))).astype(BF16)
        mix = jnp.concatenate([oa_ref[r, :], ob_ref[r, :], oc], axis=1)
        acc = jnp.dot(mix, wout_ref[...], preferred_element_type=F32)
        ms = jnp.mean(acc * acc, axis=-1, keepdims=True)
        out = acc * lax.rsqrt(ms + EPS) * gpost_ref[...]
        o_ref[r, :] = x_ref[r, :] + gate_ref[0] * out

    def after_block(j):
        if (j + 1) % blocks_per_part == 0:
            project(j // blocks_per_part)

    _attention_steps(sink_ref, qt_ref, kp_ref, kc_ref, vtp_ref, vtc_ref, ag_ref, bias_ref, oa_ref, first_tile,
                     OUT_TM // A_BLOCK, blocks_per_part, after_block)


def _out(sinks, qt, k, vt, ag, bias, ob, yc, cg, x2, gate, g_post, w_glu, b_glu, w_out, layer):
    tm = OUT_TM
    steps_per_batch = SEQ // tm
    blocks = tm // A_BLOCK
    row = lambda i: (i, 0)
    col = lambda i: (0, i)
    const = lambda i: (0, 0)
    once = dict(pipeline_mode=pl.Buffered(1))
    prev = lambda i: (i * blocks - jnp.minimum(i % steps_per_batch, 1), 0)
    prev_t = lambda i: (0, i * blocks - jnp.minimum(i % steps_per_batch, 1))
    return pl.pallas_call(
        _out_kernel,
        out_shape=jax.ShapeDtypeStruct((TOKENS, D_MODEL), F32),
        grid=(TOKENS // tm,),
        in_specs=[pl.BlockSpec(memory_space=pltpu.SMEM),
                  pl.BlockSpec((A_WIDTH, tm), col),
                  pl.BlockSpec((A_BLOCK, A_KV_WIDTH), prev),
                  pl.BlockSpec((tm, A_KV_WIDTH), row),
                  pl.BlockSpec((A_KV_WIDTH, A_BLOCK), prev_t),
                  pl.BlockSpec((A_KV_WIDTH, tm), col),
                  pl.BlockSpec((tm, A_WIDTH), row),
                  pl.BlockSpec((2, A_Q_HEADS, A_BLOCK, A_BLOCK), lambda i: (0, 0, 0, 0), **once),
                  pl.BlockSpec((tm, B_WIDTH), row),
                  pl.BlockSpec((tm, C_WIDTH), row),
                  pl.BlockSpec((tm, C_WIDTH), row),
                  pl.BlockSpec((tm, D_MODEL), row),
                  pl.BlockSpec((1, 1, D_MODEL), lambda i: (i // steps_per_batch, 0, 0)),
                  pl.BlockSpec((1, D_MODEL), const),
                  pl.BlockSpec((C_WIDTH, C_WIDTH), lambda i: (layer, 0), **once),
                  pl.BlockSpec((1, C_WIDTH), const),
                  pl.BlockSpec((2 * D_MODEL, D_MODEL), lambda i: (layer, 0), **once)],
        out_specs=pl.BlockSpec((tm, D_MODEL), row),
        scratch_shapes=[pltpu.VMEM((tm, A_WIDTH), BF16)],
        compiler_params=_cparams(("arbitrary",)),
        name="out",
    )(sinks, qt, k, k, vt, vt, ag, bias, ob, yc, cg, x2, gate, g_post.reshape(1, D_MODEL), w_glu,
      b_glu.reshape(1, C_WIDTH), w_out)


def kernel(x, c, w_mod, b_mod, g_pre, g_post, w_in, attn_sinks, gla_w_alpha, gla_b_alpha, gla_norm_g,
           s5_a_re, s5_a_im, s5_log_dt, s5_b_re, s5_b_im, s5_c_re, s5_c_im, s5_d, s5_w_glu, s5_b_glu, w_out):
    layers = w_mod.shape[0]
    x2 = x.reshape(TOKENS, D_MODEL)
    bias = _attn_bias()
    mod = _mod(jnp.pad(c, ((0, 8 - BATCH), (0, 0))), w_mod, b_mod)[:, :BATCH]
    shift, scale, gate = (m.reshape(layers, BATCH, 1, D_MODEL) for m in jnp.split(mod, 3, axis=-1))
    w_t = jnp.swapaxes(w_in, 1, 2).astype(BF16).reshape(layers * W_IN_COLS, D_MODEL)
    w_alpha = gla_w_alpha.astype(BF16)
    s5_ops = _s5prep(s5_a_re, s5_a_im, s5_log_dt, s5_b_re, s5_b_im, s5_c_re, s5_c_im, s5_d)
    *s5_ops, s5_d_rows = s5_ops
    w_glu = s5_w_glu.astype(BF16).reshape(layers * C_WIDTH, C_WIDTH)
    w_out_b = w_out.astype(BF16).reshape(layers * 2 * D_MODEL, D_MODEL)
    for l in range(layers):
        ak, ag, bq, bk, bv, bg, cu, cg, log_a, aqt, avt = _proj(
            x2, scale[l], shift[l], g_pre[l], w_t, l, w_alpha[l], gla_b_alpha[l])
        o_b = _gla(log_a, bq, bk, bv, bg, gla_norm_g[l])
        y_c = _s5(cu, *s5_ops, s5_d_rows[l], l)
        x2 = _out(attn_sinks[l], aqt, ak, avt, ag, bias, o_b, y_c, cg, x2, gate[l], g_post[l], w_glu, s5_b_glu[l],
                  w_out_b, l)
    return x2.reshape(x.shape)
```

```python
import math

import jax
import jax.numpy as jnp
import numpy as np
from jax import lax
from jax.experimental import pallas as pl
from jax.experimental.pallas import tpu as pltpu

F32 = jnp.float32
BF16 = jnp.bfloat16
HIGHEST = lax.Precision.HIGHEST

D_MODEL = 1024
BATCH = 4
SEQ = 4096
TOKENS = BATCH * SEQ
EPS = 1e-6

A_WIDTH = 1024
A_HEAD_DIM = 64
A_Q_HEADS = 16
A_KV_HEADS = 4
A_KV_WIDTH = A_KV_HEADS * A_HEAD_DIM
A_BLOCK = 128
WINDOW = 128
assert WINDOW == A_BLOCK

B_WIDTH = 512
B_HEADS = 4
B_DK = 64
B_DV = 128
B_QK_WIDTH = 256
B_GATE_RANK = 16
B_GATE_TAU = 16.0
GLA_BLOCK = 64
GLA_SUB = 8
GLA_DIAG = 16
GLA_ANCHOR_SEGS = (32, 64)
assert GLA_ANCHOR_SEGS[0] == 2 * GLA_DIAG and GLA_ANCHOR_SEGS[-1] == GLA_BLOCK
OUT_PARTS = 2

C_WIDTH = 512
C_GROUP_CH = 16
C_GROUPS = 32
C_STATE = 64
S5_CHUNK = 16
S5_NCHUNK = SEQ // S5_CHUNK
S5_TC = S5_CHUNK * C_GROUP_CH
S5_GB = 8
S5_NGB = C_GROUPS // S5_GB
S5_PAIRS_PER_GB = S5_GB // 2
S5_POW_ROWS = 24
S5PREP_PAIRS = 4
SUBLANES = 8
S5_SEGS = SUBLANES
S5_SEG_CHUNKS = S5_NCHUNK // S5_SEGS
S5_STEP_ROWS = S5_SEG_CHUNKS + SUBLANES

LANES = 128

V7X_VMEM_LIMIT = 56 * 1024 * 1024

PROJ_TM = 1024
OUT_TM = 1024

_W_IN_SIZES = (("aq", A_WIDTH), ("ak", A_KV_WIDTH), ("av", A_KV_WIDTH), ("ag", A_WIDTH), ("bq", B_QK_WIDTH),
               ("bk", B_QK_WIDTH), ("bv", B_WIDTH), ("blr", B_GATE_RANK), ("bg", B_WIDTH), ("cu", C_WIDTH),
               ("cg", C_WIDTH))
_W_IN_OFF = {}
_off = 0
for _name, _w in _W_IN_SIZES:
    _W_IN_OFF[_name] = (_off, _w)
    _off += _w
W_IN_COLS = _off
_PROJ_OUTS = (("ak", BF16), ("ag", BF16), ("bq", BF16), ("bk", BF16),
              ("bv", BF16), ("bg", BF16), ("cu", F32), ("cg", BF16))


def _silu(x):
    return x * jax.nn.sigmoid(x)


def _cparams(semantics):
    return pltpu.CompilerParams(dimension_semantics=semantics, vmem_limit_bytes=V7X_VMEM_LIMIT)


def _mod_kernel(c_ref, w_ref, b_ref, o_ref):
    c = c_ref[...]
    o_ref[0] = jnp.dot(_silu(c).astype(BF16), w_ref[0].astype(BF16), preferred_element_type=F32) + b_ref[0]


def _mod(c_pad, w_mod, b_mod):
    layers = w_mod.shape[0]
    n = 3 * D_MODEL
    tn = 768
    return pl.pallas_call(
        _mod_kernel,
        out_shape=jax.ShapeDtypeStruct((layers, 8, n), F32),
        grid=(layers, n // tn),
        in_specs=[pl.BlockSpec((8, D_MODEL), lambda l, j: (0, 0)),
                  pl.BlockSpec((1, D_MODEL, tn), lambda l, j: (l, 0, j)),
                  pl.BlockSpec((1, 1, tn), lambda l, j: (l, 0, j))],
        out_specs=pl.BlockSpec((1, 8, tn), lambda l, j: (l, 0, j)),
        compiler_params=_cparams(("arbitrary", "arbitrary")),
        name="mod",
    )(c_pad, w_mod, b_mod.reshape(layers, 1, n))


def _proj_kernel(x_ref, scale_ref, shift_ref, gpre_ref, wt_ref, walpha_ref, balpha_ref, *out_refs):
    x = x_ref[...]
    ms = jnp.mean(x * x, axis=-1, keepdims=True)
    y = x * lax.rsqrt(ms + EPS) * gpre_ref[...]
    h = (y * (1.0 + scale_ref[0]) + shift_ref[0]).astype(BF16)
    nt = (((1,), (1,)), ((), ()))

    def rows(name):
        off, w = _W_IN_OFF[name]
        return wt_ref[off:off + w, :]

    for (name, _), o_ref in zip(_PROJ_OUTS, out_refs):
        o_ref[...] = lax.dot_general(h, rows(name), nt, preferred_element_type=F32).astype(o_ref.dtype)
    la_ref, qt_ref, vt_ref = out_refs[len(_PROJ_OUTS):]
    qlr = lax.dot_general(jnp.concatenate([rows("aq"), rows("blr")], axis=0), h, nt, preferred_element_type=F32)
    qt_ref[...] = (qlr[:A_WIDTH] * (A_HEAD_DIM ** -0.5)).astype(qt_ref.dtype)
    vt_ref[...] = lax.dot_general(rows("av"), h, nt, preferred_element_type=F32).astype(vt_ref.dtype)
    lr_t = qlr[A_WIDTH:].astype(BF16)
    logits = lax.dot_general(lr_t, walpha_ref[...], (((0,), (0,)), ((), ())),
                             preferred_element_type=F32) + balpha_ref[...]
    log_sig = jnp.minimum(logits, 0.0) - jnp.log(1.0 + jnp.exp(-jnp.abs(logits)))
    la_ref[...] = log_sig * (math.log2(math.e) / B_GATE_TAU)


def _proj(x2, scale, shift, g_pre, w_t, layer, w_alpha, b_alpha):
    tm = PROJ_TM
    steps_per_batch = SEQ // tm
    row = lambda i: (i, 0)
    col = lambda i: (0, i)
    per_batch = lambda i: (i // steps_per_batch, 0, 0)
    const = lambda i: (0, 0)
    out_shape = [jax.ShapeDtypeStruct((TOKENS, _W_IN_OFF[n][1]), dt) for n, dt in _PROJ_OUTS]
    out_specs = [pl.BlockSpec((tm, _W_IN_OFF[n][1]), row) for n, _ in _PROJ_OUTS]
    out_shape += [jax.ShapeDtypeStruct((TOKENS, B_QK_WIDTH), F32),
                  jax.ShapeDtypeStruct((A_WIDTH, TOKENS), BF16),
                  jax.ShapeDtypeStruct((A_KV_WIDTH, TOKENS), BF16)]
    out_specs += [pl.BlockSpec((tm, B_QK_WIDTH), row),
                  pl.BlockSpec((A_WIDTH, tm), col),
                  pl.BlockSpec((A_KV_WIDTH, tm), col)]
    return pl.pallas_call(
        _proj_kernel,
        out_shape=out_shape,
        grid=(TOKENS // tm,),
        in_specs=[pl.BlockSpec((tm, D_MODEL), row),
                  pl.BlockSpec((1, 1, D_MODEL), per_batch),
                  pl.BlockSpec((1, 1, D_MODEL), per_batch),
                  pl.BlockSpec((1, D_MODEL), const),
                  pl.BlockSpec((W_IN_COLS, D_MODEL), lambda i: (layer, 0), pipeline_mode=pl.Buffered(1)),
                  pl.BlockSpec((B_GATE_RANK, B_QK_WIDTH), const),
                  pl.BlockSpec((1, B_QK_WIDTH), const)],
        out_specs=out_specs,
        compiler_params=_cparams(("arbitrary",)),
        name="proj",
    )(x2, scale, shift, g_pre.reshape(1, D_MODEL), w_t, w_alpha, b_alpha.reshape(1, B_QK_WIDTH))


def _attn_bias():
    j = np.arange(A_BLOCK)[:, None]
    i = np.arange(A_BLOCK)[None, :]
    dist = np.where(j > i, i + A_BLOCK - j, i - j).astype(np.float32)
    slopes = np.exp2(-8.0 * np.arange(1, A_Q_HEADS + 1, dtype=np.float32) / A_Q_HEADS).astype(np.float32)
    bias = -slopes[:, None, None] * dist[None]
    first = np.where((j > i)[None], -np.inf, bias).astype(np.float32)
    return jnp.asarray(np.stack([bias, first]))


def _attention_steps(sink_ref, qt_ref, kp_ref, kc_ref, vtp_ref, vtc_ref, g_ref, bias_ref, oa_ref, first_tile,
                     n_blocks, group_blocks, after_block):
    kj = lax.broadcasted_iota(jnp.int32, (A_BLOCK, A_BLOCK), 0)
    qi = lax.broadcasted_iota(jnp.int32, (A_BLOCK, A_BLOCK), 1)
    from_prev = kj > qi
    zero_rows = jnp.zeros((A_HEAD_DIM, A_BLOCK), BF16)
    group = A_Q_HEADS // A_KV_HEADS
    blk = lambda j: slice(A_BLOCK * j, A_BLOCK * (j + 1))

    def keys(j, sl):
        prev = kp_ref[:, sl] if j == 0 else kc_ref[blk(j - 1), sl]
        return prev, kc_ref[blk(j), sl]

    def values(j, rows):
        prev = vtp_ref[rows, :] if j == 0 else vtc_ref[rows, blk(j - 1)]
        return jnp.concatenate([prev, vtc_ref[rows, blk(j)]], axis=1)

    def scores(j, hd):
        kvh = hd // group
        sl = slice(LANES * (kvh // 2), LANES * (kvh // 2 + 1))
        qh = qt_ref[A_HEAD_DIM * hd:A_HEAD_DIM * (hd + 1), blk(j)]
        qsel = jnp.concatenate([qh, zero_rows] if kvh % 2 == 0 else [zero_rows, qh], axis=0)
        k_prev, k_cur = keys(j, sl)
        return (jnp.dot(k_prev, qsel, preferred_element_type=F32),
                jnp.dot(k_cur, qsel, preferred_element_type=F32))

    def attend(j, hd, s_prev, s_cur):
        kvh = hd // group
        v_both = values(j, slice(A_HEAD_DIM * kvh, A_HEAD_DIM * (kvh + 1)))
        table = jnp.where(first_tile, 1, 0) if j == 0 else 0
        s = jnp.where(from_prev, s_prev, s_cur) + bias_ref[table, hd]
        sink = sink_ref[hd]
        m = jnp.maximum(jnp.max(s, axis=0, keepdims=True), sink)
        p = jnp.exp(s - m)
        den = jnp.sum(p, axis=0, keepdims=True) + jnp.exp(sink - m)
        p_both = jnp.concatenate([jnp.where(from_prev, p, 0.0), jnp.where(from_prev, 0.0, p)],
                                 axis=0).astype(BF16)
        return jnp.dot(v_both, p_both, preferred_element_type=F32) / den

    pending = {}

    def issue_scores(first):
        for j in range(first, min(first + group_blocks, n_blocks)):
            for hd in range(A_Q_HEADS):
                pending[j, hd] = scores(j, hd)

    issue_scores(0)
    for j in range(n_blocks):
        if (j + 1) % group_blocks == 0:
            issue_scores(j + 1)
        outs = {}
        for hd in range(A_Q_HEADS):
            outs[hd] = attend(j, hd, *pending.pop((j, hd)))
            if hd % 2 == 1:
                qsl = slice(LANES * (hd // 2), LANES * (hd // 2 + 1))
                o_pair = jnp.concatenate([outs.pop(hd - 1), outs.pop(hd)], axis=0).T
                gate = g_ref[blk(j), qsl].astype(F32)
                oa_ref[blk(j), qsl] = (o_pair * _silu(gate)).astype(oa_ref.dtype)
        after_block(j)


def _gla_kernel(la_ref, q_ref, k_ref, v_ref, g_ref, gn_ref, o_ref, st_ref):
    cb = GLA_BLOCK

    @pl.when(pl.program_id(0) == 0)
    def _():
        st_ref[...] = jnp.zeros_like(st_ref)

    r = lax.broadcasted_iota(jnp.int32, (cb, cb), 0)
    c = lax.broadcasted_iota(jnp.int32, (cb, cb), 1)
    tri = (c <= r).astype(BF16)
    lane = lax.broadcasted_iota(jnp.int32, (1, B_QK_WIDTH), 1)
    head_masks = [(lane >= B_DK * h) & (lane < B_DK * (h + 1)) for h in range(B_HEADS)]
    rr = lax.broadcasted_iota(jnp.int32, (B_HEADS * cb, cb), 0)
    cc = lax.broadcasted_iota(jnp.int32, (B_HEADS * cb, cb), 1)
    ri = rr & (cb - 1)
    nt = (((1,), (1,)), ((), ()))
    tn = (((0,), (0,)), ((), ()))
    rows = lambda u: slice(cb * u, cb * (u + 1))
    items = [(u, b) for u in range(GLA_SUB) for b in range(BATCH)]
    def cumsum(la):
        hi = la.astype(BF16)
        r1 = la - hi.astype(F32)
        mid = r1.astype(BF16)
        lo = (r1 - mid.astype(F32)).astype(BF16)
        parts = jnp.dot(tri, jnp.concatenate([hi, mid, lo], axis=1), preferred_element_type=F32)
        w = B_QK_WIDTH
        return parts[:, :w] + (parts[:, w:2 * w] + parts[:, 2 * w:])

    bcs = {(u, b): cumsum(la_ref[b, rows(u), :]) for u, b in items}
    stack_heads = lambda a: jnp.concatenate([jnp.where(m, a, 0.0) for m in head_masks], axis=0).astype(BF16)
    row = lax.broadcasted_iota(jnp.int32, (cb, 1), 0)
    diag = ((ri ^ cc) < GLA_DIAG) & (cc <= ri)
    same_seg = {seg: (ri ^ cc) < seg for seg in GLA_ANCHOR_SEGS}

    def upper(a, seg):
        return jnp.concatenate([a[s0 + seg // 2:s0 + seg] for s0 in range(0, cb, seg)], axis=0)

    def place_upper(p, seg):
        half = seg // 2
        zero = jnp.zeros((half, cb), F32)
        parts = []
        for piece in range(B_HEADS * cb // seg):
            parts += [zero, p[half * piece:half * (piece + 1)]]
        return jnp.concatenate(parts, axis=0)

    def anchor_rows(bc, seg, offset, reps):
        parts = []
        for s0 in range(0, cb, seg):
            a = s0 + offset - 1
            val = bc[a:a + 1, :] if a >= 0 else jnp.zeros((1, B_QK_WIDTH), F32)
            parts.append(jnp.broadcast_to(val, (reps, B_QK_WIDTH)))
        return jnp.concatenate(parts, axis=0)

    q_lv, k_lv, qsts, ksts, decs, vs = {}, {}, {}, {}, {}, {}
    for it in items:
        u, b = it
        bc = bcs[it]
        bl = bc[cb - 1:cb, :]
        q = q_ref[b, rows(u), :].astype(F32) * (B_DK ** -0.5)
        k = k_ref[b, rows(u), :].astype(F32)
        for seg in GLA_ANCHOR_SEGS:
            half = seg // 2
            q_lv[it, seg] = stack_heads(upper(q, seg) * jnp.exp2(upper(bc, seg) - anchor_rows(bc, seg, half, half)))
            in_lower = (row & (seg - 1)) < half
            k_lv[it, seg] = jnp.where(in_lower, k * jnp.exp2(anchor_rows(bc, seg, half, seg) - bc), 0.0).astype(BF16)
        anc = anchor_rows(bc, GLA_DIAG, 0, GLA_DIAG)
        q_lv[it, 0] = stack_heads(q * jnp.exp2(bc - anc))
        k_lv[it, 0] = (k * jnp.exp2(anc - bc)).astype(BF16)
        qsts[it] = stack_heads(q * jnp.exp2(bc))
        ksts[it] = stack_heads(k * jnp.exp2(bl - bc))
        decs[it] = jnp.exp2(bl)
        vs[it] = v_ref[b, rows(u), :]
    a_alls = {}
    for it in items:
        prod = lambda lv: lax.dot_general(q_lv[it, lv], k_lv[it, lv], nt, preferred_element_type=F32)
        a = jnp.where(diag, prod(0), 0.0)
        for seg in GLA_ANCHOR_SEGS:
            a = a + jnp.where(same_seg[seg], place_upper(prod(seg), seg), 0.0)
        a_alls[it] = a.astype(BF16)
    upds = {}
    for it in items:
        vst = jnp.concatenate([vs[it][:, B_DV * h:B_DV * (h + 1)] for h in range(B_HEADS)], axis=0)
        upds[it] = lax.dot_general(vst, ksts[it], tn, preferred_element_type=F32)
    st_in = {}
    for b in range(BATCH):
        st = st_ref[b]
        for u in range(GLA_SUB):
            st_in[u, b] = st
            st = st * decs[u, b] + upds[u, b]
        st_ref[b] = st
    oi_alls = {it: lax.dot_general(qsts[it], st_in[it].astype(BF16), nt, preferred_element_type=F32)
               for it in items}
    o_hs = {}
    for it in items:
        a_all = a_alls[it]
        for h in range(B_HEADS):
            o_hs[it, h] = (jnp.dot(a_all[cb * h:cb * (h + 1)], vs[it][:, B_DV * h:B_DV * (h + 1)],
                                   preferred_element_type=F32) + oi_alls[it][cb * h:cb * (h + 1)])
    for it in items:
        u, b = it
        for h in range(B_HEADS):
            vsl = slice(B_DV * h, B_DV * (h + 1))
            o_h = o_hs[it, h]
            ms = jnp.mean(o_h * o_h, axis=-1, keepdims=True)
            o_n = o_h * lax.rsqrt(ms + EPS) * gn_ref[:, vsl]
            gate = g_ref[b, rows(u), vsl].astype(F32)
            o_ref[b, rows(u), vsl] = (o_n * _silu(gate)).astype(o_ref.dtype)


def _gla(log_a, bq, bk, bv, bg, g_gla):
    cb = GLA_BLOCK * GLA_SUB
    blk = lambda w: pl.BlockSpec((BATCH, cb, w), lambda i: (0, i, 0))
    r3 = lambda a: a.reshape(BATCH, SEQ, a.shape[-1])
    out = pl.pallas_call(
        _gla_kernel,
        out_shape=jax.ShapeDtypeStruct((BATCH, SEQ, B_WIDTH), BF16),
        grid=(SEQ // cb,),
        in_specs=[blk(B_QK_WIDTH), blk(B_QK_WIDTH), blk(B_QK_WIDTH), blk(B_WIDTH), blk(B_WIDTH),
                  pl.BlockSpec((1, B_WIDTH), lambda i: (0, 0))],
        out_specs=blk(B_WIDTH),
        scratch_shapes=[pltpu.VMEM((BATCH, B_DV, B_QK_WIDTH), F32)],
        compiler_params=_cparams(("arbitrary",)),
        name="gla",
    )(r3(log_a), r3(bq), r3(bk), r3(bv), r3(bg), g_gla.reshape(1, B_WIDTH))
    return out.reshape(TOKENS, B_WIDTH)


def _s5prep_kernel(ar_ref, ai_ref, ldt_ref, bre_ref, bim_ref, btre_ref, btim_ref, cre_ref, cim_ref,
                   mt_ref, wet_ref, wyt_ref, are_ref, aim_ref):
    p = C_STATE
    for pr in range(S5PREP_PAIRS):
        wet_re, wet_im, wyt_rows, a_re, a_im = [], [], [], [], []
        for g in range(2):
            i = 2 * pr + g
            kk, e_re, e_im, y_re, y_im, p_re, p_im = _s5_discretise(
                ar_ref[i], ai_ref[i], ldt_ref[i], bre_ref[i], bim_ref[i], btre_ref[i], btim_ref[i],
                cre_ref[i], cim_ref[i])
            pieces = [kk] + [jnp.concatenate([jnp.zeros((C_GROUP_CH * s, C_GROUP_CH), F32),
                                              kk[:S5_TC - C_GROUP_CH * s]], axis=0) for s in range(1, S5_CHUNK)]
            mt_ref[pr, g] = jnp.concatenate(pieces, axis=1).astype(BF16)
            e_t = jnp.concatenate([e_re, e_im], axis=1).T
            zero = jnp.zeros((p, S5_TC), F32)
            wet_re.append(jnp.concatenate([e_t[:p], zero] if g == 0 else [zero, e_t[:p]], axis=1))
            wet_im.append(jnp.concatenate([e_t[p:], zero] if g == 0 else [zero, e_t[p:]], axis=1))
            zero = jnp.zeros((S5_TC, p), F32)
            wyt_rows.append(
                jnp.concatenate([y_re, zero, -y_im, zero] if g == 0 else [zero, y_re, zero, -y_im], axis=1))
            a_re.append(p_re)
            a_im.append(p_im)
        wet_ref[pr] = jnp.concatenate(wet_re + wet_im, axis=0).astype(BF16)
        wyt_ref[pr] = jnp.concatenate(wyt_rows, axis=0).astype(BF16)
        r = lax.broadcasted_iota(jnp.int32, (S5_STEP_ROWS, 1), 0)
        f_re, f_im = jnp.concatenate(a_re, axis=1), jnp.concatenate(a_im, axis=1)
        t_re, t_im = jnp.ones((S5_STEP_ROWS, 2 * p), F32), jnp.zeros((S5_STEP_ROWS, 2 * p), F32)
        for k in range(S5_SEG_CHUNKS.bit_length()):
            s_re, s_im = jnp.where((r >> k) & 1 == 1, f_re, 1.0), jnp.where((r >> k) & 1 == 1, f_im, 0.0)
            t_re, t_im = t_re * s_re - t_im * s_im, t_re * s_im + t_im * s_re
            f_re, f_im = f_re * f_re - f_im * f_im, 2.0 * (f_re * f_im)
        are_ref[pr] = t_re
        aim_ref[pr] = t_im


def _s5_discretise(ar, ai, ldt, b_re, b_im, bt_re16, bt_im16, c_re16, c_im16):
    dt = jnp.exp(ldt)

    def cmul(xr, xi, yr, yi):
        return xr * yr - xi * yi, xr * yi + xi * yr

    kf = lax.broadcasted_iota(jnp.int32, (S5_POW_ROWS, 1), 0).astype(F32)
    mag = jnp.exp(kf * (ar * dt))
    ang = kf * (ai * dt)
    pw_re, pw_im = mag * jnp.cos(ang), mag * jnp.sin(ang)
    abar_re, abar_im = pw_re[1:2], pw_im[1:2]
    den = ar * ar + ai * ai
    num_re = abar_re - 1.0
    f_re = (num_re * ar + abar_im * ai) / den
    f_im = (abar_im * ar - num_re * ai) / den
    g_re, g_im = cmul(pw_re, pw_im, f_re, f_im)

    def pick(which, xr, xi):
        rep = lambda x: jnp.concatenate(
            [jnp.broadcast_to(x[which(i):which(i) + 1], (C_GROUP_CH, C_STATE)) for i in range(S5_CHUNK)], axis=0)
        return rep(xr), rep(xi)

    tile16 = lambda a: jnp.concatenate([a] * S5_CHUNK, axis=0)
    ct_re, ct_im = tile16(c_re16), tile16(c_im16)
    bt_re, bt_im = tile16(bt_re16), tile16(bt_im16)

    w_re, w_im = cmul(*pick(lambda i: i, g_re, g_im), ct_re, ct_im)
    kk = (jnp.dot(w_re, b_re, preferred_element_type=F32, precision=HIGHEST)
          - jnp.dot(w_im, b_im, preferred_element_type=F32, precision=HIGHEST))
    e_re, e_im = cmul(*pick(lambda i: S5_CHUNK - 1 - i, g_re, g_im), bt_re, bt_im)
    y_re, y_im = cmul(*pick(lambda i: i + 1, pw_re, pw_im), ct_re, ct_im)
    return kk, e_re, e_im, y_re, y_im, pw_re[S5_CHUNK:S5_CHUNK + 1], pw_im[S5_CHUNK:S5_CHUNK + 1]


def _s5prep(a_re, a_im, log_dt, b_re, b_im, c_re, c_im, d):
    p, ch = C_STATE, C_GROUP_CH
    layers = a_re.shape[0]
    g = layers * C_GROUPS
    npair = g // 2
    flat = lambda a: a.reshape(g, *a.shape[2:])
    a_re, a_im, log_dt, b_re, b_im, c_re, c_im = map(flat, (a_re, a_im, log_dt, b_re, b_im, c_re, c_im))
    row = lambda a: a.reshape(g, 1, p)
    ldt = jnp.broadcast_to(log_dt[:, None, None], (g, 1, p))
    b_t = lambda a: jnp.swapaxes(a, 1, 2)
    pps = S5PREP_PAIRS
    spec = lambda s1, s2: pl.BlockSpec((2 * pps, s1, s2), lambda i: (i, 0, 0))
    mt, wet, wyt, pw_re, pw_im = pl.pallas_call(
        _s5prep_kernel,
        out_shape=[jax.ShapeDtypeStruct((npair, 2, S5_TC, S5_TC), BF16),
                   jax.ShapeDtypeStruct((npair, 4 * p, 2 * S5_TC), BF16),
                   jax.ShapeDtypeStruct((npair, 2 * S5_TC, 4 * p), BF16),
                   jax.ShapeDtypeStruct((npair, S5_STEP_ROWS, 2 * p), F32),
                   jax.ShapeDtypeStruct((npair, S5_STEP_ROWS, 2 * p), F32)],
        grid=(npair // pps,),
        in_specs=[spec(1, p), spec(1, p), spec(1, p), spec(p, ch), spec(p, ch),
                  spec(ch, p), spec(ch, p), spec(ch, p), spec(ch, p)],
        out_specs=[pl.BlockSpec((pps, 2, S5_TC, S5_TC), lambda i: (i, 0, 0, 0)),
                   pl.BlockSpec((pps, 4 * p, 2 * S5_TC), lambda i: (i, 0, 0)),
                   pl.BlockSpec((pps, 2 * S5_TC, 4 * p), lambda i: (i, 0, 0)),
                   pl.BlockSpec((pps, S5_STEP_ROWS, 2 * p), lambda i: (i, 0, 0)),
                   pl.BlockSpec((pps, S5_STEP_ROWS, 2 * p), lambda i: (i, 0, 0))],
        compiler_params=_cparams(("arbitrary",)),
        name="s5prep",
    )(row(a_re), row(a_im), ldt, b_re, b_im, b_t(b_re), b_t(b_im), c_re, c_im)
    by_gb = lambda a: a.reshape(layers * S5_NGB, S5_PAIRS_PER_GB, *a.shape[1:])
    return (mt.reshape(layers * S5_NGB, S5_GB, S5_TC, S5_TC), by_gb(wet), by_gb(wyt), by_gb(pw_re), by_gb(pw_im),
            d.reshape(layers, 1, C_WIDTH))


def _s5_kernel(u_hbm, mt_ref, wet_ref, wyt_ref, are_ref, aim_ref, d_ref, y_hbm,
               ubuf, ybuf, in_sem, out_sem, ut_ref, yt_ref, ere_ref, eim_ref, hre_ref, him_ref):
    nck, t_len, ch, seg = S5_NCHUNK, S5_CHUNK, C_GROUP_CH, S5_SEG_CHUNKS
    nt = (((1,), (1,)), ((), ()))
    n_steps = S5_NGB * BATCH
    step = pl.program_id(0) * BATCH + pl.program_id(1)
    slot = step % 2

    def hbm_tile(ref, step_, s):
        gb_, b_ = step_ // BATCH, step_ % BATCH
        return ref.at[pl.ds(b_ * nck + s * seg, seg), :, pl.ds(gb_ * LANES, LANES)]

    def in_copy(step_, slot_, s):
        return pltpu.make_async_copy(hbm_tile(u_hbm, step_, s), ubuf.at[slot_, :, :, s, :], in_sem.at[slot_, s])

    def out_copy(step_, slot_, s):
        return pltpu.make_async_copy(ybuf.at[slot_, :, :, s, :], hbm_tile(y_hbm, step_, s), out_sem.at[slot_, s])

    @pl.when(step == 0)
    def _():
        for s in range(S5_SEGS):
            in_copy(step, slot, s).start()

    @pl.when(step + 1 < n_steps)
    def _():
        for s in range(S5_SEGS):
            in_copy(step + 1, 1 - slot, s).start()

    @pl.when(step >= 2)
    def _():
        for s in range(S5_SEGS):
            out_copy(step - 2, slot, s).wait()

    for s in range(S5_SEGS):
        in_copy(step, slot, s).wait()
    u_rows = lambda t: ubuf[slot, :, t].reshape(nck, LANES)
    for t in range(t_len):
        xt = u_rows(t).T
        for g in range(S5_GB):
            ut_ref[g, ch * t:ch * (t + 1), :] = xt[ch * g:ch * (g + 1), :]
    for j in range(S5_PAIRS_PER_GB):
        u0 = ut_ref[2 * j].astype(BF16)
        u1 = ut_ref[2 * j + 1].astype(BF16)
        et = jnp.dot(wet_ref[0, j], jnp.concatenate([u0, u1], axis=0), preferred_element_type=F32)
        e = et.T
        ere_ref[:, LANES * j:LANES * (j + 1)] = e[:, :LANES]
        eim_ref[:, LANES * j:LANES * (j + 1)] = e[:, LANES:]
        yt_ref[2 * j] = jnp.dot(mt_ref[0, 2 * j], u0, preferred_element_type=F32)
        yt_ref[2 * j + 1] = jnp.dot(mt_ref[0, 2 * j + 1], u1, preferred_element_type=F32)

    width = S5_GB * C_STATE

    def step_power(r, rows):
        row = lambda ref: jnp.concatenate([ref[0, j, r:r + 1, :] for j in range(S5_PAIRS_PER_GB)], axis=1)
        return (jnp.broadcast_to(row(are_ref), (rows, width)), jnp.broadcast_to(row(aim_ref), (rows, width)))

    a_re, a_im = step_power(1, S5_SEGS)

    def body(i, carry):
        h_re, h_im = carry
        rows = pl.ds(pl.multiple_of(i * S5_SEGS, S5_SEGS), S5_SEGS)
        hre_ref[rows, :] = h_re
        him_ref[rows, :] = h_im
        e_re = ere_ref[rows, :]
        e_im = eim_ref[rows, :]
        return a_re * h_re - a_im * h_im + e_re, a_re * h_im + a_im * h_re + e_im

    zero = jnp.zeros((S5_SEGS, width), F32)
    end_re, end_im = lax.fori_loop(0, seg, body, (zero, zero))
    g_re, g_im = step_power(seg, 1)
    sub = lax.broadcasted_iota(jnp.int32, (S5_SEGS, 1), 0)
    c_re = c_im = zero
    p_re = p_im = jnp.zeros((1, width), F32)
    for s in range(1, S5_SEGS):
        p_re, p_im = (g_re * p_re - g_im * p_im + end_re[s - 1:s], g_re * p_im + g_im * p_re + end_im[s - 1:s])
        c_re, c_im = jnp.where(sub == s, p_re, c_re), jnp.where(sub == s, p_im, c_im)
    for i in range(seg):
        rows = slice(S5_SEGS * i, S5_SEGS * (i + 1))
        w_re, w_im = step_power(i, S5_SEGS)
        hre_ref[rows, :] += w_re * c_re - w_im * c_im
        him_ref[rows, :] += w_re * c_im + w_im * c_re

    for j in range(S5_PAIRS_PER_GB):
        sl = slice(LANES * j, LANES * (j + 1))
        hp = jnp.concatenate([hre_ref[:, sl], him_ref[:, sl]], axis=1).astype(BF16)
        yi = lax.dot_general(wyt_ref[0, j], hp, nt, preferred_element_type=F32)
        yt_ref[2 * j] += yi[:S5_TC]
        yt_ref[2 * j + 1] += yi[S5_TC:]
    for t in range(t_len):
        ytt = jnp.concatenate([yt_ref[g, ch * t:ch * (t + 1), :] for g in range(S5_GB)], axis=0)
        ybuf[slot, :, t] = (ytt.T + d_ref[...] * u_rows(t)).reshape(seg, S5_SEGS, LANES)
    for s in range(S5_SEGS):
        out_copy(step, slot, s).start()

    @pl.when(step == n_steps - 1)
    def _():
        for s in range(S5_SEGS):
            out_copy(step - 1, 1 - slot, s).wait()
            out_copy(step, slot, s).wait()


def _s5(cu, mt, wet, wyt, pw_re, pw_im, d_row, layer):
    p4 = 4 * C_STATE
    per_gb = lambda *s: pl.BlockSpec((1,) + s, lambda gb, b: (layer * S5_NGB + gb,) + (0,) * len(s))
    state = pltpu.VMEM((S5_NCHUNK, S5_GB * C_STATE), F32)
    tiles = pltpu.VMEM((2, S5_SEG_CHUNKS, S5_CHUNK, S5_SEGS, LANES), F32)
    sems = pltpu.SemaphoreType.DMA((2, S5_SEGS))
    by_chunk = (TOKENS // S5_CHUNK, S5_CHUNK, C_WIDTH)
    y = pl.pallas_call(
        _s5_kernel,
        out_shape=jax.ShapeDtypeStruct(by_chunk, F32),
        grid=(S5_NGB, BATCH),
        in_specs=[pl.BlockSpec(memory_space=pl.ANY),
                  per_gb(S5_GB, S5_TC, S5_TC),
                  per_gb(S5_PAIRS_PER_GB, p4, 2 * S5_TC),
                  per_gb(S5_PAIRS_PER_GB, 2 * S5_TC, p4),
                  per_gb(S5_PAIRS_PER_GB, S5_STEP_ROWS, 2 * C_STATE),
                  per_gb(S5_PAIRS_PER_GB, S5_STEP_ROWS, 2 * C_STATE),
                  pl.BlockSpec((1, LANES), lambda gb, b: (0, gb))],
        out_specs=pl.BlockSpec(memory_space=pl.ANY),
        scratch_shapes=[tiles, tiles, sems, sems,
                        pltpu.VMEM((S5_GB, S5_TC, S5_NCHUNK), F32),
                        pltpu.VMEM((S5_GB, S5_TC, S5_NCHUNK), F32),
                        state, state, state, state],
        compiler_params=_cparams(("arbitrary", "arbitrary")),
        name="s5",
    )(cu.reshape(by_chunk), mt, wet, wyt, pw_re, pw_im, d_row)
    return y.reshape(TOKENS, C_WIDTH)


def _gelu_tanh(x):
    c = math.sqrt(2.0 / math.pi)
    return 0.5 * x * (1.0 + jnp.tanh(x * (c + (c * 0.044715) * (x * x))))


def _out_kernel(sink_ref, qt_ref, kp_ref, kc_ref, vtp_ref, vtc_ref, ag_ref, bias_ref,
                ob_ref, yc_ref, cg_ref, x_ref, gate_ref, gpost_ref, wglu_ref, bglu_ref, wout_ref, o_ref, oa_ref):
    first_tile = pl.program_id(0) % (SEQ // OUT_TM) == 0
    blocks_per_part = OUT_TM // (A_BLOCK * OUT_PARTS)
    gain = gpost_ref[...] * gate_ref[0]

    def project(part):
        r = slice(A_BLOCK * blocks_per_part * part, A_BLOCK * blocks_per_part * (part + 1))
        y = _gelu_tanh(yc_ref[r, :])
        z = jnp.dot(y.astype(BF16), wglu_ref[...], preferred_element_type=F32) + bglu_ref[...]
        y = y * jax.nn.sigmoid(z)
        oc = (y * _silu(cg_ref[r, :].astype(F32))).astype(BF16)
        mix = jnp.concatenate([oa_ref[r, :], ob_ref[r, :], oc], axis=1)
        acc = jnp.dot(mix, wout_ref[...], preferred_element_type=F32)
        ms = jnp.mean(acc * acc, axis=-1, keepdims=True)
        o_ref[r, :] = x_ref[r, :] + acc * lax.rsqrt(ms + EPS) * gain

    def after_block(j):
        if (j + 1) % blocks_per_part == 0:
            project(j // blocks_per_part)

    _attention_steps(sink_ref, qt_ref, kp_ref, kc_ref, vtp_ref, vtc_ref, ag_ref, bias_ref, oa_ref, first_tile,
                     OUT_TM // A_BLOCK, blocks_per_part, after_block)


def _out(sinks, qt, k, vt, ag, bias, ob, yc, cg, x2, gate, g_post, w_glu, b_glu, w_out, layer):
    tm = OUT_TM
    steps_per_batch = SEQ // tm
    blocks = tm // A_BLOCK
    row = lambda i: (i, 0)
    col = lambda i: (0, i)
    const = lambda i: (0, 0)
    once = dict(pipeline_mode=pl.Buffered(1))
    prev = lambda i: (i * blocks - jnp.minimum(i % steps_per_batch, 1), 0)
    prev_t = lambda i: (0, i * blocks - jnp.minimum(i % steps_per_batch, 1))
    return pl.pallas_call(
        _out_kernel,
        out_shape=jax.ShapeDtypeStruct((TOKENS, D_MODEL), F32),
        grid=(TOKENS // tm,),
        in_specs=[pl.BlockSpec(memory_space=pltpu.SMEM),
                  pl.BlockSpec((A_WIDTH, tm), col),
                  pl.BlockSpec((A_BLOCK, A_KV_WIDTH), prev),
                  pl.BlockSpec((tm, A_KV_WIDTH), row),
                  pl.BlockSpec((A_KV_WIDTH, A_BLOCK), prev_t),
                  pl.BlockSpec((A_KV_WIDTH, tm), col),
                  pl.BlockSpec((tm, A_WIDTH), row),
                  pl.BlockSpec((2, A_Q_HEADS, A_BLOCK, A_BLOCK), lambda i: (0, 0, 0, 0), **once),
                  pl.BlockSpec((tm, B_WIDTH), row),
                  pl.BlockSpec((tm, C_WIDTH), row),
                  pl.BlockSpec((tm, C_WIDTH), row),
                  pl.BlockSpec((tm, D_MODEL), row),
                  pl.BlockSpec((1, 1, D_MODEL), lambda i: (i // steps_per_batch, 0, 0)),
                  pl.BlockSpec((1, D_MODEL), const),
                  pl.BlockSpec((C_WIDTH, C_WIDTH), lambda i: (layer, 0), **once),
                  pl.BlockSpec((1, C_WIDTH), const),
                  pl.BlockSpec((2 * D_MODEL, D_MODEL), lambda i: (layer, 0), **once)],
        out_specs=pl.BlockSpec((tm, D_MODEL), row),
        scratch_shapes=[pltpu.VMEM((tm, A_WIDTH), BF16)],
        compiler_params=_cparams(("arbitrary",)),
        name="out",
    )(sinks, qt, k, k, vt, vt, ag, bias, ob, yc, cg, x2, gate, g_post.reshape(1, D_MODEL), w_glu,
      b_glu.reshape(1, C_WIDTH), w_out)


def kernel(x, c, w_mod, b_mod, g_pre, g_post, w_in, attn_sinks, gla_w_alpha, gla_b_alpha, gla_norm_g,
           s5_a_re, s5_a_im, s5_log_dt, s5_b_re, s5_b_im, s5_c_re, s5_c_im, s5_d, s5_w_glu, s5_b_glu, w_out):
    layers = w_mod.shape[0]
    x2 = x.reshape(TOKENS, D_MODEL)
    bias = _attn_bias()
    mod = _mod(jnp.pad(c, ((0, 8 - BATCH), (0, 0))), w_mod, b_mod)[:, :BATCH]
    shift, scale, gate = (m.reshape(layers, BATCH, 1, D_MODEL) for m in jnp.split(mod, 3, axis=-1))
    w_t = jnp.swapaxes(w_in, 1, 2).astype(BF16).reshape(layers * W_IN_COLS, D_MODEL)
    w_alpha = gla_w_alpha.astype(BF16)
    s5_ops = _s5prep(s5_a_re, s5_a_im, s5_log_dt, s5_b_re, s5_b_im, s5_c_re, s5_c_im, s5_d)
    *s5_ops, s5_d_rows = s5_ops
    w_glu = s5_w_glu.astype(BF16).reshape(layers * C_WIDTH, C_WIDTH)
    w_out_b = w_out.astype(BF16).reshape(layers * 2 * D_MODEL, D_MODEL)
    for l in range(layers):
        ak, ag, bq, bk, bv, bg, cu, cg, log_a, aqt, avt = _proj(
            x2, scale[l], shift[l], g_pre[l], w_t, l, w_alpha[l], gla_b_alpha[l])
        o_b = _gla(log_a, bq, bk, bv, bg, gla_norm_g[l])
        y_c = _s5(cu, *s5_ops, s5_d_rows[l], l)
        x2 = _out(attn_sinks[l], aqt, ak, avt, ag, bias, o_b, y_c, cg, x2, gate[l], g_post[l], w_glu, s5_b_glu[l],
                  w_out_b, l)
    return x2.reshape(x.shape)
```

```python
import math

import jax
import jax.numpy as jnp
import numpy as np
from jax import lax
from jax.experimental import pallas as pl
from jax.experimental.pallas import tpu as pltpu

F32 = jnp.float32
BF16 = jnp.bfloat16
HIGHEST = lax.Precision.HIGHEST

D_MODEL = 1024
BATCH = 4
SEQ = 4096
TOKENS = BATCH * SEQ
EPS = 1e-6

A_WIDTH = 1024
A_HEAD_DIM = 64
A_Q_HEADS = 16
A_KV_HEADS = 4
A_KV_WIDTH = A_KV_HEADS * A_HEAD_DIM
A_BLOCK = 128
WINDOW = 128
assert WINDOW == A_BLOCK

B_WIDTH = 512
B_HEADS = 4
B_DK = 64
B_DV = 128
B_QK_WIDTH = 256
B_GATE_RANK = 16
B_GATE_TAU = 16.0
GLA_BLOCK = 64
GLA_SUB = 8
GLA_DIAG = 16
GLA_ANCHOR_SEGS = (32, 64)
assert GLA_ANCHOR_SEGS[0] == 2 * GLA_DIAG and GLA_ANCHOR_SEGS[-1] == GLA_BLOCK
OUT_PARTS = 2

C_WIDTH = 512
C_GROUP_CH = 16
C_GROUPS = 32
C_STATE = 64
S5_CHUNK = 16
S5_NCHUNK = SEQ // S5_CHUNK
S5_TC = S5_CHUNK * C_GROUP_CH
S5_GB = 8
S5_NGB = C_GROUPS // S5_GB
S5_PAIRS_PER_GB = S5_GB // 2
S5_POW_ROWS = 24
S5PREP_PAIRS = 4
SUBLANES = 8
S5_SEGS = SUBLANES
S5_SEG_CHUNKS = S5_NCHUNK // S5_SEGS
S5_STEP_ROWS = S5_SEG_CHUNKS + SUBLANES

LANES = 128

V7X_VMEM_LIMIT = 56 * 1024 * 1024

PROJ_TM = 1024
OUT_TM = 1024

_W_IN_SIZES = (("aq", A_WIDTH), ("ak", A_KV_WIDTH), ("av", A_KV_WIDTH), ("ag", A_WIDTH), ("bq", B_QK_WIDTH),
               ("bk", B_QK_WIDTH), ("bv", B_WIDTH), ("blr", B_GATE_RANK), ("bg", B_WIDTH), ("cu", C_WIDTH),
               ("cg", C_WIDTH))
_W_IN_OFF = {}
_off = 0
for _name, _w in _W_IN_SIZES:
    _W_IN_OFF[_name] = (_off, _w)
    _off += _w
W_IN_COLS = _off
_PROJ_OUTS = (("ak", BF16), ("ag", BF16), ("bq", BF16), ("bk", BF16),
              ("bv", BF16), ("bg", BF16), ("cu", F32), ("cg", BF16))


def _silu(x):
    return x * jax.nn.sigmoid(x)


def _cparams(semantics):
    return pltpu.CompilerParams(dimension_semantics=semantics, vmem_limit_bytes=V7X_VMEM_LIMIT)


def _mod_kernel(c_ref, w_ref, b_ref, o_ref):
    c = c_ref[...]
    o_ref[0] = jnp.dot(_silu(c).astype(BF16), w_ref[0].astype(BF16), preferred_element_type=F32) + b_ref[0]


def _mod(c_pad, w_mod, b_mod):
    layers = w_mod.shape[0]
    n = 3 * D_MODEL
    tn = n // 2
    return pl.pallas_call(
        _mod_kernel,
        out_shape=jax.ShapeDtypeStruct((layers, 8, n), F32),
        grid=(layers, n // tn),
        in_specs=[pl.BlockSpec((8, D_MODEL), lambda l, j: (0, 0)),
                  pl.BlockSpec((1, D_MODEL, tn), lambda l, j: (l, 0, j)),
                  pl.BlockSpec((1, 1, tn), lambda l, j: (l, 0, j))],
        out_specs=pl.BlockSpec((1, 8, tn), lambda l, j: (l, 0, j)),
        compiler_params=_cparams(("arbitrary", "arbitrary")),
        name="mod",
    )(c_pad, w_mod, b_mod.reshape(layers, 1, n))


def _proj_kernel(x_ref, scale_ref, shift_ref, gpre_ref, wt_ref, walpha_ref, balpha_ref, *out_refs):
    x = x_ref[...]
    ms = jnp.mean(x * x, axis=-1, keepdims=True)
    y = x * lax.rsqrt(ms + EPS) * gpre_ref[...]
    h = (y * (1.0 + scale_ref[0]) + shift_ref[0]).astype(BF16)
    nt = (((1,), (1,)), ((), ()))

    def rows(name):
        off, w = _W_IN_OFF[name]
        return wt_ref[off:off + w, :]

    for (name, _), o_ref in zip(_PROJ_OUTS, out_refs):
        o_ref[...] = lax.dot_general(h, rows(name), nt, preferred_element_type=F32).astype(o_ref.dtype)
    la_ref, qt_ref, vt_ref = out_refs[len(_PROJ_OUTS):]
    qlr = lax.dot_general(jnp.concatenate([rows("aq"), rows("blr")], axis=0), h, nt, preferred_element_type=F32)
    qt_ref[...] = (qlr[:A_WIDTH] * (A_HEAD_DIM ** -0.5)).astype(qt_ref.dtype)
    vt_ref[...] = lax.dot_general(rows("av"), h, nt, preferred_element_type=F32).astype(vt_ref.dtype)
    lr_t = qlr[A_WIDTH:].astype(BF16)
    logits = lax.dot_general(lr_t, walpha_ref[...], (((0,), (0,)), ((), ())),
                             preferred_element_type=F32) + balpha_ref[...]
    log_sig = jnp.minimum(logits, 0.0) - jnp.log(1.0 + jnp.exp(-jnp.abs(logits)))
    la_ref[...] = log_sig * (math.log2(math.e) / B_GATE_TAU)


def _proj(x2, scale, shift, g_pre, w_t, layer, w_alpha, b_alpha):
    tm = PROJ_TM
    steps_per_batch = SEQ // tm
    row = lambda i: (i, 0)
    col = lambda i: (0, i)
    per_batch = lambda i: (i // steps_per_batch, 0, 0)
    const = lambda i: (0, 0)
    out_shape = [jax.ShapeDtypeStruct((TOKENS, _W_IN_OFF[n][1]), dt) for n, dt in _PROJ_OUTS]
    out_specs = [pl.BlockSpec((tm, _W_IN_OFF[n][1]), row) for n, _ in _PROJ_OUTS]
    out_shape += [jax.ShapeDtypeStruct((TOKENS, B_QK_WIDTH), F32),
                  jax.ShapeDtypeStruct((A_WIDTH, TOKENS), BF16),
                  jax.ShapeDtypeStruct((A_KV_WIDTH, TOKENS), BF16)]
    out_specs += [pl.BlockSpec((tm, B_QK_WIDTH), row),
                  pl.BlockSpec((A_WIDTH, tm), col),
                  pl.BlockSpec((A_KV_WIDTH, tm), col)]
    return pl.pallas_call(
        _proj_kernel,
        out_shape=out_shape,
        grid=(TOKENS // tm,),
        in_specs=[pl.BlockSpec((tm, D_MODEL), row),
                  pl.BlockSpec((1, 1, D_MODEL), per_batch),
                  pl.BlockSpec((1, 1, D_MODEL), per_batch),
                  pl.BlockSpec((1, D_MODEL), const),
                  pl.BlockSpec((W_IN_COLS, D_MODEL), lambda i: (layer, 0), pipeline_mode=pl.Buffered(1)),
                  pl.BlockSpec((B_GATE_RANK, B_QK_WIDTH), const),
                  pl.BlockSpec((1, B_QK_WIDTH), const)],
        out_specs=out_specs,
        compiler_params=_cparams(("arbitrary",)),
        name="proj",
    )(x2, scale, shift, g_pre.reshape(1, D_MODEL), w_t, w_alpha, b_alpha.reshape(1, B_QK_WIDTH))


def _attn_bias():
    j = np.arange(A_BLOCK)[:, None]
    i = np.arange(A_BLOCK)[None, :]
    dist = np.where(j > i, i + A_BLOCK - j, i - j).astype(np.float32)
    slopes = np.exp2(-8.0 * np.arange(1, A_Q_HEADS + 1, dtype=np.float32) / A_Q_HEADS).astype(np.float32)
    bias = -slopes[:, None, None] * dist[None]
    first = np.where((j > i)[None], -np.inf, bias).astype(np.float32)
    return jnp.asarray(np.stack([bias, first]))


def _attention_steps(sink_ref, qt_ref, kp_ref, kc_ref, vtp_ref, vtc_ref, g_ref, bias_ref, oa_ref, first_tile,
                     n_blocks, group_blocks, after_block):
    kj = lax.broadcasted_iota(jnp.int32, (A_BLOCK, A_BLOCK), 0)
    qi = lax.broadcasted_iota(jnp.int32, (A_BLOCK, A_BLOCK), 1)
    from_prev = kj > qi
    zero_rows = jnp.zeros((A_HEAD_DIM, A_BLOCK), BF16)
    group = A_Q_HEADS // A_KV_HEADS
    blk = lambda j: slice(A_BLOCK * j, A_BLOCK * (j + 1))

    def keys(j, sl):
        prev = kp_ref[:, sl] if j == 0 else kc_ref[blk(j - 1), sl]
        return prev, kc_ref[blk(j), sl]

    def values(j, rows):
        prev = vtp_ref[rows, :] if j == 0 else vtc_ref[rows, blk(j - 1)]
        return jnp.concatenate([prev, vtc_ref[rows, blk(j)]], axis=1)

    def scores(j, hd):
        kvh = hd // group
        sl = slice(LANES * (kvh // 2), LANES * (kvh // 2 + 1))
        qh = qt_ref[A_HEAD_DIM * hd:A_HEAD_DIM * (hd + 1), blk(j)]
        qsel = jnp.concatenate([qh, zero_rows] if kvh % 2 == 0 else [zero_rows, qh], axis=0)
        k_prev, k_cur = keys(j, sl)
        return (jnp.dot(k_prev, qsel, preferred_element_type=F32),
                jnp.dot(k_cur, qsel, preferred_element_type=F32))

    def attend(j, hd, s_prev, s_cur):
        kvh = hd // group
        v_both = values(j, slice(A_HEAD_DIM * kvh, A_HEAD_DIM * (kvh + 1)))
        table = jnp.where(first_tile, 1, 0) if j == 0 else 0
        s = jnp.where(from_prev, s_prev, s_cur) + bias_ref[table, hd]
        sink = sink_ref[hd]
        m = jnp.maximum(jnp.max(s, axis=0, keepdims=True), sink)
        p = jnp.exp(s - m)
        den = jnp.sum(p, axis=0, keepdims=True) + jnp.exp(sink - m)
        p_both = jnp.concatenate([jnp.where(from_prev, p, 0.0), jnp.where(from_prev, 0.0, p)],
                                 axis=0).astype(BF16)
        return jnp.dot(v_both, p_both, preferred_element_type=F32) / den

    pending = {}

    def issue_scores(first):
        for j in range(first, min(first + group_blocks, n_blocks)):
            for hd in range(A_Q_HEADS):
                pending[j, hd] = scores(j, hd)

    issue_scores(0)
    for j in range(n_blocks):
        if (j + 1) % group_blocks == 0:
            issue_scores(j + 1)
        outs = {}
        for hd in range(A_Q_HEADS):
            outs[hd] = attend(j, hd, *pending.pop((j, hd)))
            if hd % 2 == 1:
                qsl = slice(LANES * (hd // 2), LANES * (hd // 2 + 1))
                o_pair = jnp.concatenate([outs.pop(hd - 1), outs.pop(hd)], axis=0).T
                gate = g_ref[blk(j), qsl].astype(F32)
                oa_ref[blk(j), qsl] = (o_pair * _silu(gate)).astype(oa_ref.dtype)
        after_block(j)


def _gla_kernel(la_ref, q_ref, k_ref, v_ref, g_ref, gn_ref, o_ref, st_ref):
    cb = GLA_BLOCK

    @pl.when(pl.program_id(0) == 0)
    def _():
        st_ref[...] = jnp.zeros_like(st_ref)

    r = lax.broadcasted_iota(jnp.int32, (cb, cb), 0)
    c = lax.broadcasted_iota(jnp.int32, (cb, cb), 1)
    tri = (c <= r).astype(BF16)
    lane = lax.broadcasted_iota(jnp.int32, (1, B_QK_WIDTH), 1)
    head_masks = [(lane >= B_DK * h) & (lane < B_DK * (h + 1)) for h in range(B_HEADS)]
    rr = lax.broadcasted_iota(jnp.int32, (B_HEADS * cb, cb), 0)
    cc = lax.broadcasted_iota(jnp.int32, (B_HEADS * cb, cb), 1)
    ri = rr & (cb - 1)
    nt = (((1,), (1,)), ((), ()))
    tn = (((0,), (0,)), ((), ()))
    rows = lambda u: slice(cb * u, cb * (u + 1))
    items = [(u, b) for u in range(GLA_SUB) for b in range(BATCH)]
    def cumsum(la):
        hi = la.astype(BF16)
        r1 = la - hi.astype(F32)
        mid = r1.astype(BF16)
        lo = (r1 - mid.astype(F32)).astype(BF16)
        parts = jnp.dot(tri, jnp.concatenate([hi, mid, lo], axis=1), preferred_element_type=F32)
        w = B_QK_WIDTH
        return parts[:, :w] + (parts[:, w:2 * w] + parts[:, 2 * w:])

    bcs = {(u, b): cumsum(la_ref[b, rows(u), :]) for u, b in items}
    stack_heads = lambda a: jnp.concatenate([jnp.where(m, a, 0.0) for m in head_masks], axis=0).astype(BF16)
    row = lax.broadcasted_iota(jnp.int32, (cb, 1), 0)
    diag = ((ri ^ cc) < GLA_DIAG) & (cc <= ri)
    same_seg = {seg: (ri ^ cc) < seg for seg in GLA_ANCHOR_SEGS}

    def upper(a, seg):
        return jnp.concatenate([a[s0 + seg // 2:s0 + seg] for s0 in range(0, cb, seg)], axis=0)

    def place_upper(p, seg):
        half = seg // 2
        zero = jnp.zeros((half, cb), F32)
        parts = []
        for piece in range(B_HEADS * cb // seg):
            parts += [zero, p[half * piece:half * (piece + 1)]]
        return jnp.concatenate(parts, axis=0)

    def anchor_rows(bc, seg, offset, reps):
        parts = []
        for s0 in range(0, cb, seg):
            a = s0 + offset - 1
            val = bc[a:a + 1, :] if a >= 0 else jnp.zeros((1, B_QK_WIDTH), F32)
            parts.append(jnp.broadcast_to(val, (reps, B_QK_WIDTH)))
        return jnp.concatenate(parts, axis=0)

    q_lv, k_lv, qsts, ksts, decs, vs = {}, {}, {}, {}, {}, {}
    for it in items:
        u, b = it
        bc = bcs[it]
        bl = bc[cb - 1:cb, :]
        q = q_ref[b, rows(u), :].astype(F32) * (B_DK ** -0.5)
        k = k_ref[b, rows(u), :].astype(F32)
        for seg in GLA_ANCHOR_SEGS:
            half = seg // 2
            q_lv[it, seg] = stack_heads(upper(q, seg) * jnp.exp2(upper(bc, seg) - anchor_rows(bc, seg, half, half)))
            in_lower = (row & (seg - 1)) < half
            k_lv[it, seg] = jnp.where(in_lower, k * jnp.exp2(anchor_rows(bc, seg, half, seg) - bc), 0.0).astype(BF16)
        anc = anchor_rows(bc, GLA_DIAG, 0, GLA_DIAG)
        q_lv[it, 0] = stack_heads(q * jnp.exp2(bc - anc))
        k_lv[it, 0] = (k * jnp.exp2(anc - bc)).astype(BF16)
        qsts[it] = stack_heads(q * jnp.exp2(bc))
        ksts[it] = stack_heads(k * jnp.exp2(bl - bc))
        decs[it] = jnp.exp2(bl)
        vs[it] = v_ref[b, rows(u), :]
    a_alls = {}
    for it in items:
        prod = lambda lv: lax.dot_general(q_lv[it, lv], k_lv[it, lv], nt, preferred_element_type=F32)
        a = jnp.where(diag, prod(0), 0.0)
        for seg in GLA_ANCHOR_SEGS:
            a = a + jnp.where(same_seg[seg], place_upper(prod(seg), seg), 0.0)
        a_alls[it] = a.astype(BF16)
    upds = {}
    for it in items:
        vst = jnp.concatenate([vs[it][:, B_DV * h:B_DV * (h + 1)] for h in range(B_HEADS)], axis=0)
        upds[it] = lax.dot_general(vst, ksts[it], tn, preferred_element_type=F32)
    st_in = {}
    for b in range(BATCH):
        st = st_ref[b]
        for u in range(GLA_SUB):
            st_in[u, b] = st
            st = st * decs[u, b] + upds[u, b]
        st_ref[b] = st
    oi_alls = {it: lax.dot_general(qsts[it], st_in[it].astype(BF16), nt, preferred_element_type=F32)
               for it in items}
    o_hs = {}
    for it in items:
        a_all = a_alls[it]
        for h in range(B_HEADS):
            o_hs[it, h] = (jnp.dot(a_all[cb * h:cb * (h + 1)], vs[it][:, B_DV * h:B_DV * (h + 1)],
                                   preferred_element_type=F32) + oi_alls[it][cb * h:cb * (h + 1)])
    for it in items:
        u, b = it
        for h in range(B_HEADS):
            vsl = slice(B_DV * h, B_DV * (h + 1))
            o_h = o_hs[it, h]
            ms = jnp.mean(o_h * o_h, axis=-1, keepdims=True)
            o_n = o_h * lax.rsqrt(ms + EPS) * gn_ref[:, vsl]
            gate = g_ref[b, rows(u), vsl].astype(F32)
            o_ref[b, rows(u), vsl] = (o_n * _silu(gate)).astype(o_ref.dtype)


def _gla(log_a, bq, bk, bv, bg, g_gla):
    cb = GLA_BLOCK * GLA_SUB
    blk = lambda w: pl.BlockSpec((BATCH, cb, w), lambda i: (0, i, 0))
    r3 = lambda a: a.reshape(BATCH, SEQ, a.shape[-1])
    out = pl.pallas_call(
        _gla_kernel,
        out_shape=jax.ShapeDtypeStruct((BATCH, SEQ, B_WIDTH), BF16),
        grid=(SEQ // cb,),
        in_specs=[blk(B_QK_WIDTH), blk(B_QK_WIDTH), blk(B_QK_WIDTH), blk(B_WIDTH), blk(B_WIDTH),
                  pl.BlockSpec((1, B_WIDTH), lambda i: (0, 0))],
        out_specs=blk(B_WIDTH),
        scratch_shapes=[pltpu.VMEM((BATCH, B_DV, B_QK_WIDTH), F32)],
        compiler_params=_cparams(("arbitrary",)),
        name="gla",
    )(r3(log_a), r3(bq), r3(bk), r3(bv), r3(bg), g_gla.reshape(1, B_WIDTH))
    return out.reshape(TOKENS, B_WIDTH)


def _s5prep_kernel(ar_ref, ai_ref, ldt_ref, bre_ref, bim_ref, btre_ref, btim_ref, cre_ref, cim_ref,
                   mt_ref, wet_ref, wyt_ref, are_ref, aim_ref):
    p = C_STATE
    for pr in range(S5PREP_PAIRS):
        wet_re, wet_im, wyt_rows, a_re, a_im = [], [], [], [], []
        for g in range(2):
            i = 2 * pr + g
            kk, e_re, e_im, y_re, y_im, p_re, p_im = _s5_discretise(
                ar_ref[i], ai_ref[i], ldt_ref[i], bre_ref[i], bim_ref[i], btre_ref[i], btim_ref[i],
                cre_ref[i], cim_ref[i])
            pieces = [kk] + [jnp.concatenate([jnp.zeros((C_GROUP_CH * s, C_GROUP_CH), F32),
                                              kk[:S5_TC - C_GROUP_CH * s]], axis=0) for s in range(1, S5_CHUNK)]
            mt_ref[pr, g] = jnp.concatenate(pieces, axis=1).astype(BF16)
            e_t = jnp.concatenate([e_re, e_im], axis=1).T
            zero = jnp.zeros((p, S5_TC), F32)
            wet_re.append(jnp.concatenate([e_t[:p], zero] if g == 0 else [zero, e_t[:p]], axis=1))
            wet_im.append(jnp.concatenate([e_t[p:], zero] if g == 0 else [zero, e_t[p:]], axis=1))
            zero = jnp.zeros((S5_TC, p), F32)
            wyt_rows.append(
                jnp.concatenate([y_re, zero, -y_im, zero] if g == 0 else [zero, y_re, zero, -y_im], axis=1))
            a_re.append(p_re)
            a_im.append(p_im)
        wet_ref[pr] = jnp.concatenate(wet_re + wet_im, axis=0).astype(BF16)
        wyt_ref[pr] = jnp.concatenate(wyt_rows, axis=0).astype(BF16)
        r = lax.broadcasted_iota(jnp.int32, (S5_STEP_ROWS, 1), 0)
        f_re, f_im = jnp.concatenate(a_re, axis=1), jnp.concatenate(a_im, axis=1)
        t_re, t_im = jnp.ones((S5_STEP_ROWS, 2 * p), F32), jnp.zeros((S5_STEP_ROWS, 2 * p), F32)
        for k in range(S5_SEG_CHUNKS.bit_length()):
            s_re, s_im = jnp.where((r >> k) & 1 == 1, f_re, 1.0), jnp.where((r >> k) & 1 == 1, f_im, 0.0)
            t_re, t_im = t_re * s_re - t_im * s_im, t_re * s_im + t_im * s_re
            f_re, f_im = f_re * f_re - f_im * f_im, 2.0 * (f_re * f_im)
        are_ref[pr] = t_re
        aim_ref[pr] = t_im


def _s5_discretise(ar, ai, ldt, b_re, b_im, bt_re16, bt_im16, c_re16, c_im16):
    dt = jnp.exp(ldt)

    def cmul(xr, xi, yr, yi):
        return xr * yr - xi * yi, xr * yi + xi * yr

    kf = lax.broadcasted_iota(jnp.int32, (S5_POW_ROWS, 1), 0).astype(F32)
    mag = jnp.exp(kf * (ar * dt))
    ang = kf * (ai * dt)
    pw_re, pw_im = mag * jnp.cos(ang), mag * jnp.sin(ang)
    abar_re, abar_im = pw_re[1:2], pw_im[1:2]
    den = ar * ar + ai * ai
    num_re = abar_re - 1.0
    f_re = (num_re * ar + abar_im * ai) / den
    f_im = (abar_im * ar - num_re * ai) / den
    g_re, g_im = cmul(pw_re, pw_im, f_re, f_im)

    def pick(which, xr, xi):
        rep = lambda x: jnp.concatenate(
            [jnp.broadcast_to(x[which(i):which(i) + 1], (C_GROUP_CH, C_STATE)) for i in range(S5_CHUNK)], axis=0)
        return rep(xr), rep(xi)

    tile16 = lambda a: jnp.concatenate([a] * S5_CHUNK, axis=0)
    ct_re, ct_im = tile16(c_re16), tile16(c_im16)
    bt_re, bt_im = tile16(bt_re16), tile16(bt_im16)

    w_re, w_im = cmul(*pick(lambda i: i, g_re, g_im), ct_re, ct_im)
    kk = (jnp.dot(w_re, b_re, preferred_element_type=F32, precision=HIGHEST)
          - jnp.dot(w_im, b_im, preferred_element_type=F32, precision=HIGHEST))
    e_re, e_im = cmul(*pick(lambda i: S5_CHUNK - 1 - i, g_re, g_im), bt_re, bt_im)
    y_re, y_im = cmul(*pick(lambda i: i + 1, pw_re, pw_im), ct_re, ct_im)
    return kk, e_re, e_im, y_re, y_im, pw_re[S5_CHUNK:S5_CHUNK + 1], pw_im[S5_CHUNK:S5_CHUNK + 1]


def _s5prep(a_re, a_im, log_dt, b_re, b_im, c_re, c_im, d):
    p, ch = C_STATE, C_GROUP_CH
    layers = a_re.shape[0]
    g = layers * C_GROUPS
    npair = g // 2
    flat = lambda a: a.reshape(g, *a.shape[2:])
    a_re, a_im, log_dt, b_re, b_im, c_re, c_im = map(flat, (a_re, a_im, log_dt, b_re, b_im, c_re, c_im))
    row = lambda a: a.reshape(g, 1, p)
    ldt = jnp.broadcast_to(log_dt[:, None, None], (g, 1, p))
    b_t = lambda a: jnp.swapaxes(a, 1, 2)
    pps = S5PREP_PAIRS
    spec = lambda s1, s2: pl.BlockSpec((2 * pps, s1, s2), lambda i: (i, 0, 0))
    mt, wet, wyt, pw_re, pw_im = pl.pallas_call(
        _s5prep_kernel,
        out_shape=[jax.ShapeDtypeStruct((npair, 2, S5_TC, S5_TC), BF16),
                   jax.ShapeDtypeStruct((npair, 4 * p, 2 * S5_TC), BF16),
                   jax.ShapeDtypeStruct((npair, 2 * S5_TC, 4 * p), BF16),
                   jax.ShapeDtypeStruct((npair, S5_STEP_ROWS, 2 * p), F32),
                   jax.ShapeDtypeStruct((npair, S5_STEP_ROWS, 2 * p), F32)],
        grid=(npair // pps,),
        in_specs=[spec(1, p), spec(1, p), spec(1, p), spec(p, ch), spec(p, ch),
                  spec(ch, p), spec(ch, p), spec(ch, p), spec(ch, p)],
        out_specs=[pl.BlockSpec((pps, 2, S5_TC, S5_TC), lambda i: (i, 0, 0, 0)),
                   pl.BlockSpec((pps, 4 * p, 2 * S5_TC), lambda i: (i, 0, 0)),
                   pl.BlockSpec((pps, 2 * S5_TC, 4 * p), lambda i: (i, 0, 0)),
                   pl.BlockSpec((pps, S5_STEP_ROWS, 2 * p), lambda i: (i, 0, 0)),
                   pl.BlockSpec((pps, S5_STEP_ROWS, 2 * p), lambda i: (i, 0, 0))],
        compiler_params=_cparams(("arbitrary",)),
        name="s5prep",
    )(row(a_re), row(a_im), ldt, b_re, b_im, b_t(b_re), b_t(b_im), c_re, c_im)
    by_gb = lambda a: a.reshape(layers * S5_NGB, S5_PAIRS_PER_GB, *a.shape[1:])
    return (mt.reshape(layers * S5_NGB, S5_GB, S5_TC, S5_TC), by_gb(wet), by_gb(wyt), by_gb(pw_re), by_gb(pw_im),
            d.reshape(layers, 1, C_WIDTH))


def _s5_kernel(u_hbm, mt_ref, wet_ref, wyt_ref, are_ref, aim_ref, d_ref, y_hbm,
               ubuf, ybuf, in_sem, out_sem, ut_ref, yt_ref, ere_ref, eim_ref, hre_ref, him_ref):
    nck, t_len, ch, seg = S5_NCHUNK, S5_CHUNK, C_GROUP_CH, S5_SEG_CHUNKS
    nt = (((1,), (1,)), ((), ()))
    n_steps = S5_NGB * BATCH
    step = pl.program_id(0) * BATCH + pl.program_id(1)
    slot = step % 2

    def hbm_tile(ref, step_, s):
        gb_, b_ = step_ // BATCH, step_ % BATCH
        return ref.at[pl.ds(b_ * nck + s * seg, seg), :, pl.ds(gb_ * LANES, LANES)]

    def in_copy(step_, slot_, s):
        return pltpu.make_async_copy(hbm_tile(u_hbm, step_, s), ubuf.at[slot_, :, :, s, :], in_sem.at[slot_, s])

    def out_copy(step_, slot_, s):
        return pltpu.make_async_copy(ybuf.at[slot_, :, :, s, :], hbm_tile(y_hbm, step_, s), out_sem.at[slot_, s])

    @pl.when(step == 0)
    def _():
        for s in range(S5_SEGS):
            in_copy(step, slot, s).start()

    @pl.when(step + 1 < n_steps)
    def _():
        for s in range(S5_SEGS):
            in_copy(step + 1, 1 - slot, s).start()

    @pl.when(step >= 2)
    def _():
        for s in range(S5_SEGS):
            out_copy(step - 2, slot, s).wait()

    for s in range(S5_SEGS):
        in_copy(step, slot, s).wait()
    u_rows = lambda t: ubuf[slot, :, t].reshape(nck, LANES)
    for t in range(t_len):
        xt = u_rows(t).T
        for g in range(S5_GB):
            ut_ref[g, ch * t:ch * (t + 1), :] = xt[ch * g:ch * (g + 1), :]
    for j in range(S5_PAIRS_PER_GB):
        u0 = ut_ref[2 * j].astype(BF16)
        u1 = ut_ref[2 * j + 1].astype(BF16)
        et = jnp.dot(wet_ref[0, j], jnp.concatenate([u0, u1], axis=0), preferred_element_type=F32)
        e = et.T
        ere_ref[:, LANES * j:LANES * (j + 1)] = e[:, :LANES]
        eim_ref[:, LANES * j:LANES * (j + 1)] = e[:, LANES:]
        yt_ref[2 * j] = jnp.dot(mt_ref[0, 2 * j], u0, preferred_element_type=F32)
        yt_ref[2 * j + 1] = jnp.dot(mt_ref[0, 2 * j + 1], u1, preferred_element_type=F32)

    width = S5_GB * C_STATE

    def step_power(r, rows):
        row = lambda ref: jnp.concatenate([ref[0, j, r:r + 1, :] for j in range(S5_PAIRS_PER_GB)], axis=1)
        return (jnp.broadcast_to(row(are_ref), (rows, width)), jnp.broadcast_to(row(aim_ref), (rows, width)))

    a_re, a_im = step_power(1, S5_SEGS)

    def body(i, carry):
        h_re, h_im = carry
        rows = pl.ds(pl.multiple_of(i * S5_SEGS, S5_SEGS), S5_SEGS)
        hre_ref[rows, :] = h_re
        him_ref[rows, :] = h_im
        e_re = ere_ref[rows, :]
        e_im = eim_ref[rows, :]
        return a_re * h_re - a_im * h_im + e_re, a_re * h_im + a_im * h_re + e_im

    zero = jnp.zeros((S5_SEGS, width), F32)
    end_re, end_im = lax.fori_loop(0, seg, body, (zero, zero))
    g_re, g_im = step_power(seg, 1)
    sub = lax.broadcasted_iota(jnp.int32, (S5_SEGS, 1), 0)
    c_re = c_im = zero
    p_re = p_im = jnp.zeros((1, width), F32)
    for s in range(1, S5_SEGS):
        p_re, p_im = (g_re * p_re - g_im * p_im + end_re[s - 1:s], g_re * p_im + g_im * p_re + end_im[s - 1:s])
        c_re, c_im = jnp.where(sub == s, p_re, c_re), jnp.where(sub == s, p_im, c_im)
    for i in range(seg):
        rows = slice(S5_SEGS * i, S5_SEGS * (i + 1))
        w_re, w_im = step_power(i, S5_SEGS)
        hre_ref[rows, :] += w_re * c_re - w_im * c_im
        him_ref[rows, :] += w_re * c_im + w_im * c_re

    for j in range(S5_PAIRS_PER_GB):
        sl = slice(LANES * j, LANES * (j + 1))
        hp = jnp.concatenate([hre_ref[:, sl], him_ref[:, sl]], axis=1).astype(BF16)
        yi = lax.dot_general(wyt_ref[0, j], hp, nt, preferred_element_type=F32)
        yt_ref[2 * j] += yi[:S5_TC]
        yt_ref[2 * j + 1] += yi[S5_TC:]
    for t in range(t_len):
        ytt = jnp.concatenate([yt_ref[g, ch * t:ch * (t + 1), :] for g in range(S5_GB)], axis=0)
        ybuf[slot, :, t] = (ytt.T + d_ref[...] * u_rows(t)).reshape(seg, S5_SEGS, LANES)
    for s in range(S5_SEGS):
        out_copy(step, slot, s).start()

    @pl.when(step == n_steps - 1)
    def _():
        for s in range(S5_SEGS):
            out_copy(step - 1, 1 - slot, s).wait()
            out_copy(step, slot, s).wait()


def _s5(cu, mt, wet, wyt, pw_re, pw_im, d_row, layer):
    p4 = 4 * C_STATE
    per_gb = lambda *s: pl.BlockSpec((1,) + s, lambda gb, b: (layer * S5_NGB + gb,) + (0,) * len(s))
    state = pltpu.VMEM((S5_NCHUNK, S5_GB * C_STATE), F32)
    tiles = pltpu.VMEM((2, S5_SEG_CHUNKS, S5_CHUNK, S5_SEGS, LANES), F32)
    sems = pltpu.SemaphoreType.DMA((2, S5_SEGS))
    by_chunk = (TOKENS // S5_CHUNK, S5_CHUNK, C_WIDTH)
    y = pl.pallas_call(
        _s5_kernel,
        out_shape=jax.ShapeDtypeStruct(by_chunk, F32),
        grid=(S5_NGB, BATCH),
        in_specs=[pl.BlockSpec(memory_space=pl.ANY),
                  per_gb(S5_GB, S5_TC, S5_TC),
                  per_gb(S5_PAIRS_PER_GB, p4, 2 * S5_TC),
                  per_gb(S5_PAIRS_PER_GB, 2 * S5_TC, p4),
                  per_gb(S5_PAIRS_PER_GB, S5_STEP_ROWS, 2 * C_STATE),
                  per_gb(S5_PAIRS_PER_GB, S5_STEP_ROWS, 2 * C_STATE),
                  pl.BlockSpec((1, LANES), lambda gb, b: (0, gb))],
        out_specs=pl.BlockSpec(memory_space=pl.ANY),
        scratch_shapes=[tiles, tiles, sems, sems,
                        pltpu.VMEM((S5_GB, S5_TC, S5_NCHUNK), F32),
                        pltpu.VMEM((S5_GB, S5_TC, S5_NCHUNK), F32),
                        state, state, state, state],
        compiler_params=_cparams(("arbitrary", "arbitrary")),
        name="s5",
    )(cu.reshape(by_chunk), mt, wet, wyt, pw_re, pw_im, d_row)
    return y.reshape(TOKENS, C_WIDTH)


def _gelu_tanh(x):
    c = math.sqrt(2.0 / math.pi)
    return 0.5 * x * (1.0 + jnp.tanh(x * (c + (c * 0.044715) * (x * x))))


def _out_kernel(sink_ref, qt_ref, kp_ref, kc_ref, vtp_ref, vtc_ref, ag_ref, bias_ref,
                ob_ref, yc_ref, cg_ref, x_ref, gate_ref, gpost_ref, wglu_ref, bglu_ref, wout_ref, o_ref, oa_ref):
    first_tile = pl.program_id(0) % (SEQ // OUT_TM) == 0
    blocks_per_part = OUT_TM // (A_BLOCK * OUT_PARTS)
    gain = gpost_ref[...] * gate_ref[0]

    def project(part):
        r = slice(A_BLOCK * blocks_per_part * part, A_BLOCK * blocks_per_part * (part + 1))
        y = _gelu_tanh(yc_ref[r, :])
        z = jnp.dot(y.astype(BF16), wglu_ref[...], preferred_element_type=F32) + bglu_ref[...]
        y = y * jax.nn.sigmoid(z)
        oc = (y * _silu(cg_ref[r, :].astype(F32))).astype(BF16)
        mix = jnp.concatenate([oa_ref[r, :], ob_ref[r, :], oc], axis=1)
        acc = jnp.dot(mix, wout_ref[...], preferred_element_type=F32)
        ms = jnp.mean(acc * acc, axis=-1, keepdims=True)
        o_ref[r, :] = x_ref[r, :] + acc * lax.rsqrt(ms + EPS) * gain

    def after_block(j):
        if (j + 1) % blocks_per_part == 0:
            project(j // blocks_per_part)

    _attention_steps(sink_ref, qt_ref, kp_ref, kc_ref, vtp_ref, vtc_ref, ag_ref, bias_ref, oa_ref, first_tile,
                     OUT_TM // A_BLOCK, blocks_per_part, after_block)


def _out(sinks, qt, k, vt, ag, bias, ob, yc, cg, x2, gate, g_post, w_glu, b_glu, w_out, layer):
    tm = OUT_TM
    steps_per_batch = SEQ // tm
    blocks = tm // A_BLOCK
    row = lambda i: (i, 0)
    col = lambda i: (0, i)
    const = lambda i: (0, 0)
    once = dict(pipeline_mode=pl.Buffered(1))
    prev = lambda i: (i * blocks - jnp.minimum(i % steps_per_batch, 1), 0)
    prev_t = lambda i: (0, i * blocks - jnp.minimum(i % steps_per_batch, 1))
    return pl.pallas_call(
        _out_kernel,
        out_shape=jax.ShapeDtypeStruct((TOKENS, D_MODEL), F32),
        grid=(TOKENS // tm,),
        in_specs=[pl.BlockSpec(memory_space=pltpu.SMEM),
                  pl.BlockSpec((A_WIDTH, tm), col),
                  pl.BlockSpec((A_BLOCK, A_KV_WIDTH), prev),
                  pl.BlockSpec((tm, A_KV_WIDTH), row),
                  pl.BlockSpec((A_KV_WIDTH, A_BLOCK), prev_t),
                  pl.BlockSpec((A_KV_WIDTH, tm), col),
                  pl.BlockSpec((tm, A_WIDTH), row),
                  pl.BlockSpec((2, A_Q_HEADS, A_BLOCK, A_BLOCK), lambda i: (0, 0, 0, 0), **once),
                  pl.BlockSpec((tm, B_WIDTH), row),
                  pl.BlockSpec((tm, C_WIDTH), row),
                  pl.BlockSpec((tm, C_WIDTH), row),
                  pl.BlockSpec((tm, D_MODEL), row),
                  pl.BlockSpec((1, 1, D_MODEL), lambda i: (i // steps_per_batch, 0, 0)),
                  pl.BlockSpec((1, D_MODEL), const),
                  pl.BlockSpec((C_WIDTH, C_WIDTH), lambda i: (layer, 0), **once),
                  pl.BlockSpec((1, C_WIDTH), const),
                  pl.BlockSpec((2 * D_MODEL, D_MODEL), lambda i: (layer, 0), **once)],
        out_specs=pl.BlockSpec((tm, D_MODEL), row),
        scratch_shapes=[pltpu.VMEM((tm, A_WIDTH), BF16)],
        compiler_params=_cparams(("arbitrary",)),
        name="out",
    )(sinks, qt, k, k, vt, vt, ag, bias, ob, yc, cg, x2, gate, g_post.reshape(1, D_MODEL), w_glu,
      b_glu.reshape(1, C_WIDTH), w_out)


def kernel(x, c, w_mod, b_mod, g_pre, g_post, w_in, attn_sinks, gla_w_alpha, gla_b_alpha, gla_norm_g,
           s5_a_re, s5_a_im, s5_log_dt, s5_b_re, s5_b_im, s5_c_re, s5_c_im, s5_d, s5_w_glu, s5_b_glu, w_out):
    layers = w_mod.shape[0]
    x2 = x.reshape(TOKENS, D_MODEL)
    bias = _attn_bias()
    mod = _mod(jnp.pad(c, ((0, 8 - BATCH), (0, 0))), w_mod, b_mod)[:, :BATCH]
    shift, scale, gate = (m.reshape(layers, BATCH, 1, D_MODEL) for m in jnp.split(mod, 3, axis=-1))
    w_t = jnp.swapaxes(w_in, 1, 2).astype(BF16).reshape(layers * W_IN_COLS, D_MODEL)
    w_alpha = gla_w_alpha.astype(BF16)
    s5_ops = _s5prep(s5_a_re, s5_a_im, s5_log_dt, s5_b_re, s5_b_im, s5_c_re, s5_c_im, s5_d)
    *s5_ops, s5_d_rows = s5_ops
    w_glu = s5_w_glu.astype(BF16).reshape(layers * C_WIDTH, C_WIDTH)
    w_out_b = w_out.astype(BF16).reshape(layers * 2 * D_MODEL, D_MODEL)
    for l in range(layers):
        ak, ag, bq, bk, bv, bg, cu, cg, log_a, aqt, avt = _proj(
            x2, scale[l], shift[l], g_pre[l], w_t, l, w_alpha[l], gla_b_alpha[l])
        o_b = _gla(log_a, bq, bk, bv, bg, gla_norm_g[l])
        y_c = _s5(cu, *s5_ops, s5_d_rows[l], l)
        x2 = _out(attn_sinks[l], aqt, ak, avt, ag, bias, o_b, y_c, cg, x2, gate[l], g_post[l], w_glu, s5_b_glu[l],
                  w_out_b, l)
    return x2.reshape(x.shape)
```

```python
import math

import jax
import jax.numpy as jnp
import numpy as np
from jax import lax
from jax.experimental import pallas as pl
from jax.experimental.pallas import tpu as pltpu

F32 = jnp.float32
BF16 = jnp.bfloat16
HIGHEST = lax.Precision.HIGHEST

D_MODEL = 1024
BATCH = 4
SEQ = 4096
TOKENS = BATCH * SEQ
EPS = 1e-6

A_WIDTH = 1024
A_HEAD_DIM = 64
A_Q_HEADS = 16
A_KV_HEADS = 4
A_KV_WIDTH = A_KV_HEADS * A_HEAD_DIM
A_BLOCK = 128
WINDOW = 128
assert WINDOW == A_BLOCK

B_WIDTH = 512
B_HEADS = 4
B_DK = 64
B_DV = 128
B_QK_WIDTH = 256
B_GATE_RANK = 16
B_GATE_TAU = 16.0
GLA_BLOCK = 64
GLA_SUB = 8
GLA_DIAG = 16
GLA_ANCHOR_SEGS = (32, 64)
assert GLA_ANCHOR_SEGS[0] == 2 * GLA_DIAG and GLA_ANCHOR_SEGS[-1] == GLA_BLOCK
OUT_PARTS = 2

C_WIDTH = 512
C_GROUP_CH = 16
C_GROUPS = 32
C_STATE = 64
S5_CHUNK = 16
S5_NCHUNK = SEQ // S5_CHUNK
S5_TC = S5_CHUNK * C_GROUP_CH
S5_GB = 8
S5_NGB = C_GROUPS // S5_GB
S5_PAIRS_PER_GB = S5_GB // 2
S5_POW_ROWS = 24
S5PREP_PAIRS = 4
SUBLANES = 8
S5_SEGS = SUBLANES
S5_SEG_CHUNKS = S5_NCHUNK // S5_SEGS
S5_STEP_ROWS = S5_SEG_CHUNKS + SUBLANES

LANES = 128

V7X_VMEM_LIMIT = 56 * 1024 * 1024

PROJ_TM = 1024
OUT_TM = 1024

_W_IN_SIZES = (("aq", A_WIDTH), ("ak", A_KV_WIDTH), ("av", A_KV_WIDTH), ("ag", A_WIDTH), ("bq", B_QK_WIDTH),
               ("bk", B_QK_WIDTH), ("bv", B_WIDTH), ("blr", B_GATE_RANK), ("bg", B_WIDTH), ("cu", C_WIDTH),
               ("cg", C_WIDTH))
_W_IN_OFF = {}
_off = 0
for _name, _w in _W_IN_SIZES:
    _W_IN_OFF[_name] = (_off, _w)
    _off += _w
W_IN_COLS = _off
_PROJ_OUTS = (("ak", BF16), ("ag", BF16), ("bq", BF16), ("bk", BF16),
              ("bv", BF16), ("bg", BF16), ("cu", F32), ("cg", BF16))
_PROJ_GATES = ("ag", "bg", "cg")


def _silu(x):
    return x * jax.nn.sigmoid(x)


def _cparams(semantics):
    return pltpu.CompilerParams(dimension_semantics=semantics, vmem_limit_bytes=V7X_VMEM_LIMIT)


def _mod_kernel(c_ref, w_ref, b_ref, o_ref):
    c = c_ref[...]
    o_ref[0] = jnp.dot(_silu(c).astype(BF16), w_ref[0].astype(BF16), preferred_element_type=F32) + b_ref[0]


def _mod(c_pad, w_mod, b_mod):
    layers = w_mod.shape[0]
    n = 3 * D_MODEL
    tn = n // 2
    return pl.pallas_call(
        _mod_kernel,
        out_shape=jax.ShapeDtypeStruct((layers, 8, n), F32),
        grid=(layers, n // tn),
        in_specs=[pl.BlockSpec((8, D_MODEL), lambda l, j: (0, 0)),
                  pl.BlockSpec((1, D_MODEL, tn), lambda l, j: (l, 0, j)),
                  pl.BlockSpec((1, 1, tn), lambda l, j: (l, 0, j))],
        out_specs=pl.BlockSpec((1, 8, tn), lambda l, j: (l, 0, j)),
        compiler_params=_cparams(("arbitrary", "arbitrary")),
        name="mod",
    )(c_pad, w_mod, b_mod.reshape(layers, 1, n))


def _proj_kernel(x_ref, scale_ref, shift_ref, gpre_ref, wt_ref, walpha_ref, balpha_ref, *out_refs):
    x = x_ref[...]
    ms = jnp.mean(x * x, axis=-1, keepdims=True)
    y = x * lax.rsqrt(ms + EPS) * gpre_ref[...]
    h = (y * (1.0 + scale_ref[0]) + shift_ref[0]).astype(BF16)
    nt = (((1,), (1,)), ((), ()))

    def rows(name):
        off, w = _W_IN_OFF[name]
        return wt_ref[off:off + w, :]

    for (name, _), o_ref in zip(_PROJ_OUTS, out_refs):
        piece = lax.dot_general(h, rows(name), nt, preferred_element_type=F32)
        o_ref[...] = (_silu(piece) if name in _PROJ_GATES else piece).astype(o_ref.dtype)
    la_ref, qt_ref, vt_ref = out_refs[len(_PROJ_OUTS):]
    qlr = lax.dot_general(jnp.concatenate([rows("aq"), rows("blr")], axis=0), h, nt, preferred_element_type=F32)
    qt_ref[...] = (qlr[:A_WIDTH] * (A_HEAD_DIM ** -0.5)).astype(qt_ref.dtype)
    vt_ref[...] = lax.dot_general(rows("av"), h, nt, preferred_element_type=F32).astype(vt_ref.dtype)
    lr_t = qlr[A_WIDTH:].astype(BF16)
    logits = lax.dot_general(lr_t, walpha_ref[...], (((0,), (0,)), ((), ())),
                             preferred_element_type=F32) + balpha_ref[...]
    log_sig = jnp.minimum(logits, 0.0) - jnp.log(1.0 + jnp.exp(-jnp.abs(logits)))
    la_ref[...] = log_sig * (math.log2(math.e) / B_GATE_TAU)


def _proj(x2, scale, shift, g_pre, w_t, layer, w_alpha, b_alpha):
    tm = PROJ_TM
    steps_per_batch = SEQ // tm
    row = lambda i: (i, 0)
    col = lambda i: (0, i)
    per_batch = lambda i: (i // steps_per_batch, 0, 0)
    const = lambda i: (0, 0)
    out_shape = [jax.ShapeDtypeStruct((TOKENS, _W_IN_OFF[n][1]), dt) for n, dt in _PROJ_OUTS]
    out_specs = [pl.BlockSpec((tm, _W_IN_OFF[n][1]), row) for n, _ in _PROJ_OUTS]
    out_shape += [jax.ShapeDtypeStruct((TOKENS, B_QK_WIDTH), F32),
                  jax.ShapeDtypeStruct((A_WIDTH, TOKENS), BF16),
                  jax.ShapeDtypeStruct((A_KV_WIDTH, TOKENS), BF16)]
    out_specs += [pl.BlockSpec((tm, B_QK_WIDTH), row),
                  pl.BlockSpec((A_WIDTH, tm), col),
                  pl.BlockSpec((A_KV_WIDTH, tm), col)]
    return pl.pallas_call(
        _proj_kernel,
        out_shape=out_shape,
        grid=(TOKENS // tm,),
        in_specs=[pl.BlockSpec((tm, D_MODEL), row),
                  pl.BlockSpec((1, 1, D_MODEL), per_batch),
                  pl.BlockSpec((1, 1, D_MODEL), per_batch),
                  pl.BlockSpec((1, D_MODEL), const),
                  pl.BlockSpec((W_IN_COLS, D_MODEL), lambda i: (layer, 0), pipeline_mode=pl.Buffered(1)),
                  pl.BlockSpec((B_GATE_RANK, B_QK_WIDTH), const),
                  pl.BlockSpec((1, B_QK_WIDTH), const)],
        out_specs=out_specs,
        compiler_params=_cparams(("arbitrary",)),
        name="proj",
    )(x2, scale, shift, g_pre.reshape(1, D_MODEL), w_t, w_alpha, b_alpha.reshape(1, B_QK_WIDTH))


def _attn_bias():
    j = np.arange(A_BLOCK)[:, None]
    i = np.arange(A_BLOCK)[None, :]
    dist = np.where(j > i, i + A_BLOCK - j, i - j).astype(np.float32)
    slopes = np.exp2(-8.0 * np.arange(1, A_Q_HEADS + 1, dtype=np.float32) / A_Q_HEADS).astype(np.float32)
    bias = -slopes[:, None, None] * dist[None]
    first = np.where((j > i)[None], -np.inf, bias).astype(np.float32)
    return jnp.asarray(np.stack([bias, first]))


def _attention_steps(sink_ref, qt_ref, kp_ref, kc_ref, vtp_ref, vtc_ref, g_ref, bias_ref, oa_ref, first_tile,
                     n_blocks, group_blocks, after_block):
    kj = lax.broadcasted_iota(jnp.int32, (A_BLOCK, A_BLOCK), 0)
    qi = lax.broadcasted_iota(jnp.int32, (A_BLOCK, A_BLOCK), 1)
    from_prev = kj > qi
    zero_rows = jnp.zeros((A_HEAD_DIM, A_BLOCK), BF16)
    group = A_Q_HEADS // A_KV_HEADS
    blk = lambda j: slice(A_BLOCK * j, A_BLOCK * (j + 1))

    def keys(j, sl):
        prev = kp_ref[:, sl] if j == 0 else kc_ref[blk(j - 1), sl]
        return prev, kc_ref[blk(j), sl]

    def values(j, rows):
        prev = vtp_ref[rows, :] if j == 0 else vtc_ref[rows, blk(j - 1)]
        return jnp.concatenate([prev, vtc_ref[rows, blk(j)]], axis=1)

    def scores(j, hd):
        kvh = hd // group
        sl = slice(LANES * (kvh // 2), LANES * (kvh // 2 + 1))
        qh = qt_ref[A_HEAD_DIM * hd:A_HEAD_DIM * (hd + 1), blk(j)]
        qsel = jnp.concatenate([qh, zero_rows] if kvh % 2 == 0 else [zero_rows, qh], axis=0)
        k_prev, k_cur = keys(j, sl)
        return (jnp.dot(k_prev, qsel, preferred_element_type=F32),
                jnp.dot(k_cur, qsel, preferred_element_type=F32))

    def attend(j, hd, s_prev, s_cur):
        kvh = hd // group
        v_both = values(j, slice(A_HEAD_DIM * kvh, A_HEAD_DIM * (kvh + 1)))
        table = jnp.where(first_tile, 1, 0) if j == 0 else 0
        s = jnp.where(from_prev, s_prev, s_cur) + bias_ref[table, hd]
        sink = sink_ref[hd]
        m = jnp.maximum(jnp.max(s, axis=0, keepdims=True), sink)
        p = jnp.exp(s - m)
        den = jnp.sum(p, axis=0, keepdims=True) + jnp.exp(sink - m)
        p_both = jnp.concatenate([jnp.where(from_prev, p, 0.0), jnp.where(from_prev, 0.0, p)],
                                 axis=0).astype(BF16)
        return jnp.dot(v_both, p_both, preferred_element_type=F32) / den

    pending = {}

    def issue_scores(first):
        for j in range(first, min(first + group_blocks, n_blocks)):
            for hd in range(A_Q_HEADS):
                pending[j, hd] = scores(j, hd)

    issue_scores(0)
    for j in range(n_blocks):
        if (j + 1) % group_blocks == 0:
            issue_scores(j + 1)
        outs = {}
        for hd in range(A_Q_HEADS):
            outs[hd] = attend(j, hd, *pending.pop((j, hd)))
            if hd % 2 == 1:
                qsl = slice(LANES * (hd // 2), LANES * (hd // 2 + 1))
                o_pair = jnp.concatenate([outs.pop(hd - 1), outs.pop(hd)], axis=0).T
                gate = g_ref[blk(j), qsl].astype(F32)
                oa_ref[blk(j), qsl] = (o_pair * gate).astype(oa_ref.dtype)
        after_block(j)


def _gla_kernel(la_ref, q_ref, k_ref, v_ref, g_ref, gn_ref, o_ref, st_ref):
    cb = GLA_BLOCK

    @pl.when(pl.program_id(0) == 0)
    def _():
        st_ref[...] = jnp.zeros_like(st_ref)

    r = lax.broadcasted_iota(jnp.int32, (cb, cb), 0)
    c = lax.broadcasted_iota(jnp.int32, (cb, cb), 1)
    tri = (c <= r).astype(BF16)
    lane = lax.broadcasted_iota(jnp.int32, (1, B_QK_WIDTH), 1)
    head_masks = [(lane >= B_DK * h) & (lane < B_DK * (h + 1)) for h in range(B_HEADS)]
    rr = lax.broadcasted_iota(jnp.int32, (B_HEADS * cb, cb), 0)
    cc = lax.broadcasted_iota(jnp.int32, (B_HEADS * cb, cb), 1)
    ri = rr & (cb - 1)
    nt = (((1,), (1,)), ((), ()))
    tn = (((0,), (0,)), ((), ()))
    rows = lambda u: slice(cb * u, cb * (u + 1))
    items = [(u, b) for u in range(GLA_SUB) for b in range(BATCH)]
    def cumsum(la):
        hi = la.astype(BF16)
        r1 = la - hi.astype(F32)
        mid = r1.astype(BF16)
        lo = (r1 - mid.astype(F32)).astype(BF16)
        parts = jnp.dot(tri, jnp.concatenate([hi, mid, lo], axis=1), preferred_element_type=F32)
        w = B_QK_WIDTH
        return parts[:, :w] + (parts[:, w:2 * w] + parts[:, 2 * w:])

    bcs = {(u, b): cumsum(la_ref[b, rows(u), :]) for u, b in items}
    stack_heads = lambda a: jnp.concatenate([jnp.where(m, a, 0.0) for m in head_masks], axis=0).astype(BF16)
    row = lax.broadcasted_iota(jnp.int32, (cb, 1), 0)
    diag = ((ri ^ cc) < GLA_DIAG) & (cc <= ri)
    same_seg = {seg: (ri ^ cc) < seg for seg in GLA_ANCHOR_SEGS}

    def upper(a, seg):
        return jnp.concatenate([a[s0 + seg // 2:s0 + seg] for s0 in range(0, cb, seg)], axis=0)

    def place_upper(p, seg):
        half = seg // 2
        zero = jnp.zeros((half, cb), F32)
        parts = []
        for piece in range(B_HEADS * cb // seg):
            parts += [zero, p[half * piece:half * (piece + 1)]]
        return jnp.concatenate(parts, axis=0)

    def anchor_rows(bc, seg, offset, reps):
        parts = []
        for s0 in range(0, cb, seg):
            a = s0 + offset - 1
            val = bc[a:a + 1, :] if a >= 0 else jnp.zeros((1, B_QK_WIDTH), F32)
            parts.append(jnp.broadcast_to(val, (reps, B_QK_WIDTH)))
        return jnp.concatenate(parts, axis=0)

    q_lv, k_lv, qsts, ksts, decs, vs = {}, {}, {}, {}, {}, {}
    for it in items:
        u, b = it
        bc = bcs[it]
        bl = bc[cb - 1:cb, :]
        q = q_ref[b, rows(u), :].astype(F32) * (B_DK ** -0.5)
        k = k_ref[b, rows(u), :].astype(F32)
        for seg in GLA_ANCHOR_SEGS:
            half = seg // 2
            q_lv[it, seg] = stack_heads(upper(q, seg) * jnp.exp2(upper(bc, seg) - anchor_rows(bc, seg, half, half)))
            in_lower = (row & (seg - 1)) < half
            k_lv[it, seg] = jnp.where(in_lower, k * jnp.exp2(anchor_rows(bc, seg, half, seg) - bc), 0.0).astype(BF16)
        anc = anchor_rows(bc, GLA_DIAG, 0, GLA_DIAG)
        q_lv[it, 0] = stack_heads(q * jnp.exp2(bc - anc))
        k_lv[it, 0] = (k * jnp.exp2(anc - bc)).astype(BF16)
        qsts[it] = stack_heads(q * jnp.exp2(bc))
        ksts[it] = stack_heads(k * jnp.exp2(bl - bc))
        decs[it] = jnp.exp2(bl)
        vs[it] = v_ref[b, rows(u), :]
    a_alls = {}
    for it in items:
        prod = lambda lv: lax.dot_general(q_lv[it, lv], k_lv[it, lv], nt, preferred_element_type=F32)
        a = jnp.where(diag, prod(0), 0.0)
        for seg in GLA_ANCHOR_SEGS:
            a = a + jnp.where(same_seg[seg], place_upper(prod(seg), seg), 0.0)
        a_alls[it] = a.astype(BF16)
    upds = {}
    for it in items:
        vst = jnp.concatenate([vs[it][:, B_DV * h:B_DV * (h + 1)] for h in range(B_HEADS)], axis=0)
        upds[it] = lax.dot_general(vst, ksts[it], tn, preferred_element_type=F32)
    st_in = {}
    for b in range(BATCH):
        st = st_ref[b]
        for u in range(GLA_SUB):
            st_in[u, b] = st
            st = st * decs[u, b] + upds[u, b]
        st_ref[b] = st
    oi_alls = {it: lax.dot_general(qsts[it], st_in[it].astype(BF16), nt, preferred_element_type=F32)
               for it in items}
    o_hs = {}
    for it in items:
        a_all = a_alls[it]
        for h in range(B_HEADS):
            o_hs[it, h] = (jnp.dot(a_all[cb * h:cb * (h + 1)], vs[it][:, B_DV * h:B_DV * (h + 1)],
                                   preferred_element_type=F32) + oi_alls[it][cb * h:cb * (h + 1)])
    for it in items:
        u, b = it
        for h in range(B_HEADS):
            vsl = slice(B_DV * h, B_DV * (h + 1))
            o_h = o_hs[it, h]
            ms = jnp.mean(o_h * o_h, axis=-1, keepdims=True)
            o_n = o_h * lax.rsqrt(ms + EPS) * gn_ref[:, vsl]
            gate = g_ref[b, rows(u), vsl].astype(F32)
            o_ref[b, rows(u), vsl] = (o_n * gate).astype(o_ref.dtype)


def _gla(log_a, bq, bk, bv, bg, g_gla):
    cb = GLA_BLOCK * GLA_SUB
    blk = lambda w: pl.BlockSpec((BATCH, cb, w), lambda i: (0, i, 0))
    r3 = lambda a: a.reshape(BATCH, SEQ, a.shape[-1])
    out = pl.pallas_call(
        _gla_kernel,
        out_shape=jax.ShapeDtypeStruct((BATCH, SEQ, B_WIDTH), BF16),
        grid=(SEQ // cb,),
        in_specs=[blk(B_QK_WIDTH), blk(B_QK_WIDTH), blk(B_QK_WIDTH), blk(B_WIDTH), blk(B_WIDTH),
                  pl.BlockSpec((1, B_WIDTH), lambda i: (0, 0))],
        out_specs=blk(B_WIDTH),
        scratch_shapes=[pltpu.VMEM((BATCH, B_DV, B_QK_WIDTH), F32)],
        compiler_params=_cparams(("arbitrary",)),
        name="gla",
    )(r3(log_a), r3(bq), r3(bk), r3(bv), r3(bg), g_gla.reshape(1, B_WIDTH))
    return out.reshape(TOKENS, B_WIDTH)


def _s5prep_kernel(ar_ref, ai_ref, ldt_ref, bre_ref, bim_ref, btre_ref, btim_ref, cre_ref, cim_ref,
                   mt_ref, wet_ref, wyt_ref, are_ref, aim_ref):
    p = C_STATE
    for pr in range(S5PREP_PAIRS):
        wet_re, wet_im, wyt_rows, a_re, a_im = [], [], [], [], []
        for g in range(2):
            i = 2 * pr + g
            kk, e_re, e_im, y_re, y_im, p_re, p_im = _s5_discretise(
                ar_ref[i], ai_ref[i], ldt_ref[i], bre_ref[i], bim_ref[i], btre_ref[i], btim_ref[i],
                cre_ref[i], cim_ref[i])
            pieces = [kk] + [jnp.concatenate([jnp.zeros((C_GROUP_CH * s, C_GROUP_CH), F32),
                                              kk[:S5_TC - C_GROUP_CH * s]], axis=0) for s in range(1, S5_CHUNK)]
            mt_ref[pr, g] = jnp.concatenate(pieces, axis=1).astype(BF16)
            e_t = jnp.concatenate([e_re, e_im], axis=1).T
            zero = jnp.zeros((p, S5_TC), F32)
            wet_re.append(jnp.concatenate([e_t[:p], zero] if g == 0 else [zero, e_t[:p]], axis=1))
            wet_im.append(jnp.concatenate([e_t[p:], zero] if g == 0 else [zero, e_t[p:]], axis=1))
            zero = jnp.zeros((S5_TC, p), F32)
            wyt_rows.append(
                jnp.concatenate([y_re, zero, -y_im, zero] if g == 0 else [zero, y_re, zero, -y_im], axis=1))
            a_re.append(p_re)
            a_im.append(p_im)
        wet_ref[pr] = jnp.concatenate(wet_re + wet_im, axis=0).astype(BF16)
        wyt_ref[pr] = jnp.concatenate(wyt_rows, axis=0).astype(BF16)
        r = lax.broadcasted_iota(jnp.int32, (S5_STEP_ROWS, 1), 0)
        f_re, f_im = jnp.concatenate(a_re, axis=1), jnp.concatenate(a_im, axis=1)
        t_re, t_im = jnp.ones((S5_STEP_ROWS, 2 * p), F32), jnp.zeros((S5_STEP_ROWS, 2 * p), F32)
        for k in range(S5_SEG_CHUNKS.bit_length()):
            s_re, s_im = jnp.where((r >> k) & 1 == 1, f_re, 1.0), jnp.where((r >> k) & 1 == 1, f_im, 0.0)
            t_re, t_im = t_re * s_re - t_im * s_im, t_re * s_im + t_im * s_re
            f_re, f_im = f_re * f_re - f_im * f_im, 2.0 * (f_re * f_im)
        are_ref[pr] = t_re
        aim_ref[pr] = t_im


def _s5_discretise(ar, ai, ldt, b_re, b_im, bt_re16, bt_im16, c_re16, c_im16):
    dt = jnp.exp(ldt)

    def cmul(xr, xi, yr, yi):
        return xr * yr - xi * yi, xr * yi + xi * yr

    kf = lax.broadcasted_iota(jnp.int32, (S5_POW_ROWS, 1), 0).astype(F32)
    mag = jnp.exp(kf * (ar * dt))
    ang = kf * (ai * dt)
    pw_re, pw_im = mag * jnp.cos(ang), mag * jnp.sin(ang)
    abar_re, abar_im = pw_re[1:2], pw_im[1:2]
    den = ar * ar + ai * ai
    num_re = abar_re - 1.0
    f_re = (num_re * ar + abar_im * ai) / den
    f_im = (abar_im * ar - num_re * ai) / den
    g_re, g_im = cmul(pw_re, pw_im, f_re, f_im)

    def pick(which, xr, xi):
        rep = lambda x: jnp.concatenate(
            [jnp.broadcast_to(x[which(i):which(i) + 1], (C_GROUP_CH, C_STATE)) for i in range(S5_CHUNK)], axis=0)
        return rep(xr), rep(xi)

    tile16 = lambda a: jnp.concatenate([a] * S5_CHUNK, axis=0)
    ct_re, ct_im = tile16(c_re16), tile16(c_im16)
    bt_re, bt_im = tile16(bt_re16), tile16(bt_im16)

    w_re, w_im = cmul(*pick(lambda i: i, g_re, g_im), ct_re, ct_im)
    kk = (jnp.dot(w_re, b_re, preferred_element_type=F32, precision=HIGHEST)
          - jnp.dot(w_im, b_im, preferred_element_type=F32, precision=HIGHEST))
    e_re, e_im = cmul(*pick(lambda i: S5_CHUNK - 1 - i, g_re, g_im), bt_re, bt_im)
    y_re, y_im = cmul(*pick(lambda i: i + 1, pw_re, pw_im), ct_re, ct_im)
    return kk, e_re, e_im, y_re, y_im, pw_re[S5_CHUNK:S5_CHUNK + 1], pw_im[S5_CHUNK:S5_CHUNK + 1]


def _s5prep(a_re, a_im, log_dt, b_re, b_im, c_re, c_im, d):
    p, ch = C_STATE, C_GROUP_CH
    layers = a_re.shape[0]
    g = layers * C_GROUPS
    npair = g // 2
    flat = lambda a: a.reshape(g, *a.shape[2:])
    a_re, a_im, log_dt, b_re, b_im, c_re, c_im = map(flat, (a_re, a_im, log_dt, b_re, b_im, c_re, c_im))
    row = lambda a: a.reshape(g, 1, p)
    ldt = jnp.broadcast_to(log_dt[:, None, None], (g, 1, p))
    b_t = lambda a: jnp.swapaxes(a, 1, 2)
    pps = S5PREP_PAIRS
    spec = lambda s1, s2: pl.BlockSpec((2 * pps, s1, s2), lambda i: (i, 0, 0))
    mt, wet, wyt, pw_re, pw_im = pl.pallas_call(
        _s5prep_kernel,
        out_shape=[jax.ShapeDtypeStruct((npair, 2, S5_TC, S5_TC), BF16),
                   jax.ShapeDtypeStruct((npair, 4 * p, 2 * S5_TC), BF16),
                   jax.ShapeDtypeStruct((npair, 2 * S5_TC, 4 * p), BF16),
                   jax.ShapeDtypeStruct((npair, S5_STEP_ROWS, 2 * p), F32),
                   jax.ShapeDtypeStruct((npair, S5_STEP_ROWS, 2 * p), F32)],
        grid=(npair // pps,),
        in_specs=[spec(1, p), spec(1, p), spec(1, p), spec(p, ch), spec(p, ch),
                  spec(ch, p), spec(ch, p), spec(ch, p), spec(ch, p)],
        out_specs=[pl.BlockSpec((pps, 2, S5_TC, S5_TC), lambda i: (i, 0, 0, 0)),
                   pl.BlockSpec((pps, 4 * p, 2 * S5_TC), lambda i: (i, 0, 0)),
                   pl.BlockSpec((pps, 2 * S5_TC, 4 * p), lambda i: (i, 0, 0)),
                   pl.BlockSpec((pps, S5_STEP_ROWS, 2 * p), lambda i: (i, 0, 0)),
                   pl.BlockSpec((pps, S5_STEP_ROWS, 2 * p), lambda i: (i, 0, 0))],
        compiler_params=_cparams(("arbitrary",)),
        name="s5prep",
    )(row(a_re), row(a_im), ldt, b_re, b_im, b_t(b_re), b_t(b_im), c_re, c_im)
    by_gb = lambda a: a.reshape(layers * S5_NGB, S5_PAIRS_PER_GB, *a.shape[1:])
    return (mt.reshape(layers * S5_NGB, S5_GB, S5_TC, S5_TC), by_gb(wet), by_gb(wyt), by_gb(pw_re), by_gb(pw_im),
            d.reshape(layers, 1, C_WIDTH))


def _s5_kernel(u_hbm, mt_ref, wet_ref, wyt_ref, are_ref, aim_ref, d_ref, y_hbm,
               ubuf, ybuf, in_sem, out_sem, ut_ref, yt_ref, ere_ref, eim_ref, hre_ref, him_ref):
    nck, t_len, ch, seg = S5_NCHUNK, S5_CHUNK, C_GROUP_CH, S5_SEG_CHUNKS
    nt = (((1,), (1,)), ((), ()))
    n_steps = S5_NGB * BATCH
    step = pl.program_id(0) * BATCH + pl.program_id(1)
    slot = step % 2

    def hbm_tile(ref, step_, s):
        gb_, b_ = step_ // BATCH, step_ % BATCH
        return ref.at[pl.ds(b_ * nck + s * seg, seg), :, pl.ds(gb_ * LANES, LANES)]

    def in_copy(step_, slot_, s):
        return pltpu.make_async_copy(hbm_tile(u_hbm, step_, s), ubuf.at[slot_, :, :, s, :], in_sem.at[slot_, s])

    def out_copy(step_, slot_, s):
        return pltpu.make_async_copy(ybuf.at[slot_, :, :, s, :], hbm_tile(y_hbm, step_, s), out_sem.at[slot_, s])

    @pl.when(step == 0)
    def _():
        for s in range(S5_SEGS):
            in_copy(step, slot, s).start()

    @pl.when(step + 1 < n_steps)
    def _():
        for s in range(S5_SEGS):
            in_copy(step + 1, 1 - slot, s).start()

    @pl.when(step >= 2)
    def _():
        for s in range(S5_SEGS):
            out_copy(step - 2, slot, s).wait()

    for s in range(S5_SEGS):
        in_copy(step, slot, s).wait()
    u_rows = lambda t: ubuf[slot, :, t].reshape(nck, LANES)
    for t in range(t_len):
        xt = u_rows(t).T
        for g in range(S5_GB):
            ut_ref[g, ch * t:ch * (t + 1), :] = xt[ch * g:ch * (g + 1), :]
    for j in range(S5_PAIRS_PER_GB):
        u0 = ut_ref[2 * j].astype(BF16)
        u1 = ut_ref[2 * j + 1].astype(BF16)
        et = jnp.dot(wet_ref[0, j], jnp.concatenate([u0, u1], axis=0), preferred_element_type=F32)
        e = et.T
        ere_ref[:, LANES * j:LANES * (j + 1)] = e[:, :LANES]
        eim_ref[:, LANES * j:LANES * (j + 1)] = e[:, LANES:]
        yt_ref[2 * j] = jnp.dot(mt_ref[0, 2 * j], u0, preferred_element_type=F32)
        yt_ref[2 * j + 1] = jnp.dot(mt_ref[0, 2 * j + 1], u1, preferred_element_type=F32)

    width = S5_GB * C_STATE

    def step_power(r, rows):
        row = lambda ref: jnp.concatenate([ref[0, j, r:r + 1, :] for j in range(S5_PAIRS_PER_GB)], axis=1)
        return (jnp.broadcast_to(row(are_ref), (rows, width)), jnp.broadcast_to(row(aim_ref), (rows, width)))

    a_re, a_im = step_power(1, S5_SEGS)

    def body(i, carry):
        h_re, h_im = carry
        rows = pl.ds(pl.multiple_of(i * S5_SEGS, S5_SEGS), S5_SEGS)
        hre_ref[rows, :] = h_re
        him_ref[rows, :] = h_im
        e_re = ere_ref[rows, :]
        e_im = eim_ref[rows, :]
        return a_re * h_re - a_im * h_im + e_re, a_re * h_im + a_im * h_re + e_im

    zero = jnp.zeros((S5_SEGS, width), F32)
    end_re, end_im = lax.fori_loop(0, seg, body, (zero, zero))
    g_re, g_im = step_power(seg, 1)
    sub = lax.broadcasted_iota(jnp.int32, (S5_SEGS, 1), 0)
    c_re = c_im = zero
    p_re = p_im = jnp.zeros((1, width), F32)
    for s in range(1, S5_SEGS):
        p_re, p_im = (g_re * p_re - g_im * p_im + end_re[s - 1:s], g_re * p_im + g_im * p_re + end_im[s - 1:s])
        c_re, c_im = jnp.where(sub == s, p_re, c_re), jnp.where(sub == s, p_im, c_im)
    for i in range(seg):
        rows = slice(S5_SEGS * i, S5_SEGS * (i + 1))
        w_re, w_im = step_power(i, S5_SEGS)
        hre_ref[rows, :] += w_re * c_re - w_im * c_im
        him_ref[rows, :] += w_re * c_im + w_im * c_re

    for j in range(S5_PAIRS_PER_GB):
        sl = slice(LANES * j, LANES * (j + 1))
        hp = jnp.concatenate([hre_ref[:, sl], him_ref[:, sl]], axis=1).astype(BF16)
        yi = lax.dot_general(wyt_ref[0, j], hp, nt, preferred_element_type=F32)
        yt_ref[2 * j] += yi[:S5_TC]
        yt_ref[2 * j + 1] += yi[S5_TC:]
    for t in range(t_len):
        ytt = jnp.concatenate([yt_ref[g, ch * t:ch * (t + 1), :] for g in range(S5_GB)], axis=0)
        ybuf[slot, :, t] = (ytt.T + d_ref[...] * u_rows(t)).reshape(seg, S5_SEGS, LANES)
    for s in range(S5_SEGS):
        out_copy(step, slot, s).start()

    @pl.when(step == n_steps - 1)
    def _():
        for s in range(S5_SEGS):
            out_copy(step - 1, 1 - slot, s).wait()
            out_copy(step, slot, s).wait()


def _s5(cu, mt, wet, wyt, pw_re, pw_im, d_row, layer):
    p4 = 4 * C_STATE
    per_gb = lambda *s: pl.BlockSpec((1,) + s, lambda gb, b: (layer * S5_NGB + gb,) + (0,) * len(s))
    state = pltpu.VMEM((S5_NCHUNK, S5_GB * C_STATE), F32)
    tiles = pltpu.VMEM((2, S5_SEG_CHUNKS, S5_CHUNK, S5_SEGS, LANES), F32)
    sems = pltpu.SemaphoreType.DMA((2, S5_SEGS))
    by_chunk = (TOKENS // S5_CHUNK, S5_CHUNK, C_WIDTH)
    y = pl.pallas_call(
        _s5_kernel,
        out_shape=jax.ShapeDtypeStruct(by_chunk, F32),
        grid=(S5_NGB, BATCH),
        in_specs=[pl.BlockSpec(memory_space=pl.ANY),
                  per_gb(S5_GB, S5_TC, S5_TC),
                  per_gb(S5_PAIRS_PER_GB, p4, 2 * S5_TC),
                  per_gb(S5_PAIRS_PER_GB, 2 * S5_TC, p4),
                  per_gb(S5_PAIRS_PER_GB, S5_STEP_ROWS, 2 * C_STATE),
                  per_gb(S5_PAIRS_PER_GB, S5_STEP_ROWS, 2 * C_STATE),
                  pl.BlockSpec((1, LANES), lambda gb, b: (0, gb))],
        out_specs=pl.BlockSpec(memory_space=pl.ANY),
        scratch_shapes=[tiles, tiles, sems, sems,
                        pltpu.VMEM((S5_GB, S5_TC, S5_NCHUNK), F32),
                        pltpu.VMEM((S5_GB, S5_TC, S5_NCHUNK), F32),
                        state, state, state, state],
        compiler_params=_cparams(("arbitrary", "arbitrary")),
        name="s5",
    )(cu.reshape(by_chunk), mt, wet, wyt, pw_re, pw_im, d_row)
    return y.reshape(TOKENS, C_WIDTH)


def _gelu_tanh(x):
    c = math.sqrt(2.0 / math.pi)
    return 0.5 * x * (1.0 + jnp.tanh(x * (c + (c * 0.044715) * (x * x))))


def _out_kernel(sink_ref, qt_ref, kp_ref, kc_ref, vtp_ref, vtc_ref, ag_ref, bias_ref,
                ob_ref, yc_ref, cg_ref, x_ref, gate_ref, gpost_ref, wglu_ref, bglu_ref, wout_ref, o_ref, oa_ref):
    first_tile = pl.program_id(0) % (SEQ // OUT_TM) == 0
    blocks_per_part = OUT_TM // (A_BLOCK * OUT_PARTS)
    gain = gpost_ref[...] * gate_ref[0]

    def project(part):
        r = slice(A_BLOCK * blocks_per_part * part, A_BLOCK * blocks_per_part * (part + 1))
        y = _gelu_tanh(yc_ref[r, :])
        z = jnp.dot(y.astype(BF16), wglu_ref[...], preferred_element_type=F32) + bglu_ref[...]
        y = y * jax.nn.sigmoid(z)
        oc = (y * cg_ref[r, :].astype(F32)).astype(BF16)
        mix = jnp.concatenate([oa_ref[r, :], ob_ref[r, :], oc], axis=1)
        acc = jnp.dot(mix, wout_ref[...], preferred_element_type=F32)
        ms = jnp.mean(acc * acc, axis=-1, keepdims=True)
        o_ref[r, :] = x_ref[r, :] + acc * lax.rsqrt(ms + EPS) * gain

    def after_block(j):
        if (j + 1) % blocks_per_part == 0:
            project(j // blocks_per_part)

    _attention_steps(sink_ref, qt_ref, kp_ref, kc_ref, vtp_ref, vtc_ref, ag_ref, bias_ref, oa_ref, first_tile,
                     OUT_TM // A_BLOCK, blocks_per_part, after_block)


def _out(sinks, qt, k, vt, ag, bias, ob, yc, cg, x2, gate, g_post, w_glu, b_glu, w_out, layer):
    tm = OUT_TM
    steps_per_batch = SEQ // tm
    blocks = tm // A_BLOCK
    row = lambda i: (i, 0)
    col = lambda i: (0, i)
    const = lambda i: (0, 0)
    once = dict(pipeline_mode=pl.Buffered(1))
    prev = lambda i: (i * blocks - jnp.minimum(i % steps_per_batch, 1), 0)
    prev_t = lambda i: (0, i * blocks - jnp.minimum(i % steps_per_batch, 1))
    return pl.pallas_call(
        _out_kernel,
        out_shape=jax.ShapeDtypeStruct((TOKENS, D_MODEL), F32),
        grid=(TOKENS // tm,),
        in_specs=[pl.BlockSpec(memory_space=pltpu.SMEM),
                  pl.BlockSpec((A_WIDTH, tm), col),
                  pl.BlockSpec((A_BLOCK, A_KV_WIDTH), prev),
                  pl.BlockSpec((tm, A_KV_WIDTH), row),
                  pl.BlockSpec((A_KV_WIDTH, A_BLOCK), prev_t),
                  pl.BlockSpec((A_KV_WIDTH, tm), col),
                  pl.BlockSpec((tm, A_WIDTH), row),
                  pl.BlockSpec((2, A_Q_HEADS, A_BLOCK, A_BLOCK), lambda i: (0, 0, 0, 0), **once),
                  pl.BlockSpec((tm, B_WIDTH), row),
                  pl.BlockSpec((tm, C_WIDTH), row),
                  pl.BlockSpec((tm, C_WIDTH), row),
                  pl.BlockSpec((tm, D_MODEL), row),
                  pl.BlockSpec((1, 1, D_MODEL), lambda i: (i // steps_per_batch, 0, 0)),
                  pl.BlockSpec((1, D_MODEL), const),
                  pl.BlockSpec((C_WIDTH, C_WIDTH), lambda i: (layer, 0), **once),
                  pl.BlockSpec((1, C_WIDTH), const),
                  pl.BlockSpec((2 * D_MODEL, D_MODEL), lambda i: (layer, 0), **once)],
        out_specs=pl.BlockSpec((tm, D_MODEL), row),
        scratch_shapes=[pltpu.VMEM((tm, A_WIDTH), BF16)],
        compiler_params=_cparams(("arbitrary",)),
        name="out",
    )(sinks, qt, k, k, vt, vt, ag, bias, ob, yc, cg, x2, gate, g_post.reshape(1, D_MODEL), w_glu,
      b_glu.reshape(1, C_WIDTH), w_out)


def kernel(x, c, w_mod, b_mod, g_pre, g_post, w_in, attn_sinks, gla_w_alpha, gla_b_alpha, gla_norm_g,
           s5_a_re, s5_a_im, s5_log_dt, s5_b_re, s5_b_im, s5_c_re, s5_c_im, s5_d, s5_w_glu, s5_b_glu, w_out):
    layers = w_mod.shape[0]
    x2 = x.reshape(TOKENS, D_MODEL)
    bias = _attn_bias()
    mod = _mod(jnp.pad(c, ((0, 8 - BATCH), (0, 0))), w_mod, b_mod)[:, :BATCH]
    shift, scale, gate = (m.reshape(layers, BATCH, 1, D_MODEL) for m in jnp.split(mod, 3, axis=-1))
    w_t = jnp.swapaxes(w_in, 1, 2).astype(BF16).reshape(layers * W_IN_COLS, D_MODEL)
    w_alpha = gla_w_alpha.astype(BF16)
    s5_ops = _s5prep(s5_a_re, s5_a_im, s5_log_dt, s5_b_re, s5_b_im, s5_c_re, s5_c_im, s5_d)
    *s5_ops, s5_d_rows = s5_ops
    w_glu = s5_w_glu.astype(BF16).reshape(layers * C_WIDTH, C_WIDTH)
    w_out_b = w_out.astype(BF16).reshape(layers * 2 * D_MODEL, D_MODEL)
    for l in range(layers):
        ak, ag, bq, bk, bv, bg, cu, cg, log_a, aqt, avt = _proj(
            x2, scale[l], shift[l], g_pre[l], w_t, l, w_alpha[l], gla_b_alpha[l])
        o_b = _gla(log_a, bq, bk, bv, bg, gla_norm_g[l])
        y_c = _s5(cu, *s5_ops, s5_d_rows[l], l)
        x2 = _out(attn_sinks[l], aqt, ak, avt, ag, bias, o_b, y_c, cg, x2, gate[l], g_post[l], w_glu, s5_b_glu[l],
                  w_out_b, l)
    return x2.reshape(x.shape)
```

```python
import math

import jax
import jax.numpy as jnp
import numpy as np
from jax import lax
from jax.experimental import pallas as pl
from jax.experimental.pallas import tpu as pltpu

F32 = jnp.float32
BF16 = jnp.bfloat16
HIGHEST = lax.Precision.HIGHEST

D_MODEL = 1024
BATCH = 4
SEQ = 4096
TOKENS = BATCH * SEQ
EPS = 1e-6

A_WIDTH = 1024
A_HEAD_DIM = 64
A_Q_HEADS = 16
A_KV_HEADS = 4
A_KV_WIDTH = A_KV_HEADS * A_HEAD_DIM
A_BLOCK = 128
WINDOW = 128
assert WINDOW == A_BLOCK

B_WIDTH = 512
B_HEADS = 4
B_DK = 64
B_DV = 128
B_QK_WIDTH = 256
B_GATE_RANK = 16
B_GATE_TAU = 16.0
GLA_BLOCK = 64
GLA_SUB = 8
GLA_DIAG = 16
GLA_ANCHOR_SEGS = (32, 64)
assert GLA_ANCHOR_SEGS[0] == 2 * GLA_DIAG and GLA_ANCHOR_SEGS[-1] == GLA_BLOCK
OUT_PARTS = 2

C_WIDTH = 512
C_GROUP_CH = 16
C_GROUPS = 32
C_STATE = 64
S5_CHUNK = 16
S5_NCHUNK = SEQ // S5_CHUNK
S5_TC = S5_CHUNK * C_GROUP_CH
S5_GB = 8
S5_NGB = C_GROUPS // S5_GB
S5_PAIRS_PER_GB = S5_GB // 2
S5_POW_ROWS = 24
S5PREP_PAIRS = 4
SUBLANES = 8
S5_SEGS = SUBLANES
S5_SEG_CHUNKS = S5_NCHUNK // S5_SEGS
S5_STEP_ROWS = S5_SEG_CHUNKS + SUBLANES

LANES = 128

V7X_VMEM_LIMIT = 56 * 1024 * 1024

PROJ_TM = 1024
OUT_TM = 1024

_W_IN_SIZES = (("aq", A_WIDTH), ("ak", A_KV_WIDTH), ("av", A_KV_WIDTH), ("ag", A_WIDTH), ("bq", B_QK_WIDTH),
               ("bk", B_QK_WIDTH), ("bv", B_WIDTH), ("blr", B_GATE_RANK), ("bg", B_WIDTH), ("cu", C_WIDTH),
               ("cg", C_WIDTH))
_W_IN_OFF = {}
_off = 0
for _name, _w in _W_IN_SIZES:
    _W_IN_OFF[_name] = (_off, _w)
    _off += _w
W_IN_COLS = _off
_PROJ_OUTS = (("ak", BF16), ("ag", BF16), ("bq", BF16), ("bk", BF16),
              ("bv", BF16), ("bg", BF16), ("cu", F32), ("cg", BF16))
_PROJ_GATES = ("ag", "bg", "cg")


def _silu(x):
    return x * jax.nn.sigmoid(x)


def _cparams(semantics):
    return pltpu.CompilerParams(dimension_semantics=semantics, vmem_limit_bytes=V7X_VMEM_LIMIT)


def _mod_kernel(c_ref, w_ref, b_ref, o_ref):
    c = c_ref[...]
    o_ref[0] = jnp.dot(_silu(c).astype(BF16), w_ref[0].astype(BF16), preferred_element_type=F32) + b_ref[0]


def _mod(c_pad, w_mod, b_mod):
    layers = w_mod.shape[0]
    n = 3 * D_MODEL
    tn = n // 2
    return pl.pallas_call(
        _mod_kernel,
        out_shape=jax.ShapeDtypeStruct((layers, 8, n), F32),
        grid=(layers, n // tn),
        in_specs=[pl.BlockSpec((8, D_MODEL), lambda l, j: (0, 0)),
                  pl.BlockSpec((1, D_MODEL, tn), lambda l, j: (l, 0, j)),
                  pl.BlockSpec((1, 1, tn), lambda l, j: (l, 0, j))],
        out_specs=pl.BlockSpec((1, 8, tn), lambda l, j: (l, 0, j)),
        compiler_params=_cparams(("arbitrary", "arbitrary")),
        name="mod",
    )(c_pad, w_mod, b_mod.reshape(layers, 1, n))


def _proj_kernel(x_ref, scale_ref, shift_ref, gpre_ref, wt_ref, walpha_ref, balpha_ref, *out_refs):
    x = x_ref[...]
    ms = jnp.mean(x * x, axis=-1, keepdims=True)
    y = x * lax.rsqrt(ms + EPS) * gpre_ref[...]
    h = (y * (1.0 + scale_ref[0]) + shift_ref[0]).astype(BF16)
    nt = (((1,), (1,)), ((), ()))

    def rows(name):
        off, w = _W_IN_OFF[name]
        return wt_ref[off:off + w, :]

    for (name, _), o_ref in zip(_PROJ_OUTS, out_refs):
        piece = lax.dot_general(h, rows(name), nt, preferred_element_type=F32)
        o_ref[...] = (_silu(piece) if name in _PROJ_GATES else piece).astype(o_ref.dtype)
    la_ref, qt_ref, vt_ref = out_refs[len(_PROJ_OUTS):]
    qlr = lax.dot_general(jnp.concatenate([rows("aq"), rows("blr")], axis=0), h, nt, preferred_element_type=F32)
    qt_ref[...] = (qlr[:A_WIDTH] * (A_HEAD_DIM ** -0.5)).astype(qt_ref.dtype)
    vt_ref[...] = lax.dot_general(rows("av"), h, nt, preferred_element_type=F32).astype(vt_ref.dtype)
    lr_t = qlr[A_WIDTH:].astype(BF16)
    logits = lax.dot_general(lr_t, walpha_ref[...], (((0,), (0,)), ((), ())),
                             preferred_element_type=F32) + balpha_ref[...]
    log_sig = jnp.minimum(logits, 0.0) - jnp.log(1.0 + jnp.exp(-jnp.abs(logits)))
    la_ref[...] = log_sig * (math.log2(math.e) / B_GATE_TAU)


def _proj(x2, scale, shift, g_pre, w_t, layer, w_alpha, b_alpha):
    tm = PROJ_TM
    steps_per_batch = SEQ // tm
    row = lambda i: (i, 0)
    col = lambda i: (0, i)
    per_batch = lambda i: (i // steps_per_batch, 0, 0)
    const = lambda i: (0, 0)
    out_shape = [jax.ShapeDtypeStruct((TOKENS, _W_IN_OFF[n][1]), dt) for n, dt in _PROJ_OUTS]
    out_specs = [pl.BlockSpec((tm, _W_IN_OFF[n][1]), row) for n, _ in _PROJ_OUTS]
    out_shape += [jax.ShapeDtypeStruct((TOKENS, B_QK_WIDTH), F32),
                  jax.ShapeDtypeStruct((A_WIDTH, TOKENS), BF16),
                  jax.ShapeDtypeStruct((A_KV_WIDTH, TOKENS), BF16)]
    out_specs += [pl.BlockSpec((tm, B_QK_WIDTH), row),
                  pl.BlockSpec((A_WIDTH, tm), col),
                  pl.BlockSpec((A_KV_WIDTH, tm), col)]
    return pl.pallas_call(
        _proj_kernel,
        out_shape=out_shape,
        grid=(TOKENS // tm,),
        in_specs=[pl.BlockSpec((tm, D_MODEL), row),
                  pl.BlockSpec((1, 1, D_MODEL), per_batch),
                  pl.BlockSpec((1, 1, D_MODEL), per_batch),
                  pl.BlockSpec((1, D_MODEL), const),
                  pl.BlockSpec((W_IN_COLS, D_MODEL), lambda i: (layer, 0), pipeline_mode=pl.Buffered(1)),
                  pl.BlockSpec((B_GATE_RANK, B_QK_WIDTH), const),
                  pl.BlockSpec((1, B_QK_WIDTH), const)],
        out_specs=out_specs,
        compiler_params=_cparams(("arbitrary",)),
        name="proj",
    )(x2, scale, shift, g_pre.reshape(1, D_MODEL), w_t, w_alpha, b_alpha.reshape(1, B_QK_WIDTH))


def _attn_bias():
    j = np.arange(A_BLOCK)[:, None]
    i = np.arange(A_BLOCK)[None, :]
    dist = np.where(j > i, i + A_BLOCK - j, i - j).astype(np.float32)
    slopes = np.exp2(-8.0 * np.arange(1, A_Q_HEADS + 1, dtype=np.float32) / A_Q_HEADS).astype(np.float32)
    bias = -slopes[:, None, None] * dist[None]
    first = np.where((j > i)[None], -np.inf, bias).astype(np.float32)
    return jnp.asarray(np.stack([bias, first]))


def _attention_steps(sink_ref, qt_ref, kp_ref, kc_ref, vtp_ref, vtc_ref, g_ref, bias_ref, oa_ref, first_tile,
                     n_blocks, group_blocks, after_block):
    kj = lax.broadcasted_iota(jnp.int32, (A_BLOCK, A_BLOCK), 0)
    qi = lax.broadcasted_iota(jnp.int32, (A_BLOCK, A_BLOCK), 1)
    from_prev = kj > qi
    zero_rows = jnp.zeros((A_HEAD_DIM, A_BLOCK), BF16)
    group = A_Q_HEADS // A_KV_HEADS
    blk = lambda j: slice(A_BLOCK * j, A_BLOCK * (j + 1))

    def keys(j, sl):
        prev = kp_ref[:, sl] if j == 0 else kc_ref[blk(j - 1), sl]
        return prev, kc_ref[blk(j), sl]

    def values(j, rows):
        prev = vtp_ref[rows, :] if j == 0 else vtc_ref[rows, blk(j - 1)]
        return jnp.concatenate([prev, vtc_ref[rows, blk(j)]], axis=1)

    def scores(j, hd):
        kvh = hd // group
        sl = slice(LANES * (kvh // 2), LANES * (kvh // 2 + 1))
        qh = qt_ref[A_HEAD_DIM * hd:A_HEAD_DIM * (hd + 1), blk(j)]
        qsel = jnp.concatenate([qh, zero_rows] if kvh % 2 == 0 else [zero_rows, qh], axis=0)
        k_prev, k_cur = keys(j, sl)
        return (jnp.dot(k_prev, qsel, preferred_element_type=F32),
                jnp.dot(k_cur, qsel, preferred_element_type=F32))

    def attend(j, hd, s_prev, s_cur):
        kvh = hd // group
        v_both = values(j, slice(A_HEAD_DIM * kvh, A_HEAD_DIM * (kvh + 1)))
        table = jnp.where(first_tile, 1, 0) if j == 0 else 0
        s = jnp.where(from_prev, s_prev, s_cur) + bias_ref[table, hd]
        sink = sink_ref[hd]
        m = jnp.maximum(jnp.max(s, axis=0, keepdims=True), sink)
        p = jnp.exp(s - m)
        den = jnp.sum(p, axis=0, keepdims=True) + jnp.exp(sink - m)
        p_both = jnp.concatenate([jnp.where(from_prev, p, 0.0), jnp.where(from_prev, 0.0, p)],
                                 axis=0).astype(BF16)
        return jnp.dot(v_both, p_both, preferred_element_type=F32) / den

    pending = {}

    def issue_scores(first):
        for j in range(first, min(first + group_blocks, n_blocks)):
            for hd in range(A_Q_HEADS):
                pending[j, hd] = scores(j, hd)

    issue_scores(0)
    for j in range(n_blocks):
        if (j + 1) % group_blocks == 0:
            issue_scores(j + 1)
        outs = {}
        for hd in range(A_Q_HEADS):
            outs[hd] = attend(j, hd, *pending.pop((j, hd)))
            if hd % 2 == 1:
                qsl = slice(LANES * (hd // 2), LANES * (hd // 2 + 1))
                o_pair = jnp.concatenate([outs.pop(hd - 1), outs.pop(hd)], axis=0).T
                gate = g_ref[blk(j), qsl].astype(F32)
                oa_ref[blk(j), qsl] = (o_pair * gate).astype(oa_ref.dtype)
        after_block(j)


def _gla_kernel(la_ref, q_ref, k_ref, v_ref, g_ref, gn_ref, o_ref, st_ref):
    cb = GLA_BLOCK

    @pl.when(pl.program_id(0) == 0)
    def _():
        st_ref[...] = jnp.zeros_like(st_ref)

    r = lax.broadcasted_iota(jnp.int32, (cb, cb), 0)
    c = lax.broadcasted_iota(jnp.int32, (cb, cb), 1)
    tri = (c <= r).astype(BF16)
    lane = lax.broadcasted_iota(jnp.int32, (1, B_QK_WIDTH), 1)
    head_masks = [(lane >= B_DK * h) & (lane < B_DK * (h + 1)) for h in range(B_HEADS)]
    ri = lax.broadcasted_iota(jnp.int32, (cb, B_HEADS * cb), 0)
    cc = lax.broadcasted_iota(jnp.int32, (cb, B_HEADS * cb), 1) & (cb - 1)
    vlane = lax.broadcasted_iota(jnp.int32, (1, B_WIDTH), 1)
    value_masks = [(vlane >= B_DV * h) & (vlane < B_DV * (h + 1)) for h in range(B_HEADS)]
    nt =(((1,), (1,)), ((), ()))
    tn = (((0,), (0,)), ((), ()))
    rows = lambda u: slice(cb * u, cb * (u + 1))
    items = [(u, b) for u in range(GLA_SUB) for b in range(BATCH)]
    def cumsum(la):
        hi = la.astype(BF16)
        r1 = la - hi.astype(F32)
        mid = r1.astype(BF16)
        lo = (r1 - mid.astype(F32)).astype(BF16)
        parts = jnp.dot(tri, jnp.concatenate([hi, mid, lo], axis=1), preferred_element_type=F32)
        w = B_QK_WIDTH
        return parts[:, :w] + (parts[:, w:2 * w] + parts[:, 2 * w:])

    bcs = {(u, b): cumsum(la_ref[b, rows(u), :]) for u, b in items}
    stack_heads = lambda a: jnp.concatenate([jnp.where(m, a, 0.0) for m in head_masks], axis=0).astype(BF16)
    row = lax.broadcasted_iota(jnp.int32, (cb, 1), 0)
    diag = ((ri ^ cc) < GLA_DIAG) & (cc <= ri)
    same_seg = {seg: (ri ^ cc) < seg for seg in GLA_ANCHOR_SEGS}

    def upper(a, seg):
        return jnp.concatenate([a[s0 + seg // 2:s0 + seg] for s0 in range(0, cb, seg)], axis=0)

    def place_upper(p, seg):
        half = seg // 2
        zero = jnp.zeros((half, B_HEADS * cb), F32)
        parts = []
        for piece in range(cb // seg):
            parts += [zero, p[half * piece:half * (piece + 1)]]
        return jnp.concatenate(parts, axis=0)

    def anchor_rows(bc, seg, offset, reps):
        parts = []
        for s0 in range(0, cb, seg):
            a = s0 + offset - 1
            val = bc[a:a + 1, :] if a >= 0 else jnp.zeros((1, B_QK_WIDTH), F32)
            parts.append(jnp.broadcast_to(val, (reps, B_QK_WIDTH)))
        return jnp.concatenate(parts, axis=0)

    q_lv, k_lv, qsts, ksts, decs, vs = {}, {}, {}, {}, {}, {}
    for it in items:
        u, b = it
        bc = bcs[it]
        bl = bc[cb - 1:cb, :]
        q = q_ref[b, rows(u), :].astype(F32) * (B_DK ** -0.5)
        k = k_ref[b, rows(u), :].astype(F32)
        for seg in GLA_ANCHOR_SEGS:
            half = seg // 2
            q_lv[it, seg] = (upper(q, seg) * jnp.exp2(upper(bc, seg) - anchor_rows(bc, seg, half, half))).astype(BF16)
            in_lower = (row & (seg - 1)) < half
            k_lv[it, seg] = stack_heads(jnp.where(in_lower, k * jnp.exp2(anchor_rows(bc, seg, half, seg) - bc), 0.0))
        anc = anchor_rows(bc, GLA_DIAG, 0, GLA_DIAG)
        q_lv[it, 0] = (q * jnp.exp2(bc - anc)).astype(BF16)
        k_lv[it, 0] = stack_heads(k * jnp.exp2(anc - bc))
        qsts[it] = stack_heads(q * jnp.exp2(bc))
        ksts[it] = stack_heads(k * jnp.exp2(bl - bc))
        decs[it] = jnp.exp2(bl)
        vs[it] = v_ref[b, rows(u), :]
    a_alls = {}
    for it in items:
        prod = lambda lv: lax.dot_general(q_lv[it, lv], k_lv[it, lv], nt, preferred_element_type=F32)
        a = jnp.where(diag, prod(0), 0.0)
        for seg in GLA_ANCHOR_SEGS:
            a = a + jnp.where(same_seg[seg], place_upper(prod(seg), seg), 0.0)
        a_alls[it] = a.astype(BF16)
    upds = {}
    for it in items:
        vst = jnp.concatenate([vs[it][:, B_DV * h:B_DV * (h + 1)] for h in range(B_HEADS)], axis=0)
        upds[it] = lax.dot_general(vst, ksts[it], tn, preferred_element_type=F32)
    st_in = {}
    for b in range(BATCH):
        st = st_ref[b]
        for u in range(GLA_SUB):
            st_in[u, b] = st
            st = st * decs[u, b] + upds[u, b]
        st_ref[b] = st
    oi_alls = {it: lax.dot_general(qsts[it], st_in[it].astype(BF16), nt, preferred_element_type=F32)
               for it in items}
    o_hs = {}
    for it in items:
        v_bd = jnp.concatenate([jnp.where(m, vs[it], jnp.zeros_like(vs[it])) for m in value_masks], axis=0)
        pv = jnp.dot(a_alls[it], v_bd, preferred_element_type=F32)
        for h in range(B_HEADS):
            o_hs[it, h] = pv[:, B_DV * h:B_DV * (h + 1)] + oi_alls[it][cb * h:cb * (h + 1)]
    for it in items:
        u, b = it
        for h in range(B_HEADS):
            vsl = slice(B_DV * h, B_DV * (h + 1))
            o_h = o_hs[it, h]
            ms = jnp.mean(o_h * o_h, axis=-1, keepdims=True)
            o_n = o_h * lax.rsqrt(ms + EPS) * gn_ref[:, vsl]
            gate = g_ref[b, rows(u), vsl].astype(F32)
            o_ref[b, rows(u), vsl] = (o_n * gate).astype(o_ref.dtype)


def _gla(log_a, bq, bk, bv, bg, g_gla):
    cb = GLA_BLOCK * GLA_SUB
    blk = lambda w: pl.BlockSpec((BATCH, cb, w), lambda i: (0, i, 0))
    r3 = lambda a: a.reshape(BATCH, SEQ, a.shape[-1])
    out = pl.pallas_call(
        _gla_kernel,
        out_shape=jax.ShapeDtypeStruct((BATCH, SEQ, B_WIDTH), BF16),
        grid=(SEQ // cb,),
        in_specs=[blk(B_QK_WIDTH), blk(B_QK_WIDTH), blk(B_QK_WIDTH), blk(B_WIDTH), blk(B_WIDTH),
                  pl.BlockSpec((1, B_WIDTH), lambda i: (0, 0))],
        out_specs=blk(B_WIDTH),
        scratch_shapes=[pltpu.VMEM((BATCH, B_DV, B_QK_WIDTH), F32)],
        compiler_params=_cparams(("arbitrary",)),
        name="gla",
    )(r3(log_a), r3(bq), r3(bk), r3(bv), r3(bg), g_gla.reshape(1, B_WIDTH))
    return out.reshape(TOKENS, B_WIDTH)


def _s5prep_kernel(ar_ref, ai_ref, ldt_ref, bre_ref, bim_ref, btre_ref, btim_ref, cre_ref, cim_ref,
                   mt_ref, wet_ref, wyt_ref, are_ref, aim_ref):
    p = C_STATE
    for pr in range(S5PREP_PAIRS):
        wet_re, wet_im, wyt_rows, a_re, a_im = [], [], [], [], []
        for g in range(2):
            i = 2 * pr + g
            kk, e_re, e_im, y_re, y_im, p_re, p_im = _s5_discretise(
                ar_ref[i], ai_ref[i], ldt_ref[i], bre_ref[i], bim_ref[i], btre_ref[i], btim_ref[i],
                cre_ref[i], cim_ref[i])
            pieces = [kk] + [jnp.concatenate([jnp.zeros((C_GROUP_CH * s, C_GROUP_CH), F32),
                                              kk[:S5_TC - C_GROUP_CH * s]], axis=0) for s in range(1, S5_CHUNK)]
            mt_ref[pr, g] = jnp.concatenate(pieces, axis=1).astype(BF16)
            e_t = jnp.concatenate([e_re, e_im], axis=1).T
            zero = jnp.zeros((p, S5_TC), F32)
            wet_re.append(jnp.concatenate([e_t[:p], zero] if g == 0 else [zero, e_t[:p]], axis=1))
            wet_im.append(jnp.concatenate([e_t[p:], zero] if g == 0 else [zero, e_t[p:]], axis=1))
            zero = jnp.zeros((S5_TC, p), F32)
            wyt_rows.append(
                jnp.concatenate([y_re, zero, -y_im, zero] if g == 0 else [zero, y_re, zero, -y_im], axis=1))
            a_re.append(p_re)
            a_im.append(p_im)
        wet_ref[pr] = jnp.concatenate(wet_re + wet_im, axis=0).astype(BF16)
        wyt_ref[pr] = jnp.concatenate(wyt_rows, axis=0).astype(BF16)
        r = lax.broadcasted_iota(jnp.int32, (S5_STEP_ROWS, 1), 0)
        f_re, f_im = jnp.concatenate(a_re, axis=1), jnp.concatenate(a_im, axis=1)
        t_re, t_im = jnp.ones((S5_STEP_ROWS, 2 * p), F32), jnp.zeros((S5_STEP_ROWS, 2 * p), F32)
        for k in range(S5_SEG_CHUNKS.bit_length()):
            s_re, s_im = jnp.where((r >> k) & 1 == 1, f_re, 1.0), jnp.where((r >> k) & 1 == 1, f_im, 0.0)
            t_re, t_im = t_re * s_re - t_im * s_im, t_re * s_im + t_im * s_re
            f_re, f_im = f_re * f_re - f_im * f_im, 2.0 * (f_re * f_im)
        are_ref[pr] = t_re
        aim_ref[pr] = t_im


def _s5_discretise(ar, ai, ldt, b_re, b_im, bt_re16, bt_im16, c_re16, c_im16):
    dt = jnp.exp(ldt)

    def cmul(xr, xi, yr, yi):
        return xr * yr - xi * yi, xr * yi + xi * yr

    kf = lax.broadcasted_iota(jnp.int32, (S5_POW_ROWS, 1), 0).astype(F32)
    mag = jnp.exp(kf * (ar * dt))
    ang = kf * (ai * dt)
    pw_re, pw_im = mag * jnp.cos(ang), mag * jnp.sin(ang)
    abar_re, abar_im = pw_re[1:2], pw_im[1:2]
    den = ar * ar + ai * ai
    num_re = abar_re - 1.0
    f_re = (num_re * ar + abar_im * ai) / den
    f_im = (abar_im * ar - num_re * ai) / den
    g_re, g_im = cmul(pw_re, pw_im, f_re, f_im)

    def pick(which, xr, xi):
        rep = lambda x: jnp.concatenate(
            [jnp.broadcast_to(x[which(i):which(i) + 1], (C_GROUP_CH, C_STATE)) for i in range(S5_CHUNK)], axis=0)
        return rep(xr), rep(xi)

    tile16 = lambda a: jnp.concatenate([a] * S5_CHUNK, axis=0)
    ct_re, ct_im = tile16(c_re16), tile16(c_im16)
    bt_re, bt_im = tile16(bt_re16), tile16(bt_im16)

    w_re, w_im = cmul(*pick(lambda i: i, g_re, g_im), ct_re, ct_im)
    kk = (jnp.dot(w_re, b_re, preferred_element_type=F32, precision=HIGHEST)
          - jnp.dot(w_im, b_im, preferred_element_type=F32, precision=HIGHEST))
    e_re, e_im = cmul(*pick(lambda i: S5_CHUNK - 1 - i, g_re, g_im), bt_re, bt_im)
    y_re, y_im = cmul(*pick(lambda i: i + 1, pw_re, pw_im), ct_re, ct_im)
    return kk, e_re, e_im, y_re, y_im, pw_re[S5_CHUNK:S5_CHUNK + 1], pw_im[S5_CHUNK:S5_CHUNK + 1]


def _s5prep(a_re, a_im, log_dt, b_re, b_im, c_re, c_im, d):
    p, ch = C_STATE, C_GROUP_CH
    layers = a_re.shape[0]
    g = layers * C_GROUPS
    npair = g // 2
    flat = lambda a: a.reshape(g, *a.shape[2:])
    a_re, a_im, log_dt, b_re, b_im, c_re, c_im = map(flat, (a_re, a_im, log_dt, b_re, b_im, c_re, c_im))
    row = lambda a: a.reshape(g, 1, p)
    ldt = jnp.broadcast_to(log_dt[:, None, None], (g, 1, p))
    b_t = lambda a: jnp.swapaxes(a, 1, 2)
    pps = S5PREP_PAIRS
    spec = lambda s1, s2: pl.BlockSpec((2 * pps, s1, s2), lambda i: (i, 0, 0))
    mt, wet, wyt, pw_re, pw_im = pl.pallas_call(
        _s5prep_kernel,
        out_shape=[jax.ShapeDtypeStruct((npair, 2, S5_TC, S5_TC), BF16),
                   jax.ShapeDtypeStruct((npair, 4 * p, 2 * S5_TC), BF16),
                   jax.ShapeDtypeStruct((npair, 2 * S5_TC, 4 * p), BF16),
                   jax.ShapeDtypeStruct((npair, S5_STEP_ROWS, 2 * p), F32),
                   jax.ShapeDtypeStruct((npair, S5_STEP_ROWS, 2 * p), F32)],
        grid=(npair // pps,),
        in_specs=[spec(1, p), spec(1, p), spec(1, p), spec(p, ch), spec(p, ch),
                  spec(ch, p), spec(ch, p), spec(ch, p), spec(ch, p)],
        out_specs=[pl.BlockSpec((pps, 2, S5_TC, S5_TC), lambda i: (i, 0, 0, 0)),
                   pl.BlockSpec((pps, 4 * p, 2 * S5_TC), lambda i: (i, 0, 0)),
                   pl.BlockSpec((pps, 2 * S5_TC, 4 * p), lambda i: (i, 0, 0)),
                   pl.BlockSpec((pps, S5_STEP_ROWS, 2 * p), lambda i: (i, 0, 0)),
                   pl.BlockSpec((pps, S5_STEP_ROWS, 2 * p), lambda i: (i, 0, 0))],
        compiler_params=_cparams(("arbitrary",)),
        name="s5prep",
    )(row(a_re), row(a_im), ldt, b_re, b_im, b_t(b_re), b_t(b_im), c_re, c_im)
    by_gb = lambda a: a.reshape(layers * S5_NGB, S5_PAIRS_PER_GB, *a.shape[1:])
    return (mt.reshape(layers * S5_NGB, S5_GB, S5_TC, S5_TC), by_gb(wet), by_gb(wyt), by_gb(pw_re), by_gb(pw_im),
            d.reshape(layers, 1, C_WIDTH))


def _s5_kernel(u_hbm, mt_ref, wet_ref, wyt_ref, are_ref, aim_ref, d_ref, y_hbm,
               ubuf, ybuf, in_sem, out_sem, ut_ref, yt_ref, ere_ref, eim_ref, hre_ref, him_ref):
    nck, t_len, ch, seg = S5_NCHUNK, S5_CHUNK, C_GROUP_CH, S5_SEG_CHUNKS
    nt = (((1,), (1,)), ((), ()))
    n_steps = S5_NGB * BATCH
    step = pl.program_id(0) * BATCH + pl.program_id(1)
    slot = step % 2

    def hbm_tile(ref, step_, s):
        gb_, b_ = step_ // BATCH, step_ % BATCH
        return ref.at[pl.ds(b_ * nck + s * seg, seg), :, pl.ds(gb_ * LANES, LANES)]

    def in_copy(step_, slot_, s):
        return pltpu.make_async_copy(hbm_tile(u_hbm, step_, s), ubuf.at[slot_, :, :, s, :], in_sem.at[slot_, s])

    def out_copy(step_, slot_, s):
        return pltpu.make_async_copy(ybuf.at[slot_, :, :, s, :], hbm_tile(y_hbm, step_, s), out_sem.at[slot_, s])

    @pl.when(step == 0)
    def _():
        for s in range(S5_SEGS):
            in_copy(step, slot, s).start()

    @pl.when(step + 1 < n_steps)
    def _():
        for s in range(S5_SEGS):
            in_copy(step + 1, 1 - slot, s).start()

    @pl.when(step >= 2)
    def _():
        for s in range(S5_SEGS):
            out_copy(step - 2, slot, s).wait()

    for s in range(S5_SEGS):
        in_copy(step, slot, s).wait()
    u_rows = lambda t: ubuf[slot, :, t].reshape(nck, LANES)
    for t in range(t_len):
        xt = u_rows(t).T
        for g in range(S5_GB):
            ut_ref[g, ch * t:ch * (t + 1), :] = xt[ch * g:ch * (g + 1), :]
    for j in range(S5_PAIRS_PER_GB):
        u0 = ut_ref[2 * j].astype(BF16)
        u1 = ut_ref[2 * j + 1].astype(BF16)
        et = jnp.dot(wet_ref[0, j], jnp.concatenate([u0, u1], axis=0), preferred_element_type=F32)
        e = et.T
        ere_ref[:, LANES * j:LANES * (j + 1)] = e[:, :LANES]
        eim_ref[:, LANES * j:LANES * (j + 1)] = e[:, LANES:]
        yt_ref[2 * j] = jnp.dot(mt_ref[0, 2 * j], u0, preferred_element_type=F32)
        yt_ref[2 * j + 1] = jnp.dot(mt_ref[0, 2 * j + 1], u1, preferred_element_type=F32)

    width = S5_GB * C_STATE

    def step_power(r, rows):
        row = lambda ref: jnp.concatenate([ref[0, j, r:r + 1, :] for j in range(S5_PAIRS_PER_GB)], axis=1)
        return (jnp.broadcast_to(row(are_ref), (rows, width)), jnp.broadcast_to(row(aim_ref), (rows, width)))

    a_re, a_im = step_power(1, S5_SEGS)

    def body(i, carry):
        h_re, h_im = carry
        rows = pl.ds(pl.multiple_of(i * S5_SEGS, S5_SEGS), S5_SEGS)
        hre_ref[rows, :] = h_re
        him_ref[rows, :] = h_im
        e_re = ere_ref[rows, :]
        e_im = eim_ref[rows, :]
        return a_re * h_re - a_im * h_im + e_re, a_re * h_im + a_im * h_re + e_im

    zero = jnp.zeros((S5_SEGS, width), F32)
    end_re, end_im = lax.fori_loop(0, seg, body, (zero, zero))
    g_re, g_im = step_power(seg, 1)
    sub = lax.broadcasted_iota(jnp.int32, (S5_SEGS, 1), 0)
    c_re = c_im = zero
    p_re = p_im = jnp.zeros((1, width), F32)
    for s in range(1, S5_SEGS):
        p_re, p_im = (g_re * p_re - g_im * p_im + end_re[s - 1:s], g_re * p_im + g_im * p_re + end_im[s - 1:s])
        c_re, c_im = jnp.where(sub == s, p_re, c_re), jnp.where(sub == s, p_im, c_im)
    for i in range(seg):
        rows = slice(S5_SEGS * i, S5_SEGS * (i + 1))
        w_re, w_im = step_power(i, S5_SEGS)
        hre_ref[rows, :] += w_re * c_re - w_im * c_im
        him_ref[rows, :] += w_re * c_im + w_im * c_re

    for j in range(S5_PAIRS_PER_GB):
        sl = slice(LANES * j, LANES * (j + 1))
        hp = jnp.concatenate([hre_ref[:, sl], him_ref[:, sl]], axis=1).astype(BF16)
        yi = lax.dot_general(wyt_ref[0, j], hp, nt, preferred_element_type=F32)
        yt_ref[2 * j] += yi[:S5_TC]
        yt_ref[2 * j + 1] += yi[S5_TC:]
    for t in range(t_len):
        ytt = jnp.concatenate([yt_ref[g, ch * t:ch * (t + 1), :] for g in range(S5_GB)], axis=0)
        ybuf[slot, :, t] = (ytt.T + d_ref[...] * u_rows(t)).reshape(seg, S5_SEGS, LANES)
    for s in range(S5_SEGS):
        out_copy(step, slot, s).start()

    @pl.when(step == n_steps - 1)
    def _():
        for s in range(S5_SEGS):
            out_copy(step - 1, 1 - slot, s).wait()
            out_copy(step, slot, s).wait()


def _s5(cu, mt, wet, wyt, pw_re, pw_im, d_row, layer):
    p4 = 4 * C_STATE
    per_gb = lambda *s: pl.BlockSpec((1,) + s, lambda gb, b: (layer * S5_NGB + gb,) + (0,) * len(s))
    state = pltpu.VMEM((S5_NCHUNK, S5_GB * C_STATE), F32)
    tiles = pltpu.VMEM((2, S5_SEG_CHUNKS, S5_CHUNK, S5_SEGS, LANES), F32)
    sems = pltpu.SemaphoreType.DMA((2, S5_SEGS))
    by_chunk = (TOKENS // S5_CHUNK, S5_CHUNK, C_WIDTH)
    y = pl.pallas_call(
        _s5_kernel,
        out_shape=jax.ShapeDtypeStruct(by_chunk, F32),
        grid=(S5_NGB, BATCH),
        in_specs=[pl.BlockSpec(memory_space=pl.ANY),
                  per_gb(S5_GB, S5_TC, S5_TC),
                  per_gb(S5_PAIRS_PER_GB, p4, 2 * S5_TC),
                  per_gb(S5_PAIRS_PER_GB, 2 * S5_TC, p4),
                  per_gb(S5_PAIRS_PER_GB, S5_STEP_ROWS, 2 * C_STATE),
                  per_gb(S5_PAIRS_PER_GB, S5_STEP_ROWS, 2 * C_STATE),
                  pl.BlockSpec((1, LANES), lambda gb, b: (0, gb))],
        out_specs=pl.BlockSpec(memory_space=pl.ANY),
        scratch_shapes=[tiles, tiles, sems, sems,
                        pltpu.VMEM((S5_GB, S5_TC, S5_NCHUNK), F32),
                        pltpu.VMEM((S5_GB, S5_TC, S5_NCHUNK), F32),
                        state, state, state, state],
        compiler_params=_cparams(("arbitrary", "arbitrary")),
        name="s5",
    )(cu.reshape(by_chunk), mt, wet, wyt, pw_re, pw_im, d_row)
    return y.reshape(TOKENS, C_WIDTH)


def _gelu_tanh(x):
    c = math.sqrt(2.0 / math.pi)
    return 0.5 * x * (1.0 + jnp.tanh(x * (c + (c * 0.044715) * (x * x))))


def _out_kernel(sink_ref, qt_ref, kp_ref, kc_ref, vtp_ref, vtc_ref, ag_ref, bias_ref,
                ob_ref, yc_ref, cg_ref, x_ref, gate_ref, gpost_ref, wglu_ref, bglu_ref, wout_ref, o_ref, oa_ref):
    first_tile = pl.program_id(0) % (SEQ // OUT_TM) == 0
    blocks_per_part = OUT_TM // (A_BLOCK * OUT_PARTS)
    gain = gpost_ref[...] * gate_ref[0]

    def project(part):
        r = slice(A_BLOCK * blocks_per_part * part, A_BLOCK * blocks_per_part * (part + 1))
        y = _gelu_tanh(yc_ref[r, :])
        z = jnp.dot(y.astype(BF16), wglu_ref[...], preferred_element_type=F32) + bglu_ref[...]
        y = y * jax.nn.sigmoid(z)
        oc = (y * cg_ref[r, :].astype(F32)).astype(BF16)
        mix = jnp.concatenate([oa_ref[r, :], ob_ref[r, :], oc], axis=1)
        acc = jnp.dot(mix, wout_ref[...], preferred_element_type=F32)
        ms = jnp.mean(acc * acc, axis=-1, keepdims=True)
        o_ref[r, :] = x_ref[r, :] + acc * lax.rsqrt(ms + EPS) * gain

    def after_block(j):
        if (j + 1) % blocks_per_part == 0:
            project(j // blocks_per_part)

    _attention_steps(sink_ref, qt_ref, kp_ref, kc_ref, vtp_ref, vtc_ref, ag_ref, bias_ref, oa_ref, first_tile,
                     OUT_TM // A_BLOCK, blocks_per_part, after_block)


def _out(sinks, qt, k, vt, ag, bias, ob, yc, cg, x2, gate, g_post, w_glu, b_glu, w_out, layer):
    tm = OUT_TM
    steps_per_batch = SEQ // tm
    blocks = tm // A_BLOCK
    row = lambda i: (i, 0)
    col = lambda i: (0, i)
    const = lambda i: (0, 0)
    once = dict(pipeline_mode=pl.Buffered(1))
    prev = lambda i: (i * blocks - jnp.minimum(i % steps_per_batch, 1), 0)
    prev_t = lambda i: (0, i * blocks - jnp.minimum(i % steps_per_batch, 1))
    return pl.pallas_call(
        _out_kernel,
        out_shape=jax.ShapeDtypeStruct((TOKENS, D_MODEL), F32),
        grid=(TOKENS // tm,),
        in_specs=[pl.BlockSpec(memory_space=pltpu.SMEM),
                  pl.BlockSpec((A_WIDTH, tm), col),
                  pl.BlockSpec((A_BLOCK, A_KV_WIDTH), prev),
                  pl.BlockSpec((tm, A_KV_WIDTH), row),
                  pl.BlockSpec((A_KV_WIDTH, A_BLOCK), prev_t),
                  pl.BlockSpec((A_KV_WIDTH, tm), col),
                  pl.BlockSpec((tm, A_WIDTH), row),
                  pl.BlockSpec((2, A_Q_HEADS, A_BLOCK, A_BLOCK), lambda i: (0, 0, 0, 0), **once),
                  pl.BlockSpec((tm, B_WIDTH), row),
                  pl.BlockSpec((tm, C_WIDTH), row),
                  pl.BlockSpec((tm, C_WIDTH), row),
                  pl.BlockSpec((tm, D_MODEL), row),
                  pl.BlockSpec((1, 1, D_MODEL), lambda i: (i // steps_per_batch, 0, 0)),
                  pl.BlockSpec((1, D_MODEL), const),
                  pl.BlockSpec((C_WIDTH, C_WIDTH), lambda i: (layer, 0), **once),
                  pl.BlockSpec((1, C_WIDTH), const),
                  pl.BlockSpec((2 * D_MODEL, D_MODEL), lambda i: (layer, 0), **once)],
        out_specs=pl.BlockSpec((tm, D_MODEL), row),
        scratch_shapes=[pltpu.VMEM((tm, A_WIDTH), BF16)],
        compiler_params=_cparams(("arbitrary",)),
        name="out",
    )(sinks, qt, k, k, vt, vt, ag, bias, ob, yc, cg, x2, gate, g_post.reshape(1, D_MODEL), w_glu,
      b_glu.reshape(1, C_WIDTH), w_out)


def kernel(x, c, w_mod, b_mod, g_pre, g_post, w_in, attn_sinks, gla_w_alpha, gla_b_alpha, gla_norm_g,
           s5_a_re, s5_a_im, s5_log_dt, s5_b_re, s5_b_im, s5_c_re, s5_c_im, s5_d, s5_w_glu, s5_b_glu, w_out):
    layers = w_mod.shape[0]
    x2 = x.reshape(TOKENS, D_MODEL)
    bias = _attn_bias()
    mod = _mod(jnp.pad(c, ((0, 8 - BATCH), (0, 0))), w_mod, b_mod)[:, :BATCH]
    shift, scale, gate = (m.reshape(layers, BATCH, 1, D_MODEL) for m in jnp.split(mod, 3, axis=-1))
    w_t = jnp.swapaxes(w_in, 1, 2).astype(BF16).reshape(layers * W_IN_COLS, D_MODEL)
    w_alpha = gla_w_alpha.astype(BF16)
    s5_ops = _s5prep(s5_a_re, s5_a_im, s5_log_dt, s5_b_re, s5_b_im, s5_c_re, s5_c_im, s5_d)
    *s5_ops, s5_d_rows = s5_ops
    w_glu = s5_w_glu.astype(BF16).reshape(layers * C_WIDTH, C_WIDTH)
    w_out_b = w_out.astype(BF16).reshape(layers * 2 * D_MODEL, D_MODEL)
    for l in range(layers):
        ak, ag, bq, bk, bv, bg, cu, cg, log_a, aqt, avt = _proj(
            x2, scale[l], shift[l], g_pre[l], w_t, l, w_alpha[l], gla_b_alpha[l])
        o_b = _gla(log_a, bq, bk, bv, bg, gla_norm_g[l])
        y_c = _s5(cu, *s5_ops, s5_d_rows[l], l)
        x2 = _out(attn_sinks[l], aqt, ak, avt, ag, bias, o_b, y_c, cg, x2, gate[l], g_post[l], w_glu, s5_b_glu[l],
                  w_out_b, l)
    return x2.reshape(x.shape)
```
